```python
import functools
import jax, jax.numpy as jnp
from jax import lax
import numpy as np

D_MODEL = 1024
BATCH = 16
SEQ = 2048
DEPTH = 4

GRID_W = 64
CTX_LEN = 256
N_MIXERS = 2
N_A = (DEPTH + 1) // 2
N_B = DEPTH // 2
N_MOD = 9
D_FF = 2816
CONV_DIM = D_MODEL
CONV_W = 3
MLA_HEADS = 8
QK_NOPE = 128
QK_ROPE = 64
QK_HEAD = QK_NOPE + QK_ROPE
V_HEAD = 128
Q_LORA = 256
KV_LORA = 128
ROPE_BASE = 10000.0
QK_SCALE = QK_HEAD ** -0.5
Q_BLOCK = 128
EPS = 1e-6

kernel_name = "hybrid_shortconv_mla_macaron_dit"


def rms_norm(x, g):
    xf = x.astype(jnp.float32)
    y = xf * lax.rsqrt(jnp.mean(xf * xf, axis=-1, keepdims=True) + EPS)
    return (y * g.astype(jnp.float32)).astype(x.dtype)


def adaln_chunks(cond, w_mod, b_mod):
    m = jax.nn.silu(cond) @ w_mod + b_mod
    return jnp.split(m[:, None, :], N_MOD, axis=-1)


def pre(h, g, shift, scale):
    return rms_norm(h, g) * (1 + scale) + shift


def swiglu(h, w1, w3, w2):
    return (jax.nn.silu(h @ w1) * (h @ w3)) @ w2


def conv3_centred(u, w):
    return lax.conv_general_dilated(
        u, w[:, None, :].astype(u.dtype), window_strides=(1,), padding=((1, 1),),
        dimension_numbers=("NWC", "WIO", "NWC"), feature_group_count=u.shape[-1])


def short_conv_mixer(h, w_in, conv_w, w_out):
    b_gate, c_gate, u = jnp.split(h @ w_in, 3, axis=-1)
    return (b_gate * conv3_centred(c_gate * u, conv_w)) @ w_out


def axial_rope_tables(n):
    rows = n // GRID_W
    r = jnp.broadcast_to(jnp.arange(rows)[:, None], (rows, GRID_W)).reshape(n).astype(jnp.float32)
    col = jnp.broadcast_to(jnp.arange(GRID_W)[None, :], (rows, GRID_W)).reshape(n).astype(jnp.float32)
    n_freq = QK_ROPE // 4
    inv = ROPE_BASE ** (-jnp.arange(n_freq, dtype=jnp.float32) / n_freq)
    ang = jnp.stack([r[:, None] * inv, col[:, None] * inv], axis=1)
    return jnp.cos(ang), jnp.sin(ang)


def apply_axial_rope(t, cos, sin):
    ts = t.reshape(t.shape[:-1] + (2, 2, QK_ROPE // 4))
    x1, x2 = ts[..., 0, :], ts[..., 1, :]
    cos = cos.astype(t.dtype)
    sin = sin.astype(t.dtype)
    y = jnp.stack([x1 * cos - x2 * sin, x1 * sin + x2 * cos], axis=-2)
    return y.reshape(t.shape)


def rope_tail(t, cos, sin):
    return jnp.concatenate([t[..., :QK_NOPE], apply_axial_rope(t[..., QK_NOPE:], cos, sin)], axis=-1)


def mla_down(h, w_a):
    return jnp.split(h @ w_a, [Q_LORA, Q_LORA + KV_LORA], axis=-1)


def mla_queries(cq, g_qa, w_uq, g_q):
    b, n, _ = cq.shape
    q = (rms_norm(cq, g_qa) @ w_uq).reshape(b, n, MLA_HEADS, QK_HEAD)
    return rms_norm(q, g_q).transpose(0, 2, 1, 3)


def mla_keys_values(ckv, k_rope, g_kva, w_ukv, g_k):
    b, n, _ = ckv.shape
    kv = (rms_norm(ckv, g_kva) @ w_ukv).reshape(b, n, MLA_HEADS, QK_NOPE + V_HEAD)
    k_nope, v = jnp.split(kv, [QK_NOPE], axis=-1)
    k_r = jnp.broadcast_to(k_rope[:, :, None, :], (b, n, MLA_HEADS, QK_ROPE))
    k = rms_norm(jnp.concatenate([k_nope, k_r], axis=-1), g_k)
    return k.transpose(0, 2, 1, 3), v.transpose(0, 2, 1, 3)


def softmax_attend(q, k, v):
    s = jnp.einsum("bhqd,bhkd->bhqk", q, k).astype(jnp.float32) * QK_SCALE
    p = jax.nn.softmax(s, axis=-1).astype(v.dtype)
    return jnp.einsum("bhqk,bhkd->bhqd", p, v)


def merge_heads(o):
    b, h, n, d = o.shape
    return o.transpose(0, 2, 1, 3).reshape(b, n, h * d)


def latent_attention(q, k_all, v_all):
    b, h, n, dq = q.shape
    nb = n // Q_BLOCK
    qb = q.reshape(b, h, nb, Q_BLOCK, dq).transpose(2, 0, 1, 3, 4)
    o = lax.map(lambda qblk: softmax_attend(qblk, k_all, v_all), qb)
    return o.transpose(1, 0, 3, 2, 4).reshape(b, n, h * V_HEAD)


def _fwd_setup_inputs(seed: int = 0) -> dict:
    key = jax.random.key(seed)
    ks = jax.random.split(key, 24)
    f32 = jnp.float32

    def nrm(k, shape, scale):
        return jax.random.normal(k, shape, f32) * scale

    def gain(k, shape):
        return 1.0 + 0.1 * jax.random.normal(k, shape, f32)

    D = D_MODEL
    return {
        "x": nrm(ks[0], (BATCH, SEQ, D), 1.0),
        "c": nrm(ks[1], (BATCH, D), 1.0),
        "ctx": nrm(ks[2], (BATCH, CTX_LEN, D), 1.0),
        "c_ctx": nrm(ks[3], (D,), 1.0),
        "w_mod": nrm(ks[4], (DEPTH, D, N_MOD * D), 0.5 * D ** -0.5),
        "b_mod": nrm(ks[5], (DEPTH, N_MOD * D), 0.02),
        "g_norm": gain(ks[6], (DEPTH, 3, D)),
        "ffn_w1": nrm(ks[7], (DEPTH, 2, D, D_FF), D ** -0.5),
        "ffn_w3": nrm(ks[8], (DEPTH, 2, D, D_FF), D ** -0.5),
        "ffn_w2": nrm(ks[9], (DEPTH, 2, D_FF, D), D_FF ** -0.5),
        "sc_w_in": nrm(ks[10], (N_A, D, 3 * CONV_DIM), D ** -0.5),
        "sc_conv": nrm(ks[11], (N_A, CONV_W, CONV_DIM), CONV_W ** -0.5),
        "sc_w_out": nrm(ks[12], (N_A, CONV_DIM, D), CONV_DIM ** -0.5),
        "mla_w_a": nrm(ks[13], (N_B, D, Q_LORA + KV_LORA + QK_ROPE), D ** -0.5),
        "mla_g_qa": gain(ks[14], (N_B, Q_LORA)),
        "mla_w_uq": nrm(ks[15], (N_B, Q_LORA, MLA_HEADS * QK_HEAD), Q_LORA ** -0.5),
        "mla_g_kva": gain(ks[16], (N_B, KV_LORA)),
        "mla_w_ukv": nrm(ks[17], (N_B, KV_LORA, MLA_HEADS * (QK_NOPE + V_HEAD)), KV_LORA ** -0.5),
        "mla_g_q": gain(ks[18], (N_B, QK_HEAD)),
        "mla_g_k": gain(ks[19], (N_B, QK_HEAD)),
        "mla_w_o": nrm(ks[20], (N_B, MLA_HEADS * V_HEAD, D), (MLA_HEADS * V_HEAD) ** -0.5),
    }


def _fwd_reference(x, c, ctx, c_ctx, w_mod, b_mod, g_norm, ffn_w1, ffn_w3, ffn_w2,
              sc_w_in, sc_conv, sc_w_out, mla_w_a, mla_g_qa, mla_w_uq, mla_g_kva,
              mla_w_ukv, mla_g_q, mla_g_k, mla_w_o):
    n = x.shape[1]
    cos, sin = axial_rope_tables(n)
    h_x, h_c = x, ctx
    for i in range(DEPTH):
        kind, j = i % N_MIXERS, i // N_MIXERS
        last = i == DEPTH - 1
        run_ctx_in = (not last) or kind == 1
        run_ctx_out = not last

        mx = adaln_chunks(c, w_mod[i], b_mod[i])
        mc = adaln_chunks(c_ctx[None], w_mod[i], b_mod[i])
        ffn1 = functools.partial(swiglu, w1=ffn_w1[i, 0], w3=ffn_w3[i, 0], w2=ffn_w2[i, 0])
        ffn2 = functools.partial(swiglu, w1=ffn_w1[i, 1], w3=ffn_w3[i, 1], w2=ffn_w2[i, 1])

        h_x = h_x + 0.5 * mx[2] * ffn1(pre(h_x, g_norm[i, 0], mx[0], mx[1]))
        if run_ctx_in:
            h_c = h_c + 0.5 * mc[2] * ffn1(pre(h_c, g_norm[i, 0], mc[0], mc[1]))

        nx = pre(h_x, g_norm[i, 1], mx[3], mx[4])
        if kind == 0:
            ox = short_conv_mixer(nx, sc_w_in[j], sc_conv[j], sc_w_out[j])
            if run_ctx_out:
                nc = pre(h_c, g_norm[i, 1], mc[3], mc[4])
                oc = short_conv_mixer(nc, sc_w_in[j], sc_conv[j], sc_w_out[j])
        else:
            nc = pre(h_c, g_norm[i, 1], mc[3], mc[4])
            cq_c, ckv_c, kr_c = mla_down(nc, mla_w_a[j])
            k_c, v_c = mla_keys_values(ckv_c, kr_c, mla_g_kva[j], mla_w_ukv[j], mla_g_k[j])
            cq_x, ckv_x, kr_x = mla_down(nx, mla_w_a[j])
            k_x, v_x = mla_keys_values(ckv_x, kr_x, mla_g_kva[j], mla_w_ukv[j], mla_g_k[j])
            k_x = rope_tail(k_x, cos, sin)
            q_x = rope_tail(mla_queries(cq_x, mla_g_qa[j], mla_w_uq[j], mla_g_q[j]), cos, sin)
            k_all = jnp.concatenate([k_c, k_x], axis=2)
            v_all = jnp.concatenate([v_c, v_x], axis=2)
            ox = latent_attention(q_x, k_all, v_all) @ mla_w_o[j]
            if run_ctx_out:
                q_c = mla_queries(cq_c, mla_g_qa[j], mla_w_uq[j], mla_g_q[j])
                oc = merge_heads(softmax_attend(q_c, k_c, v_c)) @ mla_w_o[j]
        h_x = h_x + mx[5] * ox
        if run_ctx_out:
            h_c = h_c + mc[5] * oc

        h_x = h_x + 0.5 * mx[8] * ffn2(pre(h_x, g_norm[i, 2], mx[6], mx[7]))
        if run_ctx_out:
            h_c = h_c + 0.5 * mc[8] * ffn2(pre(h_c, g_norm[i, 2], mc[6], mc[7]))
    return h_x


import jax as _jax
import jax.numpy as _jnp

TWIN_FORMAT = 'train_step'
FWD_PARAMS = ['x', 'c', 'ctx', 'c_ctx', 'w_mod', 'b_mod', 'g_norm', 'ffn_w1', 'ffn_w3', 'ffn_w2', 'sc_w_in', 'sc_conv', 'sc_w_out', 'mla_w_a', 'mla_g_qa', 'mla_w_uq', 'mla_g_kva', 'mla_w_ukv', 'mla_g_q', 'mla_g_k', 'mla_w_o']
TWIN_WEIGHTS = ['c_ctx', 'w_mod', 'b_mod', 'g_norm', 'ffn_w1', 'ffn_w3', 'ffn_w2', 'sc_w_in', 'sc_conv', 'sc_w_out', 'mla_w_a', 'mla_g_qa', 'mla_w_uq', 'mla_g_kva', 'mla_w_ukv', 'mla_g_q', 'mla_g_k', 'mla_w_o']
TWIN_DIFF_INPUT = 'x'
TWIN_INPUTS = ['x', 'c', 'ctx', 'c_ctx', 'w_mod', 'b_mod', 'g_norm', 'ffn_w1', 'ffn_w3', 'ffn_w2', 'sc_w_in', 'sc_conv', 'sc_w_out', 'mla_w_a', 'mla_g_qa', 'mla_w_uq', 'mla_g_kva', 'mla_w_ukv', 'mla_g_q', 'mla_g_k', 'mla_w_o', 'loss_target', 'm_c_ctx', 'm_w_mod', 'm_b_mod', 'm_g_norm', 'm_ffn_w1', 'm_ffn_w3', 'm_ffn_w2', 'm_sc_w_in', 'm_sc_conv', 'm_sc_w_out', 'm_mla_w_a', 'm_mla_g_qa', 'm_mla_w_uq', 'm_mla_g_kva', 'm_mla_w_ukv', 'm_mla_g_q', 'm_mla_g_k', 'm_mla_w_o', 'v_c_ctx', 'v_w_mod', 'v_b_mod', 'v_g_norm', 'v_ffn_w1', 'v_ffn_w3', 'v_ffn_w2', 'v_sc_w_in', 'v_sc_conv', 'v_sc_w_out', 'v_mla_w_a', 'v_mla_g_qa', 'v_mla_w_uq', 'v_mla_g_kva', 'v_mla_w_ukv', 'v_mla_g_q', 'v_mla_g_k', 'v_mla_w_o']
TWIN_OUTPUTS = ['loss', 'grad_x', 'grad_c_ctx', 'grad_w_mod', 'grad_b_mod', 'grad_g_norm', 'grad_ffn_w1', 'grad_ffn_w3', 'grad_ffn_w2', 'grad_sc_w_in', 'grad_sc_conv', 'grad_sc_w_out', 'grad_mla_w_a', 'grad_mla_g_qa', 'grad_mla_w_uq', 'grad_mla_g_kva', 'grad_mla_w_ukv', 'grad_mla_g_q', 'grad_mla_g_k', 'grad_mla_w_o', 'delta_c_ctx', 'delta_w_mod', 'delta_b_mod', 'delta_g_norm', 'delta_ffn_w1', 'delta_ffn_w3', 'delta_ffn_w2', 'delta_sc_w_in', 'delta_sc_conv', 'delta_sc_w_out', 'delta_mla_w_a', 'delta_mla_g_qa', 'delta_mla_w_uq', 'delta_mla_g_kva', 'delta_mla_w_ukv', 'delta_mla_g_q', 'delta_mla_g_k', 'delta_mla_w_o', 'new_m_c_ctx', 'new_m_w_mod', 'new_m_b_mod', 'new_m_g_norm', 'new_m_ffn_w1', 'new_m_ffn_w3', 'new_m_ffn_w2', 'new_m_sc_w_in', 'new_m_sc_conv', 'new_m_sc_w_out', 'new_m_mla_w_a', 'new_m_mla_g_qa', 'new_m_mla_w_uq', 'new_m_mla_g_kva', 'new_m_mla_w_ukv', 'new_m_mla_g_q', 'new_m_mla_g_k', 'new_m_mla_w_o', 'new_v_c_ctx', 'new_v_w_mod', 'new_v_b_mod', 'new_v_g_norm', 'new_v_ffn_w1', 'new_v_ffn_w3', 'new_v_ffn_w2', 'new_v_sc_w_in', 'new_v_sc_conv', 'new_v_sc_w_out', 'new_v_mla_w_a', 'new_v_mla_g_qa', 'new_v_mla_w_uq', 'new_v_mla_g_kva', 'new_v_mla_w_ukv', 'new_v_mla_g_q', 'new_v_mla_g_k', 'new_v_mla_w_o']
TWIN_LEAF_KINDS = {'loss': 'loss', 'grad_x': 'grad_x', 'grad_c_ctx': 'grad_w', 'grad_w_mod': 'grad_w', 'grad_b_mod': 'grad_w', 'grad_g_norm': 'grad_w', 'grad_ffn_w1': 'grad_w', 'grad_ffn_w3': 'grad_w', 'grad_ffn_w2': 'grad_w', 'grad_sc_w_in': 'grad_w', 'grad_sc_conv': 'grad_w', 'grad_sc_w_out': 'grad_w', 'grad_mla_w_a': 'grad_w', 'grad_mla_g_qa': 'grad_w', 'grad_mla_w_uq': 'grad_w', 'grad_mla_g_kva': 'grad_w', 'grad_mla_w_ukv': 'grad_w', 'grad_mla_g_q': 'grad_w', 'grad_mla_g_k': 'grad_w', 'grad_mla_w_o': 'grad_w', 'delta_c_ctx': 'delta_w', 'delta_w_mod': 'delta_w', 'delta_b_mod': 'delta_w', 'delta_g_norm': 'delta_w', 'delta_ffn_w1': 'delta_w', 'delta_ffn_w3': 'delta_w', 'delta_ffn_w2': 'delta_w', 'delta_sc_w_in': 'delta_w', 'delta_sc_conv': 'delta_w', 'delta_sc_w_out': 'delta_w', 'delta_mla_w_a': 'delta_w', 'delta_mla_g_qa': 'delta_w', 'delta_mla_w_uq': 'delta_w', 'delta_mla_g_kva': 'delta_w', 'delta_mla_w_ukv': 'delta_w', 'delta_mla_g_q': 'delta_w', 'delta_mla_g_k': 'delta_w', 'delta_mla_w_o': 'delta_w', 'new_m_c_ctx': 'new_m', 'new_m_w_mod': 'new_m', 'new_m_b_mod': 'new_m', 'new_m_g_norm': 'new_m', 'new_m_ffn_w1': 'new_m', 'new_m_ffn_w3': 'new_m', 'new_m_ffn_w2': 'new_m', 'new_m_sc_w_in': 'new_m', 'new_m_sc_conv': 'new_m', 'new_m_sc_w_out': 'new_m', 'new_m_mla_w_a': 'new_m', 'new_m_mla_g_qa': 'new_m', 'new_m_mla_w_uq': 'new_m', 'new_m_mla_g_kva': 'new_m', 'new_m_mla_w_ukv': 'new_m', 'new_m_mla_g_q': 'new_m', 'new_m_mla_g_k': 'new_m', 'new_m_mla_w_o': 'new_m', 'new_v_c_ctx': 'new_v', 'new_v_w_mod': 'new_v', 'new_v_b_mod': 'new_v', 'new_v_g_norm': 'new_v', 'new_v_ffn_w1': 'new_v', 'new_v_ffn_w3': 'new_v', 'new_v_ffn_w2': 'new_v', 'new_v_sc_w_in': 'new_v', 'new_v_sc_conv': 'new_v', 'new_v_sc_w_out': 'new_v', 'new_v_mla_w_a': 'new_v', 'new_v_mla_g_qa': 'new_v', 'new_v_mla_w_uq': 'new_v', 'new_v_mla_g_kva': 'new_v', 'new_v_mla_w_ukv': 'new_v', 'new_v_mla_g_q': 'new_v', 'new_v_mla_g_k': 'new_v', 'new_v_mla_w_o': 'new_v'}


def _forward(args):
    return _fwd_reference(*[args[k] for k in FWD_PARAMS])


def _output_shape():
    out = _jax.eval_shape(lambda: _forward(_fwd_setup_inputs(0)))
    return out.shape, out.dtype

N_MICROBATCH = 1
ADAM_LR = 0.001
ADAM_B1 = 0.9
ADAM_B2 = 0.999
ADAM_EPS = 1e-08
ADAM_WD = 0.01
ADAM_STEP = 10
PER_EXAMPLE_BATCH_AXIS = {'x': 0, 'c': 0, 'ctx': 0, 'loss_target': 0}
SHARED_INPUTS = []
_WEIGHT_DTYPES = {'c_ctx': _jnp.float32, 'w_mod': _jnp.float32, 'b_mod': _jnp.float32, 'g_norm': _jnp.float32, 'ffn_w1': _jnp.float32, 'ffn_w3': _jnp.float32, 'ffn_w2': _jnp.float32, 'sc_w_in': _jnp.float32, 'sc_conv': _jnp.float32, 'sc_w_out': _jnp.float32, 'mla_w_a': _jnp.float32, 'mla_g_qa': _jnp.float32, 'mla_w_uq': _jnp.float32, 'mla_g_kva': _jnp.float32, 'mla_w_ukv': _jnp.float32, 'mla_g_q': _jnp.float32, 'mla_g_k': _jnp.float32, 'mla_w_o': _jnp.float32}
MOMENT_SCALE = {'c_ctx': 8.169941e-02, 'w_mod': 1.846127e+00, 'b_mod': 4.260772e+00, 'g_norm': 6.505888e+00, 'ffn_w1': 3.134856e-02, 'ffn_w3': 3.255310e-02, 'ffn_w2': 5.369360e-02, 'sc_w_in': 2.953571e-01, 'sc_conv': 3.087238e+00, 'sc_w_out': 2.202221e-01, 'mla_w_a': 5.585041e-01, 'mla_g_qa': 3.420540e-02, 'mla_w_uq': 1.608961e-02, 'mla_g_kva': 2.008465e+00, 'mla_w_ukv': 1.984740e-01, 'mla_g_q': 7.338992e-02, 'mla_g_k': 7.054944e-02, 'mla_w_o': 2.617224e-01}


def _to_microbatches(a, axis):
    t = _jnp.moveaxis(a, axis, 0)
    t = t.reshape((N_MICROBATCH, t.shape[0] // N_MICROBATCH) + t.shape[1:])
    return _jnp.moveaxis(t, 1, axis + 1)


def setup_inputs(seed: int = 0) -> dict:
    inp = _fwd_setup_inputs(seed)
    key = _jax.random.fold_in(_jax.random.key(seed), 7919)
    shape, _ = _output_shape()
    out = dict(inp)
    out["loss_target"] = _jax.random.normal(_jax.random.fold_in(key, 0), shape, _jnp.float32)
    for i, name in enumerate(TWIN_WEIGHTS):
        w = inp[name].astype(_jnp.float32)
        if MOMENT_SCALE is None:
            s = _jnp.sqrt(_jnp.mean(_jnp.square(w)) + 1e-30)
        else:
            s = MOMENT_SCALE[name]
        km, kv = _jax.random.split(_jax.random.fold_in(key, i + 1))
        out[name] = w
        out["m_" + name] = s * _jax.random.normal(km, w.shape, _jnp.float32)
        out["v_" + name] = (s * s) * _jax.random.uniform(kv, w.shape, _jnp.float32, 0.5, 1.5)
    if N_MICROBATCH > 1:
        for name, axis in PER_EXAMPLE_BATCH_AXIS.items():
            out[name] = _to_microbatches(out[name], axis)
    return {'x': out['x'], 'c': out['c'], 'ctx': out['ctx'], 'c_ctx': out['c_ctx'], 'w_mod': out['w_mod'], 'b_mod': out['b_mod'], 'g_norm': out['g_norm'], 'ffn_w1': out['ffn_w1'], 'ffn_w3': out['ffn_w3'], 'ffn_w2': out['ffn_w2'], 'sc_w_in': out['sc_w_in'], 'sc_conv': out['sc_conv'], 'sc_w_out': out['sc_w_out'], 'mla_w_a': out['mla_w_a'], 'mla_g_qa': out['mla_g_qa'], 'mla_w_uq': out['mla_w_uq'], 'mla_g_kva': out['mla_g_kva'], 'mla_w_ukv': out['mla_w_ukv'], 'mla_g_q': out['mla_g_q'], 'mla_g_k': out['mla_g_k'], 'mla_w_o': out['mla_w_o'], 'loss_target': out['loss_target'], 'm_c_ctx': out['m_c_ctx'], 'm_w_mod': out['m_w_mod'], 'm_b_mod': out['m_b_mod'], 'm_g_norm': out['m_g_norm'], 'm_ffn_w1': out['m_ffn_w1'], 'm_ffn_w3': out['m_ffn_w3'], 'm_ffn_w2': out['m_ffn_w2'], 'm_sc_w_in': out['m_sc_w_in'], 'm_sc_conv': out['m_sc_conv'], 'm_sc_w_out': out['m_sc_w_out'], 'm_mla_w_a': out['m_mla_w_a'], 'm_mla_g_qa': out['m_mla_g_qa'], 'm_mla_w_uq': out['m_mla_w_uq'], 'm_mla_g_kva': out['m_mla_g_kva'], 'm_mla_w_ukv': out['m_mla_w_ukv'], 'm_mla_g_q': out['m_mla_g_q'], 'm_mla_g_k': out['m_mla_g_k'], 'm_mla_w_o': out['m_mla_w_o'], 'v_c_ctx': out['v_c_ctx'], 'v_w_mod': out['v_w_mod'], 'v_b_mod': out['v_b_mod'], 'v_g_norm': out['v_g_norm'], 'v_ffn_w1': out['v_ffn_w1'], 'v_ffn_w3': out['v_ffn_w3'], 'v_ffn_w2': out['v_ffn_w2'], 'v_sc_w_in': out['v_sc_w_in'], 'v_sc_conv': out['v_sc_conv'], 'v_sc_w_out': out['v_sc_w_out'], 'v_mla_w_a': out['v_mla_w_a'], 'v_mla_g_qa': out['v_mla_g_qa'], 'v_mla_w_uq': out['v_mla_w_uq'], 'v_mla_g_kva': out['v_mla_g_kva'], 'v_mla_w_ukv': out['v_mla_w_ukv'], 'v_mla_g_q': out['v_mla_g_q'], 'v_mla_g_k': out['v_mla_g_k'], 'v_mla_w_o': out['v_mla_w_o']}


def _loss(weights, diff, rest, loss_target):
    with _jax.named_scope("forward"):
        args = {**rest, TWIN_DIFF_INPUT: diff, **{k: w.astype(_WEIGHT_DTYPES[k]) for k, w in weights.items()}}
        y = _forward(args)
    with _jax.named_scope("loss_head"):
        err = _jnp.square(y.astype(_jnp.float32) - loss_target)
        return 0.5 * _jnp.sum(_jnp.mean(err, axis=-1)) if err.ndim else 0.5 * err


def _adamw(w, g, m, v):
    m = ADAM_B1 * m + (1.0 - ADAM_B1) * g
    v = ADAM_B2 * v + (1.0 - ADAM_B2) * _jnp.square(g)
    m_hat = m / (1.0 - ADAM_B1 ** ADAM_STEP)
    v_hat = v / (1.0 - ADAM_B2 ** ADAM_STEP)
    delta = -ADAM_LR * (m_hat / (_jnp.sqrt(v_hat) + ADAM_EPS) + ADAM_WD * w)
    return delta, m, v


def reference(x, c, ctx, c_ctx, w_mod, b_mod, g_norm, ffn_w1, ffn_w3, ffn_w2, sc_w_in, sc_conv, sc_w_out, mla_w_a, mla_g_qa, mla_w_uq, mla_g_kva, mla_w_ukv, mla_g_q, mla_g_k, mla_w_o, loss_target, m_c_ctx, m_w_mod, m_b_mod, m_g_norm, m_ffn_w1, m_ffn_w3, m_ffn_w2, m_sc_w_in, m_sc_conv, m_sc_w_out, m_mla_w_a, m_mla_g_qa, m_mla_w_uq, m_mla_g_kva, m_mla_w_ukv, m_mla_g_q, m_mla_g_k, m_mla_w_o, v_c_ctx, v_w_mod, v_b_mod, v_g_norm, v_ffn_w1, v_ffn_w3, v_ffn_w2, v_sc_w_in, v_sc_conv, v_sc_w_out, v_mla_w_a, v_mla_g_qa, v_mla_w_uq, v_mla_g_kva, v_mla_w_ukv, v_mla_g_q, v_mla_g_k, v_mla_w_o):
    given = dict(x=x, c=c, ctx=ctx, c_ctx=c_ctx, w_mod=w_mod, b_mod=b_mod, g_norm=g_norm, ffn_w1=ffn_w1, ffn_w3=ffn_w3, ffn_w2=ffn_w2, sc_w_in=sc_w_in, sc_conv=sc_conv, sc_w_out=sc_w_out, mla_w_a=mla_w_a, mla_g_qa=mla_g_qa, mla_w_uq=mla_w_uq, mla_g_kva=mla_g_kva, mla_w_ukv=mla_w_ukv, mla_g_q=mla_g_q, mla_g_k=mla_g_k, mla_w_o=mla_w_o, loss_target=loss_target, m_c_ctx=m_c_ctx, m_w_mod=m_w_mod, m_b_mod=m_b_mod, m_g_norm=m_g_norm, m_ffn_w1=m_ffn_w1, m_ffn_w3=m_ffn_w3, m_ffn_w2=m_ffn_w2, m_sc_w_in=m_sc_w_in, m_sc_conv=m_sc_conv, m_sc_w_out=m_sc_w_out, m_mla_w_a=m_mla_w_a, m_mla_g_qa=m_mla_g_qa, m_mla_w_uq=m_mla_w_uq, m_mla_g_kva=m_mla_g_kva, m_mla_w_ukv=m_mla_w_ukv, m_mla_g_q=m_mla_g_q, m_mla_g_k=m_mla_g_k, m_mla_w_o=m_mla_w_o, v_c_ctx=v_c_ctx, v_w_mod=v_w_mod, v_b_mod=v_b_mod, v_g_norm=v_g_norm, v_ffn_w1=v_ffn_w1, v_ffn_w3=v_ffn_w3, v_ffn_w2=v_ffn_w2, v_sc_w_in=v_sc_w_in, v_sc_conv=v_sc_conv, v_sc_w_out=v_sc_w_out, v_mla_w_a=v_mla_w_a, v_mla_g_qa=v_mla_g_qa, v_mla_w_uq=v_mla_w_uq, v_mla_g_kva=v_mla_g_kva, v_mla_w_ukv=v_mla_w_ukv, v_mla_g_q=v_mla_g_q, v_mla_g_k=v_mla_g_k, v_mla_w_o=v_mla_w_o)
    weights = {n: given[n] for n in TWIN_WEIGHTS}
    shared = {n: given[n] for n in SHARED_INPUTS}
    per_example = {n: given[n] for n in ['x', 'c', 'ctx']}
    grad_fn = _jax.value_and_grad(_loss, argnums=(0, 1))

    def one_microbatch(ex, loss_target):
        ex = dict(ex)
        diff = ex.pop(TWIN_DIFF_INPUT)
        return grad_fn(weights, diff, {**shared, **ex}, loss_target)

    if N_MICROBATCH == 1:
        loss, (grad_w, grad_x) = one_microbatch(per_example, given["loss_target"])
    else:
        def body(carry, xs):
            loss_sum, grad_sum = carry
            l_k, (gw_k, gx_k) = one_microbatch(xs[0], xs[1])
            with _jax.named_scope("update"):
                return (loss_sum + l_k, _jax.tree.map(_jnp.add, grad_sum, gw_k)), gx_k

        init = (_jnp.zeros((), _jnp.float32), _jax.tree.map(_jnp.zeros_like, weights))
        (loss, grad_w), grad_x = _jax.lax.scan(body, init, (per_example, given["loss_target"]))
    with _jax.named_scope("update"):
        delta_w, new_m, new_v = {}, {}, {}
        for n in TWIN_WEIGHTS:
            delta_w[n], new_m[n], new_v[n] = _adamw(weights[n], grad_w[n], given["m_" + n], given["v_" + n])
    return (loss, grad_x, *[grad_w[n] for n in TWIN_WEIGHTS], *[delta_w[n] for n in TWIN_WEIGHTS],
            *[new_m[n] for n in TWIN_WEIGHTS], *[new_v[n] for n in TWIN_WEIGHTS])
```

```python
import functools
import math
from typing import NamedTuple

import jax
import jax.numpy as jnp
from jax import lax
from jax.experimental import pallas as pl
from jax.experimental.pallas import tpu as pltpu

F32 = jnp.float32
BF16 = jnp.bfloat16
EPS = 1e-6
GRID_W = 64
HEADS = 8
QK_NOPE = 128
QK_ROPE = 64
QK_HEAD = QK_NOPE + QK_ROPE
HEAD_PAD = 256
V_HEAD = 128
Q_LORA = 256
KV_LORA = 128
ROPE_BASE = 10000.0
QK_SCALE = QK_HEAD ** -0.5
ADAM_LR, ADAM_B1, ADAM_B2, ADAM_EPS, ADAM_WD, ADAM_STEP = 0.001, 0.9, 0.999, 1e-08, 0.01, 10
N_CHIPS = 4
VMEM_LIMIT = 56 * 1024 * 1024
MESH = pl.DeviceIdType.MESH
NEG = -1e30


class Dims(NamedTuple):
    B: int
    N: int
    CTX: int
    D: int
    T: int
    tm: int


def _cparams(*sem):
    return pltpu.CompilerParams(dimension_semantics=sem if sem else None, vmem_limit_bytes=VMEM_LIMIT)


def _dot(a, b):
    return jnp.dot(a, b, preferred_element_type=F32)


def _dot_nt(a, b):
    return lax.dot_general(a, b, (((1,), (1,)), ((), ())), preferred_element_type=F32)


def _dot_tn(a, b):
    return lax.dot_general(a, b, (((0,), (0,)), ((), ())), preferred_element_type=F32)


def _rms(x, n):
    r = lax.rsqrt(jnp.sum(x * x, axis=-1, keepdims=True) * (1.0 / n) + EPS)
    return x * r, r


def _rms_bwd(dxh, xh, r, n):
    return r * (dxh - xh * (jnp.sum(dxh * xh, axis=-1, keepdims=True) * (1.0 / n)))


def _pre(h, g, shift, scale):
    xh, _ = _rms(h, h.shape[-1])
    return (xh * g) * (1.0 + scale) + shift


def _pre_bwd(dout, h, g, scale):
    d = h.shape[-1]
    xh, r = _rms(h, d)
    n = xh * g
    dshift = jnp.sum(dout, axis=0, keepdims=True)
    dscale = jnp.sum(dout * n, axis=0, keepdims=True)
    dn = dout * (1.0 + scale)
    dg = jnp.sum(dn * xh, axis=0, keepdims=True)
    dh = _rms_bwd(dn * g, xh, r, d)
    return dh, dshift, dscale, dg


def _write_part(part_ref, dshift=None, dscale=None, dgate=None, dg=None):
    z = jnp.zeros((1, part_ref.shape[-1]), F32)
    part_ref[0, 0:1, :] = z if dshift is None else dshift
    part_ref[0, 1:2, :] = z if dscale is None else dscale
    part_ref[0, 2:3, :] = z if dgate is None else dgate
    part_ref[0, 3:4, :] = z if dg is None else dg
    part_ref[0, 4:8, :] = jnp.zeros((4, part_ref.shape[-1]), F32)


def _grp(dm):
    nb = dm.N // dm.tm
    return lambda i: jnp.minimum(i // nb, dm.B)


def _exchange(arrs, axes, scatter, name):
    n = len(arrs)
    nbits = len(axes)
    slots = 2 ** nbits
    pats = list(range(1, slots))

    def body(*refs):
        ins, outs = refs[:n], refs[n:2 * n]
        send, recv, loc = refs[2 * n:]
        pos = {a: lax.axis_index(a) for a in ("x", "y", "c")}

        def slot_of(p):
            s = 0
            for a in axes:
                s = 2 * s + p[a]
            return s

        me = slot_of(pos)
        local = []
        for i in range(n):
            cp = pltpu.make_async_copy(ins[i].at[me] if scatter else ins[i], outs[i].at[me], loc.at[i])
            cp.start()
            local.append(cp)
        remote = []
        for pi, pat in enumerate(pats):
            peer = dict(pos)
            for bi, a in enumerate(axes):
                if (pat >> (nbits - 1 - bi)) & 1:
                    peer[a] = 1 - pos[a]
            them = slot_of(peer)
            for i in range(n):
                k = pi * n + i
                cp = pltpu.make_async_remote_copy(
                    src_ref=ins[i].at[them] if scatter else ins[i], dst_ref=outs[i].at[me],
                    send_sem=send.at[k], recv_sem=recv.at[k],
                    device_id=(peer["x"], peer["y"], peer["c"]), device_id_type=MESH)
                cp.start()
                remote.append(cp)
        for cp in local:
            cp.wait()
        for cp in remote:
            cp.wait()

    out_shape = [jax.ShapeDtypeStruct(a.shape if scatter else (slots,) + a.shape, a.dtype) for a in arrs]
    any_spec = pl.BlockSpec(memory_space=pl.ANY)
    outs = pl.pallas_call(
        body, name=name, out_shape=out_shape, in_specs=[any_spec] * n, out_specs=[any_spec] * n,
        scratch_shapes=[pltpu.SemaphoreType.DMA((len(pats) * n,)), pltpu.SemaphoreType.DMA((len(pats) * n,)),
                        pltpu.SemaphoreType.DMA((n,))],
        compiler_params=pltpu.CompilerParams(has_side_effects=True),
    )(*arrs)
    return list(outs)


def _ffn_fwd(h, mod, g, w1, w3, w2, l, k, dm, name):
    T, D = h.shape
    S, F = w1.shape[0], w1.shape[-1]
    tm = dm.tm
    r0 = 6 if k else 0
    grp = _grp(dm)

    def body(h_ref, mod_ref, g_ref, w1_ref, w3_ref, w2_ref, ho_ref, a_ref, b_ref, hn_ref, y_ref, hn_s, acc):
        s = pl.program_id(1)

        @pl.when(s == 0)
        def _():
            hn = _pre(h_ref[...], g_ref[...], mod_ref[0, r0:r0 + 1, :], mod_ref[0, r0 + 1:r0 + 2, :]).astype(BF16)
            hn_s[...] = hn
            hn_ref[...] = hn
            acc[...] = jnp.zeros_like(acc)

        hn = hn_s[...]
        a = _dot(hn, w1_ref[...])
        b = _dot(hn, w3_ref[...])
        a_ref[0] = a.astype(BF16)
        b_ref[0] = b.astype(BF16)
        sw = (a * jax.nn.sigmoid(a) * b).astype(BF16)
        acc[...] += _dot(sw, w2_ref[...])

        @pl.when(s == S - 1)
        def _():
            y = acc[...]
            y_ref[...] = y.astype(BF16)
            ho_ref[...] = h_ref[...] + 0.5 * mod_ref[0, r0 + 2:r0 + 3, :] * y

    row = pl.BlockSpec((tm, D), lambda i, s: (i, 0))
    wcol = pl.BlockSpec((None, None, None, D, F), lambda i, s: (s, l, k, 0, 0))
    wrow = pl.BlockSpec((None, None, None, F, D), lambda i, s: (s, l, k, 0, 0))
    ab = pl.BlockSpec((1, tm, F), lambda i, s: (s, i, 0))
    return pl.pallas_call(
        body, name=name, grid=(T // tm, S),
        in_specs=[row, pl.BlockSpec((1, 9, D), lambda i, s: (grp(i), 0, 0)), pl.BlockSpec((1, D), lambda i, s: (0, 0)),
                  wcol, wcol, wrow],
        out_specs=[row, ab, ab, row, row],
        out_shape=[jax.ShapeDtypeStruct((T, D), F32), jax.ShapeDtypeStruct((S, T, F), BF16),
                   jax.ShapeDtypeStruct((S, T, F), BF16), jax.ShapeDtypeStruct((T, D), BF16),
                   jax.ShapeDtypeStruct((T, D), BF16)],
        scratch_shapes=[pltpu.VMEM((tm, D), BF16), pltpu.VMEM((tm, D), F32)],
        compiler_params=_cparams("parallel", "arbitrary"),
    )(h, mod, g, w1, w3, w2)


def _ffn_bwd(dh, h, mod, g, y, a, b, w1, w3, w2, l, k, dm, name):
    T, D = h.shape
    S, F = w1.shape[0], w1.shape[-1]
    tm = dm.tm
    r0 = 6 if k else 0
    grp = _grp(dm)

    def body(dh_ref, h_ref, mod_ref, g_ref, y_ref, a_ref, b_ref, w1_ref, w3_ref, w2_ref,
             dho_ref, da_ref, db_ref, sw_ref, dy_ref, part_ref, dy_s, acc):
        s = pl.program_id(1)

        @pl.when(s == 0)
        def _():
            dy = (0.5 * mod_ref[0, r0 + 2:r0 + 3, :] * dh_ref[...]).astype(BF16)
            dy_s[...] = dy
            dy_ref[...] = dy
            acc[...] = jnp.zeros_like(acc)

        ds = _dot_nt(dy_s[...], w2_ref[...])
        av = a_ref[0].astype(F32)
        bv = b_ref[0].astype(F32)
        sig = jax.nn.sigmoid(av)
        sil = av * sig
        sw_ref[0] = (sil * bv).astype(BF16)
        db = (ds * sil).astype(BF16)
        da = (ds * bv * (sig * (1.0 + av * (1.0 - sig)))).astype(BF16)
        da_ref[0] = da
        db_ref[0] = db
        acc[...] += _dot_nt(da, w1_ref[...]) + _dot_nt(db, w3_ref[...])

        @pl.when(s == S - 1)
        def _():
            dhv = dh_ref[...]
            dhb, dshift, dscale, dg = _pre_bwd(acc[...], h_ref[...], g_ref[...], mod_ref[0, r0 + 1:r0 + 2, :])
            dho_ref[...] = dhv + dhb
            dgate = 0.5 * jnp.sum(dhv * y_ref[...].astype(F32), axis=0, keepdims=True)
            _write_part(part_ref, dshift, dscale, dgate, dg)

    row = pl.BlockSpec((tm, D), lambda i, s: (i, 0))
    wcol = pl.BlockSpec((None, None, None, D, F), lambda i, s: (s, l, k, 0, 0))
    wrow = pl.BlockSpec((None, None, None, F, D), lambda i, s: (s, l, k, 0, 0))
    ab = pl.BlockSpec((1, tm, F), lambda i, s: (s, i, 0))
    stf = jax.ShapeDtypeStruct((S, T, F), BF16)
    return pl.pallas_call(
        body, name=name, grid=(T // tm, S),
        in_specs=[row, row, pl.BlockSpec((1, 9, D), lambda i, s: (grp(i), 0, 0)), pl.BlockSpec((1, D), lambda i, s: (0, 0)),
                  row, ab, ab, wcol, wcol, wrow],
        out_specs=[row, ab, ab, ab, row, pl.BlockSpec((1, 8, D), lambda i, s: (i, 0, 0))],
        out_shape=[jax.ShapeDtypeStruct((T, D), F32), stf, stf, stf, jax.ShapeDtypeStruct((T, D), BF16),
                   jax.ShapeDtypeStruct((T // tm, 8, D), F32)],
        scratch_shapes=[pltpu.VMEM((tm, D), BF16), pltpu.VMEM((tm, D), F32)],
        compiler_params=_cparams("parallel", "arbitrary"),
    )(dh, h, mod, g, y, a, b, w1, w3, w2)


def _mm_tn(a, b, a_spec, b_spec, out, out_spec, grid, name):
    nk = grid[-1]
    kax = len(grid) - 1
    blk = tuple(d for d in out_spec.block_shape if d is not None)

    def body(a_ref, b_ref, o_in, o_ref, acc):
        del o_in
        kk = pl.program_id(kax)

        @pl.when(kk == 0)
        def _():
            acc[...] = jnp.zeros_like(acc)

        acc[...] += _dot_tn(a_ref[...].astype(BF16), b_ref[...].astype(BF16))

        @pl.when(kk == nk - 1)
        def _():
            o_ref[...] = acc[...].astype(o_ref.dtype)

    return pl.pallas_call(
        body, name=name, grid=grid,
        in_specs=[a_spec, b_spec, pl.BlockSpec(memory_space=pl.ANY)], out_specs=out_spec,
        out_shape=jax.ShapeDtypeStruct(out.shape, out.dtype),
        scratch_shapes=[pltpu.VMEM(blk, F32)], input_output_aliases={2: 0},
        compiler_params=_cparams(*(["parallel"] * kax + ["arbitrary"])),
    )(a, b, out)


def _sc_in_fwd(h, mod, g, w_in, j, dm, name):
    T, D = h.shape
    tm = dm.tm
    wq = D // N_CHIPS
    nq = 3 * N_CHIPS
    grp = _grp(dm)

    def body(h_ref, mod_ref, g_ref, w_ref, p_ref, hn_ref, hn_s):
        @pl.when(pl.program_id(1) == 0)
        def _():
            hn = _pre(h_ref[...], g_ref[...], mod_ref[0, 3:4, :], mod_ref[0, 4:5, :]).astype(BF16)
            hn_s[...] = hn
            hn_ref[...] = hn

        p_ref[...] = _dot(hn_s[...], w_ref[...])

    row = pl.BlockSpec((tm, D), lambda i, q: (i, 0))
    return pl.pallas_call(
        body, name=name, grid=(T // tm, nq),
        in_specs=[row, pl.BlockSpec((1, 9, D), lambda i, q: (grp(i), 0, 0)), pl.BlockSpec((1, D), lambda i, q: (0, 0)),
                  pl.BlockSpec((None, None, D, wq), lambda i, q: (q // 3, j, 0, q % 3))],
        out_specs=[pl.BlockSpec((None, tm, wq), lambda i, q: (q // N_CHIPS, i, q % N_CHIPS)), row],
        out_shape=[jax.ShapeDtypeStruct((3, T, D), F32), jax.ShapeDtypeStruct((T, D), BF16)],
        scratch_shapes=[pltpu.VMEM((tm, D), BF16)],
        compiler_params=_cparams("parallel", "arbitrary"),
    )(h, mod, g, w_in)


def _conv_cols(dm):
    return 256 if dm.D % 256 == 0 else 128


def _seg_masks(r, dm):
    bn = dm.B * dm.N
    lat = r < bn
    off = jnp.where(lat, lax.rem(r, dm.N), lax.rem(r - bn, dm.CTX))
    seg = jnp.where(lat, dm.N, dm.CTX)
    inside = (r >= 0) & (r < dm.T)
    return ((off != 0) & inside).astype(F32), ((off != seg - 1) & inside).astype(F32)


def _conv_specs(dm):
    tb, cb, nr8 = dm.tm, _conv_cols(dm), dm.T // 8
    prev8 = lambda c, i: jnp.maximum(i * (tb // 8) - 1, 0)
    next8 = lambda c, i: jnp.minimum((i + 1) * (tb // 8), nr8 - 1)
    return dict(
        tb=tb, cb=cb,
        p=pl.BlockSpec((3, tb, cb), lambda c, i: (0, i, c)),
        p_prev=pl.BlockSpec((3, 8, cb), lambda c, i: (0, prev8(c, i), c)),
        p_next=pl.BlockSpec((3, 8, cb), lambda c, i: (0, next8(c, i), c)),
        row=pl.BlockSpec((tb, cb), lambda c, i: (i, c)),
        row_prev=pl.BlockSpec((8, cb), lambda c, i: (prev8(c, i), c)),
        row_next=pl.BlockSpec((8, cb), lambda c, i: (next8(c, i), c)),
        w=pl.BlockSpec((3, cb), lambda c, i: (0, c)),
    )


def _shift_rows(x, before, after, tb):
    rid = lax.broadcasted_iota(jnp.int32, x.shape, 0)
    down = jnp.where(rid == 0, before, pltpu.roll(x, 1, 0))
    up = jnp.where(rid == tb - 1, after, pltpu.roll(x, tb - 1, 0))
    return down, up


def _conv_fwd(p, wc, dm, name):
    T, D = dm.T, dm.D
    sp = _conv_specs(dm)
    tb, cb = sp["tb"], sp["cb"]

    def body(p_ref, pp_ref, pn_ref, w_ref, z_ref):
        r = pl.program_id(1) * tb + lax.broadcasted_iota(jnp.int32, (tb, cb), 0)
        mp, mn = _seg_masks(r, dm)
        cu = p_ref[1] * p_ref[2]
        prev, nxt = _shift_rows(cu, pp_ref[1, 7:8, :] * pp_ref[2, 7:8, :], pn_ref[1, 0:1, :] * pn_ref[2, 0:1, :], tb)
        conv = w_ref[0:1, :] * (prev * mp) + w_ref[1:2, :] * cu + w_ref[2:3, :] * (nxt * mn)
        z_ref[...] = (p_ref[0] * conv).astype(BF16)

    return pl.pallas_call(
        body, name=name, grid=(D // cb, T // tb),
        in_specs=[sp["p"], sp["p_prev"], sp["p_next"], sp["w"]], out_specs=sp["row"],
        out_shape=jax.ShapeDtypeStruct((T, D), BF16),
        compiler_params=_cparams("parallel", "parallel"),
    )(p, p, p, wc)


def _conv_bwd(dz, p, wc, dm, name):
    T, D = dm.T, dm.D
    sp = _conv_specs(dm)
    tb, cb = sp["tb"], sp["cb"]

    def body(dz_ref, dzp_ref, dzn_ref, p_ref, pp_ref, pn_ref, w_ref, dp_ref, dw_ref):
        i = pl.program_id(1)
        r = i * tb + lax.broadcasted_iota(jnp.int32, (tb, cb), 0)
        mp, mn = _seg_masks(r, dm)
        rb = i * tb + lax.broadcasted_iota(jnp.int32, (1, cb), 0)
        _, mn_before = _seg_masks(rb - 1, dm)
        mp_after, _ = _seg_masks(rb + tb, dm)
        bg, cg, u = p_ref[0], p_ref[1], p_ref[2]
        cu = cg * u
        prev, nxt = _shift_rows(cu, pp_ref[1, 7:8, :] * pp_ref[2, 7:8, :], pn_ref[1, 0:1, :] * pn_ref[2, 0:1, :], tb)
        prev = prev * mp
        nxt = nxt * mn
        w0, w1, w2 = w_ref[0:1, :], w_ref[1:2, :], w_ref[2:3, :]
        conv = w0 * prev + w1 * cu + w2 * nxt
        dz = dz_ref[...]
        dp_ref[0] = dz * conv
        dconv = dz * bg

        @pl.when(i == 0)
        def _():
            dw_ref[...] = jnp.zeros_like(dw_ref)

        dw_ref[0:1, :] += jnp.sum(dconv * prev, axis=0, keepdims=True)
        dw_ref[1:2, :] += jnp.sum(dconv * cu, axis=0, keepdims=True)
        dw_ref[2:3, :] += jnp.sum(dconv * nxt, axis=0, keepdims=True)
        dconv_before = dzp_ref[7:8, :] * pp_ref[0, 7:8, :] * mn_before
        dconv_after = dzn_ref[0:1, :] * pn_ref[0, 0:1, :] * mp_after
        from_prev, _ = _shift_rows(dconv * mn, dconv_before, dconv_after, tb)
        _, from_next = _shift_rows(dconv * mp, dconv_before, dconv_after, tb)
        dcu = w1 * dconv + w0 * from_next + w2 * from_prev
        dp_ref[1] = dcu * u
        dp_ref[2] = dcu * cg

    return pl.pallas_call(
        body, name=name, grid=(D // cb, T // tb),
        in_specs=[sp["row"], sp["row_prev"], sp["row_next"], sp["p"], sp["p_prev"], sp["p_next"], sp["w"]],
        out_specs=[sp["p"], sp["w"]],
        out_shape=[jax.ShapeDtypeStruct((3, T, D), F32), jax.ShapeDtypeStruct((3, D), F32)],
        compiler_params=_cparams("parallel", "arbitrary"),
    )(dz, dz, dz, p, p, p, wc)


def _out_fwd(z, w, h, mod, j, dm, name):
    T, D = h.shape
    K = z.shape[1]
    tm = dm.tm
    grp = _grp(dm)

    def body(z_ref, w_ref, h_ref, mod_ref, ho_ref, y_ref):
        y = _dot(z_ref[...], w_ref[...])
        y_ref[...] = y.astype(BF16)
        ho_ref[...] = h_ref[...] + mod_ref[0, 5:6, :] * y

    row = pl.BlockSpec((tm, D), lambda i: (i, 0))
    return pl.pallas_call(
        body, name=name, grid=(T // tm,),
        in_specs=[pl.BlockSpec((tm, K), lambda i: (i, 0)), pl.BlockSpec((None, K, D), lambda i: (j, 0, 0)), row,
                  pl.BlockSpec((1, 9, D), lambda i: (grp(i), 0, 0))],
        out_specs=[row, row],
        out_shape=[jax.ShapeDtypeStruct((T, D), F32), jax.ShapeDtypeStruct((T, D), BF16)],
        compiler_params=_cparams("parallel"),
    )(z, w, h, mod)


def _out_bwd(dh, y, w, mod, j, dm, name):
    T, D = dh.shape
    K = w.shape[1]
    tm = dm.tm
    grp = _grp(dm)

    def body(dh_ref, y_ref, w_ref, mod_ref, dy_ref, dz_ref, part_ref):
        dhv = dh_ref[...]
        dy = (mod_ref[0, 5:6, :] * dhv).astype(BF16)
        dy_ref[...] = dy
        dz_ref[...] = _dot_nt(dy, w_ref[...])
        _write_part(part_ref, dgate=jnp.sum(dhv * y_ref[...].astype(F32), axis=0, keepdims=True))

    row = pl.BlockSpec((tm, D), lambda i: (i, 0))
    return pl.pallas_call(
        body, name=name, grid=(T // tm,),
        in_specs=[row, row, pl.BlockSpec((None, K, D), lambda i: (j, 0, 0)), pl.BlockSpec((1, 9, D), lambda i: (grp(i), 0, 0))],
        out_specs=[row, pl.BlockSpec((tm, K), lambda i: (i, 0)), pl.BlockSpec((1, 8, D), lambda i: (i, 0, 0))],
        out_shape=[jax.ShapeDtypeStruct((T, D), BF16), jax.ShapeDtypeStruct((T, K), F32),
                   jax.ShapeDtypeStruct((T // tm, 8, D), F32)],
        compiler_params=_cparams("parallel"),
    )(dh, y, w, mod)


def _sc_in_bwd(dh, dp, h, mod, g, w_in, j, dm, name):
    T, D = h.shape
    tm = dm.tm
    wq = D // N_CHIPS
    nq = 3 * N_CHIPS
    grp = _grp(dm)

    def body(dh_ref, dp_ref, h_ref, mod_ref, g_ref, w_ref, dho_ref, part_ref, acc):
        q = pl.program_id(1)

        @pl.when(q == 0)
        def _():
            acc[...] = jnp.zeros_like(acc)

        acc[...] += _dot_nt(dp_ref[...].astype(BF16), w_ref[...])

        @pl.when(q == nq - 1)
        def _():
            dhb, dshift, dscale, dg = _pre_bwd(acc[...], h_ref[...], g_ref[...], mod_ref[0, 4:5, :])
            dho_ref[...] = dh_ref[...] + dhb
            _write_part(part_ref, dshift, dscale, None, dg)

    row = pl.BlockSpec((tm, D), lambda i, q: (i, 0))
    return pl.pallas_call(
        body, name=name, grid=(T // tm, nq),
        in_specs=[row, pl.BlockSpec((None, tm, wq), lambda i, q: (q // N_CHIPS, i, q % N_CHIPS)), row,
                  pl.BlockSpec((1, 9, D), lambda i, q: (grp(i), 0, 0)), pl.BlockSpec((1, D), lambda i, q: (0, 0)),
                  pl.BlockSpec((None, None, D, wq), lambda i, q: (q // 3, j, 0, q % 3))],
        out_specs=[row, pl.BlockSpec((1, 8, D), lambda i, q: (i, 0, 0))],
        out_shape=[jax.ShapeDtypeStruct((T, D), F32), jax.ShapeDtypeStruct((T // tm, 8, D), F32)],
        scratch_shapes=[pltpu.VMEM((tm, D), F32)],
        compiler_params=_cparams("parallel", "arbitrary"),
    )(dh, dp, h, mod, g, w_in)


def _rope(t, c, s1, s2):
    return t * c + pltpu.roll(t, HEAD_PAD - 16, 1) * s1 + pltpu.roll(t, 16, 1) * s2


def _rope_t(dy, c, s1, s2):
    return dy * c + pltpu.roll(dy * s1, 16, 1) + pltpu.roll(dy * s2, HEAD_PAD - 16, 1)


def _mla_heads_fwd(z, g_ref, wuq_ref, wukv_ref):
    cq, ckv, krp = z[:, :Q_LORA], z[:, Q_LORA:Q_LORA + KV_LORA], z[:, Q_LORA + KV_LORA:]
    cqh, rq = _rms(cq, Q_LORA)
    ckvh, rkv = _rms(ckv, KV_LORA)
    cqn = (cqh * g_ref[0:1, :]).astype(BF16)
    ckvn = (ckvh * g_ref[1:2, :KV_LORA]).astype(BF16)
    qraw = _dot(cqn, wuq_ref[...])
    kvraw = _dot(ckvn, wukv_ref[...])
    return dict(krp=krp, cqh=cqh, rq=rq, ckvh=ckvh, rkv=rkv, cqn=cqn, ckvn=ckvn, qraw=qraw, kvraw=kvraw)


def _mla_proj_fwd(h, mod, g, gains, tabs, w_a, w_uq, w_ukv, j, dm, name):
    T, D = h.shape
    tm = min(dm.tm, 256)
    grp = lambda i: jnp.minimum(i // (dm.N // tm), dm.B)
    HP = HEAD_PAD

    def body(h_ref, mod_ref, g_ref, gn_ref, tab_ref, wa_ref, wuq_ref, wukv_ref, hn_ref, q_ref, k_ref, v_ref):
        hn = _pre(h_ref[...], g_ref[...], mod_ref[0, 3:4, :], mod_ref[0, 4:5, :]).astype(BF16)
        hn_ref[...] = hn
        f = _mla_heads_fwd(_dot(hn, wa_ref[...]), gn_ref, wuq_ref, wukv_ref)
        c, s1, s2 = tab_ref[0], tab_ref[1], tab_ref[2]
        for hd in range(HEADS):
            qh, _ = _rms(f["qraw"][:, hd * HP:(hd + 1) * HP], QK_HEAD)
            q_ref[:, hd * HP:(hd + 1) * HP] = _rope(qh * gn_ref[2:3, :], c, s1, s2).astype(BF16)
            kpre = jnp.concatenate([f["kvraw"][:, hd * HP:hd * HP + QK_NOPE], f["krp"]], axis=1)
            kh, _ = _rms(kpre, QK_HEAD)
            k_ref[:, hd * HP:(hd + 1) * HP] = _rope(kh * gn_ref[3:4, :], c, s1, s2).astype(BF16)
            v_ref[:, hd * V_HEAD:(hd + 1) * V_HEAD] = f["kvraw"][:, hd * HP + QK_NOPE:(hd + 1) * HP].astype(BF16)

    row = pl.BlockSpec((tm, D), lambda i: (i, 0))
    HQ = HEADS * HP
    return pl.pallas_call(
        body, name=name, grid=(T // tm,),
        in_specs=[row, pl.BlockSpec((1, 9, D), lambda i: (grp(i), 0, 0)), pl.BlockSpec((1, D), lambda i: (0, 0)),
                  pl.BlockSpec((None, 8, HP), lambda i: (j, 0, 0)), pl.BlockSpec((3, tm, HP), lambda i: (0, i, 0)),
                  pl.BlockSpec((None, D, 512), lambda i: (j, 0, 0)), pl.BlockSpec((None, Q_LORA, HQ), lambda i: (j, 0, 0)),
                  pl.BlockSpec((None, KV_LORA, HQ), lambda i: (j, 0, 0))],
        out_specs=[row, pl.BlockSpec((tm, HQ), lambda i: (i, 0)), pl.BlockSpec((tm, HQ), lambda i: (i, 0)),
                   pl.BlockSpec((tm, HEADS * V_HEAD), lambda i: (i, 0))],
        out_shape=[jax.ShapeDtypeStruct((T, D), BF16), jax.ShapeDtypeStruct((T, HQ), BF16),
                   jax.ShapeDtypeStruct((T, HQ), BF16), jax.ShapeDtypeStruct((T, HEADS * V_HEAD), BF16)],
        compiler_params=_cparams("parallel"),
    )(h, mod, g, gains, tabs, w_a, w_uq, w_ukv)


def _mla_proj_bwd(dh, dq, dk, dv, h, mod, g, gains, tabs, w_a, w_uq, w_ukv, j, dm, name):
    T, D = h.shape
    tm = min(dm.tm, 256)
    nblk = T // tm
    grp = lambda i: jnp.minimum(i // (dm.N // tm), dm.B)
    HP = HEAD_PAD
    HQ = HEADS * HP

    def body(dh_ref, dq_ref, dk_ref, dv_ref, h_ref, mod_ref, g_ref, gn_ref, tab_ref, wa_ref, wuq_ref, wukv_ref,
             dho_ref, part_ref, gwa_ref, gwuq_ref, gwukv_ref, dgn_ref, dqraw_s, dkvraw_s):
        i = pl.program_id(0)

        @pl.when(i == 0)
        def _():
            gwa_ref[...] = jnp.zeros_like(gwa_ref)
            gwuq_ref[...] = jnp.zeros_like(gwuq_ref)
            gwukv_ref[...] = jnp.zeros_like(gwukv_ref)
            dgn_ref[...] = jnp.zeros_like(dgn_ref)

        hv = h_ref[...]
        hn = _pre(hv, g_ref[...], mod_ref[0, 3:4, :], mod_ref[0, 4:5, :]).astype(BF16)
        f = _mla_heads_fwd(_dot(hn, wa_ref[...]), gn_ref, wuq_ref, wukv_ref)
        c, s1, s2 = tab_ref[0], tab_ref[1], tab_ref[2]
        gq, gk = gn_ref[2:3, :], gn_ref[3:4, :]
        dgq = jnp.zeros((1, HP), F32)
        dgk = jnp.zeros((1, HP), F32)
        dkrp = jnp.zeros((tm, HP - QK_NOPE), F32)
        for hd in range(HEADS):
            qh, rq = _rms(f["qraw"][:, hd * HP:(hd + 1) * HP], QK_HEAD)
            dqn = _rope_t(dq_ref[:, hd * HP:(hd + 1) * HP], c, s1, s2)
            dgq = dgq + jnp.sum(dqn * qh, axis=0, keepdims=True)
            dqraw_s[:, hd * HP:(hd + 1) * HP] = _rms_bwd(dqn * gq, qh, rq, QK_HEAD)
            kpre = jnp.concatenate([f["kvraw"][:, hd * HP:hd * HP + QK_NOPE], f["krp"]], axis=1)
            kh, rk = _rms(kpre, QK_HEAD)
            dkn = _rope_t(dk_ref[:, hd * HP:(hd + 1) * HP], c, s1, s2)
            dgk = dgk + jnp.sum(dkn * kh, axis=0, keepdims=True)
            dkpre = _rms_bwd(dkn * gk, kh, rk, QK_HEAD)
            dkvraw_s[:, hd * HP:hd * HP + QK_NOPE] = dkpre[:, :QK_NOPE]
            dkrp = dkrp + dkpre[:, QK_NOPE:]
            dkvraw_s[:, hd * HP + QK_NOPE:(hd + 1) * HP] = dv_ref[:, hd * V_HEAD:(hd + 1) * V_HEAD]
        dqraw = dqraw_s[...].astype(BF16)
        dkvraw = dkvraw_s[...].astype(BF16)
        gwuq_ref[...] += _dot_tn(f["cqn"], dqraw)
        gwukv_ref[...] += _dot_tn(f["ckvn"], dkvraw)
        dcqn = _dot_nt(dqraw, wuq_ref[...])
        dckvn = _dot_nt(dkvraw, wukv_ref[...])
        dgqa = jnp.sum(dcqn * f["cqh"], axis=0, keepdims=True)
        dgkva = jnp.sum(dckvn * f["ckvh"], axis=0, keepdims=True)
        dcq = _rms_bwd(dcqn * gn_ref[0:1, :], f["cqh"], f["rq"], Q_LORA)
        dckv = _rms_bwd(dckvn * gn_ref[1:2, :KV_LORA], f["ckvh"], f["rkv"], KV_LORA)
        dz = jnp.concatenate([dcq, dckv, dkrp], axis=1).astype(BF16)
        gwa_ref[...] += _dot_tn(hn, dz)
        dhn = _dot_nt(dz, wa_ref[...])
        dhb, dshift, dscale, dg = _pre_bwd(dhn, hv, g_ref[...], mod_ref[0, 4:5, :])
        dho_ref[...] = dh_ref[...] + dhb
        _write_part(part_ref, dshift, dscale, None, dg)
        dgn_ref[0:1, :] += dgqa
        dgn_ref[1:2, :KV_LORA] += dgkva
        dgn_ref[2:3, :] += dgq
        dgn_ref[3:4, :] += dgk

    row = pl.BlockSpec((tm, D), lambda i: (i, 0))
    wide = pl.BlockSpec((tm, HQ), lambda i: (i, 0))
    const2 = lambda i: (0, 0)
    return pl.pallas_call(
        body, name=name, grid=(nblk,),
        in_specs=[row, wide, wide, pl.BlockSpec((tm, HEADS * V_HEAD), lambda i: (i, 0)), row,
                  pl.BlockSpec((1, 9, D), lambda i: (grp(i), 0, 0)), pl.BlockSpec((1, D), const2),
                  pl.BlockSpec((None, 8, HP), lambda i: (j, 0, 0)), pl.BlockSpec((3, tm, HP), lambda i: (0, i, 0)),
                  pl.BlockSpec((None, D, 512), lambda i: (j, 0, 0)), pl.BlockSpec((None, Q_LORA, HQ), lambda i: (j, 0, 0)),
                  pl.BlockSpec((None, KV_LORA, HQ), lambda i: (j, 0, 0))],
        out_specs=[row, pl.BlockSpec((1, 8, D), lambda i: (i, 0, 0)), pl.BlockSpec((D, 512), const2),
                   pl.BlockSpec((Q_LORA, HQ), const2), pl.BlockSpec((KV_LORA, HQ), const2), pl.BlockSpec((8, HP), const2)],
        out_shape=[jax.ShapeDtypeStruct((T, D), F32), jax.ShapeDtypeStruct((nblk, 8, D), F32),
                   jax.ShapeDtypeStruct((D, 512), F32), jax.ShapeDtypeStruct((Q_LORA, HQ), F32),
                   jax.ShapeDtypeStruct((KV_LORA, HQ), F32), jax.ShapeDtypeStruct((8, HP), F32)],
        scratch_shapes=[pltpu.VMEM((tm, HQ), F32), pltpu.VMEM((tm, HQ), F32)],
        compiler_params=_cparams("arbitrary"),
    )(dh, dq, dk, dv, h, mod, g, gains, tabs, w_a, w_uq, w_ukv)


def _attn_specs(dm):
    tq = dm.CTX
    nq = dm.N // tq
    cblk0 = dm.B * nq
    HP = HEAD_PAD
    qrow = lambda b, i: jnp.where(i < nq, b * nq + i, cblk0 + b)
    return dict(
        tq=tq, nq=nq,
        q=pl.BlockSpec((tq, HP), lambda b, hd, i: (qrow(b, i), hd)),
        k_lat=pl.BlockSpec((dm.N, HP), lambda b, hd, i: (b, hd)),
        k_ctx=pl.BlockSpec((tq, HP), lambda b, hd, i: (cblk0 + b, hd)),
        v_lat=pl.BlockSpec((dm.N, V_HEAD), lambda b, hd, i: (b, hd)),
        v_ctx=pl.BlockSpec((tq, V_HEAD), lambda b, hd, i: (cblk0 + b, hd)),
        o=pl.BlockSpec((tq, V_HEAD), lambda b, hd, i: (qrow(b, i), hd)),
    )


def _attn_probs(q, kl, kc, is_ctx):
    sl = _dot_nt(q, kl) * QK_SCALE
    sc = _dot_nt(q, kc) * QK_SCALE
    sl = sl + jnp.where(is_ctx, NEG, 0.0)
    m = jnp.maximum(jnp.max(sl, axis=-1, keepdims=True), jnp.max(sc, axis=-1, keepdims=True))
    pl_, pc = jnp.exp(sl - m), jnp.exp(sc - m)
    inv = 1.0 / (jnp.sum(pl_, axis=-1, keepdims=True) + jnp.sum(pc, axis=-1, keepdims=True))
    return pl_ * inv, pc * inv


def _attn_fwd(q, k, v, dm, name):
    T = dm.T
    sp = _attn_specs(dm)
    nq = sp["nq"]

    def body(q_ref, kl_ref, kc_ref, vl_ref, vc_ref, o_ref):
        is_ctx = pl.program_id(2) == nq
        pl_, pc = _attn_probs(q_ref[...], kl_ref[...], kc_ref[...], is_ctx)
        o_ref[...] = (_dot(pl_.astype(BF16), vl_ref[...]) + _dot(pc.astype(BF16), vc_ref[...])).astype(BF16)

    return pl.pallas_call(
        body, name=name, grid=(dm.B, HEADS, nq + 1),
        in_specs=[sp["q"], sp["k_lat"], sp["k_ctx"], sp["v_lat"], sp["v_ctx"]], out_specs=sp["o"],
        out_shape=jax.ShapeDtypeStruct((T, HEADS * V_HEAD), BF16),
        compiler_params=_cparams("parallel", "parallel", "arbitrary"),
    )(q, k, k, v, v)


def _attn_bwd(q, k, v, o, do, dm, name):
    T = dm.T
    sp = _attn_specs(dm)
    nq, tq = sp["nq"], sp["tq"]
    HP, HQ, HV = HEAD_PAD, HEADS * HEAD_PAD, HEADS * V_HEAD

    def body(q_ref, kl_ref, kc_ref, vl_ref, vc_ref, o_ref, do_ref, dq_ref, dkl_ref, dkc_ref, dvl_ref, dvc_ref):
        i = pl.program_id(2)

        @pl.when(i == 0)
        def _():
            dkl_ref[...] = jnp.zeros_like(dkl_ref)
            dkc_ref[...] = jnp.zeros_like(dkc_ref)
            dvl_ref[...] = jnp.zeros_like(dvl_ref)
            dvc_ref[...] = jnp.zeros_like(dvc_ref)

        qv = q_ref[...]
        pl_, pc = _attn_probs(qv, kl_ref[...], kc_ref[...], i == nq)
        dov = do_ref[...]
        dob = dov.astype(BF16)
        delta = jnp.sum(dov * o_ref[...].astype(F32), axis=-1, keepdims=True)
        dsl = (pl_ * (_dot_nt(dob, vl_ref[...]) - delta) * QK_SCALE).astype(BF16)
        dsc = (pc * (_dot_nt(dob, vc_ref[...]) - delta) * QK_SCALE).astype(BF16)
        dq_ref[...] = _dot(dsl, kl_ref[...]) + _dot(dsc, kc_ref[...])
        dkl_ref[...] += _dot_tn(dsl, qv)
        dkc_ref[...] += _dot_tn(dsc, qv)
        dvl_ref[...] += _dot_tn(pl_.astype(BF16), dob)
        dvc_ref[...] += _dot_tn(pc.astype(BF16), dob)

    return pl.pallas_call(
        body, name=name, grid=(dm.B, HEADS, nq + 1),
        in_specs=[sp["q"], sp["k_lat"], sp["k_ctx"], sp["v_lat"], sp["v_ctx"], sp["o"], sp["o"]],
        out_specs=[sp["q"], sp["k_lat"], pl.BlockSpec((tq, HP), lambda b, hd, i: (b, hd)),
                   sp["v_lat"], pl.BlockSpec((tq, V_HEAD), lambda b, hd, i: (b, hd))],
        out_shape=[jax.ShapeDtypeStruct((T, HQ), F32), jax.ShapeDtypeStruct((dm.B * dm.N, HQ), F32),
                   jax.ShapeDtypeStruct((dm.B * dm.CTX, HQ), F32), jax.ShapeDtypeStruct((dm.B * dm.N, HV), F32),
                   jax.ShapeDtypeStruct((dm.B * dm.CTX, HV), F32)],
        compiler_params=_cparams("parallel", "parallel", "arbitrary"),
    )(q, k, k, v, v, o, do)


def _loss_grad(h, target, dm, name):
    T, D = h.shape
    tm = dm.tm
    nlat = dm.B * dm.N // tm

    def body(h_ref, t_ref, dh_ref, ls_ref):
        lat = (pl.program_id(0) < nlat).astype(F32)
        diff = (h_ref[...] - t_ref[...]) * lat
        dh_ref[...] = diff * (1.0 / D)
        ls_ref[...] = jnp.zeros(ls_ref.shape, F32) + (0.5 / D) * jnp.sum(diff * diff)

    return pl.pallas_call(
        body, name=name, grid=(T // tm,),
        in_specs=[pl.BlockSpec((tm, D), lambda i: (i, 0)), pl.BlockSpec((tm, D), lambda i: (jnp.minimum(i, nlat - 1), 0))],
        out_specs=[pl.BlockSpec((tm, D), lambda i: (i, 0)), pl.BlockSpec((1, 8, 128), lambda i: (i, 0, 0))],
        out_shape=[jax.ShapeDtypeStruct((T, D), F32), jax.ShapeDtypeStruct((T // tm, 8, 128), F32)],
        compiler_params=_cparams("parallel"),
    )(h, target)


def _col_block(cols, target=1152):
    return max(t for t in range(128, min(cols, target) + 1, 128) if cols % t == 0)


def _mod_fwd(cond, w_mod, b_mod, name):
    L, D, C = w_mod.shape
    R = cond.shape[0]
    cb = _col_block(C)

    def body(c_ref, w_ref, b_ref, o_ref):
        cv = c_ref[...]
        sc = (cv * jax.nn.sigmoid(cv)).astype(BF16)
        o_ref[...] = _dot(sc, w_ref[...].astype(BF16)) + b_ref[...]

    return pl.pallas_call(
        body, name=name, grid=(L, C // cb),
        in_specs=[pl.BlockSpec((R, D), lambda l, c: (0, 0)), pl.BlockSpec((None, D, cb), lambda l, c: (l, 0, c)),
                  pl.BlockSpec((None, 1, cb), lambda l, c: (l, 0, c))],
        out_specs=pl.BlockSpec((None, R, cb), lambda l, c: (l, 0, c)),
        out_shape=jax.ShapeDtypeStruct((L, R, C), F32),
        compiler_params=_cparams("parallel", "parallel"),
    )(cond, w_mod, b_mod)


def _mod_bwd(cond, dmod, w_mod, name):
    L, D, C = w_mod.shape
    R = cond.shape[0]
    cb = _col_block(C)
    nc = C // cb

    def body(c_ref, dm_ref, w_ref, gw_ref, ds_ref):
        cv = c_ref[...]
        sc = (cv * jax.nn.sigmoid(cv)).astype(BF16)
        dmv = dm_ref[...].astype(BF16)
        gw_ref[...] = _dot_tn(sc, dmv)
        part = _dot_nt(dmv, w_ref[...].astype(BF16))

        @pl.when(pl.program_id(1) == 0)
        def _():
            ds_ref[...] = part

        @pl.when(pl.program_id(1) > 0)
        def _():
            ds_ref[...] += part

    return pl.pallas_call(
        body, name=name, grid=(L, nc),
        in_specs=[pl.BlockSpec((R, D), lambda l, c: (0, 0)), pl.BlockSpec((None, R, cb), lambda l, c: (l, 0, c)),
                  pl.BlockSpec((None, D, cb), lambda l, c: (l, 0, c))],
        out_specs=[pl.BlockSpec((None, D, cb), lambda l, c: (l, 0, c)), pl.BlockSpec((None, R, D), lambda l, c: (l, 0, 0))],
        out_shape=[jax.ShapeDtypeStruct((L, D, C), F32), jax.ShapeDtypeStruct((L, R, D), F32)],
        compiler_params=_cparams("parallel", "arbitrary"),
    )(cond, dmod, w_mod)


def _row_block(rows, cols, budget=1 << 20):
    best = None
    for t in range(16, rows + 1, 16):
        if rows % t == 0 and t * cols * 4 <= budget:
            best = t
    return best if best is not None else rows


def _sum_slots(r, name):
    S, R, C = r.shape
    tr = _row_block(R, C)

    def body(r_ref, o_ref):
        acc = r_ref[0].astype(F32)
        for s in range(1, S):
            acc = acc + r_ref[s].astype(F32)
        o_ref[...] = acc

    return pl.pallas_call(
        body, name=name, grid=(R // tr,),
        in_specs=[pl.BlockSpec((S, tr, C), lambda i: (0, i, 0))], out_specs=pl.BlockSpec((tr, C), lambda i: (i, 0)),
        out_shape=jax.ShapeDtypeStruct((R, C), F32), compiler_params=_cparams("parallel"),
    )(r)


def _adamw(w, gs, m, v, name):
    ng = len(gs)
    R, C = w.shape
    tr = _row_block(R, C)
    c1 = 1.0 / (1.0 - ADAM_B1 ** ADAM_STEP)
    c2 = 1.0 / (1.0 - ADAM_B2 ** ADAM_STEP)

    def body(w_ref, *refs):
        m_ref, v_ref, g_ref, d_ref, mo_ref, vo_ref = refs[ng:]
        g = refs[0][...]
        for g_more in refs[1:ng]:
            g = g + g_more[...]
        g_ref[...] = g
        mn = ADAM_B1 * m_ref[...] + (1.0 - ADAM_B1) * g
        vn = ADAM_B2 * v_ref[...] + (1.0 - ADAM_B2) * (g * g)
        mo_ref[...] = mn
        vo_ref[...] = vn
        d_ref[...] = -ADAM_LR * ((mn * c1) / (jnp.sqrt(vn * c2) + ADAM_EPS) + ADAM_WD * w_ref[...])

    blk = pl.BlockSpec((tr, C), lambda i: (i, 0))
    sd = jax.ShapeDtypeStruct((R, C), F32)
    return pl.pallas_call(
        body, name=name, grid=(R // tr,), in_specs=[blk] * (3 + ng), out_specs=[blk] * 4, out_shape=[sd] * 4,
        compiler_params=_cparams("parallel"),
    )(w, *gs, m, v)


def _rope_tables(dm):
    n = dm.N
    t = jnp.arange(n)
    r = (t // GRID_W).astype(F32)
    col = (t % GRID_W).astype(F32)
    nf = QK_ROPE // 4
    inv = ROPE_BASE ** (-jnp.arange(nf, dtype=F32) / nf)
    ang = jnp.stack([r[:, None] * inv, col[:, None] * inv], axis=1)
    cos, sin = jnp.cos(ang), jnp.sin(ang)
    zero = jnp.zeros_like(sin)
    c64 = jnp.stack([cos, cos], axis=2).reshape(n, QK_ROPE)
    s1 = jnp.stack([-sin, zero], axis=2).reshape(n, QK_ROPE)
    s2 = jnp.stack([zero, sin], axis=2).reshape(n, QK_ROPE)

    def pad(x, fill):
        return jnp.concatenate([jnp.full((n, QK_NOPE), fill, F32), x, jnp.full((n, HEAD_PAD - QK_HEAD), fill, F32)], axis=1)

    lat = jnp.stack([pad(c64, 1.0), pad(s1, 0.0), pad(s2, 0.0)])
    lat = jnp.tile(lat, (1, dm.B, 1))
    nctx = dm.B * dm.CTX
    ctx = jnp.stack([jnp.ones((nctx, HEAD_PAD), F32), jnp.zeros((nctx, HEAD_PAD), F32), jnp.zeros((nctx, HEAD_PAD), F32)])
    return jnp.concatenate([lat, ctx], axis=1)


def _fold_parts(part, dm):
    nblk = part.shape[0]
    nb = (dm.N * nblk) // dm.T
    groups = [part[b * nb:(b + 1) * nb].sum(axis=0) for b in range(dm.B)]
    groups.append(part[dm.B * nb:].sum(axis=0))
    return jnp.stack(groups)


def _flat2(a):
    return a.reshape(-1, a.shape[-1])


def kernel(x, c, ctx, c_ctx, w_mod, b_mod, g_norm, ffn_w1, ffn_w3, ffn_w2, sc_w_in, sc_conv, sc_w_out, mla_w_a, mla_g_qa, mla_w_uq, mla_g_kva, mla_w_ukv, mla_g_q, mla_g_k, mla_w_o, loss_target, m_c_ctx, m_w_mod, m_b_mod, m_g_norm, m_ffn_w1, m_ffn_w3, m_ffn_w2, m_sc_w_in, m_sc_conv, m_sc_w_out, m_mla_w_a, m_mla_g_qa, m_mla_w_uq, m_mla_g_kva, m_mla_w_ukv, m_mla_g_q, m_mla_g_k, m_mla_w_o, v_c_ctx, v_w_mod, v_b_mod, v_g_norm, v_ffn_w1, v_ffn_w3, v_ffn_w2, v_sc_w_in, v_sc_conv, v_sc_w_out, v_mla_w_a, v_mla_g_qa, v_mla_w_uq, v_mla_g_kva, v_mla_w_ukv, v_mla_g_q, v_mla_g_k, v_mla_w_o):
    B, N, D = x.shape
    CTX = ctx.shape[1]
    T = B * (N + CTX)
    tm = next(t for t in (512, 256, 128, 64, 32, 16) if N % t == 0 and (B * CTX) % t == 0)
    dm = Dims(B, N, CTX, D, T, tm)
    L = w_mod.shape[0]
    La, Lb = sc_w_in.shape[0], mla_w_a.shape[0]
    S = N_CHIPS
    ndev = 2 * S
    xi, yi, ci = lax.axis_index("x"), lax.axis_index("y"), lax.axis_index("c")
    chip = 2 * xi + yi
    dev = 2 * chip + ci
    weights = dict(c_ctx=c_ctx, w_mod=w_mod, b_mod=b_mod, g_norm=g_norm, ffn_w1=ffn_w1, ffn_w3=ffn_w3, ffn_w2=ffn_w2,
                   sc_w_in=sc_w_in, sc_conv=sc_conv, sc_w_out=sc_w_out, mla_w_a=mla_w_a, mla_g_qa=mla_g_qa,
                   mla_w_uq=mla_w_uq, mla_g_kva=mla_g_kva, mla_w_ukv=mla_w_ukv, mla_g_q=mla_g_q, mla_g_k=mla_g_k,
                   mla_w_o=mla_w_o)
    mom = dict(c_ctx=(m_c_ctx, v_c_ctx), w_mod=(m_w_mod, v_w_mod), b_mod=(m_b_mod, v_b_mod), g_norm=(m_g_norm, v_g_norm),
               ffn_w1=(m_ffn_w1, v_ffn_w1), ffn_w3=(m_ffn_w3, v_ffn_w3), ffn_w2=(m_ffn_w2, v_ffn_w2),
               sc_w_in=(m_sc_w_in, v_sc_w_in), sc_conv=(m_sc_conv, v_sc_conv), sc_w_out=(m_sc_w_out, v_sc_w_out),
               mla_w_a=(m_mla_w_a, v_mla_w_a), mla_g_qa=(m_mla_g_qa, v_mla_g_qa), mla_w_uq=(m_mla_w_uq, v_mla_w_uq),
               mla_g_kva=(m_mla_g_kva, v_mla_g_kva), mla_w_ukv=(m_mla_w_ukv, v_mla_w_ukv), mla_g_q=(m_mla_g_q, v_mla_g_q),
               mla_g_k=(m_mla_g_k, v_mla_g_k), mla_w_o=(m_mla_w_o, v_mla_w_o))

    big = ["ffn_w1", "ffn_w3", "ffn_w2", "sc_w_in", "sc_w_out", "mla_w_a", "mla_w_uq", "mla_w_ukv", "mla_w_o"]
    gathered = _exchange([weights[n].astype(BF16) for n in big] + [g_norm, sc_conv, mla_g_qa], ("x", "y"), False, "gather_weights")
    gw = dict(zip(big + ["g_norm", "sc_conv", "mla_g_qa"], gathered))
    w1g, w3g, w2g, wing = gw["ffn_w1"], gw["ffn_w3"], gw["ffn_w2"], gw["sc_w_in"]
    wout = jnp.moveaxis(gw["sc_w_out"], 0, 1).reshape(La, D, D)
    wo = jnp.moveaxis(gw["mla_w_o"], 0, 1).reshape(Lb, HEADS * V_HEAD, D)
    wa = jnp.pad(jnp.moveaxis(gw["mla_w_a"], 0, 1).reshape(Lb, D, -1), ((0, 0), (0, 0), (0, 512 - (Q_LORA + KV_LORA + QK_ROPE))))
    wuq = jnp.moveaxis(gw["mla_w_uq"], 0, 2).reshape(Lb, Q_LORA, HEADS, QK_HEAD)
    wuq = jnp.pad(wuq, ((0, 0), (0, 0), (0, 0), (0, HEAD_PAD - QK_HEAD))).reshape(Lb, Q_LORA, HEADS * HEAD_PAD)
    wukv = jnp.moveaxis(gw["mla_w_ukv"], 0, 2).reshape(Lb, KV_LORA, HEADS * HEAD_PAD)
    gnorm = jnp.moveaxis(gw["g_norm"], 0, 2).reshape(L, 3, D)
    convw = jnp.moveaxis(gw["sc_conv"], 0, 2).reshape(La, 3, D)
    gqa = jnp.moveaxis(gw["mla_g_qa"], 0, 1).reshape(Lb, Q_LORA)
    padl = lambda a: jnp.pad(a, ((0, 0), (0, HEAD_PAD - a.shape[1])))
    gains = jnp.stack([padl(gqa), padl(mla_g_kva), padl(mla_g_q), padl(mla_g_k)], axis=1)
    gains = jnp.pad(gains, ((0, 0), (0, 4), (0, 0)))

    R = -(-(ndev * B + 1) // 16) * 16
    call = _exchange([c], ("x", "y", "c"), False, "gather_cond")[0].reshape(ndev * B, D)
    cond = jnp.concatenate([call, c_ctx[None], jnp.zeros((R - ndev * B - 1, D), F32)], axis=0)
    C = w_mod.shape[-1]
    bm = lax.dynamic_slice_in_dim(b_mod, chip * C, C, axis=1)[:, None, :]
    mshard = _mod_fwd(cond, w_mod, bm, "mod_fwd")
    mfull = jnp.moveaxis(_exchange([mshard], ("x", "y"), False, "gather_mod")[0], 0, 2).reshape(L, R, S * C)
    mine = lax.dynamic_slice_in_dim(mfull, dev * B, B, axis=1)
    mod = jnp.concatenate([mine, mfull[:, ndev * B:ndev * B + 1]], axis=1).reshape(L, B + 1, 9, D)

    tabs = _rope_tables(dm)
    h = jnp.concatenate([x.reshape(B * N, D), ctx.reshape(B * CTX, D)], axis=0)

    saved = []
    for l in range(L):
        kind, j = l % 2, l // 2
        sv = {}
        sv["h0"] = h
        h, sv["a1"], sv["b1"], sv["hn1"], sv["y1"] = _ffn_fwd(h, mod[l], gnorm[l, 0:1], w1g, w3g, w2g, l, 0, dm, "ffn_fwd")
        sv["h1"] = h
        if kind == 0:
            sv["p"], sv["hnm"] = _sc_in_fwd(h, mod[l], gnorm[l, 1:2], wing, j, dm, "sc_in_fwd")
            sv["z"] = _conv_fwd(sv["p"], convw[j], dm, "conv_fwd")
            h, sv["ym"] = _out_fwd(sv["z"], wout, h, mod[l], j, dm, "sc_out_fwd")
        else:
            sv["hnm"], sv["q"], sv["k"], sv["v"] = _mla_proj_fwd(h, mod[l], gnorm[l, 1:2], gains, tabs, wa, wuq, wukv, j, dm, "mla_proj_fwd")
            sv["o"] = _attn_fwd(sv["q"], sv["k"], sv["v"], dm, "attn_fwd")
            h, sv["ym"] = _out_fwd(sv["o"], wo, h, mod[l], j, dm, "mla_out_fwd")
        sv["h2"] = h
        h, sv["a2"], sv["b2"], sv["hn2"], sv["y2"] = _ffn_fwd(h, mod[l], gnorm[l, 2:3], w1g, w3g, w2g, l, 1, dm, "ffn_fwd")
        saved.append(sv)

    dh, lsum = _loss_grad(h, loss_target.reshape(B * N, D), dm, "loss_grad")
    loss = lax.psum(jnp.sum(lsum[:, 0, 0]), ("x", "y", "c"))

    F = w1g.shape[-1]
    wq = D // S
    G = dict(ffn_w1=jnp.zeros((S, L, 2, D, F), BF16), ffn_w3=jnp.zeros((S, L, 2, D, F), BF16), ffn_w2=jnp.zeros((S, L, 2, F, D), BF16),
             sc_w_in=jnp.zeros((La, 3 * S, D, wq), BF16), sc_w_out=jnp.zeros((La, D, D), BF16), mla_w_o=jnp.zeros((Lb, HEADS * V_HEAD, D), BF16))
    dmod = [None] * L
    dgn = [None] * L
    dconv = [None] * La
    dgains = [None] * Lb
    gmla = dict(mla_w_a=[None] * Lb, mla_w_uq=[None] * Lb, mla_w_ukv=[None] * Lb)
    tk = tm
    nk = T // tk
    full_a = pl.BlockSpec((tk, D), lambda s, kk: (kk, 0))
    shard_b = pl.BlockSpec((None, tk, F), lambda s, kk: (s, kk, 0))

    def ffn_back(dh, sv, l, k):
        sfx = "1" if k == 0 else "2"
        dh, da, db, sw, dy, part = _ffn_bwd(dh, sv["h0" if k == 0 else "h2"], mod[l], gnorm[l, 2 * k:2 * k + 1], sv["y" + sfx],
                                            sv["a" + sfx], sv["b" + sfx], w1g, w3g, w2g, l, k, dm, "ffn_bwd")
        o5 = lambda s, kk: (s, l, k, 0, 0)
        G["ffn_w1"] = _mm_tn(sv["hn" + sfx], da, full_a, shard_b, G["ffn_w1"], pl.BlockSpec((None, None, None, D, F), o5), (S, nk), "gw1")
        G["ffn_w3"] = _mm_tn(sv["hn" + sfx], db, full_a, shard_b, G["ffn_w3"], pl.BlockSpec((None, None, None, D, F), o5), (S, nk), "gw3")
        G["ffn_w2"] = _mm_tn(sw, dy, shard_b, full_a, G["ffn_w2"], pl.BlockSpec((None, None, None, F, D), o5), (S, nk), "gw2")
        return dh, _fold_parts(part, dm)

    one = (1, nk)
    a1 = lambda kdim: pl.BlockSpec((tk, kdim), lambda s, kk: (kk, 0))
    for l in reversed(range(L)):
        kind, j = l % 2, l // 2
        sv = saved[l]
        dh, p2 = ffn_back(dh, sv, l, 1)
        if kind == 0:
            dy, dz, pg = _out_bwd(dh, sv["ym"], wout, mod[l], j, dm, "sc_out_bwd")
            G["sc_w_out"] = _mm_tn(sv["z"], dy, a1(D), a1(D), G["sc_w_out"], pl.BlockSpec((None, D, D), lambda s, kk: (j, 0, 0)), one, "gw_sc_out")
            dp, dconv[j] = _conv_bwd(dz, sv["p"], convw[j], dm, "conv_bwd")
            G["sc_w_in"] = _mm_tn(sv["hnm"], dp, pl.BlockSpec((tk, D), lambda q, kk: (kk, 0)),
                                  pl.BlockSpec((None, tk, wq), lambda q, kk: (q // S, kk, q % S)), G["sc_w_in"],
                                  pl.BlockSpec((None, None, D, wq), lambda q, kk: (j, q, 0, 0)), (3 * S, nk), "gw_sc_in")
            dh, pm = _sc_in_bwd(dh, dp, sv["h1"], mod[l], gnorm[l, 1:2], wing, j, dm, "sc_in_bwd")
        else:
            dy, do, pg = _out_bwd(dh, sv["ym"], wo, mod[l], j, dm, "mla_out_bwd")
            G["mla_w_o"] = _mm_tn(sv["o"], dy, a1(HEADS * V_HEAD), a1(D), G["mla_w_o"], pl.BlockSpec((None, HEADS * V_HEAD, D), lambda s, kk: (j, 0, 0)), one, "gw_mla_o")
            dq, dkl, dkc, dvl, dvc = _attn_bwd(sv["q"], sv["k"], sv["v"], sv["o"], do, dm, "attn_bwd")
            dk = jnp.concatenate([dkl, dkc], axis=0)
            dv = jnp.concatenate([dvl, dvc], axis=0)
            dh, pm, gmla["mla_w_a"][j], gmla["mla_w_uq"][j], gmla["mla_w_ukv"][j], dgains[j] = _mla_proj_bwd(
                dh, dq, dk, dv, sv["h1"], mod[l], gnorm[l, 1:2], gains, tabs, wa, wuq, wukv, j, dm, "mla_proj_bwd")
        pm = _fold_parts(pm, dm) + _fold_parts(pg, dm)
        dh, p0 = ffn_back(dh, sv, l, 0)
        dmod[l] = jnp.concatenate([p0[:, 0:3], pm[:, 0:3], p2[:, 0:3]], axis=1).reshape(B + 1, 9 * D)
        dgn[l] = jnp.stack([p0[:, 3].sum(0), pm[:, 3].sum(0), p2[:, 3].sum(0)])
    grad_x = dh[:B * N].reshape(B, N, D)

    Gs = dict(ffn_w1=G["ffn_w1"], ffn_w3=G["ffn_w3"], ffn_w2=G["ffn_w2"])
    Gs["sc_w_in"] = jnp.moveaxis(G["sc_w_in"].reshape(La, S, 3, D, wq), (1, 2), (0, 3)).reshape(S, La, D, 3 * wq)
    Gs["sc_w_out"] = jnp.moveaxis(G["sc_w_out"].reshape(La, S, D // S, D), 1, 0)
    Gs["mla_w_o"] = jnp.moveaxis(G["mla_w_o"].reshape(Lb, S, HEADS * V_HEAD // S, D), 1, 0)
    ga = jnp.stack(gmla["mla_w_a"])[:, :, :Q_LORA + KV_LORA + QK_ROPE]
    Gs["mla_w_a"] = jnp.moveaxis(ga.reshape(Lb, S, D // S, -1), 1, 0).astype(BF16)
    guq = jnp.stack(gmla["mla_w_uq"]).reshape(Lb, Q_LORA, HEADS, HEAD_PAD)[..., :QK_HEAD]
    Gs["mla_w_uq"] = jnp.moveaxis(guq.reshape(Lb, Q_LORA, S, -1), 2, 0).astype(BF16)
    Gs["mla_w_ukv"] = jnp.moveaxis(jnp.stack(gmla["mla_w_ukv"]).reshape(Lb, KV_LORA, S, -1), 2, 0).astype(BF16)

    recv = _exchange([Gs[n] for n in big], ("x", "y"), True, "scatter_grads")
    sums = [_sum_slots(r.reshape(S, -1, r.shape[-1]), "sum_slots") for r in recv]
    both = _exchange(sums, ("c",), False, "swap_cores")
    gpair = {n: (bt[0], bt[1]) for n, bt in zip(big, both)}

    dgains_a = jnp.stack(dgains)
    small = [jnp.stack(dmod).reshape(-1), jnp.stack(dgn).reshape(-1), jnp.stack(dconv).reshape(-1), dgains_a.reshape(-1)]
    sizes = [s_.shape[0] for s_ in small]
    flat = jnp.concatenate(small)
    pad = (-flat.shape[0]) % 1024
    flat = jnp.pad(flat, (0, pad)).reshape(-1, 128)
    allsmall = _exchange([flat], ("x", "y", "c"), False, "gather_small")[0].reshape(ndev, -1)
    offs = [0]
    for s_ in sizes:
        offs.append(offs[-1] + s_)
    dmod_all = allsmall[:, offs[0]:offs[1]].reshape(ndev, L, B + 1, 9 * D)
    tot = allsmall[:, offs[1]:offs[4]].sum(axis=0)
    g_gnorm = tot[:offs[2] - offs[1]].reshape(L, 3, D)
    g_conv = tot[offs[2] - offs[1]:offs[3] - offs[1]].reshape(La, 3, D)
    g_gains = tot[offs[3] - offs[1]:].reshape(Lb, 8, HEAD_PAD)
    dM = jnp.concatenate([jnp.moveaxis(dmod_all[:, :, :B], 0, 1).reshape(L, ndev * B, 9 * D),
                          dmod_all[:, :, B].sum(axis=0)[:, None, :], jnp.zeros((L, R - ndev * B - 1, 9 * D), F32)], axis=1)
    g_bmod = dM.sum(axis=1)
    dM_mine = lax.dynamic_slice_in_dim(dM, chip * C, C, axis=2)
    g_wmod, dsil = _mod_bwd(cond, dM_mine, w_mod, "mod_bwd")
    dsil_ctx = dsil[:, ndev * B].sum(axis=0)
    dsil_all = _exchange([jnp.pad(dsil_ctx.reshape(-1, 128), ((0, (-(D // 128)) % 8), (0, 0)))], ("x", "y"), False, "gather_dctx")[0]
    dsil_tot = dsil_all.sum(axis=0)[:D // 128].reshape(D)
    sg = jax.nn.sigmoid(c_ctx)
    g_cctx = dsil_tot * (sg * (1.0 + c_ctx * (1.0 - sg)))

    chip_cols = lambda a, width: lax.dynamic_slice_in_dim(a, chip * width, width, axis=a.ndim - 1)
    small_grads = dict(
        c_ctx=g_cctx, b_mod=g_bmod, g_norm=chip_cols(g_gnorm, D // S), sc_conv=chip_cols(g_conv, D // S),
        mla_g_qa=chip_cols(g_gains[:, 0, :Q_LORA], Q_LORA // S), mla_g_kva=g_gains[:, 1, :KV_LORA],
        mla_g_q=g_gains[:, 2, :QK_HEAD], mla_g_k=g_gains[:, 3, :QK_HEAD])

    grads, deltas, new_m, new_v = {}, {}, {}, {}
    for n, w in weights.items():
        shape = w.shape
        w2 = _flat2(w) if w.ndim > 1 else w.reshape(1, -1)
        m2, v2 = (a.reshape(w2.shape) for a in mom[n])
        if n in gpair:
            gs = [a.reshape(w2.shape) for a in gpair[n]]
        elif n == "w_mod":
            gs = [_flat2(g_wmod)]
        else:
            gs = [small_grads[n].reshape(w2.shape)]
        g_, d_, m_, v_ = _adamw(w2, gs, m2, v2, "adamw")
        grads[n], deltas[n], new_m[n], new_v[n] = (a.reshape(shape) for a in (g_, d_, m_, v_))

    names = list(weights)
    return (loss, grad_x, *[grads[n] for n in names], *[deltas[n] for n in names], *[new_m[n] for n in names],
            *[new_v[n] for n in names])
```

```python
import functools
import math
from typing import NamedTuple

import jax
import jax.numpy as jnp
from jax import lax
from jax.experimental import pallas as pl
from jax.experimental.pallas import tpu as pltpu

F32 = jnp.float32
BF16 = jnp.bfloat16
EPS = 1e-6
GRID_W = 64
HEADS = 8
QK_NOPE = 128
QK_ROPE = 64
QK_HEAD = QK_NOPE + QK_ROPE
HEAD_PAD = 256
V_HEAD = 128
Q_LORA = 256
KV_LORA = 128
ROPE_BASE = 10000.0
QK_SCALE = QK_HEAD ** -0.5
ADAM_LR, ADAM_B1, ADAM_B2, ADAM_EPS, ADAM_WD, ADAM_STEP = 0.001, 0.9, 0.999, 1e-08, 0.01, 10
N_CHIPS = 4
VMEM_LIMIT = 56 * 1024 * 1024
MESH = pl.DeviceIdType.MESH
NEG = -1e30


class Dims(NamedTuple):
    B: int
    N: int
    CTX: int
    D: int
    T: int
    tm: int


def _cparams(*sem):
    return pltpu.CompilerParams(dimension_semantics=sem if sem else None, vmem_limit_bytes=VMEM_LIMIT)


def _dot(a, b):
    return jnp.dot(a, b, preferred_element_type=F32)


def _dot_nt(a, b):
    return lax.dot_general(a, b, (((1,), (1,)), ((), ())), preferred_element_type=F32)


def _dot_tn(a, b):
    return lax.dot_general(a, b, (((0,), (0,)), ((), ())), preferred_element_type=F32)


def _rms(x, n):
    r = lax.rsqrt(jnp.sum(x * x, axis=-1, keepdims=True) * (1.0 / n) + EPS)
    return x * r, r


def _rms_bwd(dxh, xh, r, n):
    return r * (dxh - xh * (jnp.sum(dxh * xh, axis=-1, keepdims=True) * (1.0 / n)))


def _pre(h, g, shift, scale):
    xh, _ = _rms(h, h.shape[-1])
    return (xh * g) * (1.0 + scale) + shift


def _pre_bwd(dout, h, g, scale):
    d = h.shape[-1]
    xh, r = _rms(h, d)
    n = xh * g
    dshift = jnp.sum(dout, axis=0, keepdims=True)
    dscale = jnp.sum(dout * n, axis=0, keepdims=True)
    dn = dout * (1.0 + scale)
    dg = jnp.sum(dn * xh, axis=0, keepdims=True)
    dh = _rms_bwd(dn * g, xh, r, d)
    return dh, dshift, dscale, dg


def _write_part(part_ref, dshift=None, dscale=None, dgate=None, dg=None):
    z = jnp.zeros((1, part_ref.shape[-1]), F32)
    part_ref[0, 0:1, :] = z if dshift is None else dshift
    part_ref[0, 1:2, :] = z if dscale is None else dscale
    part_ref[0, 2:3, :] = z if dgate is None else dgate
    part_ref[0, 3:4, :] = z if dg is None else dg
    part_ref[0, 4:8, :] = jnp.zeros((4, part_ref.shape[-1]), F32)


def _grp(dm):
    nb = dm.N // dm.tm
    return lambda i: jnp.minimum(i // nb, dm.B)


def _n_chunks(rows, row_bytes):
    n = 16
    while n > 1 and (rows % (16 * n) or (rows // n) * row_bytes < (256 << 10)):
        n //= 2
    return n


def _exchange(arrs, axes, scatter, name):
    n = len(arrs)
    nbits = len(axes)
    slots = 2 ** nbits
    pats = list(range(1, slots))
    nck = [_n_chunks(a.shape[-2], a.shape[-1] * a.dtype.itemsize) for a in arrs]
    base = [sum(nck[:i]) * len(pats) for i in range(n)]
    nsem = sum(nck) * len(pats)

    def body(*refs):
        ins, outs = refs[:n], refs[n:2 * n]
        send, recv, loc = refs[2 * n:]
        pos = {a: lax.axis_index(a) for a in ("x", "y", "c")}

        def slot_of(p):
            s = 0
            for a in axes:
                s = 2 * s + p[a]
            return s

        me = slot_of(pos)
        local = []
        for i in range(n):
            cp = pltpu.make_async_copy(ins[i].at[me] if scatter else ins[i], outs[i].at[me], loc.at[i])
            cp.start()
            local.append(cp)
        remote = []
        for pi, pat in enumerate(pats):
            peer = dict(pos)
            for bi, a in enumerate(axes):
                if (pat >> (nbits - 1 - bi)) & 1:
                    peer[a] = 1 - pos[a]
            them = slot_of(peer)
            for i in range(n):
                ch = arrs[i].shape[-2] // nck[i]
                for j in range(nck[i]):
                    k = base[i] + pi * nck[i] + j
                    rs = pl.ds(j * ch, ch)
                    cp = pltpu.make_async_remote_copy(
                        src_ref=ins[i].at[them, rs] if scatter else ins[i].at[rs], dst_ref=outs[i].at[me, rs],
                        send_sem=send.at[k], recv_sem=recv.at[k],
                        device_id=(peer["x"], peer["y"], peer["c"]), device_id_type=MESH)
                    cp.start()
                    remote.append(cp)
        for cp in local:
            cp.wait()
        for cp in remote:
            cp.wait()

    out_shape = [jax.ShapeDtypeStruct(a.shape if scatter else (slots,) + a.shape, a.dtype) for a in arrs]
    any_spec = pl.BlockSpec(memory_space=pl.ANY)
    outs = pl.pallas_call(
        body, name=name, out_shape=out_shape, in_specs=[any_spec] * n, out_specs=[any_spec] * n,
        scratch_shapes=[pltpu.SemaphoreType.DMA((nsem,)), pltpu.SemaphoreType.DMA((nsem,)), pltpu.SemaphoreType.DMA((n,))],
        compiler_params=pltpu.CompilerParams(has_side_effects=True),
    )(*arrs)
    return list(outs)


def _gather_two_level(arrs, name):
    n = len(arrs)
    halves = [a.shape[0] // 2 for a in arrs]
    nck = [_n_chunks(h, a.shape[1] * a.dtype.itemsize) for h, a in zip(halves, arrs)]
    base = [3 * sum(nck[:i]) for i in range(n)]
    nsem = 3 * sum(nck)

    def body(*refs):
        ins, outs = refs[:n], refs[n:2 * n]
        send1, recv1, send2, recv2, loc = refs[2 * n:]
        x, y, c = lax.axis_index("x"), lax.axis_index("y"), lax.axis_index("c")
        me = 2 * x + y
        local = []
        for i in range(n):
            cp = pltpu.make_async_copy(ins[i], outs[i].at[me], loc.at[i])
            cp.start()
            local.append(cp)
        chips = [(x, 1 - y), (1 - x, y), (1 - x, 1 - y)]

        def pieces():
            for pi, (px, py) in enumerate(chips):
                for i in range(n):
                    ch = halves[i] // nck[i]
                    for j in range(nck[i]):
                        yield base[i] + pi * nck[i] + j, px, py, 2 * px + py, i, j * ch, ch

        def rows(i, off, ch, core):
            return pl.ds(pl.multiple_of(core * halves[i] + off, 16), ch)

        first = []
        for k, px, py, them, i, off, ch in pieces():
            rs = rows(i, off, ch, c)
            cp = pltpu.make_async_remote_copy(src_ref=ins[i].at[rs], dst_ref=outs[i].at[me, rs], send_sem=send1.at[k],
                                              recv_sem=recv1.at[k], device_id=(px, py, c), device_id_type=MESH)
            cp.start()
            first.append(cp)
        passed = []
        for k, px, py, them, i, off, ch in pieces():
            rs = rows(i, off, ch, c)
            land = outs[i].at[them, rs]
            pltpu.make_async_remote_copy(src_ref=ins[i].at[rs], dst_ref=land, send_sem=send1.at[k], recv_sem=recv1.at[k],
                                         device_id=(px, py, c), device_id_type=MESH).wait_recv()
            cp = pltpu.make_async_remote_copy(src_ref=land, dst_ref=land, send_sem=send2.at[k], recv_sem=recv2.at[k],
                                              device_id=(x, y, 1 - c), device_id_type=MESH)
            cp.start()
            passed.append(cp)
        for cp in first + passed:
            cp.wait_send()
        for k, px, py, them, i, off, ch in pieces():
            theirs = outs[i].at[them, rows(i, off, ch, 1 - c)]
            pltpu.make_async_remote_copy(src_ref=theirs, dst_ref=theirs, send_sem=send2.at[k], recv_sem=recv2.at[k],
                                         device_id=(x, y, 1 - c), device_id_type=MESH).wait_recv()
        for cp in local:
            cp.wait()

    any_spec = pl.BlockSpec(memory_space=pl.ANY)
    outs = pl.pallas_call(
        body, name=name, out_shape=[jax.ShapeDtypeStruct((N_CHIPS,) + a.shape, a.dtype) for a in arrs],
        in_specs=[any_spec] * n, out_specs=[any_spec] * n,
        scratch_shapes=[pltpu.SemaphoreType.DMA((nsem,))] * 4 + [pltpu.SemaphoreType.DMA((n,))],
        compiler_params=pltpu.CompilerParams(has_side_effects=True),
    )(*arrs)
    return list(outs)


def _swap_halves(arrs, name):
    n = len(arrs)
    S = arrs[0].shape[0]
    halves = [a.shape[1] // 2 for a in arrs]
    nck = [_n_chunks(h, a.shape[2] * a.dtype.itemsize) for h, a in zip(halves, arrs)]
    base = [S * sum(nck[:i]) for i in range(n)]
    nsem = S * sum(nck)

    def body(*refs):
        ins, outs = refs[:n], refs[n:2 * n]
        send, recv = refs[2 * n:]
        x, y, c = lax.axis_index("x"), lax.axis_index("y"), lax.axis_index("c")
        copies = []
        for i in range(n):
            ch = halves[i] // nck[i]
            for s in range(S):
                for j in range(nck[i]):
                    k = base[i] + s * nck[i] + j
                    src = ins[i].at[s, pl.ds(pl.multiple_of((1 - c) * halves[i] + j * ch, 16), ch)]
                    cp = pltpu.make_async_remote_copy(src_ref=src, dst_ref=outs[i].at[s, pl.ds(j * ch, ch)], send_sem=send.at[k],
                                                      recv_sem=recv.at[k], device_id=(x, y, 1 - c), device_id_type=MESH)
                    cp.start()
                    copies.append(cp)
        for cp in copies:
            cp.wait()

    any_spec = pl.BlockSpec(memory_space=pl.ANY)
    outs = pl.pallas_call(
        body, name=name, out_shape=[jax.ShapeDtypeStruct((S, h, a.shape[2]), a.dtype) for h, a in zip(halves, arrs)],
        in_specs=[any_spec] * n, out_specs=[any_spec] * n,
        scratch_shapes=[pltpu.SemaphoreType.DMA((nsem,))] * 2,
        compiler_params=pltpu.CompilerParams(has_side_effects=True),
    )(*arrs)
    return list(outs)


def _pair_sum(g, r, core, name):
    S, rows, C = g.shape
    half = rows // 2
    tr = _row_block(half, C)
    nb = half // tr

    def body(core_ref, g_ref, r_ref, o_ref):
        del core_ref
        o_ref[...] = (g_ref[...].astype(F32) + r_ref[...].astype(F32)).astype(BF16)

    blk = pl.BlockSpec((None, tr, C), lambda s, i, cr: (s, i, 0))
    return pl.pallas_call(
        body, name=name,
        grid_spec=pltpu.PrefetchScalarGridSpec(
            num_scalar_prefetch=1, grid=(S, nb),
            in_specs=[pl.BlockSpec((None, tr, C), lambda s, i, cr: (s, cr[0] * nb + i, 0)), blk], out_specs=blk),
        out_shape=jax.ShapeDtypeStruct((S, half, C), BF16),
        compiler_params=_cparams("parallel", "parallel"),
    )(core.reshape(1).astype(jnp.int32), g, r)


def _ffn_fwd(h, mod, g, w1, w3, w2, l, k, dm, name):
    T, D = h.shape
    S, F = w1.shape[0], w1.shape[-1]
    tm = dm.tm
    r0 = 6 if k else 0
    grp = _grp(dm)

    def body(h_ref, mod_ref, g_ref, w1_ref, w3_ref, w2_ref, ho_ref, a_ref, b_ref, hn_ref, y_ref, hn_s, acc):
        s = pl.program_id(1)

        @pl.when(s == 0)
        def _():
            hn = _pre(h_ref[...], g_ref[...], mod_ref[0, r0:r0 + 1, :], mod_ref[0, r0 + 1:r0 + 2, :]).astype(BF16)
            hn_s[...] = hn
            hn_ref[...] = hn
            acc[...] = jnp.zeros_like(acc)

        hn = hn_s[...]
        a = _dot(hn, w1_ref[...])
        b = _dot(hn, w3_ref[...])
        a_ref[0] = a.astype(BF16)
        b_ref[0] = b.astype(BF16)
        sw = (a * jax.nn.sigmoid(a) * b).astype(BF16)
        acc[...] += _dot(sw, w2_ref[...])

        @pl.when(s == S - 1)
        def _():
            y = acc[...]
            y_ref[...] = y.astype(BF16)
            ho_ref[...] = h_ref[...] + 0.5 * mod_ref[0, r0 + 2:r0 + 3, :] * y

    row = pl.BlockSpec((tm, D), lambda i, s: (i, 0))
    wcol = pl.BlockSpec((None, None, None, D, F), lambda i, s: (s, l, k, 0, 0))
    wrow = pl.BlockSpec((None, None, None, F, D), lambda i, s: (s, l, k, 0, 0))
    ab = pl.BlockSpec((1, tm, F), lambda i, s: (s, i, 0))
    return pl.pallas_call(
        body, name=name, grid=(T // tm, S),
        in_specs=[row, pl.BlockSpec((1, 9, D), lambda i, s: (grp(i), 0, 0)), pl.BlockSpec((1, D), lambda i, s: (0, 0)),
                  wcol, wcol, wrow],
        out_specs=[row, ab, ab, row, row],
        out_shape=[jax.ShapeDtypeStruct((T, D), F32), jax.ShapeDtypeStruct((S, T, F), BF16),
                   jax.ShapeDtypeStruct((S, T, F), BF16), jax.ShapeDtypeStruct((T, D), BF16),
                   jax.ShapeDtypeStruct((T, D), BF16)],
        scratch_shapes=[pltpu.VMEM((tm, D), BF16), pltpu.VMEM((tm, D), F32)],
        compiler_params=_cparams("parallel", "arbitrary"),
    )(h, mod, g, w1, w3, w2)


def _ffn_bwd(dh, h, mod, g, y, a, b, w1, w3, w2, l, k, dm, name):
    T, D = h.shape
    S, F = w1.shape[0], w1.shape[-1]
    tm = dm.tm
    r0 = 6 if k else 0
    grp = _grp(dm)

    def body(dh_ref, h_ref, mod_ref, g_ref, y_ref, a_ref, b_ref, w1_ref, w3_ref, w2_ref,
             dho_ref, da_ref, db_ref, sw_ref, dy_ref, part_ref, dy_s, acc):
        s = pl.program_id(1)

        @pl.when(s == 0)
        def _():
            dy = (0.5 * mod_ref[0, r0 + 2:r0 + 3, :] * dh_ref[...]).astype(BF16)
            dy_s[...] = dy
            dy_ref[...] = dy
            acc[...] = jnp.zeros_like(acc)

        ds = _dot_nt(dy_s[...], w2_ref[...])
        av = a_ref[0].astype(F32)
        bv = b_ref[0].astype(F32)
        sig = jax.nn.sigmoid(av)
        sil = av * sig
        sw_ref[0] = (sil * bv).astype(BF16)
        db = (ds * sil).astype(BF16)
        da = (ds * bv * (sig * (1.0 + av * (1.0 - sig)))).astype(BF16)
        da_ref[0] = da
        db_ref[0] = db
        acc[...] += _dot_nt(da, w1_ref[...]) + _dot_nt(db, w3_ref[...])

        @pl.when(s == S - 1)
        def _():
            dhv = dh_ref[...]
            dhb, dshift, dscale, dg = _pre_bwd(acc[...], h_ref[...], g_ref[...], mod_ref[0, r0 + 1:r0 + 2, :])
            dho_ref[...] = dhv + dhb
            dgate = 0.5 * jnp.sum(dhv * y_ref[...].astype(F32), axis=0, keepdims=True)
            _write_part(part_ref, dshift, dscale, dgate, dg)

    row = pl.BlockSpec((tm, D), lambda i, s: (i, 0))
    wcol = pl.BlockSpec((None, None, None, D, F), lambda i, s: (s, l, k, 0, 0))
    wrow = pl.BlockSpec((None, None, None, F, D), lambda i, s: (s, l, k, 0, 0))
    ab = pl.BlockSpec((1, tm, F), lambda i, s: (s, i, 0))
    stf = jax.ShapeDtypeStruct((S, T, F), BF16)
    return pl.pallas_call(
        body, name=name, grid=(T // tm, S),
        in_specs=[row, row, pl.BlockSpec((1, 9, D), lambda i, s: (grp(i), 0, 0)), pl.BlockSpec((1, D), lambda i, s: (0, 0)),
                  row, ab, ab, wcol, wcol, wrow],
        out_specs=[row, ab, ab, ab, row, pl.BlockSpec((1, 8, D), lambda i, s: (i, 0, 0))],
        out_shape=[jax.ShapeDtypeStruct((T, D), F32), stf, stf, stf, jax.ShapeDtypeStruct((T, D), BF16),
                   jax.ShapeDtypeStruct((T // tm, 8, D), F32)],
        scratch_shapes=[pltpu.VMEM((tm, D), BF16), pltpu.VMEM((tm, D), F32)],
        compiler_params=_cparams("parallel", "arbitrary"),
    )(dh, h, mod, g, y, a, b, w1, w3, w2)


def _mm_tn(a, b, a_spec, b_spec, out, out_spec, grid, name):
    nk = grid[-1]
    kax = len(grid) - 1
    blk = tuple(d for d in out_spec.block_shape if d is not None)

    def body(a_ref, b_ref, o_in, o_ref, acc):
        del o_in
        kk = pl.program_id(kax)

        @pl.when(kk == 0)
        def _():
            acc[...] = jnp.zeros_like(acc)

        acc[...] += _dot_tn(a_ref[...].astype(BF16), b_ref[...].astype(BF16))

        @pl.when(kk == nk - 1)
        def _():
            o_ref[...] = acc[...].astype(o_ref.dtype)

    return pl.pallas_call(
        body, name=name, grid=grid,
        in_specs=[a_spec, b_spec, pl.BlockSpec(memory_space=pl.ANY)], out_specs=out_spec,
        out_shape=jax.ShapeDtypeStruct(out.shape, out.dtype),
        scratch_shapes=[pltpu.VMEM(blk, F32)], input_output_aliases={2: 0},
        compiler_params=_cparams(*(["parallel"] * kax + ["arbitrary"])),
    )(a, b, out)


def _sc_in_fwd(h, mod, g, w_in, j, dm, name):
    T, D = h.shape
    tm = dm.tm
    wq = D // N_CHIPS
    nq = 3 * N_CHIPS
    grp = _grp(dm)

    def body(h_ref, mod_ref, g_ref, w_ref, p_ref, hn_ref, hn_s):
        @pl.when(pl.program_id(1) == 0)
        def _():
            hn = _pre(h_ref[...], g_ref[...], mod_ref[0, 3:4, :], mod_ref[0, 4:5, :]).astype(BF16)
            hn_s[...] = hn
            hn_ref[...] = hn

        p_ref[...] = _dot(hn_s[...], w_ref[...])

    row = pl.BlockSpec((tm, D), lambda i, q: (i, 0))
    return pl.pallas_call(
        body, name=name, grid=(T // tm, nq),
        in_specs=[row, pl.BlockSpec((1, 9, D), lambda i, q: (grp(i), 0, 0)), pl.BlockSpec((1, D), lambda i, q: (0, 0)),
                  pl.BlockSpec((None, None, D, wq), lambda i, q: (q // 3, j, 0, q % 3))],
        out_specs=[pl.BlockSpec((None, tm, wq), lambda i, q: (q // N_CHIPS, i, q % N_CHIPS)), row],
        out_shape=[jax.ShapeDtypeStruct((3, T, D), F32), jax.ShapeDtypeStruct((T, D), BF16)],
        scratch_shapes=[pltpu.VMEM((tm, D), BF16)],
        compiler_params=_cparams("parallel", "arbitrary"),
    )(h, mod, g, w_in)


def _conv_cols(dm):
    return 256 if dm.D % 256 == 0 else 128


def _seg_masks(r, dm):
    bn = dm.B * dm.N
    lat = r < bn
    off = jnp.where(lat, lax.rem(r, dm.N), lax.rem(r - bn, dm.CTX))
    seg = jnp.where(lat, dm.N, dm.CTX)
    inside = (r >= 0) & (r < dm.T)
    return ((off != 0) & inside).astype(F32), ((off != seg - 1) & inside).astype(F32)


def _conv_specs(dm):
    tb, cb, nr8 = dm.tm, _conv_cols(dm), dm.T // 8
    prev8 = lambda c, i: jnp.maximum(i * (tb // 8) - 1, 0)
    next8 = lambda c, i: jnp.minimum((i + 1) * (tb // 8), nr8 - 1)
    return dict(
        tb=tb, cb=cb,
        p=pl.BlockSpec((3, tb, cb), lambda c, i: (0, i, c)),
        p_prev=pl.BlockSpec((3, 8, cb), lambda c, i: (0, prev8(c, i), c)),
        p_next=pl.BlockSpec((3, 8, cb), lambda c, i: (0, next8(c, i), c)),
        row=pl.BlockSpec((tb, cb), lambda c, i: (i, c)),
        row_prev=pl.BlockSpec((8, cb), lambda c, i: (prev8(c, i), c)),
        row_next=pl.BlockSpec((8, cb), lambda c, i: (next8(c, i), c)),
        w=pl.BlockSpec((3, cb), lambda c, i: (0, c)),
    )


def _shift_rows(x, before, after, tb):
    rid = lax.broadcasted_iota(jnp.int32, x.shape, 0)
    down = jnp.where(rid == 0, before, pltpu.roll(x, 1, 0))
    up = jnp.where(rid == tb - 1, after, pltpu.roll(x, tb - 1, 0))
    return down, up


def _conv_fwd(p, wc, dm, name):
    T, D = dm.T, dm.D
    sp = _conv_specs(dm)
    tb, cb = sp["tb"], sp["cb"]

    def body(p_ref, pp_ref, pn_ref, w_ref, z_ref):
        r = pl.program_id(1) * tb + lax.broadcasted_iota(jnp.int32, (tb, cb), 0)
        mp, mn = _seg_masks(r, dm)
        cu = p_ref[1] * p_ref[2]
        prev, nxt = _shift_rows(cu, pp_ref[1, 7:8, :] * pp_ref[2, 7:8, :], pn_ref[1, 0:1, :] * pn_ref[2, 0:1, :], tb)
        conv = w_ref[0:1, :] * (prev * mp) + w_ref[1:2, :] * cu + w_ref[2:3, :] * (nxt * mn)
        z_ref[...] = (p_ref[0] * conv).astype(BF16)

    return pl.pallas_call(
        body, name=name, grid=(D // cb, T // tb),
        in_specs=[sp["p"], sp["p_prev"], sp["p_next"], sp["w"]], out_specs=sp["row"],
        out_shape=jax.ShapeDtypeStruct((T, D), BF16),
        compiler_params=_cparams("parallel", "parallel"),
    )(p, p, p, wc)


def _conv_bwd(dz, p, wc, dm, name):
    T, D = dm.T, dm.D
    sp = _conv_specs(dm)
    tb, cb = sp["tb"], sp["cb"]

    def body(dz_ref, dzp_ref, dzn_ref, p_ref, pp_ref, pn_ref, w_ref, dp_ref, dw_ref):
        i = pl.program_id(1)
        r = i * tb + lax.broadcasted_iota(jnp.int32, (tb, cb), 0)
        mp, mn = _seg_masks(r, dm)
        rb = i * tb + lax.broadcasted_iota(jnp.int32, (1, cb), 0)
        _, mn_before = _seg_masks(rb - 1, dm)
        mp_after, _ = _seg_masks(rb + tb, dm)
        bg, cg, u = p_ref[0], p_ref[1], p_ref[2]
        cu = cg * u
        prev, nxt = _shift_rows(cu, pp_ref[1, 7:8, :] * pp_ref[2, 7:8, :], pn_ref[1, 0:1, :] * pn_ref[2, 0:1, :], tb)
        prev = prev * mp
        nxt = nxt * mn
        w0, w1, w2 = w_ref[0:1, :], w_ref[1:2, :], w_ref[2:3, :]
        conv = w0 * prev + w1 * cu + w2 * nxt
        dz = dz_ref[...]
        dp_ref[0] = dz * conv
        dconv = dz * bg

        @pl.when(i == 0)
        def _():
            dw_ref[...] = jnp.zeros_like(dw_ref)

        dw_ref[0:1, :] += jnp.sum(dconv * prev, axis=0, keepdims=True)
        dw_ref[1:2, :] += jnp.sum(dconv * cu, axis=0, keepdims=True)
        dw_ref[2:3, :] += jnp.sum(dconv * nxt, axis=0, keepdims=True)
        dconv_before = dzp_ref[7:8, :] * pp_ref[0, 7:8, :] * mn_before
        dconv_after = dzn_ref[0:1, :] * pn_ref[0, 0:1, :] * mp_after
        from_prev, _ = _shift_rows(dconv * mn, dconv_before, dconv_after, tb)
        _, from_next = _shift_rows(dconv * mp, dconv_before, dconv_after, tb)
        dcu = w1 * dconv + w0 * from_next + w2 * from_prev
        dp_ref[1] = dcu * u
        dp_ref[2] = dcu * cg

    return pl.pallas_call(
        body, name=name, grid=(D // cb, T // tb),
        in_specs=[sp["row"], sp["row_prev"], sp["row_next"], sp["p"], sp["p_prev"], sp["p_next"], sp["w"]],
        out_specs=[sp["p"], sp["w"]],
        out_shape=[jax.ShapeDtypeStruct((3, T, D), F32), jax.ShapeDtypeStruct((3, D), F32)],
        compiler_params=_cparams("parallel", "arbitrary"),
    )(dz, dz, dz, p, p, p, wc)


def _out_fwd(z, w, h, mod, j, dm, name):
    T, D = h.shape
    K = z.shape[1]
    tm = dm.tm
    grp = _grp(dm)

    def body(z_ref, w_ref, h_ref, mod_ref, ho_ref, y_ref):
        y = _dot(z_ref[...], w_ref[...])
        y_ref[...] = y.astype(BF16)
        ho_ref[...] = h_ref[...] + mod_ref[0, 5:6, :] * y

    row = pl.BlockSpec((tm, D), lambda i: (i, 0))
    return pl.pallas_call(
        body, name=name, grid=(T // tm,),
        in_specs=[pl.BlockSpec((tm, K), lambda i: (i, 0)), pl.BlockSpec((None, K, D), lambda i: (j, 0, 0)), row,
                  pl.BlockSpec((1, 9, D), lambda i: (grp(i), 0, 0))],
        out_specs=[row, row],
        out_shape=[jax.ShapeDtypeStruct((T, D), F32), jax.ShapeDtypeStruct((T, D), BF16)],
        compiler_params=_cparams("parallel"),
    )(z, w, h, mod)


def _out_bwd(dh, y, w, mod, j, dm, name):
    T, D = dh.shape
    K = w.shape[1]
    tm = dm.tm
    grp = _grp(dm)

    def body(dh_ref, y_ref, w_ref, mod_ref, dy_ref, dz_ref, part_ref):
        dhv = dh_ref[...]
        dy = (mod_ref[0, 5:6, :] * dhv).astype(BF16)
        dy_ref[...] = dy
        dz_ref[...] = _dot_nt(dy, w_ref[...])
        _write_part(part_ref, dgate=jnp.sum(dhv * y_ref[...].astype(F32), axis=0, keepdims=True))

    row = pl.BlockSpec((tm, D), lambda i: (i, 0))
    return pl.pallas_call(
        body, name=name, grid=(T // tm,),
        in_specs=[row, row, pl.BlockSpec((None, K, D), lambda i: (j, 0, 0)), pl.BlockSpec((1, 9, D), lambda i: (grp(i), 0, 0))],
        out_specs=[row, pl.BlockSpec((tm, K), lambda i: (i, 0)), pl.BlockSpec((1, 8, D), lambda i: (i, 0, 0))],
        out_shape=[jax.ShapeDtypeStruct((T, D), BF16), jax.ShapeDtypeStruct((T, K), F32),
                   jax.ShapeDtypeStruct((T // tm, 8, D), F32)],
        compiler_params=_cparams("parallel"),
    )(dh, y, w, mod)


def _sc_in_bwd(dh, dp, h, mod, g, w_in, j, dm, name):
    T, D = h.shape
    tm = dm.tm
    wq = D // N_CHIPS
    nq = 3 * N_CHIPS
    grp = _grp(dm)

    def body(dh_ref, dp_ref, h_ref, mod_ref, g_ref, w_ref, dho_ref, part_ref, acc):
        q = pl.program_id(1)

        @pl.when(q == 0)
        def _():
            acc[...] = jnp.zeros_like(acc)

        acc[...] += _dot_nt(dp_ref[...].astype(BF16), w_ref[...])

        @pl.when(q == nq - 1)
        def _():
            dhb, dshift, dscale, dg = _pre_bwd(acc[...], h_ref[...], g_ref[...], mod_ref[0, 4:5, :])
            dho_ref[...] = dh_ref[...] + dhb
            _write_part(part_ref, dshift, dscale, None, dg)

    row = pl.BlockSpec((tm, D), lambda i, q: (i, 0))
    return pl.pallas_call(
        body, name=name, grid=(T // tm, nq),
        in_specs=[row, pl.BlockSpec((None, tm, wq), lambda i, q: (q // N_CHIPS, i, q % N_CHIPS)), row,
                  pl.BlockSpec((1, 9, D), lambda i, q: (grp(i), 0, 0)), pl.BlockSpec((1, D), lambda i, q: (0, 0)),
                  pl.BlockSpec((None, None, D, wq), lambda i, q: (q // 3, j, 0, q % 3))],
        out_specs=[row, pl.BlockSpec((1, 8, D), lambda i, q: (i, 0, 0))],
        out_shape=[jax.ShapeDtypeStruct((T, D), F32), jax.ShapeDtypeStruct((T // tm, 8, D), F32)],
        scratch_shapes=[pltpu.VMEM((tm, D), F32)],
        compiler_params=_cparams("parallel", "arbitrary"),
    )(dh, dp, h, mod, g, w_in)


def _rope(t, c, s1, s2):
    return t * c + pltpu.roll(t, HEAD_PAD - 16, 1) * s1 + pltpu.roll(t, 16, 1) * s2


def _rope_t(dy, c, s1, s2):
    return dy * c + pltpu.roll(dy * s1, 16, 1) + pltpu.roll(dy * s2, HEAD_PAD - 16, 1)


def _mla_heads_fwd(z, g_ref, wuq_ref, wukv_ref):
    cq, ckv, krp = z[:, :Q_LORA], z[:, Q_LORA:Q_LORA + KV_LORA], z[:, Q_LORA + KV_LORA:]
    cqh, rq = _rms(cq, Q_LORA)
    ckvh, rkv = _rms(ckv, KV_LORA)
    cqn = (cqh * g_ref[0:1, :]).astype(BF16)
    ckvn = (ckvh * g_ref[1:2, :KV_LORA]).astype(BF16)
    qraw = _dot(cqn, wuq_ref[...])
    kvraw = _dot(ckvn, wukv_ref[...])
    return dict(krp=krp, cqh=cqh, rq=rq, ckvh=ckvh, rkv=rkv, cqn=cqn, ckvn=ckvn, qraw=qraw, kvraw=kvraw)


def _mla_proj_fwd(h, mod, g, gains, tabs, w_a, w_uq, w_ukv, j, dm, name):
    T, D = h.shape
    tm = min(dm.tm, 256)
    grp = lambda i: jnp.minimum(i // (dm.N // tm), dm.B)
    HP = HEAD_PAD

    def body(h_ref, mod_ref, g_ref, gn_ref, tab_ref, wa_ref, wuq_ref, wukv_ref, hn_ref, q_ref, k_ref, v_ref):
        hn = _pre(h_ref[...], g_ref[...], mod_ref[0, 3:4, :], mod_ref[0, 4:5, :]).astype(BF16)
        hn_ref[...] = hn
        f = _mla_heads_fwd(_dot(hn, wa_ref[...]), gn_ref, wuq_ref, wukv_ref)
        c, s1, s2 = tab_ref[0], tab_ref[1], tab_ref[2]
        for hd in range(HEADS):
            qh, _ = _rms(f["qraw"][:, hd * HP:(hd + 1) * HP], QK_HEAD)
            q_ref[:, hd * HP:(hd + 1) * HP] = _rope(qh * gn_ref[2:3, :], c, s1, s2).astype(BF16)
            kpre = jnp.concatenate([f["kvraw"][:, hd * HP:hd * HP + QK_NOPE], f["krp"]], axis=1)
            kh, _ = _rms(kpre, QK_HEAD)
            k_ref[:, hd * HP:(hd + 1) * HP] = _rope(kh * gn_ref[3:4, :], c, s1, s2).astype(BF16)
            v_ref[:, hd * V_HEAD:(hd + 1) * V_HEAD] = f["kvraw"][:, hd * HP + QK_NOPE:(hd + 1) * HP].astype(BF16)

    row = pl.BlockSpec((tm, D), lambda i: (i, 0))
    HQ = HEADS * HP
    return pl.pallas_call(
        body, name=name, grid=(T // tm,),
        in_specs=[row, pl.BlockSpec((1, 9, D), lambda i: (grp(i), 0, 0)), pl.BlockSpec((1, D), lambda i: (0, 0)),
                  pl.BlockSpec((None, 8, HP), lambda i: (j, 0, 0)), pl.BlockSpec((3, tm, HP), lambda i: (0, i, 0)),
                  pl.BlockSpec((None, D, 512), lambda i: (j, 0, 0)), pl.BlockSpec((None, Q_LORA, HQ), lambda i: (j, 0, 0)),
                  pl.BlockSpec((None, KV_LORA, HQ), lambda i: (j, 0, 0))],
        out_specs=[row, pl.BlockSpec((tm, HQ), lambda i: (i, 0)), pl.BlockSpec((tm, HQ), lambda i: (i, 0)),
                   pl.BlockSpec((tm, HEADS * V_HEAD), lambda i: (i, 0))],
        out_shape=[jax.ShapeDtypeStruct((T, D), BF16), jax.ShapeDtypeStruct((T, HQ), BF16),
                   jax.ShapeDtypeStruct((T, HQ), BF16), jax.ShapeDtypeStruct((T, HEADS * V_HEAD), BF16)],
        compiler_params=_cparams("parallel"),
    )(h, mod, g, gains, tabs, w_a, w_uq, w_ukv)


def _mla_proj_bwd(dh, dq, dk, dv, h, mod, g, gains, tabs, w_a, w_uq, w_ukv, j, dm, name):
    T, D = h.shape
    tm = min(dm.tm, 256)
    nblk = T // tm
    grp = lambda i: jnp.minimum(i // (dm.N // tm), dm.B)
    HP = HEAD_PAD
    HQ = HEADS * HP

    def body(dh_ref, dq_ref, dk_ref, dv_ref, h_ref, mod_ref, g_ref, gn_ref, tab_ref, wa_ref, wuq_ref, wukv_ref,
             dho_ref, part_ref, gwa_ref, gwuq_ref, gwukv_ref, dgn_ref, dqraw_s, dkvraw_s):
        i = pl.program_id(0)

        @pl.when(i == 0)
        def _():
            gwa_ref[...] = jnp.zeros_like(gwa_ref)
            gwuq_ref[...] = jnp.zeros_like(gwuq_ref)
            gwukv_ref[...] = jnp.zeros_like(gwukv_ref)
            dgn_ref[...] = jnp.zeros_like(dgn_ref)

        hv = h_ref[...]
        hn = _pre(hv, g_ref[...], mod_ref[0, 3:4, :], mod_ref[0, 4:5, :]).astype(BF16)
        f = _mla_heads_fwd(_dot(hn, wa_ref[...]), gn_ref, wuq_ref, wukv_ref)
        c, s1, s2 = tab_ref[0], tab_ref[1], tab_ref[2]
        gq, gk = gn_ref[2:3, :], gn_ref[3:4, :]
        dgq = jnp.zeros((1, HP), F32)
        dgk = jnp.zeros((1, HP), F32)
        dkrp = jnp.zeros((tm, HP - QK_NOPE), F32)
        for hd in range(HEADS):
            qh, rq = _rms(f["qraw"][:, hd * HP:(hd + 1) * HP], QK_HEAD)
            dqn = _rope_t(dq_ref[:, hd * HP:(hd + 1) * HP], c, s1, s2)
            dgq = dgq + jnp.sum(dqn * qh, axis=0, keepdims=True)
            dqraw_s[:, hd * HP:(hd + 1) * HP] = _rms_bwd(dqn * gq, qh, rq, QK_HEAD)
            kpre = jnp.concatenate([f["kvraw"][:, hd * HP:hd * HP + QK_NOPE], f["krp"]], axis=1)
            kh, rk = _rms(kpre, QK_HEAD)
            dkn = _rope_t(dk_ref[:, hd * HP:(hd + 1) * HP], c, s1, s2)
            dgk = dgk + jnp.sum(dkn * kh, axis=0, keepdims=True)
            dkpre = _rms_bwd(dkn * gk, kh, rk, QK_HEAD)
            dkvraw_s[:, hd * HP:hd * HP + QK_NOPE] = dkpre[:, :QK_NOPE]
            dkrp = dkrp + dkpre[:, QK_NOPE:]
            dkvraw_s[:, hd * HP + QK_NOPE:(hd + 1) * HP] = dv_ref[:, hd * V_HEAD:(hd + 1) * V_HEAD]
        dqraw = dqraw_s[...].astype(BF16)
        dkvraw = dkvraw_s[...].astype(BF16)
        gwuq_ref[...] += _dot_tn(f["cqn"], dqraw)
        gwukv_ref[...] += _dot_tn(f["ckvn"], dkvraw)
        dcqn = _dot_nt(dqraw, wuq_ref[...])
        dckvn = _dot_nt(dkvraw, wukv_ref[...])
        dgqa = jnp.sum(dcqn * f["cqh"], axis=0, keepdims=True)
        dgkva = jnp.sum(dckvn * f["ckvh"], axis=0, keepdims=True)
        dcq = _rms_bwd(dcqn * gn_ref[0:1, :], f["cqh"], f["rq"], Q_LORA)
        dckv = _rms_bwd(dckvn * gn_ref[1:2, :KV_LORA], f["ckvh"], f["rkv"], KV_LORA)
        dz = jnp.concatenate([dcq, dckv, dkrp], axis=1).astype(BF16)
        gwa_ref[...] += _dot_tn(hn, dz)
        dhn = _dot_nt(dz, wa_ref[...])
        dhb, dshift, dscale, dg = _pre_bwd(dhn, hv, g_ref[...], mod_ref[0, 4:5, :])
        dho_ref[...] = dh_ref[...] + dhb
        _write_part(part_ref, dshift, dscale, None, dg)
        dgn_ref[0:1, :] += dgqa
        dgn_ref[1:2, :KV_LORA] += dgkva
        dgn_ref[2:3, :] += dgq
        dgn_ref[3:4, :] += dgk

    row = pl.BlockSpec((tm, D), lambda i: (i, 0))
    wide = pl.BlockSpec((tm, HQ), lambda i: (i, 0))
    const2 = lambda i: (0, 0)
    return pl.pallas_call(
        body, name=name, grid=(nblk,),
        in_specs=[row, wide, wide, pl.BlockSpec((tm, HEADS * V_HEAD), lambda i: (i, 0)), row,
                  pl.BlockSpec((1, 9, D), lambda i: (grp(i), 0, 0)), pl.BlockSpec((1, D), const2),
                  pl.BlockSpec((None, 8, HP), lambda i: (j, 0, 0)), pl.BlockSpec((3, tm, HP), lambda i: (0, i, 0)),
                  pl.BlockSpec((None, D, 512), lambda i: (j, 0, 0)), pl.BlockSpec((None, Q_LORA, HQ), lambda i: (j, 0, 0)),
                  pl.BlockSpec((None, KV_LORA, HQ), lambda i: (j, 0, 0))],
        out_specs=[row, pl.BlockSpec((1, 8, D), lambda i: (i, 0, 0)), pl.BlockSpec((D, 512), const2),
                   pl.BlockSpec((Q_LORA, HQ), const2), pl.BlockSpec((KV_LORA, HQ), const2), pl.BlockSpec((8, HP), const2)],
        out_shape=[jax.ShapeDtypeStruct((T, D), F32), jax.ShapeDtypeStruct((nblk, 8, D), F32),
                   jax.ShapeDtypeStruct((D, 512), F32), jax.ShapeDtypeStruct((Q_LORA, HQ), F32),
                   jax.ShapeDtypeStruct((KV_LORA, HQ), F32), jax.ShapeDtypeStruct((8, HP), F32)],
        scratch_shapes=[pltpu.VMEM((tm, HQ), F32), pltpu.VMEM((tm, HQ), F32)],
        compiler_params=_cparams("arbitrary"),
    )(dh, dq, dk, dv, h, mod, g, gains, tabs, w_a, w_uq, w_ukv)


def _attn_specs(dm):
    tq = dm.CTX
    nq = dm.N // tq
    cblk0 = dm.B * nq
    HP = HEAD_PAD
    qrow = lambda b, i: jnp.where(i < nq, b * nq + i, cblk0 + b)
    return dict(
        tq=tq, nq=nq,
        q=pl.BlockSpec((tq, HP), lambda b, hd, i: (qrow(b, i), hd)),
        k_lat=pl.BlockSpec((dm.N, HP), lambda b, hd, i: (b, hd)),
        k_ctx=pl.BlockSpec((tq, HP), lambda b, hd, i: (cblk0 + b, hd)),
        v_lat=pl.BlockSpec((dm.N, V_HEAD), lambda b, hd, i: (b, hd)),
        v_ctx=pl.BlockSpec((tq, V_HEAD), lambda b, hd, i: (cblk0 + b, hd)),
        o=pl.BlockSpec((tq, V_HEAD), lambda b, hd, i: (qrow(b, i), hd)),
    )


def _attn_probs(q, kl, kc, is_ctx):
    sl = _dot_nt(q, kl) * QK_SCALE
    sc = _dot_nt(q, kc) * QK_SCALE
    sl = sl + jnp.where(is_ctx, NEG, 0.0)
    m = jnp.maximum(jnp.max(sl, axis=-1, keepdims=True), jnp.max(sc, axis=-1, keepdims=True))
    pl_, pc = jnp.exp(sl - m), jnp.exp(sc - m)
    inv = 1.0 / (jnp.sum(pl_, axis=-1, keepdims=True) + jnp.sum(pc, axis=-1, keepdims=True))
    return pl_ * inv, pc * inv


def _attn_fwd(q, k, v, dm, name):
    T = dm.T
    sp = _attn_specs(dm)
    nq = sp["nq"]

    def body(q_ref, kl_ref, kc_ref, vl_ref, vc_ref, o_ref):
        is_ctx = pl.program_id(2) == nq
        pl_, pc = _attn_probs(q_ref[...], kl_ref[...], kc_ref[...], is_ctx)
        o_ref[...] = (_dot(pl_.astype(BF16), vl_ref[...]) + _dot(pc.astype(BF16), vc_ref[...])).astype(BF16)

    return pl.pallas_call(
        body, name=name, grid=(dm.B, HEADS, nq + 1),
        in_specs=[sp["q"], sp["k_lat"], sp["k_ctx"], sp["v_lat"], sp["v_ctx"]], out_specs=sp["o"],
        out_shape=jax.ShapeDtypeStruct((T, HEADS * V_HEAD), BF16),
        compiler_params=_cparams("parallel", "parallel", "arbitrary"),
    )(q, k, k, v, v)


def _attn_bwd(q, k, v, o, do, dm, name):
    T = dm.T
    sp = _attn_specs(dm)
    nq, tq = sp["nq"], sp["tq"]
    HP, HQ, HV = HEAD_PAD, HEADS * HEAD_PAD, HEADS * V_HEAD

    def body(q_ref, kl_ref, kc_ref, vl_ref, vc_ref, o_ref, do_ref, dq_ref, dkl_ref, dkc_ref, dvl_ref, dvc_ref):
        i = pl.program_id(2)

        @pl.when(i == 0)
        def _():
            dkl_ref[...] = jnp.zeros_like(dkl_ref)
            dkc_ref[...] = jnp.zeros_like(dkc_ref)
            dvl_ref[...] = jnp.zeros_like(dvl_ref)
            dvc_ref[...] = jnp.zeros_like(dvc_ref)

        qv = q_ref[...]
        pl_, pc = _attn_probs(qv, kl_ref[...], kc_ref[...], i == nq)
        dov = do_ref[...]
        dob = dov.astype(BF16)
        delta = jnp.sum(dov * o_ref[...].astype(F32), axis=-1, keepdims=True)
        dsl = (pl_ * (_dot_nt(dob, vl_ref[...]) - delta) * QK_SCALE).astype(BF16)
        dsc = (pc * (_dot_nt(dob, vc_ref[...]) - delta) * QK_SCALE).astype(BF16)
        dq_ref[...] = _dot(dsl, kl_ref[...]) + _dot(dsc, kc_ref[...])
        dkl_ref[...] += _dot_tn(dsl, qv)
        dkc_ref[...] += _dot_tn(dsc, qv)
        dvl_ref[...] += _dot_tn(pl_.astype(BF16), dob)
        dvc_ref[...] += _dot_tn(pc.astype(BF16), dob)

    return pl.pallas_call(
        body, name=name, grid=(dm.B, HEADS, nq + 1),
        in_specs=[sp["q"], sp["k_lat"], sp["k_ctx"], sp["v_lat"], sp["v_ctx"], sp["o"], sp["o"]],
        out_specs=[sp["q"], sp["k_lat"], pl.BlockSpec((tq, HP), lambda b, hd, i: (b, hd)),
                   sp["v_lat"], pl.BlockSpec((tq, V_HEAD), lambda b, hd, i: (b, hd))],
        out_shape=[jax.ShapeDtypeStruct((T, HQ), F32), jax.ShapeDtypeStruct((dm.B * dm.N, HQ), F32),
                   jax.ShapeDtypeStruct((dm.B * dm.CTX, HQ), F32), jax.ShapeDtypeStruct((dm.B * dm.N, HV), F32),
                   jax.ShapeDtypeStruct((dm.B * dm.CTX, HV), F32)],
        compiler_params=_cparams("parallel", "parallel", "arbitrary"),
    )(q, k, k, v, v, o, do)


def _loss_grad(h, target, dm, name):
    T, D = h.shape
    tm = dm.tm
    nlat = dm.B * dm.N // tm

    def body(h_ref, t_ref, dh_ref, ls_ref):
        lat = (pl.program_id(0) < nlat).astype(F32)
        diff = (h_ref[...] - t_ref[...]) * lat
        dh_ref[...] = diff * (1.0 / D)
        ls_ref[...] = jnp.zeros(ls_ref.shape, F32) + (0.5 / D) * jnp.sum(diff * diff)

    return pl.pallas_call(
        body, name=name, grid=(T // tm,),
        in_specs=[pl.BlockSpec((tm, D), lambda i: (i, 0)), pl.BlockSpec((tm, D), lambda i: (jnp.minimum(i, nlat - 1), 0))],
        out_specs=[pl.BlockSpec((tm, D), lambda i: (i, 0)), pl.BlockSpec((1, 8, 128), lambda i: (i, 0, 0))],
        out_shape=[jax.ShapeDtypeStruct((T, D), F32), jax.ShapeDtypeStruct((T // tm, 8, 128), F32)],
        compiler_params=_cparams("parallel"),
    )(h, target)


def _col_block(cols, target=1152):
    return max(t for t in range(128, min(cols, target) + 1, 128) if cols % t == 0)


def _mod_fwd(cond, w_mod, b_mod, name):
    L, D, C = w_mod.shape
    R = cond.shape[0]
    cb = _col_block(C)

    def body(c_ref, w_ref, b_ref, o_ref):
        cv = c_ref[...]
        sc = (cv * jax.nn.sigmoid(cv)).astype(BF16)
        o_ref[...] = _dot(sc, w_ref[...].astype(BF16)) + b_ref[...]

    return pl.pallas_call(
        body, name=name, grid=(L, C // cb),
        in_specs=[pl.BlockSpec((R, D), lambda l, c: (0, 0)), pl.BlockSpec((None, D, cb), lambda l, c: (l, 0, c)),
                  pl.BlockSpec((None, 1, cb), lambda l, c: (l, 0, c))],
        out_specs=pl.BlockSpec((None, R, cb), lambda l, c: (l, 0, c)),
        out_shape=jax.ShapeDtypeStruct((L, R, C), F32),
        compiler_params=_cparams("parallel", "parallel"),
    )(cond, w_mod, b_mod)


def _mod_bwd(cond, dmod, w_mod, name):
    L, D, C = w_mod.shape
    R = cond.shape[0]
    cb = _col_block(C)
    nc = C // cb

    def body(c_ref, dm_ref, w_ref, gw_ref, ds_ref):
        cv = c_ref[...]
        sc = (cv * jax.nn.sigmoid(cv)).astype(BF16)
        dmv = dm_ref[...].astype(BF16)
        gw_ref[...] = _dot_tn(sc, dmv)
        part = _dot_nt(dmv, w_ref[...].astype(BF16))

        @pl.when(pl.program_id(1) == 0)
        def _():
            ds_ref[...] = part

        @pl.when(pl.program_id(1) > 0)
        def _():
            ds_ref[...] += part

    return pl.pallas_call(
        body, name=name, grid=(L, nc),
        in_specs=[pl.BlockSpec((R, D), lambda l, c: (0, 0)), pl.BlockSpec((None, R, cb), lambda l, c: (l, 0, c)),
                  pl.BlockSpec((None, D, cb), lambda l, c: (l, 0, c))],
        out_specs=[pl.BlockSpec((None, D, cb), lambda l, c: (l, 0, c)), pl.BlockSpec((None, R, D), lambda l, c: (l, 0, 0))],
        out_shape=[jax.ShapeDtypeStruct((L, D, C), F32), jax.ShapeDtypeStruct((L, R, D), F32)],
        compiler_params=_cparams("parallel", "arbitrary"),
    )(cond, dmod, w_mod)


def _row_block(rows, cols, budget=1 << 20):
    best = None
    for t in range(16, rows + 1, 16):
        if rows % t == 0 and t * cols * 4 <= budget:
            best = t
    return best if best is not None else rows


def _sum_slots(r, name):
    S, R, C = r.shape
    tr = _row_block(R, C)

    def body(r_ref, o_ref):
        acc = r_ref[0].astype(F32)
        for s in range(1, S):
            acc = acc + r_ref[s].astype(F32)
        o_ref[...] = acc

    return pl.pallas_call(
        body, name=name, grid=(R // tr,),
        in_specs=[pl.BlockSpec((S, tr, C), lambda i: (0, i, 0))], out_specs=pl.BlockSpec((tr, C), lambda i: (i, 0)),
        out_shape=jax.ShapeDtypeStruct((R, C), F32), compiler_params=_cparams("parallel"),
    )(r)


def _adamw(w, gs, m, v, name):
    ng = len(gs)
    R, C = w.shape
    tr = _row_block(R, C)
    c1 = 1.0 / (1.0 - ADAM_B1 ** ADAM_STEP)
    c2 = 1.0 / (1.0 - ADAM_B2 ** ADAM_STEP)

    def body(w_ref, *refs):
        m_ref, v_ref, g_ref, d_ref, mo_ref, vo_ref = refs[ng:]
        g = refs[0][...]
        for g_more in refs[1:ng]:
            g = g + g_more[...]
        g_ref[...] = g
        mn = ADAM_B1 * m_ref[...] + (1.0 - ADAM_B1) * g
        vn = ADAM_B2 * v_ref[...] + (1.0 - ADAM_B2) * (g * g)
        mo_ref[...] = mn
        vo_ref[...] = vn
        d_ref[...] = -ADAM_LR * ((mn * c1) / (jnp.sqrt(vn * c2) + ADAM_EPS) + ADAM_WD * w_ref[...])

    blk = pl.BlockSpec((tr, C), lambda i: (i, 0))
    sd = jax.ShapeDtypeStruct((R, C), F32)
    return pl.pallas_call(
        body, name=name, grid=(R // tr,), in_specs=[blk] * (3 + ng), out_specs=[blk] * 4, out_shape=[sd] * 4,
        compiler_params=_cparams("parallel"),
    )(w, *gs, m, v)


def _rope_tables(dm):
    n = dm.N
    t = jnp.arange(n)
    r = (t // GRID_W).astype(F32)
    col = (t % GRID_W).astype(F32)
    nf = QK_ROPE // 4
    inv = ROPE_BASE ** (-jnp.arange(nf, dtype=F32) / nf)
    ang = jnp.stack([r[:, None] * inv, col[:, None] * inv], axis=1)
    cos, sin = jnp.cos(ang), jnp.sin(ang)
    zero = jnp.zeros_like(sin)
    c64 = jnp.stack([cos, cos], axis=2).reshape(n, QK_ROPE)
    s1 = jnp.stack([-sin, zero], axis=2).reshape(n, QK_ROPE)
    s2 = jnp.stack([zero, sin], axis=2).reshape(n, QK_ROPE)

    def pad(x, fill):
        return jnp.concatenate([jnp.full((n, QK_NOPE), fill, F32), x, jnp.full((n, HEAD_PAD - QK_HEAD), fill, F32)], axis=1)

    lat = jnp.stack([pad(c64, 1.0), pad(s1, 0.0), pad(s2, 0.0)])
    lat = jnp.tile(lat, (1, dm.B, 1))
    nctx = dm.B * dm.CTX
    ctx = jnp.stack([jnp.ones((nctx, HEAD_PAD), F32), jnp.zeros((nctx, HEAD_PAD), F32), jnp.zeros((nctx, HEAD_PAD), F32)])
    return jnp.concatenate([lat, ctx], axis=1)


def _fold_parts(part, dm):
    nblk = part.shape[0]
    nb = (dm.N * nblk) // dm.T
    groups = [part[b * nb:(b + 1) * nb].sum(axis=0) for b in range(dm.B)]
    groups.append(part[dm.B * nb:].sum(axis=0))
    return jnp.stack(groups)


def _flat2(a):
    return a.reshape(-1, a.shape[-1])


def kernel(x, c, ctx, c_ctx, w_mod, b_mod, g_norm, ffn_w1, ffn_w3, ffn_w2, sc_w_in, sc_conv, sc_w_out, mla_w_a, mla_g_qa, mla_w_uq, mla_g_kva, mla_w_ukv, mla_g_q, mla_g_k, mla_w_o, loss_target, m_c_ctx, m_w_mod, m_b_mod, m_g_norm, m_ffn_w1, m_ffn_w3, m_ffn_w2, m_sc_w_in, m_sc_conv, m_sc_w_out, m_mla_w_a, m_mla_g_qa, m_mla_w_uq, m_mla_g_kva, m_mla_w_ukv, m_mla_g_q, m_mla_g_k, m_mla_w_o, v_c_ctx, v_w_mod, v_b_mod, v_g_norm, v_ffn_w1, v_ffn_w3, v_ffn_w2, v_sc_w_in, v_sc_conv, v_sc_w_out, v_mla_w_a, v_mla_g_qa, v_mla_w_uq, v_mla_g_kva, v_mla_w_ukv, v_mla_g_q, v_mla_g_k, v_mla_w_o):
    B, N, D = x.shape
    CTX = ctx.shape[1]
    T = B * (N + CTX)
    tm = next(t for t in (512, 256, 128, 64, 32, 16) if N % t == 0 and (B * CTX) % t == 0)
    dm = Dims(B, N, CTX, D, T, tm)
    L = w_mod.shape[0]
    La, Lb = sc_w_in.shape[0], mla_w_a.shape[0]
    S = N_CHIPS
    ndev = 2 * S
    xi, yi, ci = lax.axis_index("x"), lax.axis_index("y"), lax.axis_index("c")
    chip = 2 * xi + yi
    dev = 2 * chip + ci
    weights = dict(c_ctx=c_ctx, w_mod=w_mod, b_mod=b_mod, g_norm=g_norm, ffn_w1=ffn_w1, ffn_w3=ffn_w3, ffn_w2=ffn_w2,
                   sc_w_in=sc_w_in, sc_conv=sc_conv, sc_w_out=sc_w_out, mla_w_a=mla_w_a, mla_g_qa=mla_g_qa,
                   mla_w_uq=mla_w_uq, mla_g_kva=mla_g_kva, mla_w_ukv=mla_w_ukv, mla_g_q=mla_g_q, mla_g_k=mla_g_k,
                   mla_w_o=mla_w_o)
    mom = dict(c_ctx=(m_c_ctx, v_c_ctx), w_mod=(m_w_mod, v_w_mod), b_mod=(m_b_mod, v_b_mod), g_norm=(m_g_norm, v_g_norm),
               ffn_w1=(m_ffn_w1, v_ffn_w1), ffn_w3=(m_ffn_w3, v_ffn_w3), ffn_w2=(m_ffn_w2, v_ffn_w2),
               sc_w_in=(m_sc_w_in, v_sc_w_in), sc_conv=(m_sc_conv, v_sc_conv), sc_w_out=(m_sc_w_out, v_sc_w_out),
               mla_w_a=(m_mla_w_a, v_mla_w_a), mla_g_qa=(m_mla_g_qa, v_mla_g_qa), mla_w_uq=(m_mla_w_uq, v_mla_w_uq),
               mla_g_kva=(m_mla_g_kva, v_mla_g_kva), mla_w_ukv=(m_mla_w_ukv, v_mla_w_ukv), mla_g_q=(m_mla_g_q, v_mla_g_q),
               mla_g_k=(m_mla_g_k, v_mla_g_k), mla_w_o=(m_mla_w_o, v_mla_w_o))

    big = ["ffn_w1", "ffn_w3", "ffn_w2", "sc_w_in", "sc_w_out", "mla_w_a", "mla_w_uq", "mla_w_ukv", "mla_w_o"]
    gathered = _gather_two_level([_flat2(weights[n]).astype(BF16) for n in big], "gather_weights")
    gw = {n: g.reshape((S,) + weights[n].shape) for n, g in zip(big, gathered)}
    vecs = ["g_norm", "sc_conv", "mla_g_qa"]
    gathered = _exchange([_flat2(weights[n]) for n in vecs], ("x", "y"), False, "gather_vectors")
    gw.update({n: g.reshape((S,) + weights[n].shape) for n, g in zip(vecs, gathered)})
    w1g, w3g, w2g, wing = gw["ffn_w1"], gw["ffn_w3"], gw["ffn_w2"], gw["sc_w_in"]
    wout = jnp.moveaxis(gw["sc_w_out"], 0, 1).reshape(La, D, D)
    wo = jnp.moveaxis(gw["mla_w_o"], 0, 1).reshape(Lb, HEADS * V_HEAD, D)
    wa = jnp.pad(jnp.moveaxis(gw["mla_w_a"], 0, 1).reshape(Lb, D, -1), ((0, 0), (0, 0), (0, 512 - (Q_LORA + KV_LORA + QK_ROPE))))
    wuq = jnp.moveaxis(gw["mla_w_uq"], 0, 2).reshape(Lb, Q_LORA, HEADS, QK_HEAD)
    wuq = jnp.pad(wuq, ((0, 0), (0, 0), (0, 0), (0, HEAD_PAD - QK_HEAD))).reshape(Lb, Q_LORA, HEADS * HEAD_PAD)
    wukv = jnp.moveaxis(gw["mla_w_ukv"], 0, 2).reshape(Lb, KV_LORA, HEADS * HEAD_PAD)
    gnorm = jnp.moveaxis(gw["g_norm"], 0, 2).reshape(L, 3, D)
    convw = jnp.moveaxis(gw["sc_conv"], 0, 2).reshape(La, 3, D)
    gqa = jnp.moveaxis(gw["mla_g_qa"], 0, 1).reshape(Lb, Q_LORA)
    padl = lambda a: jnp.pad(a, ((0, 0), (0, HEAD_PAD - a.shape[1])))
    gains = jnp.stack([padl(gqa), padl(mla_g_kva), padl(mla_g_q), padl(mla_g_k)], axis=1)
    gains = jnp.pad(gains, ((0, 0), (0, 4), (0, 0)))

    R = -(-(ndev * B + 1) // 16) * 16
    call = _exchange([c], ("x", "y", "c"), False, "gather_cond")[0].reshape(ndev * B, D)
    cond = jnp.concatenate([call, c_ctx[None], jnp.zeros((R - ndev * B - 1, D), F32)], axis=0)
    C = w_mod.shape[-1]
    bm = lax.dynamic_slice_in_dim(b_mod, chip * C, C, axis=1)[:, None, :]
    mshard = _mod_fwd(cond, w_mod, bm, "mod_fwd")
    mfull = _exchange([mshard.reshape(L * R, C)], ("x", "y"), False, "gather_mod")[0].reshape(S, L, R, C)
    mfull = jnp.moveaxis(mfull, 0, 2).reshape(L, R, S * C)
    mine = lax.dynamic_slice_in_dim(mfull, dev * B, B, axis=1)
    mod = jnp.concatenate([mine, mfull[:, ndev * B:ndev * B + 1]], axis=1).reshape(L, B + 1, 9, D)

    tabs = _rope_tables(dm)
    h = jnp.concatenate([x.reshape(B * N, D), ctx.reshape(B * CTX, D)], axis=0)

    saved = []
    for l in range(L):
        kind, j = l % 2, l // 2
        sv = {}
        sv["h0"] = h
        h, sv["a1"], sv["b1"], sv["hn1"], sv["y1"] = _ffn_fwd(h, mod[l], gnorm[l, 0:1], w1g, w3g, w2g, l, 0, dm, "ffn_fwd")
        sv["h1"] = h
        if kind == 0:
            sv["p"], sv["hnm"] = _sc_in_fwd(h, mod[l], gnorm[l, 1:2], wing, j, dm, "sc_in_fwd")
            sv["z"] = _conv_fwd(sv["p"], convw[j], dm, "conv_fwd")
            h, sv["ym"] = _out_fwd(sv["z"], wout, h, mod[l], j, dm, "sc_out_fwd")
        else:
            sv["hnm"], sv["q"], sv["k"], sv["v"] = _mla_proj_fwd(h, mod[l], gnorm[l, 1:2], gains, tabs, wa, wuq, wukv, j, dm, "mla_proj_fwd")
            sv["o"] = _attn_fwd(sv["q"], sv["k"], sv["v"], dm, "attn_fwd")
            h, sv["ym"] = _out_fwd(sv["o"], wo, h, mod[l], j, dm, "mla_out_fwd")
        sv["h2"] = h
        h, sv["a2"], sv["b2"], sv["hn2"], sv["y2"] = _ffn_fwd(h, mod[l], gnorm[l, 2:3], w1g, w3g, w2g, l, 1, dm, "ffn_fwd")
        saved.append(sv)

    dh, lsum = _loss_grad(h, loss_target.reshape(B * N, D), dm, "loss_grad")
    loss = lax.psum(jnp.sum(lsum[:, 0, 0]), ("x", "y", "c"))

    F = w1g.shape[-1]
    wq = D // S
    G = dict(ffn_w1=jnp.zeros((S, L, 2, D, F), BF16), ffn_w3=jnp.zeros((S, L, 2, D, F), BF16), ffn_w2=jnp.zeros((S, L, 2, F, D), BF16),
             sc_w_in=jnp.zeros((La, 3 * S, D, wq), BF16), sc_w_out=jnp.zeros((La, D, D), BF16), mla_w_o=jnp.zeros((Lb, HEADS * V_HEAD, D), BF16))
    dmod = [None] * L
    dgn = [None] * L
    dconv = [None] * La
    dgains = [None] * Lb
    gmla = dict(mla_w_a=[None] * Lb, mla_w_uq=[None] * Lb, mla_w_ukv=[None] * Lb)
    tk = tm
    nk = T // tk
    full_a = pl.BlockSpec((tk, D), lambda s, kk: (kk, 0))
    shard_b = pl.BlockSpec((None, tk, F), lambda s, kk: (s, kk, 0))

    def ffn_back(dh, sv, l, k):
        sfx = "1" if k == 0 else "2"
        dh, da, db, sw, dy, part = _ffn_bwd(dh, sv["h0" if k == 0 else "h2"], mod[l], gnorm[l, 2 * k:2 * k + 1], sv["y" + sfx],
                                            sv["a" + sfx], sv["b" + sfx], w1g, w3g, w2g, l, k, dm, "ffn_bwd")
        o5 = lambda s, kk: (s, l, k, 0, 0)
        G["ffn_w1"] = _mm_tn(sv["hn" + sfx], da, full_a, shard_b, G["ffn_w1"], pl.BlockSpec((None, None, None, D, F), o5), (S, nk), "gw1")
        G["ffn_w3"] = _mm_tn(sv["hn" + sfx], db, full_a, shard_b, G["ffn_w3"], pl.BlockSpec((None, None, None, D, F), o5), (S, nk), "gw3")
        G["ffn_w2"] = _mm_tn(sw, dy, shard_b, full_a, G["ffn_w2"], pl.BlockSpec((None, None, None, F, D), o5), (S, nk), "gw2")
        return dh, _fold_parts(part, dm)

    one = (1, nk)
    a1 = lambda kdim: pl.BlockSpec((tk, kdim), lambda s, kk: (kk, 0))
    for l in reversed(range(L)):
        kind, j = l % 2, l // 2
        sv = saved[l]
        dh, p2 = ffn_back(dh, sv, l, 1)
        if kind == 0:
            dy, dz, pg = _out_bwd(dh, sv["ym"], wout, mod[l], j, dm, "sc_out_bwd")
            G["sc_w_out"] = _mm_tn(sv["z"], dy, a1(D), a1(D), G["sc_w_out"], pl.BlockSpec((None, D, D), lambda s, kk: (j, 0, 0)), one, "gw_sc_out")
            dp, dconv[j] = _conv_bwd(dz, sv["p"], convw[j], dm, "conv_bwd")
            G["sc_w_in"] = _mm_tn(sv["hnm"], dp, pl.BlockSpec((tk, D), lambda q, kk: (kk, 0)),
                                  pl.BlockSpec((None, tk, wq), lambda q, kk: (q // S, kk, q % S)), G["sc_w_in"],
                                  pl.BlockSpec((None, None, D, wq), lambda q, kk: (j, q, 0, 0)), (3 * S, nk), "gw_sc_in")
            dh, pm = _sc_in_bwd(dh, dp, sv["h1"], mod[l], gnorm[l, 1:2], wing, j, dm, "sc_in_bwd")
        else:
            dy, do, pg = _out_bwd(dh, sv["ym"], wo, mod[l], j, dm, "mla_out_bwd")
            G["mla_w_o"] = _mm_tn(sv["o"], dy, a1(HEADS * V_HEAD), a1(D), G["mla_w_o"], pl.BlockSpec((None, HEADS * V_HEAD, D), lambda s, kk: (j, 0, 0)), one, "gw_mla_o")
            dq, dkl, dkc, dvl, dvc = _attn_bwd(sv["q"], sv["k"], sv["v"], sv["o"], do, dm, "attn_bwd")
            dk = jnp.concatenate([dkl, dkc], axis=0)
            dv = jnp.concatenate([dvl, dvc], axis=0)
            dh, pm, gmla["mla_w_a"][j], gmla["mla_w_uq"][j], gmla["mla_w_ukv"][j], dgains[j] = _mla_proj_bwd(
                dh, dq, dk, dv, sv["h1"], mod[l], gnorm[l, 1:2], gains, tabs, wa, wuq, wukv, j, dm, "mla_proj_bwd")
        pm = _fold_parts(pm, dm) + _fold_parts(pg, dm)
        dh, p0 = ffn_back(dh, sv, l, 0)
        dmod[l] = jnp.concatenate([p0[:, 0:3], pm[:, 0:3], p2[:, 0:3]], axis=1).reshape(B + 1, 9 * D)
        dgn[l] = jnp.stack([p0[:, 3].sum(0), pm[:, 3].sum(0), p2[:, 3].sum(0)])
    grad_x = dh[:B * N].reshape(B, N, D)

    Gs = dict(ffn_w1=G["ffn_w1"], ffn_w3=G["ffn_w3"], ffn_w2=G["ffn_w2"])
    Gs["sc_w_in"] = jnp.moveaxis(G["sc_w_in"].reshape(La, S, 3, D, wq), (1, 2), (0, 3)).reshape(S, La, D, 3 * wq)
    Gs["sc_w_out"] = jnp.moveaxis(G["sc_w_out"].reshape(La, S, D // S, D), 1, 0)
    Gs["mla_w_o"] = jnp.moveaxis(G["mla_w_o"].reshape(Lb, S, HEADS * V_HEAD // S, D), 1, 0)
    ga = jnp.stack(gmla["mla_w_a"])[:, :, :Q_LORA + KV_LORA + QK_ROPE]
    Gs["mla_w_a"] = jnp.moveaxis(ga.reshape(Lb, S, D // S, -1), 1, 0).astype(BF16)
    guq = jnp.stack(gmla["mla_w_uq"]).reshape(Lb, Q_LORA, HEADS, HEAD_PAD)[..., :QK_HEAD]
    Gs["mla_w_uq"] = jnp.moveaxis(guq.reshape(Lb, Q_LORA, S, -1), 2, 0).astype(BF16)
    Gs["mla_w_ukv"] = jnp.moveaxis(jnp.stack(gmla["mla_w_ukv"]).reshape(Lb, KV_LORA, S, -1), 2, 0).astype(BF16)

    parts = [Gs[n].reshape(S, -1, Gs[n].shape[-1]) for n in big]
    theirs = _swap_halves(parts, "swap_halves")
    pairs = [_pair_sum(g_, r_, ci, "pair_sum") for g_, r_ in zip(parts, theirs)]
    recv = _exchange(pairs, ("x", "y"), True, "scatter_grads")
    sums = [_sum_slots(r, "sum_slots") for r in recv]
    both = _exchange(sums, ("c",), False, "swap_cores")
    gsum = dict(zip(big, both))

    dgains_a = jnp.stack(dgains)
    small = [jnp.stack(dmod).reshape(-1), jnp.stack(dgn).reshape(-1), jnp.stack(dconv).reshape(-1), dgains_a.reshape(-1)]
    sizes = [s_.shape[0] for s_ in small]
    flat = jnp.concatenate(small)
    pad = (-flat.shape[0]) % 1024
    flat = jnp.pad(flat, (0, pad)).reshape(-1, 128)
    allsmall = _exchange([flat], ("x", "y", "c"), False, "gather_small")[0].reshape(ndev, -1)
    offs = [0]
    for s_ in sizes:
        offs.append(offs[-1] + s_)
    dmod_all = allsmall[:, offs[0]:offs[1]].reshape(ndev, L, B + 1, 9 * D)
    tot = allsmall[:, offs[1]:offs[4]].sum(axis=0)
    g_gnorm = tot[:offs[2] - offs[1]].reshape(L, 3, D)
    g_conv = tot[offs[2] - offs[1]:offs[3] - offs[1]].reshape(La, 3, D)
    g_gains = tot[offs[3] - offs[1]:].reshape(Lb, 8, HEAD_PAD)
    dM = jnp.concatenate([jnp.moveaxis(dmod_all[:, :, :B], 0, 1).reshape(L, ndev * B, 9 * D),
                          dmod_all[:, :, B].sum(axis=0)[:, None, :], jnp.zeros((L, R - ndev * B - 1, 9 * D), F32)], axis=1)
    g_bmod = dM.sum(axis=1)
    dM_mine = lax.dynamic_slice_in_dim(dM, chip * C, C, axis=2)
    g_wmod, dsil = _mod_bwd(cond, dM_mine, w_mod, "mod_bwd")
    dsil_ctx = dsil[:, ndev * B].sum(axis=0)
    dsil_all = _exchange([jnp.pad(dsil_ctx.reshape(-1, 128), ((0, (-(D // 128)) % 8), (0, 0)))], ("x", "y"), False, "gather_dctx")[0]
    dsil_tot = dsil_all.sum(axis=0)[:D // 128].reshape(D)
    sg = jax.nn.sigmoid(c_ctx)
    g_cctx = dsil_tot * (sg * (1.0 + c_ctx * (1.0 - sg)))

    chip_cols = lambda a, width: lax.dynamic_slice_in_dim(a, chip * width, width, axis=a.ndim - 1)
    small_grads = dict(
        c_ctx=g_cctx, b_mod=g_bmod, g_norm=chip_cols(g_gnorm, D // S), sc_conv=chip_cols(g_conv, D // S),
        mla_g_qa=chip_cols(g_gains[:, 0, :Q_LORA], Q_LORA // S), mla_g_kva=g_gains[:, 1, :KV_LORA],
        mla_g_q=g_gains[:, 2, :QK_HEAD], mla_g_k=g_gains[:, 3, :QK_HEAD])

    grads, deltas, new_m, new_v = {}, {}, {}, {}
    for n, w in weights.items():
        shape = w.shape
        w2 = _flat2(w) if w.ndim > 1 else w.reshape(1, -1)
        m2, v2 = (a.reshape(w2.shape) for a in mom[n])
        if n in gsum:
            gs = [gsum[n].reshape(w2.shape)]
        elif n == "w_mod":
            gs = [_flat2(g_wmod)]
        else:
            gs = [small_grads[n].reshape(w2.shape)]
        g_, d_, m_, v_ = _adamw(w2, gs, m2, v2, "adamw")
        grads[n], deltas[n], new_m[n], new_v[n] = (a.reshape(shape) for a in (g_, d_, m_, v_))

    names = list(weights)
    return (loss, grad_x, *[grads[n] for n in names], *[deltas[n] for n in names], *[new_m[n] for n in names],
            *[new_v[n] for n in names])
```

```python
import functools
import math
from typing import NamedTuple

import jax
import jax.numpy as jnp
from jax import lax
from jax.experimental import pallas as pl
from jax.experimental.pallas import tpu as pltpu

F32 = jnp.float32
BF16 = jnp.bfloat16
EPS = 1e-6
GRID_W = 64
HEADS = 8
QK_NOPE = 128
QK_ROPE = 64
QK_HEAD = QK_NOPE + QK_ROPE
HEAD_PAD = 256
V_HEAD = 128
Q_LORA = 256
KV_LORA = 128
ROPE_BASE = 10000.0
QK_SCALE = QK_HEAD ** -0.5
ADAM_LR, ADAM_B1, ADAM_B2, ADAM_EPS, ADAM_WD, ADAM_STEP = 0.001, 0.9, 0.999, 1e-08, 0.01, 10
N_CHIPS = 4
VMEM_LIMIT = 56 * 1024 * 1024
MESH = pl.DeviceIdType.MESH
NEG = -1e30


class Dims(NamedTuple):
    B: int
    N: int
    CTX: int
    D: int
    T: int
    tm: int


def _cparams(*sem):
    return pltpu.CompilerParams(dimension_semantics=sem if sem else None, vmem_limit_bytes=VMEM_LIMIT)


def _dot(a, b):
    return jnp.dot(a, b, preferred_element_type=F32)


def _dot_nt(a, b):
    return lax.dot_general(a, b, (((1,), (1,)), ((), ())), preferred_element_type=F32)


def _dot_tn(a, b):
    return lax.dot_general(a, b, (((0,), (0,)), ((), ())), preferred_element_type=F32)


def _rms(x, n):
    r = lax.rsqrt(jnp.sum(x * x, axis=-1, keepdims=True) * (1.0 / n) + EPS)
    return x * r, r


def _rms_bwd(dxh, xh, r, n):
    return r * (dxh - xh * (jnp.sum(dxh * xh, axis=-1, keepdims=True) * (1.0 / n)))


def _pre(h, g, shift, scale):
    xh, _ = _rms(h, h.shape[-1])
    return (xh * g) * (1.0 + scale) + shift


def _pre_bwd(dout, h, g, scale):
    d = h.shape[-1]
    xh, r = _rms(h, d)
    n = xh * g
    dshift = jnp.sum(dout, axis=0, keepdims=True)
    dscale = jnp.sum(dout * n, axis=0, keepdims=True)
    dn = dout * (1.0 + scale)
    dg = jnp.sum(dn * xh, axis=0, keepdims=True)
    dh = _rms_bwd(dn * g, xh, r, d)
    return dh, dshift, dscale, dg


def _write_part(part_ref, dshift=None, dscale=None, dgate=None, dg=None):
    z = jnp.zeros((1, part_ref.shape[-1]), F32)
    part_ref[0, 0:1, :] = z if dshift is None else dshift
    part_ref[0, 1:2, :] = z if dscale is None else dscale
    part_ref[0, 2:3, :] = z if dgate is None else dgate
    part_ref[0, 3:4, :] = z if dg is None else dg
    part_ref[0, 4:8, :] = jnp.zeros((4, part_ref.shape[-1]), F32)


def _grp(dm):
    nb = dm.N // dm.tm
    return lambda i: jnp.minimum(i // nb, dm.B)


def _n_chunks(rows, row_bytes):
    n = 16
    while n > 1 and (rows % (16 * n) or (rows // n) * row_bytes < (256 << 10)):
        n //= 2
    return n


def _start_local(src, dst, sems, k0, nchunk):
    ch = src.shape[0] // nchunk
    copies = []
    for j in range(nchunk):
        cp = pltpu.make_async_copy(src.at[pl.ds(j * ch, ch)], dst.at[pl.ds(j * ch, ch)], sems.at[k0 + j])
        cp.start()
        copies.append(cp)
    return copies


def _exchange(arrs, axes, scatter, name):
    n = len(arrs)
    nbits = len(axes)
    slots = 2 ** nbits
    pats = list(range(1, slots))
    nck = [_n_chunks(a.shape[-2], a.shape[-1] * a.dtype.itemsize) for a in arrs]
    base = [sum(nck[:i]) * len(pats) for i in range(n)]
    nsem = sum(nck) * len(pats)

    def body(*refs):
        ins, outs = refs[:n], refs[n:2 * n]
        send, recv, loc = refs[2 * n:]
        pos = {a: lax.axis_index(a) for a in ("x", "y", "c")}

        def slot_of(p):
            s = 0
            for a in axes:
                s = 2 * s + p[a]
            return s

        me = slot_of(pos)
        local = []
        for i in range(n):
            local += _start_local(ins[i].at[me] if scatter else ins[i], outs[i].at[me], loc, sum(nck[:i]), nck[i])
        remote = []
        for pi, pat in enumerate(pats):
            peer = dict(pos)
            for bi, a in enumerate(axes):
                if (pat >> (nbits - 1 - bi)) & 1:
                    peer[a] = 1 - pos[a]
            them = slot_of(peer)
            for i in range(n):
                ch = arrs[i].shape[-2] // nck[i]
                for j in range(nck[i]):
                    k = base[i] + pi * nck[i] + j
                    rs = pl.ds(j * ch, ch)
                    cp = pltpu.make_async_remote_copy(
                        src_ref=ins[i].at[them, rs] if scatter else ins[i].at[rs], dst_ref=outs[i].at[me, rs],
                        send_sem=send.at[k], recv_sem=recv.at[k],
                        device_id=(peer["x"], peer["y"], peer["c"]), device_id_type=MESH)
                    cp.start()
                    remote.append(cp)
        for cp in local:
            cp.wait()
        for cp in remote:
            cp.wait()

    out_shape = [jax.ShapeDtypeStruct(a.shape if scatter else (slots,) + a.shape, a.dtype) for a in arrs]
    any_spec = pl.BlockSpec(memory_space=pl.ANY)
    outs = pl.pallas_call(
        body, name=name, out_shape=out_shape, in_specs=[any_spec] * n, out_specs=[any_spec] * n,
        scratch_shapes=[pltpu.SemaphoreType.DMA((nsem,)), pltpu.SemaphoreType.DMA((nsem,)), pltpu.SemaphoreType.DMA((sum(nck),))],
        compiler_params=pltpu.CompilerParams(has_side_effects=True),
    )(*arrs)
    return list(outs)


def _gather_two_level(arrs, name):
    n = len(arrs)
    halves = [a.shape[0] // 2 for a in arrs]
    nck = [_n_chunks(h, a.shape[1] * a.dtype.itemsize) for h, a in zip(halves, arrs)]
    base = [3 * sum(nck[:i]) for i in range(n)]
    nsem = 3 * sum(nck)

    def body(*refs):
        ins, outs = refs[:n], refs[n:2 * n]
        send1, recv1, send2, recv2, loc = refs[2 * n:]
        x, y, c = lax.axis_index("x"), lax.axis_index("y"), lax.axis_index("c")
        me = 2 * x + y
        local = []
        for i in range(n):
            local += _start_local(ins[i], outs[i].at[me], loc, 2 * sum(nck[:i]), 2 * nck[i])
        chips =[(x, 1 - y), (1 - x, y), (1 - x, 1 - y)]

        def pieces():
            for pi, (px, py) in enumerate(chips):
                for i in range(n):
                    ch = halves[i] // nck[i]
                    for j in range(nck[i]):
                        yield base[i] + pi * nck[i] + j, px, py, 2 * px + py, i, j * ch, ch

        def rows(i, off, ch, core):
            return pl.ds(pl.multiple_of(core * halves[i] + off, 16), ch)

        first = []
        for k, px, py, them, i, off, ch in pieces():
            rs = rows(i, off, ch, c)
            cp = pltpu.make_async_remote_copy(src_ref=ins[i].at[rs], dst_ref=outs[i].at[me, rs], send_sem=send1.at[k],
                                              recv_sem=recv1.at[k], device_id=(px, py, c), device_id_type=MESH)
            cp.start()
            first.append(cp)
        passed = []
        for k, px, py, them, i, off, ch in pieces():
            rs = rows(i, off, ch, c)
            land = outs[i].at[them, rs]
            pltpu.make_async_remote_copy(src_ref=ins[i].at[rs], dst_ref=land, send_sem=send1.at[k], recv_sem=recv1.at[k],
                                         device_id=(px, py, c), device_id_type=MESH).wait_recv()
            cp = pltpu.make_async_remote_copy(src_ref=land, dst_ref=land, send_sem=send2.at[k], recv_sem=recv2.at[k],
                                              device_id=(x, y, 1 - c), device_id_type=MESH)
            cp.start()
            passed.append(cp)
        for cp in first + passed:
            cp.wait_send()
        for k, px, py, them, i, off, ch in pieces():
            theirs = outs[i].at[them, rows(i, off, ch, 1 - c)]
            pltpu.make_async_remote_copy(src_ref=theirs, dst_ref=theirs, send_sem=send2.at[k], recv_sem=recv2.at[k],
                                         device_id=(x, y, 1 - c), device_id_type=MESH).wait_recv()
        for cp in local:
            cp.wait()

    any_spec = pl.BlockSpec(memory_space=pl.ANY)
    outs = pl.pallas_call(
        body, name=name, out_shape=[jax.ShapeDtypeStruct((N_CHIPS,) + a.shape, a.dtype) for a in arrs],
        in_specs=[any_spec] * n, out_specs=[any_spec] * n,
        scratch_shapes=[pltpu.SemaphoreType.DMA((nsem,))] * 4 + [pltpu.SemaphoreType.DMA((2 * sum(nck),))],
        compiler_params=pltpu.CompilerParams(has_side_effects=True),
    )(*arrs)
    return list(outs)


def _swap_halves(arrs, name):
    n = len(arrs)
    S = arrs[0].shape[0]
    halves = [a.shape[1] // 2 for a in arrs]
    nck = [_n_chunks(h, a.shape[2] * a.dtype.itemsize) for h, a in zip(halves, arrs)]
    base = [S * sum(nck[:i]) for i in range(n)]
    nsem = S * sum(nck)

    def body(*refs):
        ins, outs = refs[:n], refs[n:2 * n]
        send, recv = refs[2 * n:]
        x, y, c = lax.axis_index("x"), lax.axis_index("y"), lax.axis_index("c")
        copies = []
        for i in range(n):
            ch = halves[i] // nck[i]
            for s in range(S):
                for j in range(nck[i]):
                    k = base[i] + s * nck[i] + j
                    src = ins[i].at[s, pl.ds(pl.multiple_of((1 - c) * halves[i] + j * ch, 16), ch)]
                    cp = pltpu.make_async_remote_copy(src_ref=src, dst_ref=outs[i].at[s, pl.ds(j * ch, ch)], send_sem=send.at[k],
                                                      recv_sem=recv.at[k], device_id=(x, y, 1 - c), device_id_type=MESH)
                    cp.start()
                    copies.append(cp)
        for cp in copies:
            cp.wait()

    any_spec = pl.BlockSpec(memory_space=pl.ANY)
    outs = pl.pallas_call(
        body, name=name, out_shape=[jax.ShapeDtypeStruct((S, h, a.shape[2]), a.dtype) for h, a in zip(halves, arrs)],
        in_specs=[any_spec] * n, out_specs=[any_spec] * n,
        scratch_shapes=[pltpu.SemaphoreType.DMA((nsem,))] * 2,
        compiler_params=pltpu.CompilerParams(has_side_effects=True),
    )(*arrs)
    return list(outs)


def _pair_sum(g, r, core, name):
    S, rows, C = g.shape
    half = rows // 2
    tr = _row_block(half, C)
    nb = half // tr

    def body(core_ref, g_ref, r_ref, o_ref):
        del core_ref
        o_ref[...] = (g_ref[...].astype(F32) + r_ref[...].astype(F32)).astype(BF16)

    blk = pl.BlockSpec((None, tr, C), lambda s, i, cr: (s, i, 0))
    return pl.pallas_call(
        body, name=name,
        grid_spec=pltpu.PrefetchScalarGridSpec(
            num_scalar_prefetch=1, grid=(S, nb),
            in_specs=[pl.BlockSpec((None, tr, C), lambda s, i, cr: (s, cr[0] * nb + i, 0)), blk], out_specs=blk),
        out_shape=jax.ShapeDtypeStruct((S, half, C), BF16),
        compiler_params=_cparams("parallel", "parallel"),
    )(core.reshape(1).astype(jnp.int32), g, r)


def _ffn_fwd(h, mod, g, w1, w3, w2, l, k, dm, name):
    T, D = h.shape
    S, F = w1.shape[0], w1.shape[-1]
    tm = dm.tm
    r0 = 6 if k else 0
    grp = _grp(dm)

    def body(h_ref, mod_ref, g_ref, w1_ref, w3_ref, w2_ref, ho_ref, a_ref, b_ref, hn_ref, y_ref, hn_s, acc):
        s = pl.program_id(1)

        @pl.when(s == 0)
        def _():
            hn = _pre(h_ref[...], g_ref[...], mod_ref[0, r0:r0 + 1, :], mod_ref[0, r0 + 1:r0 + 2, :]).astype(BF16)
            hn_s[...] = hn
            hn_ref[...] = hn
            acc[...] = jnp.zeros_like(acc)

        hn = hn_s[...]
        a = _dot(hn, w1_ref[...])
        b = _dot(hn, w3_ref[...])
        a_ref[0] = a.astype(BF16)
        b_ref[0] = b.astype(BF16)
        sw = (a * jax.nn.sigmoid(a) * b).astype(BF16)
        acc[...] += _dot(sw, w2_ref[...])

        @pl.when(s == S - 1)
        def _():
            y = acc[...]
            y_ref[...] = y.astype(BF16)
            ho_ref[...] = h_ref[...] + 0.5 * mod_ref[0, r0 + 2:r0 + 3, :] * y

    row = pl.BlockSpec((tm, D), lambda i, s: (i, 0))
    wcol = pl.BlockSpec((None, None, None, D, F), lambda i, s: (s, l, k, 0, 0))
    wrow = pl.BlockSpec((None, None, None, F, D), lambda i, s: (s, l, k, 0, 0))
    ab = pl.BlockSpec((1, tm, F), lambda i, s: (s, i, 0))
    return pl.pallas_call(
        body, name=name, grid=(T // tm, S),
        in_specs=[row, pl.BlockSpec((1, 9, D), lambda i, s: (grp(i), 0, 0)), pl.BlockSpec((1, D), lambda i, s: (0, 0)),
                  wcol, wcol, wrow],
        out_specs=[row, ab, ab, row, row],
        out_shape=[jax.ShapeDtypeStruct((T, D), F32), jax.ShapeDtypeStruct((S, T, F), BF16),
                   jax.ShapeDtypeStruct((S, T, F), BF16), jax.ShapeDtypeStruct((T, D), BF16),
                   jax.ShapeDtypeStruct((T, D), BF16)],
        scratch_shapes=[pltpu.VMEM((tm, D), BF16), pltpu.VMEM((tm, D), F32)],
        compiler_params=_cparams("parallel", "arbitrary"),
    )(h, mod, g, w1, w3, w2)


def _ffn_bwd(dh, h, mod, g, y, a, b, w1, w3, w2, l, k, dm, name):
    T, D = h.shape
    S, F = w1.shape[0], w1.shape[-1]
    tm = dm.tm
    r0 = 6 if k else 0
    grp = _grp(dm)

    def body(dh_ref, h_ref, mod_ref, g_ref, y_ref, a_ref, b_ref, w1_ref, w3_ref, w2_ref,
             dho_ref, da_ref, db_ref, sw_ref, dy_ref, part_ref, dy_s, acc):
        s = pl.program_id(1)

        @pl.when(s == 0)
        def _():
            dy = (0.5 * mod_ref[0, r0 + 2:r0 + 3, :] * dh_ref[...]).astype(BF16)
            dy_s[...] = dy
            dy_ref[...] = dy
            acc[...] = jnp.zeros_like(acc)

        ds = _dot_nt(dy_s[...], w2_ref[...])
        av = a_ref[0].astype(F32)
        bv = b_ref[0].astype(F32)
        sig = jax.nn.sigmoid(av)
        sil = av * sig
        sw_ref[0] = (sil * bv).astype(BF16)
        db = (ds * sil).astype(BF16)
        da = (ds * bv * (sig * (1.0 + av * (1.0 - sig)))).astype(BF16)
        da_ref[0] = da
        db_ref[0] = db
        acc[...] += _dot_nt(da, w1_ref[...]) + _dot_nt(db, w3_ref[...])

        @pl.when(s == S - 1)
        def _():
            dhv = dh_ref[...]
            dhb, dshift, dscale, dg = _pre_bwd(acc[...], h_ref[...], g_ref[...], mod_ref[0, r0 + 1:r0 + 2, :])
            dho_ref[...] = dhv + dhb
            dgate = 0.5 * jnp.sum(dhv * y_ref[...].astype(F32), axis=0, keepdims=True)
            _write_part(part_ref, dshift, dscale, dgate, dg)

    row = pl.BlockSpec((tm, D), lambda i, s: (i, 0))
    wcol = pl.BlockSpec((None, None, None, D, F), lambda i, s: (s, l, k, 0, 0))
    wrow = pl.BlockSpec((None, None, None, F, D), lambda i, s: (s, l, k, 0, 0))
    ab = pl.BlockSpec((1, tm, F), lambda i, s: (s, i, 0))
    stf = jax.ShapeDtypeStruct((S, T, F), BF16)
    return pl.pallas_call(
        body, name=name, grid=(T // tm, S),
        in_specs=[row, row, pl.BlockSpec((1, 9, D), lambda i, s: (grp(i), 0, 0)), pl.BlockSpec((1, D), lambda i, s: (0, 0)),
                  row, ab, ab, wcol, wcol, wrow],
        out_specs=[row, ab, ab, ab, row, pl.BlockSpec((1, 8, D), lambda i, s: (i, 0, 0))],
        out_shape=[jax.ShapeDtypeStruct((T, D), F32), stf, stf, stf, jax.ShapeDtypeStruct((T, D), BF16),
                   jax.ShapeDtypeStruct((T // tm, 8, D), F32)],
        scratch_shapes=[pltpu.VMEM((tm, D), BF16), pltpu.VMEM((tm, D), F32)],
        compiler_params=_cparams("parallel", "arbitrary"),
    )(dh, h, mod, g, y, a, b, w1, w3, w2)


def _mm_tn(a, b, a_spec, b_spec, out, out_spec, grid, name):
    nk = grid[-1]
    kax = len(grid) - 1
    blk = tuple(d for d in out_spec.block_shape if d is not None)

    def body(a_ref, b_ref, o_in, o_ref, acc):
        del o_in
        kk = pl.program_id(kax)

        @pl.when(kk == 0)
        def _():
            acc[...] = jnp.zeros_like(acc)

        acc[...] += _dot_tn(a_ref[...].astype(BF16), b_ref[...].astype(BF16))

        @pl.when(kk == nk - 1)
        def _():
            o_ref[...] = acc[...].astype(o_ref.dtype)

    return pl.pallas_call(
        body, name=name, grid=grid,
        in_specs=[a_spec, b_spec, pl.BlockSpec(memory_space=pl.ANY)], out_specs=out_spec,
        out_shape=jax.ShapeDtypeStruct(out.shape, out.dtype),
        scratch_shapes=[pltpu.VMEM(blk, F32)], input_output_aliases={2: 0},
        compiler_params=_cparams(*(["parallel"] * kax + ["arbitrary"])),
    )(a, b, out)


def _sc_in_fwd(h, mod, g, w_in, j, dm, name):
    T, D = h.shape
    tm = dm.tm
    wq = D // N_CHIPS
    nq = 3 * N_CHIPS
    grp = _grp(dm)

    def body(h_ref, mod_ref, g_ref, w_ref, p_ref, hn_ref, hn_s):
        @pl.when(pl.program_id(1) == 0)
        def _():
            hn = _pre(h_ref[...], g_ref[...], mod_ref[0, 3:4, :], mod_ref[0, 4:5, :]).astype(BF16)
            hn_s[...] = hn
            hn_ref[...] = hn

        p_ref[...] = _dot(hn_s[...], w_ref[...])

    row = pl.BlockSpec((tm, D), lambda i, q: (i, 0))
    return pl.pallas_call(
        body, name=name, grid=(T // tm, nq),
        in_specs=[row, pl.BlockSpec((1, 9, D), lambda i, q: (grp(i), 0, 0)), pl.BlockSpec((1, D), lambda i, q: (0, 0)),
                  pl.BlockSpec((None, None, D, wq), lambda i, q: (q // 3, j, 0, q % 3))],
        out_specs=[pl.BlockSpec((None, tm, wq), lambda i, q: (q // N_CHIPS, i, q % N_CHIPS)), row],
        out_shape=[jax.ShapeDtypeStruct((3, T, D), F32), jax.ShapeDtypeStruct((T, D), BF16)],
        scratch_shapes=[pltpu.VMEM((tm, D), BF16)],
        compiler_params=_cparams("parallel", "arbitrary"),
    )(h, mod, g, w_in)


def _conv_cols(dm):
    return 256 if dm.D % 256 == 0 else 128


def _seg_masks(r, dm):
    bn = dm.B * dm.N
    lat = r < bn
    off = jnp.where(lat, lax.rem(r, dm.N), lax.rem(r - bn, dm.CTX))
    seg = jnp.where(lat, dm.N, dm.CTX)
    inside = (r >= 0) & (r < dm.T)
    return ((off != 0) & inside).astype(F32), ((off != seg - 1) & inside).astype(F32)


def _conv_specs(dm):
    tb, cb, nr8 = dm.tm, _conv_cols(dm), dm.T // 8
    prev8 = lambda c, i: jnp.maximum(i * (tb // 8) - 1, 0)
    next8 = lambda c, i: jnp.minimum((i + 1) * (tb // 8), nr8 - 1)
    return dict(
        tb=tb, cb=cb,
        p=pl.BlockSpec((3, tb, cb), lambda c, i: (0, i, c)),
        p_prev=pl.BlockSpec((3, 8, cb), lambda c, i: (0, prev8(c, i), c)),
        p_next=pl.BlockSpec((3, 8, cb), lambda c, i: (0, next8(c, i), c)),
        row=pl.BlockSpec((tb, cb), lambda c, i: (i, c)),
        row_prev=pl.BlockSpec((8, cb), lambda c, i: (prev8(c, i), c)),
        row_next=pl.BlockSpec((8, cb), lambda c, i: (next8(c, i), c)),
        w=pl.BlockSpec((3, cb), lambda c, i: (0, c)),
    )


def _shift_rows(x, before, after, tb):
    rid = lax.broadcasted_iota(jnp.int32, x.shape, 0)
    down = jnp.where(rid == 0, before, pltpu.roll(x, 1, 0))
    up = jnp.where(rid == tb - 1, after, pltpu.roll(x, tb - 1, 0))
    return down, up


def _conv_fwd(p, wc, dm, name):
    T, D = dm.T, dm.D
    sp = _conv_specs(dm)
    tb, cb = sp["tb"], sp["cb"]

    def body(p_ref, pp_ref, pn_ref, w_ref, z_ref):
        r = pl.program_id(1) * tb + lax.broadcasted_iota(jnp.int32, (tb, cb), 0)
        mp, mn = _seg_masks(r, dm)
        cu = p_ref[1] * p_ref[2]
        prev, nxt = _shift_rows(cu, pp_ref[1, 7:8, :] * pp_ref[2, 7:8, :], pn_ref[1, 0:1, :] * pn_ref[2, 0:1, :], tb)
        conv = w_ref[0:1, :] * (prev * mp) + w_ref[1:2, :] * cu + w_ref[2:3, :] * (nxt * mn)
        z_ref[...] = (p_ref[0] * conv).astype(BF16)

    return pl.pallas_call(
        body, name=name, grid=(D // cb, T // tb),
        in_specs=[sp["p"], sp["p_prev"], sp["p_next"], sp["w"]], out_specs=sp["row"],
        out_shape=jax.ShapeDtypeStruct((T, D), BF16),
        compiler_params=_cparams("parallel", "parallel"),
    )(p, p, p, wc)


def _conv_bwd(dz, p, wc, dm, name):
    T, D = dm.T, dm.D
    sp = _conv_specs(dm)
    tb, cb = sp["tb"], sp["cb"]

    def body(dz_ref, dzp_ref, dzn_ref, p_ref, pp_ref, pn_ref, w_ref, dp_ref, dw_ref):
        i = pl.program_id(1)
        r = i * tb + lax.broadcasted_iota(jnp.int32, (tb, cb), 0)
        mp, mn = _seg_masks(r, dm)
        rb = i * tb + lax.broadcasted_iota(jnp.int32, (1, cb), 0)
        _, mn_before = _seg_masks(rb - 1, dm)
        mp_after, _ = _seg_masks(rb + tb, dm)
        bg, cg, u = p_ref[0], p_ref[1], p_ref[2]
        cu = cg * u
        prev, nxt = _shift_rows(cu, pp_ref[1, 7:8, :] * pp_ref[2, 7:8, :], pn_ref[1, 0:1, :] * pn_ref[2, 0:1, :], tb)
        prev = prev * mp
        nxt = nxt * mn
        w0, w1, w2 = w_ref[0:1, :], w_ref[1:2, :], w_ref[2:3, :]
        conv = w0 * prev + w1 * cu + w2 * nxt
        dz = dz_ref[...]
        dp_ref[0] = dz * conv
        dconv = dz * bg

        @pl.when(i == 0)
        def _():
            dw_ref[...] = jnp.zeros_like(dw_ref)

        dw_ref[0:1, :] += jnp.sum(dconv * prev, axis=0, keepdims=True)
        dw_ref[1:2, :] += jnp.sum(dconv * cu, axis=0, keepdims=True)
        dw_ref[2:3, :] += jnp.sum(dconv * nxt, axis=0, keepdims=True)
        dconv_before = dzp_ref[7:8, :] * pp_ref[0, 7:8, :] * mn_before
        dconv_after = dzn_ref[0:1, :] * pn_ref[0, 0:1, :] * mp_after
        from_prev, _ = _shift_rows(dconv * mn, dconv_before, dconv_after, tb)
        _, from_next = _shift_rows(dconv * mp, dconv_before, dconv_after, tb)
        dcu = w1 * dconv + w0 * from_next + w2 * from_prev
        dp_ref[1] = dcu * u
        dp_ref[2] = dcu * cg

    return pl.pallas_call(
        body, name=name, grid=(D // cb, T // tb),
        in_specs=[sp["row"], sp["row_prev"], sp["row_next"], sp["p"], sp["p_prev"], sp["p_next"], sp["w"]],
        out_specs=[sp["p"], sp["w"]],
        out_shape=[jax.ShapeDtypeStruct((3, T, D), F32), jax.ShapeDtypeStruct((3, D), F32)],
        compiler_params=_cparams("parallel", "arbitrary"),
    )(dz, dz, dz, p, p, p, wc)


def _out_fwd(z, w, h, mod, j, dm, name):
    T, D = h.shape
    K = z.shape[1]
    tm = dm.tm
    grp = _grp(dm)

    def body(z_ref, w_ref, h_ref, mod_ref, ho_ref, y_ref):
        y = _dot(z_ref[...], w_ref[...])
        y_ref[...] = y.astype(BF16)
        ho_ref[...] = h_ref[...] + mod_ref[0, 5:6, :] * y

    row = pl.BlockSpec((tm, D), lambda i: (i, 0))
    return pl.pallas_call(
        body, name=name, grid=(T // tm,),
        in_specs=[pl.BlockSpec((tm, K), lambda i: (i, 0)), pl.BlockSpec((None, K, D), lambda i: (j, 0, 0)), row,
                  pl.BlockSpec((1, 9, D), lambda i: (grp(i), 0, 0))],
        out_specs=[row, row],
        out_shape=[jax.ShapeDtypeStruct((T, D), F32), jax.ShapeDtypeStruct((T, D), BF16)],
        compiler_params=_cparams("parallel"),
    )(z, w, h, mod)


def _out_bwd(dh, y, w, mod, j, dm, name):
    T, D = dh.shape
    K = w.shape[1]
    tm = dm.tm
    grp = _grp(dm)

    def body(dh_ref, y_ref, w_ref, mod_ref, dy_ref, dz_ref, part_ref):
        dhv = dh_ref[...]
        dy = (mod_ref[0, 5:6, :] * dhv).astype(BF16)
        dy_ref[...] = dy
        dz_ref[...] = _dot_nt(dy, w_ref[...])
        _write_part(part_ref, dgate=jnp.sum(dhv * y_ref[...].astype(F32), axis=0, keepdims=True))

    row = pl.BlockSpec((tm, D), lambda i: (i, 0))
    return pl.pallas_call(
        body, name=name, grid=(T // tm,),
        in_specs=[row, row, pl.BlockSpec((None, K, D), lambda i: (j, 0, 0)), pl.BlockSpec((1, 9, D), lambda i: (grp(i), 0, 0))],
        out_specs=[row, pl.BlockSpec((tm, K), lambda i: (i, 0)), pl.BlockSpec((1, 8, D), lambda i: (i, 0, 0))],
        out_shape=[jax.ShapeDtypeStruct((T, D), BF16), jax.ShapeDtypeStruct((T, K), F32),
                   jax.ShapeDtypeStruct((T // tm, 8, D), F32)],
        compiler_params=_cparams("parallel"),
    )(dh, y, w, mod)


def _sc_in_bwd(dh, dp, h, mod, g, w_in, j, dm, name):
    T, D = h.shape
    tm = dm.tm
    wq = D // N_CHIPS
    nq = 3 * N_CHIPS
    grp = _grp(dm)

    def body(dh_ref, dp_ref, h_ref, mod_ref, g_ref, w_ref, dho_ref, part_ref, acc):
        q = pl.program_id(1)

        @pl.when(q == 0)
        def _():
            acc[...] = jnp.zeros_like(acc)

        acc[...] += _dot_nt(dp_ref[...].astype(BF16), w_ref[...])

        @pl.when(q == nq - 1)
        def _():
            dhb, dshift, dscale, dg = _pre_bwd(acc[...], h_ref[...], g_ref[...], mod_ref[0, 4:5, :])
            dho_ref[...] = dh_ref[...] + dhb
            _write_part(part_ref, dshift, dscale, None, dg)

    row = pl.BlockSpec((tm, D), lambda i, q: (i, 0))
    return pl.pallas_call(
        body, name=name, grid=(T // tm, nq),
        in_specs=[row, pl.BlockSpec((None, tm, wq), lambda i, q: (q // N_CHIPS, i, q % N_CHIPS)), row,
                  pl.BlockSpec((1, 9, D), lambda i, q: (grp(i), 0, 0)), pl.BlockSpec((1, D), lambda i, q: (0, 0)),
                  pl.BlockSpec((None, None, D, wq), lambda i, q: (q // 3, j, 0, q % 3))],
        out_specs=[row, pl.BlockSpec((1, 8, D), lambda i, q: (i, 0, 0))],
        out_shape=[jax.ShapeDtypeStruct((T, D), F32), jax.ShapeDtypeStruct((T // tm, 8, D), F32)],
        scratch_shapes=[pltpu.VMEM((tm, D), F32)],
        compiler_params=_cparams("parallel", "arbitrary"),
    )(dh, dp, h, mod, g, w_in)


def _rope(t, c, s1, s2):
    return t * c + pltpu.roll(t, HEAD_PAD - 16, 1) * s1 + pltpu.roll(t, 16, 1) * s2


def _rope_t(dy, c, s1, s2):
    return dy * c + pltpu.roll(dy * s1, 16, 1) + pltpu.roll(dy * s2, HEAD_PAD - 16, 1)


def _mla_heads_fwd(z, g_ref, wuq_ref, wukv_ref):
    cq, ckv, krp = z[:, :Q_LORA], z[:, Q_LORA:Q_LORA + KV_LORA], z[:, Q_LORA + KV_LORA:]
    cqh, rq = _rms(cq, Q_LORA)
    ckvh, rkv = _rms(ckv, KV_LORA)
    cqn = (cqh * g_ref[0:1, :]).astype(BF16)
    ckvn = (ckvh * g_ref[1:2, :KV_LORA]).astype(BF16)
    qraw = _dot(cqn, wuq_ref[...])
    kvraw = _dot(ckvn, wukv_ref[...])
    return dict(krp=krp, cqh=cqh, rq=rq, ckvh=ckvh, rkv=rkv, cqn=cqn, ckvn=ckvn, qraw=qraw, kvraw=kvraw)


def _mla_proj_fwd(h, mod, g, gains, tabs, w_a, w_uq, w_ukv, j, dm, name):
    T, D = h.shape
    tm = min(dm.tm, 256)
    grp = lambda i: jnp.minimum(i // (dm.N // tm), dm.B)
    HP = HEAD_PAD

    def body(h_ref, mod_ref, g_ref, gn_ref, tab_ref, wa_ref, wuq_ref, wukv_ref, hn_ref, q_ref, k_ref, v_ref):
        hn = _pre(h_ref[...], g_ref[...], mod_ref[0, 3:4, :], mod_ref[0, 4:5, :]).astype(BF16)
        hn_ref[...] = hn
        f = _mla_heads_fwd(_dot(hn, wa_ref[...]), gn_ref, wuq_ref, wukv_ref)
        c, s1, s2 = tab_ref[0], tab_ref[1], tab_ref[2]
        for hd in range(HEADS):
            qh, _ = _rms(f["qraw"][:, hd * HP:(hd + 1) * HP], QK_HEAD)
            q_ref[:, hd * HP:(hd + 1) * HP] = _rope(qh * gn_ref[2:3, :], c, s1, s2).astype(BF16)
            kpre = jnp.concatenate([f["kvraw"][:, hd * HP:hd * HP + QK_NOPE], f["krp"]], axis=1)
            kh, _ = _rms(kpre, QK_HEAD)
            k_ref[:, hd * HP:(hd + 1) * HP] = _rope(kh * gn_ref[3:4, :], c, s1, s2).astype(BF16)
            v_ref[:, hd * V_HEAD:(hd + 1) * V_HEAD] = f["kvraw"][:, hd * HP + QK_NOPE:(hd + 1) * HP].astype(BF16)

    row = pl.BlockSpec((tm, D), lambda i: (i, 0))
    HQ = HEADS * HP
    return pl.pallas_call(
        body, name=name, grid=(T // tm,),
        in_specs=[row, pl.BlockSpec((1, 9, D), lambda i: (grp(i), 0, 0)), pl.BlockSpec((1, D), lambda i: (0, 0)),
                  pl.BlockSpec((None, 8, HP), lambda i: (j, 0, 0)), pl.BlockSpec((3, tm, HP), lambda i: (0, i, 0)),
                  pl.BlockSpec((None, D, 512), lambda i: (j, 0, 0)), pl.BlockSpec((None, Q_LORA, HQ), lambda i: (j, 0, 0)),
                  pl.BlockSpec((None, KV_LORA, HQ), lambda i: (j, 0, 0))],
        out_specs=[row, pl.BlockSpec((tm, HQ), lambda i: (i, 0)), pl.BlockSpec((tm, HQ), lambda i: (i, 0)),
                   pl.BlockSpec((tm, HEADS * V_HEAD), lambda i: (i, 0))],
        out_shape=[jax.ShapeDtypeStruct((T, D), BF16), jax.ShapeDtypeStruct((T, HQ), BF16),
                   jax.ShapeDtypeStruct((T, HQ), BF16), jax.ShapeDtypeStruct((T, HEADS * V_HEAD), BF16)],
        compiler_params=_cparams("parallel"),
    )(h, mod, g, gains, tabs, w_a, w_uq, w_ukv)


def _mla_proj_bwd(dh, dq, dk, dv, h, mod, g, gains, tabs, w_a, w_uq, w_ukv, j, dm, name):
    T, D = h.shape
    tm = min(dm.tm, 256)
    nblk = T // tm
    grp = lambda i: jnp.minimum(i // (dm.N // tm), dm.B)
    HP = HEAD_PAD
    HQ = HEADS * HP

    def body(dh_ref, dq_ref, dk_ref, dv_ref, h_ref, mod_ref, g_ref, gn_ref, tab_ref, wa_ref, wuq_ref, wukv_ref,
             dho_ref, part_ref, gwa_ref, gwuq_ref, gwukv_ref, dgn_ref, dqraw_s, dkvraw_s):
        i = pl.program_id(0)

        @pl.when(i == 0)
        def _():
            gwa_ref[...] = jnp.zeros_like(gwa_ref)
            gwuq_ref[...] = jnp.zeros_like(gwuq_ref)
            gwukv_ref[...] = jnp.zeros_like(gwukv_ref)
            dgn_ref[...] = jnp.zeros_like(dgn_ref)

        hv = h_ref[...]
        hn = _pre(hv, g_ref[...], mod_ref[0, 3:4, :], mod_ref[0, 4:5, :]).astype(BF16)
        f = _mla_heads_fwd(_dot(hn, wa_ref[...]), gn_ref, wuq_ref, wukv_ref)
        c, s1, s2 = tab_ref[0], tab_ref[1], tab_ref[2]
        gq, gk = gn_ref[2:3, :], gn_ref[3:4, :]
        dgq = jnp.zeros((1, HP), F32)
        dgk = jnp.zeros((1, HP), F32)
        dkrp = jnp.zeros((tm, HP - QK_NOPE), F32)
        for hd in range(HEADS):
            qh, rq = _rms(f["qraw"][:, hd * HP:(hd + 1) * HP], QK_HEAD)
            dqn = _rope_t(dq_ref[:, hd * HP:(hd + 1) * HP], c, s1, s2)
            dgq = dgq + jnp.sum(dqn * qh, axis=0, keepdims=True)
            dqraw_s[:, hd * HP:(hd + 1) * HP] = _rms_bwd(dqn * gq, qh, rq, QK_HEAD)
            kpre = jnp.concatenate([f["kvraw"][:, hd * HP:hd * HP + QK_NOPE], f["krp"]], axis=1)
            kh, rk = _rms(kpre, QK_HEAD)
            dkn = _rope_t(dk_ref[:, hd * HP:(hd + 1) * HP], c, s1, s2)
            dgk = dgk + jnp.sum(dkn * kh, axis=0, keepdims=True)
            dkpre = _rms_bwd(dkn * gk, kh, rk, QK_HEAD)
            dkvraw_s[:, hd * HP:hd * HP + QK_NOPE] = dkpre[:, :QK_NOPE]
            dkrp = dkrp + dkpre[:, QK_NOPE:]
            dkvraw_s[:, hd * HP + QK_NOPE:(hd + 1) * HP] = dv_ref[:, hd * V_HEAD:(hd + 1) * V_HEAD]
        dqraw = dqraw_s[...].astype(BF16)
        dkvraw = dkvraw_s[...].astype(BF16)
        gwuq_ref[...] += _dot_tn(f["cqn"], dqraw)
        gwukv_ref[...] += _dot_tn(f["ckvn"], dkvraw)
        dcqn = _dot_nt(dqraw, wuq_ref[...])
        dckvn = _dot_nt(dkvraw, wukv_ref[...])
        dgqa = jnp.sum(dcqn * f["cqh"], axis=0, keepdims=True)
        dgkva = jnp.sum(dckvn * f["ckvh"], axis=0, keepdims=True)
        dcq = _rms_bwd(dcqn * gn_ref[0:1, :], f["cqh"], f["rq"], Q_LORA)
        dckv = _rms_bwd(dckvn * gn_ref[1:2, :KV_LORA], f["ckvh"], f["rkv"], KV_LORA)
        dz = jnp.concatenate([dcq, dckv, dkrp], axis=1).astype(BF16)
        gwa_ref[...] += _dot_tn(hn, dz)
        dhn = _dot_nt(dz, wa_ref[...])
        dhb, dshift, dscale, dg = _pre_bwd(dhn, hv, g_ref[...], mod_ref[0, 4:5, :])
        dho_ref[...] = dh_ref[...] + dhb
        _write_part(part_ref, dshift, dscale, None, dg)
        dgn_ref[0:1, :] += dgqa
        dgn_ref[1:2, :KV_LORA] += dgkva
        dgn_ref[2:3, :] += dgq
        dgn_ref[3:4, :] += dgk

    row = pl.BlockSpec((tm, D), lambda i: (i, 0))
    wide = pl.BlockSpec((tm, HQ), lambda i: (i, 0))
    const2 = lambda i: (0, 0)
    return pl.pallas_call(
        body, name=name, grid=(nblk,),
        in_specs=[row, wide, wide, pl.BlockSpec((tm, HEADS * V_HEAD), lambda i: (i, 0)), row,
                  pl.BlockSpec((1, 9, D), lambda i: (grp(i), 0, 0)), pl.BlockSpec((1, D), const2),
                  pl.BlockSpec((None, 8, HP), lambda i: (j, 0, 0)), pl.BlockSpec((3, tm, HP), lambda i: (0, i, 0)),
                  pl.BlockSpec((None, D, 512), lambda i: (j, 0, 0)), pl.BlockSpec((None, Q_LORA, HQ), lambda i: (j, 0, 0)),
                  pl.BlockSpec((None, KV_LORA, HQ), lambda i: (j, 0, 0))],
        out_specs=[row, pl.BlockSpec((1, 8, D), lambda i: (i, 0, 0)), pl.BlockSpec((D, 512), const2),
                   pl.BlockSpec((Q_LORA, HQ), const2), pl.BlockSpec((KV_LORA, HQ), const2), pl.BlockSpec((8, HP), const2)],
        out_shape=[jax.ShapeDtypeStruct((T, D), F32), jax.ShapeDtypeStruct((nblk, 8, D), F32),
                   jax.ShapeDtypeStruct((D, 512), F32), jax.ShapeDtypeStruct((Q_LORA, HQ), F32),
                   jax.ShapeDtypeStruct((KV_LORA, HQ), F32), jax.ShapeDtypeStruct((8, HP), F32)],
        scratch_shapes=[pltpu.VMEM((tm, HQ), F32), pltpu.VMEM((tm, HQ), F32)],
        compiler_params=_cparams("arbitrary"),
    )(dh, dq, dk, dv, h, mod, g, gains, tabs, w_a, w_uq, w_ukv)


def _attn_specs(dm):
    tq = dm.CTX
    nq = dm.N // tq
    cblk0 = dm.B * nq
    HP = HEAD_PAD
    qrow = lambda b, i: jnp.where(i < nq, b * nq + i, cblk0 + b)
    return dict(
        tq=tq, nq=nq,
        q=pl.BlockSpec((tq, HP), lambda b, hd, i: (qrow(b, i), hd)),
        k_lat=pl.BlockSpec((dm.N, HP), lambda b, hd, i: (b, hd)),
        k_ctx=pl.BlockSpec((tq, HP), lambda b, hd, i: (cblk0 + b, hd)),
        v_lat=pl.BlockSpec((dm.N, V_HEAD), lambda b, hd, i: (b, hd)),
        v_ctx=pl.BlockSpec((tq, V_HEAD), lambda b, hd, i: (cblk0 + b, hd)),
        o=pl.BlockSpec((tq, V_HEAD), lambda b, hd, i: (qrow(b, i), hd)),
    )


def _attn_probs(q, kl, kc, is_ctx):
    sl = _dot_nt(q, kl) * QK_SCALE
    sc = _dot_nt(q, kc) * QK_SCALE
    sl = sl + jnp.where(is_ctx, NEG, 0.0)
    m = jnp.maximum(jnp.max(sl, axis=-1, keepdims=True), jnp.max(sc, axis=-1, keepdims=True))
    pl_, pc = jnp.exp(sl - m), jnp.exp(sc - m)
    inv = 1.0 / (jnp.sum(pl_, axis=-1, keepdims=True) + jnp.sum(pc, axis=-1, keepdims=True))
    return pl_ * inv, pc * inv


def _attn_fwd(q, k, v, dm, name):
    T = dm.T
    sp = _attn_specs(dm)
    nq = sp["nq"]

    def body(q_ref, kl_ref, kc_ref, vl_ref, vc_ref, o_ref):
        is_ctx = pl.program_id(2) == nq
        pl_, pc = _attn_probs(q_ref[...], kl_ref[...], kc_ref[...], is_ctx)
        o_ref[...] = (_dot(pl_.astype(BF16), vl_ref[...]) + _dot(pc.astype(BF16), vc_ref[...])).astype(BF16)

    return pl.pallas_call(
        body, name=name, grid=(dm.B, HEADS, nq + 1),
        in_specs=[sp["q"], sp["k_lat"], sp["k_ctx"], sp["v_lat"], sp["v_ctx"]], out_specs=sp["o"],
        out_shape=jax.ShapeDtypeStruct((T, HEADS * V_HEAD), BF16),
        compiler_params=_cparams("parallel", "parallel", "arbitrary"),
    )(q, k, k, v, v)


def _attn_bwd(q, k, v, o, do, dm, name):
    T = dm.T
    sp = _attn_specs(dm)
    nq, tq = sp["nq"], sp["tq"]
    HP, HQ, HV = HEAD_PAD, HEADS * HEAD_PAD, HEADS * V_HEAD

    def body(q_ref, kl_ref, kc_ref, vl_ref, vc_ref, o_ref, do_ref, dq_ref, dkl_ref, dkc_ref, dvl_ref, dvc_ref):
        i = pl.program_id(2)

        @pl.when(i == 0)
        def _():
            dkl_ref[...] = jnp.zeros_like(dkl_ref)
            dkc_ref[...] = jnp.zeros_like(dkc_ref)
            dvl_ref[...] = jnp.zeros_like(dvl_ref)
            dvc_ref[...] = jnp.zeros_like(dvc_ref)

        qv = q_ref[...]
        pl_, pc = _attn_probs(qv, kl_ref[...], kc_ref[...], i == nq)
        dov = do_ref[...]
        dob = dov.astype(BF16)
        delta = jnp.sum(dov * o_ref[...].astype(F32), axis=-1, keepdims=True)
        dsl = (pl_ * (_dot_nt(dob, vl_ref[...]) - delta) * QK_SCALE).astype(BF16)
        dsc = (pc * (_dot_nt(dob, vc_ref[...]) - delta) * QK_SCALE).astype(BF16)
        dq_ref[...] = _dot(dsl, kl_ref[...]) + _dot(dsc, kc_ref[...])
        dkl_ref[...] += _dot_tn(dsl, qv)
        dkc_ref[...] += _dot_tn(dsc, qv)
        dvl_ref[...] += _dot_tn(pl_.astype(BF16), dob)
        dvc_ref[...] += _dot_tn(pc.astype(BF16), dob)

    return pl.pallas_call(
        body, name=name, grid=(dm.B, HEADS, nq + 1),
        in_specs=[sp["q"], sp["k_lat"], sp["k_ctx"], sp["v_lat"], sp["v_ctx"], sp["o"], sp["o"]],
        out_specs=[sp["q"], sp["k_lat"], pl.BlockSpec((tq, HP), lambda b, hd, i: (b, hd)),
                   sp["v_lat"], pl.BlockSpec((tq, V_HEAD), lambda b, hd, i: (b, hd))],
        out_shape=[jax.ShapeDtypeStruct((T, HQ), F32), jax.ShapeDtypeStruct((dm.B * dm.N, HQ), F32),
                   jax.ShapeDtypeStruct((dm.B * dm.CTX, HQ), F32), jax.ShapeDtypeStruct((dm.B * dm.N, HV), F32),
                   jax.ShapeDtypeStruct((dm.B * dm.CTX, HV), F32)],
        compiler_params=_cparams("parallel", "parallel", "arbitrary"),
    )(q, k, k, v, v, o, do)


def _loss_grad(h, target, dm, name):
    T, D = h.shape
    tm = dm.tm
    nlat = dm.B * dm.N // tm

    def body(h_ref, t_ref, dh_ref, ls_ref):
        lat = (pl.program_id(0) < nlat).astype(F32)
        diff = (h_ref[...] - t_ref[...]) * lat
        dh_ref[...] = diff * (1.0 / D)
        ls_ref[...] = jnp.zeros(ls_ref.shape, F32) + (0.5 / D) * jnp.sum(diff * diff)

    return pl.pallas_call(
        body, name=name, grid=(T // tm,),
        in_specs=[pl.BlockSpec((tm, D), lambda i: (i, 0)), pl.BlockSpec((tm, D), lambda i: (jnp.minimum(i, nlat - 1), 0))],
        out_specs=[pl.BlockSpec((tm, D), lambda i: (i, 0)), pl.BlockSpec((1, 8, 128), lambda i: (i, 0, 0))],
        out_shape=[jax.ShapeDtypeStruct((T, D), F32), jax.ShapeDtypeStruct((T // tm, 8, 128), F32)],
        compiler_params=_cparams("parallel"),
    )(h, target)


def _col_block(cols, target=1152):
    return max(t for t in range(128, min(cols, target) + 1, 128) if cols % t == 0)


def _mod_fwd(cond, w_mod, b_mod, name):
    L, D, C = w_mod.shape
    R = cond.shape[0]
    cb = _col_block(C)

    def body(c_ref, w_ref, b_ref, o_ref):
        cv = c_ref[...]
        sc = (cv * jax.nn.sigmoid(cv)).astype(BF16)
        o_ref[...] = _dot(sc, w_ref[...].astype(BF16)) + b_ref[...]

    return pl.pallas_call(
        body, name=name, grid=(L, C // cb),
        in_specs=[pl.BlockSpec((R, D), lambda l, c: (0, 0)), pl.BlockSpec((None, D, cb), lambda l, c: (l, 0, c)),
                  pl.BlockSpec((None, 1, cb), lambda l, c: (l, 0, c))],
        out_specs=pl.BlockSpec((None, R, cb), lambda l, c: (l, 0, c)),
        out_shape=jax.ShapeDtypeStruct((L, R, C), F32),
        compiler_params=_cparams("parallel", "parallel"),
    )(cond, w_mod, b_mod)


def _mod_bwd(cond, dmod, w_mod, name):
    L, D, C = w_mod.shape
    R = cond.shape[0]
    cb = _col_block(C)
    nc = C // cb

    def body(c_ref, dm_ref, w_ref, gw_ref, ds_ref):
        cv = c_ref[...]
        sc = (cv * jax.nn.sigmoid(cv)).astype(BF16)
        dmv = dm_ref[...].astype(BF16)
        gw_ref[...] = _dot_tn(sc, dmv)
        part = _dot_nt(dmv, w_ref[...].astype(BF16))

        @pl.when(pl.program_id(1) == 0)
        def _():
            ds_ref[...] = part

        @pl.when(pl.program_id(1) > 0)
        def _():
            ds_ref[...] += part

    return pl.pallas_call(
        body, name=name, grid=(L, nc),
        in_specs=[pl.BlockSpec((R, D), lambda l, c: (0, 0)), pl.BlockSpec((None, R, cb), lambda l, c: (l, 0, c)),
                  pl.BlockSpec((None, D, cb), lambda l, c: (l, 0, c))],
        out_specs=[pl.BlockSpec((None, D, cb), lambda l, c: (l, 0, c)), pl.BlockSpec((None, R, D), lambda l, c: (l, 0, 0))],
        out_shape=[jax.ShapeDtypeStruct((L, D, C), F32), jax.ShapeDtypeStruct((L, R, D), F32)],
        compiler_params=_cparams("parallel", "arbitrary"),
    )(cond, dmod, w_mod)


def _row_block(rows, cols, budget=1 << 20):
    best = None
    for t in range(16, rows + 1, 16):
        if rows % t == 0 and t * cols * 4 <= budget:
            best = t
    return best if best is not None else rows


def _sum_slots(r, name):
    S, R, C = r.shape
    tr = _row_block(R, C)

    def body(r_ref, o_ref):
        acc = r_ref[0].astype(F32)
        for s in range(1, S):
            acc = acc + r_ref[s].astype(F32)
        o_ref[...] = acc

    return pl.pallas_call(
        body, name=name, grid=(R // tr,),
        in_specs=[pl.BlockSpec((S, tr, C), lambda i: (0, i, 0))], out_specs=pl.BlockSpec((tr, C), lambda i: (i, 0)),
        out_shape=jax.ShapeDtypeStruct((R, C), F32), compiler_params=_cparams("parallel"),
    )(r)


def _adamw(w, gs, m, v, name):
    ng = len(gs)
    R, C = w.shape
    tr = _row_block(R, C)
    c1 = 1.0 / (1.0 - ADAM_B1 ** ADAM_STEP)
    c2 = 1.0 / (1.0 - ADAM_B2 ** ADAM_STEP)

    def body(w_ref, *refs):
        m_ref, v_ref, g_ref, d_ref, mo_ref, vo_ref = refs[ng:]
        g = refs[0][...]
        for g_more in refs[1:ng]:
            g = g + g_more[...]
        g_ref[...] = g
        mn = ADAM_B1 * m_ref[...] + (1.0 - ADAM_B1) * g
        vn = ADAM_B2 * v_ref[...] + (1.0 - ADAM_B2) * (g * g)
        mo_ref[...] = mn
        vo_ref[...] = vn
        d_ref[...] = -ADAM_LR * ((mn * c1) / (jnp.sqrt(vn * c2) + ADAM_EPS) + ADAM_WD * w_ref[...])

    blk = pl.BlockSpec((tr, C), lambda i: (i, 0))
    sd = jax.ShapeDtypeStruct((R, C), F32)
    return pl.pallas_call(
        body, name=name, grid=(R // tr,), in_specs=[blk] * (3 + ng), out_specs=[blk] * 4, out_shape=[sd] * 4,
        compiler_params=_cparams("parallel"),
    )(w, *gs, m, v)


def _rope_tables(dm):
    n = dm.N
    t = jnp.arange(n)
    r = (t // GRID_W).astype(F32)
    col = (t % GRID_W).astype(F32)
    nf = QK_ROPE // 4
    inv = ROPE_BASE ** (-jnp.arange(nf, dtype=F32) / nf)
    ang = jnp.stack([r[:, None] * inv, col[:, None] * inv], axis=1)
    cos, sin = jnp.cos(ang), jnp.sin(ang)
    zero = jnp.zeros_like(sin)
    c64 = jnp.stack([cos, cos], axis=2).reshape(n, QK_ROPE)
    s1 = jnp.stack([-sin, zero], axis=2).reshape(n, QK_ROPE)
    s2 = jnp.stack([zero, sin], axis=2).reshape(n, QK_ROPE)

    def pad(x, fill):
        return jnp.concatenate([jnp.full((n, QK_NOPE), fill, F32), x, jnp.full((n, HEAD_PAD - QK_HEAD), fill, F32)], axis=1)

    lat = jnp.stack([pad(c64, 1.0), pad(s1, 0.0), pad(s2, 0.0)])
    lat = jnp.tile(lat, (1, dm.B, 1))
    nctx = dm.B * dm.CTX
    ctx = jnp.stack([jnp.ones((nctx, HEAD_PAD), F32), jnp.zeros((nctx, HEAD_PAD), F32), jnp.zeros((nctx, HEAD_PAD), F32)])
    return jnp.concatenate([lat, ctx], axis=1)


def _fold_parts(part, dm):
    nblk = part.shape[0]
    nb = (dm.N * nblk) // dm.T
    groups = [part[b * nb:(b + 1) * nb].sum(axis=0) for b in range(dm.B)]
    groups.append(part[dm.B * nb:].sum(axis=0))
    return jnp.stack(groups)


def _flat2(a):
    return a.reshape(-1, a.shape[-1])


def kernel(x, c, ctx, c_ctx, w_mod, b_mod, g_norm, ffn_w1, ffn_w3, ffn_w2, sc_w_in, sc_conv, sc_w_out, mla_w_a, mla_g_qa, mla_w_uq, mla_g_kva, mla_w_ukv, mla_g_q, mla_g_k, mla_w_o, loss_target, m_c_ctx, m_w_mod, m_b_mod, m_g_norm, m_ffn_w1, m_ffn_w3, m_ffn_w2, m_sc_w_in, m_sc_conv, m_sc_w_out, m_mla_w_a, m_mla_g_qa, m_mla_w_uq, m_mla_g_kva, m_mla_w_ukv, m_mla_g_q, m_mla_g_k, m_mla_w_o, v_c_ctx, v_w_mod, v_b_mod, v_g_norm, v_ffn_w1, v_ffn_w3, v_ffn_w2, v_sc_w_in, v_sc_conv, v_sc_w_out, v_mla_w_a, v_mla_g_qa, v_mla_w_uq, v_mla_g_kva, v_mla_w_ukv, v_mla_g_q, v_mla_g_k, v_mla_w_o):
    B, N, D = x.shape
    CTX = ctx.shape[1]
    T = B * (N + CTX)
    tm = next(t for t in (512, 256, 128, 64, 32, 16) if N % t == 0 and (B * CTX) % t == 0)
    dm = Dims(B, N, CTX, D, T, tm)
    L = w_mod.shape[0]
    La, Lb = sc_w_in.shape[0], mla_w_a.shape[0]
    S = N_CHIPS
    ndev = 2 * S
    xi, yi, ci = lax.axis_index("x"), lax.axis_index("y"), lax.axis_index("c")
    chip = 2 * xi + yi
    dev = 2 * chip + ci
    weights = dict(c_ctx=c_ctx, w_mod=w_mod, b_mod=b_mod, g_norm=g_norm, ffn_w1=ffn_w1, ffn_w3=ffn_w3, ffn_w2=ffn_w2,
                   sc_w_in=sc_w_in, sc_conv=sc_conv, sc_w_out=sc_w_out, mla_w_a=mla_w_a, mla_g_qa=mla_g_qa,
                   mla_w_uq=mla_w_uq, mla_g_kva=mla_g_kva, mla_w_ukv=mla_w_ukv, mla_g_q=mla_g_q, mla_g_k=mla_g_k,
                   mla_w_o=mla_w_o)
    mom = dict(c_ctx=(m_c_ctx, v_c_ctx), w_mod=(m_w_mod, v_w_mod), b_mod=(m_b_mod, v_b_mod), g_norm=(m_g_norm, v_g_norm),
               ffn_w1=(m_ffn_w1, v_ffn_w1), ffn_w3=(m_ffn_w3, v_ffn_w3), ffn_w2=(m_ffn_w2, v_ffn_w2),
               sc_w_in=(m_sc_w_in, v_sc_w_in), sc_conv=(m_sc_conv, v_sc_conv), sc_w_out=(m_sc_w_out, v_sc_w_out),
               mla_w_a=(m_mla_w_a, v_mla_w_a), mla_g_qa=(m_mla_g_qa, v_mla_g_qa), mla_w_uq=(m_mla_w_uq, v_mla_w_uq),
               mla_g_kva=(m_mla_g_kva, v_mla_g_kva), mla_w_ukv=(m_mla_w_ukv, v_mla_w_ukv), mla_g_q=(m_mla_g_q, v_mla_g_q),
               mla_g_k=(m_mla_g_k, v_mla_g_k), mla_w_o=(m_mla_w_o, v_mla_w_o))

    big = ["ffn_w1", "ffn_w3", "ffn_w2", "sc_w_in", "sc_w_out", "mla_w_a", "mla_w_uq", "mla_w_ukv", "mla_w_o"]
    gathered = _gather_two_level([_flat2(weights[n]).astype(BF16) for n in big], "gather_weights")
    gw = {n: g.reshape((S,) + weights[n].shape) for n, g in zip(big, gathered)}
    vecs = ["g_norm", "sc_conv", "mla_g_qa"]
    gathered = _exchange([_flat2(weights[n]) for n in vecs], ("x", "y"), False, "gather_vectors")
    gw.update({n: g.reshape((S,) + weights[n].shape) for n, g in zip(vecs, gathered)})
    w1g, w3g, w2g, wing = gw["ffn_w1"], gw["ffn_w3"], gw["ffn_w2"], gw["sc_w_in"]
    wout = jnp.moveaxis(gw["sc_w_out"], 0, 1).reshape(La, D, D)
    wo = jnp.moveaxis(gw["mla_w_o"], 0, 1).reshape(Lb, HEADS * V_HEAD, D)
    wa = jnp.pad(jnp.moveaxis(gw["mla_w_a"], 0, 1).reshape(Lb, D, -1), ((0, 0), (0, 0), (0, 512 - (Q_LORA + KV_LORA + QK_ROPE))))
    wuq = jnp.moveaxis(gw["mla_w_uq"], 0, 2).reshape(Lb, Q_LORA, HEADS, QK_HEAD)
    wuq = jnp.pad(wuq, ((0, 0), (0, 0), (0, 0), (0, HEAD_PAD - QK_HEAD))).reshape(Lb, Q_LORA, HEADS * HEAD_PAD)
    wukv = jnp.moveaxis(gw["mla_w_ukv"], 0, 2).reshape(Lb, KV_LORA, HEADS * HEAD_PAD)
    gnorm = jnp.moveaxis(gw["g_norm"], 0, 2).reshape(L, 3, D)
    convw = jnp.moveaxis(gw["sc_conv"], 0, 2).reshape(La, 3, D)
    gqa = jnp.moveaxis(gw["mla_g_qa"], 0, 1).reshape(Lb, Q_LORA)
    padl = lambda a: jnp.pad(a, ((0, 0), (0, HEAD_PAD - a.shape[1])))
    gains = jnp.stack([padl(gqa), padl(mla_g_kva), padl(mla_g_q), padl(mla_g_k)], axis=1)
    gains = jnp.pad(gains, ((0, 0), (0, 4), (0, 0)))

    R = -(-(ndev * B + 1) // 16) * 16
    call = _exchange([c], ("x", "y", "c"), False, "gather_cond")[0].reshape(ndev * B, D)
    cond = jnp.concatenate([call, c_ctx[None], jnp.zeros((R - ndev * B - 1, D), F32)], axis=0)
    C = w_mod.shape[-1]
    bm = lax.dynamic_slice_in_dim(b_mod, chip * C, C, axis=1)[:, None, :]
    mshard = _mod_fwd(cond, w_mod, bm, "mod_fwd")
    mfull = _exchange([mshard.reshape(L * R, C)], ("x", "y"), False, "gather_mod")[0].reshape(S, L, R, C)
    mfull = jnp.moveaxis(mfull, 0, 2).reshape(L, R, S * C)
    mine = lax.dynamic_slice_in_dim(mfull, dev * B, B, axis=1)
    mod = jnp.concatenate([mine, mfull[:, ndev * B:ndev * B + 1]], axis=1).reshape(L, B + 1, 9, D)

    tabs = _rope_tables(dm)
    h = jnp.concatenate([x.reshape(B * N, D), ctx.reshape(B * CTX, D)], axis=0)

    saved = []
    for l in range(L):
        kind, j = l % 2, l // 2
        sv = {}
        sv["h0"] = h
        h, sv["a1"], sv["b1"], sv["hn1"], sv["y1"] = _ffn_fwd(h, mod[l], gnorm[l, 0:1], w1g, w3g, w2g, l, 0, dm, "ffn_fwd")
        sv["h1"] = h
        if kind == 0:
            sv["p"], sv["hnm"] = _sc_in_fwd(h, mod[l], gnorm[l, 1:2], wing, j, dm, "sc_in_fwd")
            sv["z"] = _conv_fwd(sv["p"], convw[j], dm, "conv_fwd")
            h, sv["ym"] = _out_fwd(sv["z"], wout, h, mod[l], j, dm, "sc_out_fwd")
        else:
            sv["hnm"], sv["q"], sv["k"], sv["v"] = _mla_proj_fwd(h, mod[l], gnorm[l, 1:2], gains, tabs, wa, wuq, wukv, j, dm, "mla_proj_fwd")
            sv["o"] = _attn_fwd(sv["q"], sv["k"], sv["v"], dm, "attn_fwd")
            h, sv["ym"] = _out_fwd(sv["o"], wo, h, mod[l], j, dm, "mla_out_fwd")
        sv["h2"] = h
        h, sv["a2"], sv["b2"], sv["hn2"], sv["y2"] = _ffn_fwd(h, mod[l], gnorm[l, 2:3], w1g, w3g, w2g, l, 1, dm, "ffn_fwd")
        saved.append(sv)

    dh, lsum = _loss_grad(h, loss_target.reshape(B * N, D), dm, "loss_grad")
    loss = lax.psum(jnp.sum(lsum[:, 0, 0]), ("x", "y", "c"))

    F = w1g.shape[-1]
    wq = D // S
    G = dict(ffn_w1=jnp.zeros((S, L, 2, D, F), BF16), ffn_w3=jnp.zeros((S, L, 2, D, F), BF16), ffn_w2=jnp.zeros((S, L, 2, F, D), BF16),
             sc_w_in=jnp.zeros((La, 3 * S, D, wq), BF16), sc_w_out=jnp.zeros((La, D, D), BF16), mla_w_o=jnp.zeros((Lb, HEADS * V_HEAD, D), BF16))
    dmod = [None] * L
    dgn = [None] * L
    dconv = [None] * La
    dgains = [None] * Lb
    gmla = dict(mla_w_a=[None] * Lb, mla_w_uq=[None] * Lb, mla_w_ukv=[None] * Lb)
    tk = tm
    nk = T // tk
    full_a = pl.BlockSpec((tk, D), lambda s, kk: (kk, 0))
    shard_b = pl.BlockSpec((None, tk, F), lambda s, kk: (s, kk, 0))

    def ffn_back(dh, sv, l, k):
        sfx = "1" if k == 0 else "2"
        dh, da, db, sw, dy, part = _ffn_bwd(dh, sv["h0" if k == 0 else "h2"], mod[l], gnorm[l, 2 * k:2 * k + 1], sv["y" + sfx],
                                            sv["a" + sfx], sv["b" + sfx], w1g, w3g, w2g, l, k, dm, "ffn_bwd")
        o5 = lambda s, kk: (s, l, k, 0, 0)
        G["ffn_w1"] = _mm_tn(sv["hn" + sfx], da, full_a, shard_b, G["ffn_w1"], pl.BlockSpec((None, None, None, D, F), o5), (S, nk), "gw1")
        G["ffn_w3"] = _mm_tn(sv["hn" + sfx], db, full_a, shard_b, G["ffn_w3"], pl.BlockSpec((None, None, None, D, F), o5), (S, nk), "gw3")
        G["ffn_w2"] = _mm_tn(sw, dy, shard_b, full_a, G["ffn_w2"], pl.BlockSpec((None, None, None, F, D), o5), (S, nk), "gw2")
        return dh, _fold_parts(part, dm)

    one = (1, nk)
    a1 = lambda kdim: pl.BlockSpec((tk, kdim), lambda s, kk: (kk, 0))
    for l in reversed(range(L)):
        kind, j = l % 2, l // 2
        sv = saved[l]
        dh, p2 = ffn_back(dh, sv, l, 1)
        if kind == 0:
            dy, dz, pg = _out_bwd(dh, sv["ym"], wout, mod[l], j, dm, "sc_out_bwd")
            G["sc_w_out"] = _mm_tn(sv["z"], dy, a1(D), a1(D), G["sc_w_out"], pl.BlockSpec((None, D, D), lambda s, kk: (j, 0, 0)), one, "gw_sc_out")
            dp, dconv[j] = _conv_bwd(dz, sv["p"], convw[j], dm, "conv_bwd")
            G["sc_w_in"] = _mm_tn(sv["hnm"], dp, pl.BlockSpec((tk, D), lambda q, kk: (kk, 0)),
                                  pl.BlockSpec((None, tk, wq), lambda q, kk: (q // S, kk, q % S)), G["sc_w_in"],
                                  pl.BlockSpec((None, None, D, wq), lambda q, kk: (j, q, 0, 0)), (3 * S, nk), "gw_sc_in")
            dh, pm = _sc_in_bwd(dh, dp, sv["h1"], mod[l], gnorm[l, 1:2], wing, j, dm, "sc_in_bwd")
        else:
            dy, do, pg = _out_bwd(dh, sv["ym"], wo, mod[l], j, dm, "mla_out_bwd")
            G["mla_w_o"] = _mm_tn(sv["o"], dy, a1(HEADS * V_HEAD), a1(D), G["mla_w_o"], pl.BlockSpec((None, HEADS * V_HEAD, D), lambda s, kk: (j, 0, 0)), one, "gw_mla_o")
            dq, dkl, dkc, dvl, dvc = _attn_bwd(sv["q"], sv["k"], sv["v"], sv["o"], do, dm, "attn_bwd")
            dk = jnp.concatenate([dkl, dkc], axis=0)
            dv = jnp.concatenate([dvl, dvc], axis=0)
            dh, pm, gmla["mla_w_a"][j], gmla["mla_w_uq"][j], gmla["mla_w_ukv"][j], dgains[j] = _mla_proj_bwd(
                dh, dq, dk, dv, sv["h1"], mod[l], gnorm[l, 1:2], gains, tabs, wa, wuq, wukv, j, dm, "mla_proj_bwd")
        pm = _fold_parts(pm, dm) + _fold_parts(pg, dm)
        dh, p0 = ffn_back(dh, sv, l, 0)
        dmod[l] = jnp.concatenate([p0[:, 0:3], pm[:, 0:3], p2[:, 0:3]], axis=1).reshape(B + 1, 9 * D)
        dgn[l] = jnp.stack([p0[:, 3].sum(0), pm[:, 3].sum(0), p2[:, 3].sum(0)])
    grad_x = dh[:B * N].reshape(B, N, D)

    Gs = dict(ffn_w1=G["ffn_w1"], ffn_w3=G["ffn_w3"], ffn_w2=G["ffn_w2"])
    Gs["sc_w_in"] = jnp.moveaxis(G["sc_w_in"].reshape(La, S, 3, D, wq), (1, 2), (0, 3)).reshape(S, La, D, 3 * wq)
    Gs["sc_w_out"] = jnp.moveaxis(G["sc_w_out"].reshape(La, S, D // S, D), 1, 0)
    Gs["mla_w_o"] = jnp.moveaxis(G["mla_w_o"].reshape(Lb, S, HEADS * V_HEAD // S, D), 1, 0)
    ga = jnp.stack(gmla["mla_w_a"])[:, :, :Q_LORA + KV_LORA + QK_ROPE]
    Gs["mla_w_a"] = jnp.moveaxis(ga.reshape(Lb, S, D // S, -1), 1, 0).astype(BF16)
    guq = jnp.stack(gmla["mla_w_uq"]).reshape(Lb, Q_LORA, HEADS, HEAD_PAD)[..., :QK_HEAD]
    Gs["mla_w_uq"] = jnp.moveaxis(guq.reshape(Lb, Q_LORA, S, -1), 2, 0).astype(BF16)
    Gs["mla_w_ukv"] = jnp.moveaxis(jnp.stack(gmla["mla_w_ukv"]).reshape(Lb, KV_LORA, S, -1), 2, 0).astype(BF16)

    parts = [Gs[n].reshape(S, -1, Gs[n].shape[-1]) for n in big]
    theirs = _swap_halves(parts, "swap_halves")
    pairs = [_pair_sum(g_, r_, ci, "pair_sum") for g_, r_ in zip(parts, theirs)]
    recv = _exchange(pairs, ("x", "y"), True, "scatter_grads")
    sums = [_sum_slots(r, "sum_slots") for r in recv]
    both = _exchange(sums, ("c",), False, "swap_cores")
    gsum = dict(zip(big, both))

    dgains_a = jnp.stack(dgains)
    small = [jnp.stack(dmod).reshape(-1), jnp.stack(dgn).reshape(-1), jnp.stack(dconv).reshape(-1), dgains_a.reshape(-1)]
    sizes = [s_.shape[0] for s_ in small]
    flat = jnp.concatenate(small)
    pad = (-flat.shape[0]) % 1024
    flat = jnp.pad(flat, (0, pad)).reshape(-1, 128)
    allsmall = _exchange([flat], ("x", "y", "c"), False, "gather_small")[0].reshape(ndev, -1)
    offs = [0]
    for s_ in sizes:
        offs.append(offs[-1] + s_)
    dmod_all = allsmall[:, offs[0]:offs[1]].reshape(ndev, L, B + 1, 9 * D)
    tot = allsmall[:, offs[1]:offs[4]].sum(axis=0)
    g_gnorm = tot[:offs[2] - offs[1]].reshape(L, 3, D)
    g_conv = tot[offs[2] - offs[1]:offs[3] - offs[1]].reshape(La, 3, D)
    g_gains = tot[offs[3] - offs[1]:].reshape(Lb, 8, HEAD_PAD)
    dM = jnp.concatenate([jnp.moveaxis(dmod_all[:, :, :B], 0, 1).reshape(L, ndev * B, 9 * D),
                          dmod_all[:, :, B].sum(axis=0)[:, None, :], jnp.zeros((L, R - ndev * B - 1, 9 * D), F32)], axis=1)
    g_bmod = dM.sum(axis=1)
    dM_mine = lax.dynamic_slice_in_dim(dM, chip * C, C, axis=2)
    g_wmod, dsil = _mod_bwd(cond, dM_mine, w_mod, "mod_bwd")
    dsil_ctx = dsil[:, ndev * B].sum(axis=0)
    dsil_all = _exchange([jnp.pad(dsil_ctx.reshape(-1, 128), ((0, (-(D // 128)) % 8), (0, 0)))], ("x", "y"), False, "gather_dctx")[0]
    dsil_tot = dsil_all.sum(axis=0)[:D // 128].reshape(D)
    sg = jax.nn.sigmoid(c_ctx)
    g_cctx = dsil_tot * (sg * (1.0 + c_ctx * (1.0 - sg)))

    chip_cols = lambda a, width: lax.dynamic_slice_in_dim(a, chip * width, width, axis=a.ndim - 1)
    small_grads = dict(
        c_ctx=g_cctx, b_mod=g_bmod, g_norm=chip_cols(g_gnorm, D // S), sc_conv=chip_cols(g_conv, D // S),
        mla_g_qa=chip_cols(g_gains[:, 0, :Q_LORA], Q_LORA // S), mla_g_kva=g_gains[:, 1, :KV_LORA],
        mla_g_q=g_gains[:, 2, :QK_HEAD], mla_g_k=g_gains[:, 3, :QK_HEAD])

    grads, deltas, new_m, new_v = {}, {}, {}, {}
    for n, w in weights.items():
        shape = w.shape
        w2 = _flat2(w) if w.ndim > 1 else w.reshape(1, -1)
        m2, v2 = (a.reshape(w2.shape) for a in mom[n])
        if n in gsum:
            gs = [gsum[n].reshape(w2.shape)]
        elif n == "w_mod":
            gs = [_flat2(g_wmod)]
        else:
            gs = [small_grads[n].reshape(w2.shape)]
        g_, d_, m_, v_ = _adamw(w2, gs, m2, v2, "adamw")
        grads[n], deltas[n], new_m[n], new_v[n] = (a.reshape(shape) for a in (g_, d_, m_, v_))

    names = list(weights)
    return (loss, grad_x, *[grads[n] for n in names], *[deltas[n] for n in names], *[new_m[n] for n in names],
            *[new_v[n] for n in names])
```

```python
import functools
import math
from typing import NamedTuple

import jax
import jax.numpy as jnp
from jax import lax
from jax.experimental import pallas as pl
from jax.experimental.pallas import tpu as pltpu

F32 = jnp.float32
BF16 = jnp.bfloat16
EPS = 1e-6
GRID_W = 64
HEADS = 8
QK_NOPE = 128
QK_ROPE = 64
QK_HEAD = QK_NOPE + QK_ROPE
HEAD_PAD = 256
V_HEAD = 128
Q_LORA = 256
KV_LORA = 128
ROPE_BASE = 10000.0
QK_SCALE = QK_HEAD ** -0.5
ADAM_LR, ADAM_B1, ADAM_B2, ADAM_EPS, ADAM_WD, ADAM_STEP = 0.001, 0.9, 0.999, 1e-08, 0.01, 10
N_CHIPS = 4
VMEM_LIMIT = 56 * 1024 * 1024
MESH = pl.DeviceIdType.MESH
NEG = -1e30


class Dims(NamedTuple):
    B: int
    N: int
    CTX: int
    D: int
    T: int
    tm: int


def _cparams(*sem):
    return pltpu.CompilerParams(dimension_semantics=sem if sem else None, vmem_limit_bytes=VMEM_LIMIT)


def _dot(a, b):
    return jnp.dot(a, b, preferred_element_type=F32)


def _dot_nt(a, b):
    return lax.dot_general(a, b, (((1,), (1,)), ((), ())), preferred_element_type=F32)


def _dot_tn(a, b):
    return lax.dot_general(a, b, (((0,), (0,)), ((), ())), preferred_element_type=F32)


def _rms(x, n):
    r = lax.rsqrt(jnp.sum(x * x, axis=-1, keepdims=True) * (1.0 / n) + EPS)
    return x * r, r


def _rms_bwd(dxh, xh, r, n):
    return r * (dxh - xh * (jnp.sum(dxh * xh, axis=-1, keepdims=True) * (1.0 / n)))


def _pre(h, g, shift, scale):
    xh, _ = _rms(h, h.shape[-1])
    return (xh * g) * (1.0 + scale) + shift


def _pre_bwd(dout, h, g, scale):
    d = h.shape[-1]
    xh, r = _rms(h, d)
    n = xh * g
    dshift = jnp.sum(dout, axis=0, keepdims=True)
    dscale = jnp.sum(dout * n, axis=0, keepdims=True)
    dn = dout * (1.0 + scale)
    dg = jnp.sum(dn * xh, axis=0, keepdims=True)
    dh = _rms_bwd(dn * g, xh, r, d)
    return dh, dshift, dscale, dg


def _write_part(part_ref, dshift=None, dscale=None, dgate=None, dg=None):
    z = jnp.zeros((1, part_ref.shape[-1]), F32)
    part_ref[0, 0:1, :] = z if dshift is None else dshift
    part_ref[0, 1:2, :] = z if dscale is None else dscale
    part_ref[0, 2:3, :] = z if dgate is None else dgate
    part_ref[0, 3:4, :] = z if dg is None else dg
    part_ref[0, 4:8, :] = jnp.zeros((4, part_ref.shape[-1]), F32)


def _grp(dm):
    nb = dm.N // dm.tm
    return lambda i: jnp.minimum(i // nb, dm.B)


def _n_chunks(rows, row_bytes):
    n = 16
    while n > 1 and (rows % (16 * n) or (rows // n) * row_bytes < (256 << 10)):
        n //= 2
    return n


def _start_local(src, dst, sems, k0, nchunk):
    ch = src.shape[0] // nchunk
    copies = []
    for j in range(nchunk):
        cp = pltpu.make_async_copy(src.at[pl.ds(j * ch, ch)], dst.at[pl.ds(j * ch, ch)], sems.at[k0 + j])
        cp.start()
        copies.append(cp)
    return copies


def _exchange(arrs, axes, scatter, name, own="copy"):
    n = len(arrs)
    nbits = len(axes)
    slots = 2 ** nbits
    pats = list(range(1, slots))
    inplace = own == "inplace"
    nck = [_n_chunks(a.shape[-2], a.shape[-1] * a.dtype.itemsize) for a in arrs]
    base = [sum(nck[:i]) * len(pats) for i in range(n)]
    nsem = sum(nck) * len(pats)

    def body(*refs):
        ins, outs = refs[:n], refs[n:2 * n]
        send, recv, loc = refs[2 * n:]
        pos = {a: lax.axis_index(a) for a in ("x", "y", "c")}

        def slot_of(p):
            s = 0
            for a in axes:
                s = 2 * s + p[a]
            return s

        me = slot_of(pos)
        local = []
        for i in range(n):
            if own == "copy":
                local += _start_local(ins[i].at[me] if scatter else ins[i], outs[i].at[me], loc, sum(nck[:i]), nck[i])
        remote = []
        for pi, pat in enumerate(pats):
            peer = dict(pos)
            for bi, a in enumerate(axes):
                if (pat >> (nbits - 1 - bi)) & 1:
                    peer[a] = 1 - pos[a]
            them = slot_of(peer)
            for i in range(n):
                ch = arrs[i].shape[-2] // nck[i]
                for j in range(nck[i]):
                    k = base[i] + pi * nck[i] + j
                    rs = pl.ds(j * ch, ch)
                    if inplace:
                        src = outs[i].at[me, rs]
                    else:
                        src = ins[i].at[them, rs] if scatter else ins[i].at[rs]
                    cp = pltpu.make_async_remote_copy(
                        src_ref=src, dst_ref=outs[i].at[me, rs], send_sem=send.at[k], recv_sem=recv.at[k],
                        device_id=(peer["x"], peer["y"], peer["c"]), device_id_type=MESH)
                    cp.start()
                    remote.append(cp)
        for cp in local:
            cp.wait()
        for cp in remote:
            cp.wait()

    out_shape = [jax.ShapeDtypeStruct(a.shape if (scatter or inplace) else (slots,) + a.shape, a.dtype) for a in arrs]
    any_spec = pl.BlockSpec(memory_space=pl.ANY)
    outs = pl.pallas_call(
        body, name=name, out_shape=out_shape, in_specs=[any_spec] * n, out_specs=[any_spec] * n,
        scratch_shapes=[pltpu.SemaphoreType.DMA((nsem,)), pltpu.SemaphoreType.DMA((nsem,)), pltpu.SemaphoreType.DMA((sum(nck),))],
        input_output_aliases={i: i for i in range(n)} if inplace else {},
        compiler_params=pltpu.CompilerParams(has_side_effects=True),
    )(*arrs)
    return list(outs)


def _gather_two_level(bufs, name):
    n = len(bufs)
    halves = [a.shape[1] // 2 for a in bufs]
    nck = [_n_chunks(h, a.shape[2] * a.dtype.itemsize) for h, a in zip(halves, bufs)]
    base = [3 * sum(nck[:i]) for i in range(n)]
    nsem = 3 * sum(nck)

    def body(*refs):
        outs = refs[n:2 * n]
        send1, recv1, send2, recv2 = refs[2 * n:]
        x, y, c = lax.axis_index("x"), lax.axis_index("y"), lax.axis_index("c")
        me = 2 * x + y
        chips = [(x, 1 - y), (1 - x, y), (1 - x, 1 - y)]

        def pieces():
            for pi, (px, py) in enumerate(chips):
                for i in range(n):
                    ch = halves[i] // nck[i]
                    for j in range(nck[i]):
                        yield base[i] + pi * nck[i] + j, px, py, 2 * px + py, i, j * ch, ch

        def rows(i, off, ch, core):
            return pl.ds(pl.multiple_of(core * halves[i] + off, 16), ch)

        first = []
        for k, px, py, them, i, off, ch in pieces():
            rs = rows(i, off, ch, c)
            mine = outs[i].at[me, rs]
            cp = pltpu.make_async_remote_copy(src_ref=mine, dst_ref=mine, send_sem=send1.at[k],
                                              recv_sem=recv1.at[k], device_id=(px, py, c), device_id_type=MESH)
            cp.start()
            first.append(cp)
        passed = []
        for k, px, py, them, i, off, ch in pieces():
            rs = rows(i, off, ch, c)
            land = outs[i].at[them, rs]
            pltpu.make_async_remote_copy(src_ref=land, dst_ref=land, send_sem=send1.at[k], recv_sem=recv1.at[k],
                                         device_id=(px, py, c), device_id_type=MESH).wait_recv()
            cp = pltpu.make_async_remote_copy(src_ref=land, dst_ref=land, send_sem=send2.at[k], recv_sem=recv2.at[k],
                                              device_id=(x, y, 1 - c), device_id_type=MESH)
            cp.start()
            passed.append(cp)
        for cp in first + passed:
            cp.wait_send()
        for k, px, py, them, i, off, ch in pieces():
            theirs = outs[i].at[them, rows(i, off, ch, 1 - c)]
            pltpu.make_async_remote_copy(src_ref=theirs, dst_ref=theirs, send_sem=send2.at[k], recv_sem=recv2.at[k],
                                         device_id=(x, y, 1 - c), device_id_type=MESH).wait_recv()

    any_spec = pl.BlockSpec(memory_space=pl.ANY)
    outs = pl.pallas_call(
        body, name=name, out_shape=[jax.ShapeDtypeStruct(a.shape, a.dtype) for a in bufs],
        in_specs=[any_spec] * n, out_specs=[any_spec] * n,
        scratch_shapes=[pltpu.SemaphoreType.DMA((nsem,))] * 4,
        input_output_aliases={i: i for i in range(n)},
        compiler_params=pltpu.CompilerParams(has_side_effects=True),
    )(*bufs)
    return list(outs)


def _place_cast(w, slot, slots, name):
    rows, C = w.shape
    tr = _row_block(rows, C)

    def body(slot_ref, w_ref, o_ref):
        del slot_ref
        o_ref[...] = w_ref[...].astype(BF16)

    return pl.pallas_call(
        body, name=name,
        grid_spec=pltpu.PrefetchScalarGridSpec(
            num_scalar_prefetch=1, grid=(rows // tr,), in_specs=[pl.BlockSpec((tr, C), lambda i, sr: (i, 0))],
            out_specs=pl.BlockSpec((None, tr, C), lambda i, sr: (sr[0], i, 0))),
        out_shape=jax.ShapeDtypeStruct((slots, rows, C), BF16),
        compiler_params=_cparams("parallel"),
    )(slot.reshape(1).astype(jnp.int32), w)


def _swap_halves(arrs, name):
    n = len(arrs)
    S = arrs[0].shape[0]
    halves = [a.shape[1] // 2 for a in arrs]
    nck = [_n_chunks(h, a.shape[2] * a.dtype.itemsize) for h, a in zip(halves, arrs)]
    base = [S * sum(nck[:i]) for i in range(n)]
    nsem = S * sum(nck)

    def body(*refs):
        ins, outs = refs[:n], refs[n:2 * n]
        send, recv = refs[2 * n:]
        x, y, c = lax.axis_index("x"), lax.axis_index("y"), lax.axis_index("c")
        copies = []
        for i in range(n):
            ch = halves[i] // nck[i]
            for s in range(S):
                for j in range(nck[i]):
                    k = base[i] + s * nck[i] + j
                    src = ins[i].at[s, pl.ds(pl.multiple_of((1 - c) * halves[i] + j * ch, 16), ch)]
                    cp = pltpu.make_async_remote_copy(src_ref=src, dst_ref=outs[i].at[s, pl.ds(j * ch, ch)], send_sem=send.at[k],
                                                      recv_sem=recv.at[k], device_id=(x, y, 1 - c), device_id_type=MESH)
                    cp.start()
                    copies.append(cp)
        for cp in copies:
            cp.wait()

    any_spec = pl.BlockSpec(memory_space=pl.ANY)
    outs = pl.pallas_call(
        body, name=name, out_shape=[jax.ShapeDtypeStruct((S, h, a.shape[2]), a.dtype) for h, a in zip(halves, arrs)],
        in_specs=[any_spec] * n, out_specs=[any_spec] * n,
        scratch_shapes=[pltpu.SemaphoreType.DMA((nsem,))] * 2,
        compiler_params=pltpu.CompilerParams(has_side_effects=True),
    )(*arrs)
    return list(outs)


def _pair_sum(g, r, core, name):
    S, rows, C = g.shape
    half = rows // 2
    tr = _row_block(half, C)
    nb = half // tr

    def body(core_ref, g_ref, r_ref, o_ref):
        del core_ref
        o_ref[...] = (g_ref[...].astype(F32) + r_ref[...].astype(F32)).astype(BF16)

    blk = pl.BlockSpec((None, tr, C), lambda s, i, cr: (s, i, 0))
    return pl.pallas_call(
        body, name=name,
        grid_spec=pltpu.PrefetchScalarGridSpec(
            num_scalar_prefetch=1, grid=(S, nb),
            in_specs=[pl.BlockSpec((None, tr, C), lambda s, i, cr: (s, cr[0] * nb + i, 0)), blk], out_specs=blk),
        out_shape=jax.ShapeDtypeStruct((S, half, C), BF16),
        compiler_params=_cparams("parallel", "parallel"),
    )(core.reshape(1).astype(jnp.int32), g, r)


def _ffn_fwd(h, mod, g, w1, w3, w2, l, k, dm, name):
    T, D = h.shape
    S, F = w1.shape[0], w1.shape[-1]
    tm = dm.tm
    r0 = 6 if k else 0
    grp = _grp(dm)

    def body(h_ref, mod_ref, g_ref, w1_ref, w3_ref, w2_ref, ho_ref, a_ref, b_ref, hn_ref, y_ref, hn_s, acc):
        s = pl.program_id(1)

        @pl.when(s == 0)
        def _():
            hn = _pre(h_ref[...], g_ref[...], mod_ref[0, r0:r0 + 1, :], mod_ref[0, r0 + 1:r0 + 2, :]).astype(BF16)
            hn_s[...] = hn
            hn_ref[...] = hn
            acc[...] = jnp.zeros_like(acc)

        hn = hn_s[...]
        a = _dot(hn, w1_ref[...])
        b = _dot(hn, w3_ref[...])
        a_ref[0] = a.astype(BF16)
        b_ref[0] = b.astype(BF16)
        sw = (a * jax.nn.sigmoid(a) * b).astype(BF16)
        acc[...] += _dot(sw, w2_ref[...])

        @pl.when(s == S - 1)
        def _():
            y = acc[...]
            y_ref[...] = y.astype(BF16)
            ho_ref[...] = h_ref[...] + 0.5 * mod_ref[0, r0 + 2:r0 + 3, :] * y

    row = pl.BlockSpec((tm, D), lambda i, s: (i, 0))
    wcol = pl.BlockSpec((None, None, None, D, F), lambda i, s: (s, l, k, 0, 0))
    wrow = pl.BlockSpec((None, None, None, F, D), lambda i, s: (s, l, k, 0, 0))
    ab = pl.BlockSpec((1, tm, F), lambda i, s: (s, i, 0))
    return pl.pallas_call(
        body, name=name, grid=(T // tm, S),
        in_specs=[row, pl.BlockSpec((1, 9, D), lambda i, s: (grp(i), 0, 0)), pl.BlockSpec((1, D), lambda i, s: (0, 0)),
                  wcol, wcol, wrow],
        out_specs=[row, ab, ab, row, row],
        out_shape=[jax.ShapeDtypeStruct((T, D), F32), jax.ShapeDtypeStruct((S, T, F), BF16),
                   jax.ShapeDtypeStruct((S, T, F), BF16), jax.ShapeDtypeStruct((T, D), BF16),
                   jax.ShapeDtypeStruct((T, D), BF16)],
        scratch_shapes=[pltpu.VMEM((tm, D), BF16), pltpu.VMEM((tm, D), F32)],
        compiler_params=_cparams("parallel", "arbitrary"),
    )(h, mod, g, w1, w3, w2)


def _ffn_bwd(dh, h, mod, g, y, a, b, w1, w3, w2, l, k, dm, name):
    T, D = h.shape
    S, F = w1.shape[0], w1.shape[-1]
    tm = dm.tm
    r0 = 6 if k else 0
    grp = _grp(dm)

    def body(dh_ref, h_ref, mod_ref, g_ref, y_ref, a_ref, b_ref, w1_ref, w3_ref, w2_ref,
             dho_ref, da_ref, db_ref, sw_ref, dy_ref, part_ref, dy_s, acc):
        s = pl.program_id(1)

        @pl.when(s == 0)
        def _():
            dy = (0.5 * mod_ref[0, r0 + 2:r0 + 3, :] * dh_ref[...]).astype(BF16)
            dy_s[...] = dy
            dy_ref[...] = dy
            acc[...] = jnp.zeros_like(acc)

        ds = _dot_nt(dy_s[...], w2_ref[...])
        av = a_ref[0].astype(F32)
        bv = b_ref[0].astype(F32)
        sig = jax.nn.sigmoid(av)
        sil = av * sig
        sw_ref[0] = (sil * bv).astype(BF16)
        db = (ds * sil).astype(BF16)
        da = (ds * bv * (sig * (1.0 + av * (1.0 - sig)))).astype(BF16)
        da_ref[0] = da
        db_ref[0] = db
        acc[...] += _dot_nt(da, w1_ref[...]) + _dot_nt(db, w3_ref[...])

        @pl.when(s == S - 1)
        def _():
            dhv = dh_ref[...]
            dhb, dshift, dscale, dg = _pre_bwd(acc[...], h_ref[...], g_ref[...], mod_ref[0, r0 + 1:r0 + 2, :])
            dho_ref[...] = dhv + dhb
            dgate = 0.5 * jnp.sum(dhv * y_ref[...].astype(F32), axis=0, keepdims=True)
            _write_part(part_ref, dshift, dscale, dgate, dg)

    row = pl.BlockSpec((tm, D), lambda i, s: (i, 0))
    wcol = pl.BlockSpec((None, None, None, D, F), lambda i, s: (s, l, k, 0, 0))
    wrow = pl.BlockSpec((None, None, None, F, D), lambda i, s: (s, l, k, 0, 0))
    ab = pl.BlockSpec((1, tm, F), lambda i, s: (s, i, 0))
    stf = jax.ShapeDtypeStruct((S, T, F), BF16)
    return pl.pallas_call(
        body, name=name, grid=(T // tm, S),
        in_specs=[row, row, pl.BlockSpec((1, 9, D), lambda i, s: (grp(i), 0, 0)), pl.BlockSpec((1, D), lambda i, s: (0, 0)),
                  row, ab, ab, wcol, wcol, wrow],
        out_specs=[row, ab, ab, ab, row, pl.BlockSpec((1, 8, D), lambda i, s: (i, 0, 0))],
        out_shape=[jax.ShapeDtypeStruct((T, D), F32), stf, stf, stf, jax.ShapeDtypeStruct((T, D), BF16),
                   jax.ShapeDtypeStruct((T // tm, 8, D), F32)],
        scratch_shapes=[pltpu.VMEM((tm, D), BF16), pltpu.VMEM((tm, D), F32)],
        compiler_params=_cparams("parallel", "arbitrary"),
    )(dh, h, mod, g, y, a, b, w1, w3, w2)


def _mm_tn(a, b, a_spec, b_spec, out, out_spec, grid, name):
    nk = grid[-1]
    kax = len(grid) - 1
    blk = tuple(d for d in out_spec.block_shape if d is not None)

    def body(a_ref, b_ref, o_in, o_ref, acc):
        del o_in
        kk = pl.program_id(kax)

        @pl.when(kk == 0)
        def _():
            acc[...] = jnp.zeros_like(acc)

        acc[...] += _dot_tn(a_ref[...].astype(BF16), b_ref[...].astype(BF16))

        @pl.when(kk == nk - 1)
        def _():
            o_ref[...] = acc[...].astype(o_ref.dtype)

    return pl.pallas_call(
        body, name=name, grid=grid,
        in_specs=[a_spec, b_spec, pl.BlockSpec(memory_space=pl.ANY)], out_specs=out_spec,
        out_shape=jax.ShapeDtypeStruct(out.shape, out.dtype),
        scratch_shapes=[pltpu.VMEM(blk, F32)], input_output_aliases={2: 0},
        compiler_params=_cparams(*(["parallel"] * kax + ["arbitrary"])),
    )(a, b, out)


def _sc_in_fwd(h, mod, g, w_in, j, dm, name):
    T, D = h.shape
    tm = dm.tm
    wq = D // N_CHIPS
    nq = 3 * N_CHIPS
    grp = _grp(dm)

    def body(h_ref, mod_ref, g_ref, w_ref, p_ref, hn_ref, hn_s):
        @pl.when(pl.program_id(1) == 0)
        def _():
            hn = _pre(h_ref[...], g_ref[...], mod_ref[0, 3:4, :], mod_ref[0, 4:5, :]).astype(BF16)
            hn_s[...] = hn
            hn_ref[...] = hn

        p_ref[...] = _dot(hn_s[...], w_ref[...])

    row = pl.BlockSpec((tm, D), lambda i, q: (i, 0))
    return pl.pallas_call(
        body, name=name, grid=(T // tm, nq),
        in_specs=[row, pl.BlockSpec((1, 9, D), lambda i, q: (grp(i), 0, 0)), pl.BlockSpec((1, D), lambda i, q: (0, 0)),
                  pl.BlockSpec((None, None, D, wq), lambda i, q: (q // 3, j, 0, q % 3))],
        out_specs=[pl.BlockSpec((None, tm, wq), lambda i, q: (q // N_CHIPS, i, q % N_CHIPS)), row],
        out_shape=[jax.ShapeDtypeStruct((3, T, D), F32), jax.ShapeDtypeStruct((T, D), BF16)],
        scratch_shapes=[pltpu.VMEM((tm, D), BF16)],
        compiler_params=_cparams("parallel", "arbitrary"),
    )(h, mod, g, w_in)


def _conv_cols(dm):
    return 256 if dm.D % 256 == 0 else 128


def _seg_masks(r, dm):
    bn = dm.B * dm.N
    lat = r < bn
    off = jnp.where(lat, lax.rem(r, dm.N), lax.rem(r - bn, dm.CTX))
    seg = jnp.where(lat, dm.N, dm.CTX)
    inside = (r >= 0) & (r < dm.T)
    return ((off != 0) & inside).astype(F32), ((off != seg - 1) & inside).astype(F32)


def _conv_specs(dm):
    tb, cb, nr8 = dm.tm, _conv_cols(dm), dm.T // 8
    prev8 = lambda c, i: jnp.maximum(i * (tb // 8) - 1, 0)
    next8 = lambda c, i: jnp.minimum((i + 1) * (tb // 8), nr8 - 1)
    return dict(
        tb=tb, cb=cb,
        p=pl.BlockSpec((3, tb, cb), lambda c, i: (0, i, c)),
        p_prev=pl.BlockSpec((3, 8, cb), lambda c, i: (0, prev8(c, i), c)),
        p_next=pl.BlockSpec((3, 8, cb), lambda c, i: (0, next8(c, i), c)),
        row=pl.BlockSpec((tb, cb), lambda c, i: (i, c)),
        row_prev=pl.BlockSpec((8, cb), lambda c, i: (prev8(c, i), c)),
        row_next=pl.BlockSpec((8, cb), lambda c, i: (next8(c, i), c)),
        w=pl.BlockSpec((3, cb), lambda c, i: (0, c)),
    )


def _shift_rows(x, before, after, tb):
    rid = lax.broadcasted_iota(jnp.int32, x.shape, 0)
    down = jnp.where(rid == 0, before, pltpu.roll(x, 1, 0))
    up = jnp.where(rid == tb - 1, after, pltpu.roll(x, tb - 1, 0))
    return down, up


def _conv_fwd(p, wc, dm, name):
    T, D = dm.T, dm.D
    sp = _conv_specs(dm)
    tb, cb = sp["tb"], sp["cb"]

    def body(p_ref, pp_ref, pn_ref, w_ref, z_ref):
        r = pl.program_id(1) * tb + lax.broadcasted_iota(jnp.int32, (tb, cb), 0)
        mp, mn = _seg_masks(r, dm)
        cu = p_ref[1] * p_ref[2]
        prev, nxt = _shift_rows(cu, pp_ref[1, 7:8, :] * pp_ref[2, 7:8, :], pn_ref[1, 0:1, :] * pn_ref[2, 0:1, :], tb)
        conv = w_ref[0:1, :] * (prev * mp) + w_ref[1:2, :] * cu + w_ref[2:3, :] * (nxt * mn)
        z_ref[...] = (p_ref[0] * conv).astype(BF16)

    return pl.pallas_call(
        body, name=name, grid=(D // cb, T // tb),
        in_specs=[sp["p"], sp["p_prev"], sp["p_next"], sp["w"]], out_specs=sp["row"],
        out_shape=jax.ShapeDtypeStruct((T, D), BF16),
        compiler_params=_cparams("parallel", "parallel"),
    )(p, p, p, wc)


def _conv_bwd(dz, p, wc, dm, name):
    T, D = dm.T, dm.D
    sp = _conv_specs(dm)
    tb, cb = sp["tb"], sp["cb"]

    def body(dz_ref, dzp_ref, dzn_ref, p_ref, pp_ref, pn_ref, w_ref, dp_ref, dw_ref):
        i = pl.program_id(1)
        r = i * tb + lax.broadcasted_iota(jnp.int32, (tb, cb), 0)
        mp, mn = _seg_masks(r, dm)
        rb = i * tb + lax.broadcasted_iota(jnp.int32, (1, cb), 0)
        _, mn_before = _seg_masks(rb - 1, dm)
        mp_after, _ = _seg_masks(rb + tb, dm)
        bg, cg, u = p_ref[0], p_ref[1], p_ref[2]
        cu = cg * u
        prev, nxt = _shift_rows(cu, pp_ref[1, 7:8, :] * pp_ref[2, 7:8, :], pn_ref[1, 0:1, :] * pn_ref[2, 0:1, :], tb)
        prev = prev * mp
        nxt = nxt * mn
        w0, w1, w2 = w_ref[0:1, :], w_ref[1:2, :], w_ref[2:3, :]
        conv = w0 * prev + w1 * cu + w2 * nxt
        dz = dz_ref[...]
        dp_ref[0] = dz * conv
        dconv = dz * bg

        @pl.when(i == 0)
        def _():
            dw_ref[...] = jnp.zeros_like(dw_ref)

        dw_ref[0:1, :] += jnp.sum(dconv * prev, axis=0, keepdims=True)
        dw_ref[1:2, :] += jnp.sum(dconv * cu, axis=0, keepdims=True)
        dw_ref[2:3, :] += jnp.sum(dconv * nxt, axis=0, keepdims=True)
        dconv_before = dzp_ref[7:8, :] * pp_ref[0, 7:8, :] * mn_before
        dconv_after = dzn_ref[0:1, :] * pn_ref[0, 0:1, :] * mp_after
        from_prev, _ = _shift_rows(dconv * mn, dconv_before, dconv_after, tb)
        _, from_next = _shift_rows(dconv * mp, dconv_before, dconv_after, tb)
        dcu = w1 * dconv + w0 * from_next + w2 * from_prev
        dp_ref[1] = dcu * u
        dp_ref[2] = dcu * cg

    return pl.pallas_call(
        body, name=name, grid=(D // cb, T // tb),
        in_specs=[sp["row"], sp["row_prev"], sp["row_next"], sp["p"], sp["p_prev"], sp["p_next"], sp["w"]],
        out_specs=[sp["p"], sp["w"]],
        out_shape=[jax.ShapeDtypeStruct((3, T, D), F32), jax.ShapeDtypeStruct((3, D), F32)],
        compiler_params=_cparams("parallel", "arbitrary"),
    )(dz, dz, dz, p, p, p, wc)


def _out_fwd(z, w, h, mod, j, dm, name):
    T, D = h.shape
    K = z.shape[1]
    tm = dm.tm
    grp = _grp(dm)

    def body(z_ref, w_ref, h_ref, mod_ref, ho_ref, y_ref):
        y = _dot(z_ref[...], w_ref[...])
        y_ref[...] = y.astype(BF16)
        ho_ref[...] = h_ref[...] + mod_ref[0, 5:6, :] * y

    row = pl.BlockSpec((tm, D), lambda i: (i, 0))
    return pl.pallas_call(
        body, name=name, grid=(T // tm,),
        in_specs=[pl.BlockSpec((tm, K), lambda i: (i, 0)), pl.BlockSpec((None, K, D), lambda i: (j, 0, 0)), row,
                  pl.BlockSpec((1, 9, D), lambda i: (grp(i), 0, 0))],
        out_specs=[row, row],
        out_shape=[jax.ShapeDtypeStruct((T, D), F32), jax.ShapeDtypeStruct((T, D), BF16)],
        compiler_params=_cparams("parallel"),
    )(z, w, h, mod)


def _out_bwd(dh, y, w, mod, j, dm, name):
    T, D = dh.shape
    K = w.shape[1]
    tm = dm.tm
    grp = _grp(dm)

    def body(dh_ref, y_ref, w_ref, mod_ref, dy_ref, dz_ref, part_ref):
        dhv = dh_ref[...]
        dy = (mod_ref[0, 5:6, :] * dhv).astype(BF16)
        dy_ref[...] = dy
        dz_ref[...] = _dot_nt(dy, w_ref[...])
        _write_part(part_ref, dgate=jnp.sum(dhv * y_ref[...].astype(F32), axis=0, keepdims=True))

    row = pl.BlockSpec((tm, D), lambda i: (i, 0))
    return pl.pallas_call(
        body, name=name, grid=(T // tm,),
        in_specs=[row, row, pl.BlockSpec((None, K, D), lambda i: (j, 0, 0)), pl.BlockSpec((1, 9, D), lambda i: (grp(i), 0, 0))],
        out_specs=[row, pl.BlockSpec((tm, K), lambda i: (i, 0)), pl.BlockSpec((1, 8, D), lambda i: (i, 0, 0))],
        out_shape=[jax.ShapeDtypeStruct((T, D), BF16), jax.ShapeDtypeStruct((T, K), F32),
                   jax.ShapeDtypeStruct((T // tm, 8, D), F32)],
        compiler_params=_cparams("parallel"),
    )(dh, y, w, mod)


def _sc_in_bwd(dh, dp, h, mod, g, w_in, j, dm, name):
    T, D = h.shape
    tm = dm.tm
    wq = D // N_CHIPS
    nq = 3 * N_CHIPS
    grp = _grp(dm)

    def body(dh_ref, dp_ref, h_ref, mod_ref, g_ref, w_ref, dho_ref, part_ref, acc):
        q = pl.program_id(1)

        @pl.when(q == 0)
        def _():
            acc[...] = jnp.zeros_like(acc)

        acc[...] += _dot_nt(dp_ref[...].astype(BF16), w_ref[...])

        @pl.when(q == nq - 1)
        def _():
            dhb, dshift, dscale, dg = _pre_bwd(acc[...], h_ref[...], g_ref[...], mod_ref[0, 4:5, :])
            dho_ref[...] = dh_ref[...] + dhb
            _write_part(part_ref, dshift, dscale, None, dg)

    row = pl.BlockSpec((tm, D), lambda i, q: (i, 0))
    return pl.pallas_call(
        body, name=name, grid=(T // tm, nq),
        in_specs=[row, pl.BlockSpec((None, tm, wq), lambda i, q: (q // N_CHIPS, i, q % N_CHIPS)), row,
                  pl.BlockSpec((1, 9, D), lambda i, q: (grp(i), 0, 0)), pl.BlockSpec((1, D), lambda i, q: (0, 0)),
                  pl.BlockSpec((None, None, D, wq), lambda i, q: (q // 3, j, 0, q % 3))],
        out_specs=[row, pl.BlockSpec((1, 8, D), lambda i, q: (i, 0, 0))],
        out_shape=[jax.ShapeDtypeStruct((T, D), F32), jax.ShapeDtypeStruct((T // tm, 8, D), F32)],
        scratch_shapes=[pltpu.VMEM((tm, D), F32)],
        compiler_params=_cparams("parallel", "arbitrary"),
    )(dh, dp, h, mod, g, w_in)


def _rope(t, c, s1, s2):
    return t * c + pltpu.roll(t, HEAD_PAD - 16, 1) * s1 + pltpu.roll(t, 16, 1) * s2


def _rope_t(dy, c, s1, s2):
    return dy * c + pltpu.roll(dy * s1, 16, 1) + pltpu.roll(dy * s2, HEAD_PAD - 16, 1)


def _mla_heads_fwd(z, g_ref, wuq_ref, wukv_ref):
    cq, ckv, krp = z[:, :Q_LORA], z[:, Q_LORA:Q_LORA + KV_LORA], z[:, Q_LORA + KV_LORA:]
    cqh, rq = _rms(cq, Q_LORA)
    ckvh, rkv = _rms(ckv, KV_LORA)
    cqn = (cqh * g_ref[0:1, :]).astype(BF16)
    ckvn = (ckvh * g_ref[1:2, :KV_LORA]).astype(BF16)
    qraw = _dot(cqn, wuq_ref[...])
    kvraw = _dot(ckvn, wukv_ref[...])
    return dict(krp=krp, cqh=cqh, rq=rq, ckvh=ckvh, rkv=rkv, cqn=cqn, ckvn=ckvn, qraw=qraw, kvraw=kvraw)


def _mla_proj_fwd(h, mod, g, gains, tabs, w_a, w_uq, w_ukv, j, dm, name):
    T, D = h.shape
    tm = min(dm.tm, 256)
    grp = lambda i: jnp.minimum(i // (dm.N // tm), dm.B)
    HP = HEAD_PAD

    def body(h_ref, mod_ref, g_ref, gn_ref, tab_ref, wa_ref, wuq_ref, wukv_ref, hn_ref, q_ref, k_ref, v_ref):
        hn = _pre(h_ref[...], g_ref[...], mod_ref[0, 3:4, :], mod_ref[0, 4:5, :]).astype(BF16)
        hn_ref[...] = hn
        f = _mla_heads_fwd(_dot(hn, wa_ref[...]), gn_ref, wuq_ref, wukv_ref)
        c, s1, s2 = tab_ref[0], tab_ref[1], tab_ref[2]
        for hd in range(HEADS):
            qh, _ = _rms(f["qraw"][:, hd * HP:(hd + 1) * HP], QK_HEAD)
            q_ref[:, hd * HP:(hd + 1) * HP] = _rope(qh * gn_ref[2:3, :], c, s1, s2).astype(BF16)
            kpre = jnp.concatenate([f["kvraw"][:, hd * HP:hd * HP + QK_NOPE], f["krp"]], axis=1)
            kh, _ = _rms(kpre, QK_HEAD)
            k_ref[:, hd * HP:(hd + 1) * HP] = _rope(kh * gn_ref[3:4, :], c, s1, s2).astype(BF16)
            v_ref[:, hd * V_HEAD:(hd + 1) * V_HEAD] = f["kvraw"][:, hd * HP + QK_NOPE:(hd + 1) * HP].astype(BF16)

    row = pl.BlockSpec((tm, D), lambda i: (i, 0))
    HQ = HEADS * HP
    return pl.pallas_call(
        body, name=name, grid=(T // tm,),
        in_specs=[row, pl.BlockSpec((1, 9, D), lambda i: (grp(i), 0, 0)), pl.BlockSpec((1, D), lambda i: (0, 0)),
                  pl.BlockSpec((None, 8, HP), lambda i: (j, 0, 0)), pl.BlockSpec((3, tm, HP), lambda i: (0, i, 0)),
                  pl.BlockSpec((None, D, 512), lambda i: (j, 0, 0)), pl.BlockSpec((None, Q_LORA, HQ), lambda i: (j, 0, 0)),
                  pl.BlockSpec((None, KV_LORA, HQ), lambda i: (j, 0, 0))],
        out_specs=[row, pl.BlockSpec((tm, HQ), lambda i: (i, 0)), pl.BlockSpec((tm, HQ), lambda i: (i, 0)),
                   pl.BlockSpec((tm, HEADS * V_HEAD), lambda i: (i, 0))],
        out_shape=[jax.ShapeDtypeStruct((T, D), BF16), jax.ShapeDtypeStruct((T, HQ), BF16),
                   jax.ShapeDtypeStruct((T, HQ), BF16), jax.ShapeDtypeStruct((T, HEADS * V_HEAD), BF16)],
        compiler_params=_cparams("parallel"),
    )(h, mod, g, gains, tabs, w_a, w_uq, w_ukv)


def _mla_proj_bwd(dh, dq, dk, dv, h, mod, g, gains, tabs, w_a, w_uq, w_ukv, j, dm, name):
    T, D = h.shape
    tm = min(dm.tm, 256)
    nblk = T // tm
    grp = lambda i: jnp.minimum(i // (dm.N // tm), dm.B)
    HP = HEAD_PAD
    HQ = HEADS * HP

    def body(dh_ref, dq_ref, dk_ref, dv_ref, h_ref, mod_ref, g_ref, gn_ref, tab_ref, wa_ref, wuq_ref, wukv_ref,
             dho_ref, part_ref, gwa_ref, gwuq_ref, gwukv_ref, dgn_ref, dqraw_s, dkvraw_s):
        i = pl.program_id(0)

        @pl.when(i == 0)
        def _():
            gwa_ref[...] = jnp.zeros_like(gwa_ref)
            gwuq_ref[...] = jnp.zeros_like(gwuq_ref)
            gwukv_ref[...] = jnp.zeros_like(gwukv_ref)
            dgn_ref[...] = jnp.zeros_like(dgn_ref)

        hv = h_ref[...]
        hn = _pre(hv, g_ref[...], mod_ref[0, 3:4, :], mod_ref[0, 4:5, :]).astype(BF16)
        f = _mla_heads_fwd(_dot(hn, wa_ref[...]), gn_ref, wuq_ref, wukv_ref)
        c, s1, s2 = tab_ref[0], tab_ref[1], tab_ref[2]
        gq, gk = gn_ref[2:3, :], gn_ref[3:4, :]
        dgq = jnp.zeros((1, HP), F32)
        dgk = jnp.zeros((1, HP), F32)
        dkrp = jnp.zeros((tm, HP - QK_NOPE), F32)
        for hd in range(HEADS):
            qh, rq = _rms(f["qraw"][:, hd * HP:(hd + 1) * HP], QK_HEAD)
            dqn = _rope_t(dq_ref[:, hd * HP:(hd + 1) * HP], c, s1, s2)
            dgq = dgq + jnp.sum(dqn * qh, axis=0, keepdims=True)
            dqraw_s[:, hd * HP:(hd + 1) * HP] = _rms_bwd(dqn * gq, qh, rq, QK_HEAD)
            kpre = jnp.concatenate([f["kvraw"][:, hd * HP:hd * HP + QK_NOPE], f["krp"]], axis=1)
            kh, rk = _rms(kpre, QK_HEAD)
            dkn = _rope_t(dk_ref[:, hd * HP:(hd + 1) * HP], c, s1, s2)
            dgk = dgk + jnp.sum(dkn * kh, axis=0, keepdims=True)
            dkpre = _rms_bwd(dkn * gk, kh, rk, QK_HEAD)
            dkvraw_s[:, hd * HP:hd * HP + QK_NOPE] = dkpre[:, :QK_NOPE]
            dkrp = dkrp + dkpre[:, QK_NOPE:]
            dkvraw_s[:, hd * HP + QK_NOPE:(hd + 1) * HP] = dv_ref[:, hd * V_HEAD:(hd + 1) * V_HEAD]
        dqraw = dqraw_s[...].astype(BF16)
        dkvraw = dkvraw_s[...].astype(BF16)
        gwuq_ref[...] += _dot_tn(f["cqn"], dqraw)
        gwukv_ref[...] += _dot_tn(f["ckvn"], dkvraw)
        dcqn = _dot_nt(dqraw, wuq_ref[...])
        dckvn = _dot_nt(dkvraw, wukv_ref[...])
        dgqa = jnp.sum(dcqn * f["cqh"], axis=0, keepdims=True)
        dgkva = jnp.sum(dckvn * f["ckvh"], axis=0, keepdims=True)
        dcq = _rms_bwd(dcqn * gn_ref[0:1, :], f["cqh"], f["rq"], Q_LORA)
        dckv = _rms_bwd(dckvn * gn_ref[1:2, :KV_LORA], f["ckvh"], f["rkv"], KV_LORA)
        dz = jnp.concatenate([dcq, dckv, dkrp], axis=1).astype(BF16)
        gwa_ref[...] += _dot_tn(hn, dz)
        dhn = _dot_nt(dz, wa_ref[...])
        dhb, dshift, dscale, dg = _pre_bwd(dhn, hv, g_ref[...], mod_ref[0, 4:5, :])
        dho_ref[...] = dh_ref[...] + dhb
        _write_part(part_ref, dshift, dscale, None, dg)
        dgn_ref[0:1, :] += dgqa
        dgn_ref[1:2, :KV_LORA] += dgkva
        dgn_ref[2:3, :] += dgq
        dgn_ref[3:4, :] += dgk

    row = pl.BlockSpec((tm, D), lambda i: (i, 0))
    wide = pl.BlockSpec((tm, HQ), lambda i: (i, 0))
    const2 = lambda i: (0, 0)
    return pl.pallas_call(
        body, name=name, grid=(nblk,),
        in_specs=[row, wide, wide, pl.BlockSpec((tm, HEADS * V_HEAD), lambda i: (i, 0)), row,
                  pl.BlockSpec((1, 9, D), lambda i: (grp(i), 0, 0)), pl.BlockSpec((1, D), const2),
                  pl.BlockSpec((None, 8, HP), lambda i: (j, 0, 0)), pl.BlockSpec((3, tm, HP), lambda i: (0, i, 0)),
                  pl.BlockSpec((None, D, 512), lambda i: (j, 0, 0)), pl.BlockSpec((None, Q_LORA, HQ), lambda i: (j, 0, 0)),
                  pl.BlockSpec((None, KV_LORA, HQ), lambda i: (j, 0, 0))],
        out_specs=[row, pl.BlockSpec((1, 8, D), lambda i: (i, 0, 0)), pl.BlockSpec((D, 512), const2),
                   pl.BlockSpec((Q_LORA, HQ), const2), pl.BlockSpec((KV_LORA, HQ), const2), pl.BlockSpec((8, HP), const2)],
        out_shape=[jax.ShapeDtypeStruct((T, D), F32), jax.ShapeDtypeStruct((nblk, 8, D), F32),
                   jax.ShapeDtypeStruct((D, 512), F32), jax.ShapeDtypeStruct((Q_LORA, HQ), F32),
                   jax.ShapeDtypeStruct((KV_LORA, HQ), F32), jax.ShapeDtypeStruct((8, HP), F32)],
        scratch_shapes=[pltpu.VMEM((tm, HQ), F32), pltpu.VMEM((tm, HQ), F32)],
        compiler_params=_cparams("arbitrary"),
    )(dh, dq, dk, dv, h, mod, g, gains, tabs, w_a, w_uq, w_ukv)


def _attn_specs(dm):
    tq = dm.CTX
    nq = dm.N // tq
    cblk0 = dm.B * nq
    HP = HEAD_PAD
    qrow = lambda b, i: jnp.where(i < nq, b * nq + i, cblk0 + b)
    return dict(
        tq=tq, nq=nq,
        q=pl.BlockSpec((tq, HP), lambda b, hd, i: (qrow(b, i), hd)),
        k_lat=pl.BlockSpec((dm.N, HP), lambda b, hd, i: (b, hd)),
        k_ctx=pl.BlockSpec((tq, HP), lambda b, hd, i: (cblk0 + b, hd)),
        v_lat=pl.BlockSpec((dm.N, V_HEAD), lambda b, hd, i: (b, hd)),
        v_ctx=pl.BlockSpec((tq, V_HEAD), lambda b, hd, i: (cblk0 + b, hd)),
        o=pl.BlockSpec((tq, V_HEAD), lambda b, hd, i: (qrow(b, i), hd)),
    )


def _attn_probs(q, kl, kc, is_ctx):
    sl = _dot_nt(q, kl) * QK_SCALE
    sc = _dot_nt(q, kc) * QK_SCALE
    sl = sl + jnp.where(is_ctx, NEG, 0.0)
    m = jnp.maximum(jnp.max(sl, axis=-1, keepdims=True), jnp.max(sc, axis=-1, keepdims=True))
    pl_, pc = jnp.exp(sl - m), jnp.exp(sc - m)
    inv = 1.0 / (jnp.sum(pl_, axis=-1, keepdims=True) + jnp.sum(pc, axis=-1, keepdims=True))
    return pl_ * inv, pc * inv


def _attn_fwd(q, k, v, dm, name):
    T = dm.T
    sp = _attn_specs(dm)
    nq = sp["nq"]

    def body(q_ref, kl_ref, kc_ref, vl_ref, vc_ref, o_ref):
        is_ctx = pl.program_id(2) == nq
        pl_, pc = _attn_probs(q_ref[...], kl_ref[...], kc_ref[...], is_ctx)
        o_ref[...] = (_dot(pl_.astype(BF16), vl_ref[...]) + _dot(pc.astype(BF16), vc_ref[...])).astype(BF16)

    return pl.pallas_call(
        body, name=name, grid=(dm.B, HEADS, nq + 1),
        in_specs=[sp["q"], sp["k_lat"], sp["k_ctx"], sp["v_lat"], sp["v_ctx"]], out_specs=sp["o"],
        out_shape=jax.ShapeDtypeStruct((T, HEADS * V_HEAD), BF16),
        compiler_params=_cparams("parallel", "parallel", "arbitrary"),
    )(q, k, k, v, v)


def _attn_bwd(q, k, v, o, do, dm, name):
    T = dm.T
    sp = _attn_specs(dm)
    nq, tq = sp["nq"], sp["tq"]
    HP, HQ, HV = HEAD_PAD, HEADS * HEAD_PAD, HEADS * V_HEAD

    def body(q_ref, kl_ref, kc_ref, vl_ref, vc_ref, o_ref, do_ref, dq_ref, dkl_ref, dkc_ref, dvl_ref, dvc_ref):
        i = pl.program_id(2)

        @pl.when(i == 0)
        def _():
            dkl_ref[...] = jnp.zeros_like(dkl_ref)
            dkc_ref[...] = jnp.zeros_like(dkc_ref)
            dvl_ref[...] = jnp.zeros_like(dvl_ref)
            dvc_ref[...] = jnp.zeros_like(dvc_ref)

        qv = q_ref[...]
        pl_, pc = _attn_probs(qv, kl_ref[...], kc_ref[...], i == nq)
        dov = do_ref[...]
        dob = dov.astype(BF16)
        delta = jnp.sum(dov * o_ref[...].astype(F32), axis=-1, keepdims=True)
        dsl = (pl_ * (_dot_nt(dob, vl_ref[...]) - delta) * QK_SCALE).astype(BF16)
        dsc = (pc * (_dot_nt(dob, vc_ref[...]) - delta) * QK_SCALE).astype(BF16)
        dq_ref[...] = _dot(dsl, kl_ref[...]) + _dot(dsc, kc_ref[...])
        dkl_ref[...] += _dot_tn(dsl, qv)
        dkc_ref[...] += _dot_tn(dsc, qv)
        dvl_ref[...] += _dot_tn(pl_.astype(BF16), dob)
        dvc_ref[...] += _dot_tn(pc.astype(BF16), dob)

    return pl.pallas_call(
        body, name=name, grid=(dm.B, HEADS, nq + 1),
        in_specs=[sp["q"], sp["k_lat"], sp["k_ctx"], sp["v_lat"], sp["v_ctx"], sp["o"], sp["o"]],
        out_specs=[sp["q"], sp["k_lat"], pl.BlockSpec((tq, HP), lambda b, hd, i: (b, hd)),
                   sp["v_lat"], pl.BlockSpec((tq, V_HEAD), lambda b, hd, i: (b, hd))],
        out_shape=[jax.ShapeDtypeStruct((T, HQ), F32), jax.ShapeDtypeStruct((dm.B * dm.N, HQ), F32),
                   jax.ShapeDtypeStruct((dm.B * dm.CTX, HQ), F32), jax.ShapeDtypeStruct((dm.B * dm.N, HV), F32),
                   jax.ShapeDtypeStruct((dm.B * dm.CTX, HV), F32)],
        compiler_params=_cparams("parallel", "parallel", "arbitrary"),
    )(q, k, k, v, v, o, do)


def _loss_grad(h, target, dm, name):
    T, D = h.shape
    tm = dm.tm
    nlat = dm.B * dm.N // tm

    def body(h_ref, t_ref, dh_ref, ls_ref):
        lat = (pl.program_id(0) < nlat).astype(F32)
        diff = (h_ref[...] - t_ref[...]) * lat
        dh_ref[...] = diff * (1.0 / D)
        ls_ref[...] = jnp.zeros(ls_ref.shape, F32) + (0.5 / D) * jnp.sum(diff * diff)

    return pl.pallas_call(
        body, name=name, grid=(T // tm,),
        in_specs=[pl.BlockSpec((tm, D), lambda i: (i, 0)), pl.BlockSpec((tm, D), lambda i: (jnp.minimum(i, nlat - 1), 0))],
        out_specs=[pl.BlockSpec((tm, D), lambda i: (i, 0)), pl.BlockSpec((1, 8, 128), lambda i: (i, 0, 0))],
        out_shape=[jax.ShapeDtypeStruct((T, D), F32), jax.ShapeDtypeStruct((T // tm, 8, 128), F32)],
        compiler_params=_cparams("parallel"),
    )(h, target)


def _col_block(cols, target=1152):
    return max(t for t in range(128, min(cols, target) + 1, 128) if cols % t == 0)


def _mod_fwd(cond, w_mod, b_mod, name):
    L, D, C = w_mod.shape
    R = cond.shape[0]
    cb = _col_block(C)

    def body(c_ref, w_ref, b_ref, o_ref):
        cv = c_ref[...]
        sc = (cv * jax.nn.sigmoid(cv)).astype(BF16)
        o_ref[...] = _dot(sc, w_ref[...].astype(BF16)) + b_ref[...]

    return pl.pallas_call(
        body, name=name, grid=(L, C // cb),
        in_specs=[pl.BlockSpec((R, D), lambda l, c: (0, 0)), pl.BlockSpec((None, D, cb), lambda l, c: (l, 0, c)),
                  pl.BlockSpec((None, 1, cb), lambda l, c: (l, 0, c))],
        out_specs=pl.BlockSpec((None, R, cb), lambda l, c: (l, 0, c)),
        out_shape=jax.ShapeDtypeStruct((L, R, C), F32),
        compiler_params=_cparams("parallel", "parallel"),
    )(cond, w_mod, b_mod)


def _mod_bwd(cond, dmod, w_mod, name):
    L, D, C = w_mod.shape
    R = cond.shape[0]
    cb = _col_block(C)
    nc = C // cb

    def body(c_ref, dm_ref, w_ref, gw_ref, ds_ref):
        cv = c_ref[...]
        sc = (cv * jax.nn.sigmoid(cv)).astype(BF16)
        dmv = dm_ref[...].astype(BF16)
        gw_ref[...] = _dot_tn(sc, dmv)
        part = _dot_nt(dmv, w_ref[...].astype(BF16))

        @pl.when(pl.program_id(1) == 0)
        def _():
            ds_ref[...] = part

        @pl.when(pl.program_id(1) > 0)
        def _():
            ds_ref[...] += part

    return pl.pallas_call(
        body, name=name, grid=(L, nc),
        in_specs=[pl.BlockSpec((R, D), lambda l, c: (0, 0)), pl.BlockSpec((None, R, cb), lambda l, c: (l, 0, c)),
                  pl.BlockSpec((None, D, cb), lambda l, c: (l, 0, c))],
        out_specs=[pl.BlockSpec((None, D, cb), lambda l, c: (l, 0, c)), pl.BlockSpec((None, R, D), lambda l, c: (l, 0, 0))],
        out_shape=[jax.ShapeDtypeStruct((L, D, C), F32), jax.ShapeDtypeStruct((L, R, D), F32)],
        compiler_params=_cparams("parallel", "arbitrary"),
    )(cond, dmod, w_mod)


def _row_block(rows, cols, budget=1 << 20):
    best = None
    for t in range(16, rows + 1, 16):
        if rows % t == 0 and t * cols * 4 <= budget:
            best = t
    return best if best is not None else rows


def _sum_slots(recv, own, chip, core, name):
    S, R, C = recv.shape
    tr = _row_block(R, C, budget=512 << 10)

    def body(ids_ref, r_ref, p_ref, o_ref):
        acc = None
        for s in range(S):
            v = jnp.where(ids_ref[0] == s, p_ref[s], r_ref[s]).astype(F32)
            acc = v if acc is None else acc + v
        o_ref[...] = acc

    blk = pl.BlockSpec((S, tr, C), lambda i, ids: (0, i, 0))
    return pl.pallas_call(
        body, name=name,
        grid_spec=pltpu.PrefetchScalarGridSpec(
            num_scalar_prefetch=1, grid=(R // tr,), in_specs=[blk, blk],
            out_specs=pl.BlockSpec((None, tr, C), lambda i, ids: (ids[1], i, 0))),
        out_shape=jax.ShapeDtypeStruct((2, R, C), F32), compiler_params=_cparams("parallel"),
    )(jnp.stack([chip, core]).astype(jnp.int32), recv, own)


def _adamw(w, gs, m, v, name):
    ng = len(gs)
    R, C = w.shape
    tr = _row_block(R, C)
    c1 = 1.0 / (1.0 - ADAM_B1 ** ADAM_STEP)
    c2 = 1.0 / (1.0 - ADAM_B2 ** ADAM_STEP)

    def body(w_ref, *refs):
        m_ref, v_ref, g_ref, d_ref, mo_ref, vo_ref = refs[ng:]
        g = refs[0][...]
        for g_more in refs[1:ng]:
            g = g + g_more[...]
        g_ref[...] = g
        mn = ADAM_B1 * m_ref[...] + (1.0 - ADAM_B1) * g
        vn = ADAM_B2 * v_ref[...] + (1.0 - ADAM_B2) * (g * g)
        mo_ref[...] = mn
        vo_ref[...] = vn
        d_ref[...] = -ADAM_LR * ((mn * c1) / (jnp.sqrt(vn * c2) + ADAM_EPS) + ADAM_WD * w_ref[...])

    blk = pl.BlockSpec((tr, C), lambda i: (i, 0))
    sd = jax.ShapeDtypeStruct((R, C), F32)
    return pl.pallas_call(
        body, name=name, grid=(R // tr,), in_specs=[blk] * (3 + ng), out_specs=[blk] * 4, out_shape=[sd] * 4,
        compiler_params=_cparams("parallel"),
    )(w, *gs, m, v)


def _rope_tables(dm):
    n = dm.N
    t = jnp.arange(n)
    r = (t // GRID_W).astype(F32)
    col = (t % GRID_W).astype(F32)
    nf = QK_ROPE // 4
    inv = ROPE_BASE ** (-jnp.arange(nf, dtype=F32) / nf)
    ang = jnp.stack([r[:, None] * inv, col[:, None] * inv], axis=1)
    cos, sin = jnp.cos(ang), jnp.sin(ang)
    zero = jnp.zeros_like(sin)
    c64 = jnp.stack([cos, cos], axis=2).reshape(n, QK_ROPE)
    s1 = jnp.stack([-sin, zero], axis=2).reshape(n, QK_ROPE)
    s2 = jnp.stack([zero, sin], axis=2).reshape(n, QK_ROPE)

    def pad(x, fill):
        return jnp.concatenate([jnp.full((n, QK_NOPE), fill, F32), x, jnp.full((n, HEAD_PAD - QK_HEAD), fill, F32)], axis=1)

    lat = jnp.stack([pad(c64, 1.0), pad(s1, 0.0), pad(s2, 0.0)])
    lat = jnp.tile(lat, (1, dm.B, 1))
    nctx = dm.B * dm.CTX
    ctx = jnp.stack([jnp.ones((nctx, HEAD_PAD), F32), jnp.zeros((nctx, HEAD_PAD), F32), jnp.zeros((nctx, HEAD_PAD), F32)])
    return jnp.concatenate([lat, ctx], axis=1)


def _fold_parts(part, dm):
    nblk = part.shape[0]
    nb = (dm.N * nblk) // dm.T
    groups = [part[b * nb:(b + 1) * nb].sum(axis=0) for b in range(dm.B)]
    groups.append(part[dm.B * nb:].sum(axis=0))
    return jnp.stack(groups)


def _flat2(a):
    return a.reshape(-1, a.shape[-1])


def kernel(x, c, ctx, c_ctx, w_mod, b_mod, g_norm, ffn_w1, ffn_w3, ffn_w2, sc_w_in, sc_conv, sc_w_out, mla_w_a, mla_g_qa, mla_w_uq, mla_g_kva, mla_w_ukv, mla_g_q, mla_g_k, mla_w_o, loss_target, m_c_ctx, m_w_mod, m_b_mod, m_g_norm, m_ffn_w1, m_ffn_w3, m_ffn_w2, m_sc_w_in, m_sc_conv, m_sc_w_out, m_mla_w_a, m_mla_g_qa, m_mla_w_uq, m_mla_g_kva, m_mla_w_ukv, m_mla_g_q, m_mla_g_k, m_mla_w_o, v_c_ctx, v_w_mod, v_b_mod, v_g_norm, v_ffn_w1, v_ffn_w3, v_ffn_w2, v_sc_w_in, v_sc_conv, v_sc_w_out, v_mla_w_a, v_mla_g_qa, v_mla_w_uq, v_mla_g_kva, v_mla_w_ukv, v_mla_g_q, v_mla_g_k, v_mla_w_o):
    B, N, D = x.shape
    CTX = ctx.shape[1]
    T = B * (N + CTX)
    tm = next(t for t in (512, 256, 128, 64, 32, 16) if N % t == 0 and (B * CTX) % t == 0)
    dm = Dims(B, N, CTX, D, T, tm)
    L = w_mod.shape[0]
    La, Lb = sc_w_in.shape[0], mla_w_a.shape[0]
    S = N_CHIPS
    ndev = 2 * S
    xi, yi, ci = lax.axis_index("x"), lax.axis_index("y"), lax.axis_index("c")
    chip = 2 * xi + yi
    dev = 2 * chip + ci
    weights = dict(c_ctx=c_ctx, w_mod=w_mod, b_mod=b_mod, g_norm=g_norm, ffn_w1=ffn_w1, ffn_w3=ffn_w3, ffn_w2=ffn_w2,
                   sc_w_in=sc_w_in, sc_conv=sc_conv, sc_w_out=sc_w_out, mla_w_a=mla_w_a, mla_g_qa=mla_g_qa,
                   mla_w_uq=mla_w_uq, mla_g_kva=mla_g_kva, mla_w_ukv=mla_w_ukv, mla_g_q=mla_g_q, mla_g_k=mla_g_k,
                   mla_w_o=mla_w_o)
    mom = dict(c_ctx=(m_c_ctx, v_c_ctx), w_mod=(m_w_mod, v_w_mod), b_mod=(m_b_mod, v_b_mod), g_norm=(m_g_norm, v_g_norm),
               ffn_w1=(m_ffn_w1, v_ffn_w1), ffn_w3=(m_ffn_w3, v_ffn_w3), ffn_w2=(m_ffn_w2, v_ffn_w2),
               sc_w_in=(m_sc_w_in, v_sc_w_in), sc_conv=(m_sc_conv, v_sc_conv), sc_w_out=(m_sc_w_out, v_sc_w_out),
               mla_w_a=(m_mla_w_a, v_mla_w_a), mla_g_qa=(m_mla_g_qa, v_mla_g_qa), mla_w_uq=(m_mla_w_uq, v_mla_w_uq),
               mla_g_kva=(m_mla_g_kva, v_mla_g_kva), mla_w_ukv=(m_mla_w_ukv, v_mla_w_ukv), mla_g_q=(m_mla_g_q, v_mla_g_q),
               mla_g_k=(m_mla_g_k, v_mla_g_k), mla_w_o=(m_mla_w_o, v_mla_w_o))

    big = ["ffn_w1", "ffn_w3", "ffn_w2", "sc_w_in", "sc_w_out", "mla_w_a", "mla_w_uq", "mla_w_ukv", "mla_w_o"]
    gathered = _gather_two_level([_place_cast(_flat2(weights[n]), chip, S, "place_weight") for n in big], "gather_weights")
    gw = {n: g.reshape((S,) + weights[n].shape) for n, g in zip(big, gathered)}
    vecs = ["g_norm", "sc_conv", "mla_g_qa"]
    gathered = _exchange([_flat2(weights[n]) for n in vecs], ("x", "y"), False, "gather_vectors")
    gw.update({n: g.reshape((S,) + weights[n].shape) for n, g in zip(vecs, gathered)})
    w1g, w3g, w2g, wing = gw["ffn_w1"], gw["ffn_w3"], gw["ffn_w2"], gw["sc_w_in"]
    wout = jnp.moveaxis(gw["sc_w_out"], 0, 1).reshape(La, D, D)
    wo = jnp.moveaxis(gw["mla_w_o"], 0, 1).reshape(Lb, HEADS * V_HEAD, D)
    wa = jnp.pad(jnp.moveaxis(gw["mla_w_a"], 0, 1).reshape(Lb, D, -1), ((0, 0), (0, 0), (0, 512 - (Q_LORA + KV_LORA + QK_ROPE))))
    wuq = jnp.moveaxis(gw["mla_w_uq"], 0, 2).reshape(Lb, Q_LORA, HEADS, QK_HEAD)
    wuq = jnp.pad(wuq, ((0, 0), (0, 0), (0, 0), (0, HEAD_PAD - QK_HEAD))).reshape(Lb, Q_LORA, HEADS * HEAD_PAD)
    wukv = jnp.moveaxis(gw["mla_w_ukv"], 0, 2).reshape(Lb, KV_LORA, HEADS * HEAD_PAD)
    gnorm = jnp.moveaxis(gw["g_norm"], 0, 2).reshape(L, 3, D)
    convw = jnp.moveaxis(gw["sc_conv"], 0, 2).reshape(La, 3, D)
    gqa = jnp.moveaxis(gw["mla_g_qa"], 0, 1).reshape(Lb, Q_LORA)
    padl = lambda a: jnp.pad(a, ((0, 0), (0, HEAD_PAD - a.shape[1])))
    gains = jnp.stack([padl(gqa), padl(mla_g_kva), padl(mla_g_q), padl(mla_g_k)], axis=1)
    gains = jnp.pad(gains, ((0, 0), (0, 4), (0, 0)))

    R = -(-(ndev * B + 1) // 16) * 16
    call = _exchange([c], ("x", "y", "c"), False, "gather_cond")[0].reshape(ndev * B, D)
    cond = jnp.concatenate([call, c_ctx[None], jnp.zeros((R - ndev * B - 1, D), F32)], axis=0)
    C = w_mod.shape[-1]
    bm = lax.dynamic_slice_in_dim(b_mod, chip * C, C, axis=1)[:, None, :]
    mshard = _mod_fwd(cond, w_mod, bm, "mod_fwd")
    mfull = _exchange([mshard.reshape(L * R, C)], ("x", "y"), False, "gather_mod")[0].reshape(S, L, R, C)
    mfull = jnp.moveaxis(mfull, 0, 2).reshape(L, R, S * C)
    mine = lax.dynamic_slice_in_dim(mfull, dev * B, B, axis=1)
    mod = jnp.concatenate([mine, mfull[:, ndev * B:ndev * B + 1]], axis=1).reshape(L, B + 1, 9, D)

    tabs = _rope_tables(dm)
    h = jnp.concatenate([x.reshape(B * N, D), ctx.reshape(B * CTX, D)], axis=0)

    saved = []
    for l in range(L):
        kind, j = l % 2, l // 2
        sv = {}
        sv["h0"] = h
        h, sv["a1"], sv["b1"], sv["hn1"], sv["y1"] = _ffn_fwd(h, mod[l], gnorm[l, 0:1], w1g, w3g, w2g, l, 0, dm, "ffn_fwd")
        sv["h1"] = h
        if kind == 0:
            sv["p"], sv["hnm"] = _sc_in_fwd(h, mod[l], gnorm[l, 1:2], wing, j, dm, "sc_in_fwd")
            sv["z"] = _conv_fwd(sv["p"], convw[j], dm, "conv_fwd")
            h, sv["ym"] = _out_fwd(sv["z"], wout, h, mod[l], j, dm, "sc_out_fwd")
        else:
            sv["hnm"], sv["q"], sv["k"], sv["v"] = _mla_proj_fwd(h, mod[l], gnorm[l, 1:2], gains, tabs, wa, wuq, wukv, j, dm, "mla_proj_fwd")
            sv["o"] = _attn_fwd(sv["q"], sv["k"], sv["v"], dm, "attn_fwd")
            h, sv["ym"] = _out_fwd(sv["o"], wo, h, mod[l], j, dm, "mla_out_fwd")
        sv["h2"] = h
        h, sv["a2"], sv["b2"], sv["hn2"], sv["y2"] = _ffn_fwd(h, mod[l], gnorm[l, 2:3], w1g, w3g, w2g, l, 1, dm, "ffn_fwd")
        saved.append(sv)

    dh, lsum = _loss_grad(h, loss_target.reshape(B * N, D), dm, "loss_grad")
    loss = lax.psum(jnp.sum(lsum[:, 0, 0]), ("x", "y", "c"))

    F = w1g.shape[-1]
    wq = D // S
    G = dict(ffn_w1=jnp.zeros((S, L, 2, D, F), BF16), ffn_w3=jnp.zeros((S, L, 2, D, F), BF16), ffn_w2=jnp.zeros((S, L, 2, F, D), BF16),
             sc_w_in=jnp.zeros((La, 3 * S, D, wq), BF16), sc_w_out=jnp.zeros((La, D, D), BF16), mla_w_o=jnp.zeros((Lb, HEADS * V_HEAD, D), BF16))
    dmod = [None] * L
    dgn = [None] * L
    dconv = [None] * La
    dgains = [None] * Lb
    gmla = dict(mla_w_a=[None] * Lb, mla_w_uq=[None] * Lb, mla_w_ukv=[None] * Lb)
    tk = tm
    nk = T // tk
    full_a = pl.BlockSpec((tk, D), lambda s, kk: (kk, 0))
    shard_b = pl.BlockSpec((None, tk, F), lambda s, kk: (s, kk, 0))

    def ffn_back(dh, sv, l, k):
        sfx = "1" if k == 0 else "2"
        dh, da, db, sw, dy, part = _ffn_bwd(dh, sv["h0" if k == 0 else "h2"], mod[l], gnorm[l, 2 * k:2 * k + 1], sv["y" + sfx],
                                            sv["a" + sfx], sv["b" + sfx], w1g, w3g, w2g, l, k, dm, "ffn_bwd")
        o5 = lambda s, kk: (s, l, k, 0, 0)
        G["ffn_w1"] = _mm_tn(sv["hn" + sfx], da, full_a, shard_b, G["ffn_w1"], pl.BlockSpec((None, None, None, D, F), o5), (S, nk), "gw1")
        G["ffn_w3"] = _mm_tn(sv["hn" + sfx], db, full_a, shard_b, G["ffn_w3"], pl.BlockSpec((None, None, None, D, F), o5), (S, nk), "gw3")
        G["ffn_w2"] = _mm_tn(sw, dy, shard_b, full_a, G["ffn_w2"], pl.BlockSpec((None, None, None, F, D), o5), (S, nk), "gw2")
        return dh, _fold_parts(part, dm)

    one = (1, nk)
    a1 = lambda kdim: pl.BlockSpec((tk, kdim), lambda s, kk: (kk, 0))
    for l in reversed(range(L)):
        kind, j = l % 2, l // 2
        sv = saved[l]
        dh, p2 = ffn_back(dh, sv, l, 1)
        if kind == 0:
            dy, dz, pg = _out_bwd(dh, sv["ym"], wout, mod[l], j, dm, "sc_out_bwd")
            G["sc_w_out"] = _mm_tn(sv["z"], dy, a1(D), a1(D), G["sc_w_out"], pl.BlockSpec((None, D, D), lambda s, kk: (j, 0, 0)), one, "gw_sc_out")
            dp, dconv[j] = _conv_bwd(dz, sv["p"], convw[j], dm, "conv_bwd")
            G["sc_w_in"] = _mm_tn(sv["hnm"], dp, pl.BlockSpec((tk, D), lambda q, kk: (kk, 0)),
                                  pl.BlockSpec((None, tk, wq), lambda q, kk: (q // S, kk, q % S)), G["sc_w_in"],
                                  pl.BlockSpec((None, None, D, wq), lambda q, kk: (j, q, 0, 0)), (3 * S, nk), "gw_sc_in")
            dh, pm = _sc_in_bwd(dh, dp, sv["h1"], mod[l], gnorm[l, 1:2], wing, j, dm, "sc_in_bwd")
        else:
            dy, do, pg = _out_bwd(dh, sv["ym"], wo, mod[l], j, dm, "mla_out_bwd")
            G["mla_w_o"] = _mm_tn(sv["o"], dy, a1(HEADS * V_HEAD), a1(D), G["mla_w_o"], pl.BlockSpec((None, HEADS * V_HEAD, D), lambda s, kk: (j, 0, 0)), one, "gw_mla_o")
            dq, dkl, dkc, dvl, dvc = _attn_bwd(sv["q"], sv["k"], sv["v"], sv["o"], do, dm, "attn_bwd")
            dk = jnp.concatenate([dkl, dkc], axis=0)
            dv = jnp.concatenate([dvl, dvc], axis=0)
            dh, pm, gmla["mla_w_a"][j], gmla["mla_w_uq"][j], gmla["mla_w_ukv"][j], dgains[j] = _mla_proj_bwd(
                dh, dq, dk, dv, sv["h1"], mod[l], gnorm[l, 1:2], gains, tabs, wa, wuq, wukv, j, dm, "mla_proj_bwd")
        pm = _fold_parts(pm, dm) + _fold_parts(pg, dm)
        dh, p0 = ffn_back(dh, sv, l, 0)
        dmod[l] = jnp.concatenate([p0[:, 0:3], pm[:, 0:3], p2[:, 0:3]], axis=1).reshape(B + 1, 9 * D)
        dgn[l] = jnp.stack([p0[:, 3].sum(0), pm[:, 3].sum(0), p2[:, 3].sum(0)])
    grad_x = dh[:B * N].reshape(B, N, D)

    Gs = dict(ffn_w1=G["ffn_w1"], ffn_w3=G["ffn_w3"], ffn_w2=G["ffn_w2"])
    Gs["sc_w_in"] = jnp.moveaxis(G["sc_w_in"].reshape(La, S, 3, D, wq), (1, 2), (0, 3)).reshape(S, La, D, 3 * wq)
    Gs["sc_w_out"] = jnp.moveaxis(G["sc_w_out"].reshape(La, S, D // S, D), 1, 0)
    Gs["mla_w_o"] = jnp.moveaxis(G["mla_w_o"].reshape(Lb, S, HEADS * V_HEAD // S, D), 1, 0)
    ga = jnp.stack(gmla["mla_w_a"])[:, :, :Q_LORA + KV_LORA + QK_ROPE]
    Gs["mla_w_a"] = jnp.moveaxis(ga.reshape(Lb, S, D // S, -1), 1, 0).astype(BF16)
    guq = jnp.stack(gmla["mla_w_uq"]).reshape(Lb, Q_LORA, HEADS, HEAD_PAD)[..., :QK_HEAD]
    Gs["mla_w_uq"] = jnp.moveaxis(guq.reshape(Lb, Q_LORA, S, -1), 2, 0).astype(BF16)
    Gs["mla_w_ukv"] = jnp.moveaxis(jnp.stack(gmla["mla_w_ukv"]).reshape(Lb, KV_LORA, S, -1), 2, 0).astype(BF16)

    parts = [Gs[n].reshape(S, -1, Gs[n].shape[-1]) for n in big]
    theirs = _swap_halves(parts, "swap_halves")
    pairs = [_pair_sum(g_, r_, ci, "pair_sum") for g_, r_ in zip(parts, theirs)]
    recv = _exchange(pairs, ("x", "y"), True, "scatter_grads", own="skip")
    sums = [_sum_slots(r_, p_, chip, ci, "sum_slots") for r_, p_ in zip(recv, pairs)]
    both = _exchange(sums, ("c",), False, "swap_cores", own="inplace")
    gsum = dict(zip(big, both))

    dgains_a = jnp.stack(dgains)
    small = [jnp.stack(dmod).reshape(-1), jnp.stack(dgn).reshape(-1), jnp.stack(dconv).reshape(-1), dgains_a.reshape(-1)]
    sizes = [s_.shape[0] for s_ in small]
    flat = jnp.concatenate(small)
    pad = (-flat.shape[0]) % 1024
    flat = jnp.pad(flat, (0, pad)).reshape(-1, 128)
    allsmall = _exchange([flat], ("x", "y", "c"), False, "gather_small")[0].reshape(ndev, -1)
    offs = [0]
    for s_ in sizes:
        offs.append(offs[-1] + s_)
    dmod_all = allsmall[:, offs[0]:offs[1]].reshape(ndev, L, B + 1, 9 * D)
    tot = allsmall[:, offs[1]:offs[4]].sum(axis=0)
    g_gnorm = tot[:offs[2] - offs[1]].reshape(L, 3, D)
    g_conv = tot[offs[2] - offs[1]:offs[3] - offs[1]].reshape(La, 3, D)
    g_gains = tot[offs[3] - offs[1]:].reshape(Lb, 8, HEAD_PAD)
    dM = jnp.concatenate([jnp.moveaxis(dmod_all[:, :, :B], 0, 1).reshape(L, ndev * B, 9 * D),
                          dmod_all[:, :, B].sum(axis=0)[:, None, :], jnp.zeros((L, R - ndev * B - 1, 9 * D), F32)], axis=1)
    g_bmod = dM.sum(axis=1)
    dM_mine = lax.dynamic_slice_in_dim(dM, chip * C, C, axis=2)
    g_wmod, dsil = _mod_bwd(cond, dM_mine, w_mod, "mod_bwd")
    dsil_ctx = dsil[:, ndev * B].sum(axis=0)
    dsil_all = _exchange([jnp.pad(dsil_ctx.reshape(-1, 128), ((0, (-(D // 128)) % 8), (0, 0)))], ("x", "y"), False, "gather_dctx")[0]
    dsil_tot = dsil_all.sum(axis=0)[:D // 128].reshape(D)
    sg = jax.nn.sigmoid(c_ctx)
    g_cctx = dsil_tot * (sg * (1.0 + c_ctx * (1.0 - sg)))

    chip_cols = lambda a, width: lax.dynamic_slice_in_dim(a, chip * width, width, axis=a.ndim - 1)
    small_grads = dict(
        c_ctx=g_cctx, b_mod=g_bmod, g_norm=chip_cols(g_gnorm, D // S), sc_conv=chip_cols(g_conv, D // S),
        mla_g_qa=chip_cols(g_gains[:, 0, :Q_LORA], Q_LORA // S), mla_g_kva=g_gains[:, 1, :KV_LORA],
        mla_g_q=g_gains[:, 2, :QK_HEAD], mla_g_k=g_gains[:, 3, :QK_HEAD])

    grads, deltas, new_m, new_v = {}, {}, {}, {}
    for n, w in weights.items():
        shape = w.shape
        w2 = _flat2(w) if w.ndim > 1 else w.reshape(1, -1)
        m2, v2 = (a.reshape(w2.shape) for a in mom[n])
        if n in gsum:
            gs = [gsum[n].reshape(w2.shape)]
        elif n == "w_mod":
            gs = [_flat2(g_wmod)]
        else:
            gs = [small_grads[n].reshape(w2.shape)]
        g_, d_, m_, v_ = _adamw(w2, gs, m2, v2, "adamw")
        grads[n], deltas[n], new_m[n], new_v[n] = (a.reshape(shape) for a in (g_, d_, m_, v_))

    names = list(weights)
    return (loss, grad_x, *[grads[n] for n in names], *[deltas[n] for n in names], *[new_m[n] for n in names],
            *[new_v[n] for n in names])
```

```python
import functools
import math
from typing import NamedTuple

import jax
import jax.numpy as jnp
from jax import lax
from jax.experimental import pallas as pl
from jax.experimental.pallas import tpu as pltpu

F32 = jnp.float32
BF16 = jnp.bfloat16
EPS = 1e-6
GRID_W = 64
HEADS = 8
QK_NOPE = 128
QK_ROPE = 64
QK_HEAD = QK_NOPE + QK_ROPE
HEAD_PAD = 256
V_HEAD = 128
Q_LORA = 256
KV_LORA = 128
ROPE_BASE = 10000.0
QK_SCALE = QK_HEAD ** -0.5
ADAM_LR, ADAM_B1, ADAM_B2, ADAM_EPS, ADAM_WD, ADAM_STEP = 0.001, 0.9, 0.999, 1e-08, 0.01, 10
N_CHIPS = 4
VMEM_LIMIT = 56 * 1024 * 1024
MESH = pl.DeviceIdType.MESH
NEG = -1e30


class Dims(NamedTuple):
    B: int
    N: int
    CTX: int
    D: int
    T: int
    tm: int


def _cparams(*sem):
    return pltpu.CompilerParams(dimension_semantics=sem if sem else None, vmem_limit_bytes=VMEM_LIMIT)


def _dot(a, b):
    return jnp.dot(a, b, preferred_element_type=F32)


def _dot_nt(a, b):
    return lax.dot_general(a, b, (((1,), (1,)), ((), ())), preferred_element_type=F32)


def _dot_tn(a, b):
    return lax.dot_general(a, b, (((0,), (0,)), ((), ())), preferred_element_type=F32)


def _rms(x, n):
    r = lax.rsqrt(jnp.sum(x * x, axis=-1, keepdims=True) * (1.0 / n) + EPS)
    return x * r, r


def _rms_bwd(dxh, xh, r, n):
    return r * (dxh - xh * (jnp.sum(dxh * xh, axis=-1, keepdims=True) * (1.0 / n)))


def _pre(h, g, shift, scale):
    xh, _ = _rms(h, h.shape[-1])
    return (xh * g) * (1.0 + scale) + shift


def _pre_bwd(dout, h, g, scale):
    d = h.shape[-1]
    xh, r = _rms(h, d)
    n = xh * g
    dshift = jnp.sum(dout, axis=0, keepdims=True)
    dscale = jnp.sum(dout * n, axis=0, keepdims=True)
    dn = dout * (1.0 + scale)
    dg = jnp.sum(dn * xh, axis=0, keepdims=True)
    dh = _rms_bwd(dn * g, xh, r, d)
    return dh, dshift, dscale, dg


def _write_part(part_ref, dshift=None, dscale=None, dgate=None, dg=None):
    z = jnp.zeros((1, part_ref.shape[-1]), F32)
    part_ref[0, 0:1, :] = z if dshift is None else dshift
    part_ref[0, 1:2, :] = z if dscale is None else dscale
    part_ref[0, 2:3, :] = z if dgate is None else dgate
    part_ref[0, 3:4, :] = z if dg is None else dg
    part_ref[0, 4:8, :] = jnp.zeros((4, part_ref.shape[-1]), F32)


def _grp(dm):
    nb = dm.N // dm.tm
    return lambda i: jnp.minimum(i // nb, dm.B)


def _n_chunks(rows, row_bytes):
    n = 16
    while n > 1 and (rows % (16 * n) or (rows // n) * row_bytes < (256 << 10)):
        n //= 2
    return n


def _start_local(src, dst, sems, k0, nchunk):
    ch = src.shape[0] // nchunk
    copies = []
    for j in range(nchunk):
        cp = pltpu.make_async_copy(src.at[pl.ds(j * ch, ch)], dst.at[pl.ds(j * ch, ch)], sems.at[k0 + j])
        cp.start()
        copies.append(cp)
    return copies


def _exchange(arrs, axes, scatter, name, own="copy"):
    n = len(arrs)
    nbits = len(axes)
    slots = 2 ** nbits
    pats = list(range(1, slots))
    inplace = own == "inplace"
    nck = [_n_chunks(a.shape[-2], a.shape[-1] * a.dtype.itemsize) for a in arrs]
    base = [sum(nck[:i]) * len(pats) for i in range(n)]
    nsem = sum(nck) * len(pats)

    def body(*refs):
        ins, outs = refs[:n], refs[n:2 * n]
        send, recv, loc = refs[2 * n:]
        pos = {a: lax.axis_index(a) for a in ("x", "y", "c")}

        def slot_of(p):
            s = 0
            for a in axes:
                s = 2 * s + p[a]
            return s

        me = slot_of(pos)
        local = []
        for i in range(n):
            if own == "copy":
                local += _start_local(ins[i].at[me] if scatter else ins[i], outs[i].at[me], loc, sum(nck[:i]), nck[i])
        remote = []
        for pi, pat in enumerate(pats):
            peer = dict(pos)
            for bi, a in enumerate(axes):
                if (pat >> (nbits - 1 - bi)) & 1:
                    peer[a] = 1 - pos[a]
            them = slot_of(peer)
            for i in range(n):
                ch = arrs[i].shape[-2] // nck[i]
                for j in range(nck[i]):
                    k = base[i] + pi * nck[i] + j
                    rs = pl.ds(j * ch, ch)
                    if inplace:
                        src = outs[i].at[me, rs]
                    else:
                        src = ins[i].at[them, rs] if scatter else ins[i].at[rs]
                    cp = pltpu.make_async_remote_copy(
                        src_ref=src, dst_ref=outs[i].at[me, rs], send_sem=send.at[k], recv_sem=recv.at[k],
                        device_id=(peer["x"], peer["y"], peer["c"]), device_id_type=MESH)
                    cp.start()
                    remote.append(cp)
        for cp in local:
            cp.wait()
        for cp in remote:
            cp.wait()

    out_shape = [jax.ShapeDtypeStruct(a.shape if (scatter or inplace) else (slots,) + a.shape, a.dtype) for a in arrs]
    any_spec = pl.BlockSpec(memory_space=pl.ANY)
    outs = pl.pallas_call(
        body, name=name, out_shape=out_shape, in_specs=[any_spec] * n, out_specs=[any_spec] * n,
        scratch_shapes=[pltpu.SemaphoreType.DMA((nsem,)), pltpu.SemaphoreType.DMA((nsem,)), pltpu.SemaphoreType.DMA((sum(nck),))],
        input_output_aliases={i: i for i in range(n)} if inplace else {},
        compiler_params=pltpu.CompilerParams(has_side_effects=True),
    )(*arrs)
    return list(outs)


class Rider(NamedTuple):
    ins: list
    out_shapes: list
    aliases: dict
    sems: list
    start: object
    mid: object
    end: object


MID_STEPS = 6


def _hosted(body, rider, *, name, grid, in_specs, out_specs, out_shape, scratch_shapes, sem, args):
    if rider is None:
        outs = pl.pallas_call(body, name=name, grid=grid, in_specs=in_specs, out_specs=out_specs, out_shape=out_shape,
                              scratch_shapes=scratch_shapes, compiler_params=_cparams(*sem))(*args)
        return outs, []
    n_in, n_out, n_s = len(in_specs), len(out_specs), len(scratch_shapes)
    nri, nro = len(rider.ins), len(rider.out_shapes)
    nsteps = math.prod(grid)

    def wrapped(*refs):
        bounds = [0, n_in, n_in + nri, n_in + nri + n_out, n_in + nri + n_out + nro, n_in + nri + n_out + nro + n_s, len(refs)]
        ins, rins, outs, routs, scr, sems = (refs[lo:hi] for lo, hi in zip(bounds[:-1], bounds[1:]))
        step = 0
        for ax, extent in enumerate(grid):
            step = step * extent + pl.program_id(ax)

        @pl.when(step == 0)
        def _():
            rider.start(rins, routs, sems)

        body(*ins, *outs, *scr)

        if rider.mid is not None:
            @pl.when(step == max(nsteps - 1 - MID_STEPS, 0))
            def _():
                rider.mid(rins, routs, sems)

        @pl.when(step == nsteps - 1)
        def _():
            rider.end(rins, routs, sems)

    any_spec = pl.BlockSpec(memory_space=pl.ANY)
    outs = pl.pallas_call(
        wrapped, name=name, grid=grid, in_specs=list(in_specs) + [any_spec] * nri, out_specs=list(out_specs) + [any_spec] * nro,
        out_shape=list(out_shape) + list(rider.out_shapes), scratch_shapes=list(scratch_shapes) + list(rider.sems),
        input_output_aliases={n_in + i: n_out + o for i, o in rider.aliases.items()},
        compiler_params=pltpu.CompilerParams(dimension_semantics=("arbitrary",) * len(grid), vmem_limit_bytes=VMEM_LIMIT,
                                             has_side_effects=True),
    )(*args, *rider.ins)
    return outs[:n_out], list(outs[n_out:])


def _gather_rider(bufs):
    n = len(bufs)
    halves = [a.shape[1] // 2 for a in bufs]
    nck = [_n_chunks(h, a.shape[2] * a.dtype.itemsize) for h, a in zip(halves, bufs)]
    base = [3 * sum(nck[:i]) for i in range(n)]
    nsem = 3 * sum(nck)

    def plan():
        x, y, c = lax.axis_index("x"), lax.axis_index("y"), lax.axis_index("c")
        pieces = []
        for pi, (px, py) in enumerate([(x, 1 - y), (1 - x, y), (1 - x, 1 - y)]):
            for i in range(n):
                ch = halves[i] // nck[i]
                for j in range(nck[i]):
                    pieces.append((base[i] + pi * nck[i] + j, px, py, 2 * px + py, i, j * ch, ch))
        return x, y, c, 2 * x + y, pieces

    def rows(i, off, ch, core):
        return pl.ds(pl.multiple_of(core * halves[i] + off, 16), ch)

    def over_ici(outs, sems, c, slot, k, px, py, i, off, ch):
        ref = outs[i].at[slot, rows(i, off, ch, c)]
        return pltpu.make_async_remote_copy(src_ref=ref, dst_ref=ref, send_sem=sems[0].at[k], recv_sem=sems[1].at[k],
                                            device_id=(px, py, c), device_id_type=MESH)

    def over_d2d(outs, sems, x, y, c, slot, k, i, off, ch, core):
        ref = outs[i].at[slot, rows(i, off, ch, core)]
        return pltpu.make_async_remote_copy(src_ref=ref, dst_ref=ref, send_sem=sems[2].at[k], recv_sem=sems[3].at[k],
                                            device_id=(x, y, 1 - c), device_id_type=MESH)

    def start(ins, outs, sems):
        x, y, c, me, pieces = plan()
        for k, px, py, them, i, off, ch in pieces:
            over_ici(outs, sems, c, me, k, px, py, i, off, ch).start()

    def mid(ins, outs, sems):
        x, y, c, me, pieces = plan()
        for k, px, py, them, i, off, ch in pieces:
            over_ici(outs, sems, c, them, k, px, py, i, off, ch).wait_recv()
            over_d2d(outs, sems, x, y, c, them, k, i, off, ch, c).start()

    def end(ins, outs, sems):
        x, y, c, me, pieces = plan()
        for k, px, py, them, i, off, ch in pieces:
            over_ici(outs, sems, c, me, k, px, py, i, off, ch).wait_send()
            over_d2d(outs, sems, x, y, c, them, k, i, off, ch, c).wait_send()
        for k, px, py, them, i, off, ch in pieces:
            over_d2d(outs, sems, x, y, c, them, k, i, off, ch, 1 - c).wait_recv()

    return Rider(ins=list(bufs), out_shapes=[jax.ShapeDtypeStruct(a.shape, a.dtype) for a in bufs],
                 aliases={i: i for i in range(n)}, sems=[pltpu.SemaphoreType.DMA((nsem,))] * 4, start=start, mid=mid, end=end)


def _scatter_rider(srcs):
    n = len(srcs)
    nck = [_n_chunks(a.shape[1], a.shape[2] * a.dtype.itemsize) for a in srcs]
    base = [3 * sum(nck[:i]) for i in range(n)]
    nsem = 3 * sum(nck)

    def copies(ins, outs, sems):
        x, y, c = lax.axis_index("x"), lax.axis_index("y"), lax.axis_index("c")
        me = 2 * x + y
        for pi, (px, py) in enumerate([(x, 1 - y), (1 - x, y), (1 - x, 1 - y)]):
            for i in range(n):
                ch = srcs[i].shape[1] // nck[i]
                for j in range(nck[i]):
                    k = base[i] + pi * nck[i] + j
                    rs = pl.ds(j * ch, ch)
                    yield pltpu.make_async_remote_copy(
                        src_ref=ins[i].at[2 * px + py, rs], dst_ref=outs[i].at[me, rs], send_sem=sems[0].at[k],
                        recv_sem=sems[1].at[k], device_id=(px, py, c), device_id_type=MESH)

    def start(ins, outs, sems):
        for cp in copies(ins, outs, sems):
            cp.start()

    def end(ins, outs, sems):
        for cp in copies(ins, outs, sems):
            cp.wait()

    return Rider(ins=list(srcs), out_shapes=[jax.ShapeDtypeStruct(a.shape, a.dtype) for a in srcs], aliases={},
                 sems=[pltpu.SemaphoreType.DMA((nsem,))] * 2, start=start, mid=None, end=end)


def _ride_alone(rider, name):
    n_in, n_out = len(rider.ins), len(rider.out_shapes)

    def body(*refs):
        ins, outs, sems = refs[:n_in], refs[n_in:n_in + n_out], refs[n_in + n_out:]
        rider.start(ins, outs, sems)
        if rider.mid is not None:
            rider.mid(ins, outs, sems)
        rider.end(ins, outs, sems)

    any_spec = pl.BlockSpec(memory_space=pl.ANY)
    outs = pl.pallas_call(
        body, name=name, out_shape=list(rider.out_shapes), in_specs=[any_spec] * n_in, out_specs=[any_spec] * n_out,
        scratch_shapes=list(rider.sems), input_output_aliases=dict(rider.aliases),
        compiler_params=pltpu.CompilerParams(has_side_effects=True),
    )(*rider.ins)
    return list(outs)


def _swap_cores_inplace(bufs, name):
    n = len(bufs)
    nck = [_n_chunks(a.shape[2], a.shape[3] * a.dtype.itemsize) for a in bufs]
    base = [sum(a.shape[0] * k for a, k in zip(bufs[:i], nck[:i])) for i in range(n)]
    nsem = sum(a.shape[0] * k for a, k in zip(bufs, nck))

    def body(*refs):
        outs = refs[n:2 * n]
        send, recv = refs[2 * n:]
        x, y, c = lax.axis_index("x"), lax.axis_index("y"), lax.axis_index("c")

        def copies(core):
            for i in range(n):
                ch = bufs[i].shape[2] // nck[i]
                for p in range(bufs[i].shape[0]):
                    for j in range(nck[i]):
                        k = base[i] + p * nck[i] + j
                        ref = outs[i].at[p, core, pl.ds(j * ch, ch)]
                        yield pltpu.make_async_remote_copy(src_ref=ref, dst_ref=ref, send_sem=send.at[k], recv_sem=recv.at[k],
                                                           device_id=(x, y, 1 - c), device_id_type=MESH)

        for cp in copies(c):
            cp.start()
        for cp in copies(c):
            cp.wait_send()
        for cp in copies(1 - c):
            cp.wait_recv()

    any_spec = pl.BlockSpec(memory_space=pl.ANY)
    outs = pl.pallas_call(
        body, name=name, out_shape=[jax.ShapeDtypeStruct(a.shape, a.dtype) for a in bufs], in_specs=[any_spec] * n,
        out_specs=[any_spec] * n, scratch_shapes=[pltpu.SemaphoreType.DMA((nsem,))] * 2,
        input_output_aliases={i: i for i in range(n)}, compiler_params=pltpu.CompilerParams(has_side_effects=True),
    )(*bufs)
    return list(outs)


def _place_cast(w, row0, rows, slot, slots, name):
    C = w.shape[1]
    tr = _row_block(rows, C)
    blk0 = row0 // tr

    def body(slot_ref, w_ref, o_ref):
        del slot_ref
        o_ref[...] = w_ref[...].astype(BF16)

    return pl.pallas_call(
        body, name=name,
        grid_spec=pltpu.PrefetchScalarGridSpec(
            num_scalar_prefetch=1, grid=(rows // tr,), in_specs=[pl.BlockSpec((tr, C), lambda i, sr: (blk0 + i, 0))],
            out_specs=pl.BlockSpec((None, tr, C), lambda i, sr: (sr[0], i, 0))),
        out_shape=jax.ShapeDtypeStruct((slots, rows, C), BF16),
        compiler_params=_cparams("parallel"),
    )(slot.reshape(1).astype(jnp.int32), w)


def _swap_halves(arrs, name):
    n = len(arrs)
    S = arrs[0].shape[0]
    halves = [a.shape[1] // 2 for a in arrs]
    nck = [_n_chunks(h, a.shape[2] * a.dtype.itemsize) for h, a in zip(halves, arrs)]
    base = [S * sum(nck[:i]) for i in range(n)]
    nsem = S * sum(nck)

    def body(*refs):
        ins, outs = refs[:n], refs[n:2 * n]
        send, recv = refs[2 * n:]
        x, y, c = lax.axis_index("x"), lax.axis_index("y"), lax.axis_index("c")
        copies = []
        for i in range(n):
            ch = halves[i] // nck[i]
            for s in range(S):
                for j in range(nck[i]):
                    k = base[i] + s * nck[i] + j
                    src = ins[i].at[s, pl.ds(pl.multiple_of((1 - c) * halves[i] + j * ch, 16), ch)]
                    cp = pltpu.make_async_remote_copy(src_ref=src, dst_ref=outs[i].at[s, pl.ds(j * ch, ch)], send_sem=send.at[k],
                                                      recv_sem=recv.at[k], device_id=(x, y, 1 - c), device_id_type=MESH)
                    cp.start()
                    copies.append(cp)
        for cp in copies:
            cp.wait()

    any_spec = pl.BlockSpec(memory_space=pl.ANY)
    outs = pl.pallas_call(
        body, name=name, out_shape=[jax.ShapeDtypeStruct((S, h, a.shape[2]), a.dtype) for h, a in zip(halves, arrs)],
        in_specs=[any_spec] * n, out_specs=[any_spec] * n,
        scratch_shapes=[pltpu.SemaphoreType.DMA((nsem,))] * 2,
        compiler_params=pltpu.CompilerParams(has_side_effects=True),
    )(*arrs)
    return list(outs)


def _pair_sum(g, r, core, name):
    S, rows, C = g.shape
    half = rows // 2
    tr = _row_block(half, C)
    nb = half // tr

    def body(core_ref, g_ref, r_ref, o_ref):
        del core_ref
        o_ref[...] = (g_ref[...].astype(F32) + r_ref[...].astype(F32)).astype(BF16)

    blk = pl.BlockSpec((None, tr, C), lambda s, i, cr: (s, i, 0))
    return pl.pallas_call(
        body, name=name,
        grid_spec=pltpu.PrefetchScalarGridSpec(
            num_scalar_prefetch=1, grid=(S, nb),
            in_specs=[pl.BlockSpec((None, tr, C), lambda s, i, cr: (s, cr[0] * nb + i, 0)), blk], out_specs=blk),
        out_shape=jax.ShapeDtypeStruct((S, half, C), BF16),
        compiler_params=_cparams("parallel", "parallel"),
    )(core.reshape(1).astype(jnp.int32), g, r)


def _ffn_fwd(h, mod, g, w1, w3, w2, k, dm, name, rider=None):
    T, D = h.shape
    S, F = w1.shape[0], w1.shape[-1]
    tm = dm.tm
    r0 = 6 if k else 0
    grp = _grp(dm)

    def body(h_ref, mod_ref, g_ref, w1_ref, w3_ref, w2_ref, ho_ref, a_ref, b_ref, hn_ref, y_ref, hn_s, acc):
        s = pl.program_id(1)

        @pl.when(s == 0)
        def _():
            hn = _pre(h_ref[...], g_ref[...], mod_ref[0, r0:r0 + 1, :], mod_ref[0, r0 + 1:r0 + 2, :]).astype(BF16)
            hn_s[...] = hn
            hn_ref[...] = hn
            acc[...] = jnp.zeros_like(acc)

        hn = hn_s[...]
        a = _dot(hn, w1_ref[...])
        b = _dot(hn, w3_ref[...])
        a_ref[0] = a.astype(BF16)
        b_ref[0] = b.astype(BF16)
        sw = (a * jax.nn.sigmoid(a) * b).astype(BF16)
        acc[...] += _dot(sw, w2_ref[...])

        @pl.when(s == S - 1)
        def _():
            y = acc[...]
            y_ref[...] = y.astype(BF16)
            ho_ref[...] = h_ref[...] + 0.5 * mod_ref[0, r0 + 2:r0 + 3, :] * y

    row = pl.BlockSpec((tm, D), lambda i, s: (i, 0))
    wcol = pl.BlockSpec((None, None, D, F), lambda i, s: (s, k, 0, 0))
    wrow = pl.BlockSpec((None, None, F, D), lambda i, s: (s, k, 0, 0))
    ab = pl.BlockSpec((1, tm, F), lambda i, s: (s, i, 0))
    return _hosted(
        body, rider, name=name, grid=(T // tm, S),
        in_specs=[row, pl.BlockSpec((1, 9, D), lambda i, s: (grp(i), 0, 0)), pl.BlockSpec((1, D), lambda i, s: (0, 0)),
                  wcol, wcol, wrow],
        out_specs=[row, ab, ab, row, row],
        out_shape=[jax.ShapeDtypeStruct((T, D), F32), jax.ShapeDtypeStruct((S, T, F), BF16),
                   jax.ShapeDtypeStruct((S, T, F), BF16), jax.ShapeDtypeStruct((T, D), BF16),
                   jax.ShapeDtypeStruct((T, D), BF16)],
        scratch_shapes=[pltpu.VMEM((tm, D), BF16), pltpu.VMEM((tm, D), F32)],
        sem=("parallel", "arbitrary"), args=(h, mod, g, w1, w3, w2))


def _ffn_bwd(dh, h, mod, g, y, a, b, w1, w3, w2, k, dm, name, rider=None):
    T, D = h.shape
    S, F = w1.shape[0], w1.shape[-1]
    tm = dm.tm
    r0 = 6 if k else 0
    grp = _grp(dm)

    def body(dh_ref, h_ref, mod_ref, g_ref, y_ref, a_ref, b_ref, w1_ref, w3_ref, w2_ref,
             dho_ref, da_ref, db_ref, sw_ref, dy_ref, part_ref, dy_s, acc):
        s = pl.program_id(1)

        @pl.when(s == 0)
        def _():
            dy = (0.5 * mod_ref[0, r0 + 2:r0 + 3, :] * dh_ref[...]).astype(BF16)
            dy_s[...] = dy
            dy_ref[...] = dy
            acc[...] = jnp.zeros_like(acc)

        ds = _dot_nt(dy_s[...], w2_ref[...])
        av = a_ref[0].astype(F32)
        bv = b_ref[0].astype(F32)
        sig = jax.nn.sigmoid(av)
        sil = av * sig
        sw_ref[0] = (sil * bv).astype(BF16)
        db = (ds * sil).astype(BF16)
        da = (ds * bv * (sig * (1.0 + av * (1.0 - sig)))).astype(BF16)
        da_ref[0] = da
        db_ref[0] = db
        acc[...] += _dot_nt(da, w1_ref[...]) + _dot_nt(db, w3_ref[...])

        @pl.when(s == S - 1)
        def _():
            dhv = dh_ref[...]
            dhb, dshift, dscale, dg = _pre_bwd(acc[...], h_ref[...], g_ref[...], mod_ref[0, r0 + 1:r0 + 2, :])
            dho_ref[...] = dhv + dhb
            dgate = 0.5 * jnp.sum(dhv * y_ref[...].astype(F32), axis=0, keepdims=True)
            _write_part(part_ref, dshift, dscale, dgate, dg)

    row = pl.BlockSpec((tm, D), lambda i, s: (i, 0))
    wcol = pl.BlockSpec((None, None, D, F), lambda i, s: (s, k, 0, 0))
    wrow = pl.BlockSpec((None, None, F, D), lambda i, s: (s, k, 0, 0))
    ab = pl.BlockSpec((1, tm, F), lambda i, s: (s, i, 0))
    stf = jax.ShapeDtypeStruct((S, T, F), BF16)
    return _hosted(
        body, rider, name=name, grid=(T // tm, S),
        in_specs=[row, row, pl.BlockSpec((1, 9, D), lambda i, s: (grp(i), 0, 0)), pl.BlockSpec((1, D), lambda i, s: (0, 0)),
                  row, ab, ab, wcol, wcol, wrow],
        out_specs=[row, ab, ab, ab, row, pl.BlockSpec((1, 8, D), lambda i, s: (i, 0, 0))],
        out_shape=[jax.ShapeDtypeStruct((T, D), F32), stf, stf, stf, jax.ShapeDtypeStruct((T, D), BF16),
                   jax.ShapeDtypeStruct((T // tm, 8, D), F32)],
        scratch_shapes=[pltpu.VMEM((tm, D), BF16), pltpu.VMEM((tm, D), F32)],
        sem=("parallel", "arbitrary"), args=(dh, h, mod, g, y, a, b, w1, w3, w2))


def _mm_tn(a, b, a_spec, b_spec, out_shape, out_spec, grid, name):
    nk = grid[-1]
    kax = len(grid) - 1
    blk = tuple(d for d in out_spec.block_shape if d is not None)

    def body(a_ref, b_ref, o_ref, acc):
        kk = pl.program_id(kax)

        @pl.when(kk == 0)
        def _():
            acc[...] = jnp.zeros_like(acc)

        acc[...] += _dot_tn(a_ref[...].astype(BF16), b_ref[...].astype(BF16))

        @pl.when(kk == nk - 1)
        def _():
            o_ref[...] = acc[...].astype(o_ref.dtype)

    return pl.pallas_call(
        body, name=name, grid=grid,
        in_specs=[a_spec, b_spec], out_specs=out_spec, out_shape=jax.ShapeDtypeStruct(out_shape, BF16),
        scratch_shapes=[pltpu.VMEM(blk, F32)],
        compiler_params=_cparams(*(["parallel"] * kax + ["arbitrary"])),
    )(a, b)


def _sc_in_fwd(h, mod, g, w_in, j, dm, name):
    T, D = h.shape
    tm = dm.tm
    wq = D // N_CHIPS
    nq = 3 * N_CHIPS
    grp = _grp(dm)

    def body(h_ref, mod_ref, g_ref, w_ref, p_ref, hn_ref, hn_s):
        @pl.when(pl.program_id(1) == 0)
        def _():
            hn = _pre(h_ref[...], g_ref[...], mod_ref[0, 3:4, :], mod_ref[0, 4:5, :]).astype(BF16)
            hn_s[...] = hn
            hn_ref[...] = hn

        p_ref[...] = _dot(hn_s[...], w_ref[...])

    row = pl.BlockSpec((tm, D), lambda i, q: (i, 0))
    return pl.pallas_call(
        body, name=name, grid=(T // tm, nq),
        in_specs=[row, pl.BlockSpec((1, 9, D), lambda i, q: (grp(i), 0, 0)), pl.BlockSpec((1, D), lambda i, q: (0, 0)),
                  pl.BlockSpec((None, None, D, wq), lambda i, q: (q // 3, j, 0, q % 3))],
        out_specs=[pl.BlockSpec((None, tm, wq), lambda i, q: (q // N_CHIPS, i, q % N_CHIPS)), row],
        out_shape=[jax.ShapeDtypeStruct((3, T, D), F32), jax.ShapeDtypeStruct((T, D), BF16)],
        scratch_shapes=[pltpu.VMEM((tm, D), BF16)],
        compiler_params=_cparams("parallel", "arbitrary"),
    )(h, mod, g, w_in)


def _conv_cols(dm):
    return 256 if dm.D % 256 == 0 else 128


def _seg_masks(r, dm):
    bn = dm.B * dm.N
    lat = r < bn
    off = jnp.where(lat, lax.rem(r, dm.N), lax.rem(r - bn, dm.CTX))
    seg = jnp.where(lat, dm.N, dm.CTX)
    inside = (r >= 0) & (r < dm.T)
    return ((off != 0) & inside).astype(F32), ((off != seg - 1) & inside).astype(F32)


def _conv_specs(dm):
    tb, cb, nr8 = dm.tm, _conv_cols(dm), dm.T // 8
    prev8 = lambda c, i: jnp.maximum(i * (tb // 8) - 1, 0)
    next8 = lambda c, i: jnp.minimum((i + 1) * (tb // 8), nr8 - 1)
    return dict(
        tb=tb, cb=cb,
        p=pl.BlockSpec((3, tb, cb), lambda c, i: (0, i, c)),
        p_prev=pl.BlockSpec((3, 8, cb), lambda c, i: (0, prev8(c, i), c)),
        p_next=pl.BlockSpec((3, 8, cb), lambda c, i: (0, next8(c, i), c)),
        row=pl.BlockSpec((tb, cb), lambda c, i: (i, c)),
        row_prev=pl.BlockSpec((8, cb), lambda c, i: (prev8(c, i), c)),
        row_next=pl.BlockSpec((8, cb), lambda c, i: (next8(c, i), c)),
        w=pl.BlockSpec((3, cb), lambda c, i: (0, c)),
    )


def _shift_rows(x, before, after, tb):
    rid = lax.broadcasted_iota(jnp.int32, x.shape, 0)
    down = jnp.where(rid == 0, before, pltpu.roll(x, 1, 0))
    up = jnp.where(rid == tb - 1, after, pltpu.roll(x, tb - 1, 0))
    return down, up


def _conv_fwd(p, wc, dm, name):
    T, D = dm.T, dm.D
    sp = _conv_specs(dm)
    tb, cb = sp["tb"], sp["cb"]

    def body(p_ref, pp_ref, pn_ref, w_ref, z_ref):
        r = pl.program_id(1) * tb + lax.broadcasted_iota(jnp.int32, (tb, cb), 0)
        mp, mn = _seg_masks(r, dm)
        cu = p_ref[1] * p_ref[2]
        prev, nxt = _shift_rows(cu, pp_ref[1, 7:8, :] * pp_ref[2, 7:8, :], pn_ref[1, 0:1, :] * pn_ref[2, 0:1, :], tb)
        conv = w_ref[0:1, :] * (prev * mp) + w_ref[1:2, :] * cu + w_ref[2:3, :] * (nxt * mn)
        z_ref[...] = (p_ref[0] * conv).astype(BF16)

    return pl.pallas_call(
        body, name=name, grid=(D // cb, T // tb),
        in_specs=[sp["p"], sp["p_prev"], sp["p_next"], sp["w"]], out_specs=sp["row"],
        out_shape=jax.ShapeDtypeStruct((T, D), BF16),
        compiler_params=_cparams("parallel", "parallel"),
    )(p, p, p, wc)


def _conv_bwd(dz, p, wc, dm, name):
    T, D = dm.T, dm.D
    sp = _conv_specs(dm)
    tb, cb = sp["tb"], sp["cb"]

    def body(dz_ref, dzp_ref, dzn_ref, p_ref, pp_ref, pn_ref, w_ref, dp_ref, dw_ref):
        i = pl.program_id(1)
        r = i * tb + lax.broadcasted_iota(jnp.int32, (tb, cb), 0)
        mp, mn = _seg_masks(r, dm)
        rb = i * tb + lax.broadcasted_iota(jnp.int32, (1, cb), 0)
        _, mn_before = _seg_masks(rb - 1, dm)
        mp_after, _ = _seg_masks(rb + tb, dm)
        bg, cg, u = p_ref[0], p_ref[1], p_ref[2]
        cu = cg * u
        prev, nxt = _shift_rows(cu, pp_ref[1, 7:8, :] * pp_ref[2, 7:8, :], pn_ref[1, 0:1, :] * pn_ref[2, 0:1, :], tb)
        prev = prev * mp
        nxt = nxt * mn
        w0, w1, w2 = w_ref[0:1, :], w_ref[1:2, :], w_ref[2:3, :]
        conv = w0 * prev + w1 * cu + w2 * nxt
        dz = dz_ref[...]
        dp_ref[0] = dz * conv
        dconv = dz * bg

        @pl.when(i == 0)
        def _():
            dw_ref[...] = jnp.zeros_like(dw_ref)

        dw_ref[0:1, :] += jnp.sum(dconv * prev, axis=0, keepdims=True)
        dw_ref[1:2, :] += jnp.sum(dconv * cu, axis=0, keepdims=True)
        dw_ref[2:3, :] += jnp.sum(dconv * nxt, axis=0, keepdims=True)
        dconv_before = dzp_ref[7:8, :] * pp_ref[0, 7:8, :] * mn_before
        dconv_after = dzn_ref[0:1, :] * pn_ref[0, 0:1, :] * mp_after
        from_prev, _ = _shift_rows(dconv * mn, dconv_before, dconv_after, tb)
        _, from_next = _shift_rows(dconv * mp, dconv_before, dconv_after, tb)
        dcu = w1 * dconv + w0 * from_next + w2 * from_prev
        dp_ref[1] = dcu * u
        dp_ref[2] = dcu * cg

    return pl.pallas_call(
        body, name=name, grid=(D // cb, T // tb),
        in_specs=[sp["row"], sp["row_prev"], sp["row_next"], sp["p"], sp["p_prev"], sp["p_next"], sp["w"]],
        out_specs=[sp["p"], sp["w"]],
        out_shape=[jax.ShapeDtypeStruct((3, T, D), F32), jax.ShapeDtypeStruct((3, D), F32)],
        compiler_params=_cparams("parallel", "arbitrary"),
    )(dz, dz, dz, p, p, p, wc)


def _out_fwd(z, w, h, mod, j, dm, name):
    T, D = h.shape
    K = z.shape[1]
    tm = dm.tm
    grp = _grp(dm)

    def body(z_ref, w_ref, h_ref, mod_ref, ho_ref, y_ref):
        y = _dot(z_ref[...], w_ref[...])
        y_ref[...] = y.astype(BF16)
        ho_ref[...] = h_ref[...] + mod_ref[0, 5:6, :] * y

    row = pl.BlockSpec((tm, D), lambda i: (i, 0))
    return pl.pallas_call(
        body, name=name, grid=(T // tm,),
        in_specs=[pl.BlockSpec((tm, K), lambda i: (i, 0)), pl.BlockSpec((None, K, D), lambda i: (j, 0, 0)), row,
                  pl.BlockSpec((1, 9, D), lambda i: (grp(i), 0, 0))],
        out_specs=[row, row],
        out_shape=[jax.ShapeDtypeStruct((T, D), F32), jax.ShapeDtypeStruct((T, D), BF16)],
        compiler_params=_cparams("parallel"),
    )(z, w, h, mod)


def _out_bwd(dh, y, w, mod, j, dm, name):
    T, D = dh.shape
    K = w.shape[1]
    tm = dm.tm
    grp = _grp(dm)

    def body(dh_ref, y_ref, w_ref, mod_ref, dy_ref, dz_ref, part_ref):
        dhv = dh_ref[...]
        dy = (mod_ref[0, 5:6, :] * dhv).astype(BF16)
        dy_ref[...] = dy
        dz_ref[...] = _dot_nt(dy, w_ref[...])
        _write_part(part_ref, dgate=jnp.sum(dhv * y_ref[...].astype(F32), axis=0, keepdims=True))

    row = pl.BlockSpec((tm, D), lambda i: (i, 0))
    return pl.pallas_call(
        body, name=name, grid=(T // tm,),
        in_specs=[row, row, pl.BlockSpec((None, K, D), lambda i: (j, 0, 0)), pl.BlockSpec((1, 9, D), lambda i: (grp(i), 0, 0))],
        out_specs=[row, pl.BlockSpec((tm, K), lambda i: (i, 0)), pl.BlockSpec((1, 8, D), lambda i: (i, 0, 0))],
        out_shape=[jax.ShapeDtypeStruct((T, D), BF16), jax.ShapeDtypeStruct((T, K), F32),
                   jax.ShapeDtypeStruct((T // tm, 8, D), F32)],
        compiler_params=_cparams("parallel"),
    )(dh, y, w, mod)


def _sc_in_bwd(dh, dp, h, mod, g, w_in, j, dm, name):
    T, D = h.shape
    tm = dm.tm
    wq = D // N_CHIPS
    nq = 3 * N_CHIPS
    grp = _grp(dm)

    def body(dh_ref, dp_ref, h_ref, mod_ref, g_ref, w_ref, dho_ref, part_ref, acc):
        q = pl.program_id(1)

        @pl.when(q == 0)
        def _():
            acc[...] = jnp.zeros_like(acc)

        acc[...] += _dot_nt(dp_ref[...].astype(BF16), w_ref[...])

        @pl.when(q == nq - 1)
        def _():
            dhb, dshift, dscale, dg = _pre_bwd(acc[...], h_ref[...], g_ref[...], mod_ref[0, 4:5, :])
            dho_ref[...] = dh_ref[...] + dhb
            _write_part(part_ref, dshift, dscale, None, dg)

    row = pl.BlockSpec((tm, D), lambda i, q: (i, 0))
    return pl.pallas_call(
        body, name=name, grid=(T // tm, nq),
        in_specs=[row, pl.BlockSpec((None, tm, wq), lambda i, q: (q // N_CHIPS, i, q % N_CHIPS)), row,
                  pl.BlockSpec((1, 9, D), lambda i, q: (grp(i), 0, 0)), pl.BlockSpec((1, D), lambda i, q: (0, 0)),
                  pl.BlockSpec((None, None, D, wq), lambda i, q: (q // 3, j, 0, q % 3))],
        out_specs=[row, pl.BlockSpec((1, 8, D), lambda i, q: (i, 0, 0))],
        out_shape=[jax.ShapeDtypeStruct((T, D), F32), jax.ShapeDtypeStruct((T // tm, 8, D), F32)],
        scratch_shapes=[pltpu.VMEM((tm, D), F32)],
        compiler_params=_cparams("parallel", "arbitrary"),
    )(dh, dp, h, mod, g, w_in)


def _rope(t, c, s1, s2):
    return t * c + pltpu.roll(t, HEAD_PAD - 16, 1) * s1 + pltpu.roll(t, 16, 1) * s2


def _rope_t(dy, c, s1, s2):
    return dy * c + pltpu.roll(dy * s1, 16, 1) + pltpu.roll(dy * s2, HEAD_PAD - 16, 1)


def _mla_heads_fwd(z, g_ref, wuq_ref, wukv_ref):
    cq, ckv, krp = z[:, :Q_LORA], z[:, Q_LORA:Q_LORA + KV_LORA], z[:, Q_LORA + KV_LORA:]
    cqh, rq = _rms(cq, Q_LORA)
    ckvh, rkv = _rms(ckv, KV_LORA)
    cqn = (cqh * g_ref[0:1, :]).astype(BF16)
    ckvn = (ckvh * g_ref[1:2, :KV_LORA]).astype(BF16)
    qraw = _dot(cqn, wuq_ref[...])
    kvraw = _dot(ckvn, wukv_ref[...])
    return dict(krp=krp, cqh=cqh, rq=rq, ckvh=ckvh, rkv=rkv, cqn=cqn, ckvn=ckvn, qraw=qraw, kvraw=kvraw)


def _mla_proj_fwd(h, mod, g, gains, tabs, w_a, w_uq, w_ukv, j, dm, name):
    T, D = h.shape
    tm = min(dm.tm, 256)
    grp = lambda i: jnp.minimum(i // (dm.N // tm), dm.B)
    HP = HEAD_PAD

    def body(h_ref, mod_ref, g_ref, gn_ref, tab_ref, wa_ref, wuq_ref, wukv_ref, hn_ref, q_ref, k_ref, v_ref):
        hn = _pre(h_ref[...], g_ref[...], mod_ref[0, 3:4, :], mod_ref[0, 4:5, :]).astype(BF16)
        hn_ref[...] = hn
        f = _mla_heads_fwd(_dot(hn, wa_ref[...]), gn_ref, wuq_ref, wukv_ref)
        c, s1, s2 = tab_ref[0], tab_ref[1], tab_ref[2]
        for hd in range(HEADS):
            qh, _ = _rms(f["qraw"][:, hd * HP:(hd + 1) * HP], QK_HEAD)
            q_ref[:, hd * HP:(hd + 1) * HP] = _rope(qh * gn_ref[2:3, :], c, s1, s2).astype(BF16)
            kpre = jnp.concatenate([f["kvraw"][:, hd * HP:hd * HP + QK_NOPE], f["krp"]], axis=1)
            kh, _ = _rms(kpre, QK_HEAD)
            k_ref[:, hd * HP:(hd + 1) * HP] = _rope(kh * gn_ref[3:4, :], c, s1, s2).astype(BF16)
            v_ref[:, hd * V_HEAD:(hd + 1) * V_HEAD] = f["kvraw"][:, hd * HP + QK_NOPE:(hd + 1) * HP].astype(BF16)

    row = pl.BlockSpec((tm, D), lambda i: (i, 0))
    HQ = HEADS * HP
    return pl.pallas_call(
        body, name=name, grid=(T // tm,),
        in_specs=[row, pl.BlockSpec((1, 9, D), lambda i: (grp(i), 0, 0)), pl.BlockSpec((1, D), lambda i: (0, 0)),
                  pl.BlockSpec((None, 8, HP), lambda i: (j, 0, 0)), pl.BlockSpec((3, tm, HP), lambda i: (0, i, 0)),
                  pl.BlockSpec((None, D, 512), lambda i: (j, 0, 0)), pl.BlockSpec((None, Q_LORA, HQ), lambda i: (j, 0, 0)),
                  pl.BlockSpec((None, KV_LORA, HQ), lambda i: (j, 0, 0))],
        out_specs=[row, pl.BlockSpec((tm, HQ), lambda i: (i, 0)), pl.BlockSpec((tm, HQ), lambda i: (i, 0)),
                   pl.BlockSpec((tm, HEADS * V_HEAD), lambda i: (i, 0))],
        out_shape=[jax.ShapeDtypeStruct((T, D), BF16), jax.ShapeDtypeStruct((T, HQ), BF16),
                   jax.ShapeDtypeStruct((T, HQ), BF16), jax.ShapeDtypeStruct((T, HEADS * V_HEAD), BF16)],
        compiler_params=_cparams("parallel"),
    )(h, mod, g, gains, tabs, w_a, w_uq, w_ukv)


def _mla_proj_bwd(dh, dq, dk, dv, h, mod, g, gains, tabs, w_a, w_uq, w_ukv, j, dm, name):
    T, D = h.shape
    tm = min(dm.tm, 256)
    nblk = T // tm
    grp = lambda i: jnp.minimum(i // (dm.N // tm), dm.B)
    HP = HEAD_PAD
    HQ = HEADS * HP

    def body(dh_ref, dq_ref, dk_ref, dv_ref, h_ref, mod_ref, g_ref, gn_ref, tab_ref, wa_ref, wuq_ref, wukv_ref,
             dho_ref, part_ref, gwa_ref, gwuq_ref, gwukv_ref, dgn_ref, dqraw_s, dkvraw_s):
        i = pl.program_id(0)

        @pl.when(i == 0)
        def _():
            gwa_ref[...] = jnp.zeros_like(gwa_ref)
            gwuq_ref[...] = jnp.zeros_like(gwuq_ref)
            gwukv_ref[...] = jnp.zeros_like(gwukv_ref)
            dgn_ref[...] = jnp.zeros_like(dgn_ref)

        hv = h_ref[...]
        hn = _pre(hv, g_ref[...], mod_ref[0, 3:4, :], mod_ref[0, 4:5, :]).astype(BF16)
        f = _mla_heads_fwd(_dot(hn, wa_ref[...]), gn_ref, wuq_ref, wukv_ref)
        c, s1, s2 = tab_ref[0], tab_ref[1], tab_ref[2]
        gq, gk = gn_ref[2:3, :], gn_ref[3:4, :]
        dgq = jnp.zeros((1, HP), F32)
        dgk = jnp.zeros((1, HP), F32)
        dkrp = jnp.zeros((tm, HP - QK_NOPE), F32)
        for hd in range(HEADS):
            qh, rq = _rms(f["qraw"][:, hd * HP:(hd + 1) * HP], QK_HEAD)
            dqn = _rope_t(dq_ref[:, hd * HP:(hd + 1) * HP], c, s1, s2)
            dgq = dgq + jnp.sum(dqn * qh, axis=0, keepdims=True)
            dqraw_s[:, hd * HP:(hd + 1) * HP] = _rms_bwd(dqn * gq, qh, rq, QK_HEAD)
            kpre = jnp.concatenate([f["kvraw"][:, hd * HP:hd * HP + QK_NOPE], f["krp"]], axis=1)
            kh, rk = _rms(kpre, QK_HEAD)
            dkn = _rope_t(dk_ref[:, hd * HP:(hd + 1) * HP], c, s1, s2)
            dgk = dgk + jnp.sum(dkn * kh, axis=0, keepdims=True)
            dkpre = _rms_bwd(dkn * gk, kh, rk, QK_HEAD)
            dkvraw_s[:, hd * HP:hd * HP + QK_NOPE] = dkpre[:, :QK_NOPE]
            dkrp = dkrp + dkpre[:, QK_NOPE:]
            dkvraw_s[:, hd * HP + QK_NOPE:(hd + 1) * HP] = dv_ref[:, hd * V_HEAD:(hd + 1) * V_HEAD]
        dqraw = dqraw_s[...].astype(BF16)
        dkvraw = dkvraw_s[...].astype(BF16)
        gwuq_ref[...] += _dot_tn(f["cqn"], dqraw)
        gwukv_ref[...] += _dot_tn(f["ckvn"], dkvraw)
        dcqn = _dot_nt(dqraw, wuq_ref[...])
        dckvn = _dot_nt(dkvraw, wukv_ref[...])
        dgqa = jnp.sum(dcqn * f["cqh"], axis=0, keepdims=True)
        dgkva = jnp.sum(dckvn * f["ckvh"], axis=0, keepdims=True)
        dcq = _rms_bwd(dcqn * gn_ref[0:1, :], f["cqh"], f["rq"], Q_LORA)
        dckv = _rms_bwd(dckvn * gn_ref[1:2, :KV_LORA], f["ckvh"], f["rkv"], KV_LORA)
        dz = jnp.concatenate([dcq, dckv, dkrp], axis=1).astype(BF16)
        gwa_ref[...] += _dot_tn(hn, dz)
        dhn = _dot_nt(dz, wa_ref[...])
        dhb, dshift, dscale, dg = _pre_bwd(dhn, hv, g_ref[...], mod_ref[0, 4:5, :])
        dho_ref[...] = dh_ref[...] + dhb
        _write_part(part_ref, dshift, dscale, None, dg)
        dgn_ref[0:1, :] += dgqa
        dgn_ref[1:2, :KV_LORA] += dgkva
        dgn_ref[2:3, :] += dgq
        dgn_ref[3:4, :] += dgk

    row = pl.BlockSpec((tm, D), lambda i: (i, 0))
    wide = pl.BlockSpec((tm, HQ), lambda i: (i, 0))
    const2 = lambda i: (0, 0)
    return pl.pallas_call(
        body, name=name, grid=(nblk,),
        in_specs=[row, wide, wide, pl.BlockSpec((tm, HEADS * V_HEAD), lambda i: (i, 0)), row,
                  pl.BlockSpec((1, 9, D), lambda i: (grp(i), 0, 0)), pl.BlockSpec((1, D), const2),
                  pl.BlockSpec((None, 8, HP), lambda i: (j, 0, 0)), pl.BlockSpec((3, tm, HP), lambda i: (0, i, 0)),
                  pl.BlockSpec((None, D, 512), lambda i: (j, 0, 0)), pl.BlockSpec((None, Q_LORA, HQ), lambda i: (j, 0, 0)),
                  pl.BlockSpec((None, KV_LORA, HQ), lambda i: (j, 0, 0))],
        out_specs=[row, pl.BlockSpec((1, 8, D), lambda i: (i, 0, 0)), pl.BlockSpec((D, 512), const2),
                   pl.BlockSpec((Q_LORA, HQ), const2), pl.BlockSpec((KV_LORA, HQ), const2), pl.BlockSpec((8, HP), const2)],
        out_shape=[jax.ShapeDtypeStruct((T, D), F32), jax.ShapeDtypeStruct((nblk, 8, D), F32),
                   jax.ShapeDtypeStruct((D, 512), F32), jax.ShapeDtypeStruct((Q_LORA, HQ), F32),
                   jax.ShapeDtypeStruct((KV_LORA, HQ), F32), jax.ShapeDtypeStruct((8, HP), F32)],
        scratch_shapes=[pltpu.VMEM((tm, HQ), F32), pltpu.VMEM((tm, HQ), F32)],
        compiler_params=_cparams("arbitrary"),
    )(dh, dq, dk, dv, h, mod, g, gains, tabs, w_a, w_uq, w_ukv)


def _attn_specs(dm):
    tq = dm.CTX
    nq = dm.N // tq
    cblk0 = dm.B * nq
    HP = HEAD_PAD
    qrow = lambda b, i: jnp.where(i < nq, b * nq + i, cblk0 + b)
    return dict(
        tq=tq, nq=nq,
        q=pl.BlockSpec((tq, HP), lambda b, hd, i: (qrow(b, i), hd)),
        k_lat=pl.BlockSpec((dm.N, HP), lambda b, hd, i: (b, hd)),
        k_ctx=pl.BlockSpec((tq, HP), lambda b, hd, i: (cblk0 + b, hd)),
        v_lat=pl.BlockSpec((dm.N, V_HEAD), lambda b, hd, i: (b, hd)),
        v_ctx=pl.BlockSpec((tq, V_HEAD), lambda b, hd, i: (cblk0 + b, hd)),
        o=pl.BlockSpec((tq, V_HEAD), lambda b, hd, i: (qrow(b, i), hd)),
    )


def _attn_probs(q, kl, kc, is_ctx):
    sl = _dot_nt(q, kl) * QK_SCALE
    sc = _dot_nt(q, kc) * QK_SCALE
    sl = sl + jnp.where(is_ctx, NEG, 0.0)
    m = jnp.maximum(jnp.max(sl, axis=-1, keepdims=True), jnp.max(sc, axis=-1, keepdims=True))
    pl_, pc = jnp.exp(sl - m), jnp.exp(sc - m)
    inv = 1.0 / (jnp.sum(pl_, axis=-1, keepdims=True) + jnp.sum(pc, axis=-1, keepdims=True))
    return pl_ * inv, pc * inv


def _attn_fwd(q, k, v, dm, name):
    T = dm.T
    sp = _attn_specs(dm)
    nq = sp["nq"]

    def body(q_ref, kl_ref, kc_ref, vl_ref, vc_ref, o_ref):
        is_ctx = pl.program_id(2) == nq
        pl_, pc = _attn_probs(q_ref[...], kl_ref[...], kc_ref[...], is_ctx)
        o_ref[...] = (_dot(pl_.astype(BF16), vl_ref[...]) + _dot(pc.astype(BF16), vc_ref[...])).astype(BF16)

    return pl.pallas_call(
        body, name=name, grid=(dm.B, HEADS, nq + 1),
        in_specs=[sp["q"], sp["k_lat"], sp["k_ctx"], sp["v_lat"], sp["v_ctx"]], out_specs=sp["o"],
        out_shape=jax.ShapeDtypeStruct((T, HEADS * V_HEAD), BF16),
        compiler_params=_cparams("parallel", "parallel", "arbitrary"),
    )(q, k, k, v, v)


def _attn_bwd(q, k, v, o, do, dm, name):
    T = dm.T
    sp = _attn_specs(dm)
    nq, tq = sp["nq"], sp["tq"]
    HP, HQ, HV = HEAD_PAD, HEADS * HEAD_PAD, HEADS * V_HEAD

    def body(q_ref, kl_ref, kc_ref, vl_ref, vc_ref, o_ref, do_ref, dq_ref, dkl_ref, dkc_ref, dvl_ref, dvc_ref):
        i = pl.program_id(2)

        @pl.when(i == 0)
        def _():
            dkl_ref[...] = jnp.zeros_like(dkl_ref)
            dkc_ref[...] = jnp.zeros_like(dkc_ref)
            dvl_ref[...] = jnp.zeros_like(dvl_ref)
            dvc_ref[...] = jnp.zeros_like(dvc_ref)

        qv = q_ref[...]
        pl_, pc = _attn_probs(qv, kl_ref[...], kc_ref[...], i == nq)
        dov = do_ref[...]
        dob = dov.astype(BF16)
        delta = jnp.sum(dov * o_ref[...].astype(F32), axis=-1, keepdims=True)
        dsl = (pl_ * (_dot_nt(dob, vl_ref[...]) - delta) * QK_SCALE).astype(BF16)
        dsc = (pc * (_dot_nt(dob, vc_ref[...]) - delta) * QK_SCALE).astype(BF16)
        dq_ref[...] = _dot(dsl, kl_ref[...]) + _dot(dsc, kc_ref[...])
        dkl_ref[...] += _dot_tn(dsl, qv)
        dkc_ref[...] += _dot_tn(dsc, qv)
        dvl_ref[...] += _dot_tn(pl_.astype(BF16), dob)
        dvc_ref[...] += _dot_tn(pc.astype(BF16), dob)

    return pl.pallas_call(
        body, name=name, grid=(dm.B, HEADS, nq + 1),
        in_specs=[sp["q"], sp["k_lat"], sp["k_ctx"], sp["v_lat"], sp["v_ctx"], sp["o"], sp["o"]],
        out_specs=[sp["q"], sp["k_lat"], pl.BlockSpec((tq, HP), lambda b, hd, i: (b, hd)),
                   sp["v_lat"], pl.BlockSpec((tq, V_HEAD), lambda b, hd, i: (b, hd))],
        out_shape=[jax.ShapeDtypeStruct((T, HQ), F32), jax.ShapeDtypeStruct((dm.B * dm.N, HQ), F32),
                   jax.ShapeDtypeStruct((dm.B * dm.CTX, HQ), F32), jax.ShapeDtypeStruct((dm.B * dm.N, HV), F32),
                   jax.ShapeDtypeStruct((dm.B * dm.CTX, HV), F32)],
        compiler_params=_cparams("parallel", "parallel", "arbitrary"),
    )(q, k, k, v, v, o, do)


def _loss_grad(h, target, dm, name):
    T, D = h.shape
    tm = dm.tm
    nlat = dm.B * dm.N // tm

    def body(h_ref, t_ref, dh_ref, ls_ref):
        lat = (pl.program_id(0) < nlat).astype(F32)
        diff = (h_ref[...] - t_ref[...]) * lat
        dh_ref[...] = diff * (1.0 / D)
        ls_ref[...] = jnp.zeros(ls_ref.shape, F32) + (0.5 / D) * jnp.sum(diff * diff)

    return pl.pallas_call(
        body, name=name, grid=(T // tm,),
        in_specs=[pl.BlockSpec((tm, D), lambda i: (i, 0)), pl.BlockSpec((tm, D), lambda i: (jnp.minimum(i, nlat - 1), 0))],
        out_specs=[pl.BlockSpec((tm, D), lambda i: (i, 0)), pl.BlockSpec((1, 8, 128), lambda i: (i, 0, 0))],
        out_shape=[jax.ShapeDtypeStruct((T, D), F32), jax.ShapeDtypeStruct((T // tm, 8, 128), F32)],
        compiler_params=_cparams("parallel"),
    )(h, target)


def _col_block(cols, target=1152):
    return max(t for t in range(128, min(cols, target) + 1, 128) if cols % t == 0)


def _mod_fwd(cond, w_mod, b_mod, name):
    L, D, C = w_mod.shape
    R = cond.shape[0]
    cb = _col_block(C)

    def body(c_ref, w_ref, b_ref, o_ref):
        cv = c_ref[...]
        sc = (cv * jax.nn.sigmoid(cv)).astype(BF16)
        o_ref[...] = _dot(sc, w_ref[...].astype(BF16)) + b_ref[...]

    return pl.pallas_call(
        body, name=name, grid=(L, C // cb),
        in_specs=[pl.BlockSpec((R, D), lambda l, c: (0, 0)), pl.BlockSpec((None, D, cb), lambda l, c: (l, 0, c)),
                  pl.BlockSpec((None, 1, cb), lambda l, c: (l, 0, c))],
        out_specs=pl.BlockSpec((None, R, cb), lambda l, c: (l, 0, c)),
        out_shape=jax.ShapeDtypeStruct((L, R, C), F32),
        compiler_params=_cparams("parallel", "parallel"),
    )(cond, w_mod, b_mod)


def _mod_bwd(cond, dmod, w_mod, name):
    L, D, C = w_mod.shape
    R = cond.shape[0]
    cb = _col_block(C)
    nc = C // cb

    def body(c_ref, dm_ref, w_ref, gw_ref, ds_ref):
        cv = c_ref[...]
        sc = (cv * jax.nn.sigmoid(cv)).astype(BF16)
        dmv = dm_ref[...].astype(BF16)
        gw_ref[...] = _dot_tn(sc, dmv)
        part = _dot_nt(dmv, w_ref[...].astype(BF16))

        @pl.when(pl.program_id(1) == 0)
        def _():
            ds_ref[...] = part

        @pl.when(pl.program_id(1) > 0)
        def _():
            ds_ref[...] += part

    return pl.pallas_call(
        body, name=name, grid=(L, nc),
        in_specs=[pl.BlockSpec((R, D), lambda l, c: (0, 0)), pl.BlockSpec((None, R, cb), lambda l, c: (l, 0, c)),
                  pl.BlockSpec((None, D, cb), lambda l, c: (l, 0, c))],
        out_specs=[pl.BlockSpec((None, D, cb), lambda l, c: (l, 0, c)), pl.BlockSpec((None, R, D), lambda l, c: (l, 0, 0))],
        out_shape=[jax.ShapeDtypeStruct((L, D, C), F32), jax.ShapeDtypeStruct((L, R, D), F32)],
        compiler_params=_cparams("parallel", "arbitrary"),
    )(cond, dmod, w_mod)


def _row_block(rows, cols, budget=1 << 20):
    best = None
    for t in range(16, rows + 1, 16):
        if rows % t == 0 and t * cols * 4 <= budget:
            best = t
    return best if best is not None else rows


def _sum_slots(recv, own, chip, core, buf, piece, name):
    S, R, C = recv.shape
    tr = _row_block(R, C, budget=512 << 10)

    def body(ids_ref, r_ref, p_ref, b_ref, o_ref):
        del b_ref
        acc = None
        for s in range(S):
            v = jnp.where(ids_ref[0] == s, p_ref[s], r_ref[s]).astype(F32)
            acc = v if acc is None else acc + v
        o_ref[...] = acc

    blk = pl.BlockSpec((S, tr, C), lambda i, ids: (0, i, 0))
    return pl.pallas_call(
        body, name=name,
        grid_spec=pltpu.PrefetchScalarGridSpec(
            num_scalar_prefetch=1, grid=(R // tr,), in_specs=[blk, blk, pl.BlockSpec(memory_space=pl.ANY)],
            out_specs=pl.BlockSpec((None, None, tr, C), lambda i, ids: (piece, ids[1], i, 0))),
        out_shape=jax.ShapeDtypeStruct(buf.shape, F32), input_output_aliases={3: 0}, compiler_params=_cparams("parallel"),
    )(jnp.stack([chip, core]).astype(jnp.int32), recv, own, buf)


def _adamw(w, gs, m, v, name):
    ng = len(gs)
    R, C = w.shape
    tr = _row_block(R, C)
    c1 = 1.0 / (1.0 - ADAM_B1 ** ADAM_STEP)
    c2 = 1.0 / (1.0 - ADAM_B2 ** ADAM_STEP)

    def body(w_ref, *refs):
        m_ref, v_ref, g_ref, d_ref, mo_ref, vo_ref = refs[ng:]
        g = refs[0][...]
        for g_more in refs[1:ng]:
            g = g + g_more[...]
        g_ref[...] = g
        mn = ADAM_B1 * m_ref[...] + (1.0 - ADAM_B1) * g
        vn = ADAM_B2 * v_ref[...] + (1.0 - ADAM_B2) * (g * g)
        mo_ref[...] = mn
        vo_ref[...] = vn
        d_ref[...] = -ADAM_LR * ((mn * c1) / (jnp.sqrt(vn * c2) + ADAM_EPS) + ADAM_WD * w_ref[...])

    blk = pl.BlockSpec((tr, C), lambda i: (i, 0))
    sd = jax.ShapeDtypeStruct((R, C), F32)
    return pl.pallas_call(
        body, name=name, grid=(R // tr,), in_specs=[blk] * (3 + ng), out_specs=[blk] * 4, out_shape=[sd] * 4,
        compiler_params=_cparams("parallel"),
    )(w, *gs, m, v)


def _rope_tables(dm):
    n = dm.N
    t = jnp.arange(n)
    r = (t // GRID_W).astype(F32)
    col = (t % GRID_W).astype(F32)
    nf = QK_ROPE // 4
    inv = ROPE_BASE ** (-jnp.arange(nf, dtype=F32) / nf)
    ang = jnp.stack([r[:, None] * inv, col[:, None] * inv], axis=1)
    cos, sin = jnp.cos(ang), jnp.sin(ang)
    zero = jnp.zeros_like(sin)
    c64 = jnp.stack([cos, cos], axis=2).reshape(n, QK_ROPE)
    s1 = jnp.stack([-sin, zero], axis=2).reshape(n, QK_ROPE)
    s2 = jnp.stack([zero, sin], axis=2).reshape(n, QK_ROPE)

    def pad(x, fill):
        return jnp.concatenate([jnp.full((n, QK_NOPE), fill, F32), x, jnp.full((n, HEAD_PAD - QK_HEAD), fill, F32)], axis=1)

    lat = jnp.stack([pad(c64, 1.0), pad(s1, 0.0), pad(s2, 0.0)])
    lat = jnp.tile(lat, (1, dm.B, 1))
    nctx = dm.B * dm.CTX
    ctx = jnp.stack([jnp.ones((nctx, HEAD_PAD), F32), jnp.zeros((nctx, HEAD_PAD), F32), jnp.zeros((nctx, HEAD_PAD), F32)])
    return jnp.concatenate([lat, ctx], axis=1)


def _fold_parts(part, dm):
    nblk = part.shape[0]
    nb = (dm.N * nblk) // dm.T
    groups = [part[b * nb:(b + 1) * nb].sum(axis=0) for b in range(dm.B)]
    groups.append(part[dm.B * nb:].sum(axis=0))
    return jnp.stack(groups)


def _flat2(a):
    return a.reshape(-1, a.shape[-1])


def kernel(x, c, ctx, c_ctx, w_mod, b_mod, g_norm, ffn_w1, ffn_w3, ffn_w2, sc_w_in, sc_conv, sc_w_out, mla_w_a, mla_g_qa, mla_w_uq, mla_g_kva, mla_w_ukv, mla_g_q, mla_g_k, mla_w_o, loss_target, m_c_ctx, m_w_mod, m_b_mod, m_g_norm, m_ffn_w1, m_ffn_w3, m_ffn_w2, m_sc_w_in, m_sc_conv, m_sc_w_out, m_mla_w_a, m_mla_g_qa, m_mla_w_uq, m_mla_g_kva, m_mla_w_ukv, m_mla_g_q, m_mla_g_k, m_mla_w_o, v_c_ctx, v_w_mod, v_b_mod, v_g_norm, v_ffn_w1, v_ffn_w3, v_ffn_w2, v_sc_w_in, v_sc_conv, v_sc_w_out, v_mla_w_a, v_mla_g_qa, v_mla_w_uq, v_mla_g_kva, v_mla_w_ukv, v_mla_g_q, v_mla_g_k, v_mla_w_o):
    B, N, D = x.shape
    CTX = ctx.shape[1]
    T = B * (N + CTX)
    tm = next(t for t in (512, 256, 128, 64, 32, 16) if N % t == 0 and (B * CTX) % t == 0)
    dm = Dims(B, N, CTX, D, T, tm)
    L = w_mod.shape[0]
    La, Lb = sc_w_in.shape[0], mla_w_a.shape[0]
    S = N_CHIPS
    ndev = 2 * S
    xi, yi, ci = lax.axis_index("x"), lax.axis_index("y"), lax.axis_index("c")
    chip = 2 * xi + yi
    dev = 2 * chip + ci
    weights = dict(c_ctx=c_ctx, w_mod=w_mod, b_mod=b_mod, g_norm=g_norm, ffn_w1=ffn_w1, ffn_w3=ffn_w3, ffn_w2=ffn_w2,
                   sc_w_in=sc_w_in, sc_conv=sc_conv, sc_w_out=sc_w_out, mla_w_a=mla_w_a, mla_g_qa=mla_g_qa,
                   mla_w_uq=mla_w_uq, mla_g_kva=mla_g_kva, mla_w_ukv=mla_w_ukv, mla_g_q=mla_g_q, mla_g_k=mla_g_k,
                   mla_w_o=mla_w_o)
    mom = dict(c_ctx=(m_c_ctx, v_c_ctx), w_mod=(m_w_mod, v_w_mod), b_mod=(m_b_mod, v_b_mod), g_norm=(m_g_norm, v_g_norm),
               ffn_w1=(m_ffn_w1, v_ffn_w1), ffn_w3=(m_ffn_w3, v_ffn_w3), ffn_w2=(m_ffn_w2, v_ffn_w2),
               sc_w_in=(m_sc_w_in, v_sc_w_in), sc_conv=(m_sc_conv, v_sc_conv), sc_w_out=(m_sc_w_out, v_sc_w_out),
               mla_w_a=(m_mla_w_a, v_mla_w_a), mla_g_qa=(m_mla_g_qa, v_mla_g_qa), mla_w_uq=(m_mla_w_uq, v_mla_w_uq),
               mla_g_kva=(m_mla_g_kva, v_mla_g_kva), mla_w_ukv=(m_mla_w_ukv, v_mla_w_ukv), mla_g_q=(m_mla_g_q, v_mla_g_q),
               mla_g_k=(m_mla_g_k, v_mla_g_k), mla_w_o=(m_mla_w_o, v_mla_w_o))

    big = ["ffn_w1", "ffn_w3", "ffn_w2", "sc_w_in", "sc_w_out", "mla_w_a", "mla_w_uq", "mla_w_ukv", "mla_w_o"]
    F = ffn_w1.shape[-1]
    mixer_names = (["sc_w_in", "sc_w_out"], ["mla_w_a", "mla_w_uq", "mla_w_ukv", "mla_w_o"])

    def placed(name, piece):
        w2 = _flat2(weights[name])
        rows = w2.shape[0] // weights[name].shape[0]
        return _place_cast(w2, piece * rows, rows, chip, S, "place_weight")

    bufs = [{n: placed(n, l) for n in ("ffn_w1", "ffn_w3", "ffn_w2")} for l in range(L)]
    for l in range(L):
        bufs[l].update({n: placed(n, l // 2) for n in mixer_names[l % 2]})
    group_a = ["ffn_w1", "ffn_w3"]
    group_b = lambda l: ["ffn_w2"] + mixer_names[l % 2]
    first = group_a + group_b(0)
    bufs[0].update(zip(first, _ride_alone(_gather_rider([bufs[0][n] for n in first]), "gather_weights")))

    def layer_weights(l):
        b = bufs[l]
        w = dict(w1=b["ffn_w1"].reshape(S, 2, D, F), w3=b["ffn_w3"].reshape(S, 2, D, F), w2=b["ffn_w2"].reshape(S, 2, F, D))
        if l % 2 == 0:
            w["w_in"] = b["sc_w_in"][:, None]
            w["w_out"] = b["sc_w_out"].reshape(1, D, D)
        else:
            w["w_a"] = jnp.pad(b["mla_w_a"].reshape(1, D, -1), ((0, 0), (0, 0), (0, 512 - (Q_LORA + KV_LORA + QK_ROPE))))
            wuq = jnp.moveaxis(b["mla_w_uq"], 0, 1).reshape(1, Q_LORA, HEADS, QK_HEAD)
            w["w_uq"] = jnp.pad(wuq, ((0, 0), (0, 0), (0, 0), (0, HEAD_PAD - QK_HEAD))).reshape(1, Q_LORA, HEADS * HEAD_PAD)
            w["w_ukv"] = jnp.moveaxis(b["mla_w_ukv"], 0, 1).reshape(1, KV_LORA, HEADS * HEAD_PAD)
            w["w_o"] = b["mla_w_o"].reshape(1, HEADS * V_HEAD, D)
        return w

    vecs = ["g_norm", "sc_conv", "mla_g_qa"]
    gathered = _exchange([_flat2(weights[n]) for n in vecs], ("x", "y"), False, "gather_vectors")
    gw = {n: g.reshape((S,) + weights[n].shape) for n, g in zip(vecs, gathered)}
    gnorm = jnp.moveaxis(gw["g_norm"], 0, 2).reshape(L, 3, D)
    convw = jnp.moveaxis(gw["sc_conv"], 0, 2).reshape(La, 3, D)
    gqa = jnp.moveaxis(gw["mla_g_qa"], 0, 1).reshape(Lb, Q_LORA)
    padl = lambda a: jnp.pad(a, ((0, 0), (0, HEAD_PAD - a.shape[1])))
    gains = jnp.stack([padl(gqa), padl(mla_g_kva), padl(mla_g_q), padl(mla_g_k)], axis=1)
    gains = jnp.pad(gains, ((0, 0), (0, 4), (0, 0)))

    R = -(-(ndev * B + 1) // 16) * 16
    call = _exchange([c], ("x", "y", "c"), False, "gather_cond")[0].reshape(ndev * B, D)
    cond = jnp.concatenate([call, c_ctx[None], jnp.zeros((R - ndev * B - 1, D), F32)], axis=0)
    C = w_mod.shape[-1]
    bm = lax.dynamic_slice_in_dim(b_mod, chip * C, C, axis=1)[:, None, :]
    mshard = _mod_fwd(cond, w_mod, bm, "mod_fwd")
    mfull = _exchange([mshard.reshape(L * R, C)], ("x", "y"), False, "gather_mod")[0].reshape(S, L, R, C)
    mfull = jnp.moveaxis(mfull, 0, 2).reshape(L, R, S * C)
    mine = lax.dynamic_slice_in_dim(mfull, dev * B, B, axis=1)
    mod = jnp.concatenate([mine, mfull[:, ndev * B:ndev * B + 1]], axis=1).reshape(L, B + 1, 9, D)

    tabs = _rope_tables(dm)
    h = jnp.concatenate([x.reshape(B * N, D), ctx.reshape(B * CTX, D)], axis=0)

    saved = []
    lw = [None] * L
    for l in range(L):
        kind, j = l % 2, l // 2
        W = lw[l] = layer_weights(l)
        sv = {}
        sv["h0"] = h
        names = group_a if l + 1 < L else []
        rider = _gather_rider([bufs[l + 1][n] for n in names]) if names else None
        (h, sv["a1"], sv["b1"], sv["hn1"], sv["y1"]), got = _ffn_fwd(h, mod[l], gnorm[l, 0:1], W["w1"], W["w3"], W["w2"], 0, dm,
                                                                      "ffn_fwd", rider)
        if names:
            bufs[l + 1].update(zip(names, got))
        sv["h1"] = h
        if kind == 0:
            sv["p"], sv["hnm"] = _sc_in_fwd(h, mod[l], gnorm[l, 1:2], W["w_in"], 0, dm, "sc_in_fwd")
            sv["z"] = _conv_fwd(sv["p"], convw[j], dm, "conv_fwd")
            h, sv["ym"] = _out_fwd(sv["z"], W["w_out"], h, mod[l], 0, dm, "sc_out_fwd")
        else:
            sv["hnm"], sv["q"], sv["k"], sv["v"] = _mla_proj_fwd(h, mod[l], gnorm[l, 1:2], gains[j:j + 1], tabs, W["w_a"], W["w_uq"],
                                                                 W["w_ukv"], 0, dm, "mla_proj_fwd")
            sv["o"] = _attn_fwd(sv["q"], sv["k"], sv["v"], dm, "attn_fwd")
            h, sv["ym"] = _out_fwd(sv["o"], W["w_o"], h, mod[l], 0, dm, "mla_out_fwd")
        sv["h2"] = h
        names = group_b(l + 1) if l + 1 < L else []
        rider = _gather_rider([bufs[l + 1][n] for n in names]) if names else None
        (h, sv["a2"], sv["b2"], sv["hn2"], sv["y2"]), got = _ffn_fwd(h, mod[l], gnorm[l, 2:3], W["w1"], W["w3"], W["w2"], 1, dm,
                                                                      "ffn_fwd", rider)
        if names:
            bufs[l + 1].update(zip(names, got))
        saved.append(sv)

    dh, lsum = _loss_grad(h, loss_target.reshape(B * N, D), dm, "loss_grad")
    loss = lax.psum(jnp.sum(lsum[:, 0, 0]), ("x", "y", "c"))

    wq = D // S
    gsum = {}
    for n in big:
        w = weights[n]
        npieces = w.shape[0] * (w.shape[1] if n.startswith("ffn") else 1)
        gsum[n] = jnp.zeros((npieces, 2, _flat2(w).shape[0] // npieces // 2, w.shape[-1]), F32)
    dmod = [None] * L
    dgn = [None] * L
    dconv = [None] * La
    dgains = [None] * Lb
    tk = tm
    nk = T // tk
    full_a = pl.BlockSpec((tk, D), lambda s, kk: (kk, 0))
    shard_b = pl.BlockSpec((None, tk, F), lambda s, kk: (s, kk, 0))
    per_slot = lambda r_, c_: pl.BlockSpec((None, r_, c_), lambda s, kk: (s, 0, 0))

    def make_job(grads):
        theirs = _swap_halves([g_ for _, _, g_ in grads], "swap_halves")
        return [(n, p, _pair_sum(g_, r_, ci, "pair_sum")) for (n, p, g_), r_ in zip(grads, theirs)]

    def finish_job(job, recv):
        for (n, p, pair), r_ in zip(job, recv):
            gsum[n] = _sum_slots(r_, pair, chip, ci, gsum[n], p, "sum_slots")

    def ffn_back(dh, sv, l, k, job):
        sfx = "1" if k == 0 else "2"
        W = lw[l]
        rider = _scatter_rider([pair for _, _, pair in job]) if job else None
        (dh, da, db, sw, dy, part), recv = _ffn_bwd(dh, sv["h0" if k == 0 else "h2"], mod[l], gnorm[l, 2 * k:2 * k + 1], sv["y" + sfx],
                                                    sv["a" + sfx], sv["b" + sfx], W["w1"], W["w3"], W["w2"], k, dm, "ffn_bwd", rider)
        finish_job(job, recv)
        g1 = _mm_tn(sv["hn" + sfx], da, full_a, shard_b, (S, D, F), per_slot(D, F), (S, nk), "gw1")
        g3 = _mm_tn(sv["hn" + sfx], db, full_a, shard_b, (S, D, F), per_slot(D, F), (S, nk), "gw3")
        g2 = _mm_tn(sw, dy, shard_b, full_a, (S, F, D), per_slot(F, D), (S, nk), "gw2")
        p = 2 * l + k
        return dh, _fold_parts(part, dm), [("ffn_w1", p, g1), ("ffn_w3", p, g3), ("ffn_w2", p, g2)]

    one = (1, nk)
    a1 = lambda kdim: pl.BlockSpec((tk, kdim), lambda s, kk: (kk, 0))
    pending = []
    for l in reversed(range(L)):
        kind, j = l % 2, l // 2
        sv = saved[l]
        W = lw[l]
        dh, p2, grads = ffn_back(dh, sv, l, 1, pending)
        job2 = make_job(grads)
        if kind == 0:
            dy, dz, pg = _out_bwd(dh, sv["ym"], W["w_out"], mod[l], 0, dm, "sc_out_bwd")
            g_out = _mm_tn(sv["z"], dy, a1(D), a1(D), (1, D, D), per_slot(D, D), one, "gw_sc_out")
            dp, dconv[j] = _conv_bwd(dz, sv["p"], convw[j], dm, "conv_bwd")
            g_in = _mm_tn(sv["hnm"], dp, pl.BlockSpec((tk, D), lambda q, kk: (kk, 0)),
                          pl.BlockSpec((None, tk, wq), lambda q, kk: (q // S, kk, q % S)), (3 * S, D, wq), per_slot(D, wq),
                          (3 * S, nk), "gw_sc_in")
            dh, pm = _sc_in_bwd(dh, dp, sv["h1"], mod[l], gnorm[l, 1:2], W["w_in"], 0, dm, "sc_in_bwd")
            grads = [("sc_w_in", j, jnp.moveaxis(g_in.reshape(S, 3, D, wq), 1, 2).reshape(S, D, 3 * wq)),
                     ("sc_w_out", j, g_out.reshape(S, D // S, D))]
        else:
            dy, do, pg = _out_bwd(dh, sv["ym"], W["w_o"], mod[l], 0, dm, "mla_out_bwd")
            g_o = _mm_tn(sv["o"], dy, a1(HEADS * V_HEAD), a1(D), (1, HEADS * V_HEAD, D), per_slot(HEADS * V_HEAD, D), one, "gw_mla_o")
            dq, dkl, dkc, dvl, dvc = _attn_bwd(sv["q"], sv["k"], sv["v"], sv["o"], do, dm, "attn_bwd")
            dk = jnp.concatenate([dkl, dkc], axis=0)
            dv = jnp.concatenate([dvl, dvc], axis=0)
            dh, pm, g_a, g_uq, g_ukv, dgains[j] = _mla_proj_bwd(
                dh, dq, dk, dv, sv["h1"], mod[l], gnorm[l, 1:2], gains[j:j + 1], tabs, W["w_a"], W["w_uq"], W["w_ukv"], 0, dm, "mla_proj_bwd")
            g_uq = g_uq.reshape(Q_LORA, HEADS, HEAD_PAD)[..., :QK_HEAD].reshape(Q_LORA, S, -1)
            grads = [("mla_w_a", j, g_a[:, :Q_LORA + KV_LORA + QK_ROPE].reshape(S, D // S, -1).astype(BF16)),
                     ("mla_w_uq", j, jnp.moveaxis(g_uq, 1, 0).astype(BF16)),
                     ("mla_w_ukv", j, jnp.moveaxis(g_ukv.reshape(KV_LORA, S, -1), 1, 0).astype(BF16)),
                     ("mla_w_o", j, g_o.reshape(S, HEADS * V_HEAD // S, D))]
        jobm = make_job(grads)
        pm = _fold_parts(pm, dm) + _fold_parts(pg, dm)
        dh, p0, grads = ffn_back(dh, sv, l, 0, job2 + jobm)
        pending = make_job(grads)
        dmod[l] = jnp.concatenate([p0[:, 0:3], pm[:, 0:3], p2[:, 0:3]], axis=1).reshape(B + 1, 9 * D)
        dgn[l] = jnp.stack([p0[:, 3].sum(0), pm[:, 3].sum(0), p2[:, 3].sum(0)])
    grad_x = dh[:B * N].reshape(B, N, D)
    finish_job(pending, _ride_alone(_scatter_rider([pair for _, _, pair in pending]), "scatter_grads"))
    gsum = dict(zip(big, _swap_cores_inplace([gsum[n] for n in big], "swap_cores")))

    dgains_a = jnp.stack(dgains)
    small = [jnp.stack(dmod).reshape(-1), jnp.stack(dgn).reshape(-1), jnp.stack(dconv).reshape(-1), dgains_a.reshape(-1)]
    sizes = [s_.shape[0] for s_ in small]
    flat = jnp.concatenate(small)
    pad = (-flat.shape[0]) % 1024
    flat = jnp.pad(flat, (0, pad)).reshape(-1, 128)
    allsmall = _exchange([flat], ("x", "y", "c"), False, "gather_small")[0].reshape(ndev, -1)
    offs = [0]
    for s_ in sizes:
        offs.append(offs[-1] + s_)
    dmod_all = allsmall[:, offs[0]:offs[1]].reshape(ndev, L, B + 1, 9 * D)
    tot = allsmall[:, offs[1]:offs[4]].sum(axis=0)
    g_gnorm = tot[:offs[2] - offs[1]].reshape(L, 3, D)
    g_conv = tot[offs[2] - offs[1]:offs[3] - offs[1]].reshape(La, 3, D)
    g_gains = tot[offs[3] - offs[1]:].reshape(Lb, 8, HEAD_PAD)
    dM = jnp.concatenate([jnp.moveaxis(dmod_all[:, :, :B], 0, 1).reshape(L, ndev * B, 9 * D),
                          dmod_all[:, :, B].sum(axis=0)[:, None, :], jnp.zeros((L, R - ndev * B - 1, 9 * D), F32)], axis=1)
    g_bmod = dM.sum(axis=1)
    dM_mine = lax.dynamic_slice_in_dim(dM, chip * C, C, axis=2)
    g_wmod, dsil = _mod_bwd(cond, dM_mine, w_mod, "mod_bwd")
    dsil_ctx = dsil[:, ndev * B].sum(axis=0)
    dsil_all = _exchange([jnp.pad(dsil_ctx.reshape(-1, 128), ((0, (-(D // 128)) % 8), (0, 0)))], ("x", "y"), False, "gather_dctx")[0]
    dsil_tot = dsil_all.sum(axis=0)[:D // 128].reshape(D)
    sg = jax.nn.sigmoid(c_ctx)
    g_cctx = dsil_tot * (sg * (1.0 + c_ctx * (1.0 - sg)))

    chip_cols = lambda a, width: lax.dynamic_slice_in_dim(a, chip * width, width, axis=a.ndim - 1)
    small_grads = dict(
        c_ctx=g_cctx, b_mod=g_bmod, g_norm=chip_cols(g_gnorm, D // S), sc_conv=chip_cols(g_conv, D // S),
        mla_g_qa=chip_cols(g_gains[:, 0, :Q_LORA], Q_LORA // S), mla_g_kva=g_gains[:, 1, :KV_LORA],
        mla_g_q=g_gains[:, 2, :QK_HEAD], mla_g_k=g_gains[:, 3, :QK_HEAD])

    grads, deltas, new_m, new_v = {}, {}, {}, {}
    for n, w in weights.items():
        shape = w.shape
        w2 = _flat2(w) if w.ndim > 1 else w.reshape(1, -1)
        m2, v2 = (a.reshape(w2.shape) for a in mom[n])
        if n in gsum:
            gs = [gsum[n].reshape(w2.shape)]
        elif n == "w_mod":
            gs = [_flat2(g_wmod)]
        else:
            gs = [small_grads[n].reshape(w2.shape)]
        g_, d_, m_, v_ = _adamw(w2, gs, m2, v2, "adamw")
        grads[n], deltas[n], new_m[n], new_v[n] = (a.reshape(shape) for a in (g_, d_, m_, v_))

    names = list(weights)
    return (loss, grad_x, *[grads[n] for n in names], *[deltas[n] for n in names], *[new_m[n] for n in names],
            *[new_v[n] for n in names])
```

```python
import functools
import math
from typing import NamedTuple

import jax
import jax.numpy as jnp
from jax import lax
from jax.experimental import pallas as pl
from jax.experimental.pallas import tpu as pltpu

F32 = jnp.float32
BF16 = jnp.bfloat16
EPS = 1e-6
GRID_W = 64
HEADS = 8
QK_NOPE = 128
QK_ROPE = 64
QK_HEAD = QK_NOPE + QK_ROPE
HEAD_PAD = 256
V_HEAD = 128
Q_LORA = 256
KV_LORA = 128
ROPE_BASE = 10000.0
QK_SCALE = QK_HEAD ** -0.5
ADAM_LR, ADAM_B1, ADAM_B2, ADAM_EPS, ADAM_WD, ADAM_STEP = 0.001, 0.9, 0.999, 1e-08, 0.01, 10
N_CHIPS = 4
VMEM_LIMIT = 56 * 1024 * 1024
MESH = pl.DeviceIdType.MESH
NEG = -1e30


class Dims(NamedTuple):
    B: int
    N: int
    CTX: int
    D: int
    T: int
    tm: int


def _cparams(*sem):
    return pltpu.CompilerParams(dimension_semantics=sem if sem else None, vmem_limit_bytes=VMEM_LIMIT)


def _dot(a, b):
    return jnp.dot(a, b, preferred_element_type=F32)


def _dot_nt(a, b):
    return lax.dot_general(a, b, (((1,), (1,)), ((), ())), preferred_element_type=F32)


def _dot_tn(a, b):
    return lax.dot_general(a, b, (((0,), (0,)), ((), ())), preferred_element_type=F32)


def _rms(x, n):
    r = lax.rsqrt(jnp.sum(x * x, axis=-1, keepdims=True) * (1.0 / n) + EPS)
    return x * r, r


def _rms_bwd(dxh, xh, r, n):
    return r * (dxh - xh * (jnp.sum(dxh * xh, axis=-1, keepdims=True) * (1.0 / n)))


def _pre(h, g, shift, scale):
    xh, _ = _rms(h, h.shape[-1])
    return (xh * g) * (1.0 + scale) + shift


def _pre_bwd(dout, h, g, scale):
    d = h.shape[-1]
    xh, r = _rms(h, d)
    n = xh * g
    dshift = jnp.sum(dout, axis=0, keepdims=True)
    dscale = jnp.sum(dout * n, axis=0, keepdims=True)
    dn = dout * (1.0 + scale)
    dg = jnp.sum(dn * xh, axis=0, keepdims=True)
    dh = _rms_bwd(dn * g, xh, r, d)
    return dh, dshift, dscale, dg


def _write_part(part_ref, dshift=None, dscale=None, dgate=None, dg=None):
    z = jnp.zeros((1, part_ref.shape[-1]), F32)
    part_ref[0, 0:1, :] = z if dshift is None else dshift
    part_ref[0, 1:2, :] = z if dscale is None else dscale
    part_ref[0, 2:3, :] = z if dgate is None else dgate
    part_ref[0, 3:4, :] = z if dg is None else dg
    part_ref[0, 4:8, :] = jnp.zeros((4, part_ref.shape[-1]), F32)


def _grp(dm):
    nb = dm.N // dm.tm
    return lambda i: jnp.minimum(i // nb, dm.B)


def _n_chunks(rows, row_bytes):
    n = 16
    while n > 1 and (rows % (16 * n) or (rows // n) * row_bytes < (256 << 10)):
        n //= 2
    return n


def _start_local(src, dst, sems, k0, nchunk):
    ch = src.shape[0] // nchunk
    copies = []
    for j in range(nchunk):
        cp = pltpu.make_async_copy(src.at[pl.ds(j * ch, ch)], dst.at[pl.ds(j * ch, ch)], sems.at[k0 + j])
        cp.start()
        copies.append(cp)
    return copies


def _exchange(arrs, axes, scatter, name, own="copy"):
    n = len(arrs)
    nbits = len(axes)
    slots = 2 ** nbits
    pats = list(range(1, slots))
    inplace = own == "inplace"
    nck = [_n_chunks(a.shape[-2], a.shape[-1] * a.dtype.itemsize) for a in arrs]
    base = [sum(nck[:i]) * len(pats) for i in range(n)]
    nsem = sum(nck) * len(pats)

    def body(*refs):
        ins, outs = refs[:n], refs[n:2 * n]
        send, recv, loc = refs[2 * n:]
        pos = {a: lax.axis_index(a) for a in ("x", "y", "c")}

        def slot_of(p):
            s = 0
            for a in axes:
                s = 2 * s + p[a]
            return s

        me = slot_of(pos)
        local = []
        for i in range(n):
            if own == "copy":
                local += _start_local(ins[i].at[me] if scatter else ins[i], outs[i].at[me], loc, sum(nck[:i]), nck[i])
        remote = []
        for pi, pat in enumerate(pats):
            peer = dict(pos)
            for bi, a in enumerate(axes):
                if (pat >> (nbits - 1 - bi)) & 1:
                    peer[a] = 1 - pos[a]
            them = slot_of(peer)
            for i in range(n):
                ch = arrs[i].shape[-2] // nck[i]
                for j in range(nck[i]):
                    k = base[i] + pi * nck[i] + j
                    rs = pl.ds(j * ch, ch)
                    if inplace:
                        src = outs[i].at[me, rs]
                    else:
                        src = ins[i].at[them, rs] if scatter else ins[i].at[rs]
                    cp = pltpu.make_async_remote_copy(
                        src_ref=src, dst_ref=outs[i].at[me, rs], send_sem=send.at[k], recv_sem=recv.at[k],
                        device_id=(peer["x"], peer["y"], peer["c"]), device_id_type=MESH)
                    cp.start()
                    remote.append(cp)
        for cp in local:
            cp.wait()
        for cp in remote:
            cp.wait()

    out_shape = [jax.ShapeDtypeStruct(a.shape if (scatter or inplace) else (slots,) + a.shape, a.dtype) for a in arrs]
    any_spec = pl.BlockSpec(memory_space=pl.ANY)
    outs = pl.pallas_call(
        body, name=name, out_shape=out_shape, in_specs=[any_spec] * n, out_specs=[any_spec] * n,
        scratch_shapes=[pltpu.SemaphoreType.DMA((nsem,)), pltpu.SemaphoreType.DMA((nsem,)), pltpu.SemaphoreType.DMA((sum(nck),))],
        input_output_aliases={i: i for i in range(n)} if inplace else {},
        compiler_params=pltpu.CompilerParams(has_side_effects=True),
    )(*arrs)
    return list(outs)


class Rider(NamedTuple):
    ins: list
    out_shapes: list
    aliases: dict
    sems: list
    start: object
    mid: object
    end: object


MID_STEPS = 6


def _hosted(body, rider, *, name, grid, in_specs, out_specs, out_shape, scratch_shapes, sem, args):
    if rider is None:
        outs = pl.pallas_call(body, name=name, grid=grid, in_specs=in_specs, out_specs=out_specs, out_shape=out_shape,
                              scratch_shapes=scratch_shapes, compiler_params=_cparams(*sem))(*args)
        return outs, []
    n_in, n_out, n_s = len(in_specs), len(out_specs), len(scratch_shapes)
    nri, nro = len(rider.ins), len(rider.out_shapes)
    nsteps = math.prod(grid)

    def wrapped(*refs):
        bounds = [0, n_in, n_in + nri, n_in + nri + n_out, n_in + nri + n_out + nro, n_in + nri + n_out + nro + n_s, len(refs)]
        ins, rins, outs, routs, scr, sems = (refs[lo:hi] for lo, hi in zip(bounds[:-1], bounds[1:]))
        step = 0
        for ax, extent in enumerate(grid):
            step = step * extent + pl.program_id(ax)

        @pl.when(step == 0)
        def _():
            rider.start(rins, routs, sems)

        body(*ins, *outs, *scr)

        if rider.mid is not None:
            @pl.when(step == max(nsteps - 1 - MID_STEPS, 0))
            def _():
                rider.mid(rins, routs, sems)

        @pl.when(step == nsteps - 1)
        def _():
            rider.end(rins, routs, sems)

    any_spec = pl.BlockSpec(memory_space=pl.ANY)
    outs = pl.pallas_call(
        wrapped, name=name, grid=grid, in_specs=list(in_specs) + [any_spec] * nri, out_specs=list(out_specs) + [any_spec] * nro,
        out_shape=list(out_shape) + list(rider.out_shapes), scratch_shapes=list(scratch_shapes) + list(rider.sems),
        input_output_aliases={n_in + i: n_out + o for i, o in rider.aliases.items()},
        compiler_params=pltpu.CompilerParams(dimension_semantics=("arbitrary",) * len(grid), vmem_limit_bytes=VMEM_LIMIT,
                                             has_side_effects=True),
    )(*args, *rider.ins)
    return outs[:n_out], list(outs[n_out:])


def _gather_rider(bufs):
    n = len(bufs)
    halves = [a.shape[1] // 2 for a in bufs]
    nck = [_n_chunks(h, a.shape[2] * a.dtype.itemsize) for h, a in zip(halves, bufs)]
    base = [3 * sum(nck[:i]) for i in range(n)]
    nsem = 3 * sum(nck)

    def plan():
        x, y, c = lax.axis_index("x"), lax.axis_index("y"), lax.axis_index("c")
        pieces = []
        for pi, (px, py) in enumerate([(x, 1 - y), (1 - x, y), (1 - x, 1 - y)]):
            for i in range(n):
                ch = halves[i] // nck[i]
                for j in range(nck[i]):
                    pieces.append((base[i] + pi * nck[i] + j, px, py, 2 * px + py, i, j * ch, ch))
        return x, y, c, 2 * x + y, pieces

    def rows(i, off, ch, core):
        return pl.ds(pl.multiple_of(core * halves[i] + off, 16), ch)

    def over_ici(outs, sems, c, slot, k, px, py, i, off, ch):
        ref = outs[i].at[slot, rows(i, off, ch, c)]
        return pltpu.make_async_remote_copy(src_ref=ref, dst_ref=ref, send_sem=sems[0].at[k], recv_sem=sems[1].at[k],
                                            device_id=(px, py, c), device_id_type=MESH)

    def over_d2d(outs, sems, x, y, c, slot, k, i, off, ch, core):
        ref = outs[i].at[slot, rows(i, off, ch, core)]
        return pltpu.make_async_remote_copy(src_ref=ref, dst_ref=ref, send_sem=sems[2].at[k], recv_sem=sems[3].at[k],
                                            device_id=(x, y, 1 - c), device_id_type=MESH)

    def start(ins, outs, sems):
        x, y, c, me, pieces = plan()
        for k, px, py, them, i, off, ch in pieces:
            over_ici(outs, sems, c, me, k, px, py, i, off, ch).start()

    def mid(ins, outs, sems):
        x, y, c, me, pieces = plan()
        for k, px, py, them, i, off, ch in pieces:
            over_ici(outs, sems, c, them, k, px, py, i, off, ch).wait_recv()
            over_d2d(outs, sems, x, y, c, them, k, i, off, ch, c).start()

    def end(ins, outs, sems):
        x, y, c, me, pieces = plan()
        for k, px, py, them, i, off, ch in pieces:
            over_ici(outs, sems, c, me, k, px, py, i, off, ch).wait_send()
            over_d2d(outs, sems, x, y, c, them, k, i, off, ch, c).wait_send()
        for k, px, py, them, i, off, ch in pieces:
            over_d2d(outs, sems, x, y, c, them, k, i, off, ch, 1 - c).wait_recv()

    return Rider(ins=list(bufs), out_shapes=[jax.ShapeDtypeStruct(a.shape, a.dtype) for a in bufs],
                 aliases={i: i for i in range(n)}, sems=[pltpu.SemaphoreType.DMA((nsem,))] * 4, start=start, mid=mid, end=end)


def _scatter_rider(srcs):
    n = len(srcs)
    nck = [_n_chunks(a.shape[1], a.shape[2] * a.dtype.itemsize) for a in srcs]
    base = [3 * sum(nck[:i]) for i in range(n)]
    nsem = 3 * sum(nck)

    def copies(ins, outs, sems):
        x, y, c = lax.axis_index("x"), lax.axis_index("y"), lax.axis_index("c")
        me = 2 * x + y
        for pi, (px, py) in enumerate([(x, 1 - y), (1 - x, y), (1 - x, 1 - y)]):
            for i in range(n):
                ch = srcs[i].shape[1] // nck[i]
                for j in range(nck[i]):
                    k = base[i] + pi * nck[i] + j
                    rs = pl.ds(j * ch, ch)
                    yield pltpu.make_async_remote_copy(
                        src_ref=ins[i].at[2 * px + py, rs], dst_ref=outs[i].at[me, rs], send_sem=sems[0].at[k],
                        recv_sem=sems[1].at[k], device_id=(px, py, c), device_id_type=MESH)

    def start(ins, outs, sems):
        for cp in copies(ins, outs, sems):
            cp.start()

    def end(ins, outs, sems):
        for cp in copies(ins, outs, sems):
            cp.wait()

    return Rider(ins=list(srcs), out_shapes=[jax.ShapeDtypeStruct(a.shape, a.dtype) for a in srcs], aliases={},
                 sems=[pltpu.SemaphoreType.DMA((nsem,))] * 2, start=start, mid=None, end=end)


def _ride_alone(rider, name):
    n_in, n_out = len(rider.ins), len(rider.out_shapes)

    def body(*refs):
        ins, outs, sems = refs[:n_in], refs[n_in:n_in + n_out], refs[n_in + n_out:]
        rider.start(ins, outs, sems)
        if rider.mid is not None:
            rider.mid(ins, outs, sems)
        rider.end(ins, outs, sems)

    any_spec = pl.BlockSpec(memory_space=pl.ANY)
    outs = pl.pallas_call(
        body, name=name, out_shape=list(rider.out_shapes), in_specs=[any_spec] * n_in, out_specs=[any_spec] * n_out,
        scratch_shapes=list(rider.sems), input_output_aliases=dict(rider.aliases),
        compiler_params=pltpu.CompilerParams(has_side_effects=True),
    )(*rider.ins)
    return list(outs)


def _swap_cores_inplace(bufs, name):
    n = len(bufs)
    nck = [_n_chunks(a.shape[2], a.shape[3] * a.dtype.itemsize) for a in bufs]
    base = [sum(a.shape[0] * k for a, k in zip(bufs[:i], nck[:i])) for i in range(n)]
    nsem = sum(a.shape[0] * k for a, k in zip(bufs, nck))

    def body(*refs):
        outs = refs[n:2 * n]
        send, recv = refs[2 * n:]
        x, y, c = lax.axis_index("x"), lax.axis_index("y"), lax.axis_index("c")

        def copies(core):
            for i in range(n):
                ch = bufs[i].shape[2] // nck[i]
                for p in range(bufs[i].shape[0]):
                    for j in range(nck[i]):
                        k = base[i] + p * nck[i] + j
                        ref = outs[i].at[p, core, pl.ds(j * ch, ch)]
                        yield pltpu.make_async_remote_copy(src_ref=ref, dst_ref=ref, send_sem=send.at[k], recv_sem=recv.at[k],
                                                           device_id=(x, y, 1 - c), device_id_type=MESH)

        for cp in copies(c):
            cp.start()
        for cp in copies(c):
            cp.wait_send()
        for cp in copies(1 - c):
            cp.wait_recv()

    any_spec = pl.BlockSpec(memory_space=pl.ANY)
    outs = pl.pallas_call(
        body, name=name, out_shape=[jax.ShapeDtypeStruct(a.shape, a.dtype) for a in bufs], in_specs=[any_spec] * n,
        out_specs=[any_spec] * n, scratch_shapes=[pltpu.SemaphoreType.DMA((nsem,))] * 2,
        input_output_aliases={i: i for i in range(n)}, compiler_params=pltpu.CompilerParams(has_side_effects=True),
    )(*bufs)
    return list(outs)


def _place_cast(w, row0, rows, slot, slots, name):
    C = w.shape[1]
    tr = _row_block(rows, C)
    blk0 = row0 // tr

    def body(slot_ref, w_ref, o_ref):
        del slot_ref
        o_ref[...] = w_ref[...].astype(BF16)

    return pl.pallas_call(
        body, name=name,
        grid_spec=pltpu.PrefetchScalarGridSpec(
            num_scalar_prefetch=1, grid=(rows // tr,), in_specs=[pl.BlockSpec((tr, C), lambda i, sr: (blk0 + i, 0))],
            out_specs=pl.BlockSpec((None, tr, C), lambda i, sr: (sr[0], i, 0))),
        out_shape=jax.ShapeDtypeStruct((slots, rows, C), BF16),
        compiler_params=_cparams("parallel"),
    )(slot.reshape(1).astype(jnp.int32), w)


def _swap_halves(arrs, name):
    n = len(arrs)
    S = arrs[0].shape[0]
    halves = [a.shape[1] // 2 for a in arrs]
    nck = [_n_chunks(h, a.shape[2] * a.dtype.itemsize) for h, a in zip(halves, arrs)]
    base = [S * sum(nck[:i]) for i in range(n)]
    nsem = S * sum(nck)

    def body(*refs):
        ins, outs = refs[:n], refs[n:2 * n]
        send, recv = refs[2 * n:]
        x, y, c = lax.axis_index("x"), lax.axis_index("y"), lax.axis_index("c")
        copies = []
        for i in range(n):
            ch = halves[i] // nck[i]
            for s in range(S):
                for j in range(nck[i]):
                    k = base[i] + s * nck[i] + j
                    src = ins[i].at[s, pl.ds(pl.multiple_of((1 - c) * halves[i] + j * ch, 16), ch)]
                    cp = pltpu.make_async_remote_copy(src_ref=src, dst_ref=outs[i].at[s, pl.ds(j * ch, ch)], send_sem=send.at[k],
                                                      recv_sem=recv.at[k], device_id=(x, y, 1 - c), device_id_type=MESH)
                    cp.start()
                    copies.append(cp)
        for cp in copies:
            cp.wait()

    any_spec = pl.BlockSpec(memory_space=pl.ANY)
    outs = pl.pallas_call(
        body, name=name, out_shape=[jax.ShapeDtypeStruct((S, h, a.shape[2]), a.dtype) for h, a in zip(halves, arrs)],
        in_specs=[any_spec] * n, out_specs=[any_spec] * n,
        scratch_shapes=[pltpu.SemaphoreType.DMA((nsem,))] * 2,
        compiler_params=pltpu.CompilerParams(has_side_effects=True),
    )(*arrs)
    return list(outs)


def _pair_sum(g, r, core, name):
    S, rows, C = g.shape
    half = rows // 2
    tr = _row_block(half, C)
    nb = half // tr

    def body(core_ref, g_ref, r_ref, o_ref):
        del core_ref
        o_ref[...] = (g_ref[...].astype(F32) + r_ref[...].astype(F32)).astype(BF16)

    blk = pl.BlockSpec((None, tr, C), lambda s, i, cr: (s, i, 0))
    return pl.pallas_call(
        body, name=name,
        grid_spec=pltpu.PrefetchScalarGridSpec(
            num_scalar_prefetch=1, grid=(S, nb),
            in_specs=[pl.BlockSpec((None, tr, C), lambda s, i, cr: (s, cr[0] * nb + i, 0)), blk], out_specs=blk),
        out_shape=jax.ShapeDtypeStruct((S, half, C), BF16),
        compiler_params=_cparams("parallel", "parallel"),
    )(core.reshape(1).astype(jnp.int32), g, r)


def _ffn_fwd(h, mod, g, w1, w3, w2, k, dm, name, rider=None):
    T, D = h.shape
    S, F = w1.shape[0], w1.shape[-1]
    tm = dm.tm
    r0 = 6 if k else 0
    grp = _grp(dm)

    def body(h_ref, mod_ref, g_ref, w1_ref, w3_ref, w2_ref, ho_ref, a_ref, b_ref, hn_ref, y_ref, hn_s, acc):
        s = pl.program_id(1)

        @pl.when(s == 0)
        def _():
            hn = _pre(h_ref[...], g_ref[...], mod_ref[0, r0:r0 + 1, :], mod_ref[0, r0 + 1:r0 + 2, :]).astype(BF16)
            hn_s[...] = hn
            hn_ref[...] = hn
            acc[...] = jnp.zeros_like(acc)

        hn = hn_s[...]
        a = _dot(hn, w1_ref[...])
        b = _dot(hn, w3_ref[...])
        a_ref[0] = a.astype(BF16)
        b_ref[0] = b.astype(BF16)
        sw = (a * jax.nn.sigmoid(a) * b).astype(BF16)
        acc[...] += _dot(sw, w2_ref[...])

        @pl.when(s == S - 1)
        def _():
            y = acc[...]
            y_ref[...] = y.astype(BF16)
            ho_ref[...] = h_ref[...] + 0.5 * mod_ref[0, r0 + 2:r0 + 3, :] * y

    row = pl.BlockSpec((tm, D), lambda i, s: (i, 0))
    wcol = pl.BlockSpec((None, None, D, F), lambda i, s: (s, k, 0, 0))
    wrow = pl.BlockSpec((None, None, F, D), lambda i, s: (s, k, 0, 0))
    ab = pl.BlockSpec((1, tm, F), lambda i, s: (s, i, 0))
    return _hosted(
        body, rider, name=name, grid=(T // tm, S),
        in_specs=[row, pl.BlockSpec((1, 9, D), lambda i, s: (grp(i), 0, 0)), pl.BlockSpec((1, D), lambda i, s: (0, 0)),
                  wcol, wcol, wrow],
        out_specs=[row, ab, ab, row, row],
        out_shape=[jax.ShapeDtypeStruct((T, D), F32), jax.ShapeDtypeStruct((S, T, F), BF16),
                   jax.ShapeDtypeStruct((S, T, F), BF16), jax.ShapeDtypeStruct((T, D), BF16),
                   jax.ShapeDtypeStruct((T, D), BF16)],
        scratch_shapes=[pltpu.VMEM((tm, D), BF16), pltpu.VMEM((tm, D), F32)],
        sem=("parallel", "arbitrary"), args=(h, mod, g, w1, w3, w2))


def _ffn_bwd(dh, h, mod, g, y, a, b, w1, w3, w2, k, dm, name, rider=None):
    T, D = h.shape
    S, F = w1.shape[0], w1.shape[-1]
    tm = dm.tm
    r0 = 6 if k else 0
    grp = _grp(dm)

    def body(dh_ref, h_ref, mod_ref, g_ref, y_ref, a_ref, b_ref, w1_ref, w3_ref, w2_ref,
             dho_ref, da_ref, db_ref, sw_ref, dy_ref, part_ref, dy_s, acc):
        s = pl.program_id(1)

        @pl.when(s == 0)
        def _():
            dy = (0.5 * mod_ref[0, r0 + 2:r0 + 3, :] * dh_ref[...]).astype(BF16)
            dy_s[...] = dy
            dy_ref[...] = dy
            acc[...] = jnp.zeros_like(acc)

        ds = _dot_nt(dy_s[...], w2_ref[...])
        av = a_ref[0].astype(F32)
        bv = b_ref[0].astype(F32)
        sig = jax.nn.sigmoid(av)
        sil = av * sig
        sw_ref[0] = (sil * bv).astype(BF16)
        db = (ds * sil).astype(BF16)
        da = (ds * bv * (sig * (1.0 + av * (1.0 - sig)))).astype(BF16)
        da_ref[0] = da
        db_ref[0] = db
        acc[...] += _dot_nt(da, w1_ref[...]) + _dot_nt(db, w3_ref[...])

        @pl.when(s == S - 1)
        def _():
            dhv = dh_ref[...]
            dhb, dshift, dscale, dg = _pre_bwd(acc[...], h_ref[...], g_ref[...], mod_ref[0, r0 + 1:r0 + 2, :])
            dho_ref[...] = dhv + dhb
            dgate = 0.5 * jnp.sum(dhv * y_ref[...].astype(F32), axis=0, keepdims=True)
            _write_part(part_ref, dshift, dscale, dgate, dg)

    row = pl.BlockSpec((tm, D), lambda i, s: (i, 0))
    wcol = pl.BlockSpec((None, None, D, F), lambda i, s: (s, k, 0, 0))
    wrow = pl.BlockSpec((None, None, F, D), lambda i, s: (s, k, 0, 0))
    ab = pl.BlockSpec((1, tm, F), lambda i, s: (s, i, 0))
    stf = jax.ShapeDtypeStruct((S, T, F), BF16)
    return _hosted(
        body, rider, name=name, grid=(T // tm, S),
        in_specs=[row, row, pl.BlockSpec((1, 9, D), lambda i, s: (grp(i), 0, 0)), pl.BlockSpec((1, D), lambda i, s: (0, 0)),
                  row, ab, ab, wcol, wcol, wrow],
        out_specs=[row, ab, ab, ab, row, pl.BlockSpec((1, 8, D), lambda i, s: (i, 0, 0))],
        out_shape=[jax.ShapeDtypeStruct((T, D), F32), stf, stf, stf, jax.ShapeDtypeStruct((T, D), BF16),
                   jax.ShapeDtypeStruct((T // tm, 8, D), F32)],
        scratch_shapes=[pltpu.VMEM((tm, D), BF16), pltpu.VMEM((tm, D), F32)],
        sem=("parallel", "arbitrary"), args=(dh, h, mod, g, y, a, b, w1, w3, w2))


def _mm_tn(a, b, a_spec, b_spec, out_shape, out_spec, grid, name):
    nk = grid[-1]
    kax = len(grid) - 1
    blk = tuple(d for d in out_spec.block_shape if d is not None)

    def body(a_ref, b_ref, o_ref, acc):
        kk = pl.program_id(kax)

        @pl.when(kk == 0)
        def _():
            acc[...] = jnp.zeros_like(acc)

        acc[...] += _dot_tn(a_ref[...].astype(BF16), b_ref[...].astype(BF16))

        @pl.when(kk == nk - 1)
        def _():
            o_ref[...] = acc[...].astype(o_ref.dtype)

    return pl.pallas_call(
        body, name=name, grid=grid,
        in_specs=[a_spec, b_spec], out_specs=out_spec, out_shape=jax.ShapeDtypeStruct(out_shape, BF16),
        scratch_shapes=[pltpu.VMEM(blk, F32)],
        compiler_params=_cparams(*(["parallel"] * kax + ["arbitrary"])),
    )(a, b)


def _sc_in_fwd(h, mod, g, w_in, j, dm, name, rider=None):
    T, D = h.shape
    tm = dm.tm
    wq = D // N_CHIPS
    nq = 3 * N_CHIPS
    grp = _grp(dm)

    def body(h_ref, mod_ref, g_ref, w_ref, p_ref, hn_ref, hn_s):
        @pl.when(pl.program_id(1) == 0)
        def _():
            hn = _pre(h_ref[...], g_ref[...], mod_ref[0, 3:4, :], mod_ref[0, 4:5, :]).astype(BF16)
            hn_s[...] = hn
            hn_ref[...] = hn

        p_ref[...] = _dot(hn_s[...], w_ref[...])

    row = pl.BlockSpec((tm, D), lambda i, q: (i, 0))
    return _hosted(
        body, rider, name=name, grid=(T // tm, nq),
        in_specs=[row, pl.BlockSpec((1, 9, D), lambda i, q: (grp(i), 0, 0)), pl.BlockSpec((1, D), lambda i, q: (0, 0)),
                  pl.BlockSpec((None, None, D, wq), lambda i, q: (q // 3, j, 0, q % 3))],
        out_specs=[pl.BlockSpec((None, tm, wq), lambda i, q: (q // N_CHIPS, i, q % N_CHIPS)), row],
        out_shape=[jax.ShapeDtypeStruct((3, T, D), F32), jax.ShapeDtypeStruct((T, D), BF16)],
        scratch_shapes=[pltpu.VMEM((tm, D), BF16)],
        sem=("parallel", "arbitrary"), args=(h, mod, g, w_in))


def _conv_cols(dm):
    return 256 if dm.D % 256 == 0 else 128


def _seg_masks(r, dm):
    bn = dm.B * dm.N
    lat = r < bn
    off = jnp.where(lat, lax.rem(r, dm.N), lax.rem(r - bn, dm.CTX))
    seg = jnp.where(lat, dm.N, dm.CTX)
    inside = (r >= 0) & (r < dm.T)
    return ((off != 0) & inside).astype(F32), ((off != seg - 1) & inside).astype(F32)


def _conv_specs(dm):
    tb, cb, nr8 = dm.tm, _conv_cols(dm), dm.T // 8
    prev8 = lambda c, i: jnp.maximum(i * (tb // 8) - 1, 0)
    next8 = lambda c, i: jnp.minimum((i + 1) * (tb // 8), nr8 - 1)
    return dict(
        tb=tb, cb=cb,
        p=pl.BlockSpec((3, tb, cb), lambda c, i: (0, i, c)),
        p_prev=pl.BlockSpec((3, 8, cb), lambda c, i: (0, prev8(c, i), c)),
        p_next=pl.BlockSpec((3, 8, cb), lambda c, i: (0, next8(c, i), c)),
        row=pl.BlockSpec((tb, cb), lambda c, i: (i, c)),
        row_prev=pl.BlockSpec((8, cb), lambda c, i: (prev8(c, i), c)),
        row_next=pl.BlockSpec((8, cb), lambda c, i: (next8(c, i), c)),
        w=pl.BlockSpec((3, cb), lambda c, i: (0, c)),
    )


def _shift_rows(x, before, after, tb):
    rid = lax.broadcasted_iota(jnp.int32, x.shape, 0)
    down = jnp.where(rid == 0, before, pltpu.roll(x, 1, 0))
    up = jnp.where(rid == tb - 1, after, pltpu.roll(x, tb - 1, 0))
    return down, up


def _conv_fwd(p, wc, dm, name):
    T, D = dm.T, dm.D
    sp = _conv_specs(dm)
    tb, cb = sp["tb"], sp["cb"]

    def body(p_ref, pp_ref, pn_ref, w_ref, z_ref):
        r = pl.program_id(1) * tb + lax.broadcasted_iota(jnp.int32, (tb, cb), 0)
        mp, mn = _seg_masks(r, dm)
        cu = p_ref[1] * p_ref[2]
        prev, nxt = _shift_rows(cu, pp_ref[1, 7:8, :] * pp_ref[2, 7:8, :], pn_ref[1, 0:1, :] * pn_ref[2, 0:1, :], tb)
        conv = w_ref[0:1, :] * (prev * mp) + w_ref[1:2, :] * cu + w_ref[2:3, :] * (nxt * mn)
        z_ref[...] = (p_ref[0] * conv).astype(BF16)

    return pl.pallas_call(
        body, name=name, grid=(D // cb, T // tb),
        in_specs=[sp["p"], sp["p_prev"], sp["p_next"], sp["w"]], out_specs=sp["row"],
        out_shape=jax.ShapeDtypeStruct((T, D), BF16),
        compiler_params=_cparams("parallel", "parallel"),
    )(p, p, p, wc)


def _conv_bwd(dz, p, wc, dm, name):
    T, D = dm.T, dm.D
    sp = _conv_specs(dm)
    tb, cb = sp["tb"], sp["cb"]

    def body(dz_ref, dzp_ref, dzn_ref, p_ref, pp_ref, pn_ref, w_ref, dp_ref, dw_ref):
        i = pl.program_id(1)
        r = i * tb + lax.broadcasted_iota(jnp.int32, (tb, cb), 0)
        mp, mn = _seg_masks(r, dm)
        rb = i * tb + lax.broadcasted_iota(jnp.int32, (1, cb), 0)
        _, mn_before = _seg_masks(rb - 1, dm)
        mp_after, _ = _seg_masks(rb + tb, dm)
        bg, cg, u = p_ref[0], p_ref[1], p_ref[2]
        cu = cg * u
        prev, nxt = _shift_rows(cu, pp_ref[1, 7:8, :] * pp_ref[2, 7:8, :], pn_ref[1, 0:1, :] * pn_ref[2, 0:1, :], tb)
        prev = prev * mp
        nxt = nxt * mn
        w0, w1, w2 = w_ref[0:1, :], w_ref[1:2, :], w_ref[2:3, :]
        conv = w0 * prev + w1 * cu + w2 * nxt
        dz = dz_ref[...]
        dp_ref[0] = dz * conv
        dconv = dz * bg

        @pl.when(i == 0)
        def _():
            dw_ref[...] = jnp.zeros_like(dw_ref)

        dw_ref[0:1, :] += jnp.sum(dconv * prev, axis=0, keepdims=True)
        dw_ref[1:2, :] += jnp.sum(dconv * cu, axis=0, keepdims=True)
        dw_ref[2:3, :] += jnp.sum(dconv * nxt, axis=0, keepdims=True)
        dconv_before = dzp_ref[7:8, :] * pp_ref[0, 7:8, :] * mn_before
        dconv_after = dzn_ref[0:1, :] * pn_ref[0, 0:1, :] * mp_after
        from_prev, _ = _shift_rows(dconv * mn, dconv_before, dconv_after, tb)
        _, from_next = _shift_rows(dconv * mp, dconv_before, dconv_after, tb)
        dcu = w1 * dconv + w0 * from_next + w2 * from_prev
        dp_ref[1] = dcu * u
        dp_ref[2] = dcu * cg

    return pl.pallas_call(
        body, name=name, grid=(D // cb, T // tb),
        in_specs=[sp["row"], sp["row_prev"], sp["row_next"], sp["p"], sp["p_prev"], sp["p_next"], sp["w"]],
        out_specs=[sp["p"], sp["w"]],
        out_shape=[jax.ShapeDtypeStruct((3, T, D), F32), jax.ShapeDtypeStruct((3, D), F32)],
        compiler_params=_cparams("parallel", "arbitrary"),
    )(dz, dz, dz, p, p, p, wc)


def _out_fwd(z, w, h, mod, j, dm, name):
    T, D = h.shape
    K = z.shape[1]
    tm = dm.tm
    grp = _grp(dm)

    def body(z_ref, w_ref, h_ref, mod_ref, ho_ref, y_ref):
        y = _dot(z_ref[...], w_ref[...])
        y_ref[...] = y.astype(BF16)
        ho_ref[...] = h_ref[...] + mod_ref[0, 5:6, :] * y

    row = pl.BlockSpec((tm, D), lambda i: (i, 0))
    return pl.pallas_call(
        body, name=name, grid=(T // tm,),
        in_specs=[pl.BlockSpec((tm, K), lambda i: (i, 0)), pl.BlockSpec((None, K, D), lambda i: (j, 0, 0)), row,
                  pl.BlockSpec((1, 9, D), lambda i: (grp(i), 0, 0))],
        out_specs=[row, row],
        out_shape=[jax.ShapeDtypeStruct((T, D), F32), jax.ShapeDtypeStruct((T, D), BF16)],
        compiler_params=_cparams("parallel"),
    )(z, w, h, mod)


def _out_bwd(dh, y, w, mod, j, dm, name):
    T, D = dh.shape
    K = w.shape[1]
    tm = dm.tm
    grp = _grp(dm)

    def body(dh_ref, y_ref, w_ref, mod_ref, dy_ref, dz_ref, part_ref):
        dhv = dh_ref[...]
        dy = (mod_ref[0, 5:6, :] * dhv).astype(BF16)
        dy_ref[...] = dy
        dz_ref[...] = _dot_nt(dy, w_ref[...])
        _write_part(part_ref, dgate=jnp.sum(dhv * y_ref[...].astype(F32), axis=0, keepdims=True))

    row = pl.BlockSpec((tm, D), lambda i: (i, 0))
    return pl.pallas_call(
        body, name=name, grid=(T // tm,),
        in_specs=[row, row, pl.BlockSpec((None, K, D), lambda i: (j, 0, 0)), pl.BlockSpec((1, 9, D), lambda i: (grp(i), 0, 0))],
        out_specs=[row, pl.BlockSpec((tm, K), lambda i: (i, 0)), pl.BlockSpec((1, 8, D), lambda i: (i, 0, 0))],
        out_shape=[jax.ShapeDtypeStruct((T, D), BF16), jax.ShapeDtypeStruct((T, K), F32),
                   jax.ShapeDtypeStruct((T // tm, 8, D), F32)],
        compiler_params=_cparams("parallel"),
    )(dh, y, w, mod)


def _sc_in_bwd(dh, dp, h, mod, g, w_in, j, dm, name):
    T, D = h.shape
    tm = dm.tm
    wq = D // N_CHIPS
    nq = 3 * N_CHIPS
    grp = _grp(dm)

    def body(dh_ref, dp_ref, h_ref, mod_ref, g_ref, w_ref, dho_ref, part_ref, acc):
        q = pl.program_id(1)

        @pl.when(q == 0)
        def _():
            acc[...] = jnp.zeros_like(acc)

        acc[...] += _dot_nt(dp_ref[...].astype(BF16), w_ref[...])

        @pl.when(q == nq - 1)
        def _():
            dhb, dshift, dscale, dg = _pre_bwd(acc[...], h_ref[...], g_ref[...], mod_ref[0, 4:5, :])
            dho_ref[...] = dh_ref[...] + dhb
            _write_part(part_ref, dshift, dscale, None, dg)

    row = pl.BlockSpec((tm, D), lambda i, q: (i, 0))
    return pl.pallas_call(
        body, name=name, grid=(T // tm, nq),
        in_specs=[row, pl.BlockSpec((None, tm, wq), lambda i, q: (q // N_CHIPS, i, q % N_CHIPS)), row,
                  pl.BlockSpec((1, 9, D), lambda i, q: (grp(i), 0, 0)), pl.BlockSpec((1, D), lambda i, q: (0, 0)),
                  pl.BlockSpec((None, None, D, wq), lambda i, q: (q // 3, j, 0, q % 3))],
        out_specs=[row, pl.BlockSpec((1, 8, D), lambda i, q: (i, 0, 0))],
        out_shape=[jax.ShapeDtypeStruct((T, D), F32), jax.ShapeDtypeStruct((T // tm, 8, D), F32)],
        scratch_shapes=[pltpu.VMEM((tm, D), F32)],
        compiler_params=_cparams("parallel", "arbitrary"),
    )(dh, dp, h, mod, g, w_in)


def _rope(t, c, s1, s2):
    return t * c + pltpu.roll(t, HEAD_PAD - 16, 1) * s1 + pltpu.roll(t, 16, 1) * s2


def _rope_t(dy, c, s1, s2):
    return dy * c + pltpu.roll(dy * s1, 16, 1) + pltpu.roll(dy * s2, HEAD_PAD - 16, 1)


def _mla_heads_fwd(z, g_ref, wuq_ref, wukv_ref):
    cq, ckv, krp = z[:, :Q_LORA], z[:, Q_LORA:Q_LORA + KV_LORA], z[:, Q_LORA + KV_LORA:]
    cqh, rq = _rms(cq, Q_LORA)
    ckvh, rkv = _rms(ckv, KV_LORA)
    cqn = (cqh * g_ref[0:1, :]).astype(BF16)
    ckvn = (ckvh * g_ref[1:2, :KV_LORA]).astype(BF16)
    qraw = _dot(cqn, wuq_ref[...])
    kvraw = _dot(ckvn, wukv_ref[...])
    return dict(krp=krp, cqh=cqh, rq=rq, ckvh=ckvh, rkv=rkv, cqn=cqn, ckvn=ckvn, qraw=qraw, kvraw=kvraw)


def _mla_proj_fwd(h, mod, g, gains, tabs, w_a, w_uq, w_ukv, j, dm, name):
    T, D = h.shape
    tm = min(dm.tm, 256)
    grp = lambda i: jnp.minimum(i // (dm.N // tm), dm.B)
    HP = HEAD_PAD

    def body(h_ref, mod_ref, g_ref, gn_ref, tab_ref, wa_ref, wuq_ref, wukv_ref, hn_ref, q_ref, k_ref, v_ref):
        hn = _pre(h_ref[...], g_ref[...], mod_ref[0, 3:4, :], mod_ref[0, 4:5, :]).astype(BF16)
        hn_ref[...] = hn
        f = _mla_heads_fwd(_dot(hn, wa_ref[...]), gn_ref, wuq_ref, wukv_ref)
        c, s1, s2 = tab_ref[0], tab_ref[1], tab_ref[2]
        for hd in range(HEADS):
            qh, _ = _rms(f["qraw"][:, hd * HP:(hd + 1) * HP], QK_HEAD)
            q_ref[:, hd * HP:(hd + 1) * HP] = (_rope(qh * gn_ref[2:3, :], c, s1, s2) * QK_SCALE).astype(BF16)
            kpre = jnp.concatenate([f["kvraw"][:, hd * HP:hd * HP + QK_NOPE], f["krp"]], axis=1)
            kh, _ = _rms(kpre, QK_HEAD)
            k_ref[:, hd * HP:(hd + 1) * HP] = _rope(kh * gn_ref[3:4, :], c, s1, s2).astype(BF16)
            v_ref[:, hd * V_HEAD:(hd + 1) * V_HEAD] = f["kvraw"][:, hd * HP + QK_NOPE:(hd + 1) * HP].astype(BF16)

    row = pl.BlockSpec((tm, D), lambda i: (i, 0))
    HQ = HEADS * HP
    return pl.pallas_call(
        body, name=name, grid=(T // tm,),
        in_specs=[row, pl.BlockSpec((1, 9, D), lambda i: (grp(i), 0, 0)), pl.BlockSpec((1, D), lambda i: (0, 0)),
                  pl.BlockSpec((None, 8, HP), lambda i: (j, 0, 0)), pl.BlockSpec((3, tm, HP), lambda i: (0, i, 0)),
                  pl.BlockSpec((None, D, 512), lambda i: (j, 0, 0)), pl.BlockSpec((None, Q_LORA, HQ), lambda i: (j, 0, 0)),
                  pl.BlockSpec((None, KV_LORA, HQ), lambda i: (j, 0, 0))],
        out_specs=[row, pl.BlockSpec((tm, HQ), lambda i: (i, 0)), pl.BlockSpec((tm, HQ), lambda i: (i, 0)),
                   pl.BlockSpec((tm, HEADS * V_HEAD), lambda i: (i, 0))],
        out_shape=[jax.ShapeDtypeStruct((T, D), BF16), jax.ShapeDtypeStruct((T, HQ), BF16),
                   jax.ShapeDtypeStruct((T, HQ), BF16), jax.ShapeDtypeStruct((T, HEADS * V_HEAD), BF16)],
        compiler_params=_cparams("parallel"),
    )(h, mod, g, gains, tabs, w_a, w_uq, w_ukv)


def _mla_proj_bwd(dh, dq, dkl, dkc, dvl, dvc, h, mod, g, gains, tabs, w_a, w_uq, w_ukv, j, dm, name):
    T, D = h.shape
    tm = min(dm.tm, 256)
    nblk = T // tm
    grp = lambda i: jnp.minimum(i // (dm.N // tm), dm.B)
    HP = HEAD_PAD
    HQ = HEADS * HP

    nlat = dm.B * dm.N // tm

    def body(dh_ref, dq_ref, dkl_ref, dkc_ref, dvl_ref, dvc_ref, h_ref, mod_ref, g_ref, gn_ref, tab_ref, wa_ref, wuq_ref, wukv_ref,
             dho_ref, part_ref, gwa_ref, gwuq_ref, gwukv_ref, dgn_ref, dqraw_s, dkvraw_s):
        i = pl.program_id(0)
        pick = lambda lat_ref, ctx_ref, cols: jnp.where(i < nlat, lat_ref[:, cols], ctx_ref[:, cols])

        @pl.when(i == 0)
        def _():
            gwa_ref[...] = jnp.zeros_like(gwa_ref)
            gwuq_ref[...] = jnp.zeros_like(gwuq_ref)
            gwukv_ref[...] = jnp.zeros_like(gwukv_ref)
            dgn_ref[...] = jnp.zeros_like(dgn_ref)

        hv = h_ref[...]
        hn = _pre(hv, g_ref[...], mod_ref[0, 3:4, :], mod_ref[0, 4:5, :]).astype(BF16)
        f = _mla_heads_fwd(_dot(hn, wa_ref[...]), gn_ref, wuq_ref, wukv_ref)
        c, s1, s2 = tab_ref[0], tab_ref[1], tab_ref[2]
        gq, gk = gn_ref[2:3, :], gn_ref[3:4, :]
        dgq = jnp.zeros((1, HP), F32)
        dgk = jnp.zeros((1, HP), F32)
        dkrp = jnp.zeros((tm, HP - QK_NOPE), F32)
        for hd in range(HEADS):
            qh, rq = _rms(f["qraw"][:, hd * HP:(hd + 1) * HP], QK_HEAD)
            dqn = _rope_t(dq_ref[:, hd * HP:(hd + 1) * HP] * QK_SCALE, c, s1, s2)
            dgq = dgq + jnp.sum(dqn * qh, axis=0, keepdims=True)
            dqraw_s[:, hd * HP:(hd + 1) * HP] = _rms_bwd(dqn * gq, qh, rq, QK_HEAD)
            kpre = jnp.concatenate([f["kvraw"][:, hd * HP:hd * HP + QK_NOPE], f["krp"]], axis=1)
            kh, rk = _rms(kpre, QK_HEAD)
            dkn = _rope_t(pick(dkl_ref, dkc_ref, slice(hd * HP, (hd + 1) * HP)), c, s1, s2)
            dgk = dgk + jnp.sum(dkn * kh, axis=0, keepdims=True)
            dkpre = _rms_bwd(dkn * gk, kh, rk, QK_HEAD)
            dkvraw_s[:, hd * HP:hd * HP + QK_NOPE] = dkpre[:, :QK_NOPE]
            dkrp = dkrp + dkpre[:, QK_NOPE:]
            dkvraw_s[:, hd * HP + QK_NOPE:(hd + 1) * HP] = pick(dvl_ref, dvc_ref, slice(hd * V_HEAD, (hd + 1) * V_HEAD))
        dqraw = dqraw_s[...].astype(BF16)
        dkvraw = dkvraw_s[...].astype(BF16)
        gwuq_ref[...] += _dot_tn(f["cqn"], dqraw)
        gwukv_ref[...] += _dot_tn(f["ckvn"], dkvraw)
        dcqn = _dot_nt(dqraw, wuq_ref[...])
        dckvn = _dot_nt(dkvraw, wukv_ref[...])
        dgqa = jnp.sum(dcqn * f["cqh"], axis=0, keepdims=True)
        dgkva = jnp.sum(dckvn * f["ckvh"], axis=0, keepdims=True)
        dcq = _rms_bwd(dcqn * gn_ref[0:1, :], f["cqh"], f["rq"], Q_LORA)
        dckv = _rms_bwd(dckvn * gn_ref[1:2, :KV_LORA], f["ckvh"], f["rkv"], KV_LORA)
        dz = jnp.concatenate([dcq, dckv, dkrp], axis=1).astype(BF16)
        gwa_ref[...] += _dot_tn(hn, dz)
        dhn = _dot_nt(dz, wa_ref[...])
        dhb, dshift, dscale, dg = _pre_bwd(dhn, hv, g_ref[...], mod_ref[0, 4:5, :])
        dho_ref[...] = dh_ref[...] + dhb
        _write_part(part_ref, dshift, dscale, None, dg)
        dgn_ref[0:1, :] += dgqa
        dgn_ref[1:2, :KV_LORA] += dgkva
        dgn_ref[2:3, :] += dgq
        dgn_ref[3:4, :] += dgk

    row = pl.BlockSpec((tm, D), lambda i: (i, 0))
    wide = pl.BlockSpec((tm, HQ), lambda i: (i, 0))
    const2 = lambda i: (0, 0)
    return pl.pallas_call(
        body, name=name, grid=(nblk,),
        in_specs=[row, wide, pl.BlockSpec((tm, HQ), lambda i: (jnp.minimum(i, nlat - 1), 0)),
                  pl.BlockSpec((tm, HQ), lambda i: (jnp.maximum(i - nlat, 0), 0)),
                  pl.BlockSpec((tm, HEADS * V_HEAD), lambda i: (jnp.minimum(i, nlat - 1), 0)),
                  pl.BlockSpec((tm, HEADS * V_HEAD), lambda i: (jnp.maximum(i - nlat, 0), 0)), row,
                  pl.BlockSpec((1, 9, D), lambda i: (grp(i), 0, 0)), pl.BlockSpec((1, D), const2),
                  pl.BlockSpec((None, 8, HP), lambda i: (j, 0, 0)), pl.BlockSpec((3, tm, HP), lambda i: (0, i, 0)),
                  pl.BlockSpec((None, D, 512), lambda i: (j, 0, 0)), pl.BlockSpec((None, Q_LORA, HQ), lambda i: (j, 0, 0)),
                  pl.BlockSpec((None, KV_LORA, HQ), lambda i: (j, 0, 0))],
        out_specs=[row, pl.BlockSpec((1, 8, D), lambda i: (i, 0, 0)), pl.BlockSpec((D, 512), const2),
                   pl.BlockSpec((Q_LORA, HQ), const2), pl.BlockSpec((KV_LORA, HQ), const2), pl.BlockSpec((8, HP), const2)],
        out_shape=[jax.ShapeDtypeStruct((T, D), F32), jax.ShapeDtypeStruct((nblk, 8, D), F32),
                   jax.ShapeDtypeStruct((D, 512), F32), jax.ShapeDtypeStruct((Q_LORA, HQ), F32),
                   jax.ShapeDtypeStruct((KV_LORA, HQ), F32), jax.ShapeDtypeStruct((8, HP), F32)],
        scratch_shapes=[pltpu.VMEM((tm, HQ), F32), pltpu.VMEM((tm, HQ), F32)],
        compiler_params=_cparams("arbitrary"),
    )(dh, dq, dkl, dkc, dvl, dvc, h, mod, g, gains, tabs, w_a, w_uq, w_ukv)


def _attn_specs(dm):
    tq = dm.CTX
    nq = dm.N // tq
    cblk0 = dm.B * nq
    HP = HEAD_PAD
    qrow = lambda b, i: jnp.where(i < nq, b * nq + i, cblk0 + b)
    return dict(
        tq=tq, nq=nq,
        q=pl.BlockSpec((tq, HP), lambda b, hd, i: (qrow(b, i), hd)),
        k_lat=pl.BlockSpec((dm.N, HP), lambda b, hd, i: (b, hd)),
        k_ctx=pl.BlockSpec((tq, HP), lambda b, hd, i: (cblk0 + b, hd)),
        v_lat=pl.BlockSpec((dm.N, V_HEAD), lambda b, hd, i: (b, hd)),
        v_ctx=pl.BlockSpec((tq, V_HEAD), lambda b, hd, i: (cblk0 + b, hd)),
        o=pl.BlockSpec((tq, V_HEAD), lambda b, hd, i: (qrow(b, i), hd)),
    )


def _attn_exp(q, keys):
    s = [_dot_nt(q, kk) for kk in keys]
    m = functools.reduce(jnp.maximum, [jnp.max(x, axis=-1, keepdims=True) for x in s])
    e = [jnp.exp(x - m) for x in s]
    return e, 1.0 / sum(jnp.sum(x, axis=-1, keepdims=True) for x in e)


def _attn_fwd(q, k, v, dm, name, rider=None):
    T = dm.T
    sp = _attn_specs(dm)
    nq = sp["nq"]

    def body(q_ref, kl_ref, kc_ref, vl_ref, vc_ref, o_ref):
        i = pl.program_id(2)

        @pl.when(i < nq)
        def _():
            (el, ec), inv = _attn_exp(q_ref[...], [kl_ref[...], kc_ref[...]])
            o_ref[...] = ((_dot(el.astype(BF16), vl_ref[...]) + _dot(ec.astype(BF16), vc_ref[...])) * inv).astype(BF16)

        @pl.when(i == nq)
        def _():
            (ec,), inv = _attn_exp(q_ref[...], [kc_ref[...]])
            o_ref[...] = (_dot(ec.astype(BF16), vc_ref[...]) * inv).astype(BF16)

    (o,), got = _hosted(
        body, rider, name=name, grid=(dm.B, HEADS, nq + 1),
        in_specs=[sp["q"], sp["k_lat"], sp["k_ctx"], sp["v_lat"], sp["v_ctx"]], out_specs=[sp["o"]],
        out_shape=[jax.ShapeDtypeStruct((T, HEADS * V_HEAD), BF16)], scratch_shapes=[],
        sem=("parallel", "parallel", "arbitrary"), args=(q, k, k, v, v))
    return o, got


def _attn_bwd(q, k, v, o, do, dm, name):
    T = dm.T
    sp = _attn_specs(dm)
    nq, tq = sp["nq"], sp["tq"]
    HP, HQ, HV = HEAD_PAD, HEADS * HEAD_PAD, HEADS * V_HEAD

    def body(q_ref, kl_ref, kc_ref, vl_ref, vc_ref, o_ref, do_ref, dq_ref, dkl_ref, dkc_ref, dvl_ref, dvc_ref):
        i = pl.program_id(2)

        @pl.when(i == 0)
        def _():
            dkl_ref[...] = jnp.zeros_like(dkl_ref)
            dkc_ref[...] = jnp.zeros_like(dkc_ref)
            dvl_ref[...] = jnp.zeros_like(dvl_ref)
            dvc_ref[...] = jnp.zeros_like(dvc_ref)

        def grads(k_refs, v_refs, dk_refs, dv_refs):
            qv = q_ref[...]
            dov = do_ref[...]
            dob = dov.astype(BF16)
            delta = jnp.sum(dov * o_ref[...].astype(F32), axis=-1, keepdims=True)
            e, inv = _attn_exp(qv, [r[...] for r in k_refs])
            dq = None
            for ex, k_ref, v_ref, dk_ref, dv_ref in zip(e, k_refs, v_refs, dk_refs, dv_refs):
                p = ex * inv
                ds = (p * (_dot_nt(dob, v_ref[...]) - delta)).astype(BF16)
                term = _dot(ds, k_ref[...])
                dq = term if dq is None else dq + term
                dk_ref[...] += _dot_tn(ds, qv)
                dv_ref[...] += _dot_tn(p.astype(BF16), dob)
            dq_ref[...] = dq

        @pl.when(i < nq)
        def _():
            grads([kl_ref, kc_ref], [vl_ref, vc_ref], [dkl_ref, dkc_ref], [dvl_ref, dvc_ref])

        @pl.when(i == nq)
        def _():
            grads([kc_ref], [vc_ref], [dkc_ref], [dvc_ref])

    return pl.pallas_call(
        body, name=name, grid=(dm.B, HEADS, nq + 1),
        in_specs=[sp["q"], sp["k_lat"], sp["k_ctx"], sp["v_lat"], sp["v_ctx"], sp["o"], sp["o"]],
        out_specs=[sp["q"], sp["k_lat"], pl.BlockSpec((tq, HP), lambda b, hd, i: (b, hd)),
                   sp["v_lat"], pl.BlockSpec((tq, V_HEAD), lambda b, hd, i: (b, hd))],
        out_shape=[jax.ShapeDtypeStruct((T, HQ), F32), jax.ShapeDtypeStruct((dm.B * dm.N, HQ), F32),
                   jax.ShapeDtypeStruct((dm.B * dm.CTX, HQ), F32), jax.ShapeDtypeStruct((dm.B * dm.N, HV), F32),
                   jax.ShapeDtypeStruct((dm.B * dm.CTX, HV), F32)],
        compiler_params=_cparams("parallel", "parallel", "arbitrary"),
    )(q, k, k, v, v, o, do)


def _loss_grad(h, target, dm, name):
    T, D = h.shape
    tm = dm.tm
    nlat = dm.B * dm.N // tm

    def body(h_ref, t_ref, dh_ref, ls_ref):
        lat = (pl.program_id(0) < nlat).astype(F32)
        diff = (h_ref[...] - t_ref[...]) * lat
        dh_ref[...] = diff * (1.0 / D)
        ls_ref[...] = jnp.zeros(ls_ref.shape, F32) + (0.5 / D) * jnp.sum(diff * diff)

    return pl.pallas_call(
        body, name=name, grid=(T // tm,),
        in_specs=[pl.BlockSpec((tm, D), lambda i: (i, 0)), pl.BlockSpec((tm, D), lambda i: (jnp.minimum(i, nlat - 1), 0))],
        out_specs=[pl.BlockSpec((tm, D), lambda i: (i, 0)), pl.BlockSpec((1, 8, 128), lambda i: (i, 0, 0))],
        out_shape=[jax.ShapeDtypeStruct((T, D), F32), jax.ShapeDtypeStruct((T // tm, 8, 128), F32)],
        compiler_params=_cparams("parallel"),
    )(h, target)


def _col_block(cols, target=1152):
    return max(t for t in range(128, min(cols, target) + 1, 128) if cols % t == 0)


def _mod_fwd(cond, w_mod, b_mod, name):
    L, D, C = w_mod.shape
    R = cond.shape[0]
    cb = _col_block(C)

    def body(c_ref, w_ref, b_ref, o_ref):
        cv = c_ref[...]
        sc = (cv * jax.nn.sigmoid(cv)).astype(BF16)
        o_ref[...] = _dot(sc, w_ref[...].astype(BF16)) + b_ref[...]

    return pl.pallas_call(
        body, name=name, grid=(L, C // cb),
        in_specs=[pl.BlockSpec((R, D), lambda l, c: (0, 0)), pl.BlockSpec((None, D, cb), lambda l, c: (l, 0, c)),
                  pl.BlockSpec((None, 1, cb), lambda l, c: (l, 0, c))],
        out_specs=pl.BlockSpec((None, R, cb), lambda l, c: (l, 0, c)),
        out_shape=jax.ShapeDtypeStruct((L, R, C), F32),
        compiler_params=_cparams("parallel", "parallel"),
    )(cond, w_mod, b_mod)


def _mod_bwd(cond, dmod, w_mod, name):
    L, D, C = w_mod.shape
    R = cond.shape[0]
    cb = _col_block(C)
    nc = C // cb

    def body(c_ref, dm_ref, w_ref, gw_ref, ds_ref):
        cv = c_ref[...]
        sc = (cv * jax.nn.sigmoid(cv)).astype(BF16)
        dmv = dm_ref[...].astype(BF16)
        gw_ref[...] = _dot_tn(sc, dmv)
        part = _dot_nt(dmv, w_ref[...].astype(BF16))

        @pl.when(pl.program_id(1) == 0)
        def _():
            ds_ref[...] = part

        @pl.when(pl.program_id(1) > 0)
        def _():
            ds_ref[...] += part

    return pl.pallas_call(
        body, name=name, grid=(L, nc),
        in_specs=[pl.BlockSpec((R, D), lambda l, c: (0, 0)), pl.BlockSpec((None, R, cb), lambda l, c: (l, 0, c)),
                  pl.BlockSpec((None, D, cb), lambda l, c: (l, 0, c))],
        out_specs=[pl.BlockSpec((None, D, cb), lambda l, c: (l, 0, c)), pl.BlockSpec((None, R, D), lambda l, c: (l, 0, 0))],
        out_shape=[jax.ShapeDtypeStruct((L, D, C), F32), jax.ShapeDtypeStruct((L, R, D), F32)],
        compiler_params=_cparams("parallel", "arbitrary"),
    )(cond, dmod, w_mod)


def _row_block(rows, cols, budget=1 << 20):
    best = None
    for t in range(16, rows + 1, 16):
        if rows % t == 0 and t * cols * 4 <= budget:
            best = t
    return best if best is not None else rows


def _sum_slots(recv, own, chip, core, buf, piece, name):
    S, R, C = recv.shape
    tr = _row_block(R, C, budget=512 << 10)

    def body(ids_ref, r_ref, p_ref, b_ref, o_ref):
        del b_ref
        acc = None
        for s in range(S):
            v = jnp.where(ids_ref[0] == s, p_ref[s], r_ref[s]).astype(F32)
            acc = v if acc is None else acc + v
        o_ref[...] = acc

    blk = pl.BlockSpec((S, tr, C), lambda i, ids: (0, i, 0))
    return pl.pallas_call(
        body, name=name,
        grid_spec=pltpu.PrefetchScalarGridSpec(
            num_scalar_prefetch=1, grid=(R // tr,), in_specs=[blk, blk, pl.BlockSpec(memory_space=pl.ANY)],
            out_specs=pl.BlockSpec((None, None, tr, C), lambda i, ids: (piece, ids[1], i, 0))),
        out_shape=jax.ShapeDtypeStruct(buf.shape, F32), input_output_aliases={3: 0}, compiler_params=_cparams("parallel"),
    )(jnp.stack([chip, core]).astype(jnp.int32), recv, own, buf)


def _adamw(w, gs, m, v, name):
    ng = len(gs)
    R, C = w.shape
    tr = _row_block(R, C)
    c1 = 1.0 / (1.0 - ADAM_B1 ** ADAM_STEP)
    c2 = 1.0 / (1.0 - ADAM_B2 ** ADAM_STEP)

    def body(w_ref, *refs):
        m_ref, v_ref, g_ref, d_ref, mo_ref, vo_ref = refs[ng:]
        g = refs[0][...]
        for g_more in refs[1:ng]:
            g = g + g_more[...]
        g_ref[...] = g
        mn = ADAM_B1 * m_ref[...] + (1.0 - ADAM_B1) * g
        vn = ADAM_B2 * v_ref[...] + (1.0 - ADAM_B2) * (g * g)
        mo_ref[...] = mn
        vo_ref[...] = vn
        d_ref[...] = -ADAM_LR * ((mn * c1) / (jnp.sqrt(vn * c2) + ADAM_EPS) + ADAM_WD * w_ref[...])

    blk = pl.BlockSpec((tr, C), lambda i: (i, 0))
    sd = jax.ShapeDtypeStruct((R, C), F32)
    return pl.pallas_call(
        body, name=name, grid=(R // tr,), in_specs=[blk] * (3 + ng), out_specs=[blk] * 4, out_shape=[sd] * 4,
        compiler_params=_cparams("parallel"),
    )(w, *gs, m, v)


def _rope_tables(dm):
    n = dm.N
    t = jnp.arange(n)
    r = (t // GRID_W).astype(F32)
    col = (t % GRID_W).astype(F32)
    nf = QK_ROPE // 4
    inv = ROPE_BASE ** (-jnp.arange(nf, dtype=F32) / nf)
    ang = jnp.stack([r[:, None] * inv, col[:, None] * inv], axis=1)
    cos, sin = jnp.cos(ang), jnp.sin(ang)
    zero = jnp.zeros_like(sin)
    c64 = jnp.stack([cos, cos], axis=2).reshape(n, QK_ROPE)
    s1 = jnp.stack([-sin, zero], axis=2).reshape(n, QK_ROPE)
    s2 = jnp.stack([zero, sin], axis=2).reshape(n, QK_ROPE)

    def pad(x, fill):
        return jnp.concatenate([jnp.full((n, QK_NOPE), fill, F32), x, jnp.full((n, HEAD_PAD - QK_HEAD), fill, F32)], axis=1)

    lat = jnp.stack([pad(c64, 1.0), pad(s1, 0.0), pad(s2, 0.0)])
    lat = jnp.tile(lat, (1, dm.B, 1))
    nctx = dm.B * dm.CTX
    ctx = jnp.stack([jnp.ones((nctx, HEAD_PAD), F32), jnp.zeros((nctx, HEAD_PAD), F32), jnp.zeros((nctx, HEAD_PAD), F32)])
    return jnp.concatenate([lat, ctx], axis=1)


def _fold_parts(part, dm):
    nblk = part.shape[0]
    nb = (dm.N * nblk) // dm.T
    groups = [part[b * nb:(b + 1) * nb].sum(axis=0) for b in range(dm.B)]
    groups.append(part[dm.B * nb:].sum(axis=0))
    return jnp.stack(groups)


def _flat2(a):
    return a.reshape(-1, a.shape[-1])


def kernel(x, c, ctx, c_ctx, w_mod, b_mod, g_norm, ffn_w1, ffn_w3, ffn_w2, sc_w_in, sc_conv, sc_w_out, mla_w_a, mla_g_qa, mla_w_uq, mla_g_kva, mla_w_ukv, mla_g_q, mla_g_k, mla_w_o, loss_target, m_c_ctx, m_w_mod, m_b_mod, m_g_norm, m_ffn_w1, m_ffn_w3, m_ffn_w2, m_sc_w_in, m_sc_conv, m_sc_w_out, m_mla_w_a, m_mla_g_qa, m_mla_w_uq, m_mla_g_kva, m_mla_w_ukv, m_mla_g_q, m_mla_g_k, m_mla_w_o, v_c_ctx, v_w_mod, v_b_mod, v_g_norm, v_ffn_w1, v_ffn_w3, v_ffn_w2, v_sc_w_in, v_sc_conv, v_sc_w_out, v_mla_w_a, v_mla_g_qa, v_mla_w_uq, v_mla_g_kva, v_mla_w_ukv, v_mla_g_q, v_mla_g_k, v_mla_w_o):
    B, N, D = x.shape
    CTX = ctx.shape[1]
    T = B * (N + CTX)
    tm = next(t for t in (512, 256, 128, 64, 32, 16) if N % t == 0 and (B * CTX) % t == 0)
    dm = Dims(B, N, CTX, D, T, tm)
    L = w_mod.shape[0]
    La, Lb = sc_w_in.shape[0], mla_w_a.shape[0]
    S = N_CHIPS
    ndev = 2 * S
    xi, yi, ci = lax.axis_index("x"), lax.axis_index("y"), lax.axis_index("c")
    chip = 2 * xi + yi
    dev = 2 * chip + ci
    weights = dict(c_ctx=c_ctx, w_mod=w_mod, b_mod=b_mod, g_norm=g_norm, ffn_w1=ffn_w1, ffn_w3=ffn_w3, ffn_w2=ffn_w2,
                   sc_w_in=sc_w_in, sc_conv=sc_conv, sc_w_out=sc_w_out, mla_w_a=mla_w_a, mla_g_qa=mla_g_qa,
                   mla_w_uq=mla_w_uq, mla_g_kva=mla_g_kva, mla_w_ukv=mla_w_ukv, mla_g_q=mla_g_q, mla_g_k=mla_g_k,
                   mla_w_o=mla_w_o)
    mom = dict(c_ctx=(m_c_ctx, v_c_ctx), w_mod=(m_w_mod, v_w_mod), b_mod=(m_b_mod, v_b_mod), g_norm=(m_g_norm, v_g_norm),
               ffn_w1=(m_ffn_w1, v_ffn_w1), ffn_w3=(m_ffn_w3, v_ffn_w3), ffn_w2=(m_ffn_w2, v_ffn_w2),
               sc_w_in=(m_sc_w_in, v_sc_w_in), sc_conv=(m_sc_conv, v_sc_conv), sc_w_out=(m_sc_w_out, v_sc_w_out),
               mla_w_a=(m_mla_w_a, v_mla_w_a), mla_g_qa=(m_mla_g_qa, v_mla_g_qa), mla_w_uq=(m_mla_w_uq, v_mla_w_uq),
               mla_g_kva=(m_mla_g_kva, v_mla_g_kva), mla_w_ukv=(m_mla_w_ukv, v_mla_w_ukv), mla_g_q=(m_mla_g_q, v_mla_g_q),
               mla_g_k=(m_mla_g_k, v_mla_g_k), mla_w_o=(m_mla_w_o, v_mla_w_o))

    big = ["ffn_w1", "ffn_w3", "ffn_w2", "sc_w_in", "sc_w_out", "mla_w_a", "mla_w_uq", "mla_w_ukv", "mla_w_o"]
    F = ffn_w1.shape[-1]
    mixer_names = (["sc_w_in", "sc_w_out"], ["mla_w_a", "mla_w_uq", "mla_w_ukv", "mla_w_o"])

    def placed(name, piece):
        w2 = _flat2(weights[name])
        rows = w2.shape[0] // weights[name].shape[0]
        return _place_cast(w2, piece * rows, rows, chip, S, "place_weight")

    bufs = [{n: placed(n, l) for n in ("ffn_w1", "ffn_w3", "ffn_w2")} for l in range(L)]
    for l in range(L):
        bufs[l].update({n: placed(n, l // 2) for n in mixer_names[l % 2]})
    group_a = ["ffn_w1", "ffn_w3"]
    group_b = lambda l: ["ffn_w2"] + mixer_names[l % 2]
    first = group_a + group_b(0)
    bufs[0].update(zip(first, _ride_alone(_gather_rider([bufs[0][n] for n in first]), "gather_weights")))

    def layer_weights(l):
        b = bufs[l]
        w = dict(w1=b["ffn_w1"].reshape(S, 2, D, F), w3=b["ffn_w3"].reshape(S, 2, D, F), w2=b["ffn_w2"].reshape(S, 2, F, D))
        if l % 2 == 0:
            w["w_in"] = b["sc_w_in"][:, None]
            w["w_out"] = b["sc_w_out"].reshape(1, D, D)
        else:
            w["w_a"] = jnp.pad(b["mla_w_a"].reshape(1, D, -1), ((0, 0), (0, 0), (0, 512 - (Q_LORA + KV_LORA + QK_ROPE))))
            wuq = jnp.moveaxis(b["mla_w_uq"], 0, 1).reshape(1, Q_LORA, HEADS, QK_HEAD)
            w["w_uq"] = jnp.pad(wuq, ((0, 0), (0, 0), (0, 0), (0, HEAD_PAD - QK_HEAD))).reshape(1, Q_LORA, HEADS * HEAD_PAD)
            w["w_ukv"] = jnp.moveaxis(b["mla_w_ukv"], 0, 1).reshape(1, KV_LORA, HEADS * HEAD_PAD)
            w["w_o"] = b["mla_w_o"].reshape(1, HEADS * V_HEAD, D)
        return w

    vecs = ["g_norm", "sc_conv", "mla_g_qa"]
    gathered = _exchange([_flat2(weights[n]) for n in vecs], ("x", "y"), False, "gather_vectors")
    gw = {n: g.reshape((S,) + weights[n].shape) for n, g in zip(vecs, gathered)}
    gnorm = jnp.moveaxis(gw["g_norm"], 0, 2).reshape(L, 3, D)
    convw = jnp.moveaxis(gw["sc_conv"], 0, 2).reshape(La, 3, D)
    gqa = jnp.moveaxis(gw["mla_g_qa"], 0, 1).reshape(Lb, Q_LORA)
    padl = lambda a: jnp.pad(a, ((0, 0), (0, HEAD_PAD - a.shape[1])))
    gains = jnp.stack([padl(gqa), padl(mla_g_kva), padl(mla_g_q), padl(mla_g_k)], axis=1)
    gains = jnp.pad(gains, ((0, 0), (0, 4), (0, 0)))

    R = -(-(ndev * B + 1) // 16) * 16
    call = _exchange([c], ("x", "y", "c"), False, "gather_cond")[0].reshape(ndev * B, D)
    cond = jnp.concatenate([call, c_ctx[None], jnp.zeros((R - ndev * B - 1, D), F32)], axis=0)
    C = w_mod.shape[-1]
    bm = lax.dynamic_slice_in_dim(b_mod, chip * C, C, axis=1)[:, None, :]
    mshard = _mod_fwd(cond, w_mod, bm, "mod_fwd")
    mfull = _exchange([mshard.reshape(L * R, C)], ("x", "y"), False, "gather_mod")[0].reshape(S, L, R, C)
    mfull = jnp.moveaxis(mfull, 0, 2).reshape(L, R, S * C)
    mine = lax.dynamic_slice_in_dim(mfull, dev * B, B, axis=1)
    mod = jnp.concatenate([mine, mfull[:, ndev * B:ndev * B + 1]], axis=1).reshape(L, B + 1, 9, D)

    tabs = _rope_tables(dm)
    h = jnp.concatenate([x.reshape(B * N, D), ctx.reshape(B * CTX, D)], axis=0)

    saved = []
    lw = [None] * L
    for l in range(L):
        kind, j = l % 2, l // 2
        W = lw[l] = layer_weights(l)
        sv = {}
        sv["h0"] = h

        def riding(names):
            if l + 1 == L:
                return None, lambda got: None
            return _gather_rider([bufs[l + 1][n] for n in names]), lambda got: bufs[l + 1].update(zip(names, got))

        rider, keep = riding(["ffn_w1"])
        (h, sv["a1"], sv["b1"], sv["hn1"], sv["y1"]), got = _ffn_fwd(h, mod[l], gnorm[l, 0:1], W["w1"], W["w3"], W["w2"], 0, dm,
                                                                      "ffn_fwd", rider)
        keep(got)
        sv["h1"] = h
        rider, keep = riding(["ffn_w3"] + mixer_names[(l + 1) % 2])
        if kind == 0:
            (sv["p"], sv["hnm"]), got = _sc_in_fwd(h, mod[l], gnorm[l, 1:2], W["w_in"], 0, dm, "sc_in_fwd", rider)
            sv["z"] = _conv_fwd(sv["p"], convw[j], dm, "conv_fwd")
            h, sv["ym"] = _out_fwd(sv["z"], W["w_out"], h, mod[l], 0, dm, "sc_out_fwd")
        else:
            sv["hnm"], sv["q"], sv["k"], sv["v"] = _mla_proj_fwd(h, mod[l], gnorm[l, 1:2], gains[j:j + 1], tabs, W["w_a"], W["w_uq"],
                                                                 W["w_ukv"], 0, dm, "mla_proj_fwd")
            sv["o"], got = _attn_fwd(sv["q"], sv["k"], sv["v"], dm, "attn_fwd", rider)
            h, sv["ym"] = _out_fwd(sv["o"], W["w_o"], h, mod[l], 0, dm, "mla_out_fwd")
        keep(got)
        sv["h2"] = h
        rider, keep = riding(["ffn_w2"])
        (h, sv["a2"], sv["b2"], sv["hn2"], sv["y2"]), got = _ffn_fwd(h, mod[l], gnorm[l, 2:3], W["w1"], W["w3"], W["w2"], 1, dm,
                                                                      "ffn_fwd", rider)
        keep(got)
        saved.append(sv)

    dh, lsum = _loss_grad(h, loss_target.reshape(B * N, D), dm, "loss_grad")
    loss = lax.psum(jnp.sum(lsum[:, 0, 0]), ("x", "y", "c"))

    wq = D // S
    gsum = {}
    for n in big:
        w = weights[n]
        npieces = w.shape[0] * (w.shape[1] if n.startswith("ffn") else 1)
        gsum[n] = jnp.zeros((npieces, 2, _flat2(w).shape[0] // npieces // 2, w.shape[-1]), F32)
    dmod = [None] * L
    dgn = [None] * L
    dconv = [None] * La
    dgains = [None] * Lb
    tk = tm * next(f for f in (3, 2, 1) if (T // tm) % f == 0)
    nk = T // tk
    full_a = pl.BlockSpec((tk, D), lambda s, kk: (kk, 0))
    shard_b = pl.BlockSpec((None, tk, F), lambda s, kk: (s, kk, 0))
    per_slot = lambda r_, c_: pl.BlockSpec((None, r_, c_), lambda s, kk: (s, 0, 0))

    def make_job(grads):
        theirs = _swap_halves([g_ for _, _, g_ in grads], "swap_halves")
        return [(n, p, _pair_sum(g_, r_, ci, "pair_sum")) for (n, p, g_), r_ in zip(grads, theirs)]

    def finish_job(job, recv):
        for (n, p, pair), r_ in zip(job, recv):
            gsum[n] = _sum_slots(r_, pair, chip, ci, gsum[n], p, "sum_slots")

    def ffn_back(dh, sv, l, k, job):
        sfx = "1" if k == 0 else "2"
        W = lw[l]
        rider = _scatter_rider([pair for _, _, pair in job]) if job else None
        (dh, da, db, sw, dy, part), recv = _ffn_bwd(dh, sv["h0" if k == 0 else "h2"], mod[l], gnorm[l, 2 * k:2 * k + 1], sv["y" + sfx],
                                                    sv["a" + sfx], sv["b" + sfx], W["w1"], W["w3"], W["w2"], k, dm, "ffn_bwd", rider)
        finish_job(job, recv)
        g1 = _mm_tn(sv["hn" + sfx], da, full_a, shard_b, (S, D, F), per_slot(D, F), (S, nk), "gw1")
        g3 = _mm_tn(sv["hn" + sfx], db, full_a, shard_b, (S, D, F), per_slot(D, F), (S, nk), "gw3")
        g2 = _mm_tn(sw, dy, shard_b, full_a, (S, F, D), per_slot(F, D), (S, nk), "gw2")
        p = 2 * l + k
        return dh, _fold_parts(part, dm), [("ffn_w1", p, g1), ("ffn_w3", p, g3), ("ffn_w2", p, g2)]

    one = (1, nk)
    a1 = lambda kdim: pl.BlockSpec((tk, kdim), lambda s, kk: (kk, 0))
    pending = []
    for l in reversed(range(L)):
        kind, j = l % 2, l // 2
        sv = saved[l]
        W = lw[l]
        dh, p2, grads = ffn_back(dh, sv, l, 1, pending)
        job2 = make_job(grads)
        if kind == 0:
            dy, dz, pg = _out_bwd(dh, sv["ym"], W["w_out"], mod[l], 0, dm, "sc_out_bwd")
            g_out = _mm_tn(sv["z"], dy, a1(D), a1(D), (1, D, D), per_slot(D, D), one, "gw_sc_out")
            dp, dconv[j] = _conv_bwd(dz, sv["p"], convw[j], dm, "conv_bwd")
            g_in = _mm_tn(sv["hnm"], dp, pl.BlockSpec((tk, D), lambda q, kk: (kk, 0)),
                          pl.BlockSpec((None, tk, wq), lambda q, kk: (q // S, kk, q % S)), (3 * S, D, wq), per_slot(D, wq),
                          (3 * S, nk), "gw_sc_in")
            dh, pm = _sc_in_bwd(dh, dp, sv["h1"], mod[l], gnorm[l, 1:2], W["w_in"], 0, dm, "sc_in_bwd")
            grads = [("sc_w_in", j, jnp.moveaxis(g_in.reshape(S, 3, D, wq), 1, 2).reshape(S, D, 3 * wq)),
                     ("sc_w_out", j, g_out.reshape(S, D // S, D))]
        else:
            dy, do, pg = _out_bwd(dh, sv["ym"], W["w_o"], mod[l], 0, dm, "mla_out_bwd")
            g_o = _mm_tn(sv["o"], dy, a1(HEADS * V_HEAD), a1(D), (1, HEADS * V_HEAD, D), per_slot(HEADS * V_HEAD, D), one, "gw_mla_o")
            dq, dkl, dkc, dvl, dvc = _attn_bwd(sv["q"], sv["k"], sv["v"], sv["o"], do, dm, "attn_bwd")
            dh, pm, g_a, g_uq, g_ukv, dgains[j] = _mla_proj_bwd(
                dh, dq, dkl, dkc, dvl, dvc, sv["h1"], mod[l], gnorm[l, 1:2], gains[j:j + 1], tabs, W["w_a"], W["w_uq"], W["w_ukv"], 0, dm, "mla_proj_bwd")
            g_uq = g_uq.reshape(Q_LORA, HEADS, HEAD_PAD)[..., :QK_HEAD].reshape(Q_LORA, S, -1)
            grads = [("mla_w_a", j, g_a[:, :Q_LORA + KV_LORA + QK_ROPE].reshape(S, D // S, -1).astype(BF16)),
                     ("mla_w_uq", j, jnp.moveaxis(g_uq, 1, 0).astype(BF16)),
                     ("mla_w_ukv", j, jnp.moveaxis(g_ukv.reshape(KV_LORA, S, -1), 1, 0).astype(BF16)),
                     ("mla_w_o", j, g_o.reshape(S, HEADS * V_HEAD // S, D))]
        jobm = make_job(grads)
        pm = _fold_parts(pm, dm) + _fold_parts(pg, dm)
        dh, p0, grads = ffn_back(dh, sv, l, 0, job2 + jobm)
        pending = make_job(grads)
        dmod[l] = jnp.concatenate([p0[:, 0:3], pm[:, 0:3], p2[:, 0:3]], axis=1).reshape(B + 1, 9 * D)
        dgn[l] = jnp.stack([p0[:, 3].sum(0), pm[:, 3].sum(0), p2[:, 3].sum(0)])
    grad_x = dh[:B * N].reshape(B, N, D)
    finish_job(pending, _ride_alone(_scatter_rider([pair for _, _, pair in pending]), "scatter_grads"))
    gsum = dict(zip(big, _swap_cores_inplace([gsum[n] for n in big], "swap_cores")))

    dgains_a = jnp.stack(dgains)
    small = [jnp.stack(dmod).reshape(-1), jnp.stack(dgn).reshape(-1), jnp.stack(dconv).reshape(-1), dgains_a.reshape(-1)]
    sizes = [s_.shape[0] for s_ in small]
    flat = jnp.concatenate(small)
    pad = (-flat.shape[0]) % 1024
    flat = jnp.pad(flat, (0, pad)).reshape(-1, 128)
    allsmall = _exchange([flat], ("x", "y", "c"), False, "gather_small")[0].reshape(ndev, -1)
    offs = [0]
    for s_ in sizes:
        offs.append(offs[-1] + s_)
    dmod_all = allsmall[:, offs[0]:offs[1]].reshape(ndev, L, B + 1, 9 * D)
    tot = allsmall[:, offs[1]:offs[4]].sum(axis=0)
    g_gnorm = tot[:offs[2] - offs[1]].reshape(L, 3, D)
    g_conv = tot[offs[2] - offs[1]:offs[3] - offs[1]].reshape(La, 3, D)
    g_gains = tot[offs[3] - offs[1]:].reshape(Lb, 8, HEAD_PAD)
    dM = jnp.concatenate([jnp.moveaxis(dmod_all[:, :, :B], 0, 1).reshape(L, ndev * B, 9 * D),
                          dmod_all[:, :, B].sum(axis=0)[:, None, :], jnp.zeros((L, R - ndev * B - 1, 9 * D), F32)], axis=1)
    g_bmod = dM.sum(axis=1)
    dM_mine = lax.dynamic_slice_in_dim(dM, chip * C, C, axis=2)
    g_wmod, dsil = _mod_bwd(cond, dM_mine, w_mod, "mod_bwd")
    dsil_ctx = dsil[:, ndev * B].sum(axis=0)
    dsil_all = _exchange([jnp.pad(dsil_ctx.reshape(-1, 128), ((0, (-(D // 128)) % 8), (0, 0)))], ("x", "y"), False, "gather_dctx")[0]
    dsil_tot = dsil_all.sum(axis=0)[:D // 128].reshape(D)
    sg = jax.nn.sigmoid(c_ctx)
    g_cctx = dsil_tot * (sg * (1.0 + c_ctx * (1.0 - sg)))

    chip_cols = lambda a, width: lax.dynamic_slice_in_dim(a, chip * width, width, axis=a.ndim - 1)
    small_grads = dict(
        c_ctx=g_cctx, b_mod=g_bmod, g_norm=chip_cols(g_gnorm, D // S), sc_conv=chip_cols(g_conv, D // S),
        mla_g_qa=chip_cols(g_gains[:, 0, :Q_LORA], Q_LORA // S), mla_g_kva=g_gains[:, 1, :KV_LORA],
        mla_g_q=g_gains[:, 2, :QK_HEAD], mla_g_k=g_gains[:, 3, :QK_HEAD])

    grads, deltas, new_m, new_v = {}, {}, {}, {}
    for n, w in weights.items():
        shape = w.shape
        w2 = _flat2(w) if w.ndim > 1 else w.reshape(1, -1)
        m2, v2 = (a.reshape(w2.shape) for a in mom[n])
        if n in gsum:
            gs = [gsum[n].reshape(w2.shape)]
        elif n == "w_mod":
            gs = [_flat2(g_wmod)]
        else:
            gs = [small_grads[n].reshape(w2.shape)]
        g_, d_, m_, v_ = _adamw(w2, gs, m2, v2, "adamw")
        grads[n], deltas[n], new_m[n], new_v[n] = (a.reshape(shape) for a in (g_, d_, m_, v_))

    names = list(weights)
    return (loss, grad_x, *[grads[n] for n in names], *[deltas[n] for n in names], *[new_m[n] for n in names],
            *[new_v[n] for n in names])
```

```python
import functools
import math
from typing import NamedTuple

import jax
import jax.numpy as jnp
from jax import lax
from jax.experimental import pallas as pl
from jax.experimental.pallas import tpu as pltpu

F32 = jnp.float32
BF16 = jnp.bfloat16
EPS = 1e-6
GRID_W = 64
HEADS = 8
QK_NOPE = 128
QK_ROPE = 64
QK_HEAD = QK_NOPE + QK_ROPE
HEAD_PAD = 256
V_HEAD = 128
Q_LORA = 256
KV_LORA = 128
ROPE_BASE = 10000.0
QK_SCALE = QK_HEAD ** -0.5
ADAM_LR, ADAM_B1, ADAM_B2, ADAM_EPS, ADAM_WD, ADAM_STEP = 0.001, 0.9, 0.999, 1e-08, 0.01, 10
N_CHIPS = 4
VMEM_LIMIT = 56 * 1024 * 1024
MESH = pl.DeviceIdType.MESH
NEG = -1e30


class Dims(NamedTuple):
    B: int
    N: int
    CTX: int
    D: int
    T: int
    tm: int


def _cparams(*sem):
    return pltpu.CompilerParams(dimension_semantics=sem if sem else None, vmem_limit_bytes=VMEM_LIMIT)


def _dot(a, b):
    return jnp.dot(a, b, preferred_element_type=F32)


def _dot_nt(a, b):
    return lax.dot_general(a, b, (((1,), (1,)), ((), ())), preferred_element_type=F32)


def _dot_tn(a, b):
    return lax.dot_general(a, b, (((0,), (0,)), ((), ())), preferred_element_type=F32)


def _rms(x, n):
    r = lax.rsqrt(jnp.sum(x * x, axis=-1, keepdims=True) * (1.0 / n) + EPS)
    return x * r, r


def _rms_bwd(dxh, xh, r, n):
    return r * (dxh - xh * (jnp.sum(dxh * xh, axis=-1, keepdims=True) * (1.0 / n)))


def _pre(h, g, shift, scale):
    xh, _ = _rms(h, h.shape[-1])
    return (xh * g) * (1.0 + scale) + shift


def _pre_bwd(dout, h, g, scale):
    d = h.shape[-1]
    xh, r = _rms(h, d)
    n = xh * g
    dshift = jnp.sum(dout, axis=0, keepdims=True)
    dscale = jnp.sum(dout * n, axis=0, keepdims=True)
    dn = dout * (1.0 + scale)
    dg = jnp.sum(dn * xh, axis=0, keepdims=True)
    dh = _rms_bwd(dn * g, xh, r, d)
    return dh, dshift, dscale, dg


def _write_part(part_ref, dshift=None, dscale=None, dgate=None, dg=None):
    z = jnp.zeros((1, part_ref.shape[-1]), F32)
    part_ref[0, 0:1, :] = z if dshift is None else dshift
    part_ref[0, 1:2, :] = z if dscale is None else dscale
    part_ref[0, 2:3, :] = z if dgate is None else dgate
    part_ref[0, 3:4, :] = z if dg is None else dg
    part_ref[0, 4:8, :] = jnp.zeros((4, part_ref.shape[-1]), F32)


def _grp(dm):
    nb = dm.N // dm.tm
    return lambda i: jnp.minimum(i // nb, dm.B)


def _n_chunks(rows, row_bytes):
    n = 16
    while n > 1 and (rows % (16 * n) or (rows // n) * row_bytes < (256 << 10)):
        n //= 2
    return n


def _start_local(src, dst, sems, k0, nchunk):
    ch = src.shape[0] // nchunk
    copies = []
    for j in range(nchunk):
        cp = pltpu.make_async_copy(src.at[pl.ds(j * ch, ch)], dst.at[pl.ds(j * ch, ch)], sems.at[k0 + j])
        cp.start()
        copies.append(cp)
    return copies


def _exchange(arrs, axes, scatter, name, own="copy"):
    n = len(arrs)
    nbits = len(axes)
    slots = 2 ** nbits
    pats = list(range(1, slots))
    inplace = own == "inplace"
    nck = [_n_chunks(a.shape[-2], a.shape[-1] * a.dtype.itemsize) for a in arrs]
    base = [sum(nck[:i]) * len(pats) for i in range(n)]
    nsem = sum(nck) * len(pats)

    def body(*refs):
        ins, outs = refs[:n], refs[n:2 * n]
        send, recv, loc = refs[2 * n:]
        pos = {a: lax.axis_index(a) for a in ("x", "y", "c")}

        def slot_of(p):
            s = 0
            for a in axes:
                s = 2 * s + p[a]
            return s

        me = slot_of(pos)
        local = []
        for i in range(n):
            if own == "copy":
                local += _start_local(ins[i].at[me] if scatter else ins[i], outs[i].at[me], loc, sum(nck[:i]), nck[i])
        remote = []
        for pi, pat in enumerate(pats):
            peer = dict(pos)
            for bi, a in enumerate(axes):
                if (pat >> (nbits - 1 - bi)) & 1:
                    peer[a] = 1 - pos[a]
            them = slot_of(peer)
            for i in range(n):
                ch = arrs[i].shape[-2] // nck[i]
                for j in range(nck[i]):
                    k = base[i] + pi * nck[i] + j
                    rs = pl.ds(j * ch, ch)
                    if inplace:
                        src = outs[i].at[me, rs]
                    else:
                        src = ins[i].at[them, rs] if scatter else ins[i].at[rs]
                    cp = pltpu.make_async_remote_copy(
                        src_ref=src, dst_ref=outs[i].at[me, rs], send_sem=send.at[k], recv_sem=recv.at[k],
                        device_id=(peer["x"], peer["y"], peer["c"]), device_id_type=MESH)
                    cp.start()
                    remote.append(cp)
        for cp in local:
            cp.wait()
        for cp in remote:
            cp.wait()

    out_shape = [jax.ShapeDtypeStruct(a.shape if (scatter or inplace) else (slots,) + a.shape, a.dtype) for a in arrs]
    any_spec = pl.BlockSpec(memory_space=pl.ANY)
    outs = pl.pallas_call(
        body, name=name, out_shape=out_shape, in_specs=[any_spec] * n, out_specs=[any_spec] * n,
        scratch_shapes=[pltpu.SemaphoreType.DMA((nsem,)), pltpu.SemaphoreType.DMA((nsem,)), pltpu.SemaphoreType.DMA((sum(nck),))],
        input_output_aliases={i: i for i in range(n)} if inplace else {},
        compiler_params=pltpu.CompilerParams(has_side_effects=True),
    )(*arrs)
    return list(outs)


class Rider(NamedTuple):
    ins: list
    out_shapes: list
    aliases: dict
    sems: list
    start: object
    mid: object
    end: object


MID_STEPS = 6


def _hosted(body, rider, *, name, grid, in_specs, out_specs, out_shape, scratch_shapes, sem, args):
    if rider is None:
        outs = pl.pallas_call(body, name=name, grid=grid, in_specs=in_specs, out_specs=out_specs, out_shape=out_shape,
                              scratch_shapes=scratch_shapes, compiler_params=_cparams(*sem))(*args)
        return outs, []
    n_in, n_out, n_s = len(in_specs), len(out_specs), len(scratch_shapes)
    nri, nro = len(rider.ins), len(rider.out_shapes)
    nsteps = math.prod(grid)

    def wrapped(*refs):
        bounds = [0, n_in, n_in + nri, n_in + nri + n_out, n_in + nri + n_out + nro, n_in + nri + n_out + nro + n_s, len(refs)]
        ins, rins, outs, routs, scr, sems = (refs[lo:hi] for lo, hi in zip(bounds[:-1], bounds[1:]))
        step = 0
        for ax, extent in enumerate(grid):
            step = step * extent + pl.program_id(ax)

        @pl.when(step == 0)
        def _():
            rider.start(rins, routs, sems)

        body(*ins, *outs, *scr)

        if rider.mid is not None:
            @pl.when(step == max(nsteps - 1 - MID_STEPS, 0))
            def _():
                rider.mid(rins, routs, sems)

        @pl.when(step == nsteps - 1)
        def _():
            rider.end(rins, routs, sems)

    any_spec = pl.BlockSpec(memory_space=pl.ANY)
    outs = pl.pallas_call(
        wrapped, name=name, grid=grid, in_specs=list(in_specs) + [any_spec] * nri, out_specs=list(out_specs) + [any_spec] * nro,
        out_shape=list(out_shape) + list(rider.out_shapes), scratch_shapes=list(scratch_shapes) + list(rider.sems),
        input_output_aliases={n_in + i: n_out + o for i, o in rider.aliases.items()},
        compiler_params=pltpu.CompilerParams(dimension_semantics=("arbitrary",) * len(grid), vmem_limit_bytes=VMEM_LIMIT,
                                             has_side_effects=True),
    )(*args, *rider.ins)
    return outs[:n_out], list(outs[n_out:])


def _gather_rider(bufs):
    n = len(bufs)
    halves = [a.shape[1] // 2 for a in bufs]
    nck = [_n_chunks(h, a.shape[2] * a.dtype.itemsize) for h, a in zip(halves, bufs)]
    base = [3 * sum(nck[:i]) for i in range(n)]
    nsem = 3 * sum(nck)

    def plan():
        x, y, c = lax.axis_index("x"), lax.axis_index("y"), lax.axis_index("c")
        pieces = []
        for pi, (px, py) in enumerate([(x, 1 - y), (1 - x, y), (1 - x, 1 - y)]):
            for i in range(n):
                ch = halves[i] // nck[i]
                for j in range(nck[i]):
                    pieces.append((base[i] + pi * nck[i] + j, px, py, 2 * px + py, i, j * ch, ch))
        return x, y, c, 2 * x + y, pieces

    def rows(i, off, ch, core):
        return pl.ds(pl.multiple_of(core * halves[i] + off, 16), ch)

    def over_ici(outs, sems, c, slot, k, px, py, i, off, ch):
        ref = outs[i].at[slot, rows(i, off, ch, c)]
        return pltpu.make_async_remote_copy(src_ref=ref, dst_ref=ref, send_sem=sems[0].at[k], recv_sem=sems[1].at[k],
                                            device_id=(px, py, c), device_id_type=MESH)

    def over_d2d(outs, sems, x, y, c, slot, k, i, off, ch, core):
        ref = outs[i].at[slot, rows(i, off, ch, core)]
        return pltpu.make_async_remote_copy(src_ref=ref, dst_ref=ref, send_sem=sems[2].at[k], recv_sem=sems[3].at[k],
                                            device_id=(x, y, 1 - c), device_id_type=MESH)

    def start(ins, outs, sems):
        x, y, c, me, pieces = plan()
        for k, px, py, them, i, off, ch in pieces:
            over_ici(outs, sems, c, me, k, px, py, i, off, ch).start()

    def mid(ins, outs, sems):
        x, y, c, me, pieces = plan()
        for k, px, py, them, i, off, ch in pieces:
            over_ici(outs, sems, c, them, k, px, py, i, off, ch).wait_recv()
            over_d2d(outs, sems, x, y, c, them, k, i, off, ch, c).start()

    def end(ins, outs, sems):
        x, y, c, me, pieces = plan()
        for k, px, py, them, i, off, ch in pieces:
            over_ici(outs, sems, c, me, k, px, py, i, off, ch).wait_send()
            over_d2d(outs, sems, x, y, c, them, k, i, off, ch, c).wait_send()
        for k, px, py, them, i, off, ch in pieces:
            over_d2d(outs, sems, x, y, c, them, k, i, off, ch, 1 - c).wait_recv()

    return Rider(ins=list(bufs), out_shapes=[jax.ShapeDtypeStruct(a.shape, a.dtype) for a in bufs],
                 aliases={i: i for i in range(n)}, sems=[pltpu.SemaphoreType.DMA((nsem,))] * 4, start=start, mid=mid, end=end)


def _scatter_rider(srcs):
    n = len(srcs)
    nck = [_n_chunks(a.shape[1], a.shape[2] * a.dtype.itemsize) for a in srcs]
    base = [3 * sum(nck[:i]) for i in range(n)]
    nsem = 3 * sum(nck)

    def copies(ins, outs, sems):
        x, y, c = lax.axis_index("x"), lax.axis_index("y"), lax.axis_index("c")
        me = 2 * x + y
        for pi, (px, py) in enumerate([(x, 1 - y), (1 - x, y), (1 - x, 1 - y)]):
            for i in range(n):
                ch = srcs[i].shape[1] // nck[i]
                for j in range(nck[i]):
                    k = base[i] + pi * nck[i] + j
                    rs = pl.ds(j * ch, ch)
                    yield pltpu.make_async_remote_copy(
                        src_ref=ins[i].at[2 * px + py, rs], dst_ref=outs[i].at[me, rs], send_sem=sems[0].at[k],
                        recv_sem=sems[1].at[k], device_id=(px, py, c), device_id_type=MESH)

    def start(ins, outs, sems):
        for cp in copies(ins, outs, sems):
            cp.start()

    def end(ins, outs, sems):
        for cp in copies(ins, outs, sems):
            cp.wait()

    return Rider(ins=list(srcs), out_shapes=[jax.ShapeDtypeStruct(a.shape, a.dtype) for a in srcs], aliases={},
                 sems=[pltpu.SemaphoreType.DMA((nsem,))] * 2, start=start, mid=None, end=end)


def _ride_alone(rider, name):
    n_in, n_out = len(rider.ins), len(rider.out_shapes)

    def body(*refs):
        ins, outs, sems = refs[:n_in], refs[n_in:n_in + n_out], refs[n_in + n_out:]
        rider.start(ins, outs, sems)
        if rider.mid is not None:
            rider.mid(ins, outs, sems)
        rider.end(ins, outs, sems)

    any_spec = pl.BlockSpec(memory_space=pl.ANY)
    outs = pl.pallas_call(
        body, name=name, out_shape=list(rider.out_shapes), in_specs=[any_spec] * n_in, out_specs=[any_spec] * n_out,
        scratch_shapes=list(rider.sems), input_output_aliases=dict(rider.aliases),
        compiler_params=pltpu.CompilerParams(has_side_effects=True),
    )(*rider.ins)
    return list(outs)


def _swap_cores_inplace(bufs, name):
    n = len(bufs)
    nck = [_n_chunks(a.shape[2], a.shape[3] * a.dtype.itemsize) for a in bufs]
    base = [sum(a.shape[0] * k for a, k in zip(bufs[:i], nck[:i])) for i in range(n)]
    nsem = sum(a.shape[0] * k for a, k in zip(bufs, nck))

    def body(*refs):
        outs = refs[n:2 * n]
        send, recv = refs[2 * n:]
        x, y, c = lax.axis_index("x"), lax.axis_index("y"), lax.axis_index("c")

        def copies(core):
            for i in range(n):
                ch = bufs[i].shape[2] // nck[i]
                for p in range(bufs[i].shape[0]):
                    for j in range(nck[i]):
                        k = base[i] + p * nck[i] + j
                        ref = outs[i].at[p, core, pl.ds(j * ch, ch)]
                        yield pltpu.make_async_remote_copy(src_ref=ref, dst_ref=ref, send_sem=send.at[k], recv_sem=recv.at[k],
                                                           device_id=(x, y, 1 - c), device_id_type=MESH)

        for cp in copies(c):
            cp.start()
        for cp in copies(c):
            cp.wait_send()
        for cp in copies(1 - c):
            cp.wait_recv()

    any_spec = pl.BlockSpec(memory_space=pl.ANY)
    outs = pl.pallas_call(
        body, name=name, out_shape=[jax.ShapeDtypeStruct(a.shape, a.dtype) for a in bufs], in_specs=[any_spec] * n,
        out_specs=[any_spec] * n, scratch_shapes=[pltpu.SemaphoreType.DMA((nsem,))] * 2,
        input_output_aliases={i: i for i in range(n)}, compiler_params=pltpu.CompilerParams(has_side_effects=True),
    )(*bufs)
    return list(outs)


def _place_cast(w, row0, rows, slot, slots, name):
    C = w.shape[1]
    tr = _row_block(rows, C)
    blk0 = row0 // tr

    def body(slot_ref, w_ref, o_ref):
        del slot_ref
        o_ref[...] = w_ref[...].astype(BF16)

    return pl.pallas_call(
        body, name=name,
        grid_spec=pltpu.PrefetchScalarGridSpec(
            num_scalar_prefetch=1, grid=(rows // tr,), in_specs=[pl.BlockSpec((tr, C), lambda i, sr: (blk0 + i, 0))],
            out_specs=pl.BlockSpec((None, tr, C), lambda i, sr: (sr[0], i, 0))),
        out_shape=jax.ShapeDtypeStruct((slots, rows, C), BF16),
        compiler_params=_cparams("parallel"),
    )(slot.reshape(1).astype(jnp.int32), w)


def _swap_halves(arrs, name):
    n = len(arrs)
    S = arrs[0].shape[0]
    halves = [a.shape[1] // 2 for a in arrs]
    nck = [_n_chunks(h, a.shape[2] * a.dtype.itemsize) for h, a in zip(halves, arrs)]
    base = [S * sum(nck[:i]) for i in range(n)]
    nsem = S * sum(nck)

    def body(*refs):
        ins, outs = refs[:n], refs[n:2 * n]
        send, recv = refs[2 * n:]
        x, y, c = lax.axis_index("x"), lax.axis_index("y"), lax.axis_index("c")
        copies = []
        for i in range(n):
            ch = halves[i] // nck[i]
            for s in range(S):
                for j in range(nck[i]):
                    k = base[i] + s * nck[i] + j
                    src = ins[i].at[s, pl.ds(pl.multiple_of((1 - c) * halves[i] + j * ch, 16), ch)]
                    cp = pltpu.make_async_remote_copy(src_ref=src, dst_ref=outs[i].at[s, pl.ds(j * ch, ch)], send_sem=send.at[k],
                                                      recv_sem=recv.at[k], device_id=(x, y, 1 - c), device_id_type=MESH)
                    cp.start()
                    copies.append(cp)
        for cp in copies:
            cp.wait()

    any_spec = pl.BlockSpec(memory_space=pl.ANY)
    outs = pl.pallas_call(
        body, name=name, out_shape=[jax.ShapeDtypeStruct((S, h, a.shape[2]), a.dtype) for h, a in zip(halves, arrs)],
        in_specs=[any_spec] * n, out_specs=[any_spec] * n,
        scratch_shapes=[pltpu.SemaphoreType.DMA((nsem,))] * 2,
        compiler_params=pltpu.CompilerParams(has_side_effects=True),
    )(*arrs)
    return list(outs)


def _pair_sum(g, r, core, name):
    S, rows, C = g.shape
    half = rows // 2
    tr = _row_block(half, C)
    nb = half // tr

    def body(core_ref, g_ref, r_ref, o_ref):
        del core_ref
        o_ref[...] = (g_ref[...].astype(F32) + r_ref[...].astype(F32)).astype(BF16)

    blk = pl.BlockSpec((None, tr, C), lambda s, i, cr: (s, i, 0))
    return pl.pallas_call(
        body, name=name,
        grid_spec=pltpu.PrefetchScalarGridSpec(
            num_scalar_prefetch=1, grid=(S, nb),
            in_specs=[pl.BlockSpec((None, tr, C), lambda s, i, cr: (s, cr[0] * nb + i, 0)), blk], out_specs=blk),
        out_shape=jax.ShapeDtypeStruct((S, half, C), BF16),
        compiler_params=_cparams("parallel", "parallel"),
    )(core.reshape(1).astype(jnp.int32), g, r)


def _ffn_fwd(h, mod, g, w1, w3, w2, k, dm, name, rider=None):
    T, D = h.shape
    S, F = w1.shape[0], w1.shape[-2]
    tm = dm.tm
    r0 = 6 if k else 0
    grp = _grp(dm)

    def body(h_ref, mod_ref, g_ref, w1_ref, w3_ref, w2_ref, ho_ref, a_ref, b_ref, hn_ref, y_ref, hn_s, acc):
        s = pl.program_id(1)

        @pl.when(s == 0)
        def _():
            hn = _pre(h_ref[...], g_ref[...], mod_ref[0, r0:r0 + 1, :], mod_ref[0, r0 + 1:r0 + 2, :]).astype(BF16)
            hn_s[...] = hn
            hn_ref[...] = hn
            acc[...] = jnp.zeros_like(acc)

        hn = hn_s[...]
        a = _dot_nt(hn, w1_ref[...])
        b = _dot_nt(hn, w3_ref[...])
        a_ref[0] = a.astype(BF16)
        b_ref[0] = b.astype(BF16)
        sw = (a * jax.nn.sigmoid(a) * b).astype(BF16)
        acc[...] += _dot(sw, w2_ref[...])

        @pl.when(s == S - 1)
        def _():
            y = acc[...]
            y_ref[...] = y.astype(BF16)
            ho_ref[...] = h_ref[...] + 0.5 * mod_ref[0, r0 + 2:r0 + 3, :] * y

    row = pl.BlockSpec((tm, D), lambda i, s: (i, 0))
    wrow = pl.BlockSpec((None, None, F, D), lambda i, s: (s, k, 0, 0))
    ab = pl.BlockSpec((1, tm, F), lambda i, s: (s, i, 0))
    return _hosted(
        body, rider, name=name, grid=(T // tm, S),
        in_specs=[row, pl.BlockSpec((1, 9, D), lambda i, s: (grp(i), 0, 0)), pl.BlockSpec((1, D), lambda i, s: (0, 0)),
                  wrow, wrow, wrow],
        out_specs=[row, ab, ab, row, row],
        out_shape=[jax.ShapeDtypeStruct((T, D), F32), jax.ShapeDtypeStruct((S, T, F), BF16),
                   jax.ShapeDtypeStruct((S, T, F), BF16), jax.ShapeDtypeStruct((T, D), BF16),
                   jax.ShapeDtypeStruct((T, D), BF16)],
        scratch_shapes=[pltpu.VMEM((tm, D), BF16), pltpu.VMEM((tm, D), F32)],
        sem=("parallel", "arbitrary"), args=(h, mod, g, w1, w3, w2))


def _ffn_bwd(dh, h, mod, g, y, a, b, w1, w3, w2, k, dm, name, rider=None):
    T, D = h.shape
    S, F = w1.shape[0], w1.shape[-2]
    tm = dm.tm
    r0 = 6 if k else 0
    grp = _grp(dm)

    def body(dh_ref, h_ref, mod_ref, g_ref, y_ref, a_ref, b_ref, w1_ref, w3_ref, w2_ref,
             dho_ref, da_ref, db_ref, sw_ref, dy_ref, part_ref, dy_s, acc):
        s = pl.program_id(1)

        @pl.when(s == 0)
        def _():
            dy = (0.5 * mod_ref[0, r0 + 2:r0 + 3, :] * dh_ref[...]).astype(BF16)
            dy_s[...] = dy
            dy_ref[...] = dy
            acc[...] = jnp.zeros_like(acc)

        ds = _dot_nt(dy_s[...], w2_ref[...])
        av = a_ref[0].astype(F32)
        bv = b_ref[0].astype(F32)
        sig = jax.nn.sigmoid(av)
        sil = av * sig
        sw_ref[0] = (sil * bv).astype(BF16)
        db = (ds * sil).astype(BF16)
        da = (ds * bv * (sig * (1.0 + av * (1.0 - sig)))).astype(BF16)
        da_ref[0] = da
        db_ref[0] = db
        acc[...] += _dot(da, w1_ref[...]) + _dot(db, w3_ref[...])

        @pl.when(s == S - 1)
        def _():
            dhv = dh_ref[...]
            dhb, dshift, dscale, dg = _pre_bwd(acc[...], h_ref[...], g_ref[...], mod_ref[0, r0 + 1:r0 + 2, :])
            dho_ref[...] = dhv + dhb
            dgate = 0.5 * jnp.sum(dhv * y_ref[...].astype(F32), axis=0, keepdims=True)
            _write_part(part_ref, dshift, dscale, dgate, dg)

    row = pl.BlockSpec((tm, D), lambda i, s: (i, 0))
    wrow = pl.BlockSpec((None, None, F, D), lambda i, s: (s, k, 0, 0))
    ab = pl.BlockSpec((1, tm, F), lambda i, s: (s, i, 0))
    stf = jax.ShapeDtypeStruct((S, T, F), BF16)
    return _hosted(
        body, rider, name=name, grid=(T // tm, S),
        in_specs=[row, row, pl.BlockSpec((1, 9, D), lambda i, s: (grp(i), 0, 0)), pl.BlockSpec((1, D), lambda i, s: (0, 0)),
                  row, ab, ab, wrow, wrow, wrow],
        out_specs=[row, ab, ab, ab, row, pl.BlockSpec((1, 8, D), lambda i, s: (i, 0, 0))],
        out_shape=[jax.ShapeDtypeStruct((T, D), F32), stf, stf, stf, jax.ShapeDtypeStruct((T, D), BF16),
                   jax.ShapeDtypeStruct((T // tm, 8, D), F32)],
        scratch_shapes=[pltpu.VMEM((tm, D), BF16), pltpu.VMEM((tm, D), F32)],
        sem=("parallel", "arbitrary"), args=(dh, h, mod, g, y, a, b, w1, w3, w2))


def _mm_tn(a, b, a_spec, b_spec, out_shape, out_spec, grid, name):
    nk = grid[-1]
    kax = len(grid) - 1
    blk = tuple(d for d in out_spec.block_shape if d is not None)

    def body(a_ref, b_ref, o_ref, acc):
        kk = pl.program_id(kax)

        @pl.when(kk == 0)
        def _():
            acc[...] = jnp.zeros_like(acc)

        acc[...] += _dot_tn(a_ref[...].astype(BF16), b_ref[...].astype(BF16))

        @pl.when(kk == nk - 1)
        def _():
            o_ref[...] = acc[...].astype(o_ref.dtype)

    return pl.pallas_call(
        body, name=name, grid=grid,
        in_specs=[a_spec, b_spec], out_specs=out_spec, out_shape=jax.ShapeDtypeStruct(out_shape, BF16),
        scratch_shapes=[pltpu.VMEM(blk, F32)],
        compiler_params=_cparams(*(["parallel"] * kax + ["arbitrary"])),
    )(a, b)


def _sc_in_fwd(h, mod, g, w_in, j, dm, name, rider=None):
    T, D = h.shape
    tm = dm.tm
    wq = D // N_CHIPS
    nq = 3 * N_CHIPS
    grp = _grp(dm)

    def body(h_ref, mod_ref, g_ref, w_ref, p_ref, hn_ref, hn_s):
        @pl.when(pl.program_id(1) == 0)
        def _():
            hn = _pre(h_ref[...], g_ref[...], mod_ref[0, 3:4, :], mod_ref[0, 4:5, :]).astype(BF16)
            hn_s[...] = hn
            hn_ref[...] = hn

        p_ref[...] = _dot(hn_s[...], w_ref[...])

    row = pl.BlockSpec((tm, D), lambda i, q: (i, 0))
    return _hosted(
        body, rider, name=name, grid=(T // tm, nq),
        in_specs=[row, pl.BlockSpec((1, 9, D), lambda i, q: (grp(i), 0, 0)), pl.BlockSpec((1, D), lambda i, q: (0, 0)),
                  pl.BlockSpec((None, None, D, wq), lambda i, q: (q // 3, j, 0, q % 3))],
        out_specs=[pl.BlockSpec((None, tm, wq), lambda i, q: (q // N_CHIPS, i, q % N_CHIPS)), row],
        out_shape=[jax.ShapeDtypeStruct((3, T, D), F32), jax.ShapeDtypeStruct((T, D), BF16)],
        scratch_shapes=[pltpu.VMEM((tm, D), BF16)],
        sem=("parallel", "arbitrary"), args=(h, mod, g, w_in))


def _conv_cols(dm):
    return 256 if dm.D % 256 == 0 else 128


def _seg_masks(r, dm):
    bn = dm.B * dm.N
    lat = r < bn
    off = jnp.where(lat, lax.rem(r, dm.N), lax.rem(r - bn, dm.CTX))
    seg = jnp.where(lat, dm.N, dm.CTX)
    inside = (r >= 0) & (r < dm.T)
    return ((off != 0) & inside).astype(F32), ((off != seg - 1) & inside).astype(F32)


def _conv_specs(dm):
    tb, cb, nr8 = dm.tm, _conv_cols(dm), dm.T // 8
    prev8 = lambda c, i: jnp.maximum(i * (tb // 8) - 1, 0)
    next8 = lambda c, i: jnp.minimum((i + 1) * (tb // 8), nr8 - 1)
    return dict(
        tb=tb, cb=cb,
        p=pl.BlockSpec((3, tb, cb), lambda c, i: (0, i, c)),
        p_prev=pl.BlockSpec((3, 8, cb), lambda c, i: (0, prev8(c, i), c)),
        p_next=pl.BlockSpec((3, 8, cb), lambda c, i: (0, next8(c, i), c)),
        row=pl.BlockSpec((tb, cb), lambda c, i: (i, c)),
        row_prev=pl.BlockSpec((8, cb), lambda c, i: (prev8(c, i), c)),
        row_next=pl.BlockSpec((8, cb), lambda c, i: (next8(c, i), c)),
        w=pl.BlockSpec((3, cb), lambda c, i: (0, c)),
    )


def _shift_rows(x, before, after, tb):
    rid = lax.broadcasted_iota(jnp.int32, x.shape, 0)
    down = jnp.where(rid == 0, before, pltpu.roll(x, 1, 0))
    up = jnp.where(rid == tb - 1, after, pltpu.roll(x, tb - 1, 0))
    return down, up


def _conv_fwd(p, wc, dm, name):
    T, D = dm.T, dm.D
    sp = _conv_specs(dm)
    tb, cb = sp["tb"], sp["cb"]

    def body(p_ref, pp_ref, pn_ref, w_ref, z_ref):
        r = pl.program_id(1) * tb + lax.broadcasted_iota(jnp.int32, (tb, cb), 0)
        mp, mn = _seg_masks(r, dm)
        cu = p_ref[1] * p_ref[2]
        prev, nxt = _shift_rows(cu, pp_ref[1, 7:8, :] * pp_ref[2, 7:8, :], pn_ref[1, 0:1, :] * pn_ref[2, 0:1, :], tb)
        conv = w_ref[0:1, :] * (prev * mp) + w_ref[1:2, :] * cu + w_ref[2:3, :] * (nxt * mn)
        z_ref[...] = (p_ref[0] * conv).astype(BF16)

    return pl.pallas_call(
        body, name=name, grid=(D // cb, T // tb),
        in_specs=[sp["p"], sp["p_prev"], sp["p_next"], sp["w"]], out_specs=sp["row"],
        out_shape=jax.ShapeDtypeStruct((T, D), BF16),
        compiler_params=_cparams("parallel", "parallel"),
    )(p, p, p, wc)


def _conv_bwd(dz, p, wc, dm, name):
    T, D = dm.T, dm.D
    sp = _conv_specs(dm)
    tb, cb = sp["tb"], sp["cb"]

    def body(dz_ref, dzp_ref, dzn_ref, p_ref, pp_ref, pn_ref, w_ref, dp_ref, dw_ref):
        i = pl.program_id(1)
        r = i * tb + lax.broadcasted_iota(jnp.int32, (tb, cb), 0)
        mp, mn = _seg_masks(r, dm)
        rb = i * tb + lax.broadcasted_iota(jnp.int32, (1, cb), 0)
        _, mn_before = _seg_masks(rb - 1, dm)
        mp_after, _ = _seg_masks(rb + tb, dm)
        bg, cg, u = p_ref[0], p_ref[1], p_ref[2]
        cu = cg * u
        prev, nxt = _shift_rows(cu, pp_ref[1, 7:8, :] * pp_ref[2, 7:8, :], pn_ref[1, 0:1, :] * pn_ref[2, 0:1, :], tb)
        prev = prev * mp
        nxt = nxt * mn
        w0, w1, w2 = w_ref[0:1, :], w_ref[1:2, :], w_ref[2:3, :]
        conv = w0 * prev + w1 * cu + w2 * nxt
        dz = dz_ref[...]
        dp_ref[0] = dz * conv
        dconv = dz * bg

        @pl.when(i == 0)
        def _():
            dw_ref[...] = jnp.zeros_like(dw_ref)

        dw_ref[0:1, :] += jnp.sum(dconv * prev, axis=0, keepdims=True)
        dw_ref[1:2, :] += jnp.sum(dconv * cu, axis=0, keepdims=True)
        dw_ref[2:3, :] += jnp.sum(dconv * nxt, axis=0, keepdims=True)
        dconv_before = dzp_ref[7:8, :] * pp_ref[0, 7:8, :] * mn_before
        dconv_after = dzn_ref[0:1, :] * pn_ref[0, 0:1, :] * mp_after
        from_prev, _ = _shift_rows(dconv * mn, dconv_before, dconv_after, tb)
        _, from_next = _shift_rows(dconv * mp, dconv_before, dconv_after, tb)
        dcu = w1 * dconv + w0 * from_next + w2 * from_prev
        dp_ref[1] = dcu * u
        dp_ref[2] = dcu * cg

    return pl.pallas_call(
        body, name=name, grid=(D // cb, T // tb),
        in_specs=[sp["row"], sp["row_prev"], sp["row_next"], sp["p"], sp["p_prev"], sp["p_next"], sp["w"]],
        out_specs=[sp["p"], sp["w"]],
        out_shape=[jax.ShapeDtypeStruct((3, T, D), F32), jax.ShapeDtypeStruct((3, D), F32)],
        compiler_params=_cparams("parallel", "arbitrary"),
    )(dz, dz, dz, p, p, p, wc)


def _out_fwd(z, w, h, mod, j, dm, name):
    T, D = h.shape
    K = z.shape[1]
    tm = dm.tm
    grp = _grp(dm)

    def body(z_ref, w_ref, h_ref, mod_ref, ho_ref, y_ref):
        y = _dot(z_ref[...], w_ref[...])
        y_ref[...] = y.astype(BF16)
        ho_ref[...] = h_ref[...] + mod_ref[0, 5:6, :] * y

    row = pl.BlockSpec((tm, D), lambda i: (i, 0))
    return pl.pallas_call(
        body, name=name, grid=(T // tm,),
        in_specs=[pl.BlockSpec((tm, K), lambda i: (i, 0)), pl.BlockSpec((None, K, D), lambda i: (j, 0, 0)), row,
                  pl.BlockSpec((1, 9, D), lambda i: (grp(i), 0, 0))],
        out_specs=[row, row],
        out_shape=[jax.ShapeDtypeStruct((T, D), F32), jax.ShapeDtypeStruct((T, D), BF16)],
        compiler_params=_cparams("parallel"),
    )(z, w, h, mod)


def _out_bwd(dh, y, w, mod, j, dm, name):
    T, D = dh.shape
    K = w.shape[1]
    tm = dm.tm
    grp = _grp(dm)

    def body(dh_ref, y_ref, w_ref, mod_ref, dy_ref, dz_ref, part_ref):
        dhv = dh_ref[...]
        dy = (mod_ref[0, 5:6, :] * dhv).astype(BF16)
        dy_ref[...] = dy
        dz_ref[...] = _dot_nt(dy, w_ref[...])
        _write_part(part_ref, dgate=jnp.sum(dhv * y_ref[...].astype(F32), axis=0, keepdims=True))

    row = pl.BlockSpec((tm, D), lambda i: (i, 0))
    return pl.pallas_call(
        body, name=name, grid=(T // tm,),
        in_specs=[row, row, pl.BlockSpec((None, K, D), lambda i: (j, 0, 0)), pl.BlockSpec((1, 9, D), lambda i: (grp(i), 0, 0))],
        out_specs=[row, pl.BlockSpec((tm, K), lambda i: (i, 0)), pl.BlockSpec((1, 8, D), lambda i: (i, 0, 0))],
        out_shape=[jax.ShapeDtypeStruct((T, D), BF16), jax.ShapeDtypeStruct((T, K), F32),
                   jax.ShapeDtypeStruct((T // tm, 8, D), F32)],
        compiler_params=_cparams("parallel"),
    )(dh, y, w, mod)


def _sc_in_bwd(dh, dp, h, mod, g, w_in, j, dm, name):
    T, D = h.shape
    tm = dm.tm
    wq = D // N_CHIPS
    nq = 3 * N_CHIPS
    grp = _grp(dm)

    def body(dh_ref, dp_ref, h_ref, mod_ref, g_ref, w_ref, dho_ref, part_ref, acc):
        q = pl.program_id(1)

        @pl.when(q == 0)
        def _():
            acc[...] = jnp.zeros_like(acc)

        acc[...] += _dot_nt(dp_ref[...].astype(BF16), w_ref[...])

        @pl.when(q == nq - 1)
        def _():
            dhb, dshift, dscale, dg = _pre_bwd(acc[...], h_ref[...], g_ref[...], mod_ref[0, 4:5, :])
            dho_ref[...] = dh_ref[...] + dhb
            _write_part(part_ref, dshift, dscale, None, dg)

    row = pl.BlockSpec((tm, D), lambda i, q: (i, 0))
    return pl.pallas_call(
        body, name=name, grid=(T // tm, nq),
        in_specs=[row, pl.BlockSpec((None, tm, wq), lambda i, q: (q // N_CHIPS, i, q % N_CHIPS)), row,
                  pl.BlockSpec((1, 9, D), lambda i, q: (grp(i), 0, 0)), pl.BlockSpec((1, D), lambda i, q: (0, 0)),
                  pl.BlockSpec((None, None, D, wq), lambda i, q: (q // 3, j, 0, q % 3))],
        out_specs=[row, pl.BlockSpec((1, 8, D), lambda i, q: (i, 0, 0))],
        out_shape=[jax.ShapeDtypeStruct((T, D), F32), jax.ShapeDtypeStruct((T // tm, 8, D), F32)],
        scratch_shapes=[pltpu.VMEM((tm, D), F32)],
        compiler_params=_cparams("parallel", "arbitrary"),
    )(dh, dp, h, mod, g, w_in)


def _rope(t, c, s1, s2):
    return t * c + pltpu.roll(t, HEAD_PAD - 16, 1) * s1 + pltpu.roll(t, 16, 1) * s2


def _rope_t(dy, c, s1, s2):
    return dy * c + pltpu.roll(dy * s1, 16, 1) + pltpu.roll(dy * s2, HEAD_PAD - 16, 1)


def _mla_heads_fwd(z, g_ref, wuq_ref, wukv_ref):
    cq, ckv, krp = z[:, :Q_LORA], z[:, Q_LORA:Q_LORA + KV_LORA], z[:, Q_LORA + KV_LORA:]
    cqh, rq = _rms(cq, Q_LORA)
    ckvh, rkv = _rms(ckv, KV_LORA)
    cqn = (cqh * g_ref[0:1, :]).astype(BF16)
    ckvn = (ckvh * g_ref[1:2, :KV_LORA]).astype(BF16)
    qraw = _dot(cqn, wuq_ref[...])
    kvraw = _dot(ckvn, wukv_ref[...])
    return dict(krp=krp, cqh=cqh, rq=rq, ckvh=ckvh, rkv=rkv, cqn=cqn, ckvn=ckvn, qraw=qraw, kvraw=kvraw)


def _mla_proj_fwd(h, mod, g, gains, tabs, w_a, w_uq, w_ukv, j, dm, name):
    T, D = h.shape
    tm = min(dm.tm, 256)
    grp = lambda i: jnp.minimum(i // (dm.N // tm), dm.B)
    HP = HEAD_PAD

    def body(h_ref, mod_ref, g_ref, gn_ref, tab_ref, wa_ref, wuq_ref, wukv_ref, hn_ref, q_ref, k_ref, v_ref):
        hn = _pre(h_ref[...], g_ref[...], mod_ref[0, 3:4, :], mod_ref[0, 4:5, :]).astype(BF16)
        hn_ref[...] = hn
        f = _mla_heads_fwd(_dot(hn, wa_ref[...]), gn_ref, wuq_ref, wukv_ref)
        c, s1, s2 = tab_ref[0], tab_ref[1], tab_ref[2]
        for hd in range(HEADS):
            qh, _ = _rms(f["qraw"][:, hd * HP:(hd + 1) * HP], QK_HEAD)
            q_ref[:, hd * HP:(hd + 1) * HP] = (_rope(qh * gn_ref[2:3, :], c, s1, s2) * QK_SCALE).astype(BF16)
            kpre = jnp.concatenate([f["kvraw"][:, hd * HP:hd * HP + QK_NOPE], f["krp"]], axis=1)
            kh, _ = _rms(kpre, QK_HEAD)
            k_ref[:, hd * HP:(hd + 1) * HP] = _rope(kh * gn_ref[3:4, :], c, s1, s2).astype(BF16)
            v_ref[:, hd * V_HEAD:(hd + 1) * V_HEAD] = f["kvraw"][:, hd * HP + QK_NOPE:(hd + 1) * HP].astype(BF16)

    row = pl.BlockSpec((tm, D), lambda i: (i, 0))
    HQ = HEADS * HP
    return pl.pallas_call(
        body, name=name, grid=(T // tm,),
        in_specs=[row, pl.BlockSpec((1, 9, D), lambda i: (grp(i), 0, 0)), pl.BlockSpec((1, D), lambda i: (0, 0)),
                  pl.BlockSpec((None, 8, HP), lambda i: (j, 0, 0)), pl.BlockSpec((3, tm, HP), lambda i: (0, i, 0)),
                  pl.BlockSpec((None, D, 512), lambda i: (j, 0, 0)), pl.BlockSpec((None, Q_LORA, HQ), lambda i: (j, 0, 0)),
                  pl.BlockSpec((None, KV_LORA, HQ), lambda i: (j, 0, 0))],
        out_specs=[row, pl.BlockSpec((tm, HQ), lambda i: (i, 0)), pl.BlockSpec((tm, HQ), lambda i: (i, 0)),
                   pl.BlockSpec((tm, HEADS * V_HEAD), lambda i: (i, 0))],
        out_shape=[jax.ShapeDtypeStruct((T, D), BF16), jax.ShapeDtypeStruct((T, HQ), BF16),
                   jax.ShapeDtypeStruct((T, HQ), BF16), jax.ShapeDtypeStruct((T, HEADS * V_HEAD), BF16)],
        compiler_params=_cparams("parallel"),
    )(h, mod, g, gains, tabs, w_a, w_uq, w_ukv)


def _mla_proj_bwd(dh, dq, dkl, dkc, dvl, dvc, h, mod, g, gains, tabs, w_a, w_uq, w_ukv, j, dm, name):
    T, D = h.shape
    tm = min(dm.tm, 256)
    nblk = T // tm
    grp = lambda i: jnp.minimum(i // (dm.N // tm), dm.B)
    HP = HEAD_PAD
    HQ = HEADS * HP

    nlat = dm.B * dm.N // tm

    def body(dh_ref, dq_ref, dkl_ref, dkc_ref, dvl_ref, dvc_ref, h_ref, mod_ref, g_ref, gn_ref, tab_ref, wa_ref, wuq_ref, wukv_ref,
             dho_ref, part_ref, gwa_ref, gwuq_ref, gwukv_ref, dgn_ref, dqraw_s, dkvraw_s):
        i = pl.program_id(0)
        pick = lambda lat_ref, ctx_ref, cols: jnp.where(i < nlat, lat_ref[:, cols], ctx_ref[:, cols])

        @pl.when(i == 0)
        def _():
            gwa_ref[...] = jnp.zeros_like(gwa_ref)
            gwuq_ref[...] = jnp.zeros_like(gwuq_ref)
            gwukv_ref[...] = jnp.zeros_like(gwukv_ref)
            dgn_ref[...] = jnp.zeros_like(dgn_ref)

        hv = h_ref[...]
        hn = _pre(hv, g_ref[...], mod_ref[0, 3:4, :], mod_ref[0, 4:5, :]).astype(BF16)
        f = _mla_heads_fwd(_dot(hn, wa_ref[...]), gn_ref, wuq_ref, wukv_ref)
        c, s1, s2 = tab_ref[0], tab_ref[1], tab_ref[2]
        gq, gk = gn_ref[2:3, :], gn_ref[3:4, :]
        dgq = jnp.zeros((1, HP), F32)
        dgk = jnp.zeros((1, HP), F32)
        dkrp = jnp.zeros((tm, HP - QK_NOPE), F32)
        for hd in range(HEADS):
            qh, rq = _rms(f["qraw"][:, hd * HP:(hd + 1) * HP], QK_HEAD)
            dqn = _rope_t(dq_ref[:, hd * HP:(hd + 1) * HP] * QK_SCALE, c, s1, s2)
            dgq = dgq + jnp.sum(dqn * qh, axis=0, keepdims=True)
            dqraw_s[:, hd * HP:(hd + 1) * HP] = _rms_bwd(dqn * gq, qh, rq, QK_HEAD)
            kpre = jnp.concatenate([f["kvraw"][:, hd * HP:hd * HP + QK_NOPE], f["krp"]], axis=1)
            kh, rk = _rms(kpre, QK_HEAD)
            dkn = _rope_t(pick(dkl_ref, dkc_ref, slice(hd * HP, (hd + 1) * HP)), c, s1, s2)
            dgk = dgk + jnp.sum(dkn * kh, axis=0, keepdims=True)
            dkpre = _rms_bwd(dkn * gk, kh, rk, QK_HEAD)
            dkvraw_s[:, hd * HP:hd * HP + QK_NOPE] = dkpre[:, :QK_NOPE]
            dkrp = dkrp + dkpre[:, QK_NOPE:]
            dkvraw_s[:, hd * HP + QK_NOPE:(hd + 1) * HP] = pick(dvl_ref, dvc_ref, slice(hd * V_HEAD, (hd + 1) * V_HEAD))
        dqraw = dqraw_s[...].astype(BF16)
        dkvraw = dkvraw_s[...].astype(BF16)
        gwuq_ref[...] += _dot_tn(f["cqn"], dqraw)
        gwukv_ref[...] += _dot_tn(f["ckvn"], dkvraw)
        dcqn = _dot_nt(dqraw, wuq_ref[...])
        dckvn = _dot_nt(dkvraw, wukv_ref[...])
        dgqa = jnp.sum(dcqn * f["cqh"], axis=0, keepdims=True)
        dgkva = jnp.sum(dckvn * f["ckvh"], axis=0, keepdims=True)
        dcq = _rms_bwd(dcqn * gn_ref[0:1, :], f["cqh"], f["rq"], Q_LORA)
        dckv = _rms_bwd(dckvn * gn_ref[1:2, :KV_LORA], f["ckvh"], f["rkv"], KV_LORA)
        dz = jnp.concatenate([dcq, dckv, dkrp], axis=1).astype(BF16)
        gwa_ref[...] += _dot_tn(hn, dz)
        dhn = _dot_nt(dz, wa_ref[...])
        dhb, dshift, dscale, dg = _pre_bwd(dhn, hv, g_ref[...], mod_ref[0, 4:5, :])
        dho_ref[...] = dh_ref[...] + dhb
        _write_part(part_ref, dshift, dscale, None, dg)
        dgn_ref[0:1, :] += dgqa
        dgn_ref[1:2, :KV_LORA] += dgkva
        dgn_ref[2:3, :] += dgq
        dgn_ref[3:4, :] += dgk

    row = pl.BlockSpec((tm, D), lambda i: (i, 0))
    wide = pl.BlockSpec((tm, HQ), lambda i: (i, 0))
    const2 = lambda i: (0, 0)
    return pl.pallas_call(
        body, name=name, grid=(nblk,),
        in_specs=[row, wide, pl.BlockSpec((tm, HQ), lambda i: (jnp.minimum(i, nlat - 1), 0)),
                  pl.BlockSpec((tm, HQ), lambda i: (jnp.maximum(i - nlat, 0), 0)),
                  pl.BlockSpec((tm, HEADS * V_HEAD), lambda i: (jnp.minimum(i, nlat - 1), 0)),
                  pl.BlockSpec((tm, HEADS * V_HEAD), lambda i: (jnp.maximum(i - nlat, 0), 0)), row,
                  pl.BlockSpec((1, 9, D), lambda i: (grp(i), 0, 0)), pl.BlockSpec((1, D), const2),
                  pl.BlockSpec((None, 8, HP), lambda i: (j, 0, 0)), pl.BlockSpec((3, tm, HP), lambda i: (0, i, 0)),
                  pl.BlockSpec((None, D, 512), lambda i: (j, 0, 0)), pl.BlockSpec((None, Q_LORA, HQ), lambda i: (j, 0, 0)),
                  pl.BlockSpec((None, KV_LORA, HQ), lambda i: (j, 0, 0))],
        out_specs=[row, pl.BlockSpec((1, 8, D), lambda i: (i, 0, 0)), pl.BlockSpec((D, 512), const2),
                   pl.BlockSpec((Q_LORA, HQ), const2), pl.BlockSpec((KV_LORA, HQ), const2), pl.BlockSpec((8, HP), const2)],
        out_shape=[jax.ShapeDtypeStruct((T, D), F32), jax.ShapeDtypeStruct((nblk, 8, D), F32),
                   jax.ShapeDtypeStruct((D, 512), F32), jax.ShapeDtypeStruct((Q_LORA, HQ), F32),
                   jax.ShapeDtypeStruct((KV_LORA, HQ), F32), jax.ShapeDtypeStruct((8, HP), F32)],
        scratch_shapes=[pltpu.VMEM((tm, HQ), F32), pltpu.VMEM((tm, HQ), F32)],
        compiler_params=_cparams("arbitrary"),
    )(dh, dq, dkl, dkc, dvl, dvc, h, mod, g, gains, tabs, w_a, w_uq, w_ukv)


def _attn_specs(dm):
    tq = dm.CTX
    nq = dm.N // tq
    cblk0 = dm.B * nq
    HP = HEAD_PAD
    qrow = lambda b, i: jnp.where(i < nq, b * nq + i, cblk0 + b)
    return dict(
        tq=tq, nq=nq,
        q=pl.BlockSpec((tq, HP), lambda b, hd, i: (qrow(b, i), hd)),
        k_lat=pl.BlockSpec((dm.N, HP), lambda b, hd, i: (b, hd)),
        k_ctx=pl.BlockSpec((tq, HP), lambda b, hd, i: (cblk0 + b, hd)),
        v_lat=pl.BlockSpec((dm.N, V_HEAD), lambda b, hd, i: (b, hd)),
        v_ctx=pl.BlockSpec((tq, V_HEAD), lambda b, hd, i: (cblk0 + b, hd)),
        o=pl.BlockSpec((tq, V_HEAD), lambda b, hd, i: (qrow(b, i), hd)),
    )


def _attn_exp(q, keys, first_off=None):
    s = [_dot_nt(q, kk) for kk in keys]
    if first_off is not None:
        s[0] = s[0] + first_off
    m = functools.reduce(jnp.maximum, [jnp.max(x, axis=-1, keepdims=True) for x in s])
    e = [jnp.exp(x - m) for x in s]
    return e, 1.0 / sum(jnp.sum(x, axis=-1, keepdims=True) for x in e)


def _attn_fwd(q, k, v, dm, name, rider=None):
    T = dm.T
    sp = _attn_specs(dm)
    nq = sp["nq"]

    def body(q_ref, kl_ref, kc_ref, vl_ref, vc_ref, o_ref):
        i = pl.program_id(2)

        @pl.when(i < nq)
        def _():
            (el, ec), inv = _attn_exp(q_ref[...], [kl_ref[...], kc_ref[...]])
            o_ref[...] = ((_dot(el.astype(BF16), vl_ref[...]) + _dot(ec.astype(BF16), vc_ref[...])) * inv).astype(BF16)

        @pl.when(i == nq)
        def _():
            (ec,), inv = _attn_exp(q_ref[...], [kc_ref[...]])
            o_ref[...] = (_dot(ec.astype(BF16), vc_ref[...]) * inv).astype(BF16)

    (o,), got = _hosted(
        body, rider, name=name, grid=(dm.B, HEADS, nq + 1),
        in_specs=[sp["q"], sp["k_lat"], sp["k_ctx"], sp["v_lat"], sp["v_ctx"]], out_specs=[sp["o"]],
        out_shape=[jax.ShapeDtypeStruct((T, HEADS * V_HEAD), BF16)], scratch_shapes=[],
        sem=("parallel", "parallel", "arbitrary"), args=(q, k, k, v, v))
    return o, got


def _attn_bwd(q, k, v, o, do, dm, name):
    T = dm.T
    sp = _attn_specs(dm)
    nq, tq = sp["nq"], sp["tq"]
    HP, HQ, HV = HEAD_PAD, HEADS * HEAD_PAD, HEADS * V_HEAD

    def body(q_ref, kl_ref, kc_ref, vl_ref, vc_ref, o_ref, do_ref, dq_ref, dkl_ref, dkc_ref, dvl_ref, dvc_ref):
        i = pl.program_id(2)

        @pl.when(i == 0)
        def _():
            dkl_ref[...] = jnp.zeros_like(dkl_ref)
            dkc_ref[...] = jnp.zeros_like(dkc_ref)
            dvl_ref[...] = jnp.zeros_like(dvl_ref)
            dvc_ref[...] = jnp.zeros_like(dvc_ref)

        qv = q_ref[...]
        dov = do_ref[...]
        dob = dov.astype(BF16)
        delta = jnp.sum(dov * o_ref[...].astype(F32), axis=-1, keepdims=True)
        (el, ec), inv = _attn_exp(qv, [kl_ref[...], kc_ref[...]], jnp.where(i == nq, NEG, 0.0))
        pl_, pc = el * inv, ec * inv
        dsl = (pl_ * (_dot_nt(dob, vl_ref[...]) - delta)).astype(BF16)
        dsc = (pc * (_dot_nt(dob, vc_ref[...]) - delta)).astype(BF16)
        dq_ref[...] = _dot(dsl, kl_ref[...]) + _dot(dsc, kc_ref[...])
        dkl_ref[...] += _dot_tn(dsl, qv)
        dkc_ref[...] += _dot_tn(dsc, qv)
        dvl_ref[...] += _dot_tn(pl_.astype(BF16), dob)
        dvc_ref[...] += _dot_tn(pc.astype(BF16), dob)

    return pl.pallas_call(
        body, name=name, grid=(dm.B, HEADS, nq + 1),
        in_specs=[sp["q"], sp["k_lat"], sp["k_ctx"], sp["v_lat"], sp["v_ctx"], sp["o"], sp["o"]],
        out_specs=[sp["q"], sp["k_lat"], pl.BlockSpec((tq, HP), lambda b, hd, i: (b, hd)),
                   sp["v_lat"], pl.BlockSpec((tq, V_HEAD), lambda b, hd, i: (b, hd))],
        out_shape=[jax.ShapeDtypeStruct((T, HQ), F32), jax.ShapeDtypeStruct((dm.B * dm.N, HQ), F32),
                   jax.ShapeDtypeStruct((dm.B * dm.CTX, HQ), F32), jax.ShapeDtypeStruct((dm.B * dm.N, HV), F32),
                   jax.ShapeDtypeStruct((dm.B * dm.CTX, HV), F32)],
        compiler_params=_cparams("parallel", "parallel", "arbitrary"),
    )(q, k, k, v, v, o, do)


def _loss_grad(h, target, dm, name):
    T, D = h.shape
    tm = dm.tm
    nlat = dm.B * dm.N // tm

    def body(h_ref, t_ref, dh_ref, ls_ref):
        lat = (pl.program_id(0) < nlat).astype(F32)
        diff = (h_ref[...] - t_ref[...]) * lat
        dh_ref[...] = diff * (1.0 / D)
        ls_ref[...] = jnp.zeros(ls_ref.shape, F32) + (0.5 / D) * jnp.sum(diff * diff)

    return pl.pallas_call(
        body, name=name, grid=(T // tm,),
        in_specs=[pl.BlockSpec((tm, D), lambda i: (i, 0)), pl.BlockSpec((tm, D), lambda i: (jnp.minimum(i, nlat - 1), 0))],
        out_specs=[pl.BlockSpec((tm, D), lambda i: (i, 0)), pl.BlockSpec((1, 8, 128), lambda i: (i, 0, 0))],
        out_shape=[jax.ShapeDtypeStruct((T, D), F32), jax.ShapeDtypeStruct((T // tm, 8, 128), F32)],
        compiler_params=_cparams("parallel"),
    )(h, target)


def _col_block(cols, target=1152):
    return max(t for t in range(128, min(cols, target) + 1, 128) if cols % t == 0)


def _mod_fwd(cond, w_mod, b_mod, name):
    L, D, C = w_mod.shape
    R = cond.shape[0]
    cb = _col_block(C)

    def body(c_ref, w_ref, b_ref, o_ref):
        cv = c_ref[...]
        sc = (cv * jax.nn.sigmoid(cv)).astype(BF16)
        o_ref[...] = _dot(sc, w_ref[...].astype(BF16)) + b_ref[...]

    return pl.pallas_call(
        body, name=name, grid=(L, C // cb),
        in_specs=[pl.BlockSpec((R, D), lambda l, c: (0, 0)), pl.BlockSpec((None, D, cb), lambda l, c: (l, 0, c)),
                  pl.BlockSpec((None, 1, cb), lambda l, c: (l, 0, c))],
        out_specs=pl.BlockSpec((None, R, cb), lambda l, c: (l, 0, c)),
        out_shape=jax.ShapeDtypeStruct((L, R, C), F32),
        compiler_params=_cparams("parallel", "parallel"),
    )(cond, w_mod, b_mod)


def _mod_bwd(cond, dmod, w_mod, name):
    L, D, C = w_mod.shape
    R = cond.shape[0]
    cb = _col_block(C)
    nc = C // cb

    def body(c_ref, dm_ref, w_ref, gw_ref, ds_ref):
        cv = c_ref[...]
        sc = (cv * jax.nn.sigmoid(cv)).astype(BF16)
        dmv = dm_ref[...].astype(BF16)
        gw_ref[...] = _dot_tn(sc, dmv)
        part = _dot_nt(dmv, w_ref[...].astype(BF16))

        @pl.when(pl.program_id(1) == 0)
        def _():
            ds_ref[...] = part

        @pl.when(pl.program_id(1) > 0)
        def _():
            ds_ref[...] += part

    return pl.pallas_call(
        body, name=name, grid=(L, nc),
        in_specs=[pl.BlockSpec((R, D), lambda l, c: (0, 0)), pl.BlockSpec((None, R, cb), lambda l, c: (l, 0, c)),
                  pl.BlockSpec((None, D, cb), lambda l, c: (l, 0, c))],
        out_specs=[pl.BlockSpec((None, D, cb), lambda l, c: (l, 0, c)), pl.BlockSpec((None, R, D), lambda l, c: (l, 0, 0))],
        out_shape=[jax.ShapeDtypeStruct((L, D, C), F32), jax.ShapeDtypeStruct((L, R, D), F32)],
        compiler_params=_cparams("parallel", "arbitrary"),
    )(cond, dmod, w_mod)


def _row_block(rows, cols, budget=1 << 20):
    best = None
    for t in range(16, rows + 1, 16):
        if rows % t == 0 and t * cols * 4 <= budget:
            best = t
    return best if best is not None else rows


def _sum_slots(recv, own, chip, core, buf, pieces, piece, name):
    S, R, C = recv.shape
    tr = _row_block(R, C, budget=512 << 10)

    def body(ids_ref, r_ref, p_ref, *refs):
        o_ref = refs[-1]
        acc = None
        for s in range(S):
            v = jnp.where(ids_ref[0] == s, p_ref[s], r_ref[s]).astype(F32)
            acc = v if acc is None else acc + v
        o_ref[...] = acc

    blk = pl.BlockSpec((S, tr, C), lambda i, ids: (0, i, 0))
    return pl.pallas_call(
        body, name=name,
        grid_spec=pltpu.PrefetchScalarGridSpec(
            num_scalar_prefetch=1, grid=(R // tr,), in_specs=[blk, blk] + [pl.BlockSpec(memory_space=pl.ANY)] * (buf is not None),
            out_specs=pl.BlockSpec((None, None, tr, C), lambda i, ids: (piece, ids[1], i, 0))),
        out_shape=jax.ShapeDtypeStruct((pieces, 2, R, C), F32), input_output_aliases={3: 0} if buf is not None else {},
        compiler_params=_cparams("parallel"),
    )(jnp.stack([chip, core]).astype(jnp.int32), recv, own, *([buf] if buf is not None else []))


def _adamw(w, gs, m, v, name):
    ng = len(gs)
    R, C = w.shape
    tr = _row_block(R, C)
    c1 = 1.0 / (1.0 - ADAM_B1 ** ADAM_STEP)
    c2 = 1.0 / (1.0 - ADAM_B2 ** ADAM_STEP)

    def body(w_ref, *refs):
        m_ref, v_ref, g_ref, d_ref, mo_ref, vo_ref = refs[ng:]
        g = refs[0][...]
        for g_more in refs[1:ng]:
            g = g + g_more[...]
        g_ref[...] = g
        mn = ADAM_B1 * m_ref[...] + (1.0 - ADAM_B1) * g
        vn = ADAM_B2 * v_ref[...] + (1.0 - ADAM_B2) * (g * g)
        mo_ref[...] = mn
        vo_ref[...] = vn
        d_ref[...] = -ADAM_LR * ((mn * c1) / (jnp.sqrt(vn * c2) + ADAM_EPS) + ADAM_WD * w_ref[...])

    blk = pl.BlockSpec((tr, C), lambda i: (i, 0))
    sd = jax.ShapeDtypeStruct((R, C), F32)
    return pl.pallas_call(
        body, name=name, grid=(R // tr,), in_specs=[blk] * (3 + ng), out_specs=[blk] * 4, out_shape=[sd] * 4,
        compiler_params=_cparams("parallel"),
    )(w, *gs, m, v)


def _rope_tables(dm):
    n = dm.N
    t = jnp.arange(n)
    r = (t // GRID_W).astype(F32)
    col = (t % GRID_W).astype(F32)
    nf = QK_ROPE // 4
    inv = ROPE_BASE ** (-jnp.arange(nf, dtype=F32) / nf)
    ang = jnp.stack([r[:, None] * inv, col[:, None] * inv], axis=1)
    cos, sin = jnp.cos(ang), jnp.sin(ang)
    zero = jnp.zeros_like(sin)
    c64 = jnp.stack([cos, cos], axis=2).reshape(n, QK_ROPE)
    s1 = jnp.stack([-sin, zero], axis=2).reshape(n, QK_ROPE)
    s2 = jnp.stack([zero, sin], axis=2).reshape(n, QK_ROPE)

    def pad(x, fill):
        return jnp.concatenate([jnp.full((n, QK_NOPE), fill, F32), x, jnp.full((n, HEAD_PAD - QK_HEAD), fill, F32)], axis=1)

    lat = jnp.stack([pad(c64, 1.0), pad(s1, 0.0), pad(s2, 0.0)])
    lat = jnp.tile(lat, (1, dm.B, 1))
    nctx = dm.B * dm.CTX
    ctx = jnp.stack([jnp.ones((nctx, HEAD_PAD), F32), jnp.zeros((nctx, HEAD_PAD), F32), jnp.zeros((nctx, HEAD_PAD), F32)])
    return jnp.concatenate([lat, ctx], axis=1)


def _fold_parts(part, dm):
    nblk = part.shape[0]
    nb = (dm.N * nblk) // dm.T
    groups = [part[b * nb:(b + 1) * nb].sum(axis=0) for b in range(dm.B)]
    groups.append(part[dm.B * nb:].sum(axis=0))
    return jnp.stack(groups)


def _flat2(a):
    return a.reshape(-1, a.shape[-1])


def kernel(x, c, ctx, c_ctx, w_mod, b_mod, g_norm, ffn_w1, ffn_w3, ffn_w2, sc_w_in, sc_conv, sc_w_out, mla_w_a, mla_g_qa, mla_w_uq, mla_g_kva, mla_w_ukv, mla_g_q, mla_g_k, mla_w_o, loss_target, m_c_ctx, m_w_mod, m_b_mod, m_g_norm, m_ffn_w1, m_ffn_w3, m_ffn_w2, m_sc_w_in, m_sc_conv, m_sc_w_out, m_mla_w_a, m_mla_g_qa, m_mla_w_uq, m_mla_g_kva, m_mla_w_ukv, m_mla_g_q, m_mla_g_k, m_mla_w_o, v_c_ctx, v_w_mod, v_b_mod, v_g_norm, v_ffn_w1, v_ffn_w3, v_ffn_w2, v_sc_w_in, v_sc_conv, v_sc_w_out, v_mla_w_a, v_mla_g_qa, v_mla_w_uq, v_mla_g_kva, v_mla_w_ukv, v_mla_g_q, v_mla_g_k, v_mla_w_o):
    B, N, D = x.shape
    CTX = ctx.shape[1]
    T = B * (N + CTX)
    tm = next(t for t in (512, 256, 128, 64, 32, 16) if N % t == 0 and (B * CTX) % t == 0)
    dm = Dims(B, N, CTX, D, T, tm)
    L = w_mod.shape[0]
    La, Lb = sc_w_in.shape[0], mla_w_a.shape[0]
    S = N_CHIPS
    ndev = 2 * S
    xi, yi, ci = lax.axis_index("x"), lax.axis_index("y"), lax.axis_index("c")
    chip = 2 * xi + yi
    dev = 2 * chip + ci
    weights = dict(c_ctx=c_ctx, w_mod=w_mod, b_mod=b_mod, g_norm=g_norm, ffn_w1=ffn_w1, ffn_w3=ffn_w3, ffn_w2=ffn_w2,
                   sc_w_in=sc_w_in, sc_conv=sc_conv, sc_w_out=sc_w_out, mla_w_a=mla_w_a, mla_g_qa=mla_g_qa,
                   mla_w_uq=mla_w_uq, mla_g_kva=mla_g_kva, mla_w_ukv=mla_w_ukv, mla_g_q=mla_g_q, mla_g_k=mla_g_k,
                   mla_w_o=mla_w_o)
    mom = dict(c_ctx=(m_c_ctx, v_c_ctx), w_mod=(m_w_mod, v_w_mod), b_mod=(m_b_mod, v_b_mod), g_norm=(m_g_norm, v_g_norm),
               ffn_w1=(m_ffn_w1, v_ffn_w1), ffn_w3=(m_ffn_w3, v_ffn_w3), ffn_w2=(m_ffn_w2, v_ffn_w2),
               sc_w_in=(m_sc_w_in, v_sc_w_in), sc_conv=(m_sc_conv, v_sc_conv), sc_w_out=(m_sc_w_out, v_sc_w_out),
               mla_w_a=(m_mla_w_a, v_mla_w_a), mla_g_qa=(m_mla_g_qa, v_mla_g_qa), mla_w_uq=(m_mla_w_uq, v_mla_w_uq),
               mla_g_kva=(m_mla_g_kva, v_mla_g_kva), mla_w_ukv=(m_mla_w_ukv, v_mla_w_ukv), mla_g_q=(m_mla_g_q, v_mla_g_q),
               mla_g_k=(m_mla_g_k, v_mla_g_k), mla_w_o=(m_mla_w_o, v_mla_w_o))

    big = ["ffn_w1", "ffn_w3", "ffn_w2", "sc_w_in", "sc_w_out", "mla_w_a", "mla_w_uq", "mla_w_ukv", "mla_w_o"]
    F = ffn_w1.shape[-1]
    transposed = ("ffn_w1", "ffn_w3")
    for n in transposed:
        weights[n] = jnp.swapaxes(weights[n], 2, 3)
        mom[n] = tuple(jnp.swapaxes(a, 2, 3) for a in mom[n])
    mixer_names =(["sc_w_in", "sc_w_out"], ["mla_w_a", "mla_w_uq", "mla_w_ukv", "mla_w_o"])

    def placed(name, piece):
        w2 = _flat2(weights[name])
        rows = w2.shape[0] // weights[name].shape[0]
        return _place_cast(w2, piece * rows, rows, chip, S, "place_weight")

    bufs = [{n: placed(n, l) for n in ("ffn_w1", "ffn_w3", "ffn_w2")} for l in range(L)]
    for l in range(L):
        bufs[l].update({n: placed(n, l // 2) for n in mixer_names[l % 2]})
    group_a = ["ffn_w1", "ffn_w3"]
    group_b = lambda l: ["ffn_w2"] + mixer_names[l % 2]
    first = group_a + group_b(0)
    bufs[0].update(zip(first, _ride_alone(_gather_rider([bufs[0][n] for n in first]), "gather_weights")))

    def layer_weights(l):
        b = bufs[l]
        w = dict(w1=b["ffn_w1"].reshape(S, 2, F, D), w3=b["ffn_w3"].reshape(S, 2, F, D), w2=b["ffn_w2"].reshape(S, 2, F, D))
        if l % 2 == 0:
            w["w_in"] = b["sc_w_in"][:, None]
            w["w_out"] = b["sc_w_out"].reshape(1, D, D)
        else:
            w["w_a"] = jnp.pad(b["mla_w_a"].reshape(1, D, -1), ((0, 0), (0, 0), (0, 512 - (Q_LORA + KV_LORA + QK_ROPE))))
            wuq = jnp.moveaxis(b["mla_w_uq"], 0, 1).reshape(1, Q_LORA, HEADS, QK_HEAD)
            w["w_uq"] = jnp.pad(wuq, ((0, 0), (0, 0), (0, 0), (0, HEAD_PAD - QK_HEAD))).reshape(1, Q_LORA, HEADS * HEAD_PAD)
            w["w_ukv"] = jnp.moveaxis(b["mla_w_ukv"], 0, 1).reshape(1, KV_LORA, HEADS * HEAD_PAD)
            w["w_o"] = b["mla_w_o"].reshape(1, HEADS * V_HEAD, D)
        return w

    vecs = ["g_norm", "sc_conv", "mla_g_qa"]
    gathered = _exchange([_flat2(weights[n]) for n in vecs], ("x", "y"), False, "gather_vectors")
    gw = {n: g.reshape((S,) + weights[n].shape) for n, g in zip(vecs, gathered)}
    gnorm = jnp.moveaxis(gw["g_norm"], 0, 2).reshape(L, 3, D)
    convw = jnp.moveaxis(gw["sc_conv"], 0, 2).reshape(La, 3, D)
    gqa = jnp.moveaxis(gw["mla_g_qa"], 0, 1).reshape(Lb, Q_LORA)
    padl = lambda a: jnp.pad(a, ((0, 0), (0, HEAD_PAD - a.shape[1])))
    gains = jnp.stack([padl(gqa), padl(mla_g_kva), padl(mla_g_q), padl(mla_g_k)], axis=1)
    gains = jnp.pad(gains, ((0, 0), (0, 4), (0, 0)))

    R = -(-(ndev * B + 1) // 16) * 16
    call = _exchange([c], ("x", "y", "c"), False, "gather_cond")[0].reshape(ndev * B, D)
    cond = jnp.concatenate([call, c_ctx[None], jnp.zeros((R - ndev * B - 1, D), F32)], axis=0)
    C = w_mod.shape[-1]
    bm = lax.dynamic_slice_in_dim(b_mod, chip * C, C, axis=1)[:, None, :]
    mshard = _mod_fwd(cond, w_mod, bm, "mod_fwd")
    mfull = _exchange([mshard.reshape(L * R, C)], ("x", "y"), False, "gather_mod")[0].reshape(S, L, R, C)
    mfull = jnp.moveaxis(mfull, 0, 2).reshape(L, R, S * C)
    mine = lax.dynamic_slice_in_dim(mfull, dev * B, B, axis=1)
    mod = jnp.concatenate([mine, mfull[:, ndev * B:ndev * B + 1]], axis=1).reshape(L, B + 1, 9, D)

    tabs = _rope_tables(dm)
    h = jnp.concatenate([x.reshape(B * N, D), ctx.reshape(B * CTX, D)], axis=0)

    saved = []
    lw = [None] * L
    for l in range(L):
        kind, j = l % 2, l // 2
        W = lw[l] = layer_weights(l)
        sv = {}
        sv["h0"] = h

        def riding(names):
            if l + 1 == L:
                return None, lambda got: None
            return _gather_rider([bufs[l + 1][n] for n in names]), lambda got: bufs[l + 1].update(zip(names, got))

        rider, keep = riding(["ffn_w1"])
        (h, sv["a1"], sv["b1"], sv["hn1"], sv["y1"]), got = _ffn_fwd(h, mod[l], gnorm[l, 0:1], W["w1"], W["w3"], W["w2"], 0, dm,
                                                                      "ffn_fwd", rider)
        keep(got)
        sv["h1"] = h
        rider, keep = riding(["ffn_w3"] + mixer_names[(l + 1) % 2])
        if kind == 0:
            (sv["p"], sv["hnm"]), got = _sc_in_fwd(h, mod[l], gnorm[l, 1:2], W["w_in"], 0, dm, "sc_in_fwd", rider)
            sv["z"] = _conv_fwd(sv["p"], convw[j], dm, "conv_fwd")
            h, sv["ym"] = _out_fwd(sv["z"], W["w_out"], h, mod[l], 0, dm, "sc_out_fwd")
        else:
            sv["hnm"], sv["q"], sv["k"], sv["v"] = _mla_proj_fwd(h, mod[l], gnorm[l, 1:2], gains[j:j + 1], tabs, W["w_a"], W["w_uq"],
                                                                 W["w_ukv"], 0, dm, "mla_proj_fwd")
            sv["o"], got = _attn_fwd(sv["q"], sv["k"], sv["v"], dm, "attn_fwd", rider)
            h, sv["ym"] = _out_fwd(sv["o"], W["w_o"], h, mod[l], 0, dm, "mla_out_fwd")
        keep(got)
        sv["h2"] = h
        rider, keep = riding(["ffn_w2"])
        (h, sv["a2"], sv["b2"], sv["hn2"], sv["y2"]), got = _ffn_fwd(h, mod[l], gnorm[l, 2:3], W["w1"], W["w3"], W["w2"], 1, dm,
                                                                      "ffn_fwd", rider)
        keep(got)
        saved.append(sv)

    dh, lsum = _loss_grad(h, loss_target.reshape(B * N, D), dm, "loss_grad")
    loss = lax.psum(jnp.sum(lsum[:, 0, 0]), ("x", "y", "c"))

    wq = D // S
    gsum = {n: None for n in big}
    npieces = {n: weights[n].shape[0] * (weights[n].shape[1] if n.startswith("ffn") else 1) for n in big}
    dmod = [None] * L
    dgn = [None] * L
    dconv = [None] * La
    dgains = [None] * Lb
    tk = tm * next(f for f in (3, 2, 1) if (T // tm) % f == 0)
    nk = T // tk
    full_a = pl.BlockSpec((tk, D), lambda s, kk: (kk, 0))
    shard_b = pl.BlockSpec((None, tk, F), lambda s, kk: (s, kk, 0))
    per_slot = lambda r_, c_: pl.BlockSpec((None, r_, c_), lambda s, kk: (s, 0, 0))

    def make_job(grads):
        theirs = _swap_halves([g_ for _, _, g_ in grads], "swap_halves")
        return [(n, p, _pair_sum(g_, r_, ci, "pair_sum")) for (n, p, g_), r_ in zip(grads, theirs)]

    def finish_job(job, recv):
        for (n, p, pair), r_ in zip(job, recv):
            gsum[n] = _sum_slots(r_, pair, chip, ci, gsum[n], npieces[n], p, "sum_slots")

    def ffn_back(dh, sv, l, k, job):
        sfx = "1" if k == 0 else "2"
        W = lw[l]
        rider = _scatter_rider([pair for _, _, pair in job]) if job else None
        (dh, da, db, sw, dy, part), recv = _ffn_bwd(dh, sv["h0" if k == 0 else "h2"], mod[l], gnorm[l, 2 * k:2 * k + 1], sv["y" + sfx],
                                                    sv["a" + sfx], sv["b" + sfx], W["w1"], W["w3"], W["w2"], k, dm, "ffn_bwd", rider)
        finish_job(job, recv)
        g1 = _mm_tn(da, sv["hn" + sfx], shard_b, full_a, (S, F, D), per_slot(F, D), (S, nk), "gw1")
        g3 = _mm_tn(db, sv["hn" + sfx], shard_b, full_a, (S, F, D), per_slot(F, D), (S, nk), "gw3")
        g2 = _mm_tn(sw, dy, shard_b, full_a, (S, F, D), per_slot(F, D), (S, nk), "gw2")
        p = 2 * l + k
        return dh, _fold_parts(part, dm), [("ffn_w1", p, g1), ("ffn_w3", p, g3), ("ffn_w2", p, g2)]

    one = (1, nk)
    a1 = lambda kdim: pl.BlockSpec((tk, kdim), lambda s, kk: (kk, 0))
    pending = []
    for l in reversed(range(L)):
        kind, j = l % 2, l // 2
        sv = saved[l]
        W = lw[l]
        dh, p2, grads = ffn_back(dh, sv, l, 1, pending)
        job2 = make_job(grads)
        if kind == 0:
            dy, dz, pg = _out_bwd(dh, sv["ym"], W["w_out"], mod[l], 0, dm, "sc_out_bwd")
            g_out = _mm_tn(sv["z"], dy, a1(D), a1(D), (1, D, D), per_slot(D, D), one, "gw_sc_out")
            dp, dconv[j] = _conv_bwd(dz, sv["p"], convw[j], dm, "conv_bwd")
            g_in = _mm_tn(sv["hnm"], dp, pl.BlockSpec((tk, D), lambda q, kk: (kk, 0)),
                          pl.BlockSpec((None, tk, wq), lambda q, kk: (q // S, kk, q % S)), (3 * S, D, wq), per_slot(D, wq),
                          (3 * S, nk), "gw_sc_in")
            dh, pm = _sc_in_bwd(dh, dp, sv["h1"], mod[l], gnorm[l, 1:2], W["w_in"], 0, dm, "sc_in_bwd")
            grads = [("sc_w_in", j, jnp.moveaxis(g_in.reshape(S, 3, D, wq), 1, 2).reshape(S, D, 3 * wq)),
                     ("sc_w_out", j, g_out.reshape(S, D // S, D))]
        else:
            dy, do, pg = _out_bwd(dh, sv["ym"], W["w_o"], mod[l], 0, dm, "mla_out_bwd")
            g_o = _mm_tn(sv["o"], dy, a1(HEADS * V_HEAD), a1(D), (1, HEADS * V_HEAD, D), per_slot(HEADS * V_HEAD, D), one, "gw_mla_o")
            dq, dkl, dkc, dvl, dvc = _attn_bwd(sv["q"], sv["k"], sv["v"], sv["o"], do, dm, "attn_bwd")
            dh, pm, g_a, g_uq, g_ukv, dgains[j] = _mla_proj_bwd(
                dh, dq, dkl, dkc, dvl, dvc, sv["h1"], mod[l], gnorm[l, 1:2], gains[j:j + 1], tabs, W["w_a"], W["w_uq"], W["w_ukv"], 0, dm, "mla_proj_bwd")
            g_uq = g_uq.reshape(Q_LORA, HEADS, HEAD_PAD)[..., :QK_HEAD].reshape(Q_LORA, S, -1)
            grads = [("mla_w_a", j, g_a[:, :Q_LORA + KV_LORA + QK_ROPE].reshape(S, D // S, -1).astype(BF16)),
                     ("mla_w_uq", j, jnp.moveaxis(g_uq, 1, 0).astype(BF16)),
                     ("mla_w_ukv", j, jnp.moveaxis(g_ukv.reshape(KV_LORA, S, -1), 1, 0).astype(BF16)),
                     ("mla_w_o", j, g_o.reshape(S, HEADS * V_HEAD // S, D))]
        jobm = make_job(grads)
        pm = _fold_parts(pm, dm) + _fold_parts(pg, dm)
        dh, p0, grads = ffn_back(dh, sv, l, 0, job2 + jobm)
        pending = make_job(grads)
        dmod[l] = jnp.concatenate([p0[:, 0:3], pm[:, 0:3], p2[:, 0:3]], axis=1).reshape(B + 1, 9 * D)
        dgn[l] = jnp.stack([p0[:, 3].sum(0), pm[:, 3].sum(0), p2[:, 3].sum(0)])
    grad_x = dh[:B * N].reshape(B, N, D)
    finish_job(pending, _ride_alone(_scatter_rider([pair for _, _, pair in pending]), "scatter_grads"))
    gsum = dict(zip(big, _swap_cores_inplace([gsum[n] for n in big], "swap_cores")))

    dgains_a = jnp.stack(dgains)
    small = [jnp.stack(dmod).reshape(-1), jnp.stack(dgn).reshape(-1), jnp.stack(dconv).reshape(-1), dgains_a.reshape(-1)]
    sizes = [s_.shape[0] for s_ in small]
    flat = jnp.concatenate(small)
    pad = (-flat.shape[0]) % 1024
    flat = jnp.pad(flat, (0, pad)).reshape(-1, 128)
    allsmall = _exchange([flat], ("x", "y", "c"), False, "gather_small")[0].reshape(ndev, -1)
    offs = [0]
    for s_ in sizes:
        offs.append(offs[-1] + s_)
    dmod_all = allsmall[:, offs[0]:offs[1]].reshape(ndev, L, B + 1, 9 * D)
    tot = allsmall[:, offs[1]:offs[4]].sum(axis=0)
    g_gnorm = tot[:offs[2] - offs[1]].reshape(L, 3, D)
    g_conv = tot[offs[2] - offs[1]:offs[3] - offs[1]].reshape(La, 3, D)
    g_gains = tot[offs[3] - offs[1]:].reshape(Lb, 8, HEAD_PAD)
    dM = jnp.concatenate([jnp.moveaxis(dmod_all[:, :, :B], 0, 1).reshape(L, ndev * B, 9 * D),
                          dmod_all[:, :, B].sum(axis=0)[:, None, :], jnp.zeros((L, R - ndev * B - 1, 9 * D), F32)], axis=1)
    g_bmod = dM.sum(axis=1)
    dM_mine = lax.dynamic_slice_in_dim(dM, chip * C, C, axis=2)
    g_wmod, dsil = _mod_bwd(cond, dM_mine, w_mod, "mod_bwd")
    dsil_ctx = dsil[:, ndev * B].sum(axis=0)
    dsil_all = _exchange([jnp.pad(dsil_ctx.reshape(-1, 128), ((0, (-(D // 128)) % 8), (0, 0)))], ("x", "y"), False, "gather_dctx")[0]
    dsil_tot = dsil_all.sum(axis=0)[:D // 128].reshape(D)
    sg = jax.nn.sigmoid(c_ctx)
    g_cctx = dsil_tot * (sg * (1.0 + c_ctx * (1.0 - sg)))

    chip_cols = lambda a, width: lax.dynamic_slice_in_dim(a, chip * width, width, axis=a.ndim - 1)
    small_grads = dict(
        c_ctx=g_cctx, b_mod=g_bmod, g_norm=chip_cols(g_gnorm, D // S), sc_conv=chip_cols(g_conv, D // S),
        mla_g_qa=chip_cols(g_gains[:, 0, :Q_LORA], Q_LORA // S), mla_g_kva=g_gains[:, 1, :KV_LORA],
        mla_g_q=g_gains[:, 2, :QK_HEAD], mla_g_k=g_gains[:, 3, :QK_HEAD])

    grads, deltas, new_m, new_v = {}, {}, {}, {}
    for n, w in weights.items():
        shape = w.shape
        w2 = _flat2(w) if w.ndim > 1 else w.reshape(1, -1)
        m2, v2 = (a.reshape(w2.shape) for a in mom[n])
        if n in gsum:
            gs = [gsum[n].reshape(w2.shape)]
        elif n == "w_mod":
            gs = [_flat2(g_wmod)]
        else:
            gs = [small_grads[n].reshape(w2.shape)]
        g_, d_, m_, v_ = _adamw(w2, gs, m2, v2, "adamw")
        grads[n], deltas[n], new_m[n], new_v[n] = (a.reshape(shape) for a in (g_, d_, m_, v_))
    for n in transposed:
        grads[n], deltas[n], new_m[n], new_v[n] = (jnp.swapaxes(a, 2, 3) for a in (grads[n], deltas[n], new_m[n], new_v[n]))

    names = list(weights)
    return (loss, grad_x, *[grads[n] for n in names], *[deltas[n] for n in names], *[new_m[n] for n in names],
            *[new_v[n] for n in names])
```

```python
import functools
import math
from typing import NamedTuple

import jax
import jax.numpy as jnp
from jax import lax
from jax.experimental import pallas as pl
from jax.experimental.pallas import tpu as pltpu

F32 = jnp.float32
BF16 = jnp.bfloat16
EPS = 1e-6
GRID_W = 64
HEADS = 8
QK_NOPE = 128
QK_ROPE = 64
QK_HEAD = QK_NOPE + QK_ROPE
HEAD_PAD = 256
V_HEAD = 128
Q_LORA = 256
KV_LORA = 128
ROPE_BASE = 10000.0
QK_SCALE = QK_HEAD ** -0.5
ADAM_LR, ADAM_B1, ADAM_B2, ADAM_EPS, ADAM_WD, ADAM_STEP = 0.001, 0.9, 0.999, 1e-08, 0.01, 10
N_CHIPS = 4
VMEM_LIMIT = 56 * 1024 * 1024
MESH = pl.DeviceIdType.MESH
NEG = -1e30


class Dims(NamedTuple):
    B: int
    N: int
    CTX: int
    D: int
    T: int
    tm: int


def _cparams(*sem):
    return pltpu.CompilerParams(dimension_semantics=sem if sem else None, vmem_limit_bytes=VMEM_LIMIT)


def _dot(a, b):
    return jnp.dot(a, b, preferred_element_type=F32)


def _dot_nt(a, b):
    return lax.dot_general(a, b, (((1,), (1,)), ((), ())), preferred_element_type=F32)


def _dot_tn(a, b):
    return lax.dot_general(a, b, (((0,), (0,)), ((), ())), preferred_element_type=F32)


def _rms(x, n):
    r = lax.rsqrt(jnp.sum(x * x, axis=-1, keepdims=True) * (1.0 / n) + EPS)
    return x * r, r


def _rms_bwd(dxh, xh, r, n):
    return r * (dxh - xh * (jnp.sum(dxh * xh, axis=-1, keepdims=True) * (1.0 / n)))


def _pre(h, g, shift, scale):
    xh, _ = _rms(h, h.shape[-1])
    return (xh * g) * (1.0 + scale) + shift


def _pre_bwd(dout, h, g, scale):
    d = h.shape[-1]
    xh, r = _rms(h, d)
    n = xh * g
    dshift = jnp.sum(dout, axis=0, keepdims=True)
    dscale = jnp.sum(dout * n, axis=0, keepdims=True)
    dn = dout * (1.0 + scale)
    dg = jnp.sum(dn * xh, axis=0, keepdims=True)
    dh = _rms_bwd(dn * g, xh, r, d)
    return dh, dshift, dscale, dg


def _write_part(part_ref, dshift=None, dscale=None, dgate=None, dg=None):
    z = jnp.zeros((1, part_ref.shape[-1]), F32)
    part_ref[0, 0:1, :] = z if dshift is None else dshift
    part_ref[0, 1:2, :] = z if dscale is None else dscale
    part_ref[0, 2:3, :] = z if dgate is None else dgate
    part_ref[0, 3:4, :] = z if dg is None else dg
    part_ref[0, 4:8, :] = jnp.zeros((4, part_ref.shape[-1]), F32)


def _grp(dm):
    nb = dm.N // dm.tm
    return lambda i: jnp.minimum(i // nb, dm.B)


def _n_chunks(rows, row_bytes):
    n = 16
    while n > 1 and (rows % (16 * n) or (rows // n) * row_bytes < (256 << 10)):
        n //= 2
    return n


def _start_local(src, dst, sems, k0, nchunk):
    ch = src.shape[0] // nchunk
    copies = []
    for j in range(nchunk):
        cp = pltpu.make_async_copy(src.at[pl.ds(j * ch, ch)], dst.at[pl.ds(j * ch, ch)], sems.at[k0 + j])
        cp.start()
        copies.append(cp)
    return copies


def _exchange(arrs, axes, scatter, name, own="copy"):
    n = len(arrs)
    nbits = len(axes)
    slots = 2 ** nbits
    pats = list(range(1, slots))
    inplace = own == "inplace"
    nck = [_n_chunks(a.shape[-2], a.shape[-1] * a.dtype.itemsize) for a in arrs]
    base = [sum(nck[:i]) * len(pats) for i in range(n)]
    nsem = sum(nck) * len(pats)

    def body(*refs):
        ins, outs = refs[:n], refs[n:2 * n]
        send, recv, loc = refs[2 * n:]
        pos = {a: lax.axis_index(a) for a in ("x", "y", "c")}

        def slot_of(p):
            s = 0
            for a in axes:
                s = 2 * s + p[a]
            return s

        me = slot_of(pos)
        local = []
        for i in range(n):
            if own == "copy":
                local += _start_local(ins[i].at[me] if scatter else ins[i], outs[i].at[me], loc, sum(nck[:i]), nck[i])
        remote = []
        for pi, pat in enumerate(pats):
            peer = dict(pos)
            for bi, a in enumerate(axes):
                if (pat >> (nbits - 1 - bi)) & 1:
                    peer[a] = 1 - pos[a]
            them = slot_of(peer)
            for i in range(n):
                ch = arrs[i].shape[-2] // nck[i]
                for j in range(nck[i]):
                    k = base[i] + pi * nck[i] + j
                    rs = pl.ds(j * ch, ch)
                    if inplace:
                        src = outs[i].at[me, rs]
                    else:
                        src = ins[i].at[them, rs] if scatter else ins[i].at[rs]
                    cp = pltpu.make_async_remote_copy(
                        src_ref=src, dst_ref=outs[i].at[me, rs], send_sem=send.at[k], recv_sem=recv.at[k],
                        device_id=(peer["x"], peer["y"], peer["c"]), device_id_type=MESH)
                    cp.start()
                    remote.append(cp)
        for cp in local:
            cp.wait()
        for cp in remote:
            cp.wait()

    out_shape = [jax.ShapeDtypeStruct(a.shape if (scatter or inplace) else (slots,) + a.shape, a.dtype) for a in arrs]
    any_spec = pl.BlockSpec(memory_space=pl.ANY)
    outs = pl.pallas_call(
        body, name=name, out_shape=out_shape, in_specs=[any_spec] * n, out_specs=[any_spec] * n,
        scratch_shapes=[pltpu.SemaphoreType.DMA((nsem,)), pltpu.SemaphoreType.DMA((nsem,)), pltpu.SemaphoreType.DMA((sum(nck),))],
        input_output_aliases={i: i for i in range(n)} if inplace else {},
        compiler_params=pltpu.CompilerParams(has_side_effects=True),
    )(*arrs)
    return list(outs)


class Rider(NamedTuple):
    ins: list
    out_shapes: list
    aliases: dict
    sems: list
    start: object
    mid: object
    end: object


MID_STEPS = 6


def _hosted(body, rider, *, name, grid, in_specs, out_specs, out_shape, scratch_shapes, sem, args):
    if rider is None:
        outs = pl.pallas_call(body, name=name, grid=grid, in_specs=in_specs, out_specs=out_specs, out_shape=out_shape,
                              scratch_shapes=scratch_shapes, compiler_params=_cparams(*sem))(*args)
        return outs, []
    n_in, n_out, n_s = len(in_specs), len(out_specs), len(scratch_shapes)
    nri, nro = len(rider.ins), len(rider.out_shapes)
    nsteps = math.prod(grid)

    def wrapped(*refs):
        bounds = [0, n_in, n_in + nri, n_in + nri + n_out, n_in + nri + n_out + nro, n_in + nri + n_out + nro + n_s, len(refs)]
        ins, rins, outs, routs, scr, sems = (refs[lo:hi] for lo, hi in zip(bounds[:-1], bounds[1:]))
        step = 0
        for ax, extent in enumerate(grid):
            step = step * extent + pl.program_id(ax)

        @pl.when(step == 0)
        def _():
            rider.start(rins, routs, sems)

        body(*ins, *outs, *scr)

        if rider.mid is not None:
            @pl.when(step == max(nsteps - 1 - MID_STEPS, 0))
            def _():
                rider.mid(rins, routs, sems)

        @pl.when(step == nsteps - 1)
        def _():
            rider.end(rins, routs, sems)

    any_spec = pl.BlockSpec(memory_space=pl.ANY)
    outs = pl.pallas_call(
        wrapped, name=name, grid=grid, in_specs=list(in_specs) + [any_spec] * nri, out_specs=list(out_specs) + [any_spec] * nro,
        out_shape=list(out_shape) + list(rider.out_shapes), scratch_shapes=list(scratch_shapes) + list(rider.sems),
        input_output_aliases={n_in + i: n_out + o for i, o in rider.aliases.items()},
        compiler_params=pltpu.CompilerParams(dimension_semantics=("arbitrary",) * len(grid), vmem_limit_bytes=VMEM_LIMIT,
                                             has_side_effects=True),
    )(*args, *rider.ins)
    return outs[:n_out], list(outs[n_out:])


def _gather_rider(bufs):
    n = len(bufs)
    halves = [a.shape[1] // 2 for a in bufs]
    nck = [_n_chunks(h, a.shape[2] * a.dtype.itemsize) for h, a in zip(halves, bufs)]
    base = [3 * sum(nck[:i]) for i in range(n)]
    nsem = 3 * sum(nck)

    def plan():
        x, y, c = lax.axis_index("x"), lax.axis_index("y"), lax.axis_index("c")
        pieces = []
        for pi, (px, py) in enumerate([(x, 1 - y), (1 - x, y), (1 - x, 1 - y)]):
            for i in range(n):
                ch = halves[i] // nck[i]
                for j in range(nck[i]):
                    pieces.append((base[i] + pi * nck[i] + j, px, py, 2 * px + py, i, j * ch, ch))
        return x, y, c, 2 * x + y, pieces

    def rows(i, off, ch, core):
        return pl.ds(pl.multiple_of(core * halves[i] + off, 16), ch)

    def over_ici(outs, sems, c, slot, k, px, py, i, off, ch):
        ref = outs[i].at[slot, rows(i, off, ch, c)]
        return pltpu.make_async_remote_copy(src_ref=ref, dst_ref=ref, send_sem=sems[0].at[k], recv_sem=sems[1].at[k],
                                            device_id=(px, py, c), device_id_type=MESH)

    def over_d2d(outs, sems, x, y, c, slot, k, i, off, ch, core):
        ref = outs[i].at[slot, rows(i, off, ch, core)]
        return pltpu.make_async_remote_copy(src_ref=ref, dst_ref=ref, send_sem=sems[2].at[k], recv_sem=sems[3].at[k],
                                            device_id=(x, y, 1 - c), device_id_type=MESH)

    def start(ins, outs, sems):
        x, y, c, me, pieces = plan()
        for k, px, py, them, i, off, ch in pieces:
            over_ici(outs, sems, c, me, k, px, py, i, off, ch).start()

    def mid(ins, outs, sems):
        x, y, c, me, pieces = plan()
        for k, px, py, them, i, off, ch in pieces:
            over_ici(outs, sems, c, them, k, px, py, i, off, ch).wait_recv()
            over_d2d(outs, sems, x, y, c, them, k, i, off, ch, c).start()

    def end(ins, outs, sems):
        x, y, c, me, pieces = plan()
        for k, px, py, them, i, off, ch in pieces:
            over_ici(outs, sems, c, me, k, px, py, i, off, ch).wait_send()
            over_d2d(outs, sems, x, y, c, them, k, i, off, ch, c).wait_send()
        for k, px, py, them, i, off, ch in pieces:
            over_d2d(outs, sems, x, y, c, them, k, i, off, ch, 1 - c).wait_recv()

    return Rider(ins=list(bufs), out_shapes=[jax.ShapeDtypeStruct(a.shape, a.dtype) for a in bufs],
                 aliases={i: i for i in range(n)}, sems=[pltpu.SemaphoreType.DMA((nsem,))] * 4, start=start, mid=mid, end=end)


def _scatter_rider(srcs):
    n = len(srcs)
    nck = [_n_chunks(a.shape[1], a.shape[2] * a.dtype.itemsize) for a in srcs]
    base = [3 * sum(nck[:i]) for i in range(n)]
    nsem = 3 * sum(nck)

    def copies(ins, outs, sems):
        x, y, c = lax.axis_index("x"), lax.axis_index("y"), lax.axis_index("c")
        me = 2 * x + y
        for pi, (px, py) in enumerate([(x, 1 - y), (1 - x, y), (1 - x, 1 - y)]):
            for i in range(n):
                ch = srcs[i].shape[1] // nck[i]
                for j in range(nck[i]):
                    k = base[i] + pi * nck[i] + j
                    rs = pl.ds(j * ch, ch)
                    yield pltpu.make_async_remote_copy(
                        src_ref=ins[i].at[2 * px + py, rs], dst_ref=outs[i].at[me, rs], send_sem=sems[0].at[k],
                        recv_sem=sems[1].at[k], device_id=(px, py, c), device_id_type=MESH)

    def start(ins, outs, sems):
        for cp in copies(ins, outs, sems):
            cp.start()

    def end(ins, outs, sems):
        for cp in copies(ins, outs, sems):
            cp.wait()

    return Rider(ins=list(srcs), out_shapes=[jax.ShapeDtypeStruct(a.shape, a.dtype) for a in srcs], aliases={},
                 sems=[pltpu.SemaphoreType.DMA((nsem,))] * 2, start=start, mid=None, end=end)


def _ride_alone(rider, name):
    n_in, n_out = len(rider.ins), len(rider.out_shapes)

    def body(*refs):
        ins, outs, sems = refs[:n_in], refs[n_in:n_in + n_out], refs[n_in + n_out:]
        rider.start(ins, outs, sems)
        if rider.mid is not None:
            rider.mid(ins, outs, sems)
        rider.end(ins, outs, sems)

    any_spec = pl.BlockSpec(memory_space=pl.ANY)
    outs = pl.pallas_call(
        body, name=name, out_shape=list(rider.out_shapes), in_specs=[any_spec] * n_in, out_specs=[any_spec] * n_out,
        scratch_shapes=list(rider.sems), input_output_aliases=dict(rider.aliases),
        compiler_params=pltpu.CompilerParams(has_side_effects=True),
    )(*rider.ins)
    return list(outs)


def _swap_cores_inplace(bufs, name):
    n = len(bufs)
    nck = [_n_chunks(a.shape[2], a.shape[3] * a.dtype.itemsize) for a in bufs]
    base = [sum(a.shape[0] * k for a, k in zip(bufs[:i], nck[:i])) for i in range(n)]
    nsem = sum(a.shape[0] * k for a, k in zip(bufs, nck))

    def body(*refs):
        outs = refs[n:2 * n]
        send, recv = refs[2 * n:]
        x, y, c = lax.axis_index("x"), lax.axis_index("y"), lax.axis_index("c")

        def copies(core):
            for i in range(n):
                ch = bufs[i].shape[2] // nck[i]
                for p in range(bufs[i].shape[0]):
                    for j in range(nck[i]):
                        k = base[i] + p * nck[i] + j
                        ref = outs[i].at[p, core, pl.ds(j * ch, ch)]
                        yield pltpu.make_async_remote_copy(src_ref=ref, dst_ref=ref, send_sem=send.at[k], recv_sem=recv.at[k],
                                                           device_id=(x, y, 1 - c), device_id_type=MESH)

        for cp in copies(c):
            cp.start()
        for cp in copies(c):
            cp.wait_send()
        for cp in copies(1 - c):
            cp.wait_recv()

    any_spec = pl.BlockSpec(memory_space=pl.ANY)
    outs = pl.pallas_call(
        body, name=name, out_shape=[jax.ShapeDtypeStruct(a.shape, a.dtype) for a in bufs], in_specs=[any_spec] * n,
        out_specs=[any_spec] * n, scratch_shapes=[pltpu.SemaphoreType.DMA((nsem,))] * 2,
        input_output_aliases={i: i for i in range(n)}, compiler_params=pltpu.CompilerParams(has_side_effects=True),
    )(*bufs)
    return list(outs)


def _place_cast(w, row0, rows, slot, slots, name):
    C = w.shape[1]
    tr = _row_block(rows, C)
    blk0 = row0 // tr

    def body(slot_ref, w_ref, o_ref):
        del slot_ref
        o_ref[...] = w_ref[...].astype(BF16)

    return pl.pallas_call(
        body, name=name,
        grid_spec=pltpu.PrefetchScalarGridSpec(
            num_scalar_prefetch=1, grid=(rows // tr,), in_specs=[pl.BlockSpec((tr, C), lambda i, sr: (blk0 + i, 0))],
            out_specs=pl.BlockSpec((None, tr, C), lambda i, sr: (sr[0], i, 0))),
        out_shape=jax.ShapeDtypeStruct((slots, rows, C), BF16),
        compiler_params=_cparams("parallel"),
    )(slot.reshape(1).astype(jnp.int32), w)


def _swap_halves(arrs, name):
    n = len(arrs)
    S = arrs[0].shape[0]
    halves = [a.shape[1] // 2 for a in arrs]
    nck = [_n_chunks(h, a.shape[2] * a.dtype.itemsize) for h, a in zip(halves, arrs)]
    base = [S * sum(nck[:i]) for i in range(n)]
    nsem = S * sum(nck)

    def body(*refs):
        ins, outs = refs[:n], refs[n:2 * n]
        send, recv = refs[2 * n:]
        x, y, c = lax.axis_index("x"), lax.axis_index("y"), lax.axis_index("c")
        copies = []
        for i in range(n):
            ch = halves[i] // nck[i]
            for s in range(S):
                for j in range(nck[i]):
                    k = base[i] + s * nck[i] + j
                    src = ins[i].at[s, pl.ds(pl.multiple_of((1 - c) * halves[i] + j * ch, 16), ch)]
                    cp = pltpu.make_async_remote_copy(src_ref=src, dst_ref=outs[i].at[s, pl.ds(j * ch, ch)], send_sem=send.at[k],
                                                      recv_sem=recv.at[k], device_id=(x, y, 1 - c), device_id_type=MESH)
                    cp.start()
                    copies.append(cp)
        for cp in copies:
            cp.wait()

    any_spec = pl.BlockSpec(memory_space=pl.ANY)
    outs = pl.pallas_call(
        body, name=name, out_shape=[jax.ShapeDtypeStruct((S, h, a.shape[2]), a.dtype) for h, a in zip(halves, arrs)],
        in_specs=[any_spec] * n, out_specs=[any_spec] * n,
        scratch_shapes=[pltpu.SemaphoreType.DMA((nsem,))] * 2,
        compiler_params=pltpu.CompilerParams(has_side_effects=True),
    )(*arrs)
    return list(outs)


def _pair_sum(gs, rs, core, name):
    n = len(gs)
    S, rows, C = gs[0].shape
    half = rows // 2
    tr = _row_block(half, C)
    nb = half // tr

    def body(core_ref, *refs):
        del core_ref
        for g_ref, r_ref, o_ref in zip(refs[:n], refs[n:2 * n], refs[2 * n:]):
            o_ref[...] = (g_ref[...].astype(F32) + r_ref[...].astype(F32)).astype(BF16)

    blk = pl.BlockSpec((None, tr, C), lambda s, i, cr: (s, i, 0))
    mine = pl.BlockSpec((None, tr, C), lambda s, i, cr: (s, cr[0] * nb + i, 0))
    return pl.pallas_call(
        body, name=name,
        grid_spec=pltpu.PrefetchScalarGridSpec(num_scalar_prefetch=1, grid=(S, nb), in_specs=[mine] * n + [blk] * n,
                                               out_specs=[blk] * n),
        out_shape=[jax.ShapeDtypeStruct((S, half, C), BF16)] * n,
        compiler_params=_cparams("parallel", "parallel"),
    )(core.reshape(1).astype(jnp.int32), *gs, *rs)


def _ffn_fwd(h, mod, g, w1, w3, w2, k, dm, name, rider=None):
    T, D = h.shape
    S, F = w1.shape[0], w1.shape[-2]
    tm = dm.tm
    r0 = 6 if k else 0
    grp = _grp(dm)

    def body(h_ref, mod_ref, g_ref, w1_ref, w3_ref, w2_ref, ho_ref, a_ref, b_ref, hn_ref, y_ref, hn_s, acc):
        s = pl.program_id(1)

        @pl.when(s == 0)
        def _():
            hn = _pre(h_ref[...], g_ref[...], mod_ref[0, r0:r0 + 1, :], mod_ref[0, r0 + 1:r0 + 2, :]).astype(BF16)
            hn_s[...] = hn
            hn_ref[...] = hn
            acc[...] = jnp.zeros_like(acc)

        hn = hn_s[...]
        a = _dot_nt(hn, w1_ref[...])
        b = _dot_nt(hn, w3_ref[...])
        a_ref[0] = a.astype(BF16)
        b_ref[0] = b.astype(BF16)
        sw = (a * jax.nn.sigmoid(a) * b).astype(BF16)
        acc[...] += _dot(sw, w2_ref[...])

        @pl.when(s == S - 1)
        def _():
            y = acc[...]
            y_ref[...] = y.astype(BF16)
            ho_ref[...] = h_ref[...] + 0.5 * mod_ref[0, r0 + 2:r0 + 3, :] * y

    row = pl.BlockSpec((tm, D), lambda i, s: (i, 0))
    wrow = pl.BlockSpec((None, None, F, D), lambda i, s: (s, k, 0, 0))
    ab = pl.BlockSpec((1, tm, F), lambda i, s: (s, i, 0))
    return _hosted(
        body, rider, name=name, grid=(T // tm, S),
        in_specs=[row, pl.BlockSpec((1, 9, D), lambda i, s: (grp(i), 0, 0)), pl.BlockSpec((1, D), lambda i, s: (0, 0)),
                  wrow, wrow, wrow],
        out_specs=[row, ab, ab, row, row],
        out_shape=[jax.ShapeDtypeStruct((T, D), F32), jax.ShapeDtypeStruct((S, T, F), BF16),
                   jax.ShapeDtypeStruct((S, T, F), BF16), jax.ShapeDtypeStruct((T, D), BF16),
                   jax.ShapeDtypeStruct((T, D), BF16)],
        scratch_shapes=[pltpu.VMEM((tm, D), BF16), pltpu.VMEM((tm, D), F32)],
        sem=("parallel", "arbitrary"), args=(h, mod, g, w1, w3, w2))


def _ffn_bwd(dh, h, mod, g, y, a, b, w1, w3, w2, k, dm, name, rider=None):
    T, D = h.shape
    S, F = w1.shape[0], w1.shape[-2]
    tm = dm.tm
    r0 = 6 if k else 0
    grp = _grp(dm)

    def body(dh_ref, h_ref, mod_ref, g_ref, y_ref, a_ref, b_ref, w1_ref, w3_ref, w2_ref,
             dho_ref, da_ref, db_ref, sw_ref, dy_ref, part_ref, dy_s, acc):
        s = pl.program_id(1)

        @pl.when(s == 0)
        def _():
            dy = (0.5 * mod_ref[0, r0 + 2:r0 + 3, :] * dh_ref[...]).astype(BF16)
            dy_s[...] = dy
            dy_ref[...] = dy
            acc[...] = jnp.zeros_like(acc)

        ds = _dot_nt(dy_s[...], w2_ref[...]).astype(BF16)
        av = a_ref[0]
        bv = b_ref[0]
        sig = jax.nn.sigmoid(av)
        sil = av * sig
        sw_ref[0] = sil * bv
        db = ds * sil
        da = ds * bv * (sig + sil * (1.0 - sig))
        da_ref[0] = da
        db_ref[0] = db
        acc[...] += _dot(da, w1_ref[...]) + _dot(db, w3_ref[...])

        @pl.when(s == S - 1)
        def _():
            dhv = dh_ref[...]
            dhb, dshift, dscale, dg = _pre_bwd(acc[...], h_ref[...], g_ref[...], mod_ref[0, r0 + 1:r0 + 2, :])
            dho_ref[...] = dhv + dhb
            dgate = 0.5 * jnp.sum(dhv * y_ref[...].astype(F32), axis=0, keepdims=True)
            _write_part(part_ref, dshift, dscale, dgate, dg)

    row = pl.BlockSpec((tm, D), lambda i, s: (i, 0))
    wrow = pl.BlockSpec((None, None, F, D), lambda i, s: (s, k, 0, 0))
    ab = pl.BlockSpec((1, tm, F), lambda i, s: (s, i, 0))
    stf = jax.ShapeDtypeStruct((S, T, F), BF16)
    return _hosted(
        body, rider, name=name, grid=(T // tm, S),
        in_specs=[row, row, pl.BlockSpec((1, 9, D), lambda i, s: (grp(i), 0, 0)), pl.BlockSpec((1, D), lambda i, s: (0, 0)),
                  row, ab, ab, wrow, wrow, wrow],
        out_specs=[row, ab, ab, ab, row, pl.BlockSpec((1, 8, D), lambda i, s: (i, 0, 0))],
        out_shape=[jax.ShapeDtypeStruct((T, D), F32), stf, stf, stf, jax.ShapeDtypeStruct((T, D), BF16),
                   jax.ShapeDtypeStruct((T // tm, 8, D), F32)],
        scratch_shapes=[pltpu.VMEM((tm, D), BF16), pltpu.VMEM((tm, D), F32)],
        sem=("parallel", "arbitrary"), args=(dh, h, mod, g, y, a, b, w1, w3, w2))


def _mm_tn(a, b, a_spec, b_spec, out_shape, out_spec, grid, name):
    nk = grid[-1]
    kax = len(grid) - 1
    blk = tuple(d for d in out_spec.block_shape if d is not None)

    def body(a_ref, b_ref, o_ref, acc):
        kk = pl.program_id(kax)

        @pl.when(kk == 0)
        def _():
            acc[...] = jnp.zeros_like(acc)

        acc[...] += _dot_tn(a_ref[...].astype(BF16), b_ref[...].astype(BF16))

        @pl.when(kk == nk - 1)
        def _():
            o_ref[...] = acc[...].astype(o_ref.dtype)

    return pl.pallas_call(
        body, name=name, grid=grid,
        in_specs=[a_spec, b_spec], out_specs=out_spec, out_shape=jax.ShapeDtypeStruct(out_shape, BF16),
        scratch_shapes=[pltpu.VMEM(blk, F32)],
        compiler_params=_cparams(*(["parallel"] * kax + ["arbitrary"])),
    )(a, b)


def _sc_in_fwd(h, mod, g, w_in, j, dm, name, rider=None):
    T, D = h.shape
    tm = dm.tm
    wq = D // N_CHIPS
    nq = 3 * N_CHIPS
    grp = _grp(dm)

    def body(h_ref, mod_ref, g_ref, w_ref, p_ref, hn_ref, hn_s):
        @pl.when(pl.program_id(1) == 0)
        def _():
            hn = _pre(h_ref[...], g_ref[...], mod_ref[0, 3:4, :], mod_ref[0, 4:5, :]).astype(BF16)
            hn_s[...] = hn
            hn_ref[...] = hn

        p_ref[...] = _dot(hn_s[...], w_ref[...])

    row = pl.BlockSpec((tm, D), lambda i, q: (i, 0))
    return _hosted(
        body, rider, name=name, grid=(T // tm, nq),
        in_specs=[row, pl.BlockSpec((1, 9, D), lambda i, q: (grp(i), 0, 0)), pl.BlockSpec((1, D), lambda i, q: (0, 0)),
                  pl.BlockSpec((None, None, D, wq), lambda i, q: (q // 3, j, 0, q % 3))],
        out_specs=[pl.BlockSpec((None, tm, wq), lambda i, q: (q // N_CHIPS, i, q % N_CHIPS)), row],
        out_shape=[jax.ShapeDtypeStruct((3, T, D), F32), jax.ShapeDtypeStruct((T, D), BF16)],
        scratch_shapes=[pltpu.VMEM((tm, D), BF16)],
        sem=("parallel", "arbitrary"), args=(h, mod, g, w_in))


def _conv_cols(dm):
    return 256 if dm.D % 256 == 0 else 128


def _seg_masks(r, dm):
    bn = dm.B * dm.N
    lat = r < bn
    off = jnp.where(lat, lax.rem(r, dm.N), lax.rem(r - bn, dm.CTX))
    seg = jnp.where(lat, dm.N, dm.CTX)
    inside = (r >= 0) & (r < dm.T)
    return ((off != 0) & inside).astype(F32), ((off != seg - 1) & inside).astype(F32)


def _conv_specs(dm):
    tb, cb, nr8 = dm.tm, _conv_cols(dm), dm.T // 8
    prev8 = lambda c, i: jnp.maximum(i * (tb // 8) - 1, 0)
    next8 = lambda c, i: jnp.minimum((i + 1) * (tb // 8), nr8 - 1)
    return dict(
        tb=tb, cb=cb,
        p=pl.BlockSpec((3, tb, cb), lambda c, i: (0, i, c)),
        p_prev=pl.BlockSpec((3, 8, cb), lambda c, i: (0, prev8(c, i), c)),
        p_next=pl.BlockSpec((3, 8, cb), lambda c, i: (0, next8(c, i), c)),
        row=pl.BlockSpec((tb, cb), lambda c, i: (i, c)),
        row_prev=pl.BlockSpec((8, cb), lambda c, i: (prev8(c, i), c)),
        row_next=pl.BlockSpec((8, cb), lambda c, i: (next8(c, i), c)),
        w=pl.BlockSpec((3, cb), lambda c, i: (0, c)),
    )


def _shift_rows(x, before, after, tb):
    rid = lax.broadcasted_iota(jnp.int32, x.shape, 0)
    down = jnp.where(rid == 0, before, pltpu.roll(x, 1, 0))
    up = jnp.where(rid == tb - 1, after, pltpu.roll(x, tb - 1, 0))
    return down, up


def _conv_fwd(p, wc, dm, name):
    T, D = dm.T, dm.D
    sp = _conv_specs(dm)
    tb, cb = sp["tb"], sp["cb"]

    def body(p_ref, pp_ref, pn_ref, w_ref, z_ref):
        r = pl.program_id(1) * tb + lax.broadcasted_iota(jnp.int32, (tb, cb), 0)
        mp, mn = _seg_masks(r, dm)
        cu = p_ref[1] * p_ref[2]
        prev, nxt = _shift_rows(cu, pp_ref[1, 7:8, :] * pp_ref[2, 7:8, :], pn_ref[1, 0:1, :] * pn_ref[2, 0:1, :], tb)
        conv = w_ref[0:1, :] * (prev * mp) + w_ref[1:2, :] * cu + w_ref[2:3, :] * (nxt * mn)
        z_ref[...] = (p_ref[0] * conv).astype(BF16)

    return pl.pallas_call(
        body, name=name, grid=(D // cb, T // tb),
        in_specs=[sp["p"], sp["p_prev"], sp["p_next"], sp["w"]], out_specs=sp["row"],
        out_shape=jax.ShapeDtypeStruct((T, D), BF16),
        compiler_params=_cparams("parallel", "parallel"),
    )(p, p, p, wc)


def _conv_bwd(dz, p, wc, dm, name):
    T, D = dm.T, dm.D
    sp = _conv_specs(dm)
    tb, cb = sp["tb"], sp["cb"]

    def body(dz_ref, dzp_ref, dzn_ref, p_ref, pp_ref, pn_ref, w_ref, dp_ref, dw_ref):
        i = pl.program_id(1)
        r = i * tb + lax.broadcasted_iota(jnp.int32, (tb, cb), 0)
        mp, mn = _seg_masks(r, dm)
        rb = i * tb + lax.broadcasted_iota(jnp.int32, (1, cb), 0)
        _, mn_before = _seg_masks(rb - 1, dm)
        mp_after, _ = _seg_masks(rb + tb, dm)
        bg, cg, u = p_ref[0], p_ref[1], p_ref[2]
        cu = cg * u
        prev, nxt = _shift_rows(cu, pp_ref[1, 7:8, :] * pp_ref[2, 7:8, :], pn_ref[1, 0:1, :] * pn_ref[2, 0:1, :], tb)
        prev = prev * mp
        nxt = nxt * mn
        w0, w1, w2 = w_ref[0:1, :], w_ref[1:2, :], w_ref[2:3, :]
        conv = w0 * prev + w1 * cu + w2 * nxt
        dz = dz_ref[...]
        dp_ref[0] = dz * conv
        dconv = dz * bg

        @pl.when(i == 0)
        def _():
            dw_ref[...] = jnp.zeros_like(dw_ref)

        dw_ref[0:1, :] += jnp.sum(dconv * prev, axis=0, keepdims=True)
        dw_ref[1:2, :] += jnp.sum(dconv * cu, axis=0, keepdims=True)
        dw_ref[2:3, :] += jnp.sum(dconv * nxt, axis=0, keepdims=True)
        dconv_before = dzp_ref[7:8, :] * pp_ref[0, 7:8, :] * mn_before
        dconv_after = dzn_ref[0:1, :] * pn_ref[0, 0:1, :] * mp_after
        from_prev, _ = _shift_rows(dconv * mn, dconv_before, dconv_after, tb)
        _, from_next = _shift_rows(dconv * mp, dconv_before, dconv_after, tb)
        dcu = w1 * dconv + w0 * from_next + w2 * from_prev
        dp_ref[1] = dcu * u
        dp_ref[2] = dcu * cg

    return pl.pallas_call(
        body, name=name, grid=(D // cb, T // tb),
        in_specs=[sp["row"], sp["row_prev"], sp["row_next"], sp["p"], sp["p_prev"], sp["p_next"], sp["w"]],
        out_specs=[sp["p"], sp["w"]],
        out_shape=[jax.ShapeDtypeStruct((3, T, D), F32), jax.ShapeDtypeStruct((3, D), F32)],
        compiler_params=_cparams("parallel", "arbitrary"),
    )(dz, dz, dz, p, p, p, wc)


def _out_fwd(z, w, h, mod, j, dm, name):
    T, D = h.shape
    K = z.shape[1]
    tm = dm.tm
    grp = _grp(dm)

    def body(z_ref, w_ref, h_ref, mod_ref, ho_ref, y_ref):
        y = _dot(z_ref[...], w_ref[...])
        y_ref[...] = y.astype(BF16)
        ho_ref[...] = h_ref[...] + mod_ref[0, 5:6, :] * y

    row = pl.BlockSpec((tm, D), lambda i: (i, 0))
    return pl.pallas_call(
        body, name=name, grid=(T // tm,),
        in_specs=[pl.BlockSpec((tm, K), lambda i: (i, 0)), pl.BlockSpec((None, K, D), lambda i: (j, 0, 0)), row,
                  pl.BlockSpec((1, 9, D), lambda i: (grp(i), 0, 0))],
        out_specs=[row, row],
        out_shape=[jax.ShapeDtypeStruct((T, D), F32), jax.ShapeDtypeStruct((T, D), BF16)],
        compiler_params=_cparams("parallel"),
    )(z, w, h, mod)


def _out_bwd(dh, y, w, mod, j, dm, name):
    T, D = dh.shape
    K = w.shape[1]
    tm = dm.tm
    grp = _grp(dm)

    def body(dh_ref, y_ref, w_ref, mod_ref, dy_ref, dz_ref, part_ref):
        dhv = dh_ref[...]
        dy = (mod_ref[0, 5:6, :] * dhv).astype(BF16)
        dy_ref[...] = dy
        dz_ref[...] = _dot_nt(dy, w_ref[...])
        _write_part(part_ref, dgate=jnp.sum(dhv * y_ref[...].astype(F32), axis=0, keepdims=True))

    row = pl.BlockSpec((tm, D), lambda i: (i, 0))
    return pl.pallas_call(
        body, name=name, grid=(T // tm,),
        in_specs=[row, row, pl.BlockSpec((None, K, D), lambda i: (j, 0, 0)), pl.BlockSpec((1, 9, D), lambda i: (grp(i), 0, 0))],
        out_specs=[row, pl.BlockSpec((tm, K), lambda i: (i, 0)), pl.BlockSpec((1, 8, D), lambda i: (i, 0, 0))],
        out_shape=[jax.ShapeDtypeStruct((T, D), BF16), jax.ShapeDtypeStruct((T, K), F32),
                   jax.ShapeDtypeStruct((T // tm, 8, D), F32)],
        compiler_params=_cparams("parallel"),
    )(dh, y, w, mod)


def _sc_in_bwd(dh, dp, h, mod, g, w_in, j, dm, name):
    T, D = h.shape
    tm = dm.tm
    wq = D // N_CHIPS
    nq = 3 * N_CHIPS
    grp = _grp(dm)

    def body(dh_ref, dp_ref, h_ref, mod_ref, g_ref, w_ref, dho_ref, part_ref, acc):
        q = pl.program_id(1)

        @pl.when(q == 0)
        def _():
            acc[...] = jnp.zeros_like(acc)

        acc[...] += _dot_nt(dp_ref[...].astype(BF16), w_ref[...])

        @pl.when(q == nq - 1)
        def _():
            dhb, dshift, dscale, dg = _pre_bwd(acc[...], h_ref[...], g_ref[...], mod_ref[0, 4:5, :])
            dho_ref[...] = dh_ref[...] + dhb
            _write_part(part_ref, dshift, dscale, None, dg)

    row = pl.BlockSpec((tm, D), lambda i, q: (i, 0))
    return pl.pallas_call(
        body, name=name, grid=(T // tm, nq),
        in_specs=[row, pl.BlockSpec((None, tm, wq), lambda i, q: (q // N_CHIPS, i, q % N_CHIPS)), row,
                  pl.BlockSpec((1, 9, D), lambda i, q: (grp(i), 0, 0)), pl.BlockSpec((1, D), lambda i, q: (0, 0)),
                  pl.BlockSpec((None, None, D, wq), lambda i, q: (q // 3, j, 0, q % 3))],
        out_specs=[row, pl.BlockSpec((1, 8, D), lambda i, q: (i, 0, 0))],
        out_shape=[jax.ShapeDtypeStruct((T, D), F32), jax.ShapeDtypeStruct((T // tm, 8, D), F32)],
        scratch_shapes=[pltpu.VMEM((tm, D), F32)],
        compiler_params=_cparams("parallel", "arbitrary"),
    )(dh, dp, h, mod, g, w_in)


def _rope(t, c, s1, s2):
    return t * c + pltpu.roll(t, HEAD_PAD - 16, 1) * s1 + pltpu.roll(t, 16, 1) * s2


def _rope_t(dy, c, s1, s2):
    return dy * c + pltpu.roll(dy * s1, 16, 1) + pltpu.roll(dy * s2, HEAD_PAD - 16, 1)


def _mla_heads_fwd(z, g_ref, wuq_ref, wukv_ref):
    cq, ckv, krp = z[:, :Q_LORA], z[:, Q_LORA:Q_LORA + KV_LORA], z[:, Q_LORA + KV_LORA:]
    cqh, rq = _rms(cq, Q_LORA)
    ckvh, rkv = _rms(ckv, KV_LORA)
    cqn = (cqh * g_ref[0:1, :]).astype(BF16)
    ckvn = (ckvh * g_ref[1:2, :KV_LORA]).astype(BF16)
    qraw = _dot(cqn, wuq_ref[...])
    kvraw = _dot(ckvn, wukv_ref[...])
    return dict(krp=krp, cqh=cqh, rq=rq, ckvh=ckvh, rkv=rkv, cqn=cqn, ckvn=ckvn, qraw=qraw, kvraw=kvraw)


def _mla_proj_fwd(h, mod, g, gains, tabs, w_a, w_uq, w_ukv, j, dm, name):
    T, D = h.shape
    tm = min(dm.tm, 256)
    grp = lambda i: jnp.minimum(i // (dm.N // tm), dm.B)
    HP = HEAD_PAD

    def body(h_ref, mod_ref, g_ref, gn_ref, tab_ref, wa_ref, wuq_ref, wukv_ref, hn_ref, q_ref, k_ref, v_ref):
        hn = _pre(h_ref[...], g_ref[...], mod_ref[0, 3:4, :], mod_ref[0, 4:5, :]).astype(BF16)
        hn_ref[...] = hn
        f = _mla_heads_fwd(_dot(hn, wa_ref[...]), gn_ref, wuq_ref, wukv_ref)
        c, s1, s2 = tab_ref[0], tab_ref[1], tab_ref[2]
        for hd in range(HEADS):
            qh, _ = _rms(f["qraw"][:, hd * HP:(hd + 1) * HP], QK_HEAD)
            q_ref[:, hd * HP:(hd + 1) * HP] = (_rope(qh * gn_ref[2:3, :], c, s1, s2) * QK_SCALE).astype(BF16)
            kpre = jnp.concatenate([f["kvraw"][:, hd * HP:hd * HP + QK_NOPE], f["krp"]], axis=1)
            kh, _ = _rms(kpre, QK_HEAD)
            k_ref[:, hd * HP:(hd + 1) * HP] = _rope(kh * gn_ref[3:4, :], c, s1, s2).astype(BF16)
            v_ref[:, hd * V_HEAD:(hd + 1) * V_HEAD] = f["kvraw"][:, hd * HP + QK_NOPE:(hd + 1) * HP].astype(BF16)

    row = pl.BlockSpec((tm, D), lambda i: (i, 0))
    HQ = HEADS * HP
    return pl.pallas_call(
        body, name=name, grid=(T // tm,),
        in_specs=[row, pl.BlockSpec((1, 9, D), lambda i: (grp(i), 0, 0)), pl.BlockSpec((1, D), lambda i: (0, 0)),
                  pl.BlockSpec((None, 8, HP), lambda i: (j, 0, 0)), pl.BlockSpec((3, tm, HP), lambda i: (0, i, 0)),
                  pl.BlockSpec((None, D, 512), lambda i: (j, 0, 0)), pl.BlockSpec((None, Q_LORA, HQ), lambda i: (j, 0, 0)),
                  pl.BlockSpec((None, KV_LORA, HQ), lambda i: (j, 0, 0))],
        out_specs=[row, pl.BlockSpec((tm, HQ), lambda i: (i, 0)), pl.BlockSpec((tm, HQ), lambda i: (i, 0)),
                   pl.BlockSpec((tm, HEADS * V_HEAD), lambda i: (i, 0))],
        out_shape=[jax.ShapeDtypeStruct((T, D), BF16), jax.ShapeDtypeStruct((T, HQ), BF16),
                   jax.ShapeDtypeStruct((T, HQ), BF16), jax.ShapeDtypeStruct((T, HEADS * V_HEAD), BF16)],
        compiler_params=_cparams("parallel"),
    )(h, mod, g, gains, tabs, w_a, w_uq, w_ukv)


def _mla_proj_bwd(dh, dq, dkl, dkc, dvl, dvc, h, mod, g, gains, tabs, w_a, w_uq, w_ukv, j, dm, name):
    T, D = h.shape
    tm = min(dm.tm, 256)
    nblk = T // tm
    grp = lambda i: jnp.minimum(i // (dm.N // tm), dm.B)
    HP = HEAD_PAD
    HQ = HEADS * HP

    nlat = dm.B * dm.N // tm

    def body(dh_ref, dq_ref, dkl_ref, dkc_ref, dvl_ref, dvc_ref, h_ref, mod_ref, g_ref, gn_ref, tab_ref, wa_ref, wuq_ref, wukv_ref,
             dho_ref, part_ref, gwa_ref, gwuq_ref, gwukv_ref, dgn_ref, dqraw_s, dkvraw_s):
        i = pl.program_id(0)
        pick = lambda lat_ref, ctx_ref, cols: jnp.where(i < nlat, lat_ref[:, cols], ctx_ref[:, cols])

        @pl.when(i == 0)
        def _():
            gwa_ref[...] = jnp.zeros_like(gwa_ref)
            gwuq_ref[...] = jnp.zeros_like(gwuq_ref)
            gwukv_ref[...] = jnp.zeros_like(gwukv_ref)
            dgn_ref[...] = jnp.zeros_like(dgn_ref)

        hv = h_ref[...]
        hn = _pre(hv, g_ref[...], mod_ref[0, 3:4, :], mod_ref[0, 4:5, :]).astype(BF16)
        f = _mla_heads_fwd(_dot(hn, wa_ref[...]), gn_ref, wuq_ref, wukv_ref)
        c, s1, s2 = tab_ref[0], tab_ref[1], tab_ref[2]
        gq, gk = gn_ref[2:3, :], gn_ref[3:4, :]
        dgq = jnp.zeros((1, HP), F32)
        dgk = jnp.zeros((1, HP), F32)
        dkrp = jnp.zeros((tm, HP - QK_NOPE), F32)
        for hd in range(HEADS):
            qh, rq = _rms(f["qraw"][:, hd * HP:(hd + 1) * HP], QK_HEAD)
            dqn = _rope_t(dq_ref[:, hd * HP:(hd + 1) * HP] * QK_SCALE, c, s1, s2)
            dgq = dgq + jnp.sum(dqn * qh, axis=0, keepdims=True)
            dqraw_s[:, hd * HP:(hd + 1) * HP] = _rms_bwd(dqn * gq, qh, rq, QK_HEAD)
            kpre = jnp.concatenate([f["kvraw"][:, hd * HP:hd * HP + QK_NOPE], f["krp"]], axis=1)
            kh, rk = _rms(kpre, QK_HEAD)
            dkn = _rope_t(pick(dkl_ref, dkc_ref, slice(hd * HP, (hd + 1) * HP)), c, s1, s2)
            dgk = dgk + jnp.sum(dkn * kh, axis=0, keepdims=True)
            dkpre = _rms_bwd(dkn * gk, kh, rk, QK_HEAD)
            dkvraw_s[:, hd * HP:hd * HP + QK_NOPE] = dkpre[:, :QK_NOPE]
            dkrp = dkrp + dkpre[:, QK_NOPE:]
            dkvraw_s[:, hd * HP + QK_NOPE:(hd + 1) * HP] = pick(dvl_ref, dvc_ref, slice(hd * V_HEAD, (hd + 1) * V_HEAD))
        dqraw = dqraw_s[...].astype(BF16)
        dkvraw = dkvraw_s[...].astype(BF16)
        gwuq_ref[...] += _dot_tn(f["cqn"], dqraw)
        gwukv_ref[...] += _dot_tn(f["ckvn"], dkvraw)
        dcqn = _dot_nt(dqraw, wuq_ref[...])
        dckvn = _dot_nt(dkvraw, wukv_ref[...])
        dgqa = jnp.sum(dcqn * f["cqh"], axis=0, keepdims=True)
        dgkva = jnp.sum(dckvn * f["ckvh"], axis=0, keepdims=True)
        dcq = _rms_bwd(dcqn * gn_ref[0:1, :], f["cqh"], f["rq"], Q_LORA)
        dckv = _rms_bwd(dckvn * gn_ref[1:2, :KV_LORA], f["ckvh"], f["rkv"], KV_LORA)
        dz = jnp.concatenate([dcq, dckv, dkrp], axis=1).astype(BF16)
        gwa_ref[...] += _dot_tn(hn, dz)
        dhn = _dot_nt(dz, wa_ref[...])
        dhb, dshift, dscale, dg = _pre_bwd(dhn, hv, g_ref[...], mod_ref[0, 4:5, :])
        dho_ref[...] = dh_ref[...] + dhb
        _write_part(part_ref, dshift, dscale, None, dg)
        dgn_ref[0:1, :] += dgqa
        dgn_ref[1:2, :KV_LORA] += dgkva
        dgn_ref[2:3, :] += dgq
        dgn_ref[3:4, :] += dgk

    row = pl.BlockSpec((tm, D), lambda i: (i, 0))
    wide = pl.BlockSpec((tm, HQ), lambda i: (i, 0))
    const2 = lambda i: (0, 0)
    return pl.pallas_call(
        body, name=name, grid=(nblk,),
        in_specs=[row, wide, pl.BlockSpec((tm, HQ), lambda i: (jnp.minimum(i, nlat - 1), 0)),
                  pl.BlockSpec((tm, HQ), lambda i: (jnp.maximum(i - nlat, 0), 0)),
                  pl.BlockSpec((tm, HEADS * V_HEAD), lambda i: (jnp.minimum(i, nlat - 1), 0)),
                  pl.BlockSpec((tm, HEADS * V_HEAD), lambda i: (jnp.maximum(i - nlat, 0), 0)), row,
                  pl.BlockSpec((1, 9, D), lambda i: (grp(i), 0, 0)), pl.BlockSpec((1, D), const2),
                  pl.BlockSpec((None, 8, HP), lambda i: (j, 0, 0)), pl.BlockSpec((3, tm, HP), lambda i: (0, i, 0)),
                  pl.BlockSpec((None, D, 512), lambda i: (j, 0, 0)), pl.BlockSpec((None, Q_LORA, HQ), lambda i: (j, 0, 0)),
                  pl.BlockSpec((None, KV_LORA, HQ), lambda i: (j, 0, 0))],
        out_specs=[row, pl.BlockSpec((1, 8, D), lambda i: (i, 0, 0)), pl.BlockSpec((D, 512), const2),
                   pl.BlockSpec((Q_LORA, HQ), const2), pl.BlockSpec((KV_LORA, HQ), const2), pl.BlockSpec((8, HP), const2)],
        out_shape=[jax.ShapeDtypeStruct((T, D), F32), jax.ShapeDtypeStruct((nblk, 8, D), F32),
                   jax.ShapeDtypeStruct((D, 512), F32), jax.ShapeDtypeStruct((Q_LORA, HQ), F32),
                   jax.ShapeDtypeStruct((KV_LORA, HQ), F32), jax.ShapeDtypeStruct((8, HP), F32)],
        scratch_shapes=[pltpu.VMEM((tm, HQ), F32), pltpu.VMEM((tm, HQ), F32)],
        compiler_params=_cparams("arbitrary"),
    )(dh, dq, dkl, dkc, dvl, dvc, h, mod, g, gains, tabs, w_a, w_uq, w_ukv)


def _attn_specs(dm):
    tq = dm.CTX
    nq = dm.N // tq
    cblk0 = dm.B * nq
    HP = HEAD_PAD
    qrow = lambda b, i: jnp.where(i < nq, b * nq + i, cblk0 + b)
    return dict(
        tq=tq, nq=nq,
        q=pl.BlockSpec((tq, HP), lambda b, hd, i: (qrow(b, i), hd)),
        k_lat=pl.BlockSpec((dm.N, HP), lambda b, hd, i: (b, hd)),
        k_ctx=pl.BlockSpec((tq, HP), lambda b, hd, i: (cblk0 + b, hd)),
        v_lat=pl.BlockSpec((dm.N, V_HEAD), lambda b, hd, i: (b, hd)),
        v_ctx=pl.BlockSpec((tq, V_HEAD), lambda b, hd, i: (cblk0 + b, hd)),
        o=pl.BlockSpec((tq, V_HEAD), lambda b, hd, i: (qrow(b, i), hd)),
    )


def _attn_exp(q, keys, first_off=None):
    s = [_dot_nt(q, kk) for kk in keys]
    if first_off is not None:
        s[0] = s[0] + first_off
    m = functools.reduce(jnp.maximum, [jnp.max(x, axis=-1, keepdims=True) for x in s])
    e = [jnp.exp(x - m) for x in s]
    return e, 1.0 / sum(jnp.sum(x, axis=-1, keepdims=True) for x in e)


def _attn_fwd(q, k, v, dm, name, rider=None):
    T = dm.T
    sp = _attn_specs(dm)
    nq = sp["nq"]

    def body(q_ref, kl_ref, kc_ref, vl_ref, vc_ref, o_ref):
        i = pl.program_id(2)

        @pl.when(i < nq)
        def _():
            (el, ec), inv = _attn_exp(q_ref[...], [kl_ref[...], kc_ref[...]])
            o_ref[...] = ((_dot(el.astype(BF16), vl_ref[...]) + _dot(ec.astype(BF16), vc_ref[...])) * inv).astype(BF16)

        @pl.when(i == nq)
        def _():
            (ec,), inv = _attn_exp(q_ref[...], [kc_ref[...]])
            o_ref[...] = (_dot(ec.astype(BF16), vc_ref[...]) * inv).astype(BF16)

    (o,), got = _hosted(
        body, rider, name=name, grid=(dm.B, HEADS, nq + 1),
        in_specs=[sp["q"], sp["k_lat"], sp["k_ctx"], sp["v_lat"], sp["v_ctx"]], out_specs=[sp["o"]],
        out_shape=[jax.ShapeDtypeStruct((T, HEADS * V_HEAD), BF16)], scratch_shapes=[],
        sem=("parallel", "parallel", "arbitrary"), args=(q, k, k, v, v))
    return o, got


def _attn_bwd(q, k, v, o, do, dm, name):
    T = dm.T
    sp = _attn_specs(dm)
    nq, tq = sp["nq"], sp["tq"]
    HP, HQ, HV = HEAD_PAD, HEADS * HEAD_PAD, HEADS * V_HEAD

    def body(q_ref, kl_ref, kc_ref, vl_ref, vc_ref, o_ref, do_ref, dq_ref, dkl_ref, dkc_ref, dvl_ref, dvc_ref):
        i = pl.program_id(2)

        @pl.when(i == 0)
        def _():
            dkl_ref[...] = jnp.zeros_like(dkl_ref)
            dkc_ref[...] = jnp.zeros_like(dkc_ref)
            dvl_ref[...] = jnp.zeros_like(dvl_ref)
            dvc_ref[...] = jnp.zeros_like(dvc_ref)

        qv = q_ref[...]
        dov = do_ref[...]
        dob = dov.astype(BF16)
        delta = jnp.sum(dov * o_ref[...].astype(F32), axis=-1, keepdims=True)
        (el, ec), inv = _attn_exp(qv, [kl_ref[...], kc_ref[...]], jnp.where(i == nq, NEG, 0.0))
        pl_, pc = el * inv, ec * inv
        dsl = (pl_ * (_dot_nt(dob, vl_ref[...]) - delta)).astype(BF16)
        dsc = (pc * (_dot_nt(dob, vc_ref[...]) - delta)).astype(BF16)
        dq_ref[...] = _dot(dsl, kl_ref[...]) + _dot(dsc, kc_ref[...])
        dkl_ref[...] += _dot_tn(dsl, qv)
        dkc_ref[...] += _dot_tn(dsc, qv)
        dvl_ref[...] += _dot_tn(pl_.astype(BF16), dob)
        dvc_ref[...] += _dot_tn(pc.astype(BF16), dob)

    return pl.pallas_call(
        body, name=name, grid=(dm.B, HEADS, nq + 1),
        in_specs=[sp["q"], sp["k_lat"], sp["k_ctx"], sp["v_lat"], sp["v_ctx"], sp["o"], sp["o"]],
        out_specs=[sp["q"], sp["k_lat"], pl.BlockSpec((tq, HP), lambda b, hd, i: (b, hd)),
                   sp["v_lat"], pl.BlockSpec((tq, V_HEAD), lambda b, hd, i: (b, hd))],
        out_shape=[jax.ShapeDtypeStruct((T, HQ), F32), jax.ShapeDtypeStruct((dm.B * dm.N, HQ), F32),
                   jax.ShapeDtypeStruct((dm.B * dm.CTX, HQ), F32), jax.ShapeDtypeStruct((dm.B * dm.N, HV), F32),
                   jax.ShapeDtypeStruct((dm.B * dm.CTX, HV), F32)],
        compiler_params=_cparams("parallel", "parallel", "arbitrary"),
    )(q, k, k, v, v, o, do)


def _loss_grad(h, target, dm, name):
    T, D = h.shape
    tm = dm.tm
    nlat = dm.B * dm.N // tm

    def body(h_ref, t_ref, dh_ref, ls_ref):
        lat = (pl.program_id(0) < nlat).astype(F32)
        diff = (h_ref[...] - t_ref[...]) * lat
        dh_ref[...] = diff * (1.0 / D)
        ls_ref[...] = jnp.zeros(ls_ref.shape, F32) + (0.5 / D) * jnp.sum(diff * diff)

    return pl.pallas_call(
        body, name=name, grid=(T // tm,),
        in_specs=[pl.BlockSpec((tm, D), lambda i: (i, 0)), pl.BlockSpec((tm, D), lambda i: (jnp.minimum(i, nlat - 1), 0))],
        out_specs=[pl.BlockSpec((tm, D), lambda i: (i, 0)), pl.BlockSpec((1, 8, 128), lambda i: (i, 0, 0))],
        out_shape=[jax.ShapeDtypeStruct((T, D), F32), jax.ShapeDtypeStruct((T // tm, 8, 128), F32)],
        compiler_params=_cparams("parallel"),
    )(h, target)


def _col_block(cols, target=1152):
    return max(t for t in range(128, min(cols, target) + 1, 128) if cols % t == 0)


def _mod_fwd(cond, w_mod, b_mod, name):
    L, D, C = w_mod.shape
    R = cond.shape[0]
    cb = _col_block(C)

    def body(c_ref, w_ref, b_ref, o_ref):
        cv = c_ref[...]
        sc = (cv * jax.nn.sigmoid(cv)).astype(BF16)
        o_ref[...] = _dot(sc, w_ref[...].astype(BF16)) + b_ref[...]

    return pl.pallas_call(
        body, name=name, grid=(L, C // cb),
        in_specs=[pl.BlockSpec((R, D), lambda l, c: (0, 0)), pl.BlockSpec((None, D, cb), lambda l, c: (l, 0, c)),
                  pl.BlockSpec((None, 1, cb), lambda l, c: (l, 0, c))],
        out_specs=pl.BlockSpec((None, R, cb), lambda l, c: (l, 0, c)),
        out_shape=jax.ShapeDtypeStruct((L, R, C), F32),
        compiler_params=_cparams("parallel", "parallel"),
    )(cond, w_mod, b_mod)


def _mod_bwd(cond, dmod, w_mod, name):
    L, D, C = w_mod.shape
    R = cond.shape[0]
    cb = _col_block(C)
    nc = C // cb

    def body(c_ref, dm_ref, w_ref, gw_ref, ds_ref):
        cv = c_ref[...]
        sc = (cv * jax.nn.sigmoid(cv)).astype(BF16)
        dmv = dm_ref[...].astype(BF16)
        gw_ref[...] = _dot_tn(sc, dmv)
        part = _dot_nt(dmv, w_ref[...].astype(BF16))

        @pl.when(pl.program_id(1) == 0)
        def _():
            ds_ref[...] = part

        @pl.when(pl.program_id(1) > 0)
        def _():
            ds_ref[...] += part

    return pl.pallas_call(
        body, name=name, grid=(L, nc),
        in_specs=[pl.BlockSpec((R, D), lambda l, c: (0, 0)), pl.BlockSpec((None, R, cb), lambda l, c: (l, 0, c)),
                  pl.BlockSpec((None, D, cb), lambda l, c: (l, 0, c))],
        out_specs=[pl.BlockSpec((None, D, cb), lambda l, c: (l, 0, c)), pl.BlockSpec((None, R, D), lambda l, c: (l, 0, 0))],
        out_shape=[jax.ShapeDtypeStruct((L, D, C), F32), jax.ShapeDtypeStruct((L, R, D), F32)],
        compiler_params=_cparams("parallel", "arbitrary"),
    )(cond, dmod, w_mod)


def _row_block(rows, cols, budget=1 << 20):
    best = None
    for t in range(16, rows + 1, 16):
        if rows % t == 0 and t * cols * 4 <= budget:
            best = t
    return best if best is not None else rows


def _sum_slots(recvs, owns, chip, core, bufs, pieces, piece, name):
    n = len(recvs)
    S, R, C = recvs[0].shape
    tr = _row_block(R, C, budget=512 << 10)

    def body(ids_ref, *refs):
        for r_ref, p_ref, o_ref in zip(refs[:n], refs[n:2 * n], refs[-n:]):
            acc = None
            for s in range(S):
                v = jnp.where(ids_ref[0] == s, p_ref[s], r_ref[s]).astype(F32)
                acc = v if acc is None else acc + v
            o_ref[...] = acc

    blk = pl.BlockSpec((S, tr, C), lambda i, ids: (0, i, 0))
    held = [] if bufs is None else list(bufs)
    return pl.pallas_call(
        body, name=name,
        grid_spec=pltpu.PrefetchScalarGridSpec(
            num_scalar_prefetch=1, grid=(R // tr,), in_specs=[blk] * (2 * n) + [pl.BlockSpec(memory_space=pl.ANY)] * len(held),
            out_specs=[pl.BlockSpec((None, None, tr, C), lambda i, ids: (piece, ids[1], i, 0))] * n),
        out_shape=[jax.ShapeDtypeStruct((pieces, 2, R, C), F32)] * n,
        input_output_aliases={1 + 2 * n + i: i for i in range(len(held))}, compiler_params=_cparams("parallel"),
    )(jnp.stack([chip, core]).astype(jnp.int32), *recvs, *owns, *held)


def _adamw(w, gs, m, v, name):
    ng = len(gs)
    R, C = w.shape
    tr = _row_block(R, C)
    c1 = 1.0 / (1.0 - ADAM_B1 ** ADAM_STEP)
    c2 = 1.0 / (1.0 - ADAM_B2 ** ADAM_STEP)

    def body(w_ref, *refs):
        m_ref, v_ref, g_ref, d_ref, mo_ref, vo_ref = refs[ng:]
        g = refs[0][...]
        for g_more in refs[1:ng]:
            g = g + g_more[...]
        g_ref[...] = g
        mn = ADAM_B1 * m_ref[...] + (1.0 - ADAM_B1) * g
        vn = ADAM_B2 * v_ref[...] + (1.0 - ADAM_B2) * (g * g)
        mo_ref[...] = mn
        vo_ref[...] = vn
        d_ref[...] = -ADAM_LR * ((mn * c1) / (jnp.sqrt(vn * c2) + ADAM_EPS) + ADAM_WD * w_ref[...])

    blk = pl.BlockSpec((tr, C), lambda i: (i, 0))
    sd = jax.ShapeDtypeStruct((R, C), F32)
    return pl.pallas_call(
        body, name=name, grid=(R // tr,), in_specs=[blk] * (3 + ng), out_specs=[blk] * 4, out_shape=[sd] * 4,
        compiler_params=_cparams("parallel"),
    )(w, *gs, m, v)


def _rope_tables(dm):
    n = dm.N
    t = jnp.arange(n)
    r = (t // GRID_W).astype(F32)
    col = (t % GRID_W).astype(F32)
    nf = QK_ROPE // 4
    inv = ROPE_BASE ** (-jnp.arange(nf, dtype=F32) / nf)
    ang = jnp.stack([r[:, None] * inv, col[:, None] * inv], axis=1)
    cos, sin = jnp.cos(ang), jnp.sin(ang)
    zero = jnp.zeros_like(sin)
    c64 = jnp.stack([cos, cos], axis=2).reshape(n, QK_ROPE)
    s1 = jnp.stack([-sin, zero], axis=2).reshape(n, QK_ROPE)
    s2 = jnp.stack([zero, sin], axis=2).reshape(n, QK_ROPE)

    def pad(x, fill):
        return jnp.concatenate([jnp.full((n, QK_NOPE), fill, F32), x, jnp.full((n, HEAD_PAD - QK_HEAD), fill, F32)], axis=1)

    lat = jnp.stack([pad(c64, 1.0), pad(s1, 0.0), pad(s2, 0.0)])
    lat = jnp.tile(lat, (1, dm.B, 1))
    nctx = dm.B * dm.CTX
    ctx = jnp.stack([jnp.ones((nctx, HEAD_PAD), F32), jnp.zeros((nctx, HEAD_PAD), F32), jnp.zeros((nctx, HEAD_PAD), F32)])
    return jnp.concatenate([lat, ctx], axis=1)


def _fold_parts(part, dm):
    nblk = part.shape[0]
    nb = (dm.N * nblk) // dm.T
    groups = [part[b * nb:(b + 1) * nb].sum(axis=0) for b in range(dm.B)]
    groups.append(part[dm.B * nb:].sum(axis=0))
    return jnp.stack(groups)


def grouped(items, key):
    groups = {}
    for it in items:
        groups.setdefault(key(it), []).append(it)
    return list(groups.values())


def _flat2(a):
    return a.reshape(-1, a.shape[-1])


def kernel(x, c, ctx, c_ctx, w_mod, b_mod, g_norm, ffn_w1, ffn_w3, ffn_w2, sc_w_in, sc_conv, sc_w_out, mla_w_a, mla_g_qa, mla_w_uq, mla_g_kva, mla_w_ukv, mla_g_q, mla_g_k, mla_w_o, loss_target, m_c_ctx, m_w_mod, m_b_mod, m_g_norm, m_ffn_w1, m_ffn_w3, m_ffn_w2, m_sc_w_in, m_sc_conv, m_sc_w_out, m_mla_w_a, m_mla_g_qa, m_mla_w_uq, m_mla_g_kva, m_mla_w_ukv, m_mla_g_q, m_mla_g_k, m_mla_w_o, v_c_ctx, v_w_mod, v_b_mod, v_g_norm, v_ffn_w1, v_ffn_w3, v_ffn_w2, v_sc_w_in, v_sc_conv, v_sc_w_out, v_mla_w_a, v_mla_g_qa, v_mla_w_uq, v_mla_g_kva, v_mla_w_ukv, v_mla_g_q, v_mla_g_k, v_mla_w_o):
    B, N, D = x.shape
    CTX = ctx.shape[1]
    T = B * (N + CTX)
    tm = next(t for t in (512, 256, 128, 64, 32, 16) if N % t == 0 and (B * CTX) % t == 0)
    dm = Dims(B, N, CTX, D, T, tm)
    L = w_mod.shape[0]
    La, Lb = sc_w_in.shape[0], mla_w_a.shape[0]
    S = N_CHIPS
    ndev = 2 * S
    xi, yi, ci = lax.axis_index("x"), lax.axis_index("y"), lax.axis_index("c")
    chip = 2 * xi + yi
    dev = 2 * chip + ci
    weights = dict(c_ctx=c_ctx, w_mod=w_mod, b_mod=b_mod, g_norm=g_norm, ffn_w1=ffn_w1, ffn_w3=ffn_w3, ffn_w2=ffn_w2,
                   sc_w_in=sc_w_in, sc_conv=sc_conv, sc_w_out=sc_w_out, mla_w_a=mla_w_a, mla_g_qa=mla_g_qa,
                   mla_w_uq=mla_w_uq, mla_g_kva=mla_g_kva, mla_w_ukv=mla_w_ukv, mla_g_q=mla_g_q, mla_g_k=mla_g_k,
                   mla_w_o=mla_w_o)
    mom = dict(c_ctx=(m_c_ctx, v_c_ctx), w_mod=(m_w_mod, v_w_mod), b_mod=(m_b_mod, v_b_mod), g_norm=(m_g_norm, v_g_norm),
               ffn_w1=(m_ffn_w1, v_ffn_w1), ffn_w3=(m_ffn_w3, v_ffn_w3), ffn_w2=(m_ffn_w2, v_ffn_w2),
               sc_w_in=(m_sc_w_in, v_sc_w_in), sc_conv=(m_sc_conv, v_sc_conv), sc_w_out=(m_sc_w_out, v_sc_w_out),
               mla_w_a=(m_mla_w_a, v_mla_w_a), mla_g_qa=(m_mla_g_qa, v_mla_g_qa), mla_w_uq=(m_mla_w_uq, v_mla_w_uq),
               mla_g_kva=(m_mla_g_kva, v_mla_g_kva), mla_w_ukv=(m_mla_w_ukv, v_mla_w_ukv), mla_g_q=(m_mla_g_q, v_mla_g_q),
               mla_g_k=(m_mla_g_k, v_mla_g_k), mla_w_o=(m_mla_w_o, v_mla_w_o))

    big = ["ffn_w1", "ffn_w3", "ffn_w2", "sc_w_in", "sc_w_out", "mla_w_a", "mla_w_uq", "mla_w_ukv", "mla_w_o"]
    F = ffn_w1.shape[-1]
    transposed = ("ffn_w1", "ffn_w3")
    for n in transposed:
        weights[n] = jnp.swapaxes(weights[n], 2, 3)
        mom[n] = tuple(jnp.swapaxes(a, 2, 3) for a in mom[n])
    mixer_names =(["sc_w_in", "sc_w_out"], ["mla_w_a", "mla_w_uq", "mla_w_ukv", "mla_w_o"])

    def placed(name, piece):
        w2 = _flat2(weights[name])
        rows = w2.shape[0] // weights[name].shape[0]
        return _place_cast(w2, piece * rows, rows, chip, S, "place_weight")

    bufs = [{n: placed(n, l) for n in ("ffn_w1", "ffn_w3", "ffn_w2")} for l in range(L)]
    for l in range(L):
        bufs[l].update({n: placed(n, l // 2) for n in mixer_names[l % 2]})
    group_a = ["ffn_w1", "ffn_w3"]
    group_b = lambda l: ["ffn_w2"] + mixer_names[l % 2]
    first = group_a + group_b(0)
    bufs[0].update(zip(first, _ride_alone(_gather_rider([bufs[0][n] for n in first]), "gather_weights")))

    def layer_weights(l):
        b = bufs[l]
        w = dict(w1=b["ffn_w1"].reshape(S, 2, F, D), w3=b["ffn_w3"].reshape(S, 2, F, D), w2=b["ffn_w2"].reshape(S, 2, F, D))
        if l % 2 == 0:
            w["w_in"] = b["sc_w_in"][:, None]
            w["w_out"] = b["sc_w_out"].reshape(1, D, D)
        else:
            w["w_a"] = jnp.pad(b["mla_w_a"].reshape(1, D, -1), ((0, 0), (0, 0), (0, 512 - (Q_LORA + KV_LORA + QK_ROPE))))
            wuq = jnp.moveaxis(b["mla_w_uq"], 0, 1).reshape(1, Q_LORA, HEADS, QK_HEAD)
            w["w_uq"] = jnp.pad(wuq, ((0, 0), (0, 0), (0, 0), (0, HEAD_PAD - QK_HEAD))).reshape(1, Q_LORA, HEADS * HEAD_PAD)
            w["w_ukv"] = jnp.moveaxis(b["mla_w_ukv"], 0, 1).reshape(1, KV_LORA, HEADS * HEAD_PAD)
            w["w_o"] = b["mla_w_o"].reshape(1, HEADS * V_HEAD, D)
        return w

    vecs = ["g_norm", "sc_conv", "mla_g_qa"]
    gathered = _exchange([_flat2(weights[n]) for n in vecs], ("x", "y"), False, "gather_vectors")
    gw = {n: g.reshape((S,) + weights[n].shape) for n, g in zip(vecs, gathered)}
    gnorm = jnp.moveaxis(gw["g_norm"], 0, 2).reshape(L, 3, D)
    convw = jnp.moveaxis(gw["sc_conv"], 0, 2).reshape(La, 3, D)
    gqa = jnp.moveaxis(gw["mla_g_qa"], 0, 1).reshape(Lb, Q_LORA)
    padl = lambda a: jnp.pad(a, ((0, 0), (0, HEAD_PAD - a.shape[1])))
    gains = jnp.stack([padl(gqa), padl(mla_g_kva), padl(mla_g_q), padl(mla_g_k)], axis=1)
    gains = jnp.pad(gains, ((0, 0), (0, 4), (0, 0)))

    R = -(-(ndev * B + 1) // 16) * 16
    call = _exchange([c], ("x", "y", "c"), False, "gather_cond")[0].reshape(ndev * B, D)
    cond = jnp.concatenate([call, c_ctx[None], jnp.zeros((R - ndev * B - 1, D), F32)], axis=0)
    C = w_mod.shape[-1]
    bm = lax.dynamic_slice_in_dim(b_mod, chip * C, C, axis=1)[:, None, :]
    mshard = _mod_fwd(cond, w_mod, bm, "mod_fwd")
    mfull = _exchange([mshard.reshape(L * R, C)], ("x", "y"), False, "gather_mod")[0].reshape(S, L, R, C)
    mfull = jnp.moveaxis(mfull, 0, 2).reshape(L, R, S * C)
    mine = lax.dynamic_slice_in_dim(mfull, dev * B, B, axis=1)
    mod = jnp.concatenate([mine, mfull[:, ndev * B:ndev * B + 1]], axis=1).reshape(L, B + 1, 9, D)

    tabs = _rope_tables(dm)
    h = jnp.concatenate([x.reshape(B * N, D), ctx.reshape(B * CTX, D)], axis=0)

    saved = []
    lw = [None] * L
    for l in range(L):
        kind, j = l % 2, l // 2
        W = lw[l] = layer_weights(l)
        sv = {}
        sv["h0"] = h

        def riding(names):
            if l + 1 == L:
                return None, lambda got: None
            return _gather_rider([bufs[l + 1][n] for n in names]), lambda got: bufs[l + 1].update(zip(names, got))

        rider, keep = riding(["ffn_w1"])
        (h, sv["a1"], sv["b1"], sv["hn1"], sv["y1"]), got = _ffn_fwd(h, mod[l], gnorm[l, 0:1], W["w1"], W["w3"], W["w2"], 0, dm,
                                                                      "ffn_fwd", rider)
        keep(got)
        sv["h1"] = h
        rider, keep = riding(["ffn_w3"] + mixer_names[(l + 1) % 2])
        if kind == 0:
            (sv["p"], sv["hnm"]), got = _sc_in_fwd(h, mod[l], gnorm[l, 1:2], W["w_in"], 0, dm, "sc_in_fwd", rider)
            sv["z"] = _conv_fwd(sv["p"], convw[j], dm, "conv_fwd")
            h, sv["ym"] = _out_fwd(sv["z"], W["w_out"], h, mod[l], 0, dm, "sc_out_fwd")
        else:
            sv["hnm"], sv["q"], sv["k"], sv["v"] = _mla_proj_fwd(h, mod[l], gnorm[l, 1:2], gains[j:j + 1], tabs, W["w_a"], W["w_uq"],
                                                                 W["w_ukv"], 0, dm, "mla_proj_fwd")
            sv["o"], got = _attn_fwd(sv["q"], sv["k"], sv["v"], dm, "attn_fwd", rider)
            h, sv["ym"] = _out_fwd(sv["o"], W["w_o"], h, mod[l], 0, dm, "mla_out_fwd")
        keep(got)
        sv["h2"] = h
        rider, keep = riding(["ffn_w2"])
        (h, sv["a2"], sv["b2"], sv["hn2"], sv["y2"]), got = _ffn_fwd(h, mod[l], gnorm[l, 2:3], W["w1"], W["w3"], W["w2"], 1, dm,
                                                                      "ffn_fwd", rider)
        keep(got)
        saved.append(sv)

    dh, lsum = _loss_grad(h, loss_target.reshape(B * N, D), dm, "loss_grad")
    loss = lax.psum(jnp.sum(lsum[:, 0, 0]), ("x", "y", "c"))

    wq = D // S
    gsum = {n: None for n in big}
    npieces = {n: weights[n].shape[0] * (weights[n].shape[1] if n.startswith("ffn") else 1) for n in big}
    dmod = [None] * L
    dgn = [None] * L
    dconv = [None] * La
    dgains = [None] * Lb
    tk = tm * next(f for f in (3, 2, 1) if (T // tm) % f == 0)
    nk = T // tk
    full_a = pl.BlockSpec((tk, D), lambda s, kk: (kk, 0))
    shard_b = pl.BlockSpec((None, tk, F), lambda s, kk: (s, kk, 0))
    per_slot = lambda r_, c_: pl.BlockSpec((None, r_, c_), lambda s, kk: (s, 0, 0))

    def make_job(grads):
        parts = [g_ for _, _, g_ in grads]
        theirs = _swap_halves(parts, "swap_halves")
        pairs = [None] * len(grads)
        for idx in grouped(range(len(grads)), lambda i: parts[i].shape):
            outs = _pair_sum([parts[i] for i in idx], [theirs[i] for i in idx], ci, "pair_sum")
            for i, o in zip(idx, outs):
                pairs[i] = o
        return [(n, p, pair) for (n, p, _), pair in zip(grads, pairs)]

    def finish_job(job, recv):
        key = lambda i: (recv[i].shape, npieces[job[i][0]], job[i][1], gsum[job[i][0]] is None)
        for idx in grouped(range(len(job)), key):
            names_ = [job[i][0] for i in idx]
            held = None if gsum[names_[0]] is None else [gsum[n] for n in names_]
            outs = _sum_slots([recv[i] for i in idx], [job[i][2] for i in idx], chip, ci, held, npieces[names_[0]],
                              job[idx[0]][1], "sum_slots")
            gsum.update(zip(names_, outs))

    def ffn_back(dh, sv, l, k, job):
        sfx = "1" if k == 0 else "2"
        W = lw[l]
        rider = _scatter_rider([pair for _, _, pair in job]) if job else None
        (dh, da, db, sw, dy, part), recv = _ffn_bwd(dh, sv["h0" if k == 0 else "h2"], mod[l], gnorm[l, 2 * k:2 * k + 1], sv["y" + sfx],
                                                    sv["a" + sfx], sv["b" + sfx], W["w1"], W["w3"], W["w2"], k, dm, "ffn_bwd", rider)
        finish_job(job, recv)
        g1 = _mm_tn(da, sv["hn" + sfx], shard_b, full_a, (S, F, D), per_slot(F, D), (S, nk), "gw1")
        g3 = _mm_tn(db, sv["hn" + sfx], shard_b, full_a, (S, F, D), per_slot(F, D), (S, nk), "gw3")
        g2 = _mm_tn(sw, dy, shard_b, full_a, (S, F, D), per_slot(F, D), (S, nk), "gw2")
        p = 2 * l + k
        return dh, _fold_parts(part, dm), [("ffn_w1", p, g1), ("ffn_w3", p, g3), ("ffn_w2", p, g2)]

    one = (1, nk)
    a1 = lambda kdim: pl.BlockSpec((tk, kdim), lambda s, kk: (kk, 0))
    pending = []
    for l in reversed(range(L)):
        kind, j = l % 2, l // 2
        sv = saved[l]
        W = lw[l]
        dh, p2, grads = ffn_back(dh, sv, l, 1, pending)
        job2 = make_job(grads)
        if kind == 0:
            dy, dz, pg = _out_bwd(dh, sv["ym"], W["w_out"], mod[l], 0, dm, "sc_out_bwd")
            g_out = _mm_tn(sv["z"], dy, a1(D), a1(D), (1, D, D), per_slot(D, D), one, "gw_sc_out")
            dp, dconv[j] = _conv_bwd(dz, sv["p"], convw[j], dm, "conv_bwd")
            g_in = _mm_tn(sv["hnm"], dp, pl.BlockSpec((tk, D), lambda q, kk: (kk, 0)),
                          pl.BlockSpec((None, tk, wq), lambda q, kk: (q // S, kk, q % S)), (3 * S, D, wq), per_slot(D, wq),
                          (3 * S, nk), "gw_sc_in")
            dh, pm = _sc_in_bwd(dh, dp, sv["h1"], mod[l], gnorm[l, 1:2], W["w_in"], 0, dm, "sc_in_bwd")
            grads = [("sc_w_in", j, jnp.moveaxis(g_in.reshape(S, 3, D, wq), 1, 2).reshape(S, D, 3 * wq)),
                     ("sc_w_out", j, g_out.reshape(S, D // S, D))]
        else:
            dy, do, pg = _out_bwd(dh, sv["ym"], W["w_o"], mod[l], 0, dm, "mla_out_bwd")
            g_o = _mm_tn(sv["o"], dy, a1(HEADS * V_HEAD), a1(D), (1, HEADS * V_HEAD, D), per_slot(HEADS * V_HEAD, D), one, "gw_mla_o")
            dq, dkl, dkc, dvl, dvc = _attn_bwd(sv["q"], sv["k"], sv["v"], sv["o"], do, dm, "attn_bwd")
            dh, pm, g_a, g_uq, g_ukv, dgains[j] = _mla_proj_bwd(
                dh, dq, dkl, dkc, dvl, dvc, sv["h1"], mod[l], gnorm[l, 1:2], gains[j:j + 1], tabs, W["w_a"], W["w_uq"], W["w_ukv"], 0, dm, "mla_proj_bwd")
            g_uq = g_uq.reshape(Q_LORA, HEADS, HEAD_PAD)[..., :QK_HEAD].reshape(Q_LORA, S, -1)
            grads = [("mla_w_a", j, g_a[:, :Q_LORA + KV_LORA + QK_ROPE].reshape(S, D // S, -1).astype(BF16)),
                     ("mla_w_uq", j, jnp.moveaxis(g_uq, 1, 0).astype(BF16)),
                     ("mla_w_ukv", j, jnp.moveaxis(g_ukv.reshape(KV_LORA, S, -1), 1, 0).astype(BF16)),
                     ("mla_w_o", j, g_o.reshape(S, HEADS * V_HEAD // S, D))]
        jobm = make_job(grads)
        pm = _fold_parts(pm, dm) + _fold_parts(pg, dm)
        dh, p0, grads = ffn_back(dh, sv, l, 0, job2 + jobm)
        pending = make_job(grads)
        dmod[l] = jnp.concatenate([p0[:, 0:3], pm[:, 0:3], p2[:, 0:3]], axis=1).reshape(B + 1, 9 * D)
        dgn[l] = jnp.stack([p0[:, 3].sum(0), pm[:, 3].sum(0), p2[:, 3].sum(0)])
    grad_x = dh[:B * N].reshape(B, N, D)
    finish_job(pending, _ride_alone(_scatter_rider([pair for _, _, pair in pending]), "scatter_grads"))
    gsum = dict(zip(big, _swap_cores_inplace([gsum[n] for n in big], "swap_cores")))

    dgains_a = jnp.stack(dgains)
    small = [jnp.stack(dmod).reshape(-1), jnp.stack(dgn).reshape(-1), jnp.stack(dconv).reshape(-1), dgains_a.reshape(-1)]
    sizes = [s_.shape[0] for s_ in small]
    flat = jnp.concatenate(small)
    pad = (-flat.shape[0]) % 1024
    flat = jnp.pad(flat, (0, pad)).reshape(-1, 128)
    allsmall = _exchange([flat], ("x", "y", "c"), False, "gather_small")[0].reshape(ndev, -1)
    offs = [0]
    for s_ in sizes:
        offs.append(offs[-1] + s_)
    dmod_all = allsmall[:, offs[0]:offs[1]].reshape(ndev, L, B + 1, 9 * D)
    tot = allsmall[:, offs[1]:offs[4]].sum(axis=0)
    g_gnorm = tot[:offs[2] - offs[1]].reshape(L, 3, D)
    g_conv = tot[offs[2] - offs[1]:offs[3] - offs[1]].reshape(La, 3, D)
    g_gains = tot[offs[3] - offs[1]:].reshape(Lb, 8, HEAD_PAD)
    dM = jnp.concatenate([jnp.moveaxis(dmod_all[:, :, :B], 0, 1).reshape(L, ndev * B, 9 * D),
                          dmod_all[:, :, B].sum(axis=0)[:, None, :], jnp.zeros((L, R - ndev * B - 1, 9 * D), F32)], axis=1)
    g_bmod = dM.sum(axis=1)
    dM_mine = lax.dynamic_slice_in_dim(dM, chip * C, C, axis=2)
    g_wmod, dsil = _mod_bwd(cond, dM_mine, w_mod, "mod_bwd")
    dsil_ctx = dsil[:, ndev * B].sum(axis=0)
    dsil_all = _exchange([jnp.pad(dsil_ctx.reshape(-1, 128), ((0, (-(D // 128)) % 8), (0, 0)))], ("x", "y"), False, "gather_dctx")[0]
    dsil_tot = dsil_all.sum(axis=0)[:D // 128].reshape(D)
    sg = jax.nn.sigmoid(c_ctx)
    g_cctx = dsil_tot * (sg * (1.0 + c_ctx * (1.0 - sg)))

    chip_cols = lambda a, width: lax.dynamic_slice_in_dim(a, chip * width, width, axis=a.ndim - 1)
    small_grads = dict(
        c_ctx=g_cctx, b_mod=g_bmod, g_norm=chip_cols(g_gnorm, D // S), sc_conv=chip_cols(g_conv, D // S),
        mla_g_qa=chip_cols(g_gains[:, 0, :Q_LORA], Q_LORA // S), mla_g_kva=g_gains[:, 1, :KV_LORA],
        mla_g_q=g_gains[:, 2, :QK_HEAD], mla_g_k=g_gains[:, 3, :QK_HEAD])

    grads, deltas, new_m, new_v = {}, {}, {}, {}
    for n, w in weights.items():
        shape = w.shape
        w2 = _flat2(w) if w.ndim > 1 else w.reshape(1, -1)
        m2, v2 = (a.reshape(w2.shape) for a in mom[n])
        if n in gsum:
            gs = [gsum[n].reshape(w2.shape)]
        elif n == "w_mod":
            gs = [_flat2(g_wmod)]
        else:
            gs = [small_grads[n].reshape(w2.shape)]
        g_, d_, m_, v_ = _adamw(w2, gs, m2, v2, "adamw")
        grads[n], deltas[n], new_m[n], new_v[n] = (a.reshape(shape) for a in (g_, d_, m_, v_))
    for n in transposed:
        grads[n], deltas[n], new_m[n], new_v[n] = (jnp.swapaxes(a, 2, 3) for a in (grads[n], deltas[n], new_m[n], new_v[n]))

    names = list(weights)
    return (loss, grad_x, *[grads[n] for n in names], *[deltas[n] for n in names], *[new_m[n] for n in names],
            *[new_v[n] for n in names])
```

```python
import functools
import math
from typing import NamedTuple

import jax
import jax.numpy as jnp
from jax import lax
from jax.experimental import pallas as pl
from jax.experimental.pallas import tpu as pltpu

F32 = jnp.float32
BF16 = jnp.bfloat16
EPS = 1e-6
GRID_W = 64
HEADS = 8
QK_NOPE = 128
QK_ROPE = 64
QK_HEAD = QK_NOPE + QK_ROPE
HEAD_PAD = 256
V_HEAD = 128
Q_LORA = 256
KV_LORA = 128
ROPE_BASE = 10000.0
QK_SCALE = QK_HEAD ** -0.5
ADAM_LR, ADAM_B1, ADAM_B2, ADAM_EPS, ADAM_WD, ADAM_STEP = 0.001, 0.9, 0.999, 1e-08, 0.01, 10
N_CHIPS = 4
VMEM_LIMIT = 56 * 1024 * 1024
MESH = pl.DeviceIdType.MESH
NEG = -1e30


class Dims(NamedTuple):
    B: int
    N: int
    CTX: int
    D: int
    T: int
    tm: int


def _cparams(*sem):
    return pltpu.CompilerParams(dimension_semantics=sem if sem else None, vmem_limit_bytes=VMEM_LIMIT)


def _dot(a, b):
    return jnp.dot(a, b, preferred_element_type=F32)


def _dot_nt(a, b):
    return lax.dot_general(a, b, (((1,), (1,)), ((), ())), preferred_element_type=F32)


def _dot_tn(a, b):
    return lax.dot_general(a, b, (((0,), (0,)), ((), ())), preferred_element_type=F32)


def _rms(x, n):
    r = lax.rsqrt(jnp.sum(x * x, axis=-1, keepdims=True) * (1.0 / n) + EPS)
    return x * r, r


def _rms_bwd(dxh, xh, r, n):
    return r * (dxh - xh * (jnp.sum(dxh * xh, axis=-1, keepdims=True) * (1.0 / n)))


def _pre(h, g, shift, scale):
    xh, _ = _rms(h, h.shape[-1])
    return (xh * g) * (1.0 + scale) + shift


def _pre_bwd(dout, h, g, scale):
    d = h.shape[-1]
    xh, r = _rms(h, d)
    n = xh * g
    dshift = jnp.sum(dout, axis=0, keepdims=True)
    dscale = jnp.sum(dout * n, axis=0, keepdims=True)
    dn = dout * (1.0 + scale)
    dg = jnp.sum(dn * xh, axis=0, keepdims=True)
    dh = _rms_bwd(dn * g, xh, r, d)
    return dh, dshift, dscale, dg


def _write_part(part_ref, dshift=None, dscale=None, dgate=None, dg=None):
    z = jnp.zeros((1, part_ref.shape[-1]), F32)
    part_ref[0, 0:1, :] = z if dshift is None else dshift
    part_ref[0, 1:2, :] = z if dscale is None else dscale
    part_ref[0, 2:3, :] = z if dgate is None else dgate
    part_ref[0, 3:4, :] = z if dg is None else dg
    part_ref[0, 4:8, :] = jnp.zeros((4, part_ref.shape[-1]), F32)


def _grp(dm):
    nb = dm.N // dm.tm
    return lambda i: jnp.minimum(i // nb, dm.B)


def _n_chunks(rows, row_bytes):
    n = 16
    while n > 1 and (rows % (16 * n) or (rows // n) * row_bytes < (256 << 10)):
        n //= 2
    return n


def _start_local(src, dst, sems, k0, nchunk):
    ch = src.shape[0] // nchunk
    copies = []
    for j in range(nchunk):
        cp = pltpu.make_async_copy(src.at[pl.ds(j * ch, ch)], dst.at[pl.ds(j * ch, ch)], sems.at[k0 + j])
        cp.start()
        copies.append(cp)
    return copies


def _exchange(arrs, axes, scatter, name, own="copy"):
    n = len(arrs)
    nbits = len(axes)
    slots = 2 ** nbits
    pats = list(range(1, slots))
    inplace = own == "inplace"
    nck = [_n_chunks(a.shape[-2], a.shape[-1] * a.dtype.itemsize) for a in arrs]
    base = [sum(nck[:i]) * len(pats) for i in range(n)]
    nsem = sum(nck) * len(pats)

    def body(*refs):
        ins, outs = refs[:n], refs[n:2 * n]
        send, recv, loc = refs[2 * n:]
        pos = {a: lax.axis_index(a) for a in ("x", "y", "c")}

        def slot_of(p):
            s = 0
            for a in axes:
                s = 2 * s + p[a]
            return s

        me = slot_of(pos)
        local = []
        for i in range(n):
            if own == "copy":
                local += _start_local(ins[i].at[me] if scatter else ins[i], outs[i].at[me], loc, sum(nck[:i]), nck[i])
        remote = []
        for pi, pat in enumerate(pats):
            peer = dict(pos)
            for bi, a in enumerate(axes):
                if (pat >> (nbits - 1 - bi)) & 1:
                    peer[a] = 1 - pos[a]
            them = slot_of(peer)
            for i in range(n):
                ch = arrs[i].shape[-2] // nck[i]
                for j in range(nck[i]):
                    k = base[i] + pi * nck[i] + j
                    rs = pl.ds(j * ch, ch)
                    if inplace:
                        src = outs[i].at[me, rs]
                    else:
                        src = ins[i].at[them, rs] if scatter else ins[i].at[rs]
                    cp = pltpu.make_async_remote_copy(
                        src_ref=src, dst_ref=outs[i].at[me, rs], send_sem=send.at[k], recv_sem=recv.at[k],
                        device_id=(peer["x"], peer["y"], peer["c"]), device_id_type=MESH)
                    cp.start()
                    remote.append(cp)
        for cp in local:
            cp.wait()
        for cp in remote:
            cp.wait()

    out_shape = [jax.ShapeDtypeStruct(a.shape if (scatter or inplace) else (slots,) + a.shape, a.dtype) for a in arrs]
    any_spec = pl.BlockSpec(memory_space=pl.ANY)
    outs = pl.pallas_call(
        body, name=name, out_shape=out_shape, in_specs=[any_spec] * n, out_specs=[any_spec] * n,
        scratch_shapes=[pltpu.SemaphoreType.DMA((nsem,)), pltpu.SemaphoreType.DMA((nsem,)), pltpu.SemaphoreType.DMA((sum(nck),))],
        input_output_aliases={i: i for i in range(n)} if inplace else {},
        compiler_params=pltpu.CompilerParams(has_side_effects=True),
    )(*arrs)
    return list(outs)


class Rider(NamedTuple):
    ins: list
    out_shapes: list
    aliases: dict
    sems: list
    start: object
    mid: object
    end: object


MID_STEPS = 6


def _hosted(body, rider, *, name, grid, in_specs, out_specs, out_shape, scratch_shapes, sem, args):
    if rider is None:
        outs = pl.pallas_call(body, name=name, grid=grid, in_specs=in_specs, out_specs=out_specs, out_shape=out_shape,
                              scratch_shapes=scratch_shapes, compiler_params=_cparams(*sem))(*args)
        return outs, []
    n_in, n_out, n_s = len(in_specs), len(out_specs), len(scratch_shapes)
    nri, nro = len(rider.ins), len(rider.out_shapes)
    nsteps = math.prod(grid)

    def wrapped(*refs):
        bounds = [0, n_in, n_in + nri, n_in + nri + n_out, n_in + nri + n_out + nro, n_in + nri + n_out + nro + n_s, len(refs)]
        ins, rins, outs, routs, scr, sems = (refs[lo:hi] for lo, hi in zip(bounds[:-1], bounds[1:]))
        step = 0
        for ax, extent in enumerate(grid):
            step = step * extent + pl.program_id(ax)

        @pl.when(step == 0)
        def _():
            rider.start(rins, routs, sems)

        body(*ins, *outs, *scr)

        if rider.mid is not None:
            @pl.when(step == max(nsteps - 1 - MID_STEPS, 0))
            def _():
                rider.mid(rins, routs, sems)

        @pl.when(step == nsteps - 1)
        def _():
            rider.end(rins, routs, sems)

    any_spec = pl.BlockSpec(memory_space=pl.ANY)
    outs = pl.pallas_call(
        wrapped, name=name, grid=grid, in_specs=list(in_specs) + [any_spec] * nri, out_specs=list(out_specs) + [any_spec] * nro,
        out_shape=list(out_shape) + list(rider.out_shapes), scratch_shapes=list(scratch_shapes) + list(rider.sems),
        input_output_aliases={n_in + i: n_out + o for i, o in rider.aliases.items()},
        compiler_params=pltpu.CompilerParams(dimension_semantics=("arbitrary",) * len(grid), vmem_limit_bytes=VMEM_LIMIT,
                                             has_side_effects=True),
    )(*args, *rider.ins)
    return outs[:n_out], list(outs[n_out:])


def _gather_rider(bufs):
    n = len(bufs)
    halves = [a.shape[1] // 2 for a in bufs]
    nck = [_n_chunks(h, a.shape[2] * a.dtype.itemsize) for h, a in zip(halves, bufs)]
    base = [3 * sum(nck[:i]) for i in range(n)]
    nsem = 3 * sum(nck)

    def plan():
        x, y, c = lax.axis_index("x"), lax.axis_index("y"), lax.axis_index("c")
        pieces = []
        for pi, (px, py) in enumerate([(x, 1 - y), (1 - x, y), (1 - x, 1 - y)]):
            for i in range(n):
                ch = halves[i] // nck[i]
                for j in range(nck[i]):
                    pieces.append((base[i] + pi * nck[i] + j, px, py, 2 * px + py, i, j * ch, ch))
        return x, y, c, 2 * x + y, pieces

    def rows(i, off, ch, core):
        return pl.ds(pl.multiple_of(core * halves[i] + off, 16), ch)

    def over_ici(outs, sems, c, slot, k, px, py, i, off, ch):
        ref = outs[i].at[slot, rows(i, off, ch, c)]
        return pltpu.make_async_remote_copy(src_ref=ref, dst_ref=ref, send_sem=sems[0].at[k], recv_sem=sems[1].at[k],
                                            device_id=(px, py, c), device_id_type=MESH)

    def over_d2d(outs, sems, x, y, c, slot, k, i, off, ch, core):
        ref = outs[i].at[slot, rows(i, off, ch, core)]
        return pltpu.make_async_remote_copy(src_ref=ref, dst_ref=ref, send_sem=sems[2].at[k], recv_sem=sems[3].at[k],
                                            device_id=(x, y, 1 - c), device_id_type=MESH)

    def start(ins, outs, sems):
        x, y, c, me, pieces = plan()
        for k, px, py, them, i, off, ch in pieces:
            over_ici(outs, sems, c, me, k, px, py, i, off, ch).start()

    def mid(ins, outs, sems):
        x, y, c, me, pieces = plan()
        for k, px, py, them, i, off, ch in pieces:
            over_ici(outs, sems, c, them, k, px, py, i, off, ch).wait_recv()
            over_d2d(outs, sems, x, y, c, them, k, i, off, ch, c).start()

    def end(ins, outs, sems):
        x, y, c, me, pieces = plan()
        for k, px, py, them, i, off, ch in pieces:
            over_ici(outs, sems, c, me, k, px, py, i, off, ch).wait_send()
            over_d2d(outs, sems, x, y, c, them, k, i, off, ch, c).wait_send()
        for k, px, py, them, i, off, ch in pieces:
            over_d2d(outs, sems, x, y, c, them, k, i, off, ch, 1 - c).wait_recv()

    return Rider(ins=list(bufs), out_shapes=[jax.ShapeDtypeStruct(a.shape, a.dtype) for a in bufs],
                 aliases={i: i for i in range(n)}, sems=[pltpu.SemaphoreType.DMA((nsem,))] * 4, start=start, mid=mid, end=end)


def _scatter_rider(srcs):
    n = len(srcs)
    nck = [_n_chunks(a.shape[1], a.shape[2] * a.dtype.itemsize) for a in srcs]
    base = [3 * sum(nck[:i]) for i in range(n)]
    nsem = 3 * sum(nck)

    def copies(ins, outs, sems):
        x, y, c = lax.axis_index("x"), lax.axis_index("y"), lax.axis_index("c")
        me = 2 * x + y
        for pi, (px, py) in enumerate([(x, 1 - y), (1 - x, y), (1 - x, 1 - y)]):
            for i in range(n):
                ch = srcs[i].shape[1] // nck[i]
                for j in range(nck[i]):
                    k = base[i] + pi * nck[i] + j
                    rs = pl.ds(j * ch, ch)
                    yield pltpu.make_async_remote_copy(
                        src_ref=ins[i].at[2 * px + py, rs], dst_ref=outs[i].at[me, rs], send_sem=sems[0].at[k],
                        recv_sem=sems[1].at[k], device_id=(px, py, c), device_id_type=MESH)

    def start(ins, outs, sems):
        for cp in copies(ins, outs, sems):
            cp.start()

    def end(ins, outs, sems):
        for cp in copies(ins, outs, sems):
            cp.wait()

    return Rider(ins=list(srcs), out_shapes=[jax.ShapeDtypeStruct(a.shape, a.dtype) for a in srcs], aliases={},
                 sems=[pltpu.SemaphoreType.DMA((nsem,))] * 2, start=start, mid=None, end=end)


def _ride_alone(rider, name):
    n_in, n_out = len(rider.ins), len(rider.out_shapes)

    def body(*refs):
        ins, outs, sems = refs[:n_in], refs[n_in:n_in + n_out], refs[n_in + n_out:]
        rider.start(ins, outs, sems)
        if rider.mid is not None:
            rider.mid(ins, outs, sems)
        rider.end(ins, outs, sems)

    any_spec = pl.BlockSpec(memory_space=pl.ANY)
    outs = pl.pallas_call(
        body, name=name, out_shape=list(rider.out_shapes), in_specs=[any_spec] * n_in, out_specs=[any_spec] * n_out,
        scratch_shapes=list(rider.sems), input_output_aliases=dict(rider.aliases),
        compiler_params=pltpu.CompilerParams(has_side_effects=True),
    )(*rider.ins)
    return list(outs)


def _swap_cores_inplace(bufs, name):
    n = len(bufs)
    nck = [_n_chunks(a.shape[2], a.shape[3] * a.dtype.itemsize) for a in bufs]
    base = [sum(a.shape[0] * k for a, k in zip(bufs[:i], nck[:i])) for i in range(n)]
    nsem = sum(a.shape[0] * k for a, k in zip(bufs, nck))

    def body(*refs):
        outs = refs[n:2 * n]
        send, recv = refs[2 * n:]
        x, y, c = lax.axis_index("x"), lax.axis_index("y"), lax.axis_index("c")

        def copies(core):
            for i in range(n):
                ch = bufs[i].shape[2] // nck[i]
                for p in range(bufs[i].shape[0]):
                    for j in range(nck[i]):
                        k = base[i] + p * nck[i] + j
                        ref = outs[i].at[p, core, pl.ds(j * ch, ch)]
                        yield pltpu.make_async_remote_copy(src_ref=ref, dst_ref=ref, send_sem=send.at[k], recv_sem=recv.at[k],
                                                           device_id=(x, y, 1 - c), device_id_type=MESH)

        for cp in copies(c):
            cp.start()
        for cp in copies(c):
            cp.wait_send()
        for cp in copies(1 - c):
            cp.wait_recv()

    any_spec = pl.BlockSpec(memory_space=pl.ANY)
    outs = pl.pallas_call(
        body, name=name, out_shape=[jax.ShapeDtypeStruct(a.shape, a.dtype) for a in bufs], in_specs=[any_spec] * n,
        out_specs=[any_spec] * n, scratch_shapes=[pltpu.SemaphoreType.DMA((nsem,))] * 2,
        input_output_aliases={i: i for i in range(n)}, compiler_params=pltpu.CompilerParams(has_side_effects=True),
    )(*bufs)
    return list(outs)


def _place_cast(w, row0, rows, slot, slots, name):
    C = w.shape[1]
    tr = _row_block(rows, C)
    blk0 = row0 // tr

    def body(slot_ref, w_ref, o_ref):
        del slot_ref
        o_ref[...] = w_ref[...].astype(BF16)

    return pl.pallas_call(
        body, name=name,
        grid_spec=pltpu.PrefetchScalarGridSpec(
            num_scalar_prefetch=1, grid=(rows // tr,), in_specs=[pl.BlockSpec((tr, C), lambda i, sr: (blk0 + i, 0))],
            out_specs=pl.BlockSpec((None, tr, C), lambda i, sr: (sr[0], i, 0))),
        out_shape=jax.ShapeDtypeStruct((slots, rows, C), BF16),
        compiler_params=_cparams("parallel"),
    )(slot.reshape(1).astype(jnp.int32), w)


def _swap_halves(arrs, name):
    n = len(arrs)
    S = arrs[0].shape[0]
    halves = [a.shape[1] // 2 for a in arrs]
    nck = [_n_chunks(h, a.shape[2] * a.dtype.itemsize) for h, a in zip(halves, arrs)]
    base = [S * sum(nck[:i]) for i in range(n)]
    nsem = S * sum(nck)

    def body(*refs):
        ins, outs = refs[:n], refs[n:2 * n]
        send, recv = refs[2 * n:]
        x, y, c = lax.axis_index("x"), lax.axis_index("y"), lax.axis_index("c")
        copies = []
        for i in range(n):
            ch = halves[i] // nck[i]
            for s in range(S):
                for j in range(nck[i]):
                    k = base[i] + s * nck[i] + j
                    src = ins[i].at[s, pl.ds(pl.multiple_of((1 - c) * halves[i] + j * ch, 16), ch)]
                    cp = pltpu.make_async_remote_copy(src_ref=src, dst_ref=outs[i].at[s, pl.ds(j * ch, ch)], send_sem=send.at[k],
                                                      recv_sem=recv.at[k], device_id=(x, y, 1 - c), device_id_type=MESH)
                    cp.start()
                    copies.append(cp)
        for cp in copies:
            cp.wait()

    any_spec = pl.BlockSpec(memory_space=pl.ANY)
    outs = pl.pallas_call(
        body, name=name, out_shape=[jax.ShapeDtypeStruct((S, h, a.shape[2]), a.dtype) for h, a in zip(halves, arrs)],
        in_specs=[any_spec] * n, out_specs=[any_spec] * n,
        scratch_shapes=[pltpu.SemaphoreType.DMA((nsem,))] * 2,
        compiler_params=pltpu.CompilerParams(has_side_effects=True),
    )(*arrs)
    return list(outs)


def _pair_sum(gs, rs, core, name):
    n = len(gs)
    S, rows, C = gs[0].shape
    half = rows // 2
    tr = _row_block(half, C)
    nb = half // tr

    def body(core_ref, *refs):
        del core_ref
        for g_ref, r_ref, o_ref in zip(refs[:n], refs[n:2 * n], refs[2 * n:]):
            o_ref[...] = (g_ref[...].astype(F32) + r_ref[...].astype(F32)).astype(BF16)

    blk = pl.BlockSpec((None, tr, C), lambda s, i, cr: (s, i, 0))
    mine = pl.BlockSpec((None, tr, C), lambda s, i, cr: (s, cr[0] * nb + i, 0))
    return pl.pallas_call(
        body, name=name,
        grid_spec=pltpu.PrefetchScalarGridSpec(num_scalar_prefetch=1, grid=(S, nb), in_specs=[mine] * n + [blk] * n,
                                               out_specs=[blk] * n),
        out_shape=[jax.ShapeDtypeStruct((S, half, C), BF16)] * n,
        compiler_params=_cparams("parallel", "parallel"),
    )(core.reshape(1).astype(jnp.int32), *gs, *rs)


def _ffn_fwd(h, mod, g, w1, w3, w2, k, dm, name, rider=None):
    T, D = h.shape
    S, F = w1.shape[0], w1.shape[-2]
    tm = dm.tm
    r0 = 6 if k else 0
    grp = _grp(dm)

    def body(h_ref, mod_ref, g_ref, w1_ref, w3_ref, w2_ref, ho_ref, a_ref, b_ref, hn_ref, y_ref, hn_s, acc):
        s = pl.program_id(1)

        @pl.when(s == 0)
        def _():
            hn = _pre(h_ref[...], g_ref[...], mod_ref[0, r0:r0 + 1, :], mod_ref[0, r0 + 1:r0 + 2, :]).astype(BF16)
            hn_s[...] = hn
            hn_ref[...] = hn
            acc[...] = jnp.zeros_like(acc)

        hn = hn_s[...]
        a = _dot_nt(hn, w1_ref[...])
        b = _dot_nt(hn, w3_ref[...])
        a_ref[0] = a.astype(BF16)
        b_ref[0] = b.astype(BF16)
        sw = (a * jax.nn.sigmoid(a) * b).astype(BF16)
        acc[...] += _dot(sw, w2_ref[...])

        @pl.when(s == S - 1)
        def _():
            y = acc[...]
            y_ref[...] = y.astype(BF16)
            ho_ref[...] = h_ref[...] + 0.5 * mod_ref[0, r0 + 2:r0 + 3, :] * y

    row = pl.BlockSpec((tm, D), lambda i, s: (i, 0))
    wrow = pl.BlockSpec((None, F, D), lambda i, s: (s, 0, 0))
    ab = pl.BlockSpec((1, tm, F), lambda i, s: (s, i, 0))
    return _hosted(
        body, rider, name=name, grid=(T // tm, S),
        in_specs=[row, pl.BlockSpec((1, 9, D), lambda i, s: (grp(i), 0, 0)), pl.BlockSpec((1, D), lambda i, s: (0, 0)),
                  wrow, wrow, wrow],
        out_specs=[row, ab, ab, row, row],
        out_shape=[jax.ShapeDtypeStruct((T, D), F32), jax.ShapeDtypeStruct((S, T, F), BF16),
                   jax.ShapeDtypeStruct((S, T, F), BF16), jax.ShapeDtypeStruct((T, D), BF16),
                   jax.ShapeDtypeStruct((T, D), BF16)],
        scratch_shapes=[pltpu.VMEM((tm, D), BF16), pltpu.VMEM((tm, D), F32)],
        sem=("parallel", "arbitrary"), args=(h, mod, g, w1, w3, w2))


def _ffn_bwd(dh, h, mod, g, y, a, b, w1, w3, w2, k, dm, name, rider=None):
    T, D = h.shape
    S, F = w1.shape[0], w1.shape[-2]
    tm = dm.tm
    r0 = 6 if k else 0
    grp = _grp(dm)

    def body(dh_ref, h_ref, mod_ref, g_ref, y_ref, a_ref, b_ref, w1_ref, w3_ref, w2_ref,
             dho_ref, da_ref, db_ref, sw_ref, dy_ref, part_ref, dy_s, acc):
        s = pl.program_id(1)

        @pl.when(s == 0)
        def _():
            dy = (0.5 * mod_ref[0, r0 + 2:r0 + 3, :] * dh_ref[...]).astype(BF16)
            dy_s[...] = dy
            dy_ref[...] = dy
            acc[...] = jnp.zeros_like(acc)

        ds = _dot_nt(dy_s[...], w2_ref[...]).astype(BF16)
        av = a_ref[0]
        bv = b_ref[0]
        sig = jax.nn.sigmoid(av)
        sil = av * sig
        sw_ref[0] = sil * bv
        db = ds * sil
        da = ds * bv * (sig + sil * (1.0 - sig))
        da_ref[0] = da
        db_ref[0] = db
        acc[...] += _dot(da, w1_ref[...]) + _dot(db, w3_ref[...])

        @pl.when(s == S - 1)
        def _():
            dhv = dh_ref[...]
            dhb, dshift, dscale, dg = _pre_bwd(acc[...], h_ref[...], g_ref[...], mod_ref[0, r0 + 1:r0 + 2, :])
            dho_ref[...] = dhv + dhb
            dgate = 0.5 * jnp.sum(dhv * y_ref[...].astype(F32), axis=0, keepdims=True)
            _write_part(part_ref, dshift, dscale, dgate, dg)

    row = pl.BlockSpec((tm, D), lambda i, s: (i, 0))
    wrow = pl.BlockSpec((None, F, D), lambda i, s: (s, 0, 0))
    ab = pl.BlockSpec((1, tm, F), lambda i, s: (s, i, 0))
    stf = jax.ShapeDtypeStruct((S, T, F), BF16)
    return _hosted(
        body, rider, name=name, grid=(T // tm, S),
        in_specs=[row, row, pl.BlockSpec((1, 9, D), lambda i, s: (grp(i), 0, 0)), pl.BlockSpec((1, D), lambda i, s: (0, 0)),
                  row, ab, ab, wrow, wrow, wrow],
        out_specs=[row, ab, ab, ab, row, pl.BlockSpec((1, 8, D), lambda i, s: (i, 0, 0))],
        out_shape=[jax.ShapeDtypeStruct((T, D), F32), stf, stf, stf, jax.ShapeDtypeStruct((T, D), BF16),
                   jax.ShapeDtypeStruct((T // tm, 8, D), F32)],
        scratch_shapes=[pltpu.VMEM((tm, D), BF16), pltpu.VMEM((tm, D), F32)],
        sem=("parallel", "arbitrary"), args=(dh, h, mod, g, y, a, b, w1, w3, w2))


def _mm_tn(a, b, a_spec, b_spec, out_shape, out_spec, grid, name):
    nk = grid[-1]
    kax = len(grid) - 1
    blk = tuple(d for d in out_spec.block_shape if d is not None)

    def body(a_ref, b_ref, o_ref, acc):
        kk = pl.program_id(kax)

        @pl.when(kk == 0)
        def _():
            acc[...] = jnp.zeros_like(acc)

        acc[...] += _dot_tn(a_ref[...].astype(BF16), b_ref[...].astype(BF16))

        @pl.when(kk == nk - 1)
        def _():
            o_ref[...] = acc[...].astype(o_ref.dtype)

    return pl.pallas_call(
        body, name=name, grid=grid,
        in_specs=[a_spec, b_spec], out_specs=out_spec, out_shape=jax.ShapeDtypeStruct(out_shape, BF16),
        scratch_shapes=[pltpu.VMEM(blk, F32)],
        compiler_params=_cparams(*(["parallel"] * kax + ["arbitrary"])),
    )(a, b)


def _sc_in_fwd(h, mod, g, w_in, j, dm, name, rider=None):
    T, D = h.shape
    tm = dm.tm
    wq = D // N_CHIPS
    nq = 3 * N_CHIPS
    grp = _grp(dm)

    def body(h_ref, mod_ref, g_ref, w_ref, p_ref, hn_ref, hn_s):
        @pl.when(pl.program_id(1) == 0)
        def _():
            hn = _pre(h_ref[...], g_ref[...], mod_ref[0, 3:4, :], mod_ref[0, 4:5, :]).astype(BF16)
            hn_s[...] = hn
            hn_ref[...] = hn

        p_ref[...] = _dot(hn_s[...], w_ref[...])

    row = pl.BlockSpec((tm, D), lambda i, q: (i, 0))
    return _hosted(
        body, rider, name=name, grid=(T // tm, nq),
        in_specs=[row, pl.BlockSpec((1, 9, D), lambda i, q: (grp(i), 0, 0)), pl.BlockSpec((1, D), lambda i, q: (0, 0)),
                  pl.BlockSpec((None, None, D, wq), lambda i, q: (q // 3, j, 0, q % 3))],
        out_specs=[pl.BlockSpec((None, tm, wq), lambda i, q: (q // N_CHIPS, i, q % N_CHIPS)), row],
        out_shape=[jax.ShapeDtypeStruct((3, T, D), F32), jax.ShapeDtypeStruct((T, D), BF16)],
        scratch_shapes=[pltpu.VMEM((tm, D), BF16)],
        sem=("parallel", "arbitrary"), args=(h, mod, g, w_in))


def _conv_cols(dm):
    return 256 if dm.D % 256 == 0 else 128


def _seg_masks(r, dm):
    bn = dm.B * dm.N
    lat = r < bn
    off = jnp.where(lat, lax.rem(r, dm.N), lax.rem(r - bn, dm.CTX))
    seg = jnp.where(lat, dm.N, dm.CTX)
    inside = (r >= 0) & (r < dm.T)
    return ((off != 0) & inside).astype(F32), ((off != seg - 1) & inside).astype(F32)


def _conv_specs(dm):
    tb, cb, nr8 = dm.tm, _conv_cols(dm), dm.T // 8
    prev8 = lambda c, i: jnp.maximum(i * (tb // 8) - 1, 0)
    next8 = lambda c, i: jnp.minimum((i + 1) * (tb // 8), nr8 - 1)
    return dict(
        tb=tb, cb=cb,
        p=pl.BlockSpec((3, tb, cb), lambda c, i: (0, i, c)),
        p_prev=pl.BlockSpec((3, 8, cb), lambda c, i: (0, prev8(c, i), c)),
        p_next=pl.BlockSpec((3, 8, cb), lambda c, i: (0, next8(c, i), c)),
        row=pl.BlockSpec((tb, cb), lambda c, i: (i, c)),
        row_prev=pl.BlockSpec((8, cb), lambda c, i: (prev8(c, i), c)),
        row_next=pl.BlockSpec((8, cb), lambda c, i: (next8(c, i), c)),
        w=pl.BlockSpec((3, cb), lambda c, i: (0, c)),
    )


def _shift_rows(x, before, after, tb):
    rid = lax.broadcasted_iota(jnp.int32, x.shape, 0)
    down = jnp.where(rid == 0, before, pltpu.roll(x, 1, 0))
    up = jnp.where(rid == tb - 1, after, pltpu.roll(x, tb - 1, 0))
    return down, up


def _conv_fwd(p, wc, dm, name):
    T, D = dm.T, dm.D
    sp = _conv_specs(dm)
    tb, cb = sp["tb"], sp["cb"]

    def body(p_ref, pp_ref, pn_ref, w_ref, z_ref):
        r = pl.program_id(1) * tb + lax.broadcasted_iota(jnp.int32, (tb, cb), 0)
        mp, mn = _seg_masks(r, dm)
        cu = p_ref[1] * p_ref[2]
        prev, nxt = _shift_rows(cu, pp_ref[1, 7:8, :] * pp_ref[2, 7:8, :], pn_ref[1, 0:1, :] * pn_ref[2, 0:1, :], tb)
        conv = w_ref[0:1, :] * (prev * mp) + w_ref[1:2, :] * cu + w_ref[2:3, :] * (nxt * mn)
        z_ref[...] = (p_ref[0] * conv).astype(BF16)

    return pl.pallas_call(
        body, name=name, grid=(D // cb, T // tb),
        in_specs=[sp["p"], sp["p_prev"], sp["p_next"], sp["w"]], out_specs=sp["row"],
        out_shape=jax.ShapeDtypeStruct((T, D), BF16),
        compiler_params=_cparams("parallel", "parallel"),
    )(p, p, p, wc)


def _conv_bwd(dz, p, wc, dm, name):
    T, D = dm.T, dm.D
    sp = _conv_specs(dm)
    tb, cb = sp["tb"], sp["cb"]

    def body(dz_ref, dzp_ref, dzn_ref, p_ref, pp_ref, pn_ref, w_ref, dp_ref, dw_ref):
        i = pl.program_id(1)
        r = i * tb + lax.broadcasted_iota(jnp.int32, (tb, cb), 0)
        mp, mn = _seg_masks(r, dm)
        rb = i * tb + lax.broadcasted_iota(jnp.int32, (1, cb), 0)
        _, mn_before = _seg_masks(rb - 1, dm)
        mp_after, _ = _seg_masks(rb + tb, dm)
        bg, cg, u = p_ref[0], p_ref[1], p_ref[2]
        cu = cg * u
        prev, nxt = _shift_rows(cu, pp_ref[1, 7:8, :] * pp_ref[2, 7:8, :], pn_ref[1, 0:1, :] * pn_ref[2, 0:1, :], tb)
        prev = prev * mp
        nxt = nxt * mn
        w0, w1, w2 = w_ref[0:1, :], w_ref[1:2, :], w_ref[2:3, :]
        conv = w0 * prev + w1 * cu + w2 * nxt
        dz = dz_ref[...]
        dp_ref[0] = dz * conv
        dconv = dz * bg

        @pl.when(i == 0)
        def _():
            dw_ref[...] = jnp.zeros_like(dw_ref)

        dw_ref[0:1, :] += jnp.sum(dconv * prev, axis=0, keepdims=True)
        dw_ref[1:2, :] += jnp.sum(dconv * cu, axis=0, keepdims=True)
        dw_ref[2:3, :] += jnp.sum(dconv * nxt, axis=0, keepdims=True)
        dconv_before = dzp_ref[7:8, :] * pp_ref[0, 7:8, :] * mn_before
        dconv_after = dzn_ref[0:1, :] * pn_ref[0, 0:1, :] * mp_after
        from_prev, _ = _shift_rows(dconv * mn, dconv_before, dconv_after, tb)
        _, from_next = _shift_rows(dconv * mp, dconv_before, dconv_after, tb)
        dcu = w1 * dconv + w0 * from_next + w2 * from_prev
        dp_ref[1] = dcu * u
        dp_ref[2] = dcu * cg

    return pl.pallas_call(
        body, name=name, grid=(D // cb, T // tb),
        in_specs=[sp["row"], sp["row_prev"], sp["row_next"], sp["p"], sp["p_prev"], sp["p_next"], sp["w"]],
        out_specs=[sp["p"], sp["w"]],
        out_shape=[jax.ShapeDtypeStruct((3, T, D), F32), jax.ShapeDtypeStruct((3, D), F32)],
        compiler_params=_cparams("parallel", "arbitrary"),
    )(dz, dz, dz, p, p, p, wc)


def _out_fwd(z, w, h, mod, j, dm, name):
    T, D = h.shape
    K = z.shape[1]
    tm = dm.tm
    grp = _grp(dm)

    def body(z_ref, w_ref, h_ref, mod_ref, ho_ref, y_ref):
        y = _dot(z_ref[...], w_ref[...])
        y_ref[...] = y.astype(BF16)
        ho_ref[...] = h_ref[...] + mod_ref[0, 5:6, :] * y

    row = pl.BlockSpec((tm, D), lambda i: (i, 0))
    return pl.pallas_call(
        body, name=name, grid=(T // tm,),
        in_specs=[pl.BlockSpec((tm, K), lambda i: (i, 0)), pl.BlockSpec((None, K, D), lambda i: (j, 0, 0)), row,
                  pl.BlockSpec((1, 9, D), lambda i: (grp(i), 0, 0))],
        out_specs=[row, row],
        out_shape=[jax.ShapeDtypeStruct((T, D), F32), jax.ShapeDtypeStruct((T, D), BF16)],
        compiler_params=_cparams("parallel"),
    )(z, w, h, mod)


def _out_bwd(dh, y, w, mod, j, dm, name):
    T, D = dh.shape
    K = w.shape[1]
    tm = dm.tm
    grp = _grp(dm)

    def body(dh_ref, y_ref, w_ref, mod_ref, dy_ref, dz_ref, part_ref):
        dhv = dh_ref[...]
        dy = (mod_ref[0, 5:6, :] * dhv).astype(BF16)
        dy_ref[...] = dy
        dz_ref[...] = _dot_nt(dy, w_ref[...])
        _write_part(part_ref, dgate=jnp.sum(dhv * y_ref[...].astype(F32), axis=0, keepdims=True))

    row = pl.BlockSpec((tm, D), lambda i: (i, 0))
    return pl.pallas_call(
        body, name=name, grid=(T // tm,),
        in_specs=[row, row, pl.BlockSpec((None, K, D), lambda i: (j, 0, 0)), pl.BlockSpec((1, 9, D), lambda i: (grp(i), 0, 0))],
        out_specs=[row, pl.BlockSpec((tm, K), lambda i: (i, 0)), pl.BlockSpec((1, 8, D), lambda i: (i, 0, 0))],
        out_shape=[jax.ShapeDtypeStruct((T, D), BF16), jax.ShapeDtypeStruct((T, K), F32),
                   jax.ShapeDtypeStruct((T // tm, 8, D), F32)],
        compiler_params=_cparams("parallel"),
    )(dh, y, w, mod)


def _sc_in_bwd(dh, dp, h, mod, g, w_in, j, dm, name):
    T, D = h.shape
    tm = dm.tm
    wq = D // N_CHIPS
    nq = 3 * N_CHIPS
    grp = _grp(dm)

    def body(dh_ref, dp_ref, h_ref, mod_ref, g_ref, w_ref, dho_ref, part_ref, acc):
        q = pl.program_id(1)

        @pl.when(q == 0)
        def _():
            acc[...] = jnp.zeros_like(acc)

        acc[...] += _dot_nt(dp_ref[...].astype(BF16), w_ref[...])

        @pl.when(q == nq - 1)
        def _():
            dhb, dshift, dscale, dg = _pre_bwd(acc[...], h_ref[...], g_ref[...], mod_ref[0, 4:5, :])
            dho_ref[...] = dh_ref[...] + dhb
            _write_part(part_ref, dshift, dscale, None, dg)

    row = pl.BlockSpec((tm, D), lambda i, q: (i, 0))
    return pl.pallas_call(
        body, name=name, grid=(T // tm, nq),
        in_specs=[row, pl.BlockSpec((None, tm, wq), lambda i, q: (q // N_CHIPS, i, q % N_CHIPS)), row,
                  pl.BlockSpec((1, 9, D), lambda i, q: (grp(i), 0, 0)), pl.BlockSpec((1, D), lambda i, q: (0, 0)),
                  pl.BlockSpec((None, None, D, wq), lambda i, q: (q // 3, j, 0, q % 3))],
        out_specs=[row, pl.BlockSpec((1, 8, D), lambda i, q: (i, 0, 0))],
        out_shape=[jax.ShapeDtypeStruct((T, D), F32), jax.ShapeDtypeStruct((T // tm, 8, D), F32)],
        scratch_shapes=[pltpu.VMEM((tm, D), F32)],
        compiler_params=_cparams("parallel", "arbitrary"),
    )(dh, dp, h, mod, g, w_in)


def _rope(t, c, s1, s2):
    return t * c + pltpu.roll(t, HEAD_PAD - 16, 1) * s1 + pltpu.roll(t, 16, 1) * s2


def _rope_t(dy, c, s1, s2):
    return dy * c + pltpu.roll(dy * s1, 16, 1) + pltpu.roll(dy * s2, HEAD_PAD - 16, 1)


def _mla_heads_fwd(z, g_ref, wuq_ref, wukv_ref):
    cq, ckv, krp = z[:, :Q_LORA], z[:, Q_LORA:Q_LORA + KV_LORA], z[:, Q_LORA + KV_LORA:]
    cqh, rq = _rms(cq, Q_LORA)
    ckvh, rkv = _rms(ckv, KV_LORA)
    cqn = (cqh * g_ref[0:1, :]).astype(BF16)
    ckvn = (ckvh * g_ref[1:2, :KV_LORA]).astype(BF16)
    qraw = _dot(cqn, wuq_ref[...])
    kvraw = _dot(ckvn, wukv_ref[...])
    return dict(krp=krp, cqh=cqh, rq=rq, ckvh=ckvh, rkv=rkv, cqn=cqn, ckvn=ckvn, qraw=qraw, kvraw=kvraw)


def _mla_proj_fwd(h, mod, g, gains, tabs, w_a, w_uq, w_ukv, j, dm, name):
    T, D = h.shape
    tm = min(dm.tm, 256)
    grp = lambda i: jnp.minimum(i // (dm.N // tm), dm.B)
    HP = HEAD_PAD

    def body(h_ref, mod_ref, g_ref, gn_ref, tab_ref, wa_ref, wuq_ref, wukv_ref, hn_ref, q_ref, k_ref, v_ref):
        hn = _pre(h_ref[...], g_ref[...], mod_ref[0, 3:4, :], mod_ref[0, 4:5, :]).astype(BF16)
        hn_ref[...] = hn
        f = _mla_heads_fwd(_dot(hn, wa_ref[...]), gn_ref, wuq_ref, wukv_ref)
        c, s1, s2 = tab_ref[0], tab_ref[1], tab_ref[2]
        for hd in range(HEADS):
            qh, _ = _rms(f["qraw"][:, hd * HP:(hd + 1) * HP], QK_HEAD)
            q_ref[:, hd * HP:(hd + 1) * HP] = (_rope(qh * gn_ref[2:3, :], c, s1, s2) * QK_SCALE).astype(BF16)
            kpre = jnp.concatenate([f["kvraw"][:, hd * HP:hd * HP + QK_NOPE], f["krp"]], axis=1)
            kh, _ = _rms(kpre, QK_HEAD)
            k_ref[:, hd * HP:(hd + 1) * HP] = _rope(kh * gn_ref[3:4, :], c, s1, s2).astype(BF16)
            v_ref[:, hd * V_HEAD:(hd + 1) * V_HEAD] = f["kvraw"][:, hd * HP + QK_NOPE:(hd + 1) * HP].astype(BF16)

    row = pl.BlockSpec((tm, D), lambda i: (i, 0))
    HQ = HEADS * HP
    return pl.pallas_call(
        body, name=name, grid=(T // tm,),
        in_specs=[row, pl.BlockSpec((1, 9, D), lambda i: (grp(i), 0, 0)), pl.BlockSpec((1, D), lambda i: (0, 0)),
                  pl.BlockSpec((None, 8, HP), lambda i: (j, 0, 0)), pl.BlockSpec((3, tm, HP), lambda i: (0, i, 0)),
                  pl.BlockSpec((None, D, 512), lambda i: (j, 0, 0)), pl.BlockSpec((None, Q_LORA, HQ), lambda i: (j, 0, 0)),
                  pl.BlockSpec((None, KV_LORA, HQ), lambda i: (j, 0, 0))],
        out_specs=[row, pl.BlockSpec((tm, HQ), lambda i: (i, 0)), pl.BlockSpec((tm, HQ), lambda i: (i, 0)),
                   pl.BlockSpec((tm, HEADS * V_HEAD), lambda i: (i, 0))],
        out_shape=[jax.ShapeDtypeStruct((T, D), BF16), jax.ShapeDtypeStruct((T, HQ), BF16),
                   jax.ShapeDtypeStruct((T, HQ), BF16), jax.ShapeDtypeStruct((T, HEADS * V_HEAD), BF16)],
        compiler_params=_cparams("parallel"),
    )(h, mod, g, gains, tabs, w_a, w_uq, w_ukv)


def _mla_proj_bwd(dh, dq, dkl, dkc, dvl, dvc, h, mod, g, gains, tabs, w_a, w_uq, w_ukv, j, dm, name):
    T, D = h.shape
    tm = min(dm.tm, 256)
    nblk = T // tm
    grp = lambda i: jnp.minimum(i // (dm.N // tm), dm.B)
    HP = HEAD_PAD
    HQ = HEADS * HP

    nlat = dm.B * dm.N // tm

    def body(dh_ref, dq_ref, dkl_ref, dkc_ref, dvl_ref, dvc_ref, h_ref, mod_ref, g_ref, gn_ref, tab_ref, wa_ref, wuq_ref, wukv_ref,
             dho_ref, part_ref, gwa_ref, gwuq_ref, gwukv_ref, dgn_ref, dqraw_s, dkvraw_s):
        i = pl.program_id(0)
        pick = lambda lat_ref, ctx_ref, cols: jnp.where(i < nlat, lat_ref[:, cols], ctx_ref[:, cols])

        @pl.when(i == 0)
        def _():
            gwa_ref[...] = jnp.zeros_like(gwa_ref)
            gwuq_ref[...] = jnp.zeros_like(gwuq_ref)
            gwukv_ref[...] = jnp.zeros_like(gwukv_ref)
            dgn_ref[...] = jnp.zeros_like(dgn_ref)

        hv = h_ref[...]
        hn = _pre(hv, g_ref[...], mod_ref[0, 3:4, :], mod_ref[0, 4:5, :]).astype(BF16)
        f = _mla_heads_fwd(_dot(hn, wa_ref[...]), gn_ref, wuq_ref, wukv_ref)
        c, s1, s2 = tab_ref[0], tab_ref[1], tab_ref[2]
        gq, gk = gn_ref[2:3, :], gn_ref[3:4, :]
        dgq = jnp.zeros((1, HP), F32)
        dgk = jnp.zeros((1, HP), F32)
        dkrp = jnp.zeros((tm, HP - QK_NOPE), F32)
        for hd in range(HEADS):
            qh, rq = _rms(f["qraw"][:, hd * HP:(hd + 1) * HP], QK_HEAD)
            dqn = _rope_t(dq_ref[:, hd * HP:(hd + 1) * HP] * QK_SCALE, c, s1, s2)
            dgq = dgq + jnp.sum(dqn * qh, axis=0, keepdims=True)
            dqraw_s[:, hd * HP:(hd + 1) * HP] = _rms_bwd(dqn * gq, qh, rq, QK_HEAD)
            kpre = jnp.concatenate([f["kvraw"][:, hd * HP:hd * HP + QK_NOPE], f["krp"]], axis=1)
            kh, rk = _rms(kpre, QK_HEAD)
            dkn = _rope_t(pick(dkl_ref, dkc_ref, slice(hd * HP, (hd + 1) * HP)), c, s1, s2)
            dgk = dgk + jnp.sum(dkn * kh, axis=0, keepdims=True)
            dkpre = _rms_bwd(dkn * gk, kh, rk, QK_HEAD)
            dkvraw_s[:, hd * HP:hd * HP + QK_NOPE] = dkpre[:, :QK_NOPE]
            dkrp = dkrp + dkpre[:, QK_NOPE:]
            dkvraw_s[:, hd * HP + QK_NOPE:(hd + 1) * HP] = pick(dvl_ref, dvc_ref, slice(hd * V_HEAD, (hd + 1) * V_HEAD))
        dqraw = dqraw_s[...].astype(BF16)
        dkvraw = dkvraw_s[...].astype(BF16)
        gwuq_ref[...] += _dot_tn(f["cqn"], dqraw)
        gwukv_ref[...] += _dot_tn(f["ckvn"], dkvraw)
        dcqn = _dot_nt(dqraw, wuq_ref[...])
        dckvn = _dot_nt(dkvraw, wukv_ref[...])
        dgqa = jnp.sum(dcqn * f["cqh"], axis=0, keepdims=True)
        dgkva = jnp.sum(dckvn * f["ckvh"], axis=0, keepdims=True)
        dcq = _rms_bwd(dcqn * gn_ref[0:1, :], f["cqh"], f["rq"], Q_LORA)
        dckv = _rms_bwd(dckvn * gn_ref[1:2, :KV_LORA], f["ckvh"], f["rkv"], KV_LORA)
        dz = jnp.concatenate([dcq, dckv, dkrp], axis=1).astype(BF16)
        gwa_ref[...] += _dot_tn(hn, dz)
        dhn = _dot_nt(dz, wa_ref[...])
        dhb, dshift, dscale, dg = _pre_bwd(dhn, hv, g_ref[...], mod_ref[0, 4:5, :])
        dho_ref[...] = dh_ref[...] + dhb
        _write_part(part_ref, dshift, dscale, None, dg)
        dgn_ref[0:1, :] += dgqa
        dgn_ref[1:2, :KV_LORA] += dgkva
        dgn_ref[2:3, :] += dgq
        dgn_ref[3:4, :] += dgk

    row = pl.BlockSpec((tm, D), lambda i: (i, 0))
    wide = pl.BlockSpec((tm, HQ), lambda i: (i, 0))
    const2 = lambda i: (0, 0)
    return pl.pallas_call(
        body, name=name, grid=(nblk,),
        in_specs=[row, wide, pl.BlockSpec((tm, HQ), lambda i: (jnp.minimum(i, nlat - 1), 0)),
                  pl.BlockSpec((tm, HQ), lambda i: (jnp.maximum(i - nlat, 0), 0)),
                  pl.BlockSpec((tm, HEADS * V_HEAD), lambda i: (jnp.minimum(i, nlat - 1), 0)),
                  pl.BlockSpec((tm, HEADS * V_HEAD), lambda i: (jnp.maximum(i - nlat, 0), 0)), row,
                  pl.BlockSpec((1, 9, D), lambda i: (grp(i), 0, 0)), pl.BlockSpec((1, D), const2),
                  pl.BlockSpec((None, 8, HP), lambda i: (j, 0, 0)), pl.BlockSpec((3, tm, HP), lambda i: (0, i, 0)),
                  pl.BlockSpec((None, D, 512), lambda i: (j, 0, 0)), pl.BlockSpec((None, Q_LORA, HQ), lambda i: (j, 0, 0)),
                  pl.BlockSpec((None, KV_LORA, HQ), lambda i: (j, 0, 0))],
        out_specs=[row, pl.BlockSpec((1, 8, D), lambda i: (i, 0, 0)), pl.BlockSpec((D, 512), const2),
                   pl.BlockSpec((Q_LORA, HQ), const2), pl.BlockSpec((KV_LORA, HQ), const2), pl.BlockSpec((8, HP), const2)],
        out_shape=[jax.ShapeDtypeStruct((T, D), F32), jax.ShapeDtypeStruct((nblk, 8, D), F32),
                   jax.ShapeDtypeStruct((D, 512), F32), jax.ShapeDtypeStruct((Q_LORA, HQ), F32),
                   jax.ShapeDtypeStruct((KV_LORA, HQ), F32), jax.ShapeDtypeStruct((8, HP), F32)],
        scratch_shapes=[pltpu.VMEM((tm, HQ), F32), pltpu.VMEM((tm, HQ), F32)],
        compiler_params=_cparams("arbitrary"),
    )(dh, dq, dkl, dkc, dvl, dvc, h, mod, g, gains, tabs, w_a, w_uq, w_ukv)


def _attn_specs(dm):
    tq = dm.CTX
    nq = dm.N // tq
    cblk0 = dm.B * nq
    HP = HEAD_PAD
    qrow = lambda b, i: jnp.where(i < nq, b * nq + i, cblk0 + b)
    return dict(
        tq=tq, nq=nq,
        q=pl.BlockSpec((tq, HP), lambda b, hd, i: (qrow(b, i), hd)),
        k_lat=pl.BlockSpec((dm.N, HP), lambda b, hd, i: (b, hd)),
        k_ctx=pl.BlockSpec((tq, HP), lambda b, hd, i: (cblk0 + b, hd)),
        v_lat=pl.BlockSpec((dm.N, V_HEAD), lambda b, hd, i: (b, hd)),
        v_ctx=pl.BlockSpec((tq, V_HEAD), lambda b, hd, i: (cblk0 + b, hd)),
        o=pl.BlockSpec((tq, V_HEAD), lambda b, hd, i: (qrow(b, i), hd)),
    )


def _attn_exp(q, keys, first_off=None):
    s = [_dot_nt(q, kk) for kk in keys]
    if first_off is not None:
        s[0] = s[0] + first_off
    m = functools.reduce(jnp.maximum, [jnp.max(x, axis=-1, keepdims=True) for x in s])
    e = [jnp.exp(x - m) for x in s]
    return e, 1.0 / sum(jnp.sum(x, axis=-1, keepdims=True) for x in e)


def _attn_fwd(q, k, v, dm, name, rider=None):
    T = dm.T
    sp = _attn_specs(dm)
    nq = sp["nq"]

    def body(q_ref, kl_ref, kc_ref, vl_ref, vc_ref, o_ref):
        i = pl.program_id(2)

        @pl.when(i < nq)
        def _():
            (el, ec), inv = _attn_exp(q_ref[...], [kl_ref[...], kc_ref[...]])
            o_ref[...] = ((_dot(el.astype(BF16), vl_ref[...]) + _dot(ec.astype(BF16), vc_ref[...])) * inv).astype(BF16)

        @pl.when(i == nq)
        def _():
            (ec,), inv = _attn_exp(q_ref[...], [kc_ref[...]])
            o_ref[...] = (_dot(ec.astype(BF16), vc_ref[...]) * inv).astype(BF16)

    (o,), got = _hosted(
        body, rider, name=name, grid=(dm.B, HEADS, nq + 1),
        in_specs=[sp["q"], sp["k_lat"], sp["k_ctx"], sp["v_lat"], sp["v_ctx"]], out_specs=[sp["o"]],
        out_shape=[jax.ShapeDtypeStruct((T, HEADS * V_HEAD), BF16)], scratch_shapes=[],
        sem=("parallel", "parallel", "arbitrary"), args=(q, k, k, v, v))
    return o, got


def _attn_bwd(q, k, v, o, do, dm, name):
    T = dm.T
    sp = _attn_specs(dm)
    nq, tq = sp["nq"], sp["tq"]
    HP, HQ, HV = HEAD_PAD, HEADS * HEAD_PAD, HEADS * V_HEAD

    def body(q_ref, kl_ref, kc_ref, vl_ref, vc_ref, o_ref, do_ref, dq_ref, dkl_ref, dkc_ref, dvl_ref, dvc_ref):
        i = pl.program_id(2)

        @pl.when(i == 0)
        def _():
            dkl_ref[...] = jnp.zeros_like(dkl_ref)
            dkc_ref[...] = jnp.zeros_like(dkc_ref)
            dvl_ref[...] = jnp.zeros_like(dvl_ref)
            dvc_ref[...] = jnp.zeros_like(dvc_ref)

        qv = q_ref[...]
        dov = do_ref[...]
        dob = dov.astype(BF16)
        delta = jnp.sum(dov * o_ref[...].astype(F32), axis=-1, keepdims=True)
        (el, ec), inv = _attn_exp(qv, [kl_ref[...], kc_ref[...]], jnp.where(i == nq, NEG, 0.0))
        pl_, pc = el * inv, ec * inv
        dsl = (pl_ * (_dot_nt(dob, vl_ref[...]) - delta)).astype(BF16)
        dsc = (pc * (_dot_nt(dob, vc_ref[...]) - delta)).astype(BF16)
        dq_ref[...] = _dot(dsl, kl_ref[...]) + _dot(dsc, kc_ref[...])
        dkl_ref[...] += _dot_tn(dsl, qv)
        dkc_ref[...] += _dot_tn(dsc, qv)
        dvl_ref[...] += _dot_tn(pl_.astype(BF16), dob)
        dvc_ref[...] += _dot_tn(pc.astype(BF16), dob)

    return pl.pallas_call(
        body, name=name, grid=(dm.B, HEADS, nq + 1),
        in_specs=[sp["q"], sp["k_lat"], sp["k_ctx"], sp["v_lat"], sp["v_ctx"], sp["o"], sp["o"]],
        out_specs=[sp["q"], sp["k_lat"], pl.BlockSpec((tq, HP), lambda b, hd, i: (b, hd)),
                   sp["v_lat"], pl.BlockSpec((tq, V_HEAD), lambda b, hd, i: (b, hd))],
        out_shape=[jax.ShapeDtypeStruct((T, HQ), F32), jax.ShapeDtypeStruct((dm.B * dm.N, HQ), F32),
                   jax.ShapeDtypeStruct((dm.B * dm.CTX, HQ), F32), jax.ShapeDtypeStruct((dm.B * dm.N, HV), F32),
                   jax.ShapeDtypeStruct((dm.B * dm.CTX, HV), F32)],
        compiler_params=_cparams("parallel", "parallel", "arbitrary"),
    )(q, k, k, v, v, o, do)


def _loss_grad(h, target, dm, name):
    T, D = h.shape
    tm = dm.tm
    nlat = dm.B * dm.N // tm

    def body(h_ref, t_ref, dh_ref, ls_ref):
        lat = (pl.program_id(0) < nlat).astype(F32)
        diff = (h_ref[...] - t_ref[...]) * lat
        dh_ref[...] = diff * (1.0 / D)
        ls_ref[...] = jnp.zeros(ls_ref.shape, F32) + (0.5 / D) * jnp.sum(diff * diff)

    return pl.pallas_call(
        body, name=name, grid=(T // tm,),
        in_specs=[pl.BlockSpec((tm, D), lambda i: (i, 0)), pl.BlockSpec((tm, D), lambda i: (jnp.minimum(i, nlat - 1), 0))],
        out_specs=[pl.BlockSpec((tm, D), lambda i: (i, 0)), pl.BlockSpec((1, 8, 128), lambda i: (i, 0, 0))],
        out_shape=[jax.ShapeDtypeStruct((T, D), F32), jax.ShapeDtypeStruct((T // tm, 8, 128), F32)],
        compiler_params=_cparams("parallel"),
    )(h, target)


def _col_block(cols, target=1152):
    return max(t for t in range(128, min(cols, target) + 1, 128) if cols % t == 0)


def _mod_fwd(cond, w_mod, b_mod, name):
    L, D, C = w_mod.shape
    R = cond.shape[0]
    cb = _col_block(C)

    def body(c_ref, w_ref, b_ref, o_ref):
        cv = c_ref[...]
        sc = (cv * jax.nn.sigmoid(cv)).astype(BF16)
        o_ref[...] = _dot(sc, w_ref[...].astype(BF16)) + b_ref[...]

    return pl.pallas_call(
        body, name=name, grid=(L, C // cb),
        in_specs=[pl.BlockSpec((R, D), lambda l, c: (0, 0)), pl.BlockSpec((None, D, cb), lambda l, c: (l, 0, c)),
                  pl.BlockSpec((None, 1, cb), lambda l, c: (l, 0, c))],
        out_specs=pl.BlockSpec((None, R, cb), lambda l, c: (l, 0, c)),
        out_shape=jax.ShapeDtypeStruct((L, R, C), F32),
        compiler_params=_cparams("parallel", "parallel"),
    )(cond, w_mod, b_mod)


def _mod_bwd(cond, dmod, w_mod, name):
    L, D, C = w_mod.shape
    R = cond.shape[0]
    cb = _col_block(C)
    nc = C // cb

    def body(c_ref, dm_ref, w_ref, gw_ref, ds_ref):
        cv = c_ref[...]
        sc = (cv * jax.nn.sigmoid(cv)).astype(BF16)
        dmv = dm_ref[...].astype(BF16)
        gw_ref[...] = _dot_tn(sc, dmv)
        part = _dot_nt(dmv, w_ref[...].astype(BF16))

        @pl.when(pl.program_id(1) == 0)
        def _():
            ds_ref[...] = part

        @pl.when(pl.program_id(1) > 0)
        def _():
            ds_ref[...] += part

    return pl.pallas_call(
        body, name=name, grid=(L, nc),
        in_specs=[pl.BlockSpec((R, D), lambda l, c: (0, 0)), pl.BlockSpec((None, R, cb), lambda l, c: (l, 0, c)),
                  pl.BlockSpec((None, D, cb), lambda l, c: (l, 0, c))],
        out_specs=[pl.BlockSpec((None, D, cb), lambda l, c: (l, 0, c)), pl.BlockSpec((None, R, D), lambda l, c: (l, 0, 0))],
        out_shape=[jax.ShapeDtypeStruct((L, D, C), F32), jax.ShapeDtypeStruct((L, R, D), F32)],
        compiler_params=_cparams("parallel", "arbitrary"),
    )(cond, dmod, w_mod)


def _row_block(rows, cols, budget=1 << 20):
    best = None
    for t in range(16, rows + 1, 16):
        if rows % t == 0 and t * cols * 4 <= budget:
            best = t
    return best if best is not None else rows


def _sum_slots(recvs, owns, chip, core, bufs, pieces, piece, name):
    n = len(recvs)
    S, R, C = recvs[0].shape
    tr = _row_block(R, C, budget=512 << 10)

    def body(ids_ref, *refs):
        for r_ref, p_ref, o_ref in zip(refs[:n], refs[n:2 * n], refs[-n:]):
            acc = None
            for s in range(S):
                v = jnp.where(ids_ref[0] == s, p_ref[s], r_ref[s]).astype(F32)
                acc = v if acc is None else acc + v
            o_ref[...] = acc

    blk = pl.BlockSpec((S, tr, C), lambda i, ids: (0, i, 0))
    held = [] if bufs is None else list(bufs)
    return pl.pallas_call(
        body, name=name,
        grid_spec=pltpu.PrefetchScalarGridSpec(
            num_scalar_prefetch=1, grid=(R // tr,), in_specs=[blk] * (2 * n) + [pl.BlockSpec(memory_space=pl.ANY)] * len(held),
            out_specs=[pl.BlockSpec((None, None, tr, C), lambda i, ids: (piece, ids[1], i, 0))] * n),
        out_shape=[jax.ShapeDtypeStruct((pieces, 2, R, C), F32)] * n,
        input_output_aliases={1 + 2 * n + i: i for i in range(len(held))}, compiler_params=_cparams("parallel"),
    )(jnp.stack([chip, core]).astype(jnp.int32), *recvs, *owns, *held)


def _adamw(w, gs, m, v, name, rider=None):
    ng = len(gs)
    R, C = w.shape
    tr = _row_block(R, C)
    c1 = 1.0 / (1.0 - ADAM_B1 ** ADAM_STEP)
    c2 = 1.0 / (1.0 - ADAM_B2 ** ADAM_STEP)

    def body(w_ref, *refs):
        m_ref, v_ref, g_ref, d_ref, mo_ref, vo_ref = refs[ng:]
        g = refs[0][...]
        for g_more in refs[1:ng]:
            g = g + g_more[...]
        g_ref[...] = g
        mn = ADAM_B1 * m_ref[...] + (1.0 - ADAM_B1) * g
        vn = ADAM_B2 * v_ref[...] + (1.0 - ADAM_B2) * (g * g)
        mo_ref[...] = mn
        vo_ref[...] = vn
        d_ref[...] = -ADAM_LR * ((mn * c1) / (jnp.sqrt(vn * c2) + ADAM_EPS) + ADAM_WD * w_ref[...])

    blk = pl.BlockSpec((tr, C), lambda i: (i, 0))
    sd = jax.ShapeDtypeStruct((R, C), F32)
    return _hosted(body, rider, name=name, grid=(R // tr,), in_specs=[blk] * (3 + ng), out_specs=[blk] * 4, out_shape=[sd] * 4,
                   scratch_shapes=[], sem=("parallel",), args=(w, *gs, m, v))


def _rope_tables(dm):
    n = dm.N
    t = jnp.arange(n)
    r = (t // GRID_W).astype(F32)
    col = (t % GRID_W).astype(F32)
    nf = QK_ROPE // 4
    inv = ROPE_BASE ** (-jnp.arange(nf, dtype=F32) / nf)
    ang = jnp.stack([r[:, None] * inv, col[:, None] * inv], axis=1)
    cos, sin = jnp.cos(ang), jnp.sin(ang)
    zero = jnp.zeros_like(sin)
    c64 = jnp.stack([cos, cos], axis=2).reshape(n, QK_ROPE)
    s1 = jnp.stack([-sin, zero], axis=2).reshape(n, QK_ROPE)
    s2 = jnp.stack([zero, sin], axis=2).reshape(n, QK_ROPE)

    def pad(x, fill):
        return jnp.concatenate([jnp.full((n, QK_NOPE), fill, F32), x, jnp.full((n, HEAD_PAD - QK_HEAD), fill, F32)], axis=1)

    lat = jnp.stack([pad(c64, 1.0), pad(s1, 0.0), pad(s2, 0.0)])
    lat = jnp.tile(lat, (1, dm.B, 1))
    nctx = dm.B * dm.CTX
    ctx = jnp.stack([jnp.ones((nctx, HEAD_PAD), F32), jnp.zeros((nctx, HEAD_PAD), F32), jnp.zeros((nctx, HEAD_PAD), F32)])
    return jnp.concatenate([lat, ctx], axis=1)


def _fold_parts(part, dm):
    nblk = part.shape[0]
    nb = (dm.N * nblk) // dm.T
    groups = [part[b * nb:(b + 1) * nb].sum(axis=0) for b in range(dm.B)]
    groups.append(part[dm.B * nb:].sum(axis=0))
    return jnp.stack(groups)


def grouped(items, key):
    groups = {}
    for it in items:
        groups.setdefault(key(it), []).append(it)
    return list(groups.values())


def _flat2(a):
    return a.reshape(-1, a.shape[-1])


def kernel(x, c, ctx, c_ctx, w_mod, b_mod, g_norm, ffn_w1, ffn_w3, ffn_w2, sc_w_in, sc_conv, sc_w_out, mla_w_a, mla_g_qa, mla_w_uq, mla_g_kva, mla_w_ukv, mla_g_q, mla_g_k, mla_w_o, loss_target, m_c_ctx, m_w_mod, m_b_mod, m_g_norm, m_ffn_w1, m_ffn_w3, m_ffn_w2, m_sc_w_in, m_sc_conv, m_sc_w_out, m_mla_w_a, m_mla_g_qa, m_mla_w_uq, m_mla_g_kva, m_mla_w_ukv, m_mla_g_q, m_mla_g_k, m_mla_w_o, v_c_ctx, v_w_mod, v_b_mod, v_g_norm, v_ffn_w1, v_ffn_w3, v_ffn_w2, v_sc_w_in, v_sc_conv, v_sc_w_out, v_mla_w_a, v_mla_g_qa, v_mla_w_uq, v_mla_g_kva, v_mla_w_ukv, v_mla_g_q, v_mla_g_k, v_mla_w_o):
    B, N, D = x.shape
    CTX = ctx.shape[1]
    T = B * (N + CTX)
    tm = next(t for t in (512, 256, 128, 64, 32, 16) if N % t == 0 and (B * CTX) % t == 0)
    dm = Dims(B, N, CTX, D, T, tm)
    L = w_mod.shape[0]
    La, Lb = sc_w_in.shape[0], mla_w_a.shape[0]
    S = N_CHIPS
    ndev = 2 * S
    xi, yi, ci = lax.axis_index("x"), lax.axis_index("y"), lax.axis_index("c")
    chip = 2 * xi + yi
    dev = 2 * chip + ci
    weights = dict(c_ctx=c_ctx, w_mod=w_mod, b_mod=b_mod, g_norm=g_norm, ffn_w1=ffn_w1, ffn_w3=ffn_w3, ffn_w2=ffn_w2,
                   sc_w_in=sc_w_in, sc_conv=sc_conv, sc_w_out=sc_w_out, mla_w_a=mla_w_a, mla_g_qa=mla_g_qa,
                   mla_w_uq=mla_w_uq, mla_g_kva=mla_g_kva, mla_w_ukv=mla_w_ukv, mla_g_q=mla_g_q, mla_g_k=mla_g_k,
                   mla_w_o=mla_w_o)
    mom = dict(c_ctx=(m_c_ctx, v_c_ctx), w_mod=(m_w_mod, v_w_mod), b_mod=(m_b_mod, v_b_mod), g_norm=(m_g_norm, v_g_norm),
               ffn_w1=(m_ffn_w1, v_ffn_w1), ffn_w3=(m_ffn_w3, v_ffn_w3), ffn_w2=(m_ffn_w2, v_ffn_w2),
               sc_w_in=(m_sc_w_in, v_sc_w_in), sc_conv=(m_sc_conv, v_sc_conv), sc_w_out=(m_sc_w_out, v_sc_w_out),
               mla_w_a=(m_mla_w_a, v_mla_w_a), mla_g_qa=(m_mla_g_qa, v_mla_g_qa), mla_w_uq=(m_mla_w_uq, v_mla_w_uq),
               mla_g_kva=(m_mla_g_kva, v_mla_g_kva), mla_w_ukv=(m_mla_w_ukv, v_mla_w_ukv), mla_g_q=(m_mla_g_q, v_mla_g_q),
               mla_g_k=(m_mla_g_k, v_mla_g_k), mla_w_o=(m_mla_w_o, v_mla_w_o))

    big = ["ffn_w1", "ffn_w3", "ffn_w2", "sc_w_in", "sc_w_out", "mla_w_a", "mla_w_uq", "mla_w_ukv", "mla_w_o"]
    F = ffn_w1.shape[-1]
    transposed = ("ffn_w1", "ffn_w3")
    for n in transposed:
        weights[n] = jnp.swapaxes(weights[n], 2, 3)
        mom[n] = tuple(jnp.swapaxes(a, 2, 3) for a in mom[n])
    mixer_names =(["sc_w_in", "sc_w_out"], ["mla_w_a", "mla_w_uq", "mla_w_ukv", "mla_w_o"])

    def placed(name, piece, npieces):
        w2 = _flat2(weights[name])
        rows = w2.shape[0] // npieces
        return _place_cast(w2, piece * rows, rows, chip, S, "place_weight")

    bufs = {}
    for l in range(L):
        for k in range(2):
            bufs["f", l, k] = {n: placed(n, 2 * l + k, 2 * L) for n in ("ffn_w1", "ffn_w3", "ffn_w2")}
        bufs["m", l] = {n: placed(n, l // 2, weights[n].shape[0]) for n in mixer_names[l % 2]}
    order = [stage for l in range(L) for stage in (("f", l, 0), ("m", l), ("f", l, 1))]

    def gather_after(stage):
        at = order.index(stage)
        if at + 1 == len(order):
            return None, lambda got: None
        nxt = bufs[order[at + 1]]
        names = list(nxt)
        return _gather_rider([nxt[n] for n in names]), lambda got: nxt.update(zip(names, got))

    names = list(bufs[order[0]])
    bufs[order[0]].update(zip(names, _ride_alone(_gather_rider([bufs[order[0]][n] for n in names]), "gather_weights")))

    def ffn_weights(l, k):
        b = bufs["f", l, k]
        return b["ffn_w1"], b["ffn_w3"], b["ffn_w2"]

    def mixer_weights(l):
        b = bufs["m", l]
        w = {}
        if l % 2 == 0:
            w["w_in"] = b["sc_w_in"][:, None]
            w["w_out"] = b["sc_w_out"].reshape(1, D, D)
        else:
            w["w_a"] = jnp.pad(b["mla_w_a"].reshape(1, D, -1), ((0, 0), (0, 0), (0, 512 - (Q_LORA + KV_LORA + QK_ROPE))))
            wuq = jnp.moveaxis(b["mla_w_uq"], 0, 1).reshape(1, Q_LORA, HEADS, QK_HEAD)
            w["w_uq"] = jnp.pad(wuq, ((0, 0), (0, 0), (0, 0), (0, HEAD_PAD - QK_HEAD))).reshape(1, Q_LORA, HEADS * HEAD_PAD)
            w["w_ukv"] = jnp.moveaxis(b["mla_w_ukv"], 0, 1).reshape(1, KV_LORA, HEADS * HEAD_PAD)
            w["w_o"] = b["mla_w_o"].reshape(1, HEADS * V_HEAD, D)
        return w

    vecs = ["g_norm", "sc_conv", "mla_g_qa"]
    gathered = _exchange([_flat2(weights[n]) for n in vecs], ("x", "y"), False, "gather_vectors")
    gw = {n: g.reshape((S,) + weights[n].shape) for n, g in zip(vecs, gathered)}
    gnorm = jnp.moveaxis(gw["g_norm"], 0, 2).reshape(L, 3, D)
    convw = jnp.moveaxis(gw["sc_conv"], 0, 2).reshape(La, 3, D)
    gqa = jnp.moveaxis(gw["mla_g_qa"], 0, 1).reshape(Lb, Q_LORA)
    padl = lambda a: jnp.pad(a, ((0, 0), (0, HEAD_PAD - a.shape[1])))
    gains = jnp.stack([padl(gqa), padl(mla_g_kva), padl(mla_g_q), padl(mla_g_k)], axis=1)
    gains = jnp.pad(gains, ((0, 0), (0, 4), (0, 0)))

    R = -(-(ndev * B + 1) // 16) * 16
    call = _exchange([c], ("x", "y", "c"), False, "gather_cond")[0].reshape(ndev * B, D)
    cond = jnp.concatenate([call, c_ctx[None], jnp.zeros((R - ndev * B - 1, D), F32)], axis=0)
    C = w_mod.shape[-1]
    bm = lax.dynamic_slice_in_dim(b_mod, chip * C, C, axis=1)[:, None, :]
    mshard = _mod_fwd(cond, w_mod, bm, "mod_fwd")
    mfull = _exchange([mshard.reshape(L * R, C)], ("x", "y"), False, "gather_mod")[0].reshape(S, L, R, C)
    mfull = jnp.moveaxis(mfull, 0, 2).reshape(L, R, S * C)
    mine = lax.dynamic_slice_in_dim(mfull, dev * B, B, axis=1)
    mod = jnp.concatenate([mine, mfull[:, ndev * B:ndev * B + 1]], axis=1).reshape(L, B + 1, 9, D)

    tabs = _rope_tables(dm)
    h = jnp.concatenate([x.reshape(B * N, D), ctx.reshape(B * CTX, D)], axis=0)

    saved = []
    lw = [None] * L
    for l in range(L):
        kind, j = l % 2, l // 2
        sv = {}
        sv["h0"] = h
        rider, keep = gather_after(("f", l, 0))
        (h, sv["a1"], sv["b1"], sv["hn1"], sv["y1"]), got = _ffn_fwd(h, mod[l], gnorm[l, 0:1], *ffn_weights(l, 0), 0, dm,
                                                                      "ffn_fwd", rider)
        keep(got)
        sv["h1"] = h
        W = lw[l] = mixer_weights(l)
        rider, keep = gather_after(("m", l))
        if kind == 0:
            (sv["p"], sv["hnm"]), got = _sc_in_fwd(h, mod[l], gnorm[l, 1:2], W["w_in"], 0, dm, "sc_in_fwd", rider)
            sv["z"] = _conv_fwd(sv["p"], convw[j], dm, "conv_fwd")
            h, sv["ym"] = _out_fwd(sv["z"], W["w_out"], h, mod[l], 0, dm, "sc_out_fwd")
        else:
            sv["hnm"], sv["q"], sv["k"], sv["v"] = _mla_proj_fwd(h, mod[l], gnorm[l, 1:2], gains[j:j + 1], tabs, W["w_a"], W["w_uq"],
                                                                 W["w_ukv"], 0, dm, "mla_proj_fwd")
            sv["o"], got = _attn_fwd(sv["q"], sv["k"], sv["v"], dm, "attn_fwd", rider)
            h, sv["ym"] = _out_fwd(sv["o"], W["w_o"], h, mod[l], 0, dm, "mla_out_fwd")
        keep(got)
        sv["h2"] = h
        rider, keep = gather_after(("f", l, 1))
        (h, sv["a2"], sv["b2"], sv["hn2"], sv["y2"]), got = _ffn_fwd(h, mod[l], gnorm[l, 2:3], *ffn_weights(l, 1), 1, dm,
                                                                      "ffn_fwd", rider)
        keep(got)
        saved.append(sv)

    dh, lsum = _loss_grad(h, loss_target.reshape(B * N, D), dm, "loss_grad")
    loss = lax.psum(jnp.sum(lsum[:, 0, 0]), ("x", "y", "c"))

    wq = D // S
    gsum = {n: None for n in big}
    npieces = {n: weights[n].shape[0] * (weights[n].shape[1] if n.startswith("ffn") else 1) for n in big}
    dmod = [None] * L
    dgn = [None] * L
    dconv = [None] * La
    dgains = [None] * Lb
    tk = tm * next(f for f in (3, 2, 1) if (T // tm) % f == 0)
    nk = T // tk
    full_a = pl.BlockSpec((tk, D), lambda s, kk: (kk, 0))
    shard_b = pl.BlockSpec((None, tk, F), lambda s, kk: (s, kk, 0))
    per_slot = lambda r_, c_: pl.BlockSpec((None, r_, c_), lambda s, kk: (s, 0, 0))

    def make_job(grads):
        parts = [g_ for _, _, g_ in grads]
        theirs = _swap_halves(parts, "swap_halves")
        pairs = [None] * len(grads)
        for idx in grouped(range(len(grads)), lambda i: parts[i].shape):
            outs = _pair_sum([parts[i] for i in idx], [theirs[i] for i in idx], ci, "pair_sum")
            for i, o in zip(idx, outs):
                pairs[i] = o
        return [(n, p, pair) for (n, p, _), pair in zip(grads, pairs)]

    def finish_job(job, recv):
        key = lambda i: (recv[i].shape, npieces[job[i][0]], job[i][1], gsum[job[i][0]] is None)
        for idx in grouped(range(len(job)), key):
            names_ = [job[i][0] for i in idx]
            held = None if gsum[names_[0]] is None else [gsum[n] for n in names_]
            outs = _sum_slots([recv[i] for i in idx], [job[i][2] for i in idx], chip, ci, held, npieces[names_[0]],
                              job[idx[0]][1], "sum_slots")
            gsum.update(zip(names_, outs))

    def ffn_back(dh, sv, l, k, job):
        sfx = "1" if k == 0 else "2"
        rider = _scatter_rider([pair for _, _, pair in job]) if job else None
        (dh, da, db, sw, dy, part), recv = _ffn_bwd(dh, sv["h0" if k == 0 else "h2"], mod[l], gnorm[l, 2 * k:2 * k + 1], sv["y" + sfx],
                                                    sv["a" + sfx], sv["b" + sfx], *ffn_weights(l, k), k, dm, "ffn_bwd", rider)
        finish_job(job, recv)
        g1 = _mm_tn(da, sv["hn" + sfx], shard_b, full_a, (S, F, D), per_slot(F, D), (S, nk), "gw1")
        g3 = _mm_tn(db, sv["hn" + sfx], shard_b, full_a, (S, F, D), per_slot(F, D), (S, nk), "gw3")
        g2 = _mm_tn(sw, dy, shard_b, full_a, (S, F, D), per_slot(F, D), (S, nk), "gw2")
        p = 2 * l + k
        return dh, _fold_parts(part, dm), [("ffn_w1", p, g1), ("ffn_w3", p, g3), ("ffn_w2", p, g2)]

    one = (1, nk)
    a1 = lambda kdim: pl.BlockSpec((tk, kdim), lambda s, kk: (kk, 0))
    pending = []
    for l in reversed(range(L)):
        kind, j = l % 2, l // 2
        sv = saved[l]
        W = lw[l]
        dh, p2, grads = ffn_back(dh, sv, l, 1, pending)
        job2 = make_job(grads)
        if kind == 0:
            dy, dz, pg = _out_bwd(dh, sv["ym"], W["w_out"], mod[l], 0, dm, "sc_out_bwd")
            g_out = _mm_tn(sv["z"], dy, a1(D), a1(D), (1, D, D), per_slot(D, D), one, "gw_sc_out")
            dp, dconv[j] = _conv_bwd(dz, sv["p"], convw[j], dm, "conv_bwd")
            g_in = _mm_tn(sv["hnm"], dp, pl.BlockSpec((tk, D), lambda q, kk: (kk, 0)),
                          pl.BlockSpec((None, tk, wq), lambda q, kk: (q // S, kk, q % S)), (3 * S, D, wq), per_slot(D, wq),
                          (3 * S, nk), "gw_sc_in")
            dh, pm = _sc_in_bwd(dh, dp, sv["h1"], mod[l], gnorm[l, 1:2], W["w_in"], 0, dm, "sc_in_bwd")
            grads = [("sc_w_in", j, jnp.moveaxis(g_in.reshape(S, 3, D, wq), 1, 2).reshape(S, D, 3 * wq)),
                     ("sc_w_out", j, g_out.reshape(S, D // S, D))]
        else:
            dy, do, pg = _out_bwd(dh, sv["ym"], W["w_o"], mod[l], 0, dm, "mla_out_bwd")
            g_o = _mm_tn(sv["o"], dy, a1(HEADS * V_HEAD), a1(D), (1, HEADS * V_HEAD, D), per_slot(HEADS * V_HEAD, D), one, "gw_mla_o")
            dq, dkl, dkc, dvl, dvc = _attn_bwd(sv["q"], sv["k"], sv["v"], sv["o"], do, dm, "attn_bwd")
            dh, pm, g_a, g_uq, g_ukv, dgains[j] = _mla_proj_bwd(
                dh, dq, dkl, dkc, dvl, dvc, sv["h1"], mod[l], gnorm[l, 1:2], gains[j:j + 1], tabs, W["w_a"], W["w_uq"], W["w_ukv"], 0, dm, "mla_proj_bwd")
            g_uq = g_uq.reshape(Q_LORA, HEADS, HEAD_PAD)[..., :QK_HEAD].reshape(Q_LORA, S, -1)
            grads = [("mla_w_a", j, g_a[:, :Q_LORA + KV_LORA + QK_ROPE].reshape(S, D // S, -1).astype(BF16)),
                     ("mla_w_uq", j, jnp.moveaxis(g_uq, 1, 0).astype(BF16)),
                     ("mla_w_ukv", j, jnp.moveaxis(g_ukv.reshape(KV_LORA, S, -1), 1, 0).astype(BF16)),
                     ("mla_w_o", j, g_o.reshape(S, HEADS * V_HEAD // S, D))]
        jobm = make_job(grads)
        pm = _fold_parts(pm, dm) + _fold_parts(pg, dm)
        dh, p0, grads = ffn_back(dh, sv, l, 0, job2 + jobm)
        pending = make_job(grads)
        dmod[l] = jnp.concatenate([p0[:, 0:3], pm[:, 0:3], p2[:, 0:3]], axis=1).reshape(B + 1, 9 * D)
        dgn[l] = jnp.stack([p0[:, 3].sum(0), pm[:, 3].sum(0), p2[:, 3].sum(0)])
    grad_x = dh[:B * N].reshape(B, N, D)

    dgains_a = jnp.stack(dgains)
    small = [jnp.stack(dmod).reshape(-1), jnp.stack(dgn).reshape(-1), jnp.stack(dconv).reshape(-1), dgains_a.reshape(-1)]
    sizes = [s_.shape[0] for s_ in small]
    flat = jnp.concatenate(small)
    pad = (-flat.shape[0]) % 1024
    flat = jnp.pad(flat, (0, pad)).reshape(-1, 128)
    allsmall = _exchange([flat], ("x", "y", "c"), False, "gather_small")[0].reshape(ndev, -1)
    offs = [0]
    for s_ in sizes:
        offs.append(offs[-1] + s_)
    dmod_all = allsmall[:, offs[0]:offs[1]].reshape(ndev, L, B + 1, 9 * D)
    tot = allsmall[:, offs[1]:offs[4]].sum(axis=0)
    g_gnorm = tot[:offs[2] - offs[1]].reshape(L, 3, D)
    g_conv = tot[offs[2] - offs[1]:offs[3] - offs[1]].reshape(La, 3, D)
    g_gains = tot[offs[3] - offs[1]:].reshape(Lb, 8, HEAD_PAD)
    dM = jnp.concatenate([jnp.moveaxis(dmod_all[:, :, :B], 0, 1).reshape(L, ndev * B, 9 * D),
                          dmod_all[:, :, B].sum(axis=0)[:, None, :], jnp.zeros((L, R - ndev * B - 1, 9 * D), F32)], axis=1)
    g_bmod = dM.sum(axis=1)
    dM_mine = lax.dynamic_slice_in_dim(dM, chip * C, C, axis=2)
    g_wmod, dsil = _mod_bwd(cond, dM_mine, w_mod, "mod_bwd")
    dsil_ctx = dsil[:, ndev * B].sum(axis=0)
    dsil_all = _exchange([jnp.pad(dsil_ctx.reshape(-1, 128), ((0, (-(D // 128)) % 8), (0, 0)))], ("x", "y"), False, "gather_dctx")[0]
    dsil_tot = dsil_all.sum(axis=0)[:D // 128].reshape(D)
    sg = jax.nn.sigmoid(c_ctx)
    g_cctx = dsil_tot * (sg * (1.0 + c_ctx * (1.0 - sg)))

    chip_cols = lambda a, width: lax.dynamic_slice_in_dim(a, chip * width, width, axis=a.ndim - 1)
    small_grads = dict(
        c_ctx=g_cctx, b_mod=g_bmod, g_norm=chip_cols(g_gnorm, D // S), sc_conv=chip_cols(g_conv, D // S),
        mla_g_qa=chip_cols(g_gains[:, 0, :Q_LORA], Q_LORA // S), mla_g_kva=g_gains[:, 1, :KV_LORA],
        mla_g_q=g_gains[:, 2, :QK_HEAD], mla_g_k=g_gains[:, 3, :QK_HEAD])

    grads, deltas, new_m, new_v = {}, {}, {}, {}
    for n in ["w_mod"] + [n_ for n_ in weights if n_ != "w_mod"]:
        w = weights[n]
        shape = w.shape
        w2 = _flat2(w) if w.ndim > 1 else w.reshape(1, -1)
        m2, v2 = (a.reshape(w2.shape) for a in mom[n])
        if n == "w_mod":
            (g_, d_, m_, v_), recv = _adamw(w2, [_flat2(g_wmod)], m2, v2, "adamw", _scatter_rider([pair for _, _, pair in pending]))
            finish_job(pending, recv)
            gsum = dict(zip(big, _swap_cores_inplace([gsum[n_] for n_ in big], "swap_cores")))
        else:
            gs = [gsum[n].reshape(w2.shape)] if n in gsum else [small_grads[n].reshape(w2.shape)]
            (g_, d_, m_, v_), _ = _adamw(w2, gs, m2, v2, "adamw")
        grads[n], deltas[n], new_m[n], new_v[n] = (a.reshape(shape) for a in (g_, d_, m_, v_))
    for n in transposed:
        grads[n], deltas[n], new_m[n], new_v[n] = (jnp.swapaxes(a, 2, 3) for a in (grads[n], deltas[n], new_m[n], new_v[n]))

    names = list(weights)
    return (loss, grad_x, *[grads[n] for n in names], *[deltas[n] for n in names], *[new_m[n] for n in names],
            *[new_v[n] for n in names])
```

```python
import functools
import math
from typing import NamedTuple

import jax
import jax.numpy as jnp
from jax import lax
from jax.experimental import pallas as pl
from jax.experimental.pallas import tpu as pltpu

F32 = jnp.float32
BF16 = jnp.bfloat16
EPS = 1e-6
GRID_W = 64
HEADS = 8
QK_NOPE = 128
QK_ROPE = 64
QK_HEAD = QK_NOPE + QK_ROPE
HEAD_PAD = 256
V_HEAD = 128
Q_LORA = 256
KV_LORA = 128
ROPE_BASE = 10000.0
QK_SCALE = QK_HEAD ** -0.5
ADAM_LR, ADAM_B1, ADAM_B2, ADAM_EPS, ADAM_WD, ADAM_STEP = 0.001, 0.9, 0.999, 1e-08, 0.01, 10
N_CHIPS = 4
VMEM_LIMIT = 56 * 1024 * 1024
MESH = pl.DeviceIdType.MESH
NEG = -1e30


class Dims(NamedTuple):
    B: int
    N: int
    CTX: int
    D: int
    T: int
    tm: int


def _cparams(*sem):
    return pltpu.CompilerParams(dimension_semantics=sem if sem else None, vmem_limit_bytes=VMEM_LIMIT)


def _dot(a, b):
    return jnp.dot(a, b, preferred_element_type=F32)


def _dot_nt(a, b):
    return lax.dot_general(a, b, (((1,), (1,)), ((), ())), preferred_element_type=F32)


def _dot_tn(a, b):
    return lax.dot_general(a, b, (((0,), (0,)), ((), ())), preferred_element_type=F32)


def _rms(x, n):
    r = lax.rsqrt(jnp.sum(x * x, axis=-1, keepdims=True) * (1.0 / n) + EPS)
    return x * r, r


def _rms_bwd(dxh, xh, r, n):
    return r * (dxh - xh * (jnp.sum(dxh * xh, axis=-1, keepdims=True) * (1.0 / n)))


def _pre(h, g, shift, scale):
    xh, _ = _rms(h, h.shape[-1])
    return (xh * g) * (1.0 + scale) + shift


def _pre_bwd(dout, h, g, scale):
    d = h.shape[-1]
    xh, r = _rms(h, d)
    n = xh * g
    dshift = jnp.sum(dout, axis=0, keepdims=True)
    dscale = jnp.sum(dout * n, axis=0, keepdims=True)
    dn = dout * (1.0 + scale)
    dg = jnp.sum(dn * xh, axis=0, keepdims=True)
    dh = _rms_bwd(dn * g, xh, r, d)
    return dh, dshift, dscale, dg


def _write_part(part_ref, dshift=None, dscale=None, dgate=None, dg=None):
    z = jnp.zeros((1, part_ref.shape[-1]), F32)
    part_ref[0, 0:1, :] = z if dshift is None else dshift
    part_ref[0, 1:2, :] = z if dscale is None else dscale
    part_ref[0, 2:3, :] = z if dgate is None else dgate
    part_ref[0, 3:4, :] = z if dg is None else dg
    part_ref[0, 4:8, :] = jnp.zeros((4, part_ref.shape[-1]), F32)


def _grp(dm):
    nb = dm.N // dm.tm
    return lambda i: jnp.minimum(i // nb, dm.B)


def _n_chunks(rows, row_bytes):
    n = 16
    while n > 1 and (rows % (16 * n) or (rows // n) * row_bytes < (256 << 10)):
        n //= 2
    return n


def _start_local(src, dst, sems, k0, nchunk):
    ch = src.shape[0] // nchunk
    copies = []
    for j in range(nchunk):
        cp = pltpu.make_async_copy(src.at[pl.ds(j * ch, ch)], dst.at[pl.ds(j * ch, ch)], sems.at[k0 + j])
        cp.start()
        copies.append(cp)
    return copies


def _exchange(arrs, axes, scatter, name, own="copy"):
    n = len(arrs)
    nbits = len(axes)
    slots = 2 ** nbits
    pats = list(range(1, slots))
    inplace = own == "inplace"
    nck = [_n_chunks(a.shape[-2], a.shape[-1] * a.dtype.itemsize) for a in arrs]
    base = [sum(nck[:i]) * len(pats) for i in range(n)]
    nsem = sum(nck) * len(pats)

    def body(*refs):
        ins, outs = refs[:n], refs[n:2 * n]
        send, recv, loc = refs[2 * n:]
        pos = {a: lax.axis_index(a) for a in ("x", "y", "c")}

        def slot_of(p):
            s = 0
            for a in axes:
                s = 2 * s + p[a]
            return s

        me = slot_of(pos)
        local = []
        for i in range(n):
            if own == "copy":
                local += _start_local(ins[i].at[me] if scatter else ins[i], outs[i].at[me], loc, sum(nck[:i]), nck[i])
        remote = []
        for pi, pat in enumerate(pats):
            peer = dict(pos)
            for bi, a in enumerate(axes):
                if (pat >> (nbits - 1 - bi)) & 1:
                    peer[a] = 1 - pos[a]
            them = slot_of(peer)
            for i in range(n):
                ch = arrs[i].shape[-2] // nck[i]
                for j in range(nck[i]):
                    k = base[i] + pi * nck[i] + j
                    rs = pl.ds(j * ch, ch)
                    if inplace:
                        src = outs[i].at[me, rs]
                    else:
                        src = ins[i].at[them, rs] if scatter else ins[i].at[rs]
                    cp = pltpu.make_async_remote_copy(
                        src_ref=src, dst_ref=outs[i].at[me, rs], send_sem=send.at[k], recv_sem=recv.at[k],
                        device_id=(peer["x"], peer["y"], peer["c"]), device_id_type=MESH)
                    cp.start()
                    remote.append(cp)
        for cp in local:
            cp.wait()
        for cp in remote:
            cp.wait()

    out_shape = [jax.ShapeDtypeStruct(a.shape if (scatter or inplace) else (slots,) + a.shape, a.dtype) for a in arrs]
    any_spec = pl.BlockSpec(memory_space=pl.ANY)
    outs = pl.pallas_call(
        body, name=name, out_shape=out_shape, in_specs=[any_spec] * n, out_specs=[any_spec] * n,
        scratch_shapes=[pltpu.SemaphoreType.DMA((nsem,)), pltpu.SemaphoreType.DMA((nsem,)), pltpu.SemaphoreType.DMA((sum(nck),))],
        input_output_aliases={i: i for i in range(n)} if inplace else {},
        compiler_params=pltpu.CompilerParams(has_side_effects=True),
    )(*arrs)
    return list(outs)


class Rider(NamedTuple):
    ins: list
    out_shapes: list
    aliases: dict
    sems: list
    start: object
    mid: object
    end: object


MID_STEPS = 6


def _hosted(body, rider, *, name, grid, in_specs, out_specs, out_shape, scratch_shapes, sem, args):
    if rider is None:
        outs = pl.pallas_call(body, name=name, grid=grid, in_specs=in_specs, out_specs=out_specs, out_shape=out_shape,
                              scratch_shapes=scratch_shapes, compiler_params=_cparams(*sem))(*args)
        return outs, []
    n_in, n_out, n_s = len(in_specs), len(out_specs), len(scratch_shapes)
    nri, nro = len(rider.ins), len(rider.out_shapes)
    nsteps = math.prod(grid)

    def wrapped(*refs):
        bounds = [0, n_in, n_in + nri, n_in + nri + n_out, n_in + nri + n_out + nro, n_in + nri + n_out + nro + n_s, len(refs)]
        ins, rins, outs, routs, scr, sems = (refs[lo:hi] for lo, hi in zip(bounds[:-1], bounds[1:]))
        step = 0
        for ax, extent in enumerate(grid):
            step = step * extent + pl.program_id(ax)

        @pl.when(step == 0)
        def _():
            rider.start(rins, routs, sems)

        body(*ins, *outs, *scr)

        if rider.mid is not None:
            @pl.when(step == max(nsteps - 1 - MID_STEPS, 0))
            def _():
                rider.mid(rins, routs, sems)

        @pl.when(step == nsteps - 1)
        def _():
            rider.end(rins, routs, sems)

    any_spec = pl.BlockSpec(memory_space=pl.ANY)
    outs = pl.pallas_call(
        wrapped, name=name, grid=grid, in_specs=list(in_specs) + [any_spec] * nri, out_specs=list(out_specs) + [any_spec] * nro,
        out_shape=list(out_shape) + list(rider.out_shapes), scratch_shapes=list(scratch_shapes) + list(rider.sems),
        input_output_aliases={n_in + i: n_out + o for i, o in rider.aliases.items()},
        compiler_params=pltpu.CompilerParams(dimension_semantics=("arbitrary",) * len(grid), vmem_limit_bytes=VMEM_LIMIT,
                                             has_side_effects=True),
    )(*args, *rider.ins)
    return outs[:n_out], list(outs[n_out:])


def _gather_rider(bufs):
    n = len(bufs)
    halves = [a.shape[1] // 2 for a in bufs]
    nck = [_n_chunks(h, a.shape[2] * a.dtype.itemsize) for h, a in zip(halves, bufs)]
    base = [3 * sum(nck[:i]) for i in range(n)]
    nsem = 3 * sum(nck)

    def plan():
        x, y, c = lax.axis_index("x"), lax.axis_index("y"), lax.axis_index("c")
        pieces = []
        for pi, (px, py) in enumerate([(x, 1 - y), (1 - x, y), (1 - x, 1 - y)]):
            for i in range(n):
                ch = halves[i] // nck[i]
                for j in range(nck[i]):
                    pieces.append((base[i] + pi * nck[i] + j, px, py, 2 * px + py, i, j * ch, ch))
        return x, y, c, 2 * x + y, pieces

    def rows(i, off, ch, core):
        return pl.ds(pl.multiple_of(core * halves[i] + off, 16), ch)

    def over_ici(outs, sems, c, slot, k, px, py, i, off, ch):
        ref = outs[i].at[slot, rows(i, off, ch, c)]
        return pltpu.make_async_remote_copy(src_ref=ref, dst_ref=ref, send_sem=sems[0].at[k], recv_sem=sems[1].at[k],
                                            device_id=(px, py, c), device_id_type=MESH)

    def over_d2d(outs, sems, x, y, c, slot, k, i, off, ch, core):
        ref = outs[i].at[slot, rows(i, off, ch, core)]
        return pltpu.make_async_remote_copy(src_ref=ref, dst_ref=ref, send_sem=sems[2].at[k], recv_sem=sems[3].at[k],
                                            device_id=(x, y, 1 - c), device_id_type=MESH)

    def start(ins, outs, sems):
        x, y, c, me, pieces = plan()
        for k, px, py, them, i, off, ch in pieces:
            over_ici(outs, sems, c, me, k, px, py, i, off, ch).start()

    def mid(ins, outs, sems):
        x, y, c, me, pieces = plan()
        for k, px, py, them, i, off, ch in pieces:
            over_ici(outs, sems, c, them, k, px, py, i, off, ch).wait_recv()
            over_d2d(outs, sems, x, y, c, them, k, i, off, ch, c).start()

    def end(ins, outs, sems):
        x, y, c, me, pieces = plan()
        for k, px, py, them, i, off, ch in pieces:
            over_ici(outs, sems, c, me, k, px, py, i, off, ch).wait_send()
            over_d2d(outs, sems, x, y, c, them, k, i, off, ch, c).wait_send()
        for k, px, py, them, i, off, ch in pieces:
            over_d2d(outs, sems, x, y, c, them, k, i, off, ch, 1 - c).wait_recv()

    return Rider(ins=list(bufs), out_shapes=[jax.ShapeDtypeStruct(a.shape, a.dtype) for a in bufs],
                 aliases={i: i for i in range(n)}, sems=[pltpu.SemaphoreType.DMA((nsem,))] * 4, start=start, mid=mid, end=end)


def _scatter_rider(srcs):
    n = len(srcs)
    nck = [_n_chunks(a.shape[1], a.shape[2] * a.dtype.itemsize) for a in srcs]
    base = [3 * sum(nck[:i]) for i in range(n)]
    nsem = 3 * sum(nck)

    def copies(ins, outs, sems):
        x, y, c = lax.axis_index("x"), lax.axis_index("y"), lax.axis_index("c")
        me = 2 * x + y
        for pi, (px, py) in enumerate([(x, 1 - y), (1 - x, y), (1 - x, 1 - y)]):
            for i in range(n):
                ch = srcs[i].shape[1] // nck[i]
                for j in range(nck[i]):
                    k = base[i] + pi * nck[i] + j
                    rs = pl.ds(j * ch, ch)
                    yield pltpu.make_async_remote_copy(
                        src_ref=ins[i].at[2 * px + py, rs], dst_ref=outs[i].at[me, rs], send_sem=sems[0].at[k],
                        recv_sem=sems[1].at[k], device_id=(px, py, c), device_id_type=MESH)

    def start(ins, outs, sems):
        for cp in copies(ins, outs, sems):
            cp.start()

    def end(ins, outs, sems):
        for cp in copies(ins, outs, sems):
            cp.wait()

    return Rider(ins=list(srcs), out_shapes=[jax.ShapeDtypeStruct(a.shape, a.dtype) for a in srcs], aliases={},
                 sems=[pltpu.SemaphoreType.DMA((nsem,))] * 2, start=start, mid=None, end=end)


def _ride_alone(rider, name):
    n_in, n_out = len(rider.ins), len(rider.out_shapes)

    def body(*refs):
        ins, outs, sems = refs[:n_in], refs[n_in:n_in + n_out], refs[n_in + n_out:]
        rider.start(ins, outs, sems)
        if rider.mid is not None:
            rider.mid(ins, outs, sems)
        rider.end(ins, outs, sems)

    any_spec = pl.BlockSpec(memory_space=pl.ANY)
    outs = pl.pallas_call(
        body, name=name, out_shape=list(rider.out_shapes), in_specs=[any_spec] * n_in, out_specs=[any_spec] * n_out,
        scratch_shapes=list(rider.sems), input_output_aliases=dict(rider.aliases),
        compiler_params=pltpu.CompilerParams(has_side_effects=True),
    )(*rider.ins)
    return list(outs)


def _swap_cores_inplace(bufs, name):
    n = len(bufs)
    nck = [_n_chunks(a.shape[2], a.shape[3] * a.dtype.itemsize) for a in bufs]
    base = [sum(a.shape[0] * k for a, k in zip(bufs[:i], nck[:i])) for i in range(n)]
    nsem = sum(a.shape[0] * k for a, k in zip(bufs, nck))

    def body(*refs):
        outs = refs[n:2 * n]
        send, recv = refs[2 * n:]
        x, y, c = lax.axis_index("x"), lax.axis_index("y"), lax.axis_index("c")

        def copies(core):
            for i in range(n):
                ch = bufs[i].shape[2] // nck[i]
                for p in range(bufs[i].shape[0]):
                    for j in range(nck[i]):
                        k = base[i] + p * nck[i] + j
                        ref = outs[i].at[p, core, pl.ds(j * ch, ch)]
                        yield pltpu.make_async_remote_copy(src_ref=ref, dst_ref=ref, send_sem=send.at[k], recv_sem=recv.at[k],
                                                           device_id=(x, y, 1 - c), device_id_type=MESH)

        for cp in copies(c):
            cp.start()
        for cp in copies(c):
            cp.wait_send()
        for cp in copies(1 - c):
            cp.wait_recv()

    any_spec = pl.BlockSpec(memory_space=pl.ANY)
    outs = pl.pallas_call(
        body, name=name, out_shape=[jax.ShapeDtypeStruct(a.shape, a.dtype) for a in bufs], in_specs=[any_spec] * n,
        out_specs=[any_spec] * n, scratch_shapes=[pltpu.SemaphoreType.DMA((nsem,))] * 2,
        input_output_aliases={i: i for i in range(n)}, compiler_params=pltpu.CompilerParams(has_side_effects=True),
    )(*bufs)
    return list(outs)


def _place_cast(w, row0, rows, slot, slots, name):
    C = w.shape[1]
    tr = _row_block(rows, C)
    blk0 = row0 // tr

    def body(slot_ref, w_ref, o_ref):
        del slot_ref
        o_ref[...] = w_ref[...].astype(BF16)

    return pl.pallas_call(
        body, name=name,
        grid_spec=pltpu.PrefetchScalarGridSpec(
            num_scalar_prefetch=1, grid=(rows // tr,), in_specs=[pl.BlockSpec((tr, C), lambda i, sr: (blk0 + i, 0))],
            out_specs=pl.BlockSpec((None, tr, C), lambda i, sr: (sr[0], i, 0))),
        out_shape=jax.ShapeDtypeStruct((slots, rows, C), BF16),
        compiler_params=_cparams("parallel"),
    )(slot.reshape(1).astype(jnp.int32), w)


def _swap_halves(arrs, name):
    n = len(arrs)
    S = arrs[0].shape[0]
    halves = [a.shape[1] // 2 for a in arrs]
    nck = [_n_chunks(h, a.shape[2] * a.dtype.itemsize) for h, a in zip(halves, arrs)]
    base = [S * sum(nck[:i]) for i in range(n)]
    nsem = S * sum(nck)

    def body(*refs):
        ins, outs = refs[:n], refs[n:2 * n]
        send, recv = refs[2 * n:]
        x, y, c = lax.axis_index("x"), lax.axis_index("y"), lax.axis_index("c")
        copies = []
        for i in range(n):
            ch = halves[i] // nck[i]
            for s in range(S):
                for j in range(nck[i]):
                    k = base[i] + s * nck[i] + j
                    src = ins[i].at[s, pl.ds(pl.multiple_of((1 - c) * halves[i] + j * ch, 16), ch)]
                    cp = pltpu.make_async_remote_copy(src_ref=src, dst_ref=outs[i].at[s, pl.ds(j * ch, ch)], send_sem=send.at[k],
                                                      recv_sem=recv.at[k], device_id=(x, y, 1 - c), device_id_type=MESH)
                    cp.start()
                    copies.append(cp)
        for cp in copies:
            cp.wait()

    any_spec = pl.BlockSpec(memory_space=pl.ANY)
    outs = pl.pallas_call(
        body, name=name, out_shape=[jax.ShapeDtypeStruct((S, h, a.shape[2]), a.dtype) for h, a in zip(halves, arrs)],
        in_specs=[any_spec] * n, out_specs=[any_spec] * n,
        scratch_shapes=[pltpu.SemaphoreType.DMA((nsem,))] * 2,
        compiler_params=pltpu.CompilerParams(has_side_effects=True),
    )(*arrs)
    return list(outs)


def _pair_sum(gs, rs, core, name):
    n = len(gs)
    S, rows, C = gs[0].shape
    half = rows // 2
    tr = _row_block(half, C)
    nb = half // tr

    def body(core_ref, *refs):
        del core_ref
        for g_ref, r_ref, o_ref in zip(refs[:n], refs[n:2 * n], refs[2 * n:]):
            o_ref[...] = (g_ref[...].astype(F32) + r_ref[...].astype(F32)).astype(BF16)

    blk = pl.BlockSpec((None, tr, C), lambda s, i, cr: (s, i, 0))
    mine = pl.BlockSpec((None, tr, C), lambda s, i, cr: (s, cr[0] * nb + i, 0))
    return pl.pallas_call(
        body, name=name,
        grid_spec=pltpu.PrefetchScalarGridSpec(num_scalar_prefetch=1, grid=(S, nb), in_specs=[mine] * n + [blk] * n,
                                               out_specs=[blk] * n),
        out_shape=[jax.ShapeDtypeStruct((S, half, C), BF16)] * n,
        compiler_params=_cparams("parallel", "parallel"),
    )(core.reshape(1).astype(jnp.int32), *gs, *rs)


def _ffn_fwd(h, mod, g, w1, w3, w2, k, dm, name, rider=None):
    T, D = h.shape
    S, F = w1.shape[0], w1.shape[-2]
    tm = dm.tm
    r0 = 6 if k else 0
    grp = _grp(dm)

    def body(h_ref, mod_ref, g_ref, w1_ref, w3_ref, w2_ref, ho_ref, a_ref, b_ref, hn_ref, y_ref, hn_s, acc):
        s = pl.program_id(1)

        @pl.when(s == 0)
        def _():
            hn = _pre(h_ref[...], g_ref[...], mod_ref[0, r0:r0 + 1, :], mod_ref[0, r0 + 1:r0 + 2, :]).astype(BF16)
            hn_s[...] = hn
            hn_ref[...] = hn
            acc[...] = jnp.zeros_like(acc)

        hn = hn_s[...]
        a = _dot_nt(hn, w1_ref[...])
        b = _dot_nt(hn, w3_ref[...])
        a_ref[0] = a.astype(BF16)
        b_ref[0] = b.astype(BF16)
        sw = (a * jax.nn.sigmoid(a) * b).astype(BF16)
        acc[...] += _dot(sw, w2_ref[...])

        @pl.when(s == S - 1)
        def _():
            y = acc[...]
            y_ref[...] = y.astype(BF16)
            ho_ref[...] = h_ref[...] + 0.5 * mod_ref[0, r0 + 2:r0 + 3, :] * y

    row = pl.BlockSpec((tm, D), lambda i, s: (i, 0))
    wrow = pl.BlockSpec((None, F, D), lambda i, s: (s, 0, 0))
    ab = pl.BlockSpec((1, tm, F), lambda i, s: (s, i, 0))
    return _hosted(
        body, rider, name=name, grid=(T // tm, S),
        in_specs=[row, pl.BlockSpec((1, 9, D), lambda i, s: (grp(i), 0, 0)), pl.BlockSpec((1, D), lambda i, s: (0, 0)),
                  wrow, wrow, wrow],
        out_specs=[row, ab, ab, row, row],
        out_shape=[jax.ShapeDtypeStruct((T, D), F32), jax.ShapeDtypeStruct((S, T, F), BF16),
                   jax.ShapeDtypeStruct((S, T, F), BF16), jax.ShapeDtypeStruct((T, D), BF16),
                   jax.ShapeDtypeStruct((T, D), BF16)],
        scratch_shapes=[pltpu.VMEM((tm, D), BF16), pltpu.VMEM((tm, D), F32)],
        sem=("parallel", "arbitrary"), args=(h, mod, g, w1, w3, w2))


def _ffn_bwd(dh, h, mod, g, y, a, b, w1, w3, w2, k, dm, name, rider=None):
    T, D = h.shape
    S, F = w1.shape[0], w1.shape[-2]
    tm = dm.tm
    r0 = 6 if k else 0
    grp = _grp(dm)

    def body(dh_ref, h_ref, mod_ref, g_ref, y_ref, a_ref, b_ref, w1_ref, w3_ref, w2_ref,
             dho_ref, da_ref, db_ref, sw_ref, dy_ref, part_ref, dy_s, acc):
        s = pl.program_id(1)

        @pl.when(s == 0)
        def _():
            dy = (0.5 * mod_ref[0, r0 + 2:r0 + 3, :] * dh_ref[...]).astype(BF16)
            dy_s[...] = dy
            dy_ref[...] = dy
            acc[...] = jnp.zeros_like(acc)

        ds = _dot_nt(dy_s[...], w2_ref[...]).astype(BF16)
        av = a_ref[0]
        bv = b_ref[0]
        sig = jax.nn.sigmoid(av)
        sil = av * sig
        sw_ref[0] = sil * bv
        db = ds * sil
        da = ds * bv * (sig + sil * (1.0 - sig))
        da_ref[0] = da
        db_ref[0] = db
        acc[...] += _dot(da, w1_ref[...]) + _dot(db, w3_ref[...])

        @pl.when(s == S - 1)
        def _():
            dhv = dh_ref[...]
            dhb, dshift, dscale, dg = _pre_bwd(acc[...], h_ref[...], g_ref[...], mod_ref[0, r0 + 1:r0 + 2, :])
            dho_ref[...] = dhv + dhb
            dgate = 0.5 * jnp.sum(dhv * y_ref[...].astype(F32), axis=0, keepdims=True)
            _write_part(part_ref, dshift, dscale, dgate, dg)

    row = pl.BlockSpec((tm, D), lambda i, s: (i, 0))
    wrow = pl.BlockSpec((None, F, D), lambda i, s: (s, 0, 0))
    ab = pl.BlockSpec((1, tm, F), lambda i, s: (s, i, 0))
    stf = jax.ShapeDtypeStruct((S, T, F), BF16)
    return _hosted(
        body, rider, name=name, grid=(T // tm, S),
        in_specs=[row, row, pl.BlockSpec((1, 9, D), lambda i, s: (grp(i), 0, 0)), pl.BlockSpec((1, D), lambda i, s: (0, 0)),
                  row, ab, ab, wrow, wrow, wrow],
        out_specs=[row, ab, ab, ab, row, pl.BlockSpec((1, 8, D), lambda i, s: (i, 0, 0))],
        out_shape=[jax.ShapeDtypeStruct((T, D), F32), stf, stf, stf, jax.ShapeDtypeStruct((T, D), BF16),
                   jax.ShapeDtypeStruct((T // tm, 8, D), F32)],
        scratch_shapes=[pltpu.VMEM((tm, D), BF16), pltpu.VMEM((tm, D), F32)],
        sem=("parallel", "arbitrary"), args=(dh, h, mod, g, y, a, b, w1, w3, w2))


def _mm_tn(a, b, a_spec, b_spec, out_shape, out_spec, grid, name):
    nk = grid[-1]
    kax = len(grid) - 1
    blk = tuple(d for d in out_spec.block_shape if d is not None)

    def body(a_ref, b_ref, o_ref, acc):
        kk = pl.program_id(kax)

        @pl.when(kk == 0)
        def _():
            acc[...] = jnp.zeros_like(acc)

        acc[...] += _dot_tn(a_ref[...].astype(BF16), b_ref[...].astype(BF16))

        @pl.when(kk == nk - 1)
        def _():
            o_ref[...] = acc[...].astype(o_ref.dtype)

    return pl.pallas_call(
        body, name=name, grid=grid,
        in_specs=[a_spec, b_spec], out_specs=out_spec, out_shape=jax.ShapeDtypeStruct(out_shape, BF16),
        scratch_shapes=[pltpu.VMEM(blk, F32)],
        compiler_params=_cparams(*(["parallel"] * kax + ["arbitrary"])),
    )(a, b)


def _sc_w_in_specs(D, j):
    wq = D // N_CHIPS

    def spec(piece):
        col = lambda q: piece * N_CHIPS + q
        return pl.BlockSpec((None, None, D, wq), lambda i, q: (col(q) // 3, j, 0, col(q) % 3))

    return [spec(0), spec(1), spec(2)]


def _sc_in_fwd(h, mod, g, w_in, j, dm, name, rider=None):
    T, D = h.shape
    tm = dm.tm
    wq = D // N_CHIPS
    grp = _grp(dm)

    def body(h_ref, mod_ref, g_ref, wb_ref, wc_ref, wu_ref, p_ref, hn_ref, hn_s):
        @pl.when(pl.program_id(1) == 0)
        def _():
            hn = _pre(h_ref[...], g_ref[...], mod_ref[0, 3:4, :], mod_ref[0, 4:5, :]).astype(BF16)
            hn_s[...] = hn
            hn_ref[...] = hn

        for piece, w_ref in enumerate((wb_ref, wc_ref, wu_ref)):
            p_ref[piece] = _dot(hn_s[...], w_ref[...])

    row = pl.BlockSpec((tm, D), lambda i, q: (i, 0))
    return _hosted(
        body, rider, name=name, grid=(T // tm, N_CHIPS),
        in_specs=[row, pl.BlockSpec((1, 9, D), lambda i, q: (grp(i), 0, 0)), pl.BlockSpec((1, D), lambda i, q: (0, 0))]
        + _sc_w_in_specs(D, j),
        out_specs=[pl.BlockSpec((3, tm, wq), lambda i, q: (0, i, q)), row],
        out_shape=[jax.ShapeDtypeStruct((3, T, D), F32), jax.ShapeDtypeStruct((T, D), BF16)],
        scratch_shapes=[pltpu.VMEM((tm, D), BF16)],
        sem=("parallel", "arbitrary"), args=(h, mod, g, w_in, w_in, w_in))


def _conv_cols(dm):
    return 256 if dm.D % 256 == 0 else 128


def _seg_masks(r, dm):
    bn = dm.B * dm.N
    lat = r < bn
    off = jnp.where(lat, lax.rem(r, dm.N), lax.rem(r - bn, dm.CTX))
    seg = jnp.where(lat, dm.N, dm.CTX)
    inside = (r >= 0) & (r < dm.T)
    return ((off != 0) & inside).astype(F32), ((off != seg - 1) & inside).astype(F32)


def _conv_specs(dm):
    tb, cb, nr8 = dm.tm, _conv_cols(dm), dm.T // 8
    prev8 = lambda c, i: jnp.maximum(i * (tb // 8) - 1, 0)
    next8 = lambda c, i: jnp.minimum((i + 1) * (tb // 8), nr8 - 1)
    return dict(
        tb=tb, cb=cb,
        p=pl.BlockSpec((3, tb, cb), lambda c, i: (0, i, c)),
        p_prev=pl.BlockSpec((3, 8, cb), lambda c, i: (0, prev8(c, i), c)),
        p_next=pl.BlockSpec((3, 8, cb), lambda c, i: (0, next8(c, i), c)),
        row=pl.BlockSpec((tb, cb), lambda c, i: (i, c)),
        row_prev=pl.BlockSpec((8, cb), lambda c, i: (prev8(c, i), c)),
        row_next=pl.BlockSpec((8, cb), lambda c, i: (next8(c, i), c)),
        w=pl.BlockSpec((3, cb), lambda c, i: (0, c)),
    )


def _shift_rows(x, before, after, tb):
    rid = lax.broadcasted_iota(jnp.int32, x.shape, 0)
    down = jnp.where(rid == 0, before, pltpu.roll(x, 1, 0))
    up = jnp.where(rid == tb - 1, after, pltpu.roll(x, tb - 1, 0))
    return down, up


def _conv_fwd(p, wc, dm, name):
    T, D = dm.T, dm.D
    sp = _conv_specs(dm)
    tb, cb = sp["tb"], sp["cb"]

    def body(p_ref, pp_ref, pn_ref, w_ref, z_ref):
        r = pl.program_id(1) * tb + lax.broadcasted_iota(jnp.int32, (tb, cb), 0)
        mp, mn = _seg_masks(r, dm)
        cu = p_ref[1] * p_ref[2]
        prev, nxt = _shift_rows(cu, pp_ref[1, 7:8, :] * pp_ref[2, 7:8, :], pn_ref[1, 0:1, :] * pn_ref[2, 0:1, :], tb)
        conv = w_ref[0:1, :] * (prev * mp) + w_ref[1:2, :] * cu + w_ref[2:3, :] * (nxt * mn)
        z_ref[...] = (p_ref[0] * conv).astype(BF16)

    return pl.pallas_call(
        body, name=name, grid=(D // cb, T // tb),
        in_specs=[sp["p"], sp["p_prev"], sp["p_next"], sp["w"]], out_specs=sp["row"],
        out_shape=jax.ShapeDtypeStruct((T, D), BF16),
        compiler_params=_cparams("parallel", "parallel"),
    )(p, p, p, wc)


def _conv_bwd(dz, p, wc, dm, name):
    T, D = dm.T, dm.D
    sp = _conv_specs(dm)
    tb, cb = sp["tb"], sp["cb"]

    def body(dz_ref, dzp_ref, dzn_ref, p_ref, pp_ref, pn_ref, w_ref, dp_ref, dw_ref):
        i = pl.program_id(1)
        r = i * tb + lax.broadcasted_iota(jnp.int32, (tb, cb), 0)
        mp, mn = _seg_masks(r, dm)
        rb = i * tb + lax.broadcasted_iota(jnp.int32, (1, cb), 0)
        _, mn_before = _seg_masks(rb - 1, dm)
        mp_after, _ = _seg_masks(rb + tb, dm)
        bg, cg, u = p_ref[0], p_ref[1], p_ref[2]
        cu = cg * u
        prev, nxt = _shift_rows(cu, pp_ref[1, 7:8, :] * pp_ref[2, 7:8, :], pn_ref[1, 0:1, :] * pn_ref[2, 0:1, :], tb)
        prev = prev * mp
        nxt = nxt * mn
        w0, w1, w2 = w_ref[0:1, :], w_ref[1:2, :], w_ref[2:3, :]
        conv = w0 * prev + w1 * cu + w2 * nxt
        dz = dz_ref[...]
        dp_ref[0] = dz * conv
        dconv = dz * bg

        @pl.when(i == 0)
        def _():
            dw_ref[...] = jnp.zeros_like(dw_ref)

        dw_ref[0:1, :] += jnp.sum(dconv * prev, axis=0, keepdims=True)
        dw_ref[1:2, :] += jnp.sum(dconv * cu, axis=0, keepdims=True)
        dw_ref[2:3, :] += jnp.sum(dconv * nxt, axis=0, keepdims=True)
        dconv_before = dzp_ref[7:8, :] * pp_ref[0, 7:8, :] * mn_before
        dconv_after = dzn_ref[0:1, :] * pn_ref[0, 0:1, :] * mp_after
        from_prev, _ = _shift_rows(dconv * mn, dconv_before, dconv_after, tb)
        _, from_next = _shift_rows(dconv * mp, dconv_before, dconv_after, tb)
        dcu = w1 * dconv + w0 * from_next + w2 * from_prev
        dp_ref[1] = dcu * u
        dp_ref[2] = dcu * cg

    return pl.pallas_call(
        body, name=name, grid=(D // cb, T // tb),
        in_specs=[sp["row"], sp["row_prev"], sp["row_next"], sp["p"], sp["p_prev"], sp["p_next"], sp["w"]],
        out_specs=[sp["p"], sp["w"]],
        out_shape=[jax.ShapeDtypeStruct((3, T, D), F32), jax.ShapeDtypeStruct((3, D), F32)],
        compiler_params=_cparams("parallel", "arbitrary"),
    )(dz, dz, dz, p, p, p, wc)


def _out_fwd(z, w, h, mod, j, dm, name):
    T, D = h.shape
    K = z.shape[1]
    tm = dm.tm
    grp = _grp(dm)

    def body(z_ref, w_ref, h_ref, mod_ref, ho_ref, y_ref):
        y = _dot(z_ref[...], w_ref[...])
        y_ref[...] = y.astype(BF16)
        ho_ref[...] = h_ref[...] + mod_ref[0, 5:6, :] * y

    row = pl.BlockSpec((tm, D), lambda i: (i, 0))
    return pl.pallas_call(
        body, name=name, grid=(T // tm,),
        in_specs=[pl.BlockSpec((tm, K), lambda i: (i, 0)), pl.BlockSpec((None, K, D), lambda i: (j, 0, 0)), row,
                  pl.BlockSpec((1, 9, D), lambda i: (grp(i), 0, 0))],
        out_specs=[row, row],
        out_shape=[jax.ShapeDtypeStruct((T, D), F32), jax.ShapeDtypeStruct((T, D), BF16)],
        compiler_params=_cparams("parallel"),
    )(z, w, h, mod)


def _out_bwd(dh, y, w, mod, j, dm, name):
    T, D = dh.shape
    K = w.shape[1]
    tm = dm.tm
    grp = _grp(dm)

    def body(dh_ref, y_ref, w_ref, mod_ref, dy_ref, dz_ref, part_ref):
        dhv = dh_ref[...]
        dy = (mod_ref[0, 5:6, :] * dhv).astype(BF16)
        dy_ref[...] = dy
        dz_ref[...] = _dot_nt(dy, w_ref[...])
        _write_part(part_ref, dgate=jnp.sum(dhv * y_ref[...].astype(F32), axis=0, keepdims=True))

    row = pl.BlockSpec((tm, D), lambda i: (i, 0))
    return pl.pallas_call(
        body, name=name, grid=(T // tm,),
        in_specs=[row, row, pl.BlockSpec((None, K, D), lambda i: (j, 0, 0)), pl.BlockSpec((1, 9, D), lambda i: (grp(i), 0, 0))],
        out_specs=[row, pl.BlockSpec((tm, K), lambda i: (i, 0)), pl.BlockSpec((1, 8, D), lambda i: (i, 0, 0))],
        out_shape=[jax.ShapeDtypeStruct((T, D), BF16), jax.ShapeDtypeStruct((T, K), F32),
                   jax.ShapeDtypeStruct((T // tm, 8, D), F32)],
        compiler_params=_cparams("parallel"),
    )(dh, y, w, mod)


def _sc_in_bwd(dh, dp, h, mod, g, w_in, j, dm, name):
    T, D = h.shape
    tm = dm.tm
    wq = D // N_CHIPS
    nq = N_CHIPS
    grp = _grp(dm)

    def body(dh_ref, dp_ref, h_ref, mod_ref, g_ref, wb_ref, wc_ref, wu_ref, dho_ref, part_ref, acc):
        q = pl.program_id(1)

        @pl.when(q == 0)
        def _():
            acc[...] = jnp.zeros_like(acc)

        acc[...] += sum(_dot_nt(dp_ref[piece].astype(BF16), w_ref[...]) for piece, w_ref in enumerate((wb_ref, wc_ref, wu_ref)))

        @pl.when(q == nq - 1)
        def _():
            dhb, dshift, dscale, dg = _pre_bwd(acc[...], h_ref[...], g_ref[...], mod_ref[0, 4:5, :])
            dho_ref[...] = dh_ref[...] + dhb
            _write_part(part_ref, dshift, dscale, None, dg)

    row = pl.BlockSpec((tm, D), lambda i, q: (i, 0))
    return pl.pallas_call(
        body, name=name, grid=(T // tm, nq),
        in_specs=[row, pl.BlockSpec((3, tm, wq), lambda i, q: (0, i, q)), row,
                  pl.BlockSpec((1, 9, D), lambda i, q: (grp(i), 0, 0)), pl.BlockSpec((1, D), lambda i, q: (0, 0))]
        + _sc_w_in_specs(D, j),
        out_specs=[row, pl.BlockSpec((1, 8, D), lambda i, q: (i, 0, 0))],
        out_shape=[jax.ShapeDtypeStruct((T, D), F32), jax.ShapeDtypeStruct((T // tm, 8, D), F32)],
        scratch_shapes=[pltpu.VMEM((tm, D), F32)],
        compiler_params=_cparams("parallel", "arbitrary"),
    )(dh, dp, h, mod, g, w_in, w_in, w_in)


def _rope(t, c, s1, s2):
    return t * c + pltpu.roll(t, HEAD_PAD - 16, 1) * s1 + pltpu.roll(t, 16, 1) * s2


def _rope_t(dy, c, s1, s2):
    return dy * c + pltpu.roll(dy * s1, 16, 1) + pltpu.roll(dy * s2, HEAD_PAD - 16, 1)


def _mla_heads_fwd(z, g_ref, wuq_ref, wukv_ref):
    cq, ckv, krp = z[:, :Q_LORA], z[:, Q_LORA:Q_LORA + KV_LORA], z[:, Q_LORA + KV_LORA:]
    cqh, rq = _rms(cq, Q_LORA)
    ckvh, rkv = _rms(ckv, KV_LORA)
    cqn = (cqh * g_ref[0:1, :]).astype(BF16)
    ckvn = (ckvh * g_ref[1:2, :KV_LORA]).astype(BF16)
    qraw = _dot(cqn, wuq_ref[...])
    kvraw = _dot(ckvn, wukv_ref[...])
    return dict(krp=krp, cqh=cqh, rq=rq, ckvh=ckvh, rkv=rkv, cqn=cqn, ckvn=ckvn, qraw=qraw, kvraw=kvraw)


def _mla_proj_fwd(h, mod, g, gains, tabs, w_a, w_uq, w_ukv, j, dm, name):
    T, D = h.shape
    tm = min(dm.tm, 256)
    grp = lambda i: jnp.minimum(i // (dm.N // tm), dm.B)
    HP = HEAD_PAD

    def body(h_ref, mod_ref, g_ref, gn_ref, tab_ref, wa_ref, wuq_ref, wukv_ref, hn_ref, q_ref, k_ref, v_ref):
        hn = _pre(h_ref[...], g_ref[...], mod_ref[0, 3:4, :], mod_ref[0, 4:5, :]).astype(BF16)
        hn_ref[...] = hn
        f = _mla_heads_fwd(_dot(hn, wa_ref[...]), gn_ref, wuq_ref, wukv_ref)
        c, s1, s2 = tab_ref[0], tab_ref[1], tab_ref[2]
        for hd in range(HEADS):
            qh, _ = _rms(f["qraw"][:, hd * HP:(hd + 1) * HP], QK_HEAD)
            q_ref[:, hd * HP:(hd + 1) * HP] = (_rope(qh * gn_ref[2:3, :], c, s1, s2) * QK_SCALE).astype(BF16)
            kpre = jnp.concatenate([f["kvraw"][:, hd * HP:hd * HP + QK_NOPE], f["krp"]], axis=1)
            kh, _ = _rms(kpre, QK_HEAD)
            k_ref[:, hd * HP:(hd + 1) * HP] = _rope(kh * gn_ref[3:4, :], c, s1, s2).astype(BF16)
            v_ref[:, hd * V_HEAD:(hd + 1) * V_HEAD] = f["kvraw"][:, hd * HP + QK_NOPE:(hd + 1) * HP].astype(BF16)

    row = pl.BlockSpec((tm, D), lambda i: (i, 0))
    HQ = HEADS * HP
    return pl.pallas_call(
        body, name=name, grid=(T // tm,),
        in_specs=[row, pl.BlockSpec((1, 9, D), lambda i: (grp(i), 0, 0)), pl.BlockSpec((1, D), lambda i: (0, 0)),
                  pl.BlockSpec((None, 8, HP), lambda i: (j, 0, 0)), pl.BlockSpec((3, tm, HP), lambda i: (0, i, 0)),
                  pl.BlockSpec((None, D, 512), lambda i: (j, 0, 0)), pl.BlockSpec((None, Q_LORA, HQ), lambda i: (j, 0, 0)),
                  pl.BlockSpec((None, KV_LORA, HQ), lambda i: (j, 0, 0))],
        out_specs=[row, pl.BlockSpec((tm, HQ), lambda i: (i, 0)), pl.BlockSpec((tm, HQ), lambda i: (i, 0)),
                   pl.BlockSpec((tm, HEADS * V_HEAD), lambda i: (i, 0))],
        out_shape=[jax.ShapeDtypeStruct((T, D), BF16), jax.ShapeDtypeStruct((T, HQ), BF16),
                   jax.ShapeDtypeStruct((T, HQ), BF16), jax.ShapeDtypeStruct((T, HEADS * V_HEAD), BF16)],
        compiler_params=_cparams("parallel"),
    )(h, mod, g, gains, tabs, w_a, w_uq, w_ukv)


def _mla_proj_bwd(dh, dq, dkl, dkc, dvl, dvc, h, mod, g, gains, tabs, w_a, w_uq, w_ukv, j, dm, name):
    T, D = h.shape
    tm = min(dm.tm, 256)
    nblk = T // tm
    grp = lambda i: jnp.minimum(i // (dm.N // tm), dm.B)
    HP = HEAD_PAD
    HQ = HEADS * HP

    nlat = dm.B * dm.N // tm

    def body(dh_ref, dq_ref, dkl_ref, dkc_ref, dvl_ref, dvc_ref, h_ref, mod_ref, g_ref, gn_ref, tab_ref, wa_ref, wuq_ref, wukv_ref,
             dho_ref, part_ref, gwa_ref, gwuq_ref, gwukv_ref, dgn_ref, dqraw_s, dkvraw_s):
        i = pl.program_id(0)
        pick = lambda lat_ref, ctx_ref, cols: jnp.where(i < nlat, lat_ref[:, cols], ctx_ref[:, cols])

        @pl.when(i == 0)
        def _():
            gwa_ref[...] = jnp.zeros_like(gwa_ref)
            gwuq_ref[...] = jnp.zeros_like(gwuq_ref)
            gwukv_ref[...] = jnp.zeros_like(gwukv_ref)
            dgn_ref[...] = jnp.zeros_like(dgn_ref)

        hv = h_ref[...]
        hn = _pre(hv, g_ref[...], mod_ref[0, 3:4, :], mod_ref[0, 4:5, :]).astype(BF16)
        f = _mla_heads_fwd(_dot(hn, wa_ref[...]), gn_ref, wuq_ref, wukv_ref)
        c, s1, s2 = tab_ref[0], tab_ref[1], tab_ref[2]
        gq, gk = gn_ref[2:3, :], gn_ref[3:4, :]
        dgq = jnp.zeros((1, HP), F32)
        dgk = jnp.zeros((1, HP), F32)
        dkrp = jnp.zeros((tm, HP - QK_NOPE), F32)
        for hd in range(HEADS):
            qh, rq = _rms(f["qraw"][:, hd * HP:(hd + 1) * HP], QK_HEAD)
            dqn = _rope_t(dq_ref[:, hd * HP:(hd + 1) * HP] * QK_SCALE, c, s1, s2)
            dgq = dgq + jnp.sum(dqn * qh, axis=0, keepdims=True)
            dqraw_s[:, hd * HP:(hd + 1) * HP] = _rms_bwd(dqn * gq, qh, rq, QK_HEAD)
            kpre = jnp.concatenate([f["kvraw"][:, hd * HP:hd * HP + QK_NOPE], f["krp"]], axis=1)
            kh, rk = _rms(kpre, QK_HEAD)
            dkn = _rope_t(pick(dkl_ref, dkc_ref, slice(hd * HP, (hd + 1) * HP)), c, s1, s2)
            dgk = dgk + jnp.sum(dkn * kh, axis=0, keepdims=True)
            dkpre = _rms_bwd(dkn * gk, kh, rk, QK_HEAD)
            dkvraw_s[:, hd * HP:hd * HP + QK_NOPE] = dkpre[:, :QK_NOPE]
            dkrp = dkrp + dkpre[:, QK_NOPE:]
            dkvraw_s[:, hd * HP + QK_NOPE:(hd + 1) * HP] = pick(dvl_ref, dvc_ref, slice(hd * V_HEAD, (hd + 1) * V_HEAD))
        dqraw = dqraw_s[...].astype(BF16)
        dkvraw = dkvraw_s[...].astype(BF16)
        gwuq_ref[...] += _dot_tn(f["cqn"], dqraw)
        gwukv_ref[...] += _dot_tn(f["ckvn"], dkvraw)
        dcqn = _dot_nt(dqraw, wuq_ref[...])
        dckvn = _dot_nt(dkvraw, wukv_ref[...])
        dgqa = jnp.sum(dcqn * f["cqh"], axis=0, keepdims=True)
        dgkva = jnp.sum(dckvn * f["ckvh"], axis=0, keepdims=True)
        dcq = _rms_bwd(dcqn * gn_ref[0:1, :], f["cqh"], f["rq"], Q_LORA)
        dckv = _rms_bwd(dckvn * gn_ref[1:2, :KV_LORA], f["ckvh"], f["rkv"], KV_LORA)
        dz = jnp.concatenate([dcq, dckv, dkrp], axis=1).astype(BF16)
        gwa_ref[...] += _dot_tn(hn, dz)
        dhn = _dot_nt(dz, wa_ref[...])
        dhb, dshift, dscale, dg = _pre_bwd(dhn, hv, g_ref[...], mod_ref[0, 4:5, :])
        dho_ref[...] = dh_ref[...] + dhb
        _write_part(part_ref, dshift, dscale, None, dg)
        dgn_ref[0:1, :] += dgqa
        dgn_ref[1:2, :KV_LORA] += dgkva
        dgn_ref[2:3, :] += dgq
        dgn_ref[3:4, :] += dgk

    row = pl.BlockSpec((tm, D), lambda i: (i, 0))
    wide = pl.BlockSpec((tm, HQ), lambda i: (i, 0))
    const2 = lambda i: (0, 0)
    return pl.pallas_call(
        body, name=name, grid=(nblk,),
        in_specs=[row, wide, pl.BlockSpec((tm, HQ), lambda i: (jnp.minimum(i, nlat - 1), 0)),
                  pl.BlockSpec((tm, HQ), lambda i: (jnp.maximum(i - nlat, 0), 0)),
                  pl.BlockSpec((tm, HEADS * V_HEAD), lambda i: (jnp.minimum(i, nlat - 1), 0)),
                  pl.BlockSpec((tm, HEADS * V_HEAD), lambda i: (jnp.maximum(i - nlat, 0), 0)), row,
                  pl.BlockSpec((1, 9, D), lambda i: (grp(i), 0, 0)), pl.BlockSpec((1, D), const2),
                  pl.BlockSpec((None, 8, HP), lambda i: (j, 0, 0)), pl.BlockSpec((3, tm, HP), lambda i: (0, i, 0)),
                  pl.BlockSpec((None, D, 512), lambda i: (j, 0, 0)), pl.BlockSpec((None, Q_LORA, HQ), lambda i: (j, 0, 0)),
                  pl.BlockSpec((None, KV_LORA, HQ), lambda i: (j, 0, 0))],
        out_specs=[row, pl.BlockSpec((1, 8, D), lambda i: (i, 0, 0)), pl.BlockSpec((D, 512), const2),
                   pl.BlockSpec((Q_LORA, HQ), const2), pl.BlockSpec((KV_LORA, HQ), const2), pl.BlockSpec((8, HP), const2)],
        out_shape=[jax.ShapeDtypeStruct((T, D), F32), jax.ShapeDtypeStruct((nblk, 8, D), F32),
                   jax.ShapeDtypeStruct((D, 512), F32), jax.ShapeDtypeStruct((Q_LORA, HQ), F32),
                   jax.ShapeDtypeStruct((KV_LORA, HQ), F32), jax.ShapeDtypeStruct((8, HP), F32)],
        scratch_shapes=[pltpu.VMEM((tm, HQ), F32), pltpu.VMEM((tm, HQ), F32)],
        compiler_params=_cparams("arbitrary"),
    )(dh, dq, dkl, dkc, dvl, dvc, h, mod, g, gains, tabs, w_a, w_uq, w_ukv)


def _attn_specs(dm):
    tq = dm.CTX
    nq = dm.N // tq
    cblk0 = dm.B * nq
    HP = HEAD_PAD
    qrow = lambda b, i: jnp.where(i < nq, b * nq + i, cblk0 + b)
    return dict(
        tq=tq, nq=nq,
        q=pl.BlockSpec((tq, HP), lambda b, hd, i: (qrow(b, i), hd)),
        k_lat=pl.BlockSpec((dm.N, HP), lambda b, hd, i: (b, hd)),
        k_ctx=pl.BlockSpec((tq, HP), lambda b, hd, i: (cblk0 + b, hd)),
        v_lat=pl.BlockSpec((dm.N, V_HEAD), lambda b, hd, i: (b, hd)),
        v_ctx=pl.BlockSpec((tq, V_HEAD), lambda b, hd, i: (cblk0 + b, hd)),
        o=pl.BlockSpec((tq, V_HEAD), lambda b, hd, i: (qrow(b, i), hd)),
    )


def _attn_exp(q, keys, first_off=None):
    s = [_dot_nt(q, kk) for kk in keys]
    if first_off is not None:
        s[0] = s[0] + first_off
    m = functools.reduce(jnp.maximum, [jnp.max(x, axis=-1, keepdims=True) for x in s])
    e = [jnp.exp(x - m) for x in s]
    return e, 1.0 / sum(jnp.sum(x, axis=-1, keepdims=True) for x in e)


def _attn_fwd(q, k, v, dm, name, rider=None):
    T = dm.T
    sp = _attn_specs(dm)
    nq = sp["nq"]

    def body(q_ref, kl_ref, kc_ref, vl_ref, vc_ref, o_ref):
        i = pl.program_id(2)

        @pl.when(i < nq)
        def _():
            (el, ec), inv = _attn_exp(q_ref[...], [kl_ref[...], kc_ref[...]])
            o_ref[...] = ((_dot(el.astype(BF16), vl_ref[...]) + _dot(ec.astype(BF16), vc_ref[...])) * inv).astype(BF16)

        @pl.when(i == nq)
        def _():
            (ec,), inv = _attn_exp(q_ref[...], [kc_ref[...]])
            o_ref[...] = (_dot(ec.astype(BF16), vc_ref[...]) * inv).astype(BF16)

    (o,), got = _hosted(
        body, rider, name=name, grid=(dm.B, HEADS, nq + 1),
        in_specs=[sp["q"], sp["k_lat"], sp["k_ctx"], sp["v_lat"], sp["v_ctx"]], out_specs=[sp["o"]],
        out_shape=[jax.ShapeDtypeStruct((T, HEADS * V_HEAD), BF16)], scratch_shapes=[],
        sem=("parallel", "parallel", "arbitrary"), args=(q, k, k, v, v))
    return o, got


def _attn_bwd(q, k, v, o, do, dm, name):
    T = dm.T
    sp = _attn_specs(dm)
    nq, tq = sp["nq"], sp["tq"]
    HP, HQ, HV = HEAD_PAD, HEADS * HEAD_PAD, HEADS * V_HEAD

    def body(q_ref, kl_ref, kc_ref, vl_ref, vc_ref, o_ref, do_ref, dq_ref, dkl_ref, dkc_ref, dvl_ref, dvc_ref):
        i = pl.program_id(2)

        @pl.when(i == 0)
        def _():
            dkl_ref[...] = jnp.zeros_like(dkl_ref)
            dkc_ref[...] = jnp.zeros_like(dkc_ref)
            dvl_ref[...] = jnp.zeros_like(dvl_ref)
            dvc_ref[...] = jnp.zeros_like(dvc_ref)

        qv = q_ref[...]
        dov = do_ref[...]
        dob = dov.astype(BF16)
        delta = jnp.sum(dov * o_ref[...].astype(F32), axis=-1, keepdims=True)
        (el, ec), inv = _attn_exp(qv, [kl_ref[...], kc_ref[...]], jnp.where(i == nq, NEG, 0.0))
        pl_, pc = el * inv, ec * inv
        dsl = (pl_ * (_dot_nt(dob, vl_ref[...]) - delta)).astype(BF16)
        dsc = (pc * (_dot_nt(dob, vc_ref[...]) - delta)).astype(BF16)
        dq_ref[...] = _dot(dsl, kl_ref[...]) + _dot(dsc, kc_ref[...])
        dkl_ref[...] += _dot_tn(dsl, qv)
        dkc_ref[...] += _dot_tn(dsc, qv)
        dvl_ref[...] += _dot_tn(pl_.astype(BF16), dob)
        dvc_ref[...] += _dot_tn(pc.astype(BF16), dob)

    return pl.pallas_call(
        body, name=name, grid=(dm.B, HEADS, nq + 1),
        in_specs=[sp["q"], sp["k_lat"], sp["k_ctx"], sp["v_lat"], sp["v_ctx"], sp["o"], sp["o"]],
        out_specs=[sp["q"], sp["k_lat"], pl.BlockSpec((tq, HP), lambda b, hd, i: (b, hd)),
                   sp["v_lat"], pl.BlockSpec((tq, V_HEAD), lambda b, hd, i: (b, hd))],
        out_shape=[jax.ShapeDtypeStruct((T, HQ), F32), jax.ShapeDtypeStruct((dm.B * dm.N, HQ), F32),
                   jax.ShapeDtypeStruct((dm.B * dm.CTX, HQ), F32), jax.ShapeDtypeStruct((dm.B * dm.N, HV), F32),
                   jax.ShapeDtypeStruct((dm.B * dm.CTX, HV), F32)],
        compiler_params=_cparams("parallel", "parallel", "arbitrary"),
    )(q, k, k, v, v, o, do)


def _loss_grad(h, target, dm, name):
    T, D = h.shape
    tm = dm.tm
    nlat = dm.B * dm.N // tm

    def body(h_ref, t_ref, dh_ref, ls_ref):
        lat = (pl.program_id(0) < nlat).astype(F32)
        diff = (h_ref[...] - t_ref[...]) * lat
        dh_ref[...] = diff * (1.0 / D)
        ls_ref[...] = jnp.zeros(ls_ref.shape, F32) + (0.5 / D) * jnp.sum(diff * diff)

    return pl.pallas_call(
        body, name=name, grid=(T // tm,),
        in_specs=[pl.BlockSpec((tm, D), lambda i: (i, 0)), pl.BlockSpec((tm, D), lambda i: (jnp.minimum(i, nlat - 1), 0))],
        out_specs=[pl.BlockSpec((tm, D), lambda i: (i, 0)), pl.BlockSpec((1, 8, 128), lambda i: (i, 0, 0))],
        out_shape=[jax.ShapeDtypeStruct((T, D), F32), jax.ShapeDtypeStruct((T // tm, 8, 128), F32)],
        compiler_params=_cparams("parallel"),
    )(h, target)


def _col_block(cols, target=1152):
    return max(t for t in range(128, min(cols, target) + 1, 128) if cols % t == 0)


def _mod_fwd(cond, w_mod, b_mod, name):
    L, D, C = w_mod.shape
    R = cond.shape[0]
    cb = _col_block(C)

    def body(c_ref, w_ref, b_ref, o_ref):
        cv = c_ref[...]
        sc = (cv * jax.nn.sigmoid(cv)).astype(BF16)
        o_ref[...] = _dot(sc, w_ref[...].astype(BF16)) + b_ref[...]

    return pl.pallas_call(
        body, name=name, grid=(L, C // cb),
        in_specs=[pl.BlockSpec((R, D), lambda l, c: (0, 0)), pl.BlockSpec((None, D, cb), lambda l, c: (l, 0, c)),
                  pl.BlockSpec((None, 1, cb), lambda l, c: (l, 0, c))],
        out_specs=pl.BlockSpec((None, R, cb), lambda l, c: (l, 0, c)),
        out_shape=jax.ShapeDtypeStruct((L, R, C), F32),
        compiler_params=_cparams("parallel", "parallel"),
    )(cond, w_mod, b_mod)


def _mod_bwd(cond, dmod, w_mod, name):
    L, D, C = w_mod.shape
    R = cond.shape[0]
    cb = _col_block(C)
    nc = C // cb

    def body(c_ref, dm_ref, w_ref, gw_ref, ds_ref):
        cv = c_ref[...]
        sc = (cv * jax.nn.sigmoid(cv)).astype(BF16)
        dmv = dm_ref[...].astype(BF16)
        gw_ref[...] = _dot_tn(sc, dmv)
        part = _dot_nt(dmv, w_ref[...].astype(BF16))

        @pl.when(pl.program_id(1) == 0)
        def _():
            ds_ref[...] = part

        @pl.when(pl.program_id(1) > 0)
        def _():
            ds_ref[...] += part

    return pl.pallas_call(
        body, name=name, grid=(L, nc),
        in_specs=[pl.BlockSpec((R, D), lambda l, c: (0, 0)), pl.BlockSpec((None, R, cb), lambda l, c: (l, 0, c)),
                  pl.BlockSpec((None, D, cb), lambda l, c: (l, 0, c))],
        out_specs=[pl.BlockSpec((None, D, cb), lambda l, c: (l, 0, c)), pl.BlockSpec((None, R, D), lambda l, c: (l, 0, 0))],
        out_shape=[jax.ShapeDtypeStruct((L, D, C), F32), jax.ShapeDtypeStruct((L, R, D), F32)],
        compiler_params=_cparams("parallel", "arbitrary"),
    )(cond, dmod, w_mod)


def _row_block(rows, cols, budget=1 << 20):
    best = None
    for t in range(16, rows + 1, 16):
        if rows % t == 0 and t * cols * 4 <= budget:
            best = t
    return best if best is not None else rows


def _sum_slots(recvs, owns, chip, core, bufs, pieces, piece, name):
    n = len(recvs)
    S, R, C = recvs[0].shape
    tr = _row_block(R, C, budget=512 << 10)

    def body(ids_ref, *refs):
        for r_ref, p_ref, o_ref in zip(refs[:n], refs[n:2 * n], refs[-n:]):
            acc = None
            for s in range(S):
                v = jnp.where(ids_ref[0] == s, p_ref[s], r_ref[s]).astype(F32)
                acc = v if acc is None else acc + v
            o_ref[...] = acc

    blk = pl.BlockSpec((S, tr, C), lambda i, ids: (0, i, 0))
    held = [] if bufs is None else list(bufs)
    return pl.pallas_call(
        body, name=name,
        grid_spec=pltpu.PrefetchScalarGridSpec(
            num_scalar_prefetch=1, grid=(R // tr,), in_specs=[blk] * (2 * n) + [pl.BlockSpec(memory_space=pl.ANY)] * len(held),
            out_specs=[pl.BlockSpec((None, None, tr, C), lambda i, ids: (piece, ids[1], i, 0))] * n),
        out_shape=[jax.ShapeDtypeStruct((pieces, 2, R, C), F32)] * n,
        input_output_aliases={1 + 2 * n + i: i for i in range(len(held))}, compiler_params=_cparams("parallel"),
    )(jnp.stack([chip, core]).astype(jnp.int32), *recvs, *owns, *held)


def _adamw(w, gs, m, v, name):
    ng = len(gs)
    R, C = w.shape
    tr = _row_block(R, C)
    c1 = 1.0 / (1.0 - ADAM_B1 ** ADAM_STEP)
    c2 = 1.0 / (1.0 - ADAM_B2 ** ADAM_STEP)

    def body(w_ref, *refs):
        m_ref, v_ref, g_ref, d_ref, mo_ref, vo_ref = refs[ng:]
        g = refs[0][...]
        for g_more in refs[1:ng]:
            g = g + g_more[...]
        g_ref[...] = g
        mn = ADAM_B1 * m_ref[...] + (1.0 - ADAM_B1) * g
        vn = ADAM_B2 * v_ref[...] + (1.0 - ADAM_B2) * (g * g)
        mo_ref[...] = mn
        vo_ref[...] = vn
        d_ref[...] = -ADAM_LR * ((mn * c1) / (jnp.sqrt(vn * c2) + ADAM_EPS) + ADAM_WD * w_ref[...])

    blk = pl.BlockSpec((tr, C), lambda i: (i, 0))
    sd = jax.ShapeDtypeStruct((R, C), F32)
    return pl.pallas_call(
        body, name=name, grid=(R // tr,), in_specs=[blk] * (3 + ng), out_specs=[blk] * 4, out_shape=[sd] * 4,
        compiler_params=_cparams("parallel"),
    )(w, *gs, m, v)


def _rope_tables(dm):
    n = dm.N
    t = jnp.arange(n)
    r = (t // GRID_W).astype(F32)
    col = (t % GRID_W).astype(F32)
    nf = QK_ROPE // 4
    inv = ROPE_BASE ** (-jnp.arange(nf, dtype=F32) / nf)
    ang = jnp.stack([r[:, None] * inv, col[:, None] * inv], axis=1)
    cos, sin = jnp.cos(ang), jnp.sin(ang)
    zero = jnp.zeros_like(sin)
    c64 = jnp.stack([cos, cos], axis=2).reshape(n, QK_ROPE)
    s1 = jnp.stack([-sin, zero], axis=2).reshape(n, QK_ROPE)
    s2 = jnp.stack([zero, sin], axis=2).reshape(n, QK_ROPE)

    def pad(x, fill):
        return jnp.concatenate([jnp.full((n, QK_NOPE), fill, F32), x, jnp.full((n, HEAD_PAD - QK_HEAD), fill, F32)], axis=1)

    lat = jnp.stack([pad(c64, 1.0), pad(s1, 0.0), pad(s2, 0.0)])
    lat = jnp.tile(lat, (1, dm.B, 1))
    nctx = dm.B * dm.CTX
    ctx = jnp.stack([jnp.ones((nctx, HEAD_PAD), F32), jnp.zeros((nctx, HEAD_PAD), F32), jnp.zeros((nctx, HEAD_PAD), F32)])
    return jnp.concatenate([lat, ctx], axis=1)


def _fold_parts(part, dm):
    nblk = part.shape[0]
    nb = (dm.N * nblk) // dm.T
    groups = [part[b * nb:(b + 1) * nb].sum(axis=0) for b in range(dm.B)]
    groups.append(part[dm.B * nb:].sum(axis=0))
    return jnp.stack(groups)


def grouped(items, key):
    groups = {}
    for it in items:
        groups.setdefault(key(it), []).append(it)
    return list(groups.values())


def _flat2(a):
    return a.reshape(-1, a.shape[-1])


def kernel(x, c, ctx, c_ctx, w_mod, b_mod, g_norm, ffn_w1, ffn_w3, ffn_w2, sc_w_in, sc_conv, sc_w_out, mla_w_a, mla_g_qa, mla_w_uq, mla_g_kva, mla_w_ukv, mla_g_q, mla_g_k, mla_w_o, loss_target, m_c_ctx, m_w_mod, m_b_mod, m_g_norm, m_ffn_w1, m_ffn_w3, m_ffn_w2, m_sc_w_in, m_sc_conv, m_sc_w_out, m_mla_w_a, m_mla_g_qa, m_mla_w_uq, m_mla_g_kva, m_mla_w_ukv, m_mla_g_q, m_mla_g_k, m_mla_w_o, v_c_ctx, v_w_mod, v_b_mod, v_g_norm, v_ffn_w1, v_ffn_w3, v_ffn_w2, v_sc_w_in, v_sc_conv, v_sc_w_out, v_mla_w_a, v_mla_g_qa, v_mla_w_uq, v_mla_g_kva, v_mla_w_ukv, v_mla_g_q, v_mla_g_k, v_mla_w_o):
    B, N, D = x.shape
    CTX = ctx.shape[1]
    T = B * (N + CTX)
    tm = next(t for t in (512, 256, 128, 64, 32, 16) if N % t == 0 and (B * CTX) % t == 0)
    dm = Dims(B, N, CTX, D, T, tm)
    L = w_mod.shape[0]
    La, Lb = sc_w_in.shape[0], mla_w_a.shape[0]
    S = N_CHIPS
    ndev = 2 * S
    xi, yi, ci = lax.axis_index("x"), lax.axis_index("y"), lax.axis_index("c")
    chip = 2 * xi + yi
    dev = 2 * chip + ci
    weights = dict(c_ctx=c_ctx, w_mod=w_mod, b_mod=b_mod, g_norm=g_norm, ffn_w1=ffn_w1, ffn_w3=ffn_w3, ffn_w2=ffn_w2,
                   sc_w_in=sc_w_in, sc_conv=sc_conv, sc_w_out=sc_w_out, mla_w_a=mla_w_a, mla_g_qa=mla_g_qa,
                   mla_w_uq=mla_w_uq, mla_g_kva=mla_g_kva, mla_w_ukv=mla_w_ukv, mla_g_q=mla_g_q, mla_g_k=mla_g_k,
                   mla_w_o=mla_w_o)
    mom = dict(c_ctx=(m_c_ctx, v_c_ctx), w_mod=(m_w_mod, v_w_mod), b_mod=(m_b_mod, v_b_mod), g_norm=(m_g_norm, v_g_norm),
               ffn_w1=(m_ffn_w1, v_ffn_w1), ffn_w3=(m_ffn_w3, v_ffn_w3), ffn_w2=(m_ffn_w2, v_ffn_w2),
               sc_w_in=(m_sc_w_in, v_sc_w_in), sc_conv=(m_sc_conv, v_sc_conv), sc_w_out=(m_sc_w_out, v_sc_w_out),
               mla_w_a=(m_mla_w_a, v_mla_w_a), mla_g_qa=(m_mla_g_qa, v_mla_g_qa), mla_w_uq=(m_mla_w_uq, v_mla_w_uq),
               mla_g_kva=(m_mla_g_kva, v_mla_g_kva), mla_w_ukv=(m_mla_w_ukv, v_mla_w_ukv), mla_g_q=(m_mla_g_q, v_mla_g_q),
               mla_g_k=(m_mla_g_k, v_mla_g_k), mla_w_o=(m_mla_w_o, v_mla_w_o))

    big = ["ffn_w1", "ffn_w3", "ffn_w2", "sc_w_in", "sc_w_out", "mla_w_a", "mla_w_uq", "mla_w_ukv", "mla_w_o"]
    F = ffn_w1.shape[-1]
    transposed = ("ffn_w1", "ffn_w3")
    for n in transposed:
        weights[n] = jnp.swapaxes(weights[n], 2, 3)
        mom[n] = tuple(jnp.swapaxes(a, 2, 3) for a in mom[n])
    mixer_names =(["sc_w_in", "sc_w_out"], ["mla_w_a", "mla_w_uq", "mla_w_ukv", "mla_w_o"])

    def placed(name, piece, npieces):
        w2 = _flat2(weights[name])
        rows = w2.shape[0] // npieces
        return _place_cast(w2, piece * rows, rows, chip, S, "place_weight")

    bufs = {}
    for l in range(L):
        for k in range(2):
            bufs["f", l, k] = {n: placed(n, 2 * l + k, 2 * L) for n in ("ffn_w1", "ffn_w3", "ffn_w2")}
        bufs["m", l] = {n: placed(n, l // 2, weights[n].shape[0]) for n in mixer_names[l % 2]}
    order = [stage for l in range(L) for stage in (("f", l, 0), ("m", l), ("f", l, 1))]

    def gather_after(stage):
        at = order.index(stage)
        if at + 1 == len(order):
            return None, lambda got: None
        nxt = bufs[order[at + 1]]
        names = list(nxt)
        return _gather_rider([nxt[n] for n in names]), lambda got: nxt.update(zip(names, got))

    names = list(bufs[order[0]])
    bufs[order[0]].update(zip(names, _ride_alone(_gather_rider([bufs[order[0]][n] for n in names]), "gather_weights")))

    def ffn_weights(l, k):
        b = bufs["f", l, k]
        return b["ffn_w1"], b["ffn_w3"], b["ffn_w2"]

    def mixer_weights(l):
        b = bufs["m", l]
        w = {}
        if l % 2 == 0:
            w["w_in"] = b["sc_w_in"][:, None]
            w["w_out"] = b["sc_w_out"].reshape(1, D, D)
        else:
            w["w_a"] = jnp.pad(b["mla_w_a"].reshape(1, D, -1), ((0, 0), (0, 0), (0, 512 - (Q_LORA + KV_LORA + QK_ROPE))))
            wuq = jnp.moveaxis(b["mla_w_uq"], 0, 1).reshape(1, Q_LORA, HEADS, QK_HEAD)
            w["w_uq"] = jnp.pad(wuq, ((0, 0), (0, 0), (0, 0), (0, HEAD_PAD - QK_HEAD))).reshape(1, Q_LORA, HEADS * HEAD_PAD)
            w["w_ukv"] = jnp.moveaxis(b["mla_w_ukv"], 0, 1).reshape(1, KV_LORA, HEADS * HEAD_PAD)
            w["w_o"] = b["mla_w_o"].reshape(1, HEADS * V_HEAD, D)
        return w

    vecs = ["g_norm", "sc_conv", "mla_g_qa"]
    gathered = _exchange([_flat2(weights[n]) for n in vecs], ("x", "y"), False, "gather_vectors")
    gw = {n: g.reshape((S,) + weights[n].shape) for n, g in zip(vecs, gathered)}
    gnorm = jnp.moveaxis(gw["g_norm"], 0, 2).reshape(L, 3, D)
    convw = jnp.moveaxis(gw["sc_conv"], 0, 2).reshape(La, 3, D)
    gqa = jnp.moveaxis(gw["mla_g_qa"], 0, 1).reshape(Lb, Q_LORA)
    padl = lambda a: jnp.pad(a, ((0, 0), (0, HEAD_PAD - a.shape[1])))
    gains = jnp.stack([padl(gqa), padl(mla_g_kva), padl(mla_g_q), padl(mla_g_k)], axis=1)
    gains = jnp.pad(gains, ((0, 0), (0, 4), (0, 0)))

    R = -(-(ndev * B + 1) // 16) * 16
    call = _exchange([c], ("x", "y", "c"), False, "gather_cond")[0].reshape(ndev * B, D)
    cond = jnp.concatenate([call, c_ctx[None], jnp.zeros((R - ndev * B - 1, D), F32)], axis=0)
    C = w_mod.shape[-1]
    bm = lax.dynamic_slice_in_dim(b_mod, chip * C, C, axis=1)[:, None, :]
    mshard = _mod_fwd(cond, w_mod, bm, "mod_fwd")
    mfull = _exchange([mshard.reshape(L * R, C)], ("x", "y"), False, "gather_mod")[0].reshape(S, L, R, C)
    mfull = jnp.moveaxis(mfull, 0, 2).reshape(L, R, S * C)
    mine = lax.dynamic_slice_in_dim(mfull, dev * B, B, axis=1)
    mod = jnp.concatenate([mine, mfull[:, ndev * B:ndev * B + 1]], axis=1).reshape(L, B + 1, 9, D)

    tabs = _rope_tables(dm)
    h = jnp.concatenate([x.reshape(B * N, D), ctx.reshape(B * CTX, D)], axis=0)

    saved = []
    lw = [None] * L
    for l in range(L):
        kind, j = l % 2, l // 2
        sv = {}
        sv["h0"] = h
        rider, keep = gather_after(("f", l, 0))
        (h, sv["a1"], sv["b1"], sv["hn1"], sv["y1"]), got = _ffn_fwd(h, mod[l], gnorm[l, 0:1], *ffn_weights(l, 0), 0, dm,
                                                                      "ffn_fwd", rider)
        keep(got)
        sv["h1"] = h
        W = lw[l] = mixer_weights(l)
        rider, keep = gather_after(("m", l))
        if kind == 0:
            (sv["p"], sv["hnm"]), got = _sc_in_fwd(h, mod[l], gnorm[l, 1:2], W["w_in"], 0, dm, "sc_in_fwd", rider)
            sv["z"] = _conv_fwd(sv["p"], convw[j], dm, "conv_fwd")
            h, sv["ym"] = _out_fwd(sv["z"], W["w_out"], h, mod[l], 0, dm, "sc_out_fwd")
        else:
            sv["hnm"], sv["q"], sv["k"], sv["v"] = _mla_proj_fwd(h, mod[l], gnorm[l, 1:2], gains[j:j + 1], tabs, W["w_a"], W["w_uq"],
                                                                 W["w_ukv"], 0, dm, "mla_proj_fwd")
            sv["o"], got = _attn_fwd(sv["q"], sv["k"], sv["v"], dm, "attn_fwd", rider)
            h, sv["ym"] = _out_fwd(sv["o"], W["w_o"], h, mod[l], 0, dm, "mla_out_fwd")
        keep(got)
        sv["h2"] = h
        rider, keep = gather_after(("f", l, 1))
        (h, sv["a2"], sv["b2"], sv["hn2"], sv["y2"]), got = _ffn_fwd(h, mod[l], gnorm[l, 2:3], *ffn_weights(l, 1), 1, dm,
                                                                      "ffn_fwd", rider)
        keep(got)
        saved.append(sv)

    dh, lsum = _loss_grad(h, loss_target.reshape(B * N, D), dm, "loss_grad")
    loss = lax.psum(jnp.sum(lsum[:, 0, 0]), ("x", "y", "c"))

    wq = D // S
    gsum = {n: None for n in big}
    npieces = {n: weights[n].shape[0] * (weights[n].shape[1] if n.startswith("ffn") else 1) for n in big}
    dmod = [None] * L
    dgn = [None] * L
    dconv = [None] * La
    dgains = [None] * Lb
    tk = tm * next(f for f in (3, 2, 1) if (T // tm) % f == 0)
    nk = T // tk
    full_a = pl.BlockSpec((tk, D), lambda s, kk: (kk, 0))
    shard_b = pl.BlockSpec((None, tk, F), lambda s, kk: (s, kk, 0))
    per_slot = lambda r_, c_: pl.BlockSpec((None, r_, c_), lambda s, kk: (s, 0, 0))

    def make_job(grads):
        parts = [g_ for _, _, g_ in grads]
        theirs = _swap_halves(parts, "swap_halves")
        pairs = [None] * len(grads)
        for idx in grouped(range(len(grads)), lambda i: parts[i].shape):
            outs = _pair_sum([parts[i] for i in idx], [theirs[i] for i in idx], ci, "pair_sum")
            for i, o in zip(idx, outs):
                pairs[i] = o
        return [(n, p, pair) for (n, p, _), pair in zip(grads, pairs)]

    def finish_job(job, recv):
        key = lambda i: (recv[i].shape, npieces[job[i][0]], job[i][1], gsum[job[i][0]] is None)
        for idx in grouped(range(len(job)), key):
            names_ = [job[i][0] for i in idx]
            held = None if gsum[names_[0]] is None else [gsum[n] for n in names_]
            outs = _sum_slots([recv[i] for i in idx], [job[i][2] for i in idx], chip, ci, held, npieces[names_[0]],
                              job[idx[0]][1], "sum_slots")
            gsum.update(zip(names_, outs))

    def ffn_back(dh, sv, l, k, job):
        sfx = "1" if k == 0 else "2"
        rider = _scatter_rider([pair for _, _, pair in job]) if job else None
        (dh, da, db, sw, dy, part), recv = _ffn_bwd(dh, sv["h0" if k == 0 else "h2"], mod[l], gnorm[l, 2 * k:2 * k + 1], sv["y" + sfx],
                                                    sv["a" + sfx], sv["b" + sfx], *ffn_weights(l, k), k, dm, "ffn_bwd", rider)
        finish_job(job, recv)
        g1 = _mm_tn(da, sv["hn" + sfx], shard_b, full_a, (S, F, D), per_slot(F, D), (S, nk), "gw1")
        g3 = _mm_tn(db, sv["hn" + sfx], shard_b, full_a, (S, F, D), per_slot(F, D), (S, nk), "gw3")
        g2 = _mm_tn(sw, dy, shard_b, full_a, (S, F, D), per_slot(F, D), (S, nk), "gw2")
        p = 2 * l + k
        return dh, _fold_parts(part, dm), [("ffn_w1", p, g1), ("ffn_w3", p, g3), ("ffn_w2", p, g2)]

    one = (1, nk)
    a1 = lambda kdim: pl.BlockSpec((tk, kdim), lambda s, kk: (kk, 0))
    pending = []
    for l in reversed(range(L)):
        kind, j = l % 2, l // 2
        sv = saved[l]
        W = lw[l]
        dh, p2, grads = ffn_back(dh, sv, l, 1, pending)
        job2 = make_job(grads)
        if kind == 0:
            dy, dz, pg = _out_bwd(dh, sv["ym"], W["w_out"], mod[l], 0, dm, "sc_out_bwd")
            g_out = _mm_tn(sv["z"], dy, a1(D), a1(D), (1, D, D), per_slot(D, D), one, "gw_sc_out")
            dp, dconv[j] = _conv_bwd(dz, sv["p"], convw[j], dm, "conv_bwd")
            g_in = _mm_tn(sv["hnm"], dp, pl.BlockSpec((tk, D), lambda q, kk: (kk, 0)),
                          pl.BlockSpec((None, tk, wq), lambda q, kk: (q // S, kk, q % S)), (3 * S, D, wq), per_slot(D, wq),
                          (3 * S, nk), "gw_sc_in")
            dh, pm = _sc_in_bwd(dh, dp, sv["h1"], mod[l], gnorm[l, 1:2], W["w_in"], 0, dm, "sc_in_bwd")
            grads = [("sc_w_in", j, jnp.moveaxis(g_in.reshape(S, 3, D, wq), 1, 2).reshape(S, D, 3 * wq)),
                     ("sc_w_out", j, g_out.reshape(S, D // S, D))]
        else:
            dy, do, pg = _out_bwd(dh, sv["ym"], W["w_o"], mod[l], 0, dm, "mla_out_bwd")
            g_o = _mm_tn(sv["o"], dy, a1(HEADS * V_HEAD), a1(D), (1, HEADS * V_HEAD, D), per_slot(HEADS * V_HEAD, D), one, "gw_mla_o")
            dq, dkl, dkc, dvl, dvc = _attn_bwd(sv["q"], sv["k"], sv["v"], sv["o"], do, dm, "attn_bwd")
            dh, pm, g_a, g_uq, g_ukv, dgains[j] = _mla_proj_bwd(
                dh, dq, dkl, dkc, dvl, dvc, sv["h1"], mod[l], gnorm[l, 1:2], gains[j:j + 1], tabs, W["w_a"], W["w_uq"], W["w_ukv"], 0, dm, "mla_proj_bwd")
            g_uq = g_uq.reshape(Q_LORA, HEADS, HEAD_PAD)[..., :QK_HEAD].reshape(Q_LORA, S, -1)
            grads = [("mla_w_a", j, g_a[:, :Q_LORA + KV_LORA + QK_ROPE].reshape(S, D // S, -1).astype(BF16)),
                     ("mla_w_uq", j, jnp.moveaxis(g_uq, 1, 0).astype(BF16)),
                     ("mla_w_ukv", j, jnp.moveaxis(g_ukv.reshape(KV_LORA, S, -1), 1, 0).astype(BF16)),
                     ("mla_w_o", j, g_o.reshape(S, HEADS * V_HEAD // S, D))]
        jobm = make_job(grads)
        pm = _fold_parts(pm, dm) + _fold_parts(pg, dm)
        dh, p0, grads = ffn_back(dh, sv, l, 0, job2 + jobm)
        pending = make_job(grads)
        dmod[l] = jnp.concatenate([p0[:, 0:3], pm[:, 0:3], p2[:, 0:3]], axis=1).reshape(B + 1, 9 * D)
        dgn[l] = jnp.stack([p0[:, 3].sum(0), pm[:, 3].sum(0), p2[:, 3].sum(0)])
    grad_x = dh[:B * N].reshape(B, N, D)

    dgains_a = jnp.stack(dgains)
    small = [jnp.stack(dmod).reshape(-1), jnp.stack(dgn).reshape(-1), jnp.stack(dconv).reshape(-1), dgains_a.reshape(-1)]
    sizes = [s_.shape[0] for s_ in small]
    flat = jnp.concatenate(small)
    pad = (-flat.shape[0]) % 1024
    flat = jnp.pad(flat, (0, pad)).reshape(-1, 128)
    allsmall = _exchange([flat], ("x", "y", "c"), False, "gather_small")[0].reshape(ndev, -1)
    offs = [0]
    for s_ in sizes:
        offs.append(offs[-1] + s_)
    dmod_all = allsmall[:, offs[0]:offs[1]].reshape(ndev, L, B + 1, 9 * D)
    tot = allsmall[:, offs[1]:offs[4]].sum(axis=0)
    g_gnorm = tot[:offs[2] - offs[1]].reshape(L, 3, D)
    g_conv = tot[offs[2] - offs[1]:offs[3] - offs[1]].reshape(La, 3, D)
    g_gains = tot[offs[3] - offs[1]:].reshape(Lb, 8, HEAD_PAD)
    dM = jnp.concatenate([jnp.moveaxis(dmod_all[:, :, :B], 0, 1).reshape(L, ndev * B, 9 * D),
                          dmod_all[:, :, B].sum(axis=0)[:, None, :], jnp.zeros((L, R - ndev * B - 1, 9 * D), F32)], axis=1)
    g_bmod = dM.sum(axis=1)
    dM_mine = lax.dynamic_slice_in_dim(dM, chip * C, C, axis=2)
    g_wmod, dsil = _mod_bwd(cond, dM_mine, w_mod, "mod_bwd")
    dsil_ctx = dsil[:, ndev * B].sum(axis=0)
    dsil_all = _exchange([jnp.pad(dsil_ctx.reshape(-1, 128), ((0, (-(D // 128)) % 8), (0, 0)))], ("x", "y"), False, "gather_dctx")[0]
    dsil_tot = dsil_all.sum(axis=0)[:D // 128].reshape(D)
    sg = jax.nn.sigmoid(c_ctx)
    g_cctx = dsil_tot * (sg * (1.0 + c_ctx * (1.0 - sg)))

    chip_cols = lambda a, width: lax.dynamic_slice_in_dim(a, chip * width, width, axis=a.ndim - 1)
    small_grads = dict(
        c_ctx=g_cctx, b_mod=g_bmod, g_norm=chip_cols(g_gnorm, D // S), sc_conv=chip_cols(g_conv, D // S),
        mla_g_qa=chip_cols(g_gains[:, 0, :Q_LORA], Q_LORA // S), mla_g_kva=g_gains[:, 1, :KV_LORA],
        mla_g_q=g_gains[:, 2, :QK_HEAD], mla_g_k=g_gains[:, 3, :QK_HEAD])

    finish_job(pending, _ride_alone(_scatter_rider([pair for _, _, pair in pending]), "scatter_grads"))
    gsum = dict(zip(big, _swap_cores_inplace([gsum[n] for n in big], "swap_cores")))

    grads, deltas, new_m, new_v = {}, {}, {}, {}
    for n, w in weights.items():
        shape = w.shape
        w2 = _flat2(w) if w.ndim > 1 else w.reshape(1, -1)
        m2, v2 = (a.reshape(w2.shape) for a in mom[n])
        if n in gsum:
            gs = [gsum[n].reshape(w2.shape)]
        elif n == "w_mod":
            gs = [_flat2(g_wmod)]
        else:
            gs = [small_grads[n].reshape(w2.shape)]
        g_, d_, m_, v_ = _adamw(w2, gs, m2, v2, "adamw")
        grads[n], deltas[n], new_m[n], new_v[n] = (a.reshape(shape) for a in (g_, d_, m_, v_))
    for n in transposed:
        grads[n], deltas[n], new_m[n], new_v[n] = (jnp.swapaxes(a, 2, 3) for a in (grads[n], deltas[n], new_m[n], new_v[n]))

    names = list(weights)
    return (loss, grad_x, *[grads[n] for n in names], *[deltas[n] for n in names], *[new_m[n] for n in names],
            *[new_v[n] for n in names])
```

```python
import functools
import math
from typing import NamedTuple

import jax
import jax.numpy as jnp
from jax import lax
from jax.experimental import pallas as pl
from jax.experimental.pallas import tpu as pltpu

F32 = jnp.float32
BF16 = jnp.bfloat16
EPS = 1e-6
GRID_W = 64
HEADS = 8
QK_NOPE = 128
QK_ROPE = 64
QK_HEAD = QK_NOPE + QK_ROPE
HEAD_PAD = 256
V_HEAD = 128
Q_LORA = 256
KV_LORA = 128
ROPE_BASE = 10000.0
QK_SCALE = QK_HEAD ** -0.5
ADAM_LR, ADAM_B1, ADAM_B2, ADAM_EPS, ADAM_WD, ADAM_STEP = 0.001, 0.9, 0.999, 1e-08, 0.01, 10
N_CHIPS = 4
VMEM_LIMIT = 56 * 1024 * 1024
MESH = pl.DeviceIdType.MESH
NEG = -1e30


class Dims(NamedTuple):
    B: int
    N: int
    CTX: int
    D: int
    T: int
    tm: int


def _cparams(*sem):
    return pltpu.CompilerParams(dimension_semantics=sem if sem else None, vmem_limit_bytes=VMEM_LIMIT)


def _dot(a, b):
    return jnp.dot(a, b, preferred_element_type=F32)


def _dot_nt(a, b):
    return lax.dot_general(a, b, (((1,), (1,)), ((), ())), preferred_element_type=F32)


def _dot_tn(a, b):
    return lax.dot_general(a, b, (((0,), (0,)), ((), ())), preferred_element_type=F32)


def _rms(x, n):
    r = lax.rsqrt(jnp.sum(x * x, axis=-1, keepdims=True) * (1.0 / n) + EPS)
    return x * r, r


def _rms_bwd(dxh, xh, r, n):
    return r * (dxh - xh * (jnp.sum(dxh * xh, axis=-1, keepdims=True) * (1.0 / n)))


def _pre(h, g, shift, scale):
    xh, _ = _rms(h, h.shape[-1])
    return (xh * g) * (1.0 + scale) + shift


def _pre_bwd(dout, h, g, scale):
    d = h.shape[-1]
    xh, r = _rms(h, d)
    n = xh * g
    dshift = jnp.sum(dout, axis=0, keepdims=True)
    dscale = jnp.sum(dout * n, axis=0, keepdims=True)
    dn = dout * (1.0 + scale)
    dg = jnp.sum(dn * xh, axis=0, keepdims=True)
    dh = _rms_bwd(dn * g, xh, r, d)
    return dh, dshift, dscale, dg


def _write_part(part_ref, dshift=None, dscale=None, dgate=None, dg=None):
    z = jnp.zeros((1, part_ref.shape[-1]), F32)
    part_ref[0, 0:1, :] = z if dshift is None else dshift
    part_ref[0, 1:2, :] = z if dscale is None else dscale
    part_ref[0, 2:3, :] = z if dgate is None else dgate
    part_ref[0, 3:4, :] = z if dg is None else dg
    part_ref[0, 4:8, :] = jnp.zeros((4, part_ref.shape[-1]), F32)


def _grp(dm):
    nb = dm.N // dm.tm
    return lambda i: jnp.minimum(i // nb, dm.B)


def _n_chunks(rows, row_bytes):
    n = 16
    while n > 1 and (rows % (16 * n) or (rows // n) * row_bytes < (256 << 10)):
        n //= 2
    return n


def _start_local(src, dst, sems, k0, nchunk):
    ch = src.shape[0] // nchunk
    copies = []
    for j in range(nchunk):
        cp = pltpu.make_async_copy(src.at[pl.ds(j * ch, ch)], dst.at[pl.ds(j * ch, ch)], sems.at[k0 + j])
        cp.start()
        copies.append(cp)
    return copies


def _gather_small(arrs, axes, name):
    n = len(arrs)
    nbits = len(axes)
    slots = 2 ** nbits
    pats = list(range(1, slots))
    nck = [_n_chunks(a.shape[-2], a.shape[-1] * a.dtype.itemsize) for a in arrs]
    base = [sum(nck[:i]) * len(pats) for i in range(n)]
    nsem = sum(nck) * len(pats)

    def body(*refs):
        ins, outs = refs[:n], refs[n:2 * n]
        send, recv, loc = refs[2 * n:]
        pos = {a: lax.axis_index(a) for a in ("x", "y", "c")}

        def slot_of(p):
            s = 0
            for a in axes:
                s = 2 * s + p[a]
            return s

        me = slot_of(pos)
        local = []
        for i in range(n):
            local += _start_local(ins[i], outs[i].at[me], loc, sum(nck[:i]), nck[i])
        remote = []
        for pi, pat in enumerate(pats):
            peer = dict(pos)
            for bi, a in enumerate(axes):
                if (pat >> (nbits - 1 - bi)) & 1:
                    peer[a] = 1 - pos[a]
            them = slot_of(peer)
            for i in range(n):
                ch = arrs[i].shape[-2] // nck[i]
                for j in range(nck[i]):
                    k = base[i] + pi * nck[i] + j
                    rs = pl.ds(j * ch, ch)
                    cp = pltpu.make_async_remote_copy(
                        src_ref=ins[i].at[rs], dst_ref=outs[i].at[me, rs], send_sem=send.at[k], recv_sem=recv.at[k],
                        device_id=(peer["x"], peer["y"], peer["c"]), device_id_type=MESH)
                    cp.start()
                    remote.append(cp)
        for cp in local:
            cp.wait()
        for cp in remote:
            cp.wait()

    out_shape = [jax.ShapeDtypeStruct((slots,) + a.shape, a.dtype) for a in arrs]
    any_spec = pl.BlockSpec(memory_space=pl.ANY)
    outs = pl.pallas_call(
        body, name=name, out_shape=out_shape, in_specs=[any_spec] * n, out_specs=[any_spec] * n,
        scratch_shapes=[pltpu.SemaphoreType.DMA((nsem,)), pltpu.SemaphoreType.DMA((nsem,)), pltpu.SemaphoreType.DMA((sum(nck),))],
        compiler_params=pltpu.CompilerParams(has_side_effects=True),
    )(*arrs)
    return list(outs)


class Rider(NamedTuple):
    ins: list
    out_shapes: list
    aliases: dict
    sems: list
    start: object
    mid: object
    end: object


MID_STEPS = 6


def _hosted(body, rider, *, name, grid, in_specs, out_specs, out_shape, scratch_shapes, sem, args):
    if rider is None:
        outs = pl.pallas_call(body, name=name, grid=grid, in_specs=in_specs, out_specs=out_specs, out_shape=out_shape,
                              scratch_shapes=scratch_shapes, compiler_params=_cparams(*sem))(*args)
        return outs, []
    n_in, n_out, n_s = len(in_specs), len(out_specs), len(scratch_shapes)
    nri, nro = len(rider.ins), len(rider.out_shapes)
    nsteps = math.prod(grid)

    def wrapped(*refs):
        bounds = [0, n_in, n_in + nri, n_in + nri + n_out, n_in + nri + n_out + nro, n_in + nri + n_out + nro + n_s, len(refs)]
        ins, rins, outs, routs, scr, sems = (refs[lo:hi] for lo, hi in zip(bounds[:-1], bounds[1:]))
        step = 0
        for ax, extent in enumerate(grid):
            step = step * extent + pl.program_id(ax)

        @pl.when(step == 0)
        def _():
            rider.start(rins, routs, sems)

        body(*ins, *outs, *scr)

        if rider.mid is not None:
            @pl.when(step == max(nsteps - 1 - MID_STEPS, 0))
            def _():
                rider.mid(rins, routs, sems)

        @pl.when(step == nsteps - 1)
        def _():
            rider.end(rins, routs, sems)

    any_spec = pl.BlockSpec(memory_space=pl.ANY)
    outs = pl.pallas_call(
        wrapped, name=name, grid=grid, in_specs=list(in_specs) + [any_spec] * nri, out_specs=list(out_specs) + [any_spec] * nro,
        out_shape=list(out_shape) + list(rider.out_shapes), scratch_shapes=list(scratch_shapes) + list(rider.sems),
        input_output_aliases={n_in + i: n_out + o for i, o in rider.aliases.items()},
        compiler_params=pltpu.CompilerParams(dimension_semantics=("arbitrary",) * len(grid), vmem_limit_bytes=VMEM_LIMIT,
                                             has_side_effects=True),
    )(*args, *rider.ins)
    return outs[:n_out], list(outs[n_out:])


def _gather_rider(bufs):
    n = len(bufs)
    halves = [a.shape[1] // 2 for a in bufs]
    nck = [_n_chunks(h, a.shape[2] * a.dtype.itemsize) for h, a in zip(halves, bufs)]
    base = [3 * sum(nck[:i]) for i in range(n)]
    nsem = 3 * sum(nck)

    def plan():
        x, y, c = lax.axis_index("x"), lax.axis_index("y"), lax.axis_index("c")
        pieces = []
        for pi, (px, py) in enumerate([(x, 1 - y), (1 - x, y), (1 - x, 1 - y)]):
            for i in range(n):
                ch = halves[i] // nck[i]
                for j in range(nck[i]):
                    pieces.append((base[i] + pi * nck[i] + j, px, py, 2 * px + py, i, j * ch, ch))
        return x, y, c, 2 * x + y, pieces

    def rows(i, off, ch, core):
        return pl.ds(pl.multiple_of(core * halves[i] + off, 16), ch)

    def over_ici(outs, sems, c, slot, k, px, py, i, off, ch):
        ref = outs[i].at[slot, rows(i, off, ch, c)]
        return pltpu.make_async_remote_copy(src_ref=ref, dst_ref=ref, send_sem=sems[0].at[k], recv_sem=sems[1].at[k],
                                            device_id=(px, py, c), device_id_type=MESH)

    def over_d2d(outs, sems, x, y, c, slot, k, i, off, ch, core):
        ref = outs[i].at[slot, rows(i, off, ch, core)]
        return pltpu.make_async_remote_copy(src_ref=ref, dst_ref=ref, send_sem=sems[2].at[k], recv_sem=sems[3].at[k],
                                            device_id=(x, y, 1 - c), device_id_type=MESH)

    def start(ins, outs, sems):
        x, y, c, me, pieces = plan()
        for k, px, py, them, i, off, ch in pieces:
            over_ici(outs, sems, c, me, k, px, py, i, off, ch).start()

    def mid(ins, outs, sems):
        x, y, c, me, pieces = plan()
        for k, px, py, them, i, off, ch in pieces:
            over_ici(outs, sems, c, them, k, px, py, i, off, ch).wait_recv()
            over_d2d(outs, sems, x, y, c, them, k, i, off, ch, c).start()

    def end(ins, outs, sems):
        x, y, c, me, pieces = plan()
        for k, px, py, them, i, off, ch in pieces:
            over_ici(outs, sems, c, me, k, px, py, i, off, ch).wait_send()
            over_d2d(outs, sems, x, y, c, them, k, i, off, ch, c).wait_send()
        for k, px, py, them, i, off, ch in pieces:
            over_d2d(outs, sems, x, y, c, them, k, i, off, ch, 1 - c).wait_recv()

    return Rider(ins=list(bufs), out_shapes=[jax.ShapeDtypeStruct(a.shape, a.dtype) for a in bufs],
                 aliases={i: i for i in range(n)}, sems=[pltpu.SemaphoreType.DMA((nsem,))] * 4, start=start, mid=mid, end=end)


def _scatter_rider(srcs):
    n = len(srcs)
    nck = [_n_chunks(a.shape[1], a.shape[2] * a.dtype.itemsize) for a in srcs]
    base = [3 * sum(nck[:i]) for i in range(n)]
    nsem = 3 * sum(nck)

    def copies(ins, outs, sems):
        x, y, c = lax.axis_index("x"), lax.axis_index("y"), lax.axis_index("c")
        me = 2 * x + y
        for pi, (px, py) in enumerate([(x, 1 - y), (1 - x, y), (1 - x, 1 - y)]):
            for i in range(n):
                ch = srcs[i].shape[1] // nck[i]
                for j in range(nck[i]):
                    k = base[i] + pi * nck[i] + j
                    rs = pl.ds(j * ch, ch)
                    yield pltpu.make_async_remote_copy(
                        src_ref=ins[i].at[2 * px + py, rs], dst_ref=outs[i].at[me, rs], send_sem=sems[0].at[k],
                        recv_sem=sems[1].at[k], device_id=(px, py, c), device_id_type=MESH)

    def start(ins, outs, sems):
        for cp in copies(ins, outs, sems):
            cp.start()

    def end(ins, outs, sems):
        for cp in copies(ins, outs, sems):
            cp.wait()

    return Rider(ins=list(srcs), out_shapes=[jax.ShapeDtypeStruct(a.shape, a.dtype) for a in srcs], aliases={},
                 sems=[pltpu.SemaphoreType.DMA((nsem,))] * 2, start=start, mid=None, end=end)


def _ride_alone(rider, name):
    n_in, n_out = len(rider.ins), len(rider.out_shapes)

    def body(*refs):
        ins, outs, sems = refs[:n_in], refs[n_in:n_in + n_out], refs[n_in + n_out:]
        rider.start(ins, outs, sems)
        if rider.mid is not None:
            rider.mid(ins, outs, sems)
        rider.end(ins, outs, sems)

    any_spec = pl.BlockSpec(memory_space=pl.ANY)
    outs = pl.pallas_call(
        body, name=name, out_shape=list(rider.out_shapes), in_specs=[any_spec] * n_in, out_specs=[any_spec] * n_out,
        scratch_shapes=list(rider.sems), input_output_aliases=dict(rider.aliases),
        compiler_params=pltpu.CompilerParams(has_side_effects=True),
    )(*rider.ins)
    return list(outs)


def _swap_cores_inplace(bufs, name):
    n = len(bufs)
    nck = [_n_chunks(a.shape[2], a.shape[3] * a.dtype.itemsize) for a in bufs]
    base = [sum(a.shape[0] * k for a, k in zip(bufs[:i], nck[:i])) for i in range(n)]
    nsem = sum(a.shape[0] * k for a, k in zip(bufs, nck))

    def body(*refs):
        outs = refs[n:2 * n]
        send, recv = refs[2 * n:]
        x, y, c = lax.axis_index("x"), lax.axis_index("y"), lax.axis_index("c")

        def copies(core):
            for i in range(n):
                ch = bufs[i].shape[2] // nck[i]
                for p in range(bufs[i].shape[0]):
                    for j in range(nck[i]):
                        k = base[i] + p * nck[i] + j
                        ref = outs[i].at[p, core, pl.ds(j * ch, ch)]
                        yield pltpu.make_async_remote_copy(src_ref=ref, dst_ref=ref, send_sem=send.at[k], recv_sem=recv.at[k],
                                                           device_id=(x, y, 1 - c), device_id_type=MESH)

        for cp in copies(c):
            cp.start()
        for cp in copies(c):
            cp.wait_send()
        for cp in copies(1 - c):
            cp.wait_recv()

    any_spec = pl.BlockSpec(memory_space=pl.ANY)
    outs = pl.pallas_call(
        body, name=name, out_shape=[jax.ShapeDtypeStruct(a.shape, a.dtype) for a in bufs], in_specs=[any_spec] * n,
        out_specs=[any_spec] * n, scratch_shapes=[pltpu.SemaphoreType.DMA((nsem,))] * 2,
        input_output_aliases={i: i for i in range(n)}, compiler_params=pltpu.CompilerParams(has_side_effects=True),
    )(*bufs)
    return list(outs)


def _place_cast(ws, row0, rows, slot, slots, name):
    n = len(ws)
    C = ws[0].shape[1]
    tr = _row_block(rows, C)
    blk0 = row0 // tr

    def body(slot_ref, *refs):
        del slot_ref
        for w_ref, o_ref in zip(refs[:n], refs[n:]):
            o_ref[...] = w_ref[...].astype(BF16)

    return pl.pallas_call(
        body, name=name,
        grid_spec=pltpu.PrefetchScalarGridSpec(
            num_scalar_prefetch=1, grid=(rows // tr,), in_specs=[pl.BlockSpec((tr, C), lambda i, sr: (blk0 + i, 0))] * n,
            out_specs=[pl.BlockSpec((None, tr, C), lambda i, sr: (sr[0], i, 0))] * n),
        out_shape=[jax.ShapeDtypeStruct((slots, rows, C), BF16)] * n,
        compiler_params=_cparams("parallel"),
    )(slot.reshape(1).astype(jnp.int32), *ws)


def _swap_halves(arrs, name):
    n = len(arrs)
    S = arrs[0].shape[0]
    halves = [a.shape[1] // 2 for a in arrs]
    nck = [_n_chunks(h, a.shape[2] * a.dtype.itemsize) for h, a in zip(halves, arrs)]
    base = [S * sum(nck[:i]) for i in range(n)]
    nsem = S * sum(nck)

    def body(*refs):
        ins, outs = refs[:n], refs[n:2 * n]
        send, recv = refs[2 * n:]
        x, y, c = lax.axis_index("x"), lax.axis_index("y"), lax.axis_index("c")
        copies = []
        for i in range(n):
            ch = halves[i] // nck[i]
            for s in range(S):
                for j in range(nck[i]):
                    k = base[i] + s * nck[i] + j
                    src = ins[i].at[s, pl.ds(pl.multiple_of((1 - c) * halves[i] + j * ch, 16), ch)]
                    cp = pltpu.make_async_remote_copy(src_ref=src, dst_ref=outs[i].at[s, pl.ds(j * ch, ch)], send_sem=send.at[k],
                                                      recv_sem=recv.at[k], device_id=(x, y, 1 - c), device_id_type=MESH)
                    cp.start()
                    copies.append(cp)
        for cp in copies:
            cp.wait()

    any_spec = pl.BlockSpec(memory_space=pl.ANY)
    outs = pl.pallas_call(
        body, name=name, out_shape=[jax.ShapeDtypeStruct((S, h, a.shape[2]), a.dtype) for h, a in zip(halves, arrs)],
        in_specs=[any_spec] * n, out_specs=[any_spec] * n,
        scratch_shapes=[pltpu.SemaphoreType.DMA((nsem,))] * 2,
        compiler_params=pltpu.CompilerParams(has_side_effects=True),
    )(*arrs)
    return list(outs)


def _pair_sum(gs, rs, core, name):
    n = len(gs)
    S, rows, C = gs[0].shape
    half = rows // 2
    tr = _row_block(half, C)
    nb = half // tr

    def body(core_ref, *refs):
        del core_ref
        for g_ref, r_ref, o_ref in zip(refs[:n], refs[n:2 * n], refs[2 * n:]):
            o_ref[...] = (g_ref[...].astype(F32) + r_ref[...].astype(F32)).astype(BF16)

    blk = pl.BlockSpec((None, tr, C), lambda s, i, cr: (s, i, 0))
    mine = pl.BlockSpec((None, tr, C), lambda s, i, cr: (s, cr[0] * nb + i, 0))
    return pl.pallas_call(
        body, name=name,
        grid_spec=pltpu.PrefetchScalarGridSpec(num_scalar_prefetch=1, grid=(S, nb), in_specs=[mine] * n + [blk] * n,
                                               out_specs=[blk] * n),
        out_shape=[jax.ShapeDtypeStruct((S, half, C), BF16)] * n,
        compiler_params=_cparams("parallel", "parallel"),
    )(core.reshape(1).astype(jnp.int32), *gs, *rs)


def _ffn_fwd(h, mod, g, w1, w3, w2, k, dm, name, rider=None):
    T, D = h.shape
    S, F = w1.shape[0], w1.shape[-2]
    tm = dm.tm
    r0 = 6 if k else 0
    grp = _grp(dm)

    def body(h_ref, mod_ref, g_ref, w1_ref, w3_ref, w2_ref, ho_ref, a_ref, b_ref, hn_ref, y_ref, hn_s, acc):
        s = pl.program_id(1)

        @pl.when(s == 0)
        def _():
            hn = _pre(h_ref[...], g_ref[...], mod_ref[0, r0:r0 + 1, :], mod_ref[0, r0 + 1:r0 + 2, :]).astype(BF16)
            hn_s[...] = hn
            hn_ref[...] = hn
            acc[...] = jnp.zeros_like(acc)

        hn = hn_s[...]
        a = _dot_nt(hn, w1_ref[...])
        b = _dot_nt(hn, w3_ref[...])
        a_ref[0] = a.astype(BF16)
        b_ref[0] = b.astype(BF16)
        sw = (a * jax.nn.sigmoid(a) * b).astype(BF16)
        acc[...] += _dot(sw, w2_ref[...])

        @pl.when(s == S - 1)
        def _():
            y = acc[...]
            y_ref[...] = y.astype(BF16)
            ho_ref[...] = h_ref[...] + 0.5 * mod_ref[0, r0 + 2:r0 + 3, :] * y

    row = pl.BlockSpec((tm, D), lambda i, s: (i, 0))
    wrow = pl.BlockSpec((None, F, D), lambda i, s: (s, 0, 0))
    ab = pl.BlockSpec((1, tm, F), lambda i, s: (s, i, 0))
    return _hosted(
        body, rider, name=name, grid=(T // tm, S),
        in_specs=[row, pl.BlockSpec((1, 9, D), lambda i, s: (grp(i), 0, 0)), pl.BlockSpec((1, D), lambda i, s: (0, 0)),
                  wrow, wrow, wrow],
        out_specs=[row, ab, ab, row, row],
        out_shape=[jax.ShapeDtypeStruct((T, D), F32), jax.ShapeDtypeStruct((S, T, F), BF16),
                   jax.ShapeDtypeStruct((S, T, F), BF16), jax.ShapeDtypeStruct((T, D), BF16),
                   jax.ShapeDtypeStruct((T, D), BF16)],
        scratch_shapes=[pltpu.VMEM((tm, D), BF16), pltpu.VMEM((tm, D), F32)],
        sem=("parallel", "arbitrary"), args=(h, mod, g, w1, w3, w2))


def _ffn_bwd(dh, h, mod, g, y, a, b, w1, w3, w2, k, dm, name, rider=None):
    T, D = h.shape
    S, F = w1.shape[0], w1.shape[-2]
    tm = dm.tm
    r0 = 6 if k else 0
    grp = _grp(dm)

    def body(dh_ref, h_ref, mod_ref, g_ref, y_ref, a_ref, b_ref, w1_ref, w3_ref, w2_ref,
             dho_ref, da_ref, db_ref, sw_ref, dy_ref, part_ref, dy_s, acc):
        s = pl.program_id(1)

        @pl.when(s == 0)
        def _():
            dy = (0.5 * mod_ref[0, r0 + 2:r0 + 3, :] * dh_ref[...]).astype(BF16)
            dy_s[...] = dy
            dy_ref[...] = dy
            acc[...] = jnp.zeros_like(acc)

        ds = _dot_nt(dy_s[...], w2_ref[...]).astype(BF16)
        av = a_ref[0]
        bv = b_ref[0]
        sig = jax.nn.sigmoid(av)
        sil = av * sig
        sw_ref[0] = sil * bv
        db = ds * sil
        da = ds * bv * (sig + sil * (1.0 - sig))
        da_ref[0] = da
        db_ref[0] = db
        acc[...] += _dot(da, w1_ref[...]) + _dot(db, w3_ref[...])

        @pl.when(s == S - 1)
        def _():
            dhv = dh_ref[...]
            dhb, dshift, dscale, dg = _pre_bwd(acc[...], h_ref[...], g_ref[...], mod_ref[0, r0 + 1:r0 + 2, :])
            dho_ref[...] = dhv + dhb
            dgate = 0.5 * jnp.sum(dhv * y_ref[...].astype(F32), axis=0, keepdims=True)
            _write_part(part_ref, dshift, dscale, dgate, dg)

    row = pl.BlockSpec((tm, D), lambda i, s: (i, 0))
    wrow = pl.BlockSpec((None, F, D), lambda i, s: (s, 0, 0))
    ab = pl.BlockSpec((1, tm, F), lambda i, s: (s, i, 0))
    stf = jax.ShapeDtypeStruct((S, T, F), BF16)
    return _hosted(
        body, rider, name=name, grid=(T // tm, S),
        in_specs=[row, row, pl.BlockSpec((1, 9, D), lambda i, s: (grp(i), 0, 0)), pl.BlockSpec((1, D), lambda i, s: (0, 0)),
                  row, ab, ab, wrow, wrow, wrow],
        out_specs=[row, ab, ab, ab, row, pl.BlockSpec((1, 8, D), lambda i, s: (i, 0, 0))],
        out_shape=[jax.ShapeDtypeStruct((T, D), F32), stf, stf, stf, jax.ShapeDtypeStruct((T, D), BF16),
                   jax.ShapeDtypeStruct((T // tm, 8, D), F32)],
        scratch_shapes=[pltpu.VMEM((tm, D), BF16), pltpu.VMEM((tm, D), F32)],
        sem=("parallel", "arbitrary"), args=(dh, h, mod, g, y, a, b, w1, w3, w2))


def _mm_tn(a, b, a_spec, b_spec, out_shape, out_spec, grid, name):
    nk = grid[-1]
    kax = len(grid) - 1
    blk = tuple(d for d in out_spec.block_shape if d is not None)

    def body(a_ref, b_ref, o_ref, acc):
        kk = pl.program_id(kax)

        @pl.when(kk == 0)
        def _():
            acc[...] = jnp.zeros_like(acc)

        av = a_ref[...].astype(BF16)
        if len(b_ref.shape) == 3:
            for p in range(b_ref.shape[0]):
                acc[p] += _dot_tn(av, b_ref[p].astype(BF16))
        else:
            acc[...] += _dot_tn(av, b_ref[...].astype(BF16))

        @pl.when(kk == nk - 1)
        def _():
            o_ref[...] = acc[...].astype(o_ref.dtype)

    return pl.pallas_call(
        body, name=name, grid=grid,
        in_specs=[a_spec, b_spec], out_specs=out_spec, out_shape=jax.ShapeDtypeStruct(out_shape, BF16),
        scratch_shapes=[pltpu.VMEM(blk, F32)],
        compiler_params=_cparams(*(["parallel"] * kax + ["arbitrary"])),
    )(a, b)


def _sc_w_in_specs(D, j):
    wq = D // N_CHIPS

    def spec(piece):
        col = lambda q: piece * N_CHIPS + q
        return pl.BlockSpec((None, None, D, wq), lambda i, q: (col(q) // 3, j, 0, col(q) % 3))

    return [spec(0), spec(1), spec(2)]


def _sc_in_fwd(h, mod, g, w_in, j, dm, name, rider=None):
    T, D = h.shape
    tm = dm.tm
    wq = D // N_CHIPS
    grp = _grp(dm)

    def body(h_ref, mod_ref, g_ref, wb_ref, wc_ref, wu_ref, p_ref, hn_ref, hn_s):
        @pl.when(pl.program_id(1) == 0)
        def _():
            hn = _pre(h_ref[...], g_ref[...], mod_ref[0, 3:4, :], mod_ref[0, 4:5, :]).astype(BF16)
            hn_s[...] = hn
            hn_ref[...] = hn

        for piece, w_ref in enumerate((wb_ref, wc_ref, wu_ref)):
            p_ref[piece] = _dot(hn_s[...], w_ref[...])

    row = pl.BlockSpec((tm, D), lambda i, q: (i, 0))
    return _hosted(
        body, rider, name=name, grid=(T // tm, N_CHIPS),
        in_specs=[row, pl.BlockSpec((1, 9, D), lambda i, q: (grp(i), 0, 0)), pl.BlockSpec((1, D), lambda i, q: (0, 0))]
        + _sc_w_in_specs(D, j),
        out_specs=[pl.BlockSpec((3, tm, wq), lambda i, q: (0, i, q)), row],
        out_shape=[jax.ShapeDtypeStruct((3, T, D), F32), jax.ShapeDtypeStruct((T, D), BF16)],
        scratch_shapes=[pltpu.VMEM((tm, D), BF16)],
        sem=("parallel", "arbitrary"), args=(h, mod, g, w_in, w_in, w_in))


def _conv_cols(dm):
    return 256 if dm.D % 256 == 0 else 128


def _seg_masks(r, dm):
    bn = dm.B * dm.N
    lat = r < bn
    off = jnp.where(lat, lax.rem(r, dm.N), lax.rem(r - bn, dm.CTX))
    seg = jnp.where(lat, dm.N, dm.CTX)
    inside = (r >= 0) & (r < dm.T)
    return ((off != 0) & inside).astype(F32), ((off != seg - 1) & inside).astype(F32)


def _conv_specs(dm):
    tb, cb, nr8 = dm.tm, _conv_cols(dm), dm.T // 8
    prev8 = lambda c, i: jnp.maximum(i * (tb // 8) - 1, 0)
    next8 = lambda c, i: jnp.minimum((i + 1) * (tb // 8), nr8 - 1)
    return dict(
        tb=tb, cb=cb,
        p=pl.BlockSpec((3, tb, cb), lambda c, i: (0, i, c)),
        p_prev=pl.BlockSpec((3, 8, cb), lambda c, i: (0, prev8(c, i), c)),
        p_next=pl.BlockSpec((3, 8, cb), lambda c, i: (0, next8(c, i), c)),
        row=pl.BlockSpec((tb, cb), lambda c, i: (i, c)),
        row_prev=pl.BlockSpec((8, cb), lambda c, i: (prev8(c, i), c)),
        row_next=pl.BlockSpec((8, cb), lambda c, i: (next8(c, i), c)),
        w=pl.BlockSpec((3, cb), lambda c, i: (0, c)),
    )


def _shift_rows(x, before, after, tb):
    rid = lax.broadcasted_iota(jnp.int32, x.shape, 0)
    down = jnp.where(rid == 0, before, pltpu.roll(x, 1, 0))
    up = jnp.where(rid == tb - 1, after, pltpu.roll(x, tb - 1, 0))
    return down, up


def _conv_fwd(p, wc, dm, name):
    T, D = dm.T, dm.D
    sp = _conv_specs(dm)
    tb, cb = sp["tb"], sp["cb"]

    def body(p_ref, pp_ref, pn_ref, w_ref, z_ref):
        r = pl.program_id(1) * tb + lax.broadcasted_iota(jnp.int32, (tb, cb), 0)
        mp, mn = _seg_masks(r, dm)
        cu = p_ref[1] * p_ref[2]
        prev, nxt = _shift_rows(cu, pp_ref[1, 7:8, :] * pp_ref[2, 7:8, :], pn_ref[1, 0:1, :] * pn_ref[2, 0:1, :], tb)
        conv = w_ref[0:1, :] * (prev * mp) + w_ref[1:2, :] * cu + w_ref[2:3, :] * (nxt * mn)
        z_ref[...] = (p_ref[0] * conv).astype(BF16)

    return pl.pallas_call(
        body, name=name, grid=(D // cb, T // tb),
        in_specs=[sp["p"], sp["p_prev"], sp["p_next"], sp["w"]], out_specs=sp["row"],
        out_shape=jax.ShapeDtypeStruct((T, D), BF16),
        compiler_params=_cparams("parallel", "parallel"),
    )(p, p, p, wc)


def _conv_bwd(dz, p, wc, dm, name):
    T, D = dm.T, dm.D
    sp = _conv_specs(dm)
    tb, cb = sp["tb"], sp["cb"]

    def body(dz_ref, dzp_ref, dzn_ref, p_ref, pp_ref, pn_ref, w_ref, dp_ref, dw_ref):
        i = pl.program_id(1)
        r = i * tb + lax.broadcasted_iota(jnp.int32, (tb, cb), 0)
        mp, mn = _seg_masks(r, dm)
        rb = i * tb + lax.broadcasted_iota(jnp.int32, (1, cb), 0)
        _, mn_before = _seg_masks(rb - 1, dm)
        mp_after, _ = _seg_masks(rb + tb, dm)
        bg, cg, u = p_ref[0], p_ref[1], p_ref[2]
        cu = cg * u
        prev, nxt = _shift_rows(cu, pp_ref[1, 7:8, :] * pp_ref[2, 7:8, :], pn_ref[1, 0:1, :] * pn_ref[2, 0:1, :], tb)
        prev = prev * mp
        nxt = nxt * mn
        w0, w1, w2 = w_ref[0:1, :], w_ref[1:2, :], w_ref[2:3, :]
        conv = w0 * prev + w1 * cu + w2 * nxt
        dz = dz_ref[...]
        dp_ref[0] = dz * conv
        dconv = dz * bg

        @pl.when(i == 0)
        def _():
            dw_ref[...] = jnp.zeros_like(dw_ref)

        dw_ref[0:1, :] += jnp.sum(dconv * prev, axis=0, keepdims=True)
        dw_ref[1:2, :] += jnp.sum(dconv * cu, axis=0, keepdims=True)
        dw_ref[2:3, :] += jnp.sum(dconv * nxt, axis=0, keepdims=True)
        dconv_before = dzp_ref[7:8, :] * pp_ref[0, 7:8, :] * mn_before
        dconv_after = dzn_ref[0:1, :] * pn_ref[0, 0:1, :] * mp_after
        from_prev, _ = _shift_rows(dconv * mn, dconv_before, dconv_after, tb)
        _, from_next = _shift_rows(dconv * mp, dconv_before, dconv_after, tb)
        dcu = w1 * dconv + w0 * from_next + w2 * from_prev
        dp_ref[1] = dcu * u
        dp_ref[2] = dcu * cg

    return pl.pallas_call(
        body, name=name, grid=(D // cb, T // tb),
        in_specs=[sp["row"], sp["row_prev"], sp["row_next"], sp["p"], sp["p_prev"], sp["p_next"], sp["w"]],
        out_specs=[sp["p"], sp["w"]],
        out_shape=[jax.ShapeDtypeStruct((3, T, D), F32), jax.ShapeDtypeStruct((3, D), F32)],
        compiler_params=_cparams("parallel", "arbitrary"),
    )(dz, dz, dz, p, p, p, wc)


def _out_fwd(z, w, h, mod, j, dm, name):
    T, D = h.shape
    K = z.shape[1]
    tm = dm.tm
    grp = _grp(dm)

    def body(z_ref, w_ref, h_ref, mod_ref, ho_ref, y_ref):
        y = _dot(z_ref[...], w_ref[...])
        y_ref[...] = y.astype(BF16)
        ho_ref[...] = h_ref[...] + mod_ref[0, 5:6, :] * y

    row = pl.BlockSpec((tm, D), lambda i: (i, 0))
    return pl.pallas_call(
        body, name=name, grid=(T // tm,),
        in_specs=[pl.BlockSpec((tm, K), lambda i: (i, 0)), pl.BlockSpec((None, K, D), lambda i: (j, 0, 0)), row,
                  pl.BlockSpec((1, 9, D), lambda i: (grp(i), 0, 0))],
        out_specs=[row, row],
        out_shape=[jax.ShapeDtypeStruct((T, D), F32), jax.ShapeDtypeStruct((T, D), BF16)],
        compiler_params=_cparams("parallel"),
    )(z, w, h, mod)


def _out_bwd(dh, y, w, mod, j, dm, name):
    T, D = dh.shape
    K = w.shape[1]
    tm = dm.tm
    grp = _grp(dm)

    def body(dh_ref, y_ref, w_ref, mod_ref, dy_ref, dz_ref, part_ref):
        dhv = dh_ref[...]
        dy = (mod_ref[0, 5:6, :] * dhv).astype(BF16)
        dy_ref[...] = dy
        dz_ref[...] = _dot_nt(dy, w_ref[...])
        _write_part(part_ref, dgate=jnp.sum(dhv * y_ref[...].astype(F32), axis=0, keepdims=True))

    row = pl.BlockSpec((tm, D), lambda i: (i, 0))
    return pl.pallas_call(
        body, name=name, grid=(T // tm,),
        in_specs=[row, row, pl.BlockSpec((None, K, D), lambda i: (j, 0, 0)), pl.BlockSpec((1, 9, D), lambda i: (grp(i), 0, 0))],
        out_specs=[row, pl.BlockSpec((tm, K), lambda i: (i, 0)), pl.BlockSpec((1, 8, D), lambda i: (i, 0, 0))],
        out_shape=[jax.ShapeDtypeStruct((T, D), BF16), jax.ShapeDtypeStruct((T, K), F32),
                   jax.ShapeDtypeStruct((T // tm, 8, D), F32)],
        compiler_params=_cparams("parallel"),
    )(dh, y, w, mod)


def _sc_in_bwd(dh, dp, h, mod, g, w_in, j, dm, name):
    T, D = h.shape
    tm = dm.tm
    wq = D // N_CHIPS
    nq = N_CHIPS
    grp = _grp(dm)

    def body(dh_ref, dp_ref, h_ref, mod_ref, g_ref, wb_ref, wc_ref, wu_ref, dho_ref, part_ref, acc):
        q = pl.program_id(1)

        @pl.when(q == 0)
        def _():
            acc[...] = jnp.zeros_like(acc)

        acc[...] += sum(_dot_nt(dp_ref[piece].astype(BF16), w_ref[...]) for piece, w_ref in enumerate((wb_ref, wc_ref, wu_ref)))

        @pl.when(q == nq - 1)
        def _():
            dhb, dshift, dscale, dg = _pre_bwd(acc[...], h_ref[...], g_ref[...], mod_ref[0, 4:5, :])
            dho_ref[...] = dh_ref[...] + dhb
            _write_part(part_ref, dshift, dscale, None, dg)

    row = pl.BlockSpec((tm, D), lambda i, q: (i, 0))
    return pl.pallas_call(
        body, name=name, grid=(T // tm, nq),
        in_specs=[row, pl.BlockSpec((3, tm, wq), lambda i, q: (0, i, q)), row,
                  pl.BlockSpec((1, 9, D), lambda i, q: (grp(i), 0, 0)), pl.BlockSpec((1, D), lambda i, q: (0, 0))]
        + _sc_w_in_specs(D, j),
        out_specs=[row, pl.BlockSpec((1, 8, D), lambda i, q: (i, 0, 0))],
        out_shape=[jax.ShapeDtypeStruct((T, D), F32), jax.ShapeDtypeStruct((T // tm, 8, D), F32)],
        scratch_shapes=[pltpu.VMEM((tm, D), F32)],
        compiler_params=_cparams("parallel", "arbitrary"),
    )(dh, dp, h, mod, g, w_in, w_in, w_in)


def _rope(t, c, s1, s2):
    return t * c + pltpu.roll(t, HEAD_PAD - 16, 1) * s1 + pltpu.roll(t, 16, 1) * s2


def _rope_t(dy, c, s1, s2):
    return dy * c + pltpu.roll(dy * s1, 16, 1) + pltpu.roll(dy * s2, HEAD_PAD - 16, 1)


def _mla_heads_fwd(z, g_ref, wuq_ref, wukv_ref):
    cq, ckv, krp = z[:, :Q_LORA], z[:, Q_LORA:Q_LORA + KV_LORA], z[:, Q_LORA + KV_LORA:]
    cqh, rq = _rms(cq, Q_LORA)
    ckvh, rkv = _rms(ckv, KV_LORA)
    cqn = (cqh * g_ref[0:1, :]).astype(BF16)
    ckvn = (ckvh * g_ref[1:2, :KV_LORA]).astype(BF16)
    qraw = _dot(cqn, wuq_ref[...])
    kvraw = _dot(ckvn, wukv_ref[...])
    return dict(krp=krp, cqh=cqh, rq=rq, ckvh=ckvh, rkv=rkv, cqn=cqn, ckvn=ckvn, qraw=qraw, kvraw=kvraw)


def _mla_proj_fwd(h, mod, g, gains, tabs, w_a, w_uq, w_ukv, j, dm, name):
    T, D = h.shape
    tm = min(dm.tm, 256)
    grp = lambda i: jnp.minimum(i // (dm.N // tm), dm.B)
    HP = HEAD_PAD

    def body(h_ref, mod_ref, g_ref, gn_ref, tab_ref, wa_ref, wuq_ref, wukv_ref, hn_ref, q_ref, k_ref, v_ref):
        hn = _pre(h_ref[...], g_ref[...], mod_ref[0, 3:4, :], mod_ref[0, 4:5, :]).astype(BF16)
        hn_ref[...] = hn
        f = _mla_heads_fwd(_dot(hn, wa_ref[...]), gn_ref, wuq_ref, wukv_ref)
        c, s1, s2 = tab_ref[0], tab_ref[1], tab_ref[2]
        for hd in range(HEADS):
            qh, _ = _rms(f["qraw"][:, hd * HP:(hd + 1) * HP], QK_HEAD)
            q_ref[:, hd * HP:(hd + 1) * HP] = (_rope(qh * gn_ref[2:3, :], c, s1, s2) * QK_SCALE).astype(BF16)
            kpre = jnp.concatenate([f["kvraw"][:, hd * HP:hd * HP + QK_NOPE], f["krp"]], axis=1)
            kh, _ = _rms(kpre, QK_HEAD)
            k_ref[:, hd * HP:(hd + 1) * HP] = _rope(kh * gn_ref[3:4, :], c, s1, s2).astype(BF16)
            v_ref[:, hd * V_HEAD:(hd + 1) * V_HEAD] = f["kvraw"][:, hd * HP + QK_NOPE:(hd + 1) * HP].astype(BF16)

    row = pl.BlockSpec((tm, D), lambda i: (i, 0))
    HQ = HEADS * HP
    return pl.pallas_call(
        body, name=name, grid=(T // tm,),
        in_specs=[row, pl.BlockSpec((1, 9, D), lambda i: (grp(i), 0, 0)), pl.BlockSpec((1, D), lambda i: (0, 0)),
                  pl.BlockSpec((None, 8, HP), lambda i: (j, 0, 0)), pl.BlockSpec((3, tm, HP), lambda i: (0, i, 0)),
                  pl.BlockSpec((None, D, 512), lambda i: (j, 0, 0)), pl.BlockSpec((None, Q_LORA, HQ), lambda i: (j, 0, 0)),
                  pl.BlockSpec((None, KV_LORA, HQ), lambda i: (j, 0, 0))],
        out_specs=[row, pl.BlockSpec((tm, HQ), lambda i: (i, 0)), pl.BlockSpec((tm, HQ), lambda i: (i, 0)),
                   pl.BlockSpec((tm, HEADS * V_HEAD), lambda i: (i, 0))],
        out_shape=[jax.ShapeDtypeStruct((T, D), BF16), jax.ShapeDtypeStruct((T, HQ), BF16),
                   jax.ShapeDtypeStruct((T, HQ), BF16), jax.ShapeDtypeStruct((T, HEADS * V_HEAD), BF16)],
        compiler_params=_cparams("parallel"),
    )(h, mod, g, gains, tabs, w_a, w_uq, w_ukv)


def _mla_proj_bwd(dh, dq, dkl, dkc, dvl, dvc, h, mod, g, gains, tabs, w_a, w_uq, w_ukv, j, dm, name):
    T, D = h.shape
    tm = min(dm.tm, 256)
    nblk = T // tm
    grp = lambda i: jnp.minimum(i // (dm.N // tm), dm.B)
    HP = HEAD_PAD
    HQ = HEADS * HP

    nlat = dm.B * dm.N // tm

    def body(dh_ref, dq_ref, dkl_ref, dkc_ref, dvl_ref, dvc_ref, h_ref, mod_ref, g_ref, gn_ref, tab_ref, wa_ref, wuq_ref, wukv_ref,
             dho_ref, part_ref, gwa_ref, gwuq_ref, gwukv_ref, dgn_ref, dqraw_s, dkvraw_s):
        i = pl.program_id(0)
        pick = lambda lat_ref, ctx_ref, cols: jnp.where(i < nlat, lat_ref[:, cols], ctx_ref[:, cols])

        @pl.when(i == 0)
        def _():
            gwa_ref[...] = jnp.zeros_like(gwa_ref)
            gwuq_ref[...] = jnp.zeros_like(gwuq_ref)
            gwukv_ref[...] = jnp.zeros_like(gwukv_ref)
            dgn_ref[...] = jnp.zeros_like(dgn_ref)

        hv = h_ref[...]
        hn = _pre(hv, g_ref[...], mod_ref[0, 3:4, :], mod_ref[0, 4:5, :]).astype(BF16)
        f = _mla_heads_fwd(_dot(hn, wa_ref[...]), gn_ref, wuq_ref, wukv_ref)
        c, s1, s2 = tab_ref[0], tab_ref[1], tab_ref[2]
        gq, gk = gn_ref[2:3, :], gn_ref[3:4, :]
        dgq = jnp.zeros((1, HP), F32)
        dgk = jnp.zeros((1, HP), F32)
        dkrp = jnp.zeros((tm, HP - QK_NOPE), F32)
        for hd in range(HEADS):
            qh, rq = _rms(f["qraw"][:, hd * HP:(hd + 1) * HP], QK_HEAD)
            dqn = _rope_t(dq_ref[:, hd * HP:(hd + 1) * HP] * QK_SCALE, c, s1, s2)
            dgq = dgq + jnp.sum(dqn * qh, axis=0, keepdims=True)
            dqraw_s[:, hd * HP:(hd + 1) * HP] = _rms_bwd(dqn * gq, qh, rq, QK_HEAD)
            kpre = jnp.concatenate([f["kvraw"][:, hd * HP:hd * HP + QK_NOPE], f["krp"]], axis=1)
            kh, rk = _rms(kpre, QK_HEAD)
            dkn = _rope_t(pick(dkl_ref, dkc_ref, slice(hd * HP, (hd + 1) * HP)), c, s1, s2)
            dgk = dgk + jnp.sum(dkn * kh, axis=0, keepdims=True)
            dkpre = _rms_bwd(dkn * gk, kh, rk, QK_HEAD)
            dkvraw_s[:, hd * HP:hd * HP + QK_NOPE] = dkpre[:, :QK_NOPE]
            dkrp = dkrp + dkpre[:, QK_NOPE:]
            dkvraw_s[:, hd * HP + QK_NOPE:(hd + 1) * HP] = pick(dvl_ref, dvc_ref, slice(hd * V_HEAD, (hd + 1) * V_HEAD))
        dqraw = dqraw_s[...].astype(BF16)
        dkvraw = dkvraw_s[...].astype(BF16)
        gwuq_ref[...] += _dot_tn(f["cqn"], dqraw)
        gwukv_ref[...] += _dot_tn(f["ckvn"], dkvraw)
        dcqn = _dot_nt(dqraw, wuq_ref[...])
        dckvn = _dot_nt(dkvraw, wukv_ref[...])
        dgqa = jnp.sum(dcqn * f["cqh"], axis=0, keepdims=True)
        dgkva = jnp.sum(dckvn * f["ckvh"], axis=0, keepdims=True)
        dcq = _rms_bwd(dcqn * gn_ref[0:1, :], f["cqh"], f["rq"], Q_LORA)
        dckv = _rms_bwd(dckvn * gn_ref[1:2, :KV_LORA], f["ckvh"], f["rkv"], KV_LORA)
        dz = jnp.concatenate([dcq, dckv, dkrp], axis=1).astype(BF16)
        gwa_ref[...] += _dot_tn(hn, dz)
        dhn = _dot_nt(dz, wa_ref[...])
        dhb, dshift, dscale, dg = _pre_bwd(dhn, hv, g_ref[...], mod_ref[0, 4:5, :])
        dho_ref[...] = dh_ref[...] + dhb
        _write_part(part_ref, dshift, dscale, None, dg)
        dgn_ref[0:1, :] += dgqa
        dgn_ref[1:2, :KV_LORA] += dgkva
        dgn_ref[2:3, :] += dgq
        dgn_ref[3:4, :] += dgk

    row = pl.BlockSpec((tm, D), lambda i: (i, 0))
    wide = pl.BlockSpec((tm, HQ), lambda i: (i, 0))
    const2 = lambda i: (0, 0)
    return pl.pallas_call(
        body, name=name, grid=(nblk,),
        in_specs=[row, wide, pl.BlockSpec((tm, HQ), lambda i: (jnp.minimum(i, nlat - 1), 0)),
                  pl.BlockSpec((tm, HQ), lambda i: (jnp.maximum(i - nlat, 0), 0)),
                  pl.BlockSpec((tm, HEADS * V_HEAD), lambda i: (jnp.minimum(i, nlat - 1), 0)),
                  pl.BlockSpec((tm, HEADS * V_HEAD), lambda i: (jnp.maximum(i - nlat, 0), 0)), row,
                  pl.BlockSpec((1, 9, D), lambda i: (grp(i), 0, 0)), pl.BlockSpec((1, D), const2),
                  pl.BlockSpec((None, 8, HP), lambda i: (j, 0, 0)), pl.BlockSpec((3, tm, HP), lambda i: (0, i, 0)),
                  pl.BlockSpec((None, D, 512), lambda i: (j, 0, 0)), pl.BlockSpec((None, Q_LORA, HQ), lambda i: (j, 0, 0)),
                  pl.BlockSpec((None, KV_LORA, HQ), lambda i: (j, 0, 0))],
        out_specs=[row, pl.BlockSpec((1, 8, D), lambda i: (i, 0, 0)), pl.BlockSpec((D, 512), const2),
                   pl.BlockSpec((Q_LORA, HQ), const2), pl.BlockSpec((KV_LORA, HQ), const2), pl.BlockSpec((8, HP), const2)],
        out_shape=[jax.ShapeDtypeStruct((T, D), F32), jax.ShapeDtypeStruct((nblk, 8, D), F32),
                   jax.ShapeDtypeStruct((D, 512), F32), jax.ShapeDtypeStruct((Q_LORA, HQ), F32),
                   jax.ShapeDtypeStruct((KV_LORA, HQ), F32), jax.ShapeDtypeStruct((8, HP), F32)],
        scratch_shapes=[pltpu.VMEM((tm, HQ), F32), pltpu.VMEM((tm, HQ), F32)],
        compiler_params=_cparams("arbitrary"),
    )(dh, dq, dkl, dkc, dvl, dvc, h, mod, g, gains, tabs, w_a, w_uq, w_ukv)


def _attn_specs(dm):
    tq = dm.CTX
    nq = dm.N // tq
    cblk0 = dm.B * nq
    HP = HEAD_PAD
    qrow = lambda b, i: jnp.where(i < nq, b * nq + i, cblk0 + b)
    return dict(
        tq=tq, nq=nq,
        q=pl.BlockSpec((tq, HP), lambda b, hd, i: (qrow(b, i), hd)),
        k_lat=pl.BlockSpec((dm.N, HP), lambda b, hd, i: (b, hd)),
        k_ctx=pl.BlockSpec((tq, HP), lambda b, hd, i: (cblk0 + b, hd)),
        v_lat=pl.BlockSpec((dm.N, V_HEAD), lambda b, hd, i: (b, hd)),
        v_ctx=pl.BlockSpec((tq, V_HEAD), lambda b, hd, i: (cblk0 + b, hd)),
        o=pl.BlockSpec((tq, V_HEAD), lambda b, hd, i: (qrow(b, i), hd)),
    )


def _attn_exp(q, keys, first_off=None):
    s = [_dot_nt(q, kk) for kk in keys]
    if first_off is not None:
        s[0] = s[0] + first_off
    m = functools.reduce(jnp.maximum, [jnp.max(x, axis=-1, keepdims=True) for x in s])
    e = [jnp.exp(x - m) for x in s]
    return e, 1.0 / sum(jnp.sum(x, axis=-1, keepdims=True) for x in e)


def _attn_fwd(q, k, v, dm, name, rider=None):
    T = dm.T
    sp = _attn_specs(dm)
    nq = sp["nq"]

    def body(q_ref, kl_ref, kc_ref, vl_ref, vc_ref, o_ref):
        i = pl.program_id(2)

        @pl.when(i < nq)
        def _():
            (el, ec), inv = _attn_exp(q_ref[...], [kl_ref[...], kc_ref[...]])
            o_ref[...] = ((_dot(el.astype(BF16), vl_ref[...]) + _dot(ec.astype(BF16), vc_ref[...])) * inv).astype(BF16)

        @pl.when(i == nq)
        def _():
            (ec,), inv = _attn_exp(q_ref[...], [kc_ref[...]])
            o_ref[...] = (_dot(ec.astype(BF16), vc_ref[...]) * inv).astype(BF16)

    (o,), got = _hosted(
        body, rider, name=name, grid=(dm.B, HEADS, nq + 1),
        in_specs=[sp["q"], sp["k_lat"], sp["k_ctx"], sp["v_lat"], sp["v_ctx"]], out_specs=[sp["o"]],
        out_shape=[jax.ShapeDtypeStruct((T, HEADS * V_HEAD), BF16)], scratch_shapes=[],
        sem=("parallel", "parallel", "arbitrary"), args=(q, k, k, v, v))
    return o, got


def _attn_bwd(q, k, v, o, do, dm, name):
    T = dm.T
    sp = _attn_specs(dm)
    nq, tq = sp["nq"], sp["tq"]
    HP, HQ, HV = HEAD_PAD, HEADS * HEAD_PAD, HEADS * V_HEAD

    def body(q_ref, kl_ref, kc_ref, vl_ref, vc_ref, o_ref, do_ref, dq_ref, dkl_ref, dkc_ref, dvl_ref, dvc_ref):
        i = pl.program_id(2)

        @pl.when(i == 0)
        def _():
            dkl_ref[...] = jnp.zeros_like(dkl_ref)
            dkc_ref[...] = jnp.zeros_like(dkc_ref)
            dvl_ref[...] = jnp.zeros_like(dvl_ref)
            dvc_ref[...] = jnp.zeros_like(dvc_ref)

        qv = q_ref[...]
        dov = do_ref[...]
        dob = dov.astype(BF16)
        delta = jnp.sum(dov * o_ref[...].astype(F32), axis=-1, keepdims=True)
        (el, ec), inv = _attn_exp(qv, [kl_ref[...], kc_ref[...]], jnp.where(i == nq, NEG, 0.0))
        pl_, pc = el * inv, ec * inv
        dsl = (pl_ * (_dot_nt(dob, vl_ref[...]) - delta)).astype(BF16)
        dsc = (pc * (_dot_nt(dob, vc_ref[...]) - delta)).astype(BF16)
        dq_ref[...] = _dot(dsl, kl_ref[...]) + _dot(dsc, kc_ref[...])
        dkl_ref[...] += _dot_tn(dsl, qv)
        dkc_ref[...] += _dot_tn(dsc, qv)
        dvl_ref[...] += _dot_tn(pl_.astype(BF16), dob)
        dvc_ref[...] += _dot_tn(pc.astype(BF16), dob)

    return pl.pallas_call(
        body, name=name, grid=(dm.B, HEADS, nq + 1),
        in_specs=[sp["q"], sp["k_lat"], sp["k_ctx"], sp["v_lat"], sp["v_ctx"], sp["o"], sp["o"]],
        out_specs=[sp["q"], sp["k_lat"], pl.BlockSpec((tq, HP), lambda b, hd, i: (b, hd)),
                   sp["v_lat"], pl.BlockSpec((tq, V_HEAD), lambda b, hd, i: (b, hd))],
        out_shape=[jax.ShapeDtypeStruct((T, HQ), F32), jax.ShapeDtypeStruct((dm.B * dm.N, HQ), F32),
                   jax.ShapeDtypeStruct((dm.B * dm.CTX, HQ), F32), jax.ShapeDtypeStruct((dm.B * dm.N, HV), F32),
                   jax.ShapeDtypeStruct((dm.B * dm.CTX, HV), F32)],
        compiler_params=_cparams("parallel", "parallel", "arbitrary"),
    )(q, k, k, v, v, o, do)


def _loss_grad(h, target, dm, name):
    T, D = h.shape
    tm = dm.tm
    nlat = dm.B * dm.N // tm

    def body(h_ref, t_ref, dh_ref, ls_ref):
        lat = (pl.program_id(0) < nlat).astype(F32)
        diff = (h_ref[...] - t_ref[...]) * lat
        dh_ref[...] = diff * (1.0 / D)
        ls_ref[...] = jnp.zeros(ls_ref.shape, F32) + (0.5 / D) * jnp.sum(diff * diff)

    return pl.pallas_call(
        body, name=name, grid=(T // tm,),
        in_specs=[pl.BlockSpec((tm, D), lambda i: (i, 0)), pl.BlockSpec((tm, D), lambda i: (jnp.minimum(i, nlat - 1), 0))],
        out_specs=[pl.BlockSpec((tm, D), lambda i: (i, 0)), pl.BlockSpec((1, 8, 128), lambda i: (i, 0, 0))],
        out_shape=[jax.ShapeDtypeStruct((T, D), F32), jax.ShapeDtypeStruct((T // tm, 8, 128), F32)],
        compiler_params=_cparams("parallel"),
    )(h, target)


def _col_block(cols, target=1152):
    return max(t for t in range(128, min(cols, target) + 1, 128) if cols % t == 0)


def _mod_fwd(cond, w_mod, b_mod, name):
    L, D, C = w_mod.shape
    R = cond.shape[0]
    cb = _col_block(C)

    def body(c_ref, w_ref, b_ref, o_ref):
        cv = c_ref[...]
        sc = (cv * jax.nn.sigmoid(cv)).astype(BF16)
        o_ref[...] = _dot(sc, w_ref[...].astype(BF16)) + b_ref[...]

    return pl.pallas_call(
        body, name=name, grid=(L, C // cb),
        in_specs=[pl.BlockSpec((R, D), lambda l, c: (0, 0)), pl.BlockSpec((None, D, cb), lambda l, c: (l, 0, c)),
                  pl.BlockSpec((None, 1, cb), lambda l, c: (l, 0, c))],
        out_specs=pl.BlockSpec((None, R, cb), lambda l, c: (l, 0, c)),
        out_shape=jax.ShapeDtypeStruct((L, R, C), F32),
        compiler_params=_cparams("parallel", "parallel"),
    )(cond, w_mod, b_mod)


def _mod_bwd(cond, dmod, w_mod, name):
    L, D, C = w_mod.shape
    R = cond.shape[0]
    cb = _col_block(C)
    nc = C // cb

    def body(c_ref, dm_ref, w_ref, gw_ref, ds_ref):
        cv = c_ref[...]
        sc = (cv * jax.nn.sigmoid(cv)).astype(BF16)
        dmv = dm_ref[...].astype(BF16)
        gw_ref[...] = _dot_tn(sc, dmv)
        part = _dot_nt(dmv, w_ref[...].astype(BF16))

        @pl.when(pl.program_id(1) == 0)
        def _():
            ds_ref[...] = part

        @pl.when(pl.program_id(1) > 0)
        def _():
            ds_ref[...] += part

    return pl.pallas_call(
        body, name=name, grid=(L, nc),
        in_specs=[pl.BlockSpec((R, D), lambda l, c: (0, 0)), pl.BlockSpec((None, R, cb), lambda l, c: (l, 0, c)),
                  pl.BlockSpec((None, D, cb), lambda l, c: (l, 0, c))],
        out_specs=[pl.BlockSpec((None, D, cb), lambda l, c: (l, 0, c)), pl.BlockSpec((None, R, D), lambda l, c: (l, 0, 0))],
        out_shape=[jax.ShapeDtypeStruct((L, D, C), F32), jax.ShapeDtypeStruct((L, R, D), F32)],
        compiler_params=_cparams("parallel", "arbitrary"),
    )(cond, dmod, w_mod)


def _row_block(rows, cols, budget=1 << 20):
    best = None
    for t in range(16, rows + 1, 16):
        if rows % t == 0 and t * cols * 4 <= budget:
            best = t
    return best if best is not None else rows


def _sum_slots(recvs, owns, chip, core, bufs, pieces, piece, name):
    n = len(recvs)
    S, R, C = recvs[0].shape
    tr = _row_block(R, C, budget=512 << 10)

    def body(ids_ref, *refs):
        for r_ref, p_ref, o_ref in zip(refs[:n], refs[n:2 * n], refs[-n:]):
            acc = None
            for s in range(S):
                v = jnp.where(ids_ref[0] == s, p_ref[s], r_ref[s]).astype(F32)
                acc = v if acc is None else acc + v
            o_ref[...] = acc

    blk = pl.BlockSpec((S, tr, C), lambda i, ids: (0, i, 0))
    held = [] if bufs is None else list(bufs)
    return pl.pallas_call(
        body, name=name,
        grid_spec=pltpu.PrefetchScalarGridSpec(
            num_scalar_prefetch=1, grid=(R // tr,), in_specs=[blk] * (2 * n) + [pl.BlockSpec(memory_space=pl.ANY)] * len(held),
            out_specs=[pl.BlockSpec((None, None, tr, C), lambda i, ids: (piece, ids[1], i, 0))] * n),
        out_shape=[jax.ShapeDtypeStruct((pieces, 2, R, C), F32)] * n,
        input_output_aliases={1 + 2 * n + i: i for i in range(len(held))}, compiler_params=_cparams("parallel"),
    )(jnp.stack([chip, core]).astype(jnp.int32), *recvs, *owns, *held)


def _adamw(w, gs, m, v, name):
    ng = len(gs)
    R, C = w.shape
    tr = _row_block(R, C)
    c1 = 1.0 / (1.0 - ADAM_B1 ** ADAM_STEP)
    c2 = 1.0 / (1.0 - ADAM_B2 ** ADAM_STEP)

    def body(w_ref, *refs):
        m_ref, v_ref, g_ref, d_ref, mo_ref, vo_ref = refs[ng:]
        g = refs[0][...]
        for g_more in refs[1:ng]:
            g = g + g_more[...]
        g_ref[...] = g
        mn = ADAM_B1 * m_ref[...] + (1.0 - ADAM_B1) * g
        vn = ADAM_B2 * v_ref[...] + (1.0 - ADAM_B2) * (g * g)
        mo_ref[...] = mn
        vo_ref[...] = vn
        d_ref[...] = -ADAM_LR * ((mn * c1) / (jnp.sqrt(vn * c2) + ADAM_EPS) + ADAM_WD * w_ref[...])

    blk = pl.BlockSpec((tr, C), lambda i: (i, 0))
    sd = jax.ShapeDtypeStruct((R, C), F32)
    return pl.pallas_call(
        body, name=name, grid=(R // tr,), in_specs=[blk] * (3 + ng), out_specs=[blk] * 4, out_shape=[sd] * 4,
        compiler_params=_cparams("parallel"),
    )(w, *gs, m, v)


def _rope_tables(dm):
    n = dm.N
    t = jnp.arange(n)
    r = (t // GRID_W).astype(F32)
    col = (t % GRID_W).astype(F32)
    nf = QK_ROPE // 4
    inv = ROPE_BASE ** (-jnp.arange(nf, dtype=F32) / nf)
    ang = jnp.stack([r[:, None] * inv, col[:, None] * inv], axis=1)
    cos, sin = jnp.cos(ang), jnp.sin(ang)
    zero = jnp.zeros_like(sin)
    c64 = jnp.stack([cos, cos], axis=2).reshape(n, QK_ROPE)
    s1 = jnp.stack([-sin, zero], axis=2).reshape(n, QK_ROPE)
    s2 = jnp.stack([zero, sin], axis=2).reshape(n, QK_ROPE)

    def pad(x, fill):
        return jnp.concatenate([jnp.full((n, QK_NOPE), fill, F32), x, jnp.full((n, HEAD_PAD - QK_HEAD), fill, F32)], axis=1)

    lat = jnp.stack([pad(c64, 1.0), pad(s1, 0.0), pad(s2, 0.0)])
    lat = jnp.tile(lat, (1, dm.B, 1))
    nctx = dm.B * dm.CTX
    ctx = jnp.stack([jnp.ones((nctx, HEAD_PAD), F32), jnp.zeros((nctx, HEAD_PAD), F32), jnp.zeros((nctx, HEAD_PAD), F32)])
    return jnp.concatenate([lat, ctx], axis=1)


def _fold_parts(part, dm):
    nblk = part.shape[0]
    nb = (dm.N * nblk) // dm.T
    groups = [part[b * nb:(b + 1) * nb].sum(axis=0) for b in range(dm.B)]
    groups.append(part[dm.B * nb:].sum(axis=0))
    return jnp.stack(groups)


def grouped(items, key):
    groups = {}
    for it in items:
        groups.setdefault(key(it), []).append(it)
    return list(groups.values())


def _flat2(a):
    return a.reshape(-1, a.shape[-1])


def kernel(x, c, ctx, c_ctx, w_mod, b_mod, g_norm, ffn_w1, ffn_w3, ffn_w2, sc_w_in, sc_conv, sc_w_out, mla_w_a, mla_g_qa, mla_w_uq, mla_g_kva, mla_w_ukv, mla_g_q, mla_g_k, mla_w_o, loss_target, m_c_ctx, m_w_mod, m_b_mod, m_g_norm, m_ffn_w1, m_ffn_w3, m_ffn_w2, m_sc_w_in, m_sc_conv, m_sc_w_out, m_mla_w_a, m_mla_g_qa, m_mla_w_uq, m_mla_g_kva, m_mla_w_ukv, m_mla_g_q, m_mla_g_k, m_mla_w_o, v_c_ctx, v_w_mod, v_b_mod, v_g_norm, v_ffn_w1, v_ffn_w3, v_ffn_w2, v_sc_w_in, v_sc_conv, v_sc_w_out, v_mla_w_a, v_mla_g_qa, v_mla_w_uq, v_mla_g_kva, v_mla_w_ukv, v_mla_g_q, v_mla_g_k, v_mla_w_o):
    B, N, D = x.shape
    CTX = ctx.shape[1]
    T = B * (N + CTX)
    tm = next(t for t in (512, 256, 128, 64, 32, 16) if N % t == 0 and (B * CTX) % t == 0)
    dm = Dims(B, N, CTX, D, T, tm)
    L = w_mod.shape[0]
    La, Lb = sc_w_in.shape[0], mla_w_a.shape[0]
    S = N_CHIPS
    ndev = 2 * S
    xi, yi, ci = lax.axis_index("x"), lax.axis_index("y"), lax.axis_index("c")
    chip = 2 * xi + yi
    dev = 2 * chip + ci
    weights = dict(c_ctx=c_ctx, w_mod=w_mod, b_mod=b_mod, g_norm=g_norm, ffn_w1=ffn_w1, ffn_w3=ffn_w3, ffn_w2=ffn_w2,
                   sc_w_in=sc_w_in, sc_conv=sc_conv, sc_w_out=sc_w_out, mla_w_a=mla_w_a, mla_g_qa=mla_g_qa,
                   mla_w_uq=mla_w_uq, mla_g_kva=mla_g_kva, mla_w_ukv=mla_w_ukv, mla_g_q=mla_g_q, mla_g_k=mla_g_k,
                   mla_w_o=mla_w_o)
    mom = dict(c_ctx=(m_c_ctx, v_c_ctx), w_mod=(m_w_mod, v_w_mod), b_mod=(m_b_mod, v_b_mod), g_norm=(m_g_norm, v_g_norm),
               ffn_w1=(m_ffn_w1, v_ffn_w1), ffn_w3=(m_ffn_w3, v_ffn_w3), ffn_w2=(m_ffn_w2, v_ffn_w2),
               sc_w_in=(m_sc_w_in, v_sc_w_in), sc_conv=(m_sc_conv, v_sc_conv), sc_w_out=(m_sc_w_out, v_sc_w_out),
               mla_w_a=(m_mla_w_a, v_mla_w_a), mla_g_qa=(m_mla_g_qa, v_mla_g_qa), mla_w_uq=(m_mla_w_uq, v_mla_w_uq),
               mla_g_kva=(m_mla_g_kva, v_mla_g_kva), mla_w_ukv=(m_mla_w_ukv, v_mla_w_ukv), mla_g_q=(m_mla_g_q, v_mla_g_q),
               mla_g_k=(m_mla_g_k, v_mla_g_k), mla_w_o=(m_mla_w_o, v_mla_w_o))

    big = ["ffn_w1", "ffn_w3", "ffn_w2", "sc_w_in", "sc_w_out", "mla_w_a", "mla_w_uq", "mla_w_ukv", "mla_w_o"]
    F = ffn_w1.shape[-1]
    transposed = ("ffn_w1", "ffn_w3")
    for n in transposed:
        weights[n] = jnp.swapaxes(weights[n], 2, 3)
        mom[n] = tuple(jnp.swapaxes(a, 2, 3) for a in mom[n])
    mixer_names =(["sc_w_in", "sc_w_out"], ["mla_w_a", "mla_w_uq", "mla_w_ukv", "mla_w_o"])

    def placed(names, piece, npieces):
        w2 = [_flat2(weights[n]) for n in names]
        rows = w2[0].shape[0] // npieces
        return dict(zip(names, _place_cast(w2, piece * rows, rows, chip, S, "place_weight")))

    bufs = {}
    for l in range(L):
        for k in range(2):
            bufs["f", l, k] = placed(["ffn_w1", "ffn_w3", "ffn_w2"], 2 * l + k, 2 * L)
        bufs["m", l] = {}
        for n in mixer_names[l % 2]:
            bufs["m", l].update(placed([n], l // 2, weights[n].shape[0]))
    order = [stage for l in range(L) for stage in (("f", l, 0), ("m", l), ("f", l, 1))]

    def gather_after(stage):
        at = order.index(stage)
        if at + 1 == len(order):
            return None, lambda got: None
        nxt = bufs[order[at + 1]]
        names = list(nxt)
        return _gather_rider([nxt[n] for n in names]), lambda got: nxt.update(zip(names, got))

    names = list(bufs[order[0]])
    bufs[order[0]].update(zip(names, _ride_alone(_gather_rider([bufs[order[0]][n] for n in names]), "gather_weights")))

    def ffn_weights(l, k):
        b = bufs["f", l, k]
        return b["ffn_w1"], b["ffn_w3"], b["ffn_w2"]

    def mixer_weights(l):
        b = bufs["m", l]
        w = {}
        if l % 2 == 0:
            w["w_in"] = b["sc_w_in"][:, None]
            w["w_out"] = b["sc_w_out"].reshape(1, D, D)
        else:
            w["w_a"] = jnp.pad(b["mla_w_a"].reshape(1, D, -1), ((0, 0), (0, 0), (0, 512 - (Q_LORA + KV_LORA + QK_ROPE))))
            wuq = jnp.moveaxis(b["mla_w_uq"], 0, 1).reshape(1, Q_LORA, HEADS, QK_HEAD)
            w["w_uq"] = jnp.pad(wuq, ((0, 0), (0, 0), (0, 0), (0, HEAD_PAD - QK_HEAD))).reshape(1, Q_LORA, HEADS * HEAD_PAD)
            w["w_ukv"] = jnp.moveaxis(b["mla_w_ukv"], 0, 1).reshape(1, KV_LORA, HEADS * HEAD_PAD)
            w["w_o"] = b["mla_w_o"].reshape(1, HEADS * V_HEAD, D)
        return w

    vecs = ["g_norm", "sc_conv", "mla_g_qa"]
    gathered = _gather_small([_flat2(weights[n]) for n in vecs], ("x", "y"), "gather_vectors")
    gw = {n: g.reshape((S,) + weights[n].shape) for n, g in zip(vecs, gathered)}
    gnorm = jnp.moveaxis(gw["g_norm"], 0, 2).reshape(L, 3, D)
    convw = jnp.moveaxis(gw["sc_conv"], 0, 2).reshape(La, 3, D)
    gqa = jnp.moveaxis(gw["mla_g_qa"], 0, 1).reshape(Lb, Q_LORA)
    padl = lambda a: jnp.pad(a, ((0, 0), (0, HEAD_PAD - a.shape[1])))
    gains = jnp.stack([padl(gqa), padl(mla_g_kva), padl(mla_g_q), padl(mla_g_k)], axis=1)
    gains = jnp.pad(gains, ((0, 0), (0, 4), (0, 0)))

    R = -(-(ndev * B + 1) // 16) * 16
    call = _gather_small([c], ("x", "y", "c"), "gather_cond")[0].reshape(ndev * B, D)
    cond = jnp.concatenate([call, c_ctx[None], jnp.zeros((R - ndev * B - 1, D), F32)], axis=0)
    C = w_mod.shape[-1]
    bm = lax.dynamic_slice_in_dim(b_mod, chip * C, C, axis=1)[:, None, :]
    mshard = _mod_fwd(cond, w_mod, bm, "mod_fwd")
    mfull = _gather_small([mshard.reshape(L * R, C)], ("x", "y"), "gather_mod")[0].reshape(S, L, R, C)
    mfull = jnp.moveaxis(mfull, 0, 2).reshape(L, R, S * C)
    mine = lax.dynamic_slice_in_dim(mfull, dev * B, B, axis=1)
    mod = jnp.concatenate([mine, mfull[:, ndev * B:ndev * B + 1]], axis=1).reshape(L, B + 1, 9, D)

    tabs = _rope_tables(dm)
    h = jnp.concatenate([x.reshape(B * N, D), ctx.reshape(B * CTX, D)], axis=0)

    saved = []
    lw = [None] * L
    for l in range(L):
        kind, j = l % 2, l // 2
        sv = {}
        sv["h0"] = h
        rider, keep = gather_after(("f", l, 0))
        (h, sv["a1"], sv["b1"], sv["hn1"], sv["y1"]), got = _ffn_fwd(h, mod[l], gnorm[l, 0:1], *ffn_weights(l, 0), 0, dm,
                                                                      "ffn_fwd", rider)
        keep(got)
        sv["h1"] = h
        W = lw[l] = mixer_weights(l)
        rider, keep = gather_after(("m", l))
        if kind == 0:
            (sv["p"], sv["hnm"]), got = _sc_in_fwd(h, mod[l], gnorm[l, 1:2], W["w_in"], 0, dm, "sc_in_fwd", rider)
            sv["z"] = _conv_fwd(sv["p"], convw[j], dm, "conv_fwd")
            h, sv["ym"] = _out_fwd(sv["z"], W["w_out"], h, mod[l], 0, dm, "sc_out_fwd")
        else:
            sv["hnm"], sv["q"], sv["k"], sv["v"] = _mla_proj_fwd(h, mod[l], gnorm[l, 1:2], gains[j:j + 1], tabs, W["w_a"], W["w_uq"],
                                                                 W["w_ukv"], 0, dm, "mla_proj_fwd")
            sv["o"], got = _attn_fwd(sv["q"], sv["k"], sv["v"], dm, "attn_fwd", rider)
            h, sv["ym"] = _out_fwd(sv["o"], W["w_o"], h, mod[l], 0, dm, "mla_out_fwd")
        keep(got)
        sv["h2"] = h
        rider, keep = gather_after(("f", l, 1))
        (h, sv["a2"], sv["b2"], sv["hn2"], sv["y2"]), got = _ffn_fwd(h, mod[l], gnorm[l, 2:3], *ffn_weights(l, 1), 1, dm,
                                                                      "ffn_fwd", rider)
        keep(got)
        saved.append(sv)

    dh, lsum = _loss_grad(h, loss_target.reshape(B * N, D), dm, "loss_grad")
    loss = lax.psum(jnp.sum(lsum[:, 0, 0]), ("x", "y", "c"))

    wq = D // S
    gsum = {n: None for n in big}
    npieces = {n: weights[n].shape[0] * (weights[n].shape[1] if n.startswith("ffn") else 1) for n in big}
    dmod = [None] * L
    dgn = [None] * L
    dconv = [None] * La
    dgains = [None] * Lb
    tk = tm * next(f for f in (3, 2, 1) if (T // tm) % f == 0)
    nk = T // tk
    full_a = pl.BlockSpec((tk, D), lambda s, kk: (kk, 0))
    shard_b = pl.BlockSpec((None, tk, F), lambda s, kk: (s, kk, 0))
    per_slot = lambda r_, c_: pl.BlockSpec((None, r_, c_), lambda s, kk: (s, 0, 0))

    def make_job(grads):
        parts = [g_ for _, _, g_ in grads]
        theirs = _swap_halves(parts, "swap_halves")
        pairs = [None] * len(grads)
        for idx in grouped(range(len(grads)), lambda i: parts[i].shape):
            outs = _pair_sum([parts[i] for i in idx], [theirs[i] for i in idx], ci, "pair_sum")
            for i, o in zip(idx, outs):
                pairs[i] = o
        return [(n, p, pair) for (n, p, _), pair in zip(grads, pairs)]

    def finish_job(job, recv):
        key = lambda i: (recv[i].shape, npieces[job[i][0]], job[i][1], gsum[job[i][0]] is None)
        for idx in grouped(range(len(job)), key):
            names_ = [job[i][0] for i in idx]
            held = None if gsum[names_[0]] is None else [gsum[n] for n in names_]
            outs = _sum_slots([recv[i] for i in idx], [job[i][2] for i in idx], chip, ci, held, npieces[names_[0]],
                              job[idx[0]][1], "sum_slots")
            gsum.update(zip(names_, outs))

    def ffn_back(dh, sv, l, k, job):
        sfx = "1" if k == 0 else "2"
        rider = _scatter_rider([pair for _, _, pair in job]) if job else None
        (dh, da, db, sw, dy, part), recv = _ffn_bwd(dh, sv["h0" if k == 0 else "h2"], mod[l], gnorm[l, 2 * k:2 * k + 1], sv["y" + sfx],
                                                    sv["a" + sfx], sv["b" + sfx], *ffn_weights(l, k), k, dm, "ffn_bwd", rider)
        finish_job(job, recv)
        g1 = _mm_tn(da, sv["hn" + sfx], shard_b, full_a, (S, F, D), per_slot(F, D), (S, nk), "gw1")
        g3 = _mm_tn(db, sv["hn" + sfx], shard_b, full_a, (S, F, D), per_slot(F, D), (S, nk), "gw3")
        g2 = _mm_tn(sw, dy, shard_b, full_a, (S, F, D), per_slot(F, D), (S, nk), "gw2")
        p = 2 * l + k
        return dh, _fold_parts(part, dm), [("ffn_w1", p, g1), ("ffn_w3", p, g3), ("ffn_w2", p, g2)]

    one = (1, nk)
    a1 = lambda kdim: pl.BlockSpec((tk, kdim), lambda s, kk: (kk, 0))
    pending = []
    for l in reversed(range(L)):
        kind, j = l % 2, l // 2
        sv = saved[l]
        W = lw[l]
        dh, p2, grads = ffn_back(dh, sv, l, 1, pending)
        job2 = make_job(grads)
        if kind == 0:
            dy, dz, pg = _out_bwd(dh, sv["ym"], W["w_out"], mod[l], 0, dm, "sc_out_bwd")
            g_out = _mm_tn(sv["z"], dy, a1(D), a1(D), (1, D, D), per_slot(D, D), one, "gw_sc_out")
            dp, dconv[j] = _conv_bwd(dz, sv["p"], convw[j], dm, "conv_bwd")
            g_in = _mm_tn(sv["hnm"], dp, pl.BlockSpec((tk, D), lambda q, kk: (kk, 0)),
                          pl.BlockSpec((3, tk, wq), lambda q, kk: (0, kk, q)), (3, S, D, wq),
                          pl.BlockSpec((3, None, D, wq), lambda q, kk: (0, q, 0, 0)), (S, nk), "gw_sc_in")
            dh, pm = _sc_in_bwd(dh, dp, sv["h1"], mod[l], gnorm[l, 1:2], W["w_in"], 0, dm, "sc_in_bwd")
            grads = [("sc_w_in", j, jnp.moveaxis(g_in.reshape(S, 3, D, wq), 1, 2).reshape(S, D, 3 * wq)),
                     ("sc_w_out", j, g_out.reshape(S, D // S, D))]
        else:
            dy, do, pg = _out_bwd(dh, sv["ym"], W["w_o"], mod[l], 0, dm, "mla_out_bwd")
            g_o = _mm_tn(sv["o"], dy, a1(HEADS * V_HEAD), a1(D), (1, HEADS * V_HEAD, D), per_slot(HEADS * V_HEAD, D), one, "gw_mla_o")
            dq, dkl, dkc, dvl, dvc = _attn_bwd(sv["q"], sv["k"], sv["v"], sv["o"], do, dm, "attn_bwd")
            dh, pm, g_a, g_uq, g_ukv, dgains[j] = _mla_proj_bwd(
                dh, dq, dkl, dkc, dvl, dvc, sv["h1"], mod[l], gnorm[l, 1:2], gains[j:j + 1], tabs, W["w_a"], W["w_uq"], W["w_ukv"], 0, dm, "mla_proj_bwd")
            g_uq = g_uq.reshape(Q_LORA, HEADS, HEAD_PAD)[..., :QK_HEAD].reshape(Q_LORA, S, -1)
            grads = [("mla_w_a", j, g_a[:, :Q_LORA + KV_LORA + QK_ROPE].reshape(S, D // S, -1).astype(BF16)),
                     ("mla_w_uq", j, jnp.moveaxis(g_uq, 1, 0).astype(BF16)),
                     ("mla_w_ukv", j, jnp.moveaxis(g_ukv.reshape(KV_LORA, S, -1), 1, 0).astype(BF16)),
                     ("mla_w_o", j, g_o.reshape(S, HEADS * V_HEAD // S, D))]
        jobm = make_job(grads)
        pm = _fold_parts(pm, dm) + _fold_parts(pg, dm)
        dh, p0, grads = ffn_back(dh, sv, l, 0, job2 + jobm)
        pending = make_job(grads)
        dmod[l] = jnp.concatenate([p0[:, 0:3], pm[:, 0:3], p2[:, 0:3]], axis=1).reshape(B + 1, 9 * D)
        dgn[l] = jnp.stack([p0[:, 3].sum(0), pm[:, 3].sum(0), p2[:, 3].sum(0)])
    grad_x = dh[:B * N].reshape(B, N, D)

    dgains_a = jnp.stack(dgains)
    small = [jnp.stack(dmod).reshape(-1), jnp.stack(dgn).reshape(-1), jnp.stack(dconv).reshape(-1), dgains_a.reshape(-1)]
    sizes = [s_.shape[0] for s_ in small]
    flat = jnp.concatenate(small)
    pad = (-flat.shape[0]) % 1024
    flat = jnp.pad(flat, (0, pad)).reshape(-1, 128)
    allsmall = _gather_small([flat], ("x", "y", "c"), "gather_small")[0].reshape(ndev, -1)
    offs = [0]
    for s_ in sizes:
        offs.append(offs[-1] + s_)
    dmod_all = allsmall[:, offs[0]:offs[1]].reshape(ndev, L, B + 1, 9 * D)
    tot = allsmall[:, offs[1]:offs[4]].sum(axis=0)
    g_gnorm = tot[:offs[2] - offs[1]].reshape(L, 3, D)
    g_conv = tot[offs[2] - offs[1]:offs[3] - offs[1]].reshape(La, 3, D)
    g_gains = tot[offs[3] - offs[1]:].reshape(Lb, 8, HEAD_PAD)
    dM = jnp.concatenate([jnp.moveaxis(dmod_all[:, :, :B], 0, 1).reshape(L, ndev * B, 9 * D),
                          dmod_all[:, :, B].sum(axis=0)[:, None, :], jnp.zeros((L, R - ndev * B - 1, 9 * D), F32)], axis=1)
    g_bmod = dM.sum(axis=1)
    dM_mine = lax.dynamic_slice_in_dim(dM, chip * C, C, axis=2)
    g_wmod, dsil = _mod_bwd(cond, dM_mine, w_mod, "mod_bwd")
    dsil_ctx = dsil[:, ndev * B].sum(axis=0)
    dsil_all = _gather_small([jnp.pad(dsil_ctx.reshape(-1, 128), ((0, (-(D // 128)) % 8), (0, 0)))], ("x", "y"), "gather_dctx")[0]
    dsil_tot = dsil_all.sum(axis=0)[:D // 128].reshape(D)
    sg = jax.nn.sigmoid(c_ctx)
    g_cctx = dsil_tot * (sg * (1.0 + c_ctx * (1.0 - sg)))

    chip_cols = lambda a, width: lax.dynamic_slice_in_dim(a, chip * width, width, axis=a.ndim - 1)
    small_grads = dict(
        c_ctx=g_cctx, b_mod=g_bmod, g_norm=chip_cols(g_gnorm, D // S), sc_conv=chip_cols(g_conv, D // S),
        mla_g_qa=chip_cols(g_gains[:, 0, :Q_LORA], Q_LORA // S), mla_g_kva=g_gains[:, 1, :KV_LORA],
        mla_g_q=g_gains[:, 2, :QK_HEAD], mla_g_k=g_gains[:, 3, :QK_HEAD])

    finish_job(pending, _ride_alone(_scatter_rider([pair for _, _, pair in pending]), "scatter_grads"))
    gsum = dict(zip(big, _swap_cores_inplace([gsum[n] for n in big], "swap_cores")))

    grads, deltas, new_m, new_v = {}, {}, {}, {}
    for n, w in weights.items():
        shape = w.shape
        w2 = _flat2(w) if w.ndim > 1 else w.reshape(1, -1)
        m2, v2 = (a.reshape(w2.shape) for a in mom[n])
        if n in gsum:
            gs = [gsum[n].reshape(w2.shape)]
        elif n == "w_mod":
            gs = [_flat2(g_wmod)]
        else:
            gs = [small_grads[n].reshape(w2.shape)]
        g_, d_, m_, v_ = _adamw(w2, gs, m2, v2, "adamw")
        grads[n], deltas[n], new_m[n], new_v[n] = (a.reshape(shape) for a in (g_, d_, m_, v_))
    for n in transposed:
        grads[n], deltas[n], new_m[n], new_v[n] = (jnp.swapaxes(a, 2, 3) for a in (grads[n], deltas[n], new_m[n], new_v[n]))

    names = list(weights)
    return (loss, grad_x, *[grads[n] for n in names], *[deltas[n] for n in names], *[new_m[n] for n in names],
            *[new_v[n] for n in names])
```

```python
import functools
import math
from typing import NamedTuple

import jax
import jax.numpy as jnp
from jax import lax
from jax.experimental import pallas as pl
from jax.experimental.pallas import tpu as pltpu

F32 = jnp.float32
BF16 = jnp.bfloat16
EPS = 1e-6
GRID_W = 64
HEADS = 8
QK_NOPE = 128
QK_ROPE = 64
QK_HEAD = QK_NOPE + QK_ROPE
HEAD_PAD = 256
V_HEAD = 128
Q_LORA = 256
KV_LORA = 128
ROPE_BASE = 10000.0
QK_SCALE = QK_HEAD ** -0.5
ADAM_LR, ADAM_B1, ADAM_B2, ADAM_EPS, ADAM_WD, ADAM_STEP = 0.001, 0.9, 0.999, 1e-08, 0.01, 10
N_CHIPS = 4
VMEM_LIMIT = 56 * 1024 * 1024
MESH = pl.DeviceIdType.MESH
NEG = -1e30


class Dims(NamedTuple):
    B: int
    N: int
    CTX: int
    D: int
    T: int
    tm: int


def _cparams(*sem):
    return pltpu.CompilerParams(dimension_semantics=sem if sem else None, vmem_limit_bytes=VMEM_LIMIT)


def _dot(a, b):
    return jnp.dot(a, b, preferred_element_type=F32)


def _dot_nt(a, b):
    return lax.dot_general(a, b, (((1,), (1,)), ((), ())), preferred_element_type=F32)


def _dot_tn(a, b):
    return lax.dot_general(a, b, (((0,), (0,)), ((), ())), preferred_element_type=F32)


def _rms(x, n):
    r = lax.rsqrt(jnp.sum(x * x, axis=-1, keepdims=True) * (1.0 / n) + EPS)
    return x * r, r


def _rms_bwd(dxh, xh, r, n):
    return r * (dxh - xh * (jnp.sum(dxh * xh, axis=-1, keepdims=True) * (1.0 / n)))


def _pre(h, g, shift, scale):
    xh, _ = _rms(h, h.shape[-1])
    return (xh * g) * (1.0 + scale) + shift


def _pre_bwd(dout, h, g, scale):
    d = h.shape[-1]
    xh, r = _rms(h, d)
    n = xh * g
    dshift = jnp.sum(dout, axis=0, keepdims=True)
    dscale = jnp.sum(dout * n, axis=0, keepdims=True)
    dn = dout * (1.0 + scale)
    dg = jnp.sum(dn * xh, axis=0, keepdims=True)
    dh = _rms_bwd(dn * g, xh, r, d)
    return dh, dshift, dscale, dg


def _write_part(part_ref, dshift=None, dscale=None, dgate=None, dg=None):
    z = jnp.zeros((1, part_ref.shape[-1]), F32)
    part_ref[0, 0:1, :] = z if dshift is None else dshift
    part_ref[0, 1:2, :] = z if dscale is None else dscale
    part_ref[0, 2:3, :] = z if dgate is None else dgate
    part_ref[0, 3:4, :] = z if dg is None else dg
    part_ref[0, 4:8, :] = jnp.zeros((4, part_ref.shape[-1]), F32)


def _grp(dm):
    nb = dm.N // dm.tm
    return lambda i: jnp.minimum(i // nb, dm.B)


def _n_chunks(rows, row_bytes):
    n = 16
    while n > 1 and (rows % (16 * n) or (rows // n) * row_bytes < (256 << 10)):
        n //= 2
    return n


def _start_local(src, dst, sems, k0, nchunk):
    ch = src.shape[0] // nchunk
    copies = []
    for j in range(nchunk):
        cp = pltpu.make_async_copy(src.at[pl.ds(j * ch, ch)], dst.at[pl.ds(j * ch, ch)], sems.at[k0 + j])
        cp.start()
        copies.append(cp)
    return copies


def _gather_small(arrs, axes, name):
    n = len(arrs)
    nbits = len(axes)
    slots = 2 ** nbits
    pats = list(range(1, slots))
    nck = [_n_chunks(a.shape[-2], a.shape[-1] * a.dtype.itemsize) for a in arrs]
    base = [sum(nck[:i]) * len(pats) for i in range(n)]
    nsem = sum(nck) * len(pats)

    def body(*refs):
        ins, outs = refs[:n], refs[n:2 * n]
        send, recv, loc = refs[2 * n:]
        pos = {a: lax.axis_index(a) for a in ("x", "y", "c")}

        def slot_of(p):
            s = 0
            for a in axes:
                s = 2 * s + p[a]
            return s

        me = slot_of(pos)
        local = []
        for i in range(n):
            local += _start_local(ins[i], outs[i].at[me], loc, sum(nck[:i]), nck[i])
        remote = []
        for pi, pat in enumerate(pats):
            peer = dict(pos)
            for bi, a in enumerate(axes):
                if (pat >> (nbits - 1 - bi)) & 1:
                    peer[a] = 1 - pos[a]
            them = slot_of(peer)
            for i in range(n):
                ch = arrs[i].shape[-2] // nck[i]
                for j in range(nck[i]):
                    k = base[i] + pi * nck[i] + j
                    rs = pl.ds(j * ch, ch)
                    cp = pltpu.make_async_remote_copy(
                        src_ref=ins[i].at[rs], dst_ref=outs[i].at[me, rs], send_sem=send.at[k], recv_sem=recv.at[k],
                        device_id=(peer["x"], peer["y"], peer["c"]), device_id_type=MESH)
                    cp.start()
                    remote.append(cp)
        for cp in local:
            cp.wait()
        for cp in remote:
            cp.wait()

    out_shape = [jax.ShapeDtypeStruct((slots,) + a.shape, a.dtype) for a in arrs]
    any_spec = pl.BlockSpec(memory_space=pl.ANY)
    outs = pl.pallas_call(
        body, name=name, out_shape=out_shape, in_specs=[any_spec] * n, out_specs=[any_spec] * n,
        scratch_shapes=[pltpu.SemaphoreType.DMA((nsem,)), pltpu.SemaphoreType.DMA((nsem,)), pltpu.SemaphoreType.DMA((sum(nck),))],
        compiler_params=pltpu.CompilerParams(has_side_effects=True),
    )(*arrs)
    return list(outs)


class Rider(NamedTuple):
    ins: list
    out_shapes: list
    aliases: dict
    sems: list
    start: object
    mid: object
    end: object


MID_STEPS = 6


def _hosted(body, rider, *, name, grid, in_specs, out_specs, out_shape, scratch_shapes, sem, args):
    if rider is None:
        outs = pl.pallas_call(body, name=name, grid=grid, in_specs=in_specs, out_specs=out_specs, out_shape=out_shape,
                              scratch_shapes=scratch_shapes, compiler_params=_cparams(*sem))(*args)
        return outs, []
    n_in, n_out, n_s = len(in_specs), len(out_specs), len(scratch_shapes)
    nri, nro = len(rider.ins), len(rider.out_shapes)
    nsteps = math.prod(grid)

    def wrapped(*refs):
        bounds = [0, n_in, n_in + nri, n_in + nri + n_out, n_in + nri + n_out + nro, n_in + nri + n_out + nro + n_s, len(refs)]
        ins, rins, outs, routs, scr, sems = (refs[lo:hi] for lo, hi in zip(bounds[:-1], bounds[1:]))
        step = 0
        for ax, extent in enumerate(grid):
            step = step * extent + pl.program_id(ax)

        @pl.when(step == 0)
        def _():
            rider.start(rins, routs, sems)

        body(*ins, *outs, *scr)

        if rider.mid is not None:
            @pl.when(step == max(nsteps - 1 - MID_STEPS, 0))
            def _():
                rider.mid(rins, routs, sems)

        @pl.when(step == nsteps - 1)
        def _():
            rider.end(rins, routs, sems)

    any_spec = pl.BlockSpec(memory_space=pl.ANY)
    outs = pl.pallas_call(
        wrapped, name=name, grid=grid, in_specs=list(in_specs) + [any_spec] * nri, out_specs=list(out_specs) + [any_spec] * nro,
        out_shape=list(out_shape) + list(rider.out_shapes), scratch_shapes=list(scratch_shapes) + list(rider.sems),
        input_output_aliases={n_in + i: n_out + o for i, o in rider.aliases.items()},
        compiler_params=pltpu.CompilerParams(dimension_semantics=("arbitrary",) * len(grid), vmem_limit_bytes=VMEM_LIMIT,
                                             has_side_effects=True),
    )(*args, *rider.ins)
    return outs[:n_out], list(outs[n_out:])


def _gather_rider(bufs):
    n = len(bufs)
    halves = [a.shape[1] // 2 for a in bufs]
    nck = [_n_chunks(h, a.shape[2] * a.dtype.itemsize) for h, a in zip(halves, bufs)]
    base = [3 * sum(nck[:i]) for i in range(n)]
    nsem = 3 * sum(nck)

    def plan():
        x, y, c = lax.axis_index("x"), lax.axis_index("y"), lax.axis_index("c")
        pieces = []
        for pi, (px, py) in enumerate([(x, 1 - y), (1 - x, y), (1 - x, 1 - y)]):
            for i in range(n):
                ch = halves[i] // nck[i]
                for j in range(nck[i]):
                    pieces.append((base[i] + pi * nck[i] + j, px, py, 2 * px + py, i, j * ch, ch))
        return x, y, c, 2 * x + y, pieces

    def rows(i, off, ch, core):
        return pl.ds(pl.multiple_of(core * halves[i] + off, 16), ch)

    def over_ici(outs, sems, c, slot, k, px, py, i, off, ch):
        ref = outs[i].at[slot, rows(i, off, ch, c)]
        return pltpu.make_async_remote_copy(src_ref=ref, dst_ref=ref, send_sem=sems[0].at[k], recv_sem=sems[1].at[k],
                                            device_id=(px, py, c), device_id_type=MESH)

    def over_d2d(outs, sems, x, y, c, slot, k, i, off, ch, core):
        ref = outs[i].at[slot, rows(i, off, ch, core)]
        return pltpu.make_async_remote_copy(src_ref=ref, dst_ref=ref, send_sem=sems[2].at[k], recv_sem=sems[3].at[k],
                                            device_id=(x, y, 1 - c), device_id_type=MESH)

    def start(ins, outs, sems):
        x, y, c, me, pieces = plan()
        for k, px, py, them, i, off, ch in pieces:
            over_ici(outs, sems, c, me, k, px, py, i, off, ch).start()

    def mid(ins, outs, sems):
        x, y, c, me, pieces = plan()
        for k, px, py, them, i, off, ch in pieces:
            over_ici(outs, sems, c, them, k, px, py, i, off, ch).wait_recv()
            over_d2d(outs, sems, x, y, c, them, k, i, off, ch, c).start()

    def end(ins, outs, sems):
        x, y, c, me, pieces = plan()
        for k, px, py, them, i, off, ch in pieces:
            over_ici(outs, sems, c, me, k, px, py, i, off, ch).wait_send()
            over_d2d(outs, sems, x, y, c, them, k, i, off, ch, c).wait_send()
        for k, px, py, them, i, off, ch in pieces:
            over_d2d(outs, sems, x, y, c, them, k, i, off, ch, 1 - c).wait_recv()

    return Rider(ins=list(bufs), out_shapes=[jax.ShapeDtypeStruct(a.shape, a.dtype) for a in bufs],
                 aliases={i: i for i in range(n)}, sems=[pltpu.SemaphoreType.DMA((nsem,))] * 4, start=start, mid=mid, end=end)


def _scatter_rider(srcs):
    n = len(srcs)
    nck = [_n_chunks(a.shape[1], a.shape[2] * a.dtype.itemsize) for a in srcs]
    base = [3 * sum(nck[:i]) for i in range(n)]
    nsem = 3 * sum(nck)

    def copies(ins, outs, sems):
        x, y, c = lax.axis_index("x"), lax.axis_index("y"), lax.axis_index("c")
        me = 2 * x + y
        for pi, (px, py) in enumerate([(x, 1 - y), (1 - x, y), (1 - x, 1 - y)]):
            for i in range(n):
                ch = srcs[i].shape[1] // nck[i]
                for j in range(nck[i]):
                    k = base[i] + pi * nck[i] + j
                    rs = pl.ds(j * ch, ch)
                    yield pltpu.make_async_remote_copy(
                        src_ref=ins[i].at[2 * px + py, rs], dst_ref=outs[i].at[me, rs], send_sem=sems[0].at[k],
                        recv_sem=sems[1].at[k], device_id=(px, py, c), device_id_type=MESH)

    def start(ins, outs, sems):
        for cp in copies(ins, outs, sems):
            cp.start()

    def end(ins, outs, sems):
        for cp in copies(ins, outs, sems):
            cp.wait()

    return Rider(ins=list(srcs), out_shapes=[jax.ShapeDtypeStruct(a.shape, a.dtype) for a in srcs], aliases={},
                 sems=[pltpu.SemaphoreType.DMA((nsem,))] * 2, start=start, mid=None, end=end)


def _ride_alone(rider, name):
    n_in, n_out = len(rider.ins), len(rider.out_shapes)

    def body(*refs):
        ins, outs, sems = refs[:n_in], refs[n_in:n_in + n_out], refs[n_in + n_out:]
        rider.start(ins, outs, sems)
        if rider.mid is not None:
            rider.mid(ins, outs, sems)
        rider.end(ins, outs, sems)

    any_spec = pl.BlockSpec(memory_space=pl.ANY)
    outs = pl.pallas_call(
        body, name=name, out_shape=list(rider.out_shapes), in_specs=[any_spec] * n_in, out_specs=[any_spec] * n_out,
        scratch_shapes=list(rider.sems), input_output_aliases=dict(rider.aliases),
        compiler_params=pltpu.CompilerParams(has_side_effects=True),
    )(*rider.ins)
    return list(outs)


def _swap_cores_inplace(bufs, name):
    n = len(bufs)
    nck = [_n_chunks(a.shape[2], a.shape[3] * a.dtype.itemsize) for a in bufs]
    base = [sum(a.shape[0] * k for a, k in zip(bufs[:i], nck[:i])) for i in range(n)]
    nsem = sum(a.shape[0] * k for a, k in zip(bufs, nck))

    def body(*refs):
        outs = refs[n:2 * n]
        send, recv = refs[2 * n:]
        x, y, c = lax.axis_index("x"), lax.axis_index("y"), lax.axis_index("c")

        def copies(core):
            for i in range(n):
                ch = bufs[i].shape[2] // nck[i]
                for p in range(bufs[i].shape[0]):
                    for j in range(nck[i]):
                        k = base[i] + p * nck[i] + j
                        ref = outs[i].at[p, core, pl.ds(j * ch, ch)]
                        yield pltpu.make_async_remote_copy(src_ref=ref, dst_ref=ref, send_sem=send.at[k], recv_sem=recv.at[k],
                                                           device_id=(x, y, 1 - c), device_id_type=MESH)

        for cp in copies(c):
            cp.start()
        for cp in copies(c):
            cp.wait_send()
        for cp in copies(1 - c):
            cp.wait_recv()

    any_spec = pl.BlockSpec(memory_space=pl.ANY)
    outs = pl.pallas_call(
        body, name=name, out_shape=[jax.ShapeDtypeStruct(a.shape, a.dtype) for a in bufs], in_specs=[any_spec] * n,
        out_specs=[any_spec] * n, scratch_shapes=[pltpu.SemaphoreType.DMA((nsem,))] * 2,
        input_output_aliases={i: i for i in range(n)}, compiler_params=pltpu.CompilerParams(has_side_effects=True),
    )(*bufs)
    return list(outs)


def _place_cast(ws, row0, rows, slot, slots, name):
    n = len(ws)
    C = ws[0].shape[1]
    tr = _row_block(rows, C)
    blk0 = row0 // tr

    def body(slot_ref, *refs):
        del slot_ref
        for w_ref, o_ref in zip(refs[:n], refs[n:]):
            o_ref[...] = w_ref[...].astype(BF16)

    return pl.pallas_call(
        body, name=name,
        grid_spec=pltpu.PrefetchScalarGridSpec(
            num_scalar_prefetch=1, grid=(rows // tr,), in_specs=[pl.BlockSpec((tr, C), lambda i, sr: (blk0 + i, 0))] * n,
            out_specs=[pl.BlockSpec((None, tr, C), lambda i, sr: (sr[0], i, 0))] * n),
        out_shape=[jax.ShapeDtypeStruct((slots, rows, C), BF16)] * n,
        compiler_params=_cparams("parallel"),
    )(slot.reshape(1).astype(jnp.int32), *ws)


def _swap_halves(arrs, name):
    n = len(arrs)
    S = arrs[0].shape[0]
    halves = [a.shape[1] // 2 for a in arrs]
    nck = [_n_chunks(h, a.shape[2] * a.dtype.itemsize) for h, a in zip(halves, arrs)]
    base = [S * sum(nck[:i]) for i in range(n)]
    nsem = S * sum(nck)

    def body(*refs):
        ins, outs = refs[:n], refs[n:2 * n]
        send, recv = refs[2 * n:]
        x, y, c = lax.axis_index("x"), lax.axis_index("y"), lax.axis_index("c")
        copies = []
        for i in range(n):
            ch = halves[i] // nck[i]
            for s in range(S):
                for j in range(nck[i]):
                    k = base[i] + s * nck[i] + j
                    src = ins[i].at[s, pl.ds(pl.multiple_of((1 - c) * halves[i] + j * ch, 16), ch)]
                    cp = pltpu.make_async_remote_copy(src_ref=src, dst_ref=outs[i].at[s, pl.ds(j * ch, ch)], send_sem=send.at[k],
                                                      recv_sem=recv.at[k], device_id=(x, y, 1 - c), device_id_type=MESH)
                    cp.start()
                    copies.append(cp)
        for cp in copies:
            cp.wait()

    any_spec = pl.BlockSpec(memory_space=pl.ANY)
    outs = pl.pallas_call(
        body, name=name, out_shape=[jax.ShapeDtypeStruct((S, h, a.shape[2]), a.dtype) for h, a in zip(halves, arrs)],
        in_specs=[any_spec] * n, out_specs=[any_spec] * n,
        scratch_shapes=[pltpu.SemaphoreType.DMA((nsem,))] * 2,
        compiler_params=pltpu.CompilerParams(has_side_effects=True),
    )(*arrs)
    return list(outs)


def _pair_sum(gs, rs, core, name):
    n = len(gs)
    S, rows, C = gs[0].shape
    half = rows // 2
    tr = _row_block(half, C)
    nb = half // tr

    def body(core_ref, *refs):
        del core_ref
        for g_ref, r_ref, o_ref in zip(refs[:n], refs[n:2 * n], refs[2 * n:]):
            o_ref[...] = (g_ref[...].astype(F32) + r_ref[...].astype(F32)).astype(BF16)

    blk = pl.BlockSpec((None, tr, C), lambda s, i, cr: (s, i, 0))
    mine = pl.BlockSpec((None, tr, C), lambda s, i, cr: (s, cr[0] * nb + i, 0))
    return pl.pallas_call(
        body, name=name,
        grid_spec=pltpu.PrefetchScalarGridSpec(num_scalar_prefetch=1, grid=(S, nb), in_specs=[mine] * n + [blk] * n,
                                               out_specs=[blk] * n),
        out_shape=[jax.ShapeDtypeStruct((S, half, C), BF16)] * n,
        compiler_params=_cparams("parallel", "parallel"),
    )(core.reshape(1).astype(jnp.int32), *gs, *rs)


FFN_FWD_PARTS = 1
FFN_BWD_PARTS = 2


def _row_parts(rows, parts):
    parts = parts if rows % (16 * parts) == 0 else 1
    return [pl.ds(p * (rows // parts), rows // parts) for p in range(parts)]


def _ffn_fwd(h, mod, g, w1, w3, w2, k, dm, name, rider=None):
    T, D = h.shape
    S, F = w1.shape[0], w1.shape[-2]
    tm = dm.tm
    r0 = 6 if k else 0
    grp = _grp(dm)

    def body(h_ref, mod_ref, g_ref, w1_ref, w3_ref, w2_ref, ho_ref, a_ref, b_ref, hn_ref, y_ref, hn_s, acc):
        s = pl.program_id(1)

        @pl.when(s == 0)
        def _():
            hn = _pre(h_ref[...], g_ref[...], mod_ref[0, r0:r0 + 1, :], mod_ref[0, r0 + 1:r0 + 2, :]).astype(BF16)
            hn_s[...] = hn
            hn_ref[...] = hn
            acc[...] = jnp.zeros_like(acc)

        for rows in _row_parts(tm, FFN_FWD_PARTS):
            hn = hn_s[rows, :]
            a = _dot_nt(hn, w1_ref[...])
            b = _dot_nt(hn, w3_ref[...])
            a_ref[0, rows, :] = a.astype(BF16)
            b_ref[0, rows, :] = b.astype(BF16)
            sw = (a * jax.nn.sigmoid(a) * b).astype(BF16)
            acc[rows, :] += _dot(sw, w2_ref[...])

        @pl.when(s == S - 1)
        def _():
            y = acc[...]
            y_ref[...] = y.astype(BF16)
            ho_ref[...] = h_ref[...] + 0.5 * mod_ref[0, r0 + 2:r0 + 3, :] * y

    row = pl.BlockSpec((tm, D), lambda i, s: (i, 0))
    wrow = pl.BlockSpec((None, F, D), lambda i, s: (s, 0, 0))
    ab = pl.BlockSpec((1, tm, F), lambda i, s: (s, i, 0))
    return _hosted(
        body, rider, name=name, grid=(T // tm, S),
        in_specs=[row, pl.BlockSpec((1, 9, D), lambda i, s: (grp(i), 0, 0)), pl.BlockSpec((1, D), lambda i, s: (0, 0)),
                  wrow, wrow, wrow],
        out_specs=[row, ab, ab, row, row],
        out_shape=[jax.ShapeDtypeStruct((T, D), F32), jax.ShapeDtypeStruct((S, T, F), BF16),
                   jax.ShapeDtypeStruct((S, T, F), BF16), jax.ShapeDtypeStruct((T, D), BF16),
                   jax.ShapeDtypeStruct((T, D), BF16)],
        scratch_shapes=[pltpu.VMEM((tm, D), BF16), pltpu.VMEM((tm, D), F32)],
        sem=("parallel", "arbitrary"), args=(h, mod, g, w1, w3, w2))


def _ffn_bwd(dh, h, mod, g, y, a, b, w1, w3, w2, k, dm, name, rider=None):
    T, D = h.shape
    S, F = w1.shape[0], w1.shape[-2]
    tm = dm.tm
    r0 = 6 if k else 0
    grp = _grp(dm)

    def body(dh_ref, h_ref, mod_ref, g_ref, y_ref, a_ref, b_ref, w1_ref, w3_ref, w2_ref,
             dho_ref, da_ref, db_ref, sw_ref, dy_ref, part_ref, dy_s, acc):
        s = pl.program_id(1)

        @pl.when(s == 0)
        def _():
            dy = (0.5 * mod_ref[0, r0 + 2:r0 + 3, :] * dh_ref[...]).astype(BF16)
            dy_s[...] = dy
            dy_ref[...] = dy
            acc[...] = jnp.zeros_like(acc)

        for rows in _row_parts(tm, FFN_BWD_PARTS):
            ds = _dot_nt(dy_s[rows, :], w2_ref[...]).astype(BF16)
            av = a_ref[0, rows, :]
            bv = b_ref[0, rows, :]
            sig = jax.nn.sigmoid(av)
            sil = av * sig
            sw_ref[0, rows, :] = sil * bv
            db = ds * sil
            da = ds * bv * (sig + sil * (1.0 - sig))
            da_ref[0, rows, :] = da
            db_ref[0, rows, :] = db
            acc[rows, :] += _dot(da, w1_ref[...]) + _dot(db, w3_ref[...])

        @pl.when(s == S - 1)
        def _():
            dhv = dh_ref[...]
            dhb, dshift, dscale, dg = _pre_bwd(acc[...], h_ref[...], g_ref[...], mod_ref[0, r0 + 1:r0 + 2, :])
            dho_ref[...] = dhv + dhb
            dgate = 0.5 * jnp.sum(dhv * y_ref[...].astype(F32), axis=0, keepdims=True)
            _write_part(part_ref, dshift, dscale, dgate, dg)

    row = pl.BlockSpec((tm, D), lambda i, s: (i, 0))
    wrow = pl.BlockSpec((None, F, D), lambda i, s: (s, 0, 0))
    ab = pl.BlockSpec((1, tm, F), lambda i, s: (s, i, 0))
    stf = jax.ShapeDtypeStruct((S, T, F), BF16)
    return _hosted(
        body, rider, name=name, grid=(T // tm, S),
        in_specs=[row, row, pl.BlockSpec((1, 9, D), lambda i, s: (grp(i), 0, 0)), pl.BlockSpec((1, D), lambda i, s: (0, 0)),
                  row, ab, ab, wrow, wrow, wrow],
        out_specs=[row, ab, ab, ab, row, pl.BlockSpec((1, 8, D), lambda i, s: (i, 0, 0))],
        out_shape=[jax.ShapeDtypeStruct((T, D), F32), stf, stf, stf, jax.ShapeDtypeStruct((T, D), BF16),
                   jax.ShapeDtypeStruct((T // tm, 8, D), F32)],
        scratch_shapes=[pltpu.VMEM((tm, D), BF16), pltpu.VMEM((tm, D), F32)],
        sem=("parallel", "arbitrary"), args=(dh, h, mod, g, y, a, b, w1, w3, w2))


def _mm_tn(a, b, a_spec, b_spec, out_shape, out_spec, grid, name):
    nk = grid[-1]
    kax = len(grid) - 1
    blk = tuple(d for d in out_spec.block_shape if d is not None)

    def body(a_ref, b_ref, o_ref, acc):
        kk = pl.program_id(kax)

        @pl.when(kk == 0)
        def _():
            acc[...] = jnp.zeros_like(acc)

        av = a_ref[...].astype(BF16)
        if len(b_ref.shape) == 3:
            for p in range(b_ref.shape[0]):
                acc[p] += _dot_tn(av, b_ref[p].astype(BF16))
        else:
            acc[...] += _dot_tn(av, b_ref[...].astype(BF16))

        @pl.when(kk == nk - 1)
        def _():
            o_ref[...] = acc[...].astype(o_ref.dtype)

    return pl.pallas_call(
        body, name=name, grid=grid,
        in_specs=[a_spec, b_spec], out_specs=out_spec, out_shape=jax.ShapeDtypeStruct(out_shape, BF16),
        scratch_shapes=[pltpu.VMEM(blk, F32)],
        compiler_params=_cparams(*(["parallel"] * kax + ["arbitrary"])),
    )(a, b)


def _sc_w_in_specs(D, j):
    wq = D // N_CHIPS

    def spec(piece):
        col = lambda q: piece * N_CHIPS + q
        return pl.BlockSpec((None, None, D, wq), lambda i, q: (col(q) // 3, j, 0, col(q) % 3))

    return [spec(0), spec(1), spec(2)]


def _sc_in_fwd(h, mod, g, w_in, j, dm, name, rider=None):
    T, D = h.shape
    tm = dm.tm
    wq = D // N_CHIPS
    grp = _grp(dm)

    def body(h_ref, mod_ref, g_ref, wb_ref, wc_ref, wu_ref, p_ref, hn_ref, hn_s):
        @pl.when(pl.program_id(1) == 0)
        def _():
            hn = _pre(h_ref[...], g_ref[...], mod_ref[0, 3:4, :], mod_ref[0, 4:5, :]).astype(BF16)
            hn_s[...] = hn
            hn_ref[...] = hn

        for piece, w_ref in enumerate((wb_ref, wc_ref, wu_ref)):
            p_ref[piece] = _dot(hn_s[...], w_ref[...])

    row = pl.BlockSpec((tm, D), lambda i, q: (i, 0))
    return _hosted(
        body, rider, name=name, grid=(T // tm, N_CHIPS),
        in_specs=[row, pl.BlockSpec((1, 9, D), lambda i, q: (grp(i), 0, 0)), pl.BlockSpec((1, D), lambda i, q: (0, 0))]
        + _sc_w_in_specs(D, j),
        out_specs=[pl.BlockSpec((3, tm, wq), lambda i, q: (0, i, q)), row],
        out_shape=[jax.ShapeDtypeStruct((3, T, D), F32), jax.ShapeDtypeStruct((T, D), BF16)],
        scratch_shapes=[pltpu.VMEM((tm, D), BF16)],
        sem=("parallel", "arbitrary"), args=(h, mod, g, w_in, w_in, w_in))


def _conv_cols(dm):
    return 256 if dm.D % 256 == 0 else 128


def _seg_masks(r, dm):
    bn = dm.B * dm.N
    lat = r < bn
    off = jnp.where(lat, lax.rem(r, dm.N), lax.rem(r - bn, dm.CTX))
    seg = jnp.where(lat, dm.N, dm.CTX)
    inside = (r >= 0) & (r < dm.T)
    return ((off != 0) & inside).astype(F32), ((off != seg - 1) & inside).astype(F32)


def _conv_specs(dm):
    tb, cb, nr8 = dm.tm, _conv_cols(dm), dm.T // 8
    prev8 = lambda c, i: jnp.maximum(i * (tb // 8) - 1, 0)
    next8 = lambda c, i: jnp.minimum((i + 1) * (tb // 8), nr8 - 1)
    return dict(
        tb=tb, cb=cb,
        p=pl.BlockSpec((3, tb, cb), lambda c, i: (0, i, c)),
        p_prev=pl.BlockSpec((3, 8, cb), lambda c, i: (0, prev8(c, i), c)),
        p_next=pl.BlockSpec((3, 8, cb), lambda c, i: (0, next8(c, i), c)),
        row=pl.BlockSpec((tb, cb), lambda c, i: (i, c)),
        row_prev=pl.BlockSpec((8, cb), lambda c, i: (prev8(c, i), c)),
        row_next=pl.BlockSpec((8, cb), lambda c, i: (next8(c, i), c)),
        w=pl.BlockSpec((3, cb), lambda c, i: (0, c)),
    )


def _shift_rows(x, before, after, tb):
    rid = lax.broadcasted_iota(jnp.int32, x.shape, 0)
    down = jnp.where(rid == 0, before, pltpu.roll(x, 1, 0))
    up = jnp.where(rid == tb - 1, after, pltpu.roll(x, tb - 1, 0))
    return down, up


def _conv_fwd(p, wc, dm, name):
    T, D = dm.T, dm.D
    sp = _conv_specs(dm)
    tb, cb = sp["tb"], sp["cb"]

    def body(p_ref, pp_ref, pn_ref, w_ref, z_ref):
        r = pl.program_id(1) * tb + lax.broadcasted_iota(jnp.int32, (tb, cb), 0)
        mp, mn = _seg_masks(r, dm)
        cu = p_ref[1] * p_ref[2]
        prev, nxt = _shift_rows(cu, pp_ref[1, 7:8, :] * pp_ref[2, 7:8, :], pn_ref[1, 0:1, :] * pn_ref[2, 0:1, :], tb)
        conv = w_ref[0:1, :] * (prev * mp) + w_ref[1:2, :] * cu + w_ref[2:3, :] * (nxt * mn)
        z_ref[...] = (p_ref[0] * conv).astype(BF16)

    return pl.pallas_call(
        body, name=name, grid=(D // cb, T // tb),
        in_specs=[sp["p"], sp["p_prev"], sp["p_next"], sp["w"]], out_specs=sp["row"],
        out_shape=jax.ShapeDtypeStruct((T, D), BF16),
        compiler_params=_cparams("parallel", "parallel"),
    )(p, p, p, wc)


def _conv_bwd(dz, p, wc, dm, name):
    T, D = dm.T, dm.D
    sp = _conv_specs(dm)
    tb, cb = sp["tb"], sp["cb"]

    def body(dz_ref, dzp_ref, dzn_ref, p_ref, pp_ref, pn_ref, w_ref, dp_ref, dw_ref):
        i = pl.program_id(1)
        r = i * tb + lax.broadcasted_iota(jnp.int32, (tb, cb), 0)
        mp, mn = _seg_masks(r, dm)
        rb = i * tb + lax.broadcasted_iota(jnp.int32, (1, cb), 0)
        _, mn_before = _seg_masks(rb - 1, dm)
        mp_after, _ = _seg_masks(rb + tb, dm)
        bg, cg, u = p_ref[0], p_ref[1], p_ref[2]
        cu = cg * u
        prev, nxt = _shift_rows(cu, pp_ref[1, 7:8, :] * pp_ref[2, 7:8, :], pn_ref[1, 0:1, :] * pn_ref[2, 0:1, :], tb)
        prev = prev * mp
        nxt = nxt * mn
        w0, w1, w2 = w_ref[0:1, :], w_ref[1:2, :], w_ref[2:3, :]
        conv = w0 * prev + w1 * cu + w2 * nxt
        dz = dz_ref[...]
        dp_ref[0] = dz * conv
        dconv = dz * bg

        @pl.when(i == 0)
        def _():
            dw_ref[...] = jnp.zeros_like(dw_ref)

        dw_ref[0:1, :] += jnp.sum(dconv * prev, axis=0, keepdims=True)
        dw_ref[1:2, :] += jnp.sum(dconv * cu, axis=0, keepdims=True)
        dw_ref[2:3, :] += jnp.sum(dconv * nxt, axis=0, keepdims=True)
        dconv_before = dzp_ref[7:8, :] * pp_ref[0, 7:8, :] * mn_before
        dconv_after = dzn_ref[0:1, :] * pn_ref[0, 0:1, :] * mp_after
        from_prev, _ = _shift_rows(dconv * mn, dconv_before, dconv_after, tb)
        _, from_next = _shift_rows(dconv * mp, dconv_before, dconv_after, tb)
        dcu = w1 * dconv + w0 * from_next + w2 * from_prev
        dp_ref[1] = dcu * u
        dp_ref[2] = dcu * cg

    return pl.pallas_call(
        body, name=name, grid=(D // cb, T // tb),
        in_specs=[sp["row"], sp["row_prev"], sp["row_next"], sp["p"], sp["p_prev"], sp["p_next"], sp["w"]],
        out_specs=[sp["p"], sp["w"]],
        out_shape=[jax.ShapeDtypeStruct((3, T, D), F32), jax.ShapeDtypeStruct((3, D), F32)],
        compiler_params=_cparams("parallel", "arbitrary"),
    )(dz, dz, dz, p, p, p, wc)


def _out_fwd(z, w, h, mod, j, dm, name):
    T, D = h.shape
    K = z.shape[1]
    tm = dm.tm
    grp = _grp(dm)

    def body(z_ref, w_ref, h_ref, mod_ref, ho_ref, y_ref):
        y = _dot(z_ref[...], w_ref[...])
        y_ref[...] = y.astype(BF16)
        ho_ref[...] = h_ref[...] + mod_ref[0, 5:6, :] * y

    row = pl.BlockSpec((tm, D), lambda i: (i, 0))
    return pl.pallas_call(
        body, name=name, grid=(T // tm,),
        in_specs=[pl.BlockSpec((tm, K), lambda i: (i, 0)), pl.BlockSpec((None, K, D), lambda i: (j, 0, 0)), row,
                  pl.BlockSpec((1, 9, D), lambda i: (grp(i), 0, 0))],
        out_specs=[row, row],
        out_shape=[jax.ShapeDtypeStruct((T, D), F32), jax.ShapeDtypeStruct((T, D), BF16)],
        compiler_params=_cparams("parallel"),
    )(z, w, h, mod)


def _out_bwd(dh, y, w, mod, j, dm, name):
    T, D = dh.shape
    K = w.shape[1]
    tm = dm.tm
    grp = _grp(dm)

    def body(dh_ref, y_ref, w_ref, mod_ref, dy_ref, dz_ref, part_ref):
        dhv = dh_ref[...]
        dy = (mod_ref[0, 5:6, :] * dhv).astype(BF16)
        dy_ref[...] = dy
        dz_ref[...] = _dot_nt(dy, w_ref[...])
        _write_part(part_ref, dgate=jnp.sum(dhv * y_ref[...].astype(F32), axis=0, keepdims=True))

    row = pl.BlockSpec((tm, D), lambda i: (i, 0))
    return pl.pallas_call(
        body, name=name, grid=(T // tm,),
        in_specs=[row, row, pl.BlockSpec((None, K, D), lambda i: (j, 0, 0)), pl.BlockSpec((1, 9, D), lambda i: (grp(i), 0, 0))],
        out_specs=[row, pl.BlockSpec((tm, K), lambda i: (i, 0)), pl.BlockSpec((1, 8, D), lambda i: (i, 0, 0))],
        out_shape=[jax.ShapeDtypeStruct((T, D), BF16), jax.ShapeDtypeStruct((T, K), F32),
                   jax.ShapeDtypeStruct((T // tm, 8, D), F32)],
        compiler_params=_cparams("parallel"),
    )(dh, y, w, mod)


def _sc_in_bwd(dh, dp, h, mod, g, w_in, j, dm, name):
    T, D = h.shape
    tm = dm.tm
    wq = D // N_CHIPS
    nq = N_CHIPS
    grp = _grp(dm)

    def body(dh_ref, dp_ref, h_ref, mod_ref, g_ref, wb_ref, wc_ref, wu_ref, dho_ref, part_ref, acc):
        q = pl.program_id(1)

        @pl.when(q == 0)
        def _():
            acc[...] = jnp.zeros_like(acc)

        acc[...] += sum(_dot_nt(dp_ref[piece].astype(BF16), w_ref[...]) for piece, w_ref in enumerate((wb_ref, wc_ref, wu_ref)))

        @pl.when(q == nq - 1)
        def _():
            dhb, dshift, dscale, dg = _pre_bwd(acc[...], h_ref[...], g_ref[...], mod_ref[0, 4:5, :])
            dho_ref[...] = dh_ref[...] + dhb
            _write_part(part_ref, dshift, dscale, None, dg)

    row = pl.BlockSpec((tm, D), lambda i, q: (i, 0))
    return pl.pallas_call(
        body, name=name, grid=(T // tm, nq),
        in_specs=[row, pl.BlockSpec((3, tm, wq), lambda i, q: (0, i, q)), row,
                  pl.BlockSpec((1, 9, D), lambda i, q: (grp(i), 0, 0)), pl.BlockSpec((1, D), lambda i, q: (0, 0))]
        + _sc_w_in_specs(D, j),
        out_specs=[row, pl.BlockSpec((1, 8, D), lambda i, q: (i, 0, 0))],
        out_shape=[jax.ShapeDtypeStruct((T, D), F32), jax.ShapeDtypeStruct((T // tm, 8, D), F32)],
        scratch_shapes=[pltpu.VMEM((tm, D), F32)],
        compiler_params=_cparams("parallel", "arbitrary"),
    )(dh, dp, h, mod, g, w_in, w_in, w_in)


def _rope(t, c, s1, s2):
    return t * c + pltpu.roll(t, HEAD_PAD - 16, 1) * s1 + pltpu.roll(t, 16, 1) * s2


def _rope_t(dy, c, s1, s2):
    return dy * c + pltpu.roll(dy * s1, 16, 1) + pltpu.roll(dy * s2, HEAD_PAD - 16, 1)


def _mla_heads_fwd(z, g_ref, wuq_ref, wukv_ref):
    cq, ckv, krp = z[:, :Q_LORA], z[:, Q_LORA:Q_LORA + KV_LORA], z[:, Q_LORA + KV_LORA:]
    cqh, rq = _rms(cq, Q_LORA)
    ckvh, rkv = _rms(ckv, KV_LORA)
    cqn = (cqh * g_ref[0:1, :]).astype(BF16)
    ckvn = (ckvh * g_ref[1:2, :KV_LORA]).astype(BF16)
    qraw = _dot(cqn, wuq_ref[...])
    kvraw = _dot(ckvn, wukv_ref[...])
    return dict(krp=krp, cqh=cqh, rq=rq, ckvh=ckvh, rkv=rkv, cqn=cqn, ckvn=ckvn, qraw=qraw, kvraw=kvraw)


def _mla_proj_fwd(h, mod, g, gains, tabs, w_a, w_uq, w_ukv, j, dm, name):
    T, D = h.shape
    tm = min(dm.tm, 256)
    grp = lambda i: jnp.minimum(i // (dm.N // tm), dm.B)
    HP = HEAD_PAD

    def body(h_ref, mod_ref, g_ref, gn_ref, tab_ref, wa_ref, wuq_ref, wukv_ref, hn_ref, q_ref, k_ref, v_ref):
        hn = _pre(h_ref[...], g_ref[...], mod_ref[0, 3:4, :], mod_ref[0, 4:5, :]).astype(BF16)
        hn_ref[...] = hn
        f = _mla_heads_fwd(_dot(hn, wa_ref[...]), gn_ref, wuq_ref, wukv_ref)
        c, s1, s2 = tab_ref[0], tab_ref[1], tab_ref[2]
        for hd in range(HEADS):
            qh, _ = _rms(f["qraw"][:, hd * HP:(hd + 1) * HP], QK_HEAD)
            q_ref[:, hd * HP:(hd + 1) * HP] = (_rope(qh * gn_ref[2:3, :], c, s1, s2) * QK_SCALE).astype(BF16)
            kpre = jnp.concatenate([f["kvraw"][:, hd * HP:hd * HP + QK_NOPE], f["krp"]], axis=1)
            kh, _ = _rms(kpre, QK_HEAD)
            k_ref[:, hd * HP:(hd + 1) * HP] = _rope(kh * gn_ref[3:4, :], c, s1, s2).astype(BF16)
            v_ref[:, hd * V_HEAD:(hd + 1) * V_HEAD] = f["kvraw"][:, hd * HP + QK_NOPE:(hd + 1) * HP].astype(BF16)

    row = pl.BlockSpec((tm, D), lambda i: (i, 0))
    HQ = HEADS * HP
    return pl.pallas_call(
        body, name=name, grid=(T // tm,),
        in_specs=[row, pl.BlockSpec((1, 9, D), lambda i: (grp(i), 0, 0)), pl.BlockSpec((1, D), lambda i: (0, 0)),
                  pl.BlockSpec((None, 8, HP), lambda i: (j, 0, 0)), pl.BlockSpec((3, tm, HP), lambda i: (0, i, 0)),
                  pl.BlockSpec((None, D, 512), lambda i: (j, 0, 0)), pl.BlockSpec((None, Q_LORA, HQ), lambda i: (j, 0, 0)),
                  pl.BlockSpec((None, KV_LORA, HQ), lambda i: (j, 0, 0))],
        out_specs=[row, pl.BlockSpec((tm, HQ), lambda i: (i, 0)), pl.BlockSpec((tm, HQ), lambda i: (i, 0)),
                   pl.BlockSpec((tm, HEADS * V_HEAD), lambda i: (i, 0))],
        out_shape=[jax.ShapeDtypeStruct((T, D), BF16), jax.ShapeDtypeStruct((T, HQ), BF16),
                   jax.ShapeDtypeStruct((T, HQ), BF16), jax.ShapeDtypeStruct((T, HEADS * V_HEAD), BF16)],
        compiler_params=_cparams("parallel"),
    )(h, mod, g, gains, tabs, w_a, w_uq, w_ukv)


def _mla_proj_bwd(dh, dq, dkl, dkc, dvl, dvc, h, mod, g, gains, tabs, w_a, w_uq, w_ukv, j, dm, name):
    T, D = h.shape
    tm = min(dm.tm, 256)
    nblk = T // tm
    grp = lambda i: jnp.minimum(i // (dm.N // tm), dm.B)
    HP = HEAD_PAD
    HQ = HEADS * HP

    nlat = dm.B * dm.N // tm

    def body(dh_ref, dq_ref, dkl_ref, dkc_ref, dvl_ref, dvc_ref, h_ref, mod_ref, g_ref, gn_ref, tab_ref, wa_ref, wuq_ref, wukv_ref,
             dho_ref, part_ref, gwa_ref, gwuq_ref, gwukv_ref, dgn_ref, dqraw_s, dkvraw_s):
        i = pl.program_id(0)
        pick = lambda lat_ref, ctx_ref, cols: jnp.where(i < nlat, lat_ref[:, cols], ctx_ref[:, cols])

        @pl.when(i == 0)
        def _():
            gwa_ref[...] = jnp.zeros_like(gwa_ref)
            gwuq_ref[...] = jnp.zeros_like(gwuq_ref)
            gwukv_ref[...] = jnp.zeros_like(gwukv_ref)
            dgn_ref[...] = jnp.zeros_like(dgn_ref)

        hv = h_ref[...]
        hn = _pre(hv, g_ref[...], mod_ref[0, 3:4, :], mod_ref[0, 4:5, :]).astype(BF16)
        f = _mla_heads_fwd(_dot(hn, wa_ref[...]), gn_ref, wuq_ref, wukv_ref)
        c, s1, s2 = tab_ref[0], tab_ref[1], tab_ref[2]
        gq, gk = gn_ref[2:3, :], gn_ref[3:4, :]
        dgq = jnp.zeros((1, HP), F32)
        dgk = jnp.zeros((1, HP), F32)
        dkrp = jnp.zeros((tm, HP - QK_NOPE), F32)
        for hd in range(HEADS):
            qh, rq = _rms(f["qraw"][:, hd * HP:(hd + 1) * HP], QK_HEAD)
            dqn = _rope_t(dq_ref[:, hd * HP:(hd + 1) * HP] * QK_SCALE, c, s1, s2)
            dgq = dgq + jnp.sum(dqn * qh, axis=0, keepdims=True)
            dqraw_s[:, hd * HP:(hd + 1) * HP] = _rms_bwd(dqn * gq, qh, rq, QK_HEAD)
            kpre = jnp.concatenate([f["kvraw"][:, hd * HP:hd * HP + QK_NOPE], f["krp"]], axis=1)
            kh, rk = _rms(kpre, QK_HEAD)
            dkn = _rope_t(pick(dkl_ref, dkc_ref, slice(hd * HP, (hd + 1) * HP)), c, s1, s2)
            dgk = dgk + jnp.sum(dkn * kh, axis=0, keepdims=True)
            dkpre = _rms_bwd(dkn * gk, kh, rk, QK_HEAD)
            dkvraw_s[:, hd * HP:hd * HP + QK_NOPE] = dkpre[:, :QK_NOPE]
            dkrp = dkrp + dkpre[:, QK_NOPE:]
            dkvraw_s[:, hd * HP + QK_NOPE:(hd + 1) * HP] = pick(dvl_ref, dvc_ref, slice(hd * V_HEAD, (hd + 1) * V_HEAD))
        dqraw = dqraw_s[...].astype(BF16)
        dkvraw = dkvraw_s[...].astype(BF16)
        gwuq_ref[...] += _dot_tn(f["cqn"], dqraw)
        gwukv_ref[...] += _dot_tn(f["ckvn"], dkvraw)
        dcqn = _dot_nt(dqraw, wuq_ref[...])
        dckvn = _dot_nt(dkvraw, wukv_ref[...])
        dgqa = jnp.sum(dcqn * f["cqh"], axis=0, keepdims=True)
        dgkva = jnp.sum(dckvn * f["ckvh"], axis=0, keepdims=True)
        dcq = _rms_bwd(dcqn * gn_ref[0:1, :], f["cqh"], f["rq"], Q_LORA)
        dckv = _rms_bwd(dckvn * gn_ref[1:2, :KV_LORA], f["ckvh"], f["rkv"], KV_LORA)
        dz = jnp.concatenate([dcq, dckv, dkrp], axis=1).astype(BF16)
        gwa_ref[...] += _dot_tn(hn, dz)
        dhn = _dot_nt(dz, wa_ref[...])
        dhb, dshift, dscale, dg = _pre_bwd(dhn, hv, g_ref[...], mod_ref[0, 4:5, :])
        dho_ref[...] = dh_ref[...] + dhb
        _write_part(part_ref, dshift, dscale, None, dg)
        dgn_ref[0:1, :] += dgqa
        dgn_ref[1:2, :KV_LORA] += dgkva
        dgn_ref[2:3, :] += dgq
        dgn_ref[3:4, :] += dgk

    row = pl.BlockSpec((tm, D), lambda i: (i, 0))
    wide = pl.BlockSpec((tm, HQ), lambda i: (i, 0))
    const2 = lambda i: (0, 0)
    return pl.pallas_call(
        body, name=name, grid=(nblk,),
        in_specs=[row, wide, pl.BlockSpec((tm, HQ), lambda i: (jnp.minimum(i, nlat - 1), 0)),
                  pl.BlockSpec((tm, HQ), lambda i: (jnp.maximum(i - nlat, 0), 0)),
                  pl.BlockSpec((tm, HEADS * V_HEAD), lambda i: (jnp.minimum(i, nlat - 1), 0)),
                  pl.BlockSpec((tm, HEADS * V_HEAD), lambda i: (jnp.maximum(i - nlat, 0), 0)), row,
                  pl.BlockSpec((1, 9, D), lambda i: (grp(i), 0, 0)), pl.BlockSpec((1, D), const2),
                  pl.BlockSpec((None, 8, HP), lambda i: (j, 0, 0)), pl.BlockSpec((3, tm, HP), lambda i: (0, i, 0)),
                  pl.BlockSpec((None, D, 512), lambda i: (j, 0, 0)), pl.BlockSpec((None, Q_LORA, HQ), lambda i: (j, 0, 0)),
                  pl.BlockSpec((None, KV_LORA, HQ), lambda i: (j, 0, 0))],
        out_specs=[row, pl.BlockSpec((1, 8, D), lambda i: (i, 0, 0)), pl.BlockSpec((D, 512), const2),
                   pl.BlockSpec((Q_LORA, HQ), const2), pl.BlockSpec((KV_LORA, HQ), const2), pl.BlockSpec((8, HP), const2)],
        out_shape=[jax.ShapeDtypeStruct((T, D), F32), jax.ShapeDtypeStruct((nblk, 8, D), F32),
                   jax.ShapeDtypeStruct((D, 512), F32), jax.ShapeDtypeStruct((Q_LORA, HQ), F32),
                   jax.ShapeDtypeStruct((KV_LORA, HQ), F32), jax.ShapeDtypeStruct((8, HP), F32)],
        scratch_shapes=[pltpu.VMEM((tm, HQ), F32), pltpu.VMEM((tm, HQ), F32)],
        compiler_params=_cparams("arbitrary"),
    )(dh, dq, dkl, dkc, dvl, dvc, h, mod, g, gains, tabs, w_a, w_uq, w_ukv)


def _attn_specs(dm):
    tq = dm.CTX
    nq = dm.N // tq
    cblk0 = dm.B * nq
    HP = HEAD_PAD
    qrow = lambda b, i: jnp.where(i < nq, b * nq + i, cblk0 + b)
    return dict(
        tq=tq, nq=nq,
        q=pl.BlockSpec((tq, HP), lambda b, hd, i: (qrow(b, i), hd)),
        k_lat=pl.BlockSpec((dm.N, HP), lambda b, hd, i: (b, hd)),
        k_ctx=pl.BlockSpec((tq, HP), lambda b, hd, i: (cblk0 + b, hd)),
        v_lat=pl.BlockSpec((dm.N, V_HEAD), lambda b, hd, i: (b, hd)),
        v_ctx=pl.BlockSpec((tq, V_HEAD), lambda b, hd, i: (cblk0 + b, hd)),
        o=pl.BlockSpec((tq, V_HEAD), lambda b, hd, i: (qrow(b, i), hd)),
    )


def _attn_exp(q, keys, first_off=None):
    s = [_dot_nt(q, kk) for kk in keys]
    if first_off is not None:
        s[0] = s[0] + first_off
    m = functools.reduce(jnp.maximum, [jnp.max(x, axis=-1, keepdims=True) for x in s])
    e = [jnp.exp(x - m) for x in s]
    return e, 1.0 / sum(jnp.sum(x, axis=-1, keepdims=True) for x in e)


def _attn_fwd(q, k, v, dm, name, rider=None):
    T = dm.T
    sp = _attn_specs(dm)
    nq = sp["nq"]

    def body(q_ref, kl_ref, kc_ref, vl_ref, vc_ref, o_ref):
        i = pl.program_id(2)

        @pl.when(i < nq)
        def _():
            (el, ec), inv = _attn_exp(q_ref[...], [kl_ref[...], kc_ref[...]])
            o_ref[...] = ((_dot(el.astype(BF16), vl_ref[...]) + _dot(ec.astype(BF16), vc_ref[...])) * inv).astype(BF16)

        @pl.when(i == nq)
        def _():
            (ec,), inv = _attn_exp(q_ref[...], [kc_ref[...]])
            o_ref[...] = (_dot(ec.astype(BF16), vc_ref[...]) * inv).astype(BF16)

    (o,), got = _hosted(
        body, rider, name=name, grid=(dm.B, HEADS, nq + 1),
        in_specs=[sp["q"], sp["k_lat"], sp["k_ctx"], sp["v_lat"], sp["v_ctx"]], out_specs=[sp["o"]],
        out_shape=[jax.ShapeDtypeStruct((T, HEADS * V_HEAD), BF16)], scratch_shapes=[],
        sem=("parallel", "parallel", "arbitrary"), args=(q, k, k, v, v))
    return o, got


def _attn_bwd(q, k, v, o, do, dm, name):
    T = dm.T
    sp = _attn_specs(dm)
    nq, tq = sp["nq"], sp["tq"]
    HP, HQ, HV = HEAD_PAD, HEADS * HEAD_PAD, HEADS * V_HEAD

    def body(q_ref, kl_ref, kc_ref, vl_ref, vc_ref, o_ref, do_ref, dq_ref, dkl_ref, dkc_ref, dvl_ref, dvc_ref):
        i = pl.program_id(2)

        @pl.when(i == 0)
        def _():
            dkl_ref[...] = jnp.zeros_like(dkl_ref)
            dkc_ref[...] = jnp.zeros_like(dkc_ref)
            dvl_ref[...] = jnp.zeros_like(dvl_ref)
            dvc_ref[...] = jnp.zeros_like(dvc_ref)

        qv = q_ref[...]
        dov = do_ref[...]
        dob = dov.astype(BF16)
        delta = jnp.sum(dov * o_ref[...].astype(F32), axis=-1, keepdims=True)
        (el, ec), inv = _attn_exp(qv, [kl_ref[...], kc_ref[...]], jnp.where(i == nq, NEG, 0.0))
        pl_, pc = el * inv, ec * inv
        dsl = (pl_ * (_dot_nt(dob, vl_ref[...]) - delta)).astype(BF16)
        dsc = (pc * (_dot_nt(dob, vc_ref[...]) - delta)).astype(BF16)
        dq_ref[...] = _dot(dsl, kl_ref[...]) + _dot(dsc, kc_ref[...])
        dkl_ref[...] += _dot_tn(dsl, qv)
        dkc_ref[...] += _dot_tn(dsc, qv)
        dvl_ref[...] += _dot_tn(pl_.astype(BF16), dob)
        dvc_ref[...] += _dot_tn(pc.astype(BF16), dob)

    return pl.pallas_call(
        body, name=name, grid=(dm.B, HEADS, nq + 1),
        in_specs=[sp["q"], sp["k_lat"], sp["k_ctx"], sp["v_lat"], sp["v_ctx"], sp["o"], sp["o"]],
        out_specs=[sp["q"], sp["k_lat"], pl.BlockSpec((tq, HP), lambda b, hd, i: (b, hd)),
                   sp["v_lat"], pl.BlockSpec((tq, V_HEAD), lambda b, hd, i: (b, hd))],
        out_shape=[jax.ShapeDtypeStruct((T, HQ), F32), jax.ShapeDtypeStruct((dm.B * dm.N, HQ), F32),
                   jax.ShapeDtypeStruct((dm.B * dm.CTX, HQ), F32), jax.ShapeDtypeStruct((dm.B * dm.N, HV), F32),
                   jax.ShapeDtypeStruct((dm.B * dm.CTX, HV), F32)],
        compiler_params=_cparams("parallel", "parallel", "arbitrary"),
    )(q, k, k, v, v, o, do)


def _loss_grad(h, target, dm, name):
    T, D = h.shape
    tm = dm.tm
    nlat = dm.B * dm.N // tm

    def body(h_ref, t_ref, dh_ref, ls_ref):
        lat = (pl.program_id(0) < nlat).astype(F32)
        diff = (h_ref[...] - t_ref[...]) * lat
        dh_ref[...] = diff * (1.0 / D)
        ls_ref[...] = jnp.zeros(ls_ref.shape, F32) + (0.5 / D) * jnp.sum(diff * diff)

    return pl.pallas_call(
        body, name=name, grid=(T // tm,),
        in_specs=[pl.BlockSpec((tm, D), lambda i: (i, 0)), pl.BlockSpec((tm, D), lambda i: (jnp.minimum(i, nlat - 1), 0))],
        out_specs=[pl.BlockSpec((tm, D), lambda i: (i, 0)), pl.BlockSpec((1, 8, 128), lambda i: (i, 0, 0))],
        out_shape=[jax.ShapeDtypeStruct((T, D), F32), jax.ShapeDtypeStruct((T // tm, 8, 128), F32)],
        compiler_params=_cparams("parallel"),
    )(h, target)


def _col_block(cols, target=1152):
    return max(t for t in range(128, min(cols, target) + 1, 128) if cols % t == 0)


def _mod_fwd(cond, w_mod, b_mod, name):
    L, D, C = w_mod.shape
    R = cond.shape[0]
    cb = _col_block(C)

    def body(c_ref, w_ref, b_ref, o_ref):
        cv = c_ref[...]
        sc = (cv * jax.nn.sigmoid(cv)).astype(BF16)
        o_ref[...] = _dot(sc, w_ref[...].astype(BF16)) + b_ref[...]

    return pl.pallas_call(
        body, name=name, grid=(L, C // cb),
        in_specs=[pl.BlockSpec((R, D), lambda l, c: (0, 0)), pl.BlockSpec((None, D, cb), lambda l, c: (l, 0, c)),
                  pl.BlockSpec((None, 1, cb), lambda l, c: (l, 0, c))],
        out_specs=pl.BlockSpec((None, R, cb), lambda l, c: (l, 0, c)),
        out_shape=jax.ShapeDtypeStruct((L, R, C), F32),
        compiler_params=_cparams("parallel", "parallel"),
    )(cond, w_mod, b_mod)


def _mod_bwd(cond, dmod, w_mod, name):
    L, D, C = w_mod.shape
    R = cond.shape[0]
    cb = _col_block(C)
    nc = C // cb

    def body(c_ref, dm_ref, w_ref, gw_ref, ds_ref):
        cv = c_ref[...]
        sc = (cv * jax.nn.sigmoid(cv)).astype(BF16)
        dmv = dm_ref[...].astype(BF16)
        gw_ref[...] = _dot_tn(sc, dmv)
        part = _dot_nt(dmv, w_ref[...].astype(BF16))

        @pl.when(pl.program_id(1) == 0)
        def _():
            ds_ref[...] = part

        @pl.when(pl.program_id(1) > 0)
        def _():
            ds_ref[...] += part

    return pl.pallas_call(
        body, name=name, grid=(L, nc),
        in_specs=[pl.BlockSpec((R, D), lambda l, c: (0, 0)), pl.BlockSpec((None, R, cb), lambda l, c: (l, 0, c)),
                  pl.BlockSpec((None, D, cb), lambda l, c: (l, 0, c))],
        out_specs=[pl.BlockSpec((None, D, cb), lambda l, c: (l, 0, c)), pl.BlockSpec((None, R, D), lambda l, c: (l, 0, 0))],
        out_shape=[jax.ShapeDtypeStruct((L, D, C), F32), jax.ShapeDtypeStruct((L, R, D), F32)],
        compiler_params=_cparams("parallel", "arbitrary"),
    )(cond, dmod, w_mod)


def _row_block(rows, cols, budget=1 << 20):
    best = None
    for t in range(16, rows + 1, 16):
        if rows % t == 0 and t * cols * 4 <= budget:
            best = t
    return best if best is not None else rows


def _sum_slots(recvs, owns, chip, core, bufs, pieces, piece, name):
    n = len(recvs)
    S, R, C = recvs[0].shape
    tr = _row_block(R, C, budget=512 << 10)

    def body(ids_ref, *refs):
        for r_ref, p_ref, o_ref in zip(refs[:n], refs[n:2 * n], refs[-n:]):
            acc = None
            for s in range(S):
                v = jnp.where(ids_ref[0] == s, p_ref[s], r_ref[s]).astype(F32)
                acc = v if acc is None else acc + v
            o_ref[...] = acc

    blk = pl.BlockSpec((S, tr, C), lambda i, ids: (0, i, 0))
    held = [] if bufs is None else list(bufs)
    return pl.pallas_call(
        body, name=name,
        grid_spec=pltpu.PrefetchScalarGridSpec(
            num_scalar_prefetch=1, grid=(R // tr,), in_specs=[blk] * (2 * n) + [pl.BlockSpec(memory_space=pl.ANY)] * len(held),
            out_specs=[pl.BlockSpec((None, None, tr, C), lambda i, ids: (piece, ids[1], i, 0))] * n),
        out_shape=[jax.ShapeDtypeStruct((pieces, 2, R, C), F32)] * n,
        input_output_aliases={1 + 2 * n + i: i for i in range(len(held))}, compiler_params=_cparams("parallel"),
    )(jnp.stack([chip, core]).astype(jnp.int32), *recvs, *owns, *held)


def _adamw(w, gs, m, v, name):
    ng = len(gs)
    R, C = w.shape
    tr = _row_block(R, C)
    c1 = 1.0 / (1.0 - ADAM_B1 ** ADAM_STEP)
    c2 = 1.0 / (1.0 - ADAM_B2 ** ADAM_STEP)

    def body(w_ref, *refs):
        m_ref, v_ref, g_ref, d_ref, mo_ref, vo_ref = refs[ng:]
        g = refs[0][...]
        for g_more in refs[1:ng]:
            g = g + g_more[...]
        g_ref[...] = g
        mn = ADAM_B1 * m_ref[...] + (1.0 - ADAM_B1) * g
        vn = ADAM_B2 * v_ref[...] + (1.0 - ADAM_B2) * (g * g)
        mo_ref[...] = mn
        vo_ref[...] = vn
        d_ref[...] = -ADAM_LR * ((mn * c1) / (jnp.sqrt(vn * c2) + ADAM_EPS) + ADAM_WD * w_ref[...])

    blk = pl.BlockSpec((tr, C), lambda i: (i, 0))
    sd = jax.ShapeDtypeStruct((R, C), F32)
    return pl.pallas_call(
        body, name=name, grid=(R // tr,), in_specs=[blk] * (3 + ng), out_specs=[blk] * 4, out_shape=[sd] * 4,
        compiler_params=_cparams("parallel"),
    )(w, *gs, m, v)


def _rope_tables(dm):
    n = dm.N
    t = jnp.arange(n)
    r = (t // GRID_W).astype(F32)
    col = (t % GRID_W).astype(F32)
    nf = QK_ROPE // 4
    inv = ROPE_BASE ** (-jnp.arange(nf, dtype=F32) / nf)
    ang = jnp.stack([r[:, None] * inv, col[:, None] * inv], axis=1)
    cos, sin = jnp.cos(ang), jnp.sin(ang)
    zero = jnp.zeros_like(sin)
    c64 = jnp.stack([cos, cos], axis=2).reshape(n, QK_ROPE)
    s1 = jnp.stack([-sin, zero], axis=2).reshape(n, QK_ROPE)
    s2 = jnp.stack([zero, sin], axis=2).reshape(n, QK_ROPE)

    def pad(x, fill):
        return jnp.concatenate([jnp.full((n, QK_NOPE), fill, F32), x, jnp.full((n, HEAD_PAD - QK_HEAD), fill, F32)], axis=1)

    lat = jnp.stack([pad(c64, 1.0), pad(s1, 0.0), pad(s2, 0.0)])
    lat = jnp.tile(lat, (1, dm.B, 1))
    nctx = dm.B * dm.CTX
    ctx = jnp.stack([jnp.ones((nctx, HEAD_PAD), F32), jnp.zeros((nctx, HEAD_PAD), F32), jnp.zeros((nctx, HEAD_PAD), F32)])
    return jnp.concatenate([lat, ctx], axis=1)


def _fold_parts(part, dm):
    nblk = part.shape[0]
    nb = (dm.N * nblk) // dm.T
    groups = [part[b * nb:(b + 1) * nb].sum(axis=0) for b in range(dm.B)]
    groups.append(part[dm.B * nb:].sum(axis=0))
    return jnp.stack(groups)


def grouped(items, key):
    groups = {}
    for it in items:
        groups.setdefault(key(it), []).append(it)
    return list(groups.values())


def _flat2(a):
    return a.reshape(-1, a.shape[-1])


def kernel(x, c, ctx, c_ctx, w_mod, b_mod, g_norm, ffn_w1, ffn_w3, ffn_w2, sc_w_in, sc_conv, sc_w_out, mla_w_a, mla_g_qa, mla_w_uq, mla_g_kva, mla_w_ukv, mla_g_q, mla_g_k, mla_w_o, loss_target, m_c_ctx, m_w_mod, m_b_mod, m_g_norm, m_ffn_w1, m_ffn_w3, m_ffn_w2, m_sc_w_in, m_sc_conv, m_sc_w_out, m_mla_w_a, m_mla_g_qa, m_mla_w_uq, m_mla_g_kva, m_mla_w_ukv, m_mla_g_q, m_mla_g_k, m_mla_w_o, v_c_ctx, v_w_mod, v_b_mod, v_g_norm, v_ffn_w1, v_ffn_w3, v_ffn_w2, v_sc_w_in, v_sc_conv, v_sc_w_out, v_mla_w_a, v_mla_g_qa, v_mla_w_uq, v_mla_g_kva, v_mla_w_ukv, v_mla_g_q, v_mla_g_k, v_mla_w_o):
    B, N, D = x.shape
    CTX = ctx.shape[1]
    T = B * (N + CTX)
    tm = next(t for t in (512, 256, 128, 64, 32, 16) if N % t == 0 and (B * CTX) % t == 0)
    dm = Dims(B, N, CTX, D, T, tm)
    L = w_mod.shape[0]
    La, Lb = sc_w_in.shape[0], mla_w_a.shape[0]
    S = N_CHIPS
    ndev = 2 * S
    xi, yi, ci = lax.axis_index("x"), lax.axis_index("y"), lax.axis_index("c")
    chip = 2 * xi + yi
    dev = 2 * chip + ci
    weights = dict(c_ctx=c_ctx, w_mod=w_mod, b_mod=b_mod, g_norm=g_norm, ffn_w1=ffn_w1, ffn_w3=ffn_w3, ffn_w2=ffn_w2,
                   sc_w_in=sc_w_in, sc_conv=sc_conv, sc_w_out=sc_w_out, mla_w_a=mla_w_a, mla_g_qa=mla_g_qa,
                   mla_w_uq=mla_w_uq, mla_g_kva=mla_g_kva, mla_w_ukv=mla_w_ukv, mla_g_q=mla_g_q, mla_g_k=mla_g_k,
                   mla_w_o=mla_w_o)
    mom = dict(c_ctx=(m_c_ctx, v_c_ctx), w_mod=(m_w_mod, v_w_mod), b_mod=(m_b_mod, v_b_mod), g_norm=(m_g_norm, v_g_norm),
               ffn_w1=(m_ffn_w1, v_ffn_w1), ffn_w3=(m_ffn_w3, v_ffn_w3), ffn_w2=(m_ffn_w2, v_ffn_w2),
               sc_w_in=(m_sc_w_in, v_sc_w_in), sc_conv=(m_sc_conv, v_sc_conv), sc_w_out=(m_sc_w_out, v_sc_w_out),
               mla_w_a=(m_mla_w_a, v_mla_w_a), mla_g_qa=(m_mla_g_qa, v_mla_g_qa), mla_w_uq=(m_mla_w_uq, v_mla_w_uq),
               mla_g_kva=(m_mla_g_kva, v_mla_g_kva), mla_w_ukv=(m_mla_w_ukv, v_mla_w_ukv), mla_g_q=(m_mla_g_q, v_mla_g_q),
               mla_g_k=(m_mla_g_k, v_mla_g_k), mla_w_o=(m_mla_w_o, v_mla_w_o))

    big = ["ffn_w1", "ffn_w3", "ffn_w2", "sc_w_in", "sc_w_out", "mla_w_a", "mla_w_uq", "mla_w_ukv", "mla_w_o"]
    F = ffn_w1.shape[-1]
    transposed = ("ffn_w1", "ffn_w3")
    for n in transposed:
        weights[n] = jnp.swapaxes(weights[n], 2, 3)
        mom[n] = tuple(jnp.swapaxes(a, 2, 3) for a in mom[n])
    mixer_names =(["sc_w_in", "sc_w_out"], ["mla_w_a", "mla_w_uq", "mla_w_ukv", "mla_w_o"])

    def placed(names, piece, npieces):
        w2 = [_flat2(weights[n]) for n in names]
        rows = w2[0].shape[0] // npieces
        return dict(zip(names, _place_cast(w2, piece * rows, rows, chip, S, "place_weight")))

    bufs = {}
    for l in range(L):
        for k in range(2):
            bufs["f", l, k] = placed(["ffn_w1", "ffn_w3", "ffn_w2"], 2 * l + k, 2 * L)
        bufs["m", l] = {}
        for n in mixer_names[l % 2]:
            bufs["m", l].update(placed([n], l // 2, weights[n].shape[0]))
    order = [stage for l in range(L) for stage in (("f", l, 0), ("m", l), ("f", l, 1))]

    def gather_after(stage):
        at = order.index(stage)
        if at + 1 == len(order):
            return None, lambda got: None
        nxt = bufs[order[at + 1]]
        names = list(nxt)
        return _gather_rider([nxt[n] for n in names]), lambda got: nxt.update(zip(names, got))

    names = list(bufs[order[0]])
    bufs[order[0]].update(zip(names, _ride_alone(_gather_rider([bufs[order[0]][n] for n in names]), "gather_weights")))

    def ffn_weights(l, k):
        b = bufs["f", l, k]
        return b["ffn_w1"], b["ffn_w3"], b["ffn_w2"]

    def mixer_weights(l):
        b = bufs["m", l]
        w = {}
        if l % 2 == 0:
            w["w_in"] = b["sc_w_in"][:, None]
            w["w_out"] = b["sc_w_out"].reshape(1, D, D)
        else:
            w["w_a"] = jnp.pad(b["mla_w_a"].reshape(1, D, -1), ((0, 0), (0, 0), (0, 512 - (Q_LORA + KV_LORA + QK_ROPE))))
            wuq = jnp.moveaxis(b["mla_w_uq"], 0, 1).reshape(1, Q_LORA, HEADS, QK_HEAD)
            w["w_uq"] = jnp.pad(wuq, ((0, 0), (0, 0), (0, 0), (0, HEAD_PAD - QK_HEAD))).reshape(1, Q_LORA, HEADS * HEAD_PAD)
            w["w_ukv"] = jnp.moveaxis(b["mla_w_ukv"], 0, 1).reshape(1, KV_LORA, HEADS * HEAD_PAD)
            w["w_o"] = b["mla_w_o"].reshape(1, HEADS * V_HEAD, D)
        return w

    vecs = ["g_norm", "sc_conv", "mla_g_qa"]
    gathered = _gather_small([_flat2(weights[n]) for n in vecs], ("x", "y"), "gather_vectors")
    gw = {n: g.reshape((S,) + weights[n].shape) for n, g in zip(vecs, gathered)}
    gnorm = jnp.moveaxis(gw["g_norm"], 0, 2).reshape(L, 3, D)
    convw = jnp.moveaxis(gw["sc_conv"], 0, 2).reshape(La, 3, D)
    gqa = jnp.moveaxis(gw["mla_g_qa"], 0, 1).reshape(Lb, Q_LORA)
    padl = lambda a: jnp.pad(a, ((0, 0), (0, HEAD_PAD - a.shape[1])))
    gains = jnp.stack([padl(gqa), padl(mla_g_kva), padl(mla_g_q), padl(mla_g_k)], axis=1)
    gains = jnp.pad(gains, ((0, 0), (0, 4), (0, 0)))

    R = -(-(ndev * B + 1) // 16) * 16
    call = _gather_small([c], ("x", "y", "c"), "gather_cond")[0].reshape(ndev * B, D)
    cond = jnp.concatenate([call, c_ctx[None], jnp.zeros((R - ndev * B - 1, D), F32)], axis=0)
    C = w_mod.shape[-1]
    bm = lax.dynamic_slice_in_dim(b_mod, chip * C, C, axis=1)[:, None, :]
    mshard = _mod_fwd(cond, w_mod, bm, "mod_fwd")
    mfull = _gather_small([mshard.reshape(L * R, C)], ("x", "y"), "gather_mod")[0].reshape(S, L, R, C)
    mfull = jnp.moveaxis(mfull, 0, 2).reshape(L, R, S * C)
    mine = lax.dynamic_slice_in_dim(mfull, dev * B, B, axis=1)
    mod = jnp.concatenate([mine, mfull[:, ndev * B:ndev * B + 1]], axis=1).reshape(L, B + 1, 9, D)

    tabs = _rope_tables(dm)
    h = jnp.concatenate([x.reshape(B * N, D), ctx.reshape(B * CTX, D)], axis=0)

    saved = []
    lw = [None] * L
    for l in range(L):
        kind, j = l % 2, l // 2
        sv = {}
        sv["h0"] = h
        rider, keep = gather_after(("f", l, 0))
        (h, sv["a1"], sv["b1"], sv["hn1"], sv["y1"]), got = _ffn_fwd(h, mod[l], gnorm[l, 0:1], *ffn_weights(l, 0), 0, dm,
                                                                      "ffn_fwd", rider)
        keep(got)
        sv["h1"] = h
        W = lw[l] = mixer_weights(l)
        rider, keep = gather_after(("m", l))
        if kind == 0:
            (sv["p"], sv["hnm"]), got = _sc_in_fwd(h, mod[l], gnorm[l, 1:2], W["w_in"], 0, dm, "sc_in_fwd", rider)
            sv["z"] = _conv_fwd(sv["p"], convw[j], dm, "conv_fwd")
            h, sv["ym"] = _out_fwd(sv["z"], W["w_out"], h, mod[l], 0, dm, "sc_out_fwd")
        else:
            sv["hnm"], sv["q"], sv["k"], sv["v"] = _mla_proj_fwd(h, mod[l], gnorm[l, 1:2], gains[j:j + 1], tabs, W["w_a"], W["w_uq"],
                                                                 W["w_ukv"], 0, dm, "mla_proj_fwd")
            sv["o"], got = _attn_fwd(sv["q"], sv["k"], sv["v"], dm, "attn_fwd", rider)
            h, sv["ym"] = _out_fwd(sv["o"], W["w_o"], h, mod[l], 0, dm, "mla_out_fwd")
        keep(got)
        sv["h2"] = h
        rider, keep = gather_after(("f", l, 1))
        (h, sv["a2"], sv["b2"], sv["hn2"], sv["y2"]), got = _ffn_fwd(h, mod[l], gnorm[l, 2:3], *ffn_weights(l, 1), 1, dm,
                                                                      "ffn_fwd", rider)
        keep(got)
        saved.append(sv)

    dh, lsum = _loss_grad(h, loss_target.reshape(B * N, D), dm, "loss_grad")
    loss = lax.psum(jnp.sum(lsum[:, 0, 0]), ("x", "y", "c"))

    wq = D // S
    gsum = {n: None for n in big}
    npieces = {n: weights[n].shape[0] * (weights[n].shape[1] if n.startswith("ffn") else 1) for n in big}
    dmod = [None] * L
    dgn = [None] * L
    dconv = [None] * La
    dgains = [None] * Lb
    tk = tm * next(f for f in (3, 2, 1) if (T // tm) % f == 0)
    nk = T // tk
    full_a = pl.BlockSpec((tk, D), lambda s, kk: (kk, 0))
    shard_b = pl.BlockSpec((None, tk, F), lambda s, kk: (s, kk, 0))
    per_slot = lambda r_, c_: pl.BlockSpec((None, r_, c_), lambda s, kk: (s, 0, 0))

    def make_job(grads):
        parts = [g_ for _, _, g_ in grads]
        theirs = _swap_halves(parts, "swap_halves")
        pairs = [None] * len(grads)
        for idx in grouped(range(len(grads)), lambda i: parts[i].shape):
            outs = _pair_sum([parts[i] for i in idx], [theirs[i] for i in idx], ci, "pair_sum")
            for i, o in zip(idx, outs):
                pairs[i] = o
        return [(n, p, pair) for (n, p, _), pair in zip(grads, pairs)]

    def finish_job(job, recv):
        key = lambda i: (recv[i].shape, npieces[job[i][0]], job[i][1], gsum[job[i][0]] is None)
        for idx in grouped(range(len(job)), key):
            names_ = [job[i][0] for i in idx]
            held = None if gsum[names_[0]] is None else [gsum[n] for n in names_]
            outs = _sum_slots([recv[i] for i in idx], [job[i][2] for i in idx], chip, ci, held, npieces[names_[0]],
                              job[idx[0]][1], "sum_slots")
            gsum.update(zip(names_, outs))

    def ffn_back(dh, sv, l, k, job):
        sfx = "1" if k == 0 else "2"
        rider = _scatter_rider([pair for _, _, pair in job]) if job else None
        (dh, da, db, sw, dy, part), recv = _ffn_bwd(dh, sv["h0" if k == 0 else "h2"], mod[l], gnorm[l, 2 * k:2 * k + 1], sv["y" + sfx],
                                                    sv["a" + sfx], sv["b" + sfx], *ffn_weights(l, k), k, dm, "ffn_bwd", rider)
        finish_job(job, recv)
        g1 = _mm_tn(da, sv["hn" + sfx], shard_b, full_a, (S, F, D), per_slot(F, D), (S, nk), "gw1")
        g3 = _mm_tn(db, sv["hn" + sfx], shard_b, full_a, (S, F, D), per_slot(F, D), (S, nk), "gw3")
        g2 = _mm_tn(sw, dy, shard_b, full_a, (S, F, D), per_slot(F, D), (S, nk), "gw2")
        p = 2 * l + k
        return dh, _fold_parts(part, dm), [("ffn_w1", p, g1), ("ffn_w3", p, g3), ("ffn_w2", p, g2)]

    one = (1, nk)
    a1 = lambda kdim: pl.BlockSpec((tk, kdim), lambda s, kk: (kk, 0))
    pending = []
    for l in reversed(range(L)):
        kind, j = l % 2, l // 2
        sv = saved[l]
        W = lw[l]
        dh, p2, grads = ffn_back(dh, sv, l, 1, pending)
        job2 = make_job(grads)
        if kind == 0:
            dy, dz, pg = _out_bwd(dh, sv["ym"], W["w_out"], mod[l], 0, dm, "sc_out_bwd")
            g_out = _mm_tn(sv["z"], dy, a1(D), a1(D), (1, D, D), per_slot(D, D), one, "gw_sc_out")
            dp, dconv[j] = _conv_bwd(dz, sv["p"], convw[j], dm, "conv_bwd")
            g_in = _mm_tn(sv["hnm"], dp, pl.BlockSpec((tk, D), lambda q, kk: (kk, 0)),
                          pl.BlockSpec((3, tk, wq), lambda q, kk: (0, kk, q)), (3, S, D, wq),
                          pl.BlockSpec((3, None, D, wq), lambda q, kk: (0, q, 0, 0)), (S, nk), "gw_sc_in")
            dh, pm = _sc_in_bwd(dh, dp, sv["h1"], mod[l], gnorm[l, 1:2], W["w_in"], 0, dm, "sc_in_bwd")
            grads = [("sc_w_in", j, jnp.moveaxis(g_in.reshape(S, 3, D, wq), 1, 2).reshape(S, D, 3 * wq)),
                     ("sc_w_out", j, g_out.reshape(S, D // S, D))]
        else:
            dy, do, pg = _out_bwd(dh, sv["ym"], W["w_o"], mod[l], 0, dm, "mla_out_bwd")
            g_o = _mm_tn(sv["o"], dy, a1(HEADS * V_HEAD), a1(D), (1, HEADS * V_HEAD, D), per_slot(HEADS * V_HEAD, D), one, "gw_mla_o")
            dq, dkl, dkc, dvl, dvc = _attn_bwd(sv["q"], sv["k"], sv["v"], sv["o"], do, dm, "attn_bwd")
            dh, pm, g_a, g_uq, g_ukv, dgains[j] = _mla_proj_bwd(
                dh, dq, dkl, dkc, dvl, dvc, sv["h1"], mod[l], gnorm[l, 1:2], gains[j:j + 1], tabs, W["w_a"], W["w_uq"], W["w_ukv"], 0, dm, "mla_proj_bwd")
            g_uq = g_uq.reshape(Q_LORA, HEADS, HEAD_PAD)[..., :QK_HEAD].reshape(Q_LORA, S, -1)
            grads = [("mla_w_a", j, g_a[:, :Q_LORA + KV_LORA + QK_ROPE].reshape(S, D // S, -1).astype(BF16)),
                     ("mla_w_uq", j, jnp.moveaxis(g_uq, 1, 0).astype(BF16)),
                     ("mla_w_ukv", j, jnp.moveaxis(g_ukv.reshape(KV_LORA, S, -1), 1, 0).astype(BF16)),
                     ("mla_w_o", j, g_o.reshape(S, HEADS * V_HEAD // S, D))]
        jobm = make_job(grads)
        pm = _fold_parts(pm, dm) + _fold_parts(pg, dm)
        dh, p0, grads = ffn_back(dh, sv, l, 0, job2 + jobm)
        pending = make_job(grads)
        dmod[l] = jnp.concatenate([p0[:, 0:3], pm[:, 0:3], p2[:, 0:3]], axis=1).reshape(B + 1, 9 * D)
        dgn[l] = jnp.stack([p0[:, 3].sum(0), pm[:, 3].sum(0), p2[:, 3].sum(0)])
    grad_x = dh[:B * N].reshape(B, N, D)

    dgains_a = jnp.stack(dgains)
    small = [jnp.stack(dmod).reshape(-1), jnp.stack(dgn).reshape(-1), jnp.stack(dconv).reshape(-1), dgains_a.reshape(-1)]
    sizes = [s_.shape[0] for s_ in small]
    flat = jnp.concatenate(small)
    pad = (-flat.shape[0]) % 1024
    flat = jnp.pad(flat, (0, pad)).reshape(-1, 128)
    allsmall = _gather_small([flat], ("x", "y", "c"), "gather_small")[0].reshape(ndev, -1)
    offs = [0]
    for s_ in sizes:
        offs.append(offs[-1] + s_)
    dmod_all = allsmall[:, offs[0]:offs[1]].reshape(ndev, L, B + 1, 9 * D)
    tot = allsmall[:, offs[1]:offs[4]].sum(axis=0)
    g_gnorm = tot[:offs[2] - offs[1]].reshape(L, 3, D)
    g_conv = tot[offs[2] - offs[1]:offs[3] - offs[1]].reshape(La, 3, D)
    g_gains = tot[offs[3] - offs[1]:].reshape(Lb, 8, HEAD_PAD)
    dM = jnp.concatenate([jnp.moveaxis(dmod_all[:, :, :B], 0, 1).reshape(L, ndev * B, 9 * D),
                          dmod_all[:, :, B].sum(axis=0)[:, None, :], jnp.zeros((L, R - ndev * B - 1, 9 * D), F32)], axis=1)
    g_bmod = dM.sum(axis=1)
    dM_mine = lax.dynamic_slice_in_dim(dM, chip * C, C, axis=2)
    g_wmod, dsil = _mod_bwd(cond, dM_mine, w_mod, "mod_bwd")
    dsil_ctx = dsil[:, ndev * B].sum(axis=0)
    dsil_all = _gather_small([jnp.pad(dsil_ctx.reshape(-1, 128), ((0, (-(D // 128)) % 8), (0, 0)))], ("x", "y"), "gather_dctx")[0]
    dsil_tot = dsil_all.sum(axis=0)[:D // 128].reshape(D)
    sg = jax.nn.sigmoid(c_ctx)
    g_cctx = dsil_tot * (sg * (1.0 + c_ctx * (1.0 - sg)))

    chip_cols = lambda a, width: lax.dynamic_slice_in_dim(a, chip * width, width, axis=a.ndim - 1)
    small_grads = dict(
        c_ctx=g_cctx, b_mod=g_bmod, g_norm=chip_cols(g_gnorm, D // S), sc_conv=chip_cols(g_conv, D // S),
        mla_g_qa=chip_cols(g_gains[:, 0, :Q_LORA], Q_LORA // S), mla_g_kva=g_gains[:, 1, :KV_LORA],
        mla_g_q=g_gains[:, 2, :QK_HEAD], mla_g_k=g_gains[:, 3, :QK_HEAD])

    finish_job(pending, _ride_alone(_scatter_rider([pair for _, _, pair in pending]), "scatter_grads"))
    gsum = dict(zip(big, _swap_cores_inplace([gsum[n] for n in big], "swap_cores")))

    grads, deltas, new_m, new_v = {}, {}, {}, {}
    for n, w in weights.items():
        shape = w.shape
        w2 = _flat2(w) if w.ndim > 1 else w.reshape(1, -1)
        m2, v2 = (a.reshape(w2.shape) for a in mom[n])
        if n in gsum:
            gs = [gsum[n].reshape(w2.shape)]
        elif n == "w_mod":
            gs = [_flat2(g_wmod)]
        else:
            gs = [small_grads[n].reshape(w2.shape)]
        g_, d_, m_, v_ = _adamw(w2, gs, m2, v2, "adamw")
        grads[n], deltas[n], new_m[n], new_v[n] = (a.reshape(shape) for a in (g_, d_, m_, v_))
    for n in transposed:
        grads[n], deltas[n], new_m[n], new_v[n] = (jnp.swapaxes(a, 2, 3) for a in (grads[n], deltas[n], new_m[n], new_v[n]))

    names = list(weights)
    return (loss, grad_x, *[grads[n] for n in names], *[deltas[n] for n in names], *[new_m[n] for n in names],
            *[new_v[n] for n in names])
```

```python
import functools
import math
from typing import NamedTuple

import jax
import jax.numpy as jnp
from jax import lax
from jax.experimental import pallas as pl
from jax.experimental.pallas import tpu as pltpu

F32 = jnp.float32
BF16 = jnp.bfloat16
EPS = 1e-6
GRID_W = 64
HEADS = 8
QK_NOPE = 128
QK_ROPE = 64
QK_HEAD = QK_NOPE + QK_ROPE
HEAD_PAD = 256
V_HEAD = 128
Q_LORA = 256
KV_LORA = 128
ROPE_BASE = 10000.0
QK_SCALE = QK_HEAD ** -0.5
ADAM_LR, ADAM_B1, ADAM_B2, ADAM_EPS, ADAM_WD, ADAM_STEP = 0.001, 0.9, 0.999, 1e-08, 0.01, 10
N_CHIPS = 4
VMEM_LIMIT = 56 * 1024 * 1024
MESH = pl.DeviceIdType.MESH
NEG = -1e30


class Dims(NamedTuple):
    B: int
    N: int
    CTX: int
    D: int
    T: int
    tm: int


def _cparams(*sem):
    return pltpu.CompilerParams(dimension_semantics=sem if sem else None, vmem_limit_bytes=VMEM_LIMIT)


def _dot(a, b):
    return jnp.dot(a, b, preferred_element_type=F32)


def _dot_nt(a, b):
    return lax.dot_general(a, b, (((1,), (1,)), ((), ())), preferred_element_type=F32)


def _dot_tn(a, b):
    return lax.dot_general(a, b, (((0,), (0,)), ((), ())), preferred_element_type=F32)


def _rms(x, n):
    r = lax.rsqrt(jnp.sum(x * x, axis=-1, keepdims=True) * (1.0 / n) + EPS)
    return x * r, r


def _rms_bwd(dxh, xh, r, n):
    return r * (dxh - xh * (jnp.sum(dxh * xh, axis=-1, keepdims=True) * (1.0 / n)))


def _pre(h, g, shift, scale):
    xh, _ = _rms(h, h.shape[-1])
    return (xh * g) * (1.0 + scale) + shift


def _pre_bwd(dout, h, g, scale):
    d = h.shape[-1]
    xh, r = _rms(h, d)
    n = xh * g
    dshift = jnp.sum(dout, axis=0, keepdims=True)
    dscale = jnp.sum(dout * n, axis=0, keepdims=True)
    dn = dout * (1.0 + scale)
    dg = jnp.sum(dn * xh, axis=0, keepdims=True)
    dh = _rms_bwd(dn * g, xh, r, d)
    return dh, dshift, dscale, dg


def _write_part(part_ref, dshift=None, dscale=None, dgate=None, dg=None):
    z = jnp.zeros((1, part_ref.shape[-1]), F32)
    part_ref[0, 0:1, :] = z if dshift is None else dshift
    part_ref[0, 1:2, :] = z if dscale is None else dscale
    part_ref[0, 2:3, :] = z if dgate is None else dgate
    part_ref[0, 3:4, :] = z if dg is None else dg
    part_ref[0, 4:8, :] = jnp.zeros((4, part_ref.shape[-1]), F32)


def _grp(dm):
    nb = dm.N // dm.tm
    return lambda i: jnp.minimum(i // nb, dm.B)


def _n_chunks(rows, row_bytes):
    n = 16
    while n > 1 and (rows % (16 * n) or (rows // n) * row_bytes < (256 << 10)):
        n //= 2
    return n


def _start_local(src, dst, sems, k0, nchunk):
    ch = src.shape[0] // nchunk
    copies = []
    for j in range(nchunk):
        cp = pltpu.make_async_copy(src.at[pl.ds(j * ch, ch)], dst.at[pl.ds(j * ch, ch)], sems.at[k0 + j])
        cp.start()
        copies.append(cp)
    return copies


def _gather_small(arrs, axes, name):
    n = len(arrs)
    nbits = len(axes)
    slots = 2 ** nbits
    pats = list(range(1, slots))
    nck = [_n_chunks(a.shape[-2], a.shape[-1] * a.dtype.itemsize) for a in arrs]
    base = [sum(nck[:i]) * len(pats) for i in range(n)]
    nsem = sum(nck) * len(pats)

    def body(*refs):
        ins, outs = refs[:n], refs[n:2 * n]
        send, recv, loc = refs[2 * n:]
        pos = {a: lax.axis_index(a) for a in ("x", "y", "c")}

        def slot_of(p):
            s = 0
            for a in axes:
                s = 2 * s + p[a]
            return s

        me = slot_of(pos)
        local = []
        for i in range(n):
            local += _start_local(ins[i], outs[i].at[me], loc, sum(nck[:i]), nck[i])
        remote = []
        for pi, pat in enumerate(pats):
            peer = dict(pos)
            for bi, a in enumerate(axes):
                if (pat >> (nbits - 1 - bi)) & 1:
                    peer[a] = 1 - pos[a]
            them = slot_of(peer)
            for i in range(n):
                ch = arrs[i].shape[-2] // nck[i]
                for j in range(nck[i]):
                    k = base[i] + pi * nck[i] + j
                    rs = pl.ds(j * ch, ch)
                    cp = pltpu.make_async_remote_copy(
                        src_ref=ins[i].at[rs], dst_ref=outs[i].at[me, rs], send_sem=send.at[k], recv_sem=recv.at[k],
                        device_id=(peer["x"], peer["y"], peer["c"]), device_id_type=MESH)
                    cp.start()
                    remote.append(cp)
        for cp in local:
            cp.wait()
        for cp in remote:
            cp.wait()

    out_shape = [jax.ShapeDtypeStruct((slots,) + a.shape, a.dtype) for a in arrs]
    any_spec = pl.BlockSpec(memory_space=pl.ANY)
    outs = pl.pallas_call(
        body, name=name, out_shape=out_shape, in_specs=[any_spec] * n, out_specs=[any_spec] * n,
        scratch_shapes=[pltpu.SemaphoreType.DMA((nsem,)), pltpu.SemaphoreType.DMA((nsem,)), pltpu.SemaphoreType.DMA((sum(nck),))],
        compiler_params=pltpu.CompilerParams(has_side_effects=True),
    )(*arrs)
    return list(outs)


class Rider(NamedTuple):
    ins: list
    out_shapes: list
    aliases: dict
    sems: list
    start: object
    mid: object
    end: object


MID_STEPS = 6


def _hosted(body, rider, *, name, grid, in_specs, out_specs, out_shape, scratch_shapes, sem, args):
    if rider is None:
        outs = pl.pallas_call(body, name=name, grid=grid, in_specs=in_specs, out_specs=out_specs, out_shape=out_shape,
                              scratch_shapes=scratch_shapes, compiler_params=_cparams(*sem))(*args)
        return outs, []
    n_in, n_out, n_s = len(in_specs), len(out_specs), len(scratch_shapes)
    nri, nro = len(rider.ins), len(rider.out_shapes)
    nsteps = math.prod(grid)

    def wrapped(*refs):
        bounds = [0, n_in, n_in + nri, n_in + nri + n_out, n_in + nri + n_out + nro, n_in + nri + n_out + nro + n_s, len(refs)]
        ins, rins, outs, routs, scr, sems = (refs[lo:hi] for lo, hi in zip(bounds[:-1], bounds[1:]))
        step = 0
        for ax, extent in enumerate(grid):
            step = step * extent + pl.program_id(ax)

        @pl.when(step == 0)
        def _():
            rider.start(rins, routs, sems)

        body(*ins, *outs, *scr)

        if rider.mid is not None:
            @pl.when(step == max(nsteps - 1 - MID_STEPS, 0))
            def _():
                rider.mid(rins, routs, sems)

        @pl.when(step == nsteps - 1)
        def _():
            rider.end(rins, routs, sems)

    any_spec = pl.BlockSpec(memory_space=pl.ANY)
    outs = pl.pallas_call(
        wrapped, name=name, grid=grid, in_specs=list(in_specs) + [any_spec] * nri, out_specs=list(out_specs) + [any_spec] * nro,
        out_shape=list(out_shape) + list(rider.out_shapes), scratch_shapes=list(scratch_shapes) + list(rider.sems),
        input_output_aliases={n_in + i: n_out + o for i, o in rider.aliases.items()},
        compiler_params=pltpu.CompilerParams(dimension_semantics=("arbitrary",) * len(grid), vmem_limit_bytes=VMEM_LIMIT,
                                             has_side_effects=True),
    )(*args, *rider.ins)
    return outs[:n_out], list(outs[n_out:])


def _gather_rider(bufs):
    n = len(bufs)
    halves = [a.shape[1] // 2 for a in bufs]
    nck = [_n_chunks(h, a.shape[2] * a.dtype.itemsize) for h, a in zip(halves, bufs)]
    base = [3 * sum(nck[:i]) for i in range(n)]
    nsem = 3 * sum(nck)

    def plan():
        x, y, c = lax.axis_index("x"), lax.axis_index("y"), lax.axis_index("c")
        pieces = []
        for pi, (px, py) in enumerate([(x, 1 - y), (1 - x, y), (1 - x, 1 - y)]):
            for i in range(n):
                ch = halves[i] // nck[i]
                for j in range(nck[i]):
                    pieces.append((base[i] + pi * nck[i] + j, px, py, 2 * px + py, i, j * ch, ch))
        return x, y, c, 2 * x + y, pieces

    def rows(i, off, ch, core):
        return pl.ds(pl.multiple_of(core * halves[i] + off, 16), ch)

    def over_ici(outs, sems, c, slot, k, px, py, i, off, ch):
        ref = outs[i].at[slot, rows(i, off, ch, c)]
        return pltpu.make_async_remote_copy(src_ref=ref, dst_ref=ref, send_sem=sems[0].at[k], recv_sem=sems[1].at[k],
                                            device_id=(px, py, c), device_id_type=MESH)

    def over_d2d(outs, sems, x, y, c, slot, k, i, off, ch, core):
        ref = outs[i].at[slot, rows(i, off, ch, core)]
        return pltpu.make_async_remote_copy(src_ref=ref, dst_ref=ref, send_sem=sems[2].at[k], recv_sem=sems[3].at[k],
                                            device_id=(x, y, 1 - c), device_id_type=MESH)

    def start(ins, outs, sems):
        x, y, c, me, pieces = plan()
        for k, px, py, them, i, off, ch in pieces:
            over_ici(outs, sems, c, me, k, px, py, i, off, ch).start()

    def mid(ins, outs, sems):
        x, y, c, me, pieces = plan()
        for k, px, py, them, i, off, ch in pieces:
            over_ici(outs, sems, c, them, k, px, py, i, off, ch).wait_recv()
            over_d2d(outs, sems, x, y, c, them, k, i, off, ch, c).start()

    def end(ins, outs, sems):
        x, y, c, me, pieces = plan()
        for k, px, py, them, i, off, ch in pieces:
            over_ici(outs, sems, c, me, k, px, py, i, off, ch).wait_send()
            over_d2d(outs, sems, x, y, c, them, k, i, off, ch, c).wait_send()
        for k, px, py, them, i, off, ch in pieces:
            over_d2d(outs, sems, x, y, c, them, k, i, off, ch, 1 - c).wait_recv()

    return Rider(ins=list(bufs), out_shapes=[jax.ShapeDtypeStruct(a.shape, a.dtype) for a in bufs],
                 aliases={i: i for i in range(n)}, sems=[pltpu.SemaphoreType.DMA((nsem,))] * 4, start=start, mid=mid, end=end)


def _scatter_rider(srcs):
    n = len(srcs)
    nck = [_n_chunks(a.shape[1], a.shape[2] * a.dtype.itemsize) for a in srcs]
    base = [3 * sum(nck[:i]) for i in range(n)]
    nsem = 3 * sum(nck)

    def copies(ins, outs, sems):
        x, y, c = lax.axis_index("x"), lax.axis_index("y"), lax.axis_index("c")
        me = 2 * x + y
        for pi, (px, py) in enumerate([(x, 1 - y), (1 - x, y), (1 - x, 1 - y)]):
            for i in range(n):
                ch = srcs[i].shape[1] // nck[i]
                for j in range(nck[i]):
                    k = base[i] + pi * nck[i] + j
                    rs = pl.ds(j * ch, ch)
                    yield pltpu.make_async_remote_copy(
                        src_ref=ins[i].at[2 * px + py, rs], dst_ref=outs[i].at[me, rs], send_sem=sems[0].at[k],
                        recv_sem=sems[1].at[k], device_id=(px, py, c), device_id_type=MESH)

    def start(ins, outs, sems):
        for cp in copies(ins, outs, sems):
            cp.start()

    def end(ins, outs, sems):
        for cp in copies(ins, outs, sems):
            cp.wait()

    return Rider(ins=list(srcs), out_shapes=[jax.ShapeDtypeStruct(a.shape, a.dtype) for a in srcs], aliases={},
                 sems=[pltpu.SemaphoreType.DMA((nsem,))] * 2, start=start, mid=None, end=end)


def _ride_alone(rider, name):
    n_in, n_out = len(rider.ins), len(rider.out_shapes)

    def body(*refs):
        ins, outs, sems = refs[:n_in], refs[n_in:n_in + n_out], refs[n_in + n_out:]
        rider.start(ins, outs, sems)
        if rider.mid is not None:
            rider.mid(ins, outs, sems)
        rider.end(ins, outs, sems)

    any_spec = pl.BlockSpec(memory_space=pl.ANY)
    outs = pl.pallas_call(
        body, name=name, out_shape=list(rider.out_shapes), in_specs=[any_spec] * n_in, out_specs=[any_spec] * n_out,
        scratch_shapes=list(rider.sems), input_output_aliases=dict(rider.aliases),
        compiler_params=pltpu.CompilerParams(has_side_effects=True),
    )(*rider.ins)
    return list(outs)


def _swap_cores_inplace(bufs, name):
    n = len(bufs)
    nck = [_n_chunks(a.shape[2], a.shape[3] * a.dtype.itemsize) for a in bufs]
    base = [sum(a.shape[0] * k for a, k in zip(bufs[:i], nck[:i])) for i in range(n)]
    nsem = sum(a.shape[0] * k for a, k in zip(bufs, nck))

    def body(*refs):
        outs = refs[n:2 * n]
        send, recv = refs[2 * n:]
        x, y, c = lax.axis_index("x"), lax.axis_index("y"), lax.axis_index("c")

        def copies(core):
            for i in range(n):
                ch = bufs[i].shape[2] // nck[i]
                for p in range(bufs[i].shape[0]):
                    for j in range(nck[i]):
                        k = base[i] + p * nck[i] + j
                        ref = outs[i].at[p, core, pl.ds(j * ch, ch)]
                        yield pltpu.make_async_remote_copy(src_ref=ref, dst_ref=ref, send_sem=send.at[k], recv_sem=recv.at[k],
                                                           device_id=(x, y, 1 - c), device_id_type=MESH)

        for cp in copies(c):
            cp.start()
        for cp in copies(c):
            cp.wait_send()
        for cp in copies(1 - c):
            cp.wait_recv()

    any_spec = pl.BlockSpec(memory_space=pl.ANY)
    outs = pl.pallas_call(
        body, name=name, out_shape=[jax.ShapeDtypeStruct(a.shape, a.dtype) for a in bufs], in_specs=[any_spec] * n,
        out_specs=[any_spec] * n, scratch_shapes=[pltpu.SemaphoreType.DMA((nsem,))] * 2,
        input_output_aliases={i: i for i in range(n)}, compiler_params=pltpu.CompilerParams(has_side_effects=True),
    )(*bufs)
    return list(outs)


def _place_cast(ws, row0, rows, slot, slots, name):
    n = len(ws)
    C = ws[0].shape[1]
    tr = _row_block(rows, C)
    blk0 = row0 // tr

    def body(slot_ref, *refs):
        del slot_ref
        for w_ref, o_ref in zip(refs[:n], refs[n:]):
            o_ref[...] = w_ref[...].astype(BF16)

    return pl.pallas_call(
        body, name=name,
        grid_spec=pltpu.PrefetchScalarGridSpec(
            num_scalar_prefetch=1, grid=(rows // tr,), in_specs=[pl.BlockSpec((tr, C), lambda i, sr: (blk0 + i, 0))] * n,
            out_specs=[pl.BlockSpec((None, tr, C), lambda i, sr: (sr[0], i, 0))] * n),
        out_shape=[jax.ShapeDtypeStruct((slots, rows, C), BF16)] * n,
        compiler_params=_cparams("parallel"),
    )(slot.reshape(1).astype(jnp.int32), *ws)


def _swap_halves(arrs, name):
    n = len(arrs)
    S = arrs[0].shape[0]
    halves = [a.shape[1] // 2 for a in arrs]
    nck = [_n_chunks(h, a.shape[2] * a.dtype.itemsize) for h, a in zip(halves, arrs)]
    base = [S * sum(nck[:i]) for i in range(n)]
    nsem = S * sum(nck)

    def body(*refs):
        ins, outs = refs[:n], refs[n:2 * n]
        send, recv = refs[2 * n:]
        x, y, c = lax.axis_index("x"), lax.axis_index("y"), lax.axis_index("c")
        copies = []
        for i in range(n):
            ch = halves[i] // nck[i]
            for s in range(S):
                for j in range(nck[i]):
                    k = base[i] + s * nck[i] + j
                    src = ins[i].at[s, pl.ds(pl.multiple_of((1 - c) * halves[i] + j * ch, 16), ch)]
                    cp = pltpu.make_async_remote_copy(src_ref=src, dst_ref=outs[i].at[s, pl.ds(j * ch, ch)], send_sem=send.at[k],
                                                      recv_sem=recv.at[k], device_id=(x, y, 1 - c), device_id_type=MESH)
                    cp.start()
                    copies.append(cp)
        for cp in copies:
            cp.wait()

    any_spec = pl.BlockSpec(memory_space=pl.ANY)
    outs = pl.pallas_call(
        body, name=name, out_shape=[jax.ShapeDtypeStruct((S, h, a.shape[2]), a.dtype) for h, a in zip(halves, arrs)],
        in_specs=[any_spec] * n, out_specs=[any_spec] * n,
        scratch_shapes=[pltpu.SemaphoreType.DMA((nsem,))] * 2,
        compiler_params=pltpu.CompilerParams(has_side_effects=True),
    )(*arrs)
    return list(outs)


def _pair_sum(gs, rs, core, name):
    n = len(gs)
    S, rows, C = gs[0].shape
    half = rows // 2
    tr = _row_block(half, C)
    nb = half // tr

    def body(core_ref, *refs):
        del core_ref
        for g_ref, r_ref, o_ref in zip(refs[:n], refs[n:2 * n], refs[2 * n:]):
            o_ref[...] = (g_ref[...].astype(F32) + r_ref[...].astype(F32)).astype(BF16)

    blk = pl.BlockSpec((None, tr, C), lambda s, i, cr: (s, i, 0))
    mine = pl.BlockSpec((None, tr, C), lambda s, i, cr: (s, cr[0] * nb + i, 0))
    return pl.pallas_call(
        body, name=name,
        grid_spec=pltpu.PrefetchScalarGridSpec(num_scalar_prefetch=1, grid=(S, nb), in_specs=[mine] * n + [blk] * n,
                                               out_specs=[blk] * n),
        out_shape=[jax.ShapeDtypeStruct((S, half, C), BF16)] * n,
        compiler_params=_cparams("parallel", "parallel"),
    )(core.reshape(1).astype(jnp.int32), *gs, *rs)


FFN_FWD_PARTS = 1
FFN_BWD_PARTS = 2


def _row_parts(rows, parts):
    parts = parts if rows % (16 * parts) == 0 else 1
    return [pl.ds(p * (rows // parts), rows // parts) for p in range(parts)]


def _ffn_fwd(h, mod, g, w1, w3, w2, k, dm, name, rider=None):
    T, D = h.shape
    S, F = w1.shape[0], w1.shape[-2]
    tm = dm.tm
    r0 = 6 if k else 0
    grp = _grp(dm)

    def body(h_ref, mod_ref, g_ref, w1_ref, w3_ref, w2_ref, ho_ref, a_ref, b_ref, hn_ref, y_ref, hn_s, acc):
        s = pl.program_id(1)

        @pl.when(s == 0)
        def _():
            hn = _pre(h_ref[...], g_ref[...], mod_ref[0, r0:r0 + 1, :], mod_ref[0, r0 + 1:r0 + 2, :]).astype(BF16)
            hn_s[...] = hn
            hn_ref[...] = hn
            acc[...] = jnp.zeros_like(acc)

        for rows in _row_parts(tm, FFN_FWD_PARTS):
            hn = hn_s[rows, :]
            a = _dot_nt(hn, w1_ref[...])
            b = _dot_nt(hn, w3_ref[...])
            a_ref[0, rows, :] = a.astype(BF16)
            b_ref[0, rows, :] = b.astype(BF16)
            sw = (a * jax.nn.sigmoid(a) * b).astype(BF16)
            acc[rows, :] += _dot(sw, w2_ref[...])

        @pl.when(s == S - 1)
        def _():
            y = acc[...]
            y_ref[...] = y.astype(BF16)
            ho_ref[...] = h_ref[...] + 0.5 * mod_ref[0, r0 + 2:r0 + 3, :] * y

    row = pl.BlockSpec((tm, D), lambda i, s: (i, 0))
    wrow = pl.BlockSpec((None, F, D), lambda i, s: (s, 0, 0))
    ab = pl.BlockSpec((1, tm, F), lambda i, s: (s, i, 0))
    return _hosted(
        body, rider, name=name, grid=(T // tm, S),
        in_specs=[row, pl.BlockSpec((1, 9, D), lambda i, s: (grp(i), 0, 0)), pl.BlockSpec((1, D), lambda i, s: (0, 0)),
                  wrow, wrow, wrow],
        out_specs=[row, ab, ab, row, row],
        out_shape=[jax.ShapeDtypeStruct((T, D), F32), jax.ShapeDtypeStruct((S, T, F), BF16),
                   jax.ShapeDtypeStruct((S, T, F), BF16), jax.ShapeDtypeStruct((T, D), BF16),
                   jax.ShapeDtypeStruct((T, D), BF16)],
        scratch_shapes=[pltpu.VMEM((tm, D), BF16), pltpu.VMEM((tm, D), F32)],
        sem=("parallel", "arbitrary"), args=(h, mod, g, w1, w3, w2))


def _ffn_bwd(dh, h, mod, g, y, a, b, w1, w3, w2, k, dm, name, rider=None):
    T, D = h.shape
    S, F = w1.shape[0], w1.shape[-2]
    tm = dm.tm
    r0 = 6 if k else 0
    grp = _grp(dm)

    def body(dh_ref, h_ref, mod_ref, g_ref, y_ref, a_ref, b_ref, w1_ref, w3_ref, w2_ref,
             dho_ref, da_ref, db_ref, sw_ref, dy_ref, part_ref, dy_s, acc):
        s = pl.program_id(1)

        @pl.when(s == 0)
        def _():
            dy = (0.5 * mod_ref[0, r0 + 2:r0 + 3, :] * dh_ref[...]).astype(BF16)
            dy_s[...] = dy
            dy_ref[...] = dy
            acc[...] = jnp.zeros_like(acc)

        for rows in _row_parts(tm, FFN_BWD_PARTS):
            ds = _dot_nt(dy_s[rows, :], w2_ref[...]).astype(BF16)
            av = a_ref[0, rows, :]
            bv = b_ref[0, rows, :]
            sig = jax.nn.sigmoid(av)
            sil = av * sig
            sw_ref[0, rows, :] = sil * bv
            db = ds * sil
            da = ds * bv * (sig + sil * (1.0 - sig))
            da_ref[0, rows, :] = da
            db_ref[0, rows, :] = db
            acc[rows, :] += _dot(da, w1_ref[...]) + _dot(db, w3_ref[...])

        @pl.when(s == S - 1)
        def _():
            dhv = dh_ref[...]
            dhb, dshift, dscale, dg = _pre_bwd(acc[...], h_ref[...], g_ref[...], mod_ref[0, r0 + 1:r0 + 2, :])
            dho_ref[...] = dhv + dhb
            dgate = 0.5 * jnp.sum(dhv * y_ref[...].astype(F32), axis=0, keepdims=True)
            _write_part(part_ref, dshift, dscale, dgate, dg)

    row = pl.BlockSpec((tm, D), lambda i, s: (i, 0))
    wrow = pl.BlockSpec((None, F, D), lambda i, s: (s, 0, 0))
    ab = pl.BlockSpec((1, tm, F), lambda i, s: (s, i, 0))
    stf = jax.ShapeDtypeStruct((S, T, F), BF16)
    return _hosted(
        body, rider, name=name, grid=(T // tm, S),
        in_specs=[row, row, pl.BlockSpec((1, 9, D), lambda i, s: (grp(i), 0, 0)), pl.BlockSpec((1, D), lambda i, s: (0, 0)),
                  row, ab, ab, wrow, wrow, wrow],
        out_specs=[row, ab, ab, ab, row, pl.BlockSpec((1, 8, D), lambda i, s: (i, 0, 0))],
        out_shape=[jax.ShapeDtypeStruct((T, D), F32), stf, stf, stf, jax.ShapeDtypeStruct((T, D), BF16),
                   jax.ShapeDtypeStruct((T // tm, 8, D), F32)],
        scratch_shapes=[pltpu.VMEM((tm, D), BF16), pltpu.VMEM((tm, D), F32)],
        sem=("parallel", "arbitrary"), args=(dh, h, mod, g, y, a, b, w1, w3, w2))


def _mm_tn(a, b, a_spec, b_spec, out_shape, out_spec, grid, name):
    nk = grid[-1]
    kax = len(grid) - 1
    blk = tuple(d for d in out_spec.block_shape if d is not None)

    def body(a_ref, b_ref, o_ref, acc):
        kk = pl.program_id(kax)

        @pl.when(kk == 0)
        def _():
            acc[...] = jnp.zeros_like(acc)

        av = a_ref[...].astype(BF16)
        if len(b_ref.shape) == 3:
            for p in range(b_ref.shape[0]):
                acc[p] += _dot_tn(av, b_ref[p].astype(BF16))
        else:
            acc[...] += _dot_tn(av, b_ref[...].astype(BF16))

        @pl.when(kk == nk - 1)
        def _():
            o_ref[...] = acc[...].astype(o_ref.dtype)

    return pl.pallas_call(
        body, name=name, grid=grid,
        in_specs=[a_spec, b_spec], out_specs=out_spec, out_shape=jax.ShapeDtypeStruct(out_shape, BF16),
        scratch_shapes=[pltpu.VMEM(blk, F32)],
        compiler_params=_cparams(*(["parallel"] * kax + ["arbitrary"])),
    )(a, b)


def _sc_w_in_specs(D, j):
    wq = D // N_CHIPS

    def spec(piece):
        col = lambda q: piece * N_CHIPS + q
        return pl.BlockSpec((None, None, D, wq), lambda i, q: (col(q) // 3, j, 0, col(q) % 3))

    return [spec(0), spec(1), spec(2)]


def _sc_in_fwd(h, mod, g, w_in, j, dm, name, rider=None):
    T, D = h.shape
    tm = dm.tm
    wq = D // N_CHIPS
    grp = _grp(dm)

    def body(h_ref, mod_ref, g_ref, wb_ref, wc_ref, wu_ref, p_ref, hn_ref, hn_s):
        @pl.when(pl.program_id(1) == 0)
        def _():
            hn = _pre(h_ref[...], g_ref[...], mod_ref[0, 3:4, :], mod_ref[0, 4:5, :]).astype(BF16)
            hn_s[...] = hn
            hn_ref[...] = hn

        for piece, w_ref in enumerate((wb_ref, wc_ref, wu_ref)):
            p_ref[piece] = _dot(hn_s[...], w_ref[...])

    row = pl.BlockSpec((tm, D), lambda i, q: (i, 0))
    return _hosted(
        body, rider, name=name, grid=(T // tm, N_CHIPS),
        in_specs=[row, pl.BlockSpec((1, 9, D), lambda i, q: (grp(i), 0, 0)), pl.BlockSpec((1, D), lambda i, q: (0, 0))]
        + _sc_w_in_specs(D, j),
        out_specs=[pl.BlockSpec((3, tm, wq), lambda i, q: (0, i, q)), row],
        out_shape=[jax.ShapeDtypeStruct((3, T, D), F32), jax.ShapeDtypeStruct((T, D), BF16)],
        scratch_shapes=[pltpu.VMEM((tm, D), BF16)],
        sem=("parallel", "arbitrary"), args=(h, mod, g, w_in, w_in, w_in))


def _conv_cols(dm):
    return 256 if dm.D % 256 == 0 else 128


def _seg_masks(r, dm):
    bn = dm.B * dm.N
    lat = r < bn
    off = jnp.where(lat, lax.rem(r, dm.N), lax.rem(r - bn, dm.CTX))
    seg = jnp.where(lat, dm.N, dm.CTX)
    inside = (r >= 0) & (r < dm.T)
    return ((off != 0) & inside).astype(F32), ((off != seg - 1) & inside).astype(F32)


def _conv_specs(dm):
    tb, cb, nr8 = dm.tm, _conv_cols(dm), dm.T // 8
    prev8 = lambda c, i: jnp.maximum(i * (tb // 8) - 1, 0)
    next8 = lambda c, i: jnp.minimum((i + 1) * (tb // 8), nr8 - 1)
    return dict(
        tb=tb, cb=cb,
        p=pl.BlockSpec((3, tb, cb), lambda c, i: (0, i, c)),
        p_prev=pl.BlockSpec((3, 8, cb), lambda c, i: (0, prev8(c, i), c)),
        p_next=pl.BlockSpec((3, 8, cb), lambda c, i: (0, next8(c, i), c)),
        row=pl.BlockSpec((tb, cb), lambda c, i: (i, c)),
        row_prev=pl.BlockSpec((8, cb), lambda c, i: (prev8(c, i), c)),
        row_next=pl.BlockSpec((8, cb), lambda c, i: (next8(c, i), c)),
        w=pl.BlockSpec((3, cb), lambda c, i: (0, c)),
    )


def _shift_rows(x, before, after, tb):
    rid = lax.broadcasted_iota(jnp.int32, x.shape, 0)
    down = jnp.where(rid == 0, before, pltpu.roll(x, 1, 0))
    up = jnp.where(rid == tb - 1, after, pltpu.roll(x, tb - 1, 0))
    return down, up


def _conv_fwd(p, wc, dm, name):
    T, D = dm.T, dm.D
    sp = _conv_specs(dm)
    tb, cb = sp["tb"], sp["cb"]

    def body(p_ref, pp_ref, pn_ref, w_ref, z_ref):
        r = pl.program_id(1) * tb + lax.broadcasted_iota(jnp.int32, (tb, cb), 0)
        mp, mn = _seg_masks(r, dm)
        cu = p_ref[1] * p_ref[2]
        prev, nxt = _shift_rows(cu, pp_ref[1, 7:8, :] * pp_ref[2, 7:8, :], pn_ref[1, 0:1, :] * pn_ref[2, 0:1, :], tb)
        conv = w_ref[0:1, :] * (prev * mp) + w_ref[1:2, :] * cu + w_ref[2:3, :] * (nxt * mn)
        z_ref[...] = (p_ref[0] * conv).astype(BF16)

    return pl.pallas_call(
        body, name=name, grid=(D // cb, T // tb),
        in_specs=[sp["p"], sp["p_prev"], sp["p_next"], sp["w"]], out_specs=sp["row"],
        out_shape=jax.ShapeDtypeStruct((T, D), BF16),
        compiler_params=_cparams("parallel", "parallel"),
    )(p, p, p, wc)


def _conv_bwd(dz, p, wc, dm, name):
    T, D = dm.T, dm.D
    sp = _conv_specs(dm)
    tb, cb = sp["tb"], sp["cb"]

    def body(dz_ref, dzp_ref, dzn_ref, p_ref, pp_ref, pn_ref, w_ref, dp_ref, dw_ref):
        i = pl.program_id(1)
        r = i * tb + lax.broadcasted_iota(jnp.int32, (tb, cb), 0)
        mp, mn = _seg_masks(r, dm)
        rb = i * tb + lax.broadcasted_iota(jnp.int32, (1, cb), 0)
        _, mn_before = _seg_masks(rb - 1, dm)
        mp_after, _ = _seg_masks(rb + tb, dm)
        bg, cg, u = p_ref[0], p_ref[1], p_ref[2]
        cu = cg * u
        prev, nxt = _shift_rows(cu, pp_ref[1, 7:8, :] * pp_ref[2, 7:8, :], pn_ref[1, 0:1, :] * pn_ref[2, 0:1, :], tb)
        prev = prev * mp
        nxt = nxt * mn
        w0, w1, w2 = w_ref[0:1, :], w_ref[1:2, :], w_ref[2:3, :]
        conv = w0 * prev + w1 * cu + w2 * nxt
        dz = dz_ref[...]
        dp_ref[0] = dz * conv
        dconv = dz * bg

        @pl.when(i == 0)
        def _():
            dw_ref[...] = jnp.zeros_like(dw_ref)

        dw_ref[0:1, :] += jnp.sum(dconv * prev, axis=0, keepdims=True)
        dw_ref[1:2, :] += jnp.sum(dconv * cu, axis=0, keepdims=True)
        dw_ref[2:3, :] += jnp.sum(dconv * nxt, axis=0, keepdims=True)
        dconv_before = dzp_ref[7:8, :] * pp_ref[0, 7:8, :] * mn_before
        dconv_after = dzn_ref[0:1, :] * pn_ref[0, 0:1, :] * mp_after
        from_prev, _ = _shift_rows(dconv * mn, dconv_before, dconv_after, tb)
        _, from_next = _shift_rows(dconv * mp, dconv_before, dconv_after, tb)
        dcu = w1 * dconv + w0 * from_next + w2 * from_prev
        dp_ref[1] = dcu * u
        dp_ref[2] = dcu * cg

    return pl.pallas_call(
        body, name=name, grid=(D // cb, T // tb),
        in_specs=[sp["row"], sp["row_prev"], sp["row_next"], sp["p"], sp["p_prev"], sp["p_next"], sp["w"]],
        out_specs=[sp["p"], sp["w"]],
        out_shape=[jax.ShapeDtypeStruct((3, T, D), F32), jax.ShapeDtypeStruct((3, D), F32)],
        compiler_params=_cparams("parallel", "arbitrary"),
    )(dz, dz, dz, p, p, p, wc)


def _out_fwd(z, w, h, mod, j, dm, name):
    T, D = h.shape
    K = z.shape[1]
    tm = dm.tm
    grp = _grp(dm)

    def body(z_ref, w_ref, h_ref, mod_ref, ho_ref, y_ref):
        y = _dot(z_ref[...], w_ref[...])
        y_ref[...] = y.astype(BF16)
        ho_ref[...] = h_ref[...] + mod_ref[0, 5:6, :] * y

    row = pl.BlockSpec((tm, D), lambda i: (i, 0))
    return pl.pallas_call(
        body, name=name, grid=(T // tm,),
        in_specs=[pl.BlockSpec((tm, K), lambda i: (i, 0)), pl.BlockSpec((None, K, D), lambda i: (j, 0, 0)), row,
                  pl.BlockSpec((1, 9, D), lambda i: (grp(i), 0, 0))],
        out_specs=[row, row],
        out_shape=[jax.ShapeDtypeStruct((T, D), F32), jax.ShapeDtypeStruct((T, D), BF16)],
        compiler_params=_cparams("parallel"),
    )(z, w, h, mod)


def _out_bwd(dh, y, w, mod, j, dm, name):
    T, D = dh.shape
    K = w.shape[1]
    tm = dm.tm
    grp = _grp(dm)

    def body(dh_ref, y_ref, w_ref, mod_ref, dy_ref, dz_ref, part_ref):
        dhv = dh_ref[...]
        dy = (mod_ref[0, 5:6, :] * dhv).astype(BF16)
        dy_ref[...] = dy
        dz_ref[...] = _dot_nt(dy, w_ref[...])
        _write_part(part_ref, dgate=jnp.sum(dhv * y_ref[...].astype(F32), axis=0, keepdims=True))

    row = pl.BlockSpec((tm, D), lambda i: (i, 0))
    return pl.pallas_call(
        body, name=name, grid=(T // tm,),
        in_specs=[row, row, pl.BlockSpec((None, K, D), lambda i: (j, 0, 0)), pl.BlockSpec((1, 9, D), lambda i: (grp(i), 0, 0))],
        out_specs=[row, pl.BlockSpec((tm, K), lambda i: (i, 0)), pl.BlockSpec((1, 8, D), lambda i: (i, 0, 0))],
        out_shape=[jax.ShapeDtypeStruct((T, D), BF16), jax.ShapeDtypeStruct((T, K), F32),
                   jax.ShapeDtypeStruct((T // tm, 8, D), F32)],
        compiler_params=_cparams("parallel"),
    )(dh, y, w, mod)


def _sc_in_bwd(dh, dp, h, mod, g, w_in, j, dm, name):
    T, D = h.shape
    tm = dm.tm
    wq = D // N_CHIPS
    nq = N_CHIPS
    grp = _grp(dm)

    def body(dh_ref, dp_ref, h_ref, mod_ref, g_ref, wb_ref, wc_ref, wu_ref, dho_ref, part_ref, acc):
        q = pl.program_id(1)

        @pl.when(q == 0)
        def _():
            acc[...] = jnp.zeros_like(acc)

        acc[...] += sum(_dot_nt(dp_ref[piece].astype(BF16), w_ref[...]) for piece, w_ref in enumerate((wb_ref, wc_ref, wu_ref)))

        @pl.when(q == nq - 1)
        def _():
            dhb, dshift, dscale, dg = _pre_bwd(acc[...], h_ref[...], g_ref[...], mod_ref[0, 4:5, :])
            dho_ref[...] = dh_ref[...] + dhb
            _write_part(part_ref, dshift, dscale, None, dg)

    row = pl.BlockSpec((tm, D), lambda i, q: (i, 0))
    return pl.pallas_call(
        body, name=name, grid=(T // tm, nq),
        in_specs=[row, pl.BlockSpec((3, tm, wq), lambda i, q: (0, i, q)), row,
                  pl.BlockSpec((1, 9, D), lambda i, q: (grp(i), 0, 0)), pl.BlockSpec((1, D), lambda i, q: (0, 0))]
        + _sc_w_in_specs(D, j),
        out_specs=[row, pl.BlockSpec((1, 8, D), lambda i, q: (i, 0, 0))],
        out_shape=[jax.ShapeDtypeStruct((T, D), F32), jax.ShapeDtypeStruct((T // tm, 8, D), F32)],
        scratch_shapes=[pltpu.VMEM((tm, D), F32)],
        compiler_params=_cparams("parallel", "arbitrary"),
    )(dh, dp, h, mod, g, w_in, w_in, w_in)


def _rope(t, c, s1, s2):
    return t * c + pltpu.roll(t, HEAD_PAD - 16, 1) * s1 + pltpu.roll(t, 16, 1) * s2


def _rope_t(dy, c, s1, s2):
    return dy * c + pltpu.roll(dy * s1, 16, 1) + pltpu.roll(dy * s2, HEAD_PAD - 16, 1)


def _mla_heads_fwd(z, g_ref, wuq_ref, wukv_ref):
    cq, ckv, krp = z[:, :Q_LORA], z[:, Q_LORA:Q_LORA + KV_LORA], z[:, Q_LORA + KV_LORA:]
    cqh, rq = _rms(cq, Q_LORA)
    ckvh, rkv = _rms(ckv, KV_LORA)
    cqn = (cqh * g_ref[0:1, :]).astype(BF16)
    ckvn = (ckvh * g_ref[1:2, :KV_LORA]).astype(BF16)
    qraw = _dot(cqn, wuq_ref[...])
    kvraw = _dot(ckvn, wukv_ref[...])
    return dict(krp=krp, cqh=cqh, rq=rq, ckvh=ckvh, rkv=rkv, cqn=cqn, ckvn=ckvn, qraw=qraw, kvraw=kvraw)


def _mla_proj_fwd(h, mod, g, gains, tabs, w_a, w_uq, w_ukv, j, dm, name):
    T, D = h.shape
    tm = min(dm.tm, 256)
    grp = lambda i: jnp.minimum(i // (dm.N // tm), dm.B)
    HP = HEAD_PAD

    def body(h_ref, mod_ref, g_ref, gn_ref, tab_ref, wa_ref, wuq_ref, wukv_ref, hn_ref, q_ref, k_ref, v_ref):
        hn = _pre(h_ref[...], g_ref[...], mod_ref[0, 3:4, :], mod_ref[0, 4:5, :]).astype(BF16)
        hn_ref[...] = hn
        f = _mla_heads_fwd(_dot(hn, wa_ref[...]), gn_ref, wuq_ref, wukv_ref)
        c, s1, s2 = tab_ref[0], tab_ref[1], tab_ref[2]
        for hd in range(HEADS):
            qh, _ = _rms(f["qraw"][:, hd * HP:(hd + 1) * HP], QK_HEAD)
            q_ref[:, hd * HP:(hd + 1) * HP] = (_rope(qh * gn_ref[2:3, :], c, s1, s2) * QK_SCALE).astype(BF16)
            kpre = jnp.concatenate([f["kvraw"][:, hd * HP:hd * HP + QK_NOPE], f["krp"]], axis=1)
            kh, _ = _rms(kpre, QK_HEAD)
            k_ref[:, hd * HP:(hd + 1) * HP] = _rope(kh * gn_ref[3:4, :], c, s1, s2).astype(BF16)
            v_ref[:, hd * V_HEAD:(hd + 1) * V_HEAD] = f["kvraw"][:, hd * HP + QK_NOPE:(hd + 1) * HP].astype(BF16)

    row = pl.BlockSpec((tm, D), lambda i: (i, 0))
    HQ = HEADS * HP
    return pl.pallas_call(
        body, name=name, grid=(T // tm,),
        in_specs=[row, pl.BlockSpec((1, 9, D), lambda i: (grp(i), 0, 0)), pl.BlockSpec((1, D), lambda i: (0, 0)),
                  pl.BlockSpec((None, 8, HP), lambda i: (j, 0, 0)), pl.BlockSpec((3, tm, HP), lambda i: (0, i, 0)),
                  pl.BlockSpec((None, D, 512), lambda i: (j, 0, 0)), pl.BlockSpec((None, Q_LORA, HQ), lambda i: (j, 0, 0)),
                  pl.BlockSpec((None, KV_LORA, HQ), lambda i: (j, 0, 0))],
        out_specs=[row, pl.BlockSpec((tm, HQ), lambda i: (i, 0)), pl.BlockSpec((tm, HQ), lambda i: (i, 0)),
                   pl.BlockSpec((tm, HEADS * V_HEAD), lambda i: (i, 0))],
        out_shape=[jax.ShapeDtypeStruct((T, D), BF16), jax.ShapeDtypeStruct((T, HQ), BF16),
                   jax.ShapeDtypeStruct((T, HQ), BF16), jax.ShapeDtypeStruct((T, HEADS * V_HEAD), BF16)],
        compiler_params=_cparams("parallel"),
    )(h, mod, g, gains, tabs, w_a, w_uq, w_ukv)


def _mla_proj_bwd(dh, dq, dkl, dkc, dvl, dvc, h, mod, g, gains, tabs, w_a, w_uq, w_ukv, j, dm, name):
    T, D = h.shape
    tm = min(dm.tm, 256)
    nblk = T // tm
    grp = lambda i: jnp.minimum(i // (dm.N // tm), dm.B)
    HP = HEAD_PAD
    HQ = HEADS * HP

    nlat = dm.B * dm.N // tm

    def body(dh_ref, dq_ref, dkl_ref, dkc_ref, dvl_ref, dvc_ref, h_ref, mod_ref, g_ref, gn_ref, tab_ref, wa_ref, wuq_ref, wukv_ref,
             dho_ref, part_ref, gwa_ref, gwuq_ref, gwukv_ref, dgn_ref, dqraw_s, dkvraw_s):
        i = pl.program_id(0)
        pick = lambda lat_ref, ctx_ref, cols: jnp.where(i < nlat, lat_ref[:, cols], ctx_ref[:, cols])

        @pl.when(i == 0)
        def _():
            gwa_ref[...] = jnp.zeros_like(gwa_ref)
            gwuq_ref[...] = jnp.zeros_like(gwuq_ref)
            gwukv_ref[...] = jnp.zeros_like(gwukv_ref)
            dgn_ref[...] = jnp.zeros_like(dgn_ref)

        hv = h_ref[...]
        hn = _pre(hv, g_ref[...], mod_ref[0, 3:4, :], mod_ref[0, 4:5, :]).astype(BF16)
        f = _mla_heads_fwd(_dot(hn, wa_ref[...]), gn_ref, wuq_ref, wukv_ref)
        c, s1, s2 = tab_ref[0], tab_ref[1], tab_ref[2]
        gq, gk = gn_ref[2:3, :], gn_ref[3:4, :]
        dgq = jnp.zeros((1, HP), F32)
        dgk = jnp.zeros((1, HP), F32)
        dkrp = jnp.zeros((tm, HP - QK_NOPE), F32)
        for hd in range(HEADS):
            qh, rq = _rms(f["qraw"][:, hd * HP:(hd + 1) * HP], QK_HEAD)
            dqn = _rope_t(dq_ref[:, hd * HP:(hd + 1) * HP] * QK_SCALE, c, s1, s2)
            dgq = dgq + jnp.sum(dqn * qh, axis=0, keepdims=True)
            dqraw_s[:, hd * HP:(hd + 1) * HP] = _rms_bwd(dqn * gq, qh, rq, QK_HEAD)
            kpre = jnp.concatenate([f["kvraw"][:, hd * HP:hd * HP + QK_NOPE], f["krp"]], axis=1)
            kh, rk = _rms(kpre, QK_HEAD)
            dkn = _rope_t(pick(dkl_ref, dkc_ref, slice(hd * HP, (hd + 1) * HP)), c, s1, s2)
            dgk = dgk + jnp.sum(dkn * kh, axis=0, keepdims=True)
            dkpre = _rms_bwd(dkn * gk, kh, rk, QK_HEAD)
            dkvraw_s[:, hd * HP:hd * HP + QK_NOPE] = dkpre[:, :QK_NOPE]
            dkrp = dkrp + dkpre[:, QK_NOPE:]
            dkvraw_s[:, hd * HP + QK_NOPE:(hd + 1) * HP] = pick(dvl_ref, dvc_ref, slice(hd * V_HEAD, (hd + 1) * V_HEAD))
        dqraw = dqraw_s[...].astype(BF16)
        dkvraw = dkvraw_s[...].astype(BF16)
        gwuq_ref[...] += _dot_tn(f["cqn"], dqraw)
        gwukv_ref[...] += _dot_tn(f["ckvn"], dkvraw)
        dcqn = _dot_nt(dqraw, wuq_ref[...])
        dckvn = _dot_nt(dkvraw, wukv_ref[...])
        dgqa = jnp.sum(dcqn * f["cqh"], axis=0, keepdims=True)
        dgkva = jnp.sum(dckvn * f["ckvh"], axis=0, keepdims=True)
        dcq = _rms_bwd(dcqn * gn_ref[0:1, :], f["cqh"], f["rq"], Q_LORA)
        dckv = _rms_bwd(dckvn * gn_ref[1:2, :KV_LORA], f["ckvh"], f["rkv"], KV_LORA)
        dz = jnp.concatenate([dcq, dckv, dkrp], axis=1).astype(BF16)
        gwa_ref[...] += _dot_tn(hn, dz)
        dhn = _dot_nt(dz, wa_ref[...])
        dhb, dshift, dscale, dg = _pre_bwd(dhn, hv, g_ref[...], mod_ref[0, 4:5, :])
        dho_ref[...] = dh_ref[...] + dhb
        _write_part(part_ref, dshift, dscale, None, dg)
        dgn_ref[0:1, :] += dgqa
        dgn_ref[1:2, :KV_LORA] += dgkva
        dgn_ref[2:3, :] += dgq
        dgn_ref[3:4, :] += dgk

    row = pl.BlockSpec((tm, D), lambda i: (i, 0))
    wide = pl.BlockSpec((tm, HQ), lambda i: (i, 0))
    const2 = lambda i: (0, 0)
    return pl.pallas_call(
        body, name=name, grid=(nblk,),
        in_specs=[row, wide, pl.BlockSpec((tm, HQ), lambda i: (jnp.minimum(i, nlat - 1), 0)),
                  pl.BlockSpec((tm, HQ), lambda i: (jnp.maximum(i - nlat, 0), 0)),
                  pl.BlockSpec((tm, HEADS * V_HEAD), lambda i: (jnp.minimum(i, nlat - 1), 0)),
                  pl.BlockSpec((tm, HEADS * V_HEAD), lambda i: (jnp.maximum(i - nlat, 0), 0)), row,
                  pl.BlockSpec((1, 9, D), lambda i: (grp(i), 0, 0)), pl.BlockSpec((1, D), const2),
                  pl.BlockSpec((None, 8, HP), lambda i: (j, 0, 0)), pl.BlockSpec((3, tm, HP), lambda i: (0, i, 0)),
                  pl.BlockSpec((None, D, 512), lambda i: (j, 0, 0)), pl.BlockSpec((None, Q_LORA, HQ), lambda i: (j, 0, 0)),
                  pl.BlockSpec((None, KV_LORA, HQ), lambda i: (j, 0, 0))],
        out_specs=[row, pl.BlockSpec((1, 8, D), lambda i: (i, 0, 0)), pl.BlockSpec((D, 512), const2),
                   pl.BlockSpec((Q_LORA, HQ), const2), pl.BlockSpec((KV_LORA, HQ), const2), pl.BlockSpec((8, HP), const2)],
        out_shape=[jax.ShapeDtypeStruct((T, D), F32), jax.ShapeDtypeStruct((nblk, 8, D), F32),
                   jax.ShapeDtypeStruct((D, 512), F32), jax.ShapeDtypeStruct((Q_LORA, HQ), F32),
                   jax.ShapeDtypeStruct((KV_LORA, HQ), F32), jax.ShapeDtypeStruct((8, HP), F32)],
        scratch_shapes=[pltpu.VMEM((tm, HQ), F32), pltpu.VMEM((tm, HQ), F32)],
        compiler_params=_cparams("arbitrary"),
    )(dh, dq, dkl, dkc, dvl, dvc, h, mod, g, gains, tabs, w_a, w_uq, w_ukv)


def _attn_specs(dm):
    tq = dm.CTX
    nq = dm.N // tq
    cblk0 = dm.B * nq
    HP = HEAD_PAD
    qrow = lambda b, i: jnp.where(i < nq, b * nq + i, cblk0 + b)
    return dict(
        tq=tq, nq=nq,
        q=pl.BlockSpec((tq, HP), lambda b, hd, i: (qrow(b, i), hd)),
        k_lat=pl.BlockSpec((dm.N, HP), lambda b, hd, i: (b, hd)),
        k_ctx=pl.BlockSpec((tq, HP), lambda b, hd, i: (cblk0 + b, hd)),
        v_lat=pl.BlockSpec((dm.N, V_HEAD), lambda b, hd, i: (b, hd)),
        v_ctx=pl.BlockSpec((tq, V_HEAD), lambda b, hd, i: (cblk0 + b, hd)),
        o=pl.BlockSpec((tq, V_HEAD), lambda b, hd, i: (qrow(b, i), hd)),
    )


def _attn_exp(q, keys, first_off=None):
    s = [_dot_nt(q, kk) for kk in keys]
    if first_off is not None:
        s[0] = s[0] + first_off
    m = functools.reduce(jnp.maximum, [jnp.max(x, axis=-1, keepdims=True) for x in s])
    e = [jnp.exp(x - m) for x in s]
    return e, 1.0 / sum(jnp.sum(x, axis=-1, keepdims=True) for x in e)


def _attn_fwd(q, k, v, dm, name, rider=None):
    T = dm.T
    sp = _attn_specs(dm)
    nq = sp["nq"]

    def body(q_ref, kl_ref, kc_ref, vl_ref, vc_ref, o_ref):
        i = pl.program_id(2)

        @pl.when(i < nq)
        def _():
            (el, ec), inv = _attn_exp(q_ref[...], [kl_ref[...], kc_ref[...]])
            o_ref[...] = ((_dot(el.astype(BF16), vl_ref[...]) + _dot(ec.astype(BF16), vc_ref[...])) * inv).astype(BF16)

        @pl.when(i == nq)
        def _():
            (ec,), inv = _attn_exp(q_ref[...], [kc_ref[...]])
            o_ref[...] = (_dot(ec.astype(BF16), vc_ref[...]) * inv).astype(BF16)

    (o,), got = _hosted(
        body, rider, name=name, grid=(dm.B, HEADS, nq + 1),
        in_specs=[sp["q"], sp["k_lat"], sp["k_ctx"], sp["v_lat"], sp["v_ctx"]], out_specs=[sp["o"]],
        out_shape=[jax.ShapeDtypeStruct((T, HEADS * V_HEAD), BF16)], scratch_shapes=[],
        sem=("parallel", "parallel", "arbitrary"), args=(q, k, k, v, v))
    return o, got


def _attn_bwd(q, k, v, o, do, dm, name):
    T = dm.T
    sp = _attn_specs(dm)
    nq, tq = sp["nq"], sp["tq"]
    HP, HQ, HV = HEAD_PAD, HEADS * HEAD_PAD, HEADS * V_HEAD

    def body(q_ref, kl_ref, kc_ref, vl_ref, vc_ref, o_ref, do_ref, dq_ref, dkl_ref, dkc_ref, dvl_ref, dvc_ref):
        i = pl.program_id(2)

        @pl.when(i == 0)
        def _():
            dkl_ref[...] = jnp.zeros_like(dkl_ref)
            dkc_ref[...] = jnp.zeros_like(dkc_ref)
            dvl_ref[...] = jnp.zeros_like(dvl_ref)
            dvc_ref[...] = jnp.zeros_like(dvc_ref)

        qv = q_ref[...]
        dov = do_ref[...]
        dob = dov.astype(BF16)
        delta = jnp.sum(dov * o_ref[...].astype(F32), axis=-1, keepdims=True)
        (el, ec), inv = _attn_exp(qv, [kl_ref[...], kc_ref[...]], jnp.where(i == nq, NEG, 0.0))
        pl_, pc = el * inv, ec * inv
        dsl = (pl_ * (_dot_nt(dob, vl_ref[...]) - delta)).astype(BF16)
        dsc = (pc * (_dot_nt(dob, vc_ref[...]) - delta)).astype(BF16)
        dq_ref[...] = _dot(dsl, kl_ref[...]) + _dot(dsc, kc_ref[...])
        dkl_ref[...] += _dot_tn(dsl, qv)
        dkc_ref[...] += _dot_tn(dsc, qv)
        dvl_ref[...] += _dot_tn(pl_.astype(BF16), dob)
        dvc_ref[...] += _dot_tn(pc.astype(BF16), dob)

    return pl.pallas_call(
        body, name=name, grid=(dm.B, HEADS, nq + 1),
        in_specs=[sp["q"], sp["k_lat"], sp["k_ctx"], sp["v_lat"], sp["v_ctx"], sp["o"], sp["o"]],
        out_specs=[sp["q"], sp["k_lat"], pl.BlockSpec((tq, HP), lambda b, hd, i: (b, hd)),
                   sp["v_lat"], pl.BlockSpec((tq, V_HEAD), lambda b, hd, i: (b, hd))],
        out_shape=[jax.ShapeDtypeStruct((T, HQ), F32), jax.ShapeDtypeStruct((dm.B * dm.N, HQ), F32),
                   jax.ShapeDtypeStruct((dm.B * dm.CTX, HQ), F32), jax.ShapeDtypeStruct((dm.B * dm.N, HV), F32),
                   jax.ShapeDtypeStruct((dm.B * dm.CTX, HV), F32)],
        compiler_params=_cparams("parallel", "parallel", "arbitrary"),
    )(q, k, k, v, v, o, do)


def _loss_grad(h, target, dm, name):
    T, D = h.shape
    tm = dm.tm
    nlat = dm.B * dm.N // tm

    def body(h_ref, t_ref, dh_ref, ls_ref):
        lat = (pl.program_id(0) < nlat).astype(F32)
        diff = (h_ref[...] - t_ref[...]) * lat
        dh_ref[...] = diff * (1.0 / D)
        ls_ref[...] = jnp.zeros(ls_ref.shape, F32) + (0.5 / D) * jnp.sum(diff * diff)

    return pl.pallas_call(
        body, name=name, grid=(T // tm,),
        in_specs=[pl.BlockSpec((tm, D), lambda i: (i, 0)), pl.BlockSpec((tm, D), lambda i: (jnp.minimum(i, nlat - 1), 0))],
        out_specs=[pl.BlockSpec((tm, D), lambda i: (i, 0)), pl.BlockSpec((1, 8, 128), lambda i: (i, 0, 0))],
        out_shape=[jax.ShapeDtypeStruct((T, D), F32), jax.ShapeDtypeStruct((T // tm, 8, 128), F32)],
        compiler_params=_cparams("parallel"),
    )(h, target)


def _col_block(cols, target=1152):
    return max(t for t in range(128, min(cols, target) + 1, 128) if cols % t == 0)


def _mod_fwd(cond, w_mod, b_mod, name):
    L, D, C = w_mod.shape
    R = cond.shape[0]
    cb = _col_block(C)

    def body(c_ref, w_ref, b_ref, o_ref):
        cv = c_ref[...]
        sc = (cv * jax.nn.sigmoid(cv)).astype(BF16)
        o_ref[...] = _dot(sc, w_ref[...].astype(BF16)) + b_ref[...]

    return pl.pallas_call(
        body, name=name, grid=(L, C // cb),
        in_specs=[pl.BlockSpec((R, D), lambda l, c: (0, 0)), pl.BlockSpec((None, D, cb), lambda l, c: (l, 0, c)),
                  pl.BlockSpec((None, 1, cb), lambda l, c: (l, 0, c))],
        out_specs=pl.BlockSpec((None, R, cb), lambda l, c: (l, 0, c)),
        out_shape=jax.ShapeDtypeStruct((L, R, C), F32),
        compiler_params=_cparams("parallel", "parallel"),
    )(cond, w_mod, b_mod)


def _mod_bwd(cond, dmod, w_mod, name):
    L, D, C = w_mod.shape
    R = cond.shape[0]
    cb = _col_block(C)
    nc = C // cb

    def body(c_ref, dm_ref, w_ref, gw_ref, ds_ref):
        cv = c_ref[...]
        sc = (cv * jax.nn.sigmoid(cv)).astype(BF16)
        dmv = dm_ref[...].astype(BF16)
        gw_ref[...] = _dot_tn(sc, dmv)
        part = _dot_nt(dmv, w_ref[...].astype(BF16))

        @pl.when(pl.program_id(1) == 0)
        def _():
            ds_ref[...] = part

        @pl.when(pl.program_id(1) > 0)
        def _():
            ds_ref[...] += part

    return pl.pallas_call(
        body, name=name, grid=(L, nc),
        in_specs=[pl.BlockSpec((R, D), lambda l, c: (0, 0)), pl.BlockSpec((None, R, cb), lambda l, c: (l, 0, c)),
                  pl.BlockSpec((None, D, cb), lambda l, c: (l, 0, c))],
        out_specs=[pl.BlockSpec((None, D, cb), lambda l, c: (l, 0, c)), pl.BlockSpec((None, R, D), lambda l, c: (l, 0, 0))],
        out_shape=[jax.ShapeDtypeStruct((L, D, C), F32), jax.ShapeDtypeStruct((L, R, D), F32)],
        compiler_params=_cparams("parallel", "arbitrary"),
    )(cond, dmod, w_mod)


def _row_block(rows, cols, budget=1 << 20):
    best = None
    for t in range(16, rows + 1, 16):
        if rows % t == 0 and t * cols * 4 <= budget:
            best = t
    return best if best is not None else rows


def _sum_slots(recvs, owns, chip, core, bufs, pieces, piece, name):
    n = len(recvs)
    S, R, C = recvs[0].shape
    tr = _row_block(R, C, budget=512 << 10)

    def body(ids_ref, *refs):
        for r_ref, p_ref, o_ref in zip(refs[:n], refs[n:2 * n], refs[-n:]):
            acc = None
            for s in range(S):
                v = jnp.where(ids_ref[0] == s, p_ref[s], r_ref[s]).astype(F32)
                acc = v if acc is None else acc + v
            o_ref[...] = acc

    blk = pl.BlockSpec((S, tr, C), lambda i, ids: (0, i, 0))
    held = [] if bufs is None else list(bufs)
    return pl.pallas_call(
        body, name=name,
        grid_spec=pltpu.PrefetchScalarGridSpec(
            num_scalar_prefetch=1, grid=(R // tr,), in_specs=[blk] * (2 * n) + [pl.BlockSpec(memory_space=pl.ANY)] * len(held),
            out_specs=[pl.BlockSpec((None, None, tr, C), lambda i, ids: (piece, ids[1], i, 0))] * n),
        out_shape=[jax.ShapeDtypeStruct((pieces, 2, R, C), F32)] * n,
        input_output_aliases={1 + 2 * n + i: i for i in range(len(held))}, compiler_params=_cparams("parallel"),
    )(jnp.stack([chip, core]).astype(jnp.int32), *recvs, *owns, *held)


def _adamw(w, gs, m, v, name):
    ng = len(gs)
    R, C = w.shape
    tr = _row_block(R, C)
    c1 = 1.0 / (1.0 - ADAM_B1 ** ADAM_STEP)
    c2 = 1.0 / (1.0 - ADAM_B2 ** ADAM_STEP)

    def body(w_ref, *refs):
        m_ref, v_ref, g_ref, d_ref, mo_ref, vo_ref = refs[ng:]
        g = refs[0][...]
        for g_more in refs[1:ng]:
            g = g + g_more[...]
        g_ref[...] = g
        mn = ADAM_B1 * m_ref[...] + (1.0 - ADAM_B1) * g
        vn = ADAM_B2 * v_ref[...] + (1.0 - ADAM_B2) * (g * g)
        mo_ref[...] = mn
        vo_ref[...] = vn
        d_ref[...] = -ADAM_LR * ((mn * c1) / (jnp.sqrt(vn * c2) + ADAM_EPS) + ADAM_WD * w_ref[...])

    blk = pl.BlockSpec((tr, C), lambda i: (i, 0))
    sd = jax.ShapeDtypeStruct((R, C), F32)
    return pl.pallas_call(
        body, name=name, grid=(R // tr,), in_specs=[blk] * (3 + ng), out_specs=[blk] * 4, out_shape=[sd] * 4,
        compiler_params=_cparams("parallel"),
    )(w, *gs, m, v)


def _rope_tables(dm):
    n = dm.N
    t = jnp.arange(n)
    r = (t // GRID_W).astype(F32)
    col = (t % GRID_W).astype(F32)
    nf = QK_ROPE // 4
    inv = ROPE_BASE ** (-jnp.arange(nf, dtype=F32) / nf)
    ang = jnp.stack([r[:, None] * inv, col[:, None] * inv], axis=1)
    cos, sin = jnp.cos(ang), jnp.sin(ang)
    zero = jnp.zeros_like(sin)
    c64 = jnp.stack([cos, cos], axis=2).reshape(n, QK_ROPE)
    s1 = jnp.stack([-sin, zero], axis=2).reshape(n, QK_ROPE)
    s2 = jnp.stack([zero, sin], axis=2).reshape(n, QK_ROPE)

    def pad(x, fill):
        return jnp.concatenate([jnp.full((n, QK_NOPE), fill, F32), x, jnp.full((n, HEAD_PAD - QK_HEAD), fill, F32)], axis=1)

    lat = jnp.stack([pad(c64, 1.0), pad(s1, 0.0), pad(s2, 0.0)])
    lat = jnp.tile(lat, (1, dm.B, 1))
    nctx = dm.B * dm.CTX
    ctx = jnp.stack([jnp.ones((nctx, HEAD_PAD), F32), jnp.zeros((nctx, HEAD_PAD), F32), jnp.zeros((nctx, HEAD_PAD), F32)])
    return jnp.concatenate([lat, ctx], axis=1)


def _fold_parts(part, dm):
    nblk = part.shape[0]
    nb = (dm.N * nblk) // dm.T
    groups = [part[b * nb:(b + 1) * nb].sum(axis=0) for b in range(dm.B)]
    groups.append(part[dm.B * nb:].sum(axis=0))
    return jnp.stack(groups)


def grouped(items, key):
    groups = {}
    for it in items:
        groups.setdefault(key(it), []).append(it)
    return list(groups.values())


def _flat2(a):
    return a.reshape(-1, a.shape[-1])


def kernel(x, c, ctx, c_ctx, w_mod, b_mod, g_norm, ffn_w1, ffn_w3, ffn_w2, sc_w_in, sc_conv, sc_w_out, mla_w_a, mla_g_qa, mla_w_uq, mla_g_kva, mla_w_ukv, mla_g_q, mla_g_k, mla_w_o, loss_target, m_c_ctx, m_w_mod, m_b_mod, m_g_norm, m_ffn_w1, m_ffn_w3, m_ffn_w2, m_sc_w_in, m_sc_conv, m_sc_w_out, m_mla_w_a, m_mla_g_qa, m_mla_w_uq, m_mla_g_kva, m_mla_w_ukv, m_mla_g_q, m_mla_g_k, m_mla_w_o, v_c_ctx, v_w_mod, v_b_mod, v_g_norm, v_ffn_w1, v_ffn_w3, v_ffn_w2, v_sc_w_in, v_sc_conv, v_sc_w_out, v_mla_w_a, v_mla_g_qa, v_mla_w_uq, v_mla_g_kva, v_mla_w_ukv, v_mla_g_q, v_mla_g_k, v_mla_w_o):
    B, N, D = x.shape
    CTX = ctx.shape[1]
    T = B * (N + CTX)
    tm = next(t for t in (512, 256, 128, 64, 32, 16) if N % t == 0 and (B * CTX) % t == 0)
    dm = Dims(B, N, CTX, D, T, tm)
    L = w_mod.shape[0]
    La, Lb = sc_w_in.shape[0], mla_w_a.shape[0]
    S = N_CHIPS
    ndev = 2 * S
    xi, yi, ci = lax.axis_index("x"), lax.axis_index("y"), lax.axis_index("c")
    chip = 2 * xi + yi
    dev = 2 * chip + ci
    weights = dict(c_ctx=c_ctx, w_mod=w_mod, b_mod=b_mod, g_norm=g_norm, ffn_w1=ffn_w1, ffn_w3=ffn_w3, ffn_w2=ffn_w2,
                   sc_w_in=sc_w_in, sc_conv=sc_conv, sc_w_out=sc_w_out, mla_w_a=mla_w_a, mla_g_qa=mla_g_qa,
                   mla_w_uq=mla_w_uq, mla_g_kva=mla_g_kva, mla_w_ukv=mla_w_ukv, mla_g_q=mla_g_q, mla_g_k=mla_g_k,
                   mla_w_o=mla_w_o)
    mom = dict(c_ctx=(m_c_ctx, v_c_ctx), w_mod=(m_w_mod, v_w_mod), b_mod=(m_b_mod, v_b_mod), g_norm=(m_g_norm, v_g_norm),
               ffn_w1=(m_ffn_w1, v_ffn_w1), ffn_w3=(m_ffn_w3, v_ffn_w3), ffn_w2=(m_ffn_w2, v_ffn_w2),
               sc_w_in=(m_sc_w_in, v_sc_w_in), sc_conv=(m_sc_conv, v_sc_conv), sc_w_out=(m_sc_w_out, v_sc_w_out),
               mla_w_a=(m_mla_w_a, v_mla_w_a), mla_g_qa=(m_mla_g_qa, v_mla_g_qa), mla_w_uq=(m_mla_w_uq, v_mla_w_uq),
               mla_g_kva=(m_mla_g_kva, v_mla_g_kva), mla_w_ukv=(m_mla_w_ukv, v_mla_w_ukv), mla_g_q=(m_mla_g_q, v_mla_g_q),
               mla_g_k=(m_mla_g_k, v_mla_g_k), mla_w_o=(m_mla_w_o, v_mla_w_o))

    big = ["ffn_w1", "ffn_w3", "ffn_w2", "sc_w_in", "sc_w_out", "mla_w_a", "mla_w_uq", "mla_w_ukv", "mla_w_o"]
    F = ffn_w1.shape[-1]
    transposed = ("ffn_w1", "ffn_w3")
    for n in transposed:
        weights[n] = jnp.swapaxes(weights[n], 2, 3)
        mom[n] = tuple(jnp.swapaxes(a, 2, 3) for a in mom[n])
    mixer_names =(["sc_w_in", "sc_w_out"], ["mla_w_a", "mla_w_uq", "mla_w_ukv", "mla_w_o"])

    def placed(names, piece, npieces):
        w2 = [_flat2(weights[n]) for n in names]
        rows = w2[0].shape[0] // npieces
        return dict(zip(names, _place_cast(w2, piece * rows, rows, chip, S, "place_weight")))

    bufs = {}
    for l in range(L):
        for k in range(2):
            bufs["f", l, k] = placed(["ffn_w1", "ffn_w3", "ffn_w2"], 2 * l + k, 2 * L)
        bufs["m", l] = {}
        for n in mixer_names[l % 2]:
            bufs["m", l].update(placed([n], l // 2, weights[n].shape[0]))
    order = [stage for l in range(L) for stage in (("f", l, 0), ("m", l), ("f", l, 1))]

    def gather_after(stage):
        at = order.index(stage)
        if at + 1 == len(order):
            return None, lambda got: None
        nxt = bufs[order[at + 1]]
        names = list(nxt)
        return _gather_rider([nxt[n] for n in names]), lambda got: nxt.update(zip(names, got))

    names = list(bufs[order[0]])
    bufs[order[0]].update(zip(names, _ride_alone(_gather_rider([bufs[order[0]][n] for n in names]), "gather_weights")))

    def ffn_weights(l, k):
        b = bufs["f", l, k]
        return b["ffn_w1"], b["ffn_w3"], b["ffn_w2"]

    def mixer_weights(l):
        b = bufs["m", l]
        w = {}
        if l % 2 == 0:
            w["w_in"] = b["sc_w_in"][:, None]
            w["w_out"] = b["sc_w_out"].reshape(1, D, D)
        else:
            w["w_a"] = jnp.pad(b["mla_w_a"].reshape(1, D, -1), ((0, 0), (0, 0), (0, 512 - (Q_LORA + KV_LORA + QK_ROPE))))
            wuq = jnp.moveaxis(b["mla_w_uq"], 0, 1).reshape(1, Q_LORA, HEADS, QK_HEAD)
            w["w_uq"] = jnp.pad(wuq, ((0, 0), (0, 0), (0, 0), (0, HEAD_PAD - QK_HEAD))).reshape(1, Q_LORA, HEADS * HEAD_PAD)
            w["w_ukv"] = jnp.moveaxis(b["mla_w_ukv"], 0, 1).reshape(1, KV_LORA, HEADS * HEAD_PAD)
            w["w_o"] = b["mla_w_o"].reshape(1, HEADS * V_HEAD, D)
        return w

    vecs = ["g_norm", "sc_conv", "mla_g_qa"]
    gathered = _gather_small([_flat2(weights[n]) for n in vecs], ("x", "y"), "gather_vectors")
    gw = {n: g.reshape((S,) + weights[n].shape) for n, g in zip(vecs, gathered)}
    gnorm = jnp.moveaxis(gw["g_norm"], 0, 2).reshape(L, 3, D)
    convw = jnp.moveaxis(gw["sc_conv"], 0, 2).reshape(La, 3, D)
    gqa = jnp.moveaxis(gw["mla_g_qa"], 0, 1).reshape(Lb, Q_LORA)
    padl = lambda a: jnp.pad(a, ((0, 0), (0, HEAD_PAD - a.shape[1])))
    gains = jnp.stack([padl(gqa), padl(mla_g_kva), padl(mla_g_q), padl(mla_g_k)], axis=1)
    gains = jnp.pad(gains, ((0, 0), (0, 4), (0, 0)))

    R = -(-(ndev * B + 1) // 16) * 16
    call = _gather_small([c], ("x", "y", "c"), "gather_cond")[0].reshape(ndev * B, D)
    cond = jnp.concatenate([call, c_ctx[None], jnp.zeros((R - ndev * B - 1, D), F32)], axis=0)
    C = w_mod.shape[-1]
    bm = lax.dynamic_slice_in_dim(b_mod, chip * C, C, axis=1)[:, None, :]
    mshard = _mod_fwd(cond, w_mod, bm, "mod_fwd")
    mfull = _gather_small([mshard.reshape(L * R, C)], ("x", "y"), "gather_mod")[0].reshape(S, L, R, C)
    mfull = jnp.moveaxis(mfull, 0, 2).reshape(L, R, S * C)
    mine = lax.dynamic_slice_in_dim(mfull, dev * B, B, axis=1)
    mod = jnp.concatenate([mine, mfull[:, ndev * B:ndev * B + 1]], axis=1).reshape(L, B + 1, 9, D)

    tabs = _rope_tables(dm)
    h = jnp.concatenate([x.reshape(B * N, D), ctx.reshape(B * CTX, D)], axis=0)

    saved = []
    lw = [None] * L
    for l in range(L):
        kind, j = l % 2, l // 2
        sv = {}
        sv["h0"] = h
        rider, keep = gather_after(("f", l, 0))
        (h, sv["a1"], sv["b1"], sv["hn1"], sv["y1"]), got = _ffn_fwd(h, mod[l], gnorm[l, 0:1], *ffn_weights(l, 0), 0, dm,
                                                                      "ffn_fwd", rider)
        keep(got)
        sv["h1"] = h
        W = lw[l] = mixer_weights(l)
        rider, keep = gather_after(("m", l))
        if kind == 0:
            (sv["p"], sv["hnm"]), got = _sc_in_fwd(h, mod[l], gnorm[l, 1:2], W["w_in"], 0, dm, "sc_in_fwd", rider)
            sv["z"] = _conv_fwd(sv["p"], convw[j], dm, "conv_fwd")
            h, sv["ym"] = _out_fwd(sv["z"], W["w_out"], h, mod[l], 0, dm, "sc_out_fwd")
        else:
            sv["hnm"], sv["q"], sv["k"], sv["v"] = _mla_proj_fwd(h, mod[l], gnorm[l, 1:2], gains[j:j + 1], tabs, W["w_a"], W["w_uq"],
                                                                 W["w_ukv"], 0, dm, "mla_proj_fwd")
            sv["o"], got = _attn_fwd(sv["q"], sv["k"], sv["v"], dm, "attn_fwd", rider)
            h, sv["ym"] = _out_fwd(sv["o"], W["w_o"], h, mod[l], 0, dm, "mla_out_fwd")
        keep(got)
        sv["h2"] = h
        rider, keep = gather_after(("f", l, 1))
        (h, sv["a2"], sv["b2"], sv["hn2"], sv["y2"]), got = _ffn_fwd(h, mod[l], gnorm[l, 2:3], *ffn_weights(l, 1), 1, dm,
                                                                      "ffn_fwd", rider)
        keep(got)
        saved.append(sv)

    dh, lsum = _loss_grad(h, loss_target.reshape(B * N, D), dm, "loss_grad")
    loss = lax.psum(jnp.sum(lsum[:, 0, 0]), ("x", "y", "c"))

    wq = D // S
    gsum = {n: None for n in big}
    npieces = {n: weights[n].shape[0] * (weights[n].shape[1] if n.startswith("ffn") else 1) for n in big}
    dmod = [None] * L
    dgn = [None] * L
    dconv = [None] * La
    dgains = [None] * Lb
    tk = T // 2 if T % 32 == 0 else T
    nk = T // tk
    full_a = pl.BlockSpec((tk, D), lambda s, kk: (kk, 0))
    shard_b = pl.BlockSpec((None, tk, F), lambda s, kk: (s, kk, 0))
    per_slot = lambda r_, c_: pl.BlockSpec((None, r_, c_), lambda s, kk: (s, 0, 0))

    def make_job(grads):
        parts = [g_ for _, _, g_ in grads]
        theirs = _swap_halves(parts, "swap_halves")
        pairs = [None] * len(grads)
        for idx in grouped(range(len(grads)), lambda i: parts[i].shape):
            outs = _pair_sum([parts[i] for i in idx], [theirs[i] for i in idx], ci, "pair_sum")
            for i, o in zip(idx, outs):
                pairs[i] = o
        return [(n, p, pair) for (n, p, _), pair in zip(grads, pairs)]

    def finish_job(job, recv):
        key = lambda i: (recv[i].shape, npieces[job[i][0]], job[i][1], gsum[job[i][0]] is None)
        for idx in grouped(range(len(job)), key):
            names_ = [job[i][0] for i in idx]
            held = None if gsum[names_[0]] is None else [gsum[n] for n in names_]
            outs = _sum_slots([recv[i] for i in idx], [job[i][2] for i in idx], chip, ci, held, npieces[names_[0]],
                              job[idx[0]][1], "sum_slots")
            gsum.update(zip(names_, outs))

    def ffn_back(dh, sv, l, k, job):
        sfx = "1" if k == 0 else "2"
        rider = _scatter_rider([pair for _, _, pair in job]) if job else None
        (dh, da, db, sw, dy, part), recv = _ffn_bwd(dh, sv["h0" if k == 0 else "h2"], mod[l], gnorm[l, 2 * k:2 * k + 1], sv["y" + sfx],
                                                    sv["a" + sfx], sv["b" + sfx], *ffn_weights(l, k), k, dm, "ffn_bwd", rider)
        finish_job(job, recv)
        g1 = _mm_tn(da, sv["hn" + sfx], shard_b, full_a, (S, F, D), per_slot(F, D), (S, nk), "gw1")
        g3 = _mm_tn(db, sv["hn" + sfx], shard_b, full_a, (S, F, D), per_slot(F, D), (S, nk), "gw3")
        g2 = _mm_tn(sw, dy, shard_b, full_a, (S, F, D), per_slot(F, D), (S, nk), "gw2")
        p = 2 * l + k
        return dh, _fold_parts(part, dm), [("ffn_w1", p, g1), ("ffn_w3", p, g3), ("ffn_w2", p, g2)]

    one = (1, nk)
    a1 = lambda kdim: pl.BlockSpec((tk, kdim), lambda s, kk: (kk, 0))
    pending = []
    for l in reversed(range(L)):
        kind, j = l % 2, l // 2
        sv = saved[l]
        W = lw[l]
        dh, p2, grads = ffn_back(dh, sv, l, 1, pending)
        job2 = make_job(grads)
        if kind == 0:
            dy, dz, pg = _out_bwd(dh, sv["ym"], W["w_out"], mod[l], 0, dm, "sc_out_bwd")
            g_out = _mm_tn(sv["z"], dy, a1(D), a1(D), (1, D, D), per_slot(D, D), one, "gw_sc_out")
            dp, dconv[j] = _conv_bwd(dz, sv["p"], convw[j], dm, "conv_bwd")
            g_in = _mm_tn(sv["hnm"], dp, pl.BlockSpec((tk, D), lambda q, kk: (kk, 0)),
                          pl.BlockSpec((3, tk, wq), lambda q, kk: (0, kk, q)), (3, S, D, wq),
                          pl.BlockSpec((3, None, D, wq), lambda q, kk: (0, q, 0, 0)), (S, nk), "gw_sc_in")
            dh, pm = _sc_in_bwd(dh, dp, sv["h1"], mod[l], gnorm[l, 1:2], W["w_in"], 0, dm, "sc_in_bwd")
            grads = [("sc_w_in", j, jnp.moveaxis(g_in.reshape(S, 3, D, wq), 1, 2).reshape(S, D, 3 * wq)),
                     ("sc_w_out", j, g_out.reshape(S, D // S, D))]
        else:
            dy, do, pg = _out_bwd(dh, sv["ym"], W["w_o"], mod[l], 0, dm, "mla_out_bwd")
            g_o = _mm_tn(sv["o"], dy, a1(HEADS * V_HEAD), a1(D), (1, HEADS * V_HEAD, D), per_slot(HEADS * V_HEAD, D), one, "gw_mla_o")
            dq, dkl, dkc, dvl, dvc = _attn_bwd(sv["q"], sv["k"], sv["v"], sv["o"], do, dm, "attn_bwd")
            dh, pm, g_a, g_uq, g_ukv, dgains[j] = _mla_proj_bwd(
                dh, dq, dkl, dkc, dvl, dvc, sv["h1"], mod[l], gnorm[l, 1:2], gains[j:j + 1], tabs, W["w_a"], W["w_uq"], W["w_ukv"], 0, dm, "mla_proj_bwd")
            g_uq = g_uq.reshape(Q_LORA, HEADS, HEAD_PAD)[..., :QK_HEAD].reshape(Q_LORA, S, -1)
            grads = [("mla_w_a", j, g_a[:, :Q_LORA + KV_LORA + QK_ROPE].reshape(S, D // S, -1).astype(BF16)),
                     ("mla_w_uq", j, jnp.moveaxis(g_uq, 1, 0).astype(BF16)),
                     ("mla_w_ukv", j, jnp.moveaxis(g_ukv.reshape(KV_LORA, S, -1), 1, 0).astype(BF16)),
                     ("mla_w_o", j, g_o.reshape(S, HEADS * V_HEAD // S, D))]
        jobm = make_job(grads)
        pm = _fold_parts(pm, dm) + _fold_parts(pg, dm)
        dh, p0, grads = ffn_back(dh, sv, l, 0, job2 + jobm)
        pending = make_job(grads)
        dmod[l] = jnp.concatenate([p0[:, 0:3], pm[:, 0:3], p2[:, 0:3]], axis=1).reshape(B + 1, 9 * D)
        dgn[l] = jnp.stack([p0[:, 3].sum(0), pm[:, 3].sum(0), p2[:, 3].sum(0)])
    grad_x = dh[:B * N].reshape(B, N, D)

    dgains_a = jnp.stack(dgains)
    small = [jnp.stack(dmod).reshape(-1), jnp.stack(dgn).reshape(-1), jnp.stack(dconv).reshape(-1), dgains_a.reshape(-1)]
    sizes = [s_.shape[0] for s_ in small]
    flat = jnp.concatenate(small)
    pad = (-flat.shape[0]) % 1024
    flat = jnp.pad(flat, (0, pad)).reshape(-1, 128)
    allsmall = _gather_small([flat], ("x", "y", "c"), "gather_small")[0].reshape(ndev, -1)
    offs = [0]
    for s_ in sizes:
        offs.append(offs[-1] + s_)
    dmod_all = allsmall[:, offs[0]:offs[1]].reshape(ndev, L, B + 1, 9 * D)
    tot = allsmall[:, offs[1]:offs[4]].sum(axis=0)
    g_gnorm = tot[:offs[2] - offs[1]].reshape(L, 3, D)
    g_conv = tot[offs[2] - offs[1]:offs[3] - offs[1]].reshape(La, 3, D)
    g_gains = tot[offs[3] - offs[1]:].reshape(Lb, 8, HEAD_PAD)
    dM = jnp.concatenate([jnp.moveaxis(dmod_all[:, :, :B], 0, 1).reshape(L, ndev * B, 9 * D),
                          dmod_all[:, :, B].sum(axis=0)[:, None, :], jnp.zeros((L, R - ndev * B - 1, 9 * D), F32)], axis=1)
    g_bmod = dM.sum(axis=1)
    dM_mine = lax.dynamic_slice_in_dim(dM, chip * C, C, axis=2)
    g_wmod, dsil = _mod_bwd(cond, dM_mine, w_mod, "mod_bwd")
    dsil_ctx = dsil[:, ndev * B].sum(axis=0)
    dsil_all = _gather_small([jnp.pad(dsil_ctx.reshape(-1, 128), ((0, (-(D // 128)) % 8), (0, 0)))], ("x", "y"), "gather_dctx")[0]
    dsil_tot = dsil_all.sum(axis=0)[:D // 128].reshape(D)
    sg = jax.nn.sigmoid(c_ctx)
    g_cctx = dsil_tot * (sg * (1.0 + c_ctx * (1.0 - sg)))

    chip_cols = lambda a, width: lax.dynamic_slice_in_dim(a, chip * width, width, axis=a.ndim - 1)
    small_grads = dict(
        c_ctx=g_cctx, b_mod=g_bmod, g_norm=chip_cols(g_gnorm, D // S), sc_conv=chip_cols(g_conv, D // S),
        mla_g_qa=chip_cols(g_gains[:, 0, :Q_LORA], Q_LORA // S), mla_g_kva=g_gains[:, 1, :KV_LORA],
        mla_g_q=g_gains[:, 2, :QK_HEAD], mla_g_k=g_gains[:, 3, :QK_HEAD])

    finish_job(pending, _ride_alone(_scatter_rider([pair for _, _, pair in pending]), "scatter_grads"))
    gsum = dict(zip(big, _swap_cores_inplace([gsum[n] for n in big], "swap_cores")))

    grads, deltas, new_m, new_v = {}, {}, {}, {}
    for n, w in weights.items():
        shape = w.shape
        w2 = _flat2(w) if w.ndim > 1 else w.reshape(1, -1)
        m2, v2 = (a.reshape(w2.shape) for a in mom[n])
        if n in gsum:
            gs = [gsum[n].reshape(w2.shape)]
        elif n == "w_mod":
            gs = [_flat2(g_wmod)]
        else:
            gs = [small_grads[n].reshape(w2.shape)]
        g_, d_, m_, v_ = _adamw(w2, gs, m2, v2, "adamw")
        grads[n], deltas[n], new_m[n], new_v[n] = (a.reshape(shape) for a in (g_, d_, m_, v_))
    for n in transposed:
        grads[n], deltas[n], new_m[n], new_v[n] = (jnp.swapaxes(a, 2, 3) for a in (grads[n], deltas[n], new_m[n], new_v[n]))

    names = list(weights)
    return (loss, grad_x, *[grads[n] for n in names], *[deltas[n] for n in names], *[new_m[n] for n in names],
            *[new_v[n] for n in names])
```

```python
import functools
import math
from typing import NamedTuple

import jax
import jax.numpy as jnp
from jax import lax
from jax.experimental import pallas as pl
from jax.experimental.pallas import tpu as pltpu

F32 = jnp.float32
BF16 = jnp.bfloat16
EPS = 1e-6
GRID_W = 64
HEADS = 8
QK_NOPE = 128
QK_ROPE = 64
QK_HEAD = QK_NOPE + QK_ROPE
HEAD_PAD = 256
V_HEAD = 128
Q_LORA = 256
KV_LORA = 128
ROPE_BASE = 10000.0
QK_SCALE = QK_HEAD ** -0.5
ADAM_LR, ADAM_B1, ADAM_B2, ADAM_EPS, ADAM_WD, ADAM_STEP = 0.001, 0.9, 0.999, 1e-08, 0.01, 10
N_CHIPS = 4
VMEM_LIMIT = 56 * 1024 * 1024
MESH = pl.DeviceIdType.MESH
NEG = -1e30


class Dims(NamedTuple):
    B: int
    N: int
    CTX: int
    D: int
    T: int
    tm: int


def _cparams(*sem):
    return pltpu.CompilerParams(dimension_semantics=sem if sem else None, vmem_limit_bytes=VMEM_LIMIT)


def _dot(a, b):
    return jnp.dot(a, b, preferred_element_type=F32)


def _dot_nt(a, b):
    return lax.dot_general(a, b, (((1,), (1,)), ((), ())), preferred_element_type=F32)


def _dot_tn(a, b):
    return lax.dot_general(a, b, (((0,), (0,)), ((), ())), preferred_element_type=F32)


def _rms(x, n):
    r = lax.rsqrt(jnp.sum(x * x, axis=-1, keepdims=True) * (1.0 / n) + EPS)
    return x * r, r


def _rms_bwd(dxh, xh, r, n):
    return r * (dxh - xh * (jnp.sum(dxh * xh, axis=-1, keepdims=True) * (1.0 / n)))


def _pre(h, g, shift, scale):
    xh, _ = _rms(h, h.shape[-1])
    return (xh * g) * (1.0 + scale) + shift


def _pre_bwd(dout, h, g, scale):
    d = h.shape[-1]
    xh, r = _rms(h, d)
    n = xh * g
    dshift = jnp.sum(dout, axis=0, keepdims=True)
    dscale = jnp.sum(dout * n, axis=0, keepdims=True)
    dn = dout * (1.0 + scale)
    dg = jnp.sum(dn * xh, axis=0, keepdims=True)
    dh = _rms_bwd(dn * g, xh, r, d)
    return dh, dshift, dscale, dg


def _write_part(part_ref, dshift=None, dscale=None, dgate=None, dg=None):
    z = jnp.zeros((1, part_ref.shape[-1]), F32)
    part_ref[0, 0:1, :] = z if dshift is None else dshift
    part_ref[0, 1:2, :] = z if dscale is None else dscale
    part_ref[0, 2:3, :] = z if dgate is None else dgate
    part_ref[0, 3:4, :] = z if dg is None else dg
    part_ref[0, 4:8, :] = jnp.zeros((4, part_ref.shape[-1]), F32)


def _grp(dm):
    nb = dm.N // dm.tm
    return lambda i: jnp.minimum(i // nb, dm.B)


def _n_chunks(rows, row_bytes):
    n = 16
    while n > 1 and (rows % (16 * n) or (rows // n) * row_bytes < (256 << 10)):
        n //= 2
    return n


def _start_local(src, dst, sems, k0, nchunk):
    ch = src.shape[0] // nchunk
    copies = []
    for j in range(nchunk):
        cp = pltpu.make_async_copy(src.at[pl.ds(j * ch, ch)], dst.at[pl.ds(j * ch, ch)], sems.at[k0 + j])
        cp.start()
        copies.append(cp)
    return copies


def _gather_small(arrs, axes, name):
    n = len(arrs)
    nbits = len(axes)
    slots = 2 ** nbits
    pats = list(range(1, slots))
    nck = [_n_chunks(a.shape[-2], a.shape[-1] * a.dtype.itemsize) for a in arrs]
    base = [sum(nck[:i]) * len(pats) for i in range(n)]
    nsem = sum(nck) * len(pats)

    def body(*refs):
        ins, outs = refs[:n], refs[n:2 * n]
        send, recv, loc = refs[2 * n:]
        pos = {a: lax.axis_index(a) for a in ("x", "y", "c")}

        def slot_of(p):
            s = 0
            for a in axes:
                s = 2 * s + p[a]
            return s

        me = slot_of(pos)
        local = []
        for i in range(n):
            local += _start_local(ins[i], outs[i].at[me], loc, sum(nck[:i]), nck[i])
        remote = []
        for pi, pat in enumerate(pats):
            peer = dict(pos)
            for bi, a in enumerate(axes):
                if (pat >> (nbits - 1 - bi)) & 1:
                    peer[a] = 1 - pos[a]
            them = slot_of(peer)
            for i in range(n):
                ch = arrs[i].shape[-2] // nck[i]
                for j in range(nck[i]):
                    k = base[i] + pi * nck[i] + j
                    rs = pl.ds(j * ch, ch)
                    cp = pltpu.make_async_remote_copy(
                        src_ref=ins[i].at[rs], dst_ref=outs[i].at[me, rs], send_sem=send.at[k], recv_sem=recv.at[k],
                        device_id=(peer["x"], peer["y"], peer["c"]), device_id_type=MESH)
                    cp.start()
                    remote.append(cp)
        for cp in local:
            cp.wait()
        for cp in remote:
            cp.wait()

    out_shape = [jax.ShapeDtypeStruct((slots,) + a.shape, a.dtype) for a in arrs]
    any_spec = pl.BlockSpec(memory_space=pl.ANY)
    outs = pl.pallas_call(
        body, name=name, out_shape=out_shape, in_specs=[any_spec] * n, out_specs=[any_spec] * n,
        scratch_shapes=[pltpu.SemaphoreType.DMA((nsem,)), pltpu.SemaphoreType.DMA((nsem,)), pltpu.SemaphoreType.DMA((sum(nck),))],
        compiler_params=pltpu.CompilerParams(has_side_effects=True),
    )(*arrs)
    return list(outs)


class Rider(NamedTuple):
    ins: list
    out_shapes: list
    aliases: dict
    sems: list
    start: object
    mid: object
    end: object


MID_STEPS = 6


def _hosted(body, rider, *, name, grid, in_specs, out_specs, out_shape, scratch_shapes, sem, args):
    if rider is None:
        outs = pl.pallas_call(body, name=name, grid=grid, in_specs=in_specs, out_specs=out_specs, out_shape=out_shape,
                              scratch_shapes=scratch_shapes, compiler_params=_cparams(*sem))(*args)
        return outs, []
    n_in, n_out, n_s = len(in_specs), len(out_specs), len(scratch_shapes)
    nri, nro = len(rider.ins), len(rider.out_shapes)
    nsteps = math.prod(grid)

    def wrapped(*refs):
        bounds = [0, n_in, n_in + nri, n_in + nri + n_out, n_in + nri + n_out + nro, n_in + nri + n_out + nro + n_s, len(refs)]
        ins, rins, outs, routs, scr, sems = (refs[lo:hi] for lo, hi in zip(bounds[:-1], bounds[1:]))
        step = 0
        for ax, extent in enumerate(grid):
            step = step * extent + pl.program_id(ax)

        @pl.when(step == 0)
        def _():
            rider.start(rins, routs, sems)

        body(*ins, *outs, *scr)

        if rider.mid is not None:
            @pl.when(step == max(nsteps - 1 - MID_STEPS, 0))
            def _():
                rider.mid(rins, routs, sems)

        @pl.when(step == nsteps - 1)
        def _():
            rider.end(rins, routs, sems)

    any_spec = pl.BlockSpec(memory_space=pl.ANY)
    outs = pl.pallas_call(
        wrapped, name=name, grid=grid, in_specs=list(in_specs) + [any_spec] * nri, out_specs=list(out_specs) + [any_spec] * nro,
        out_shape=list(out_shape) + list(rider.out_shapes), scratch_shapes=list(scratch_shapes) + list(rider.sems),
        input_output_aliases={n_in + i: n_out + o for i, o in rider.aliases.items()},
        compiler_params=pltpu.CompilerParams(dimension_semantics=("arbitrary",) * len(grid), vmem_limit_bytes=VMEM_LIMIT,
                                             has_side_effects=True),
    )(*args, *rider.ins)
    return outs[:n_out], list(outs[n_out:])


def _gather_rider(bufs):
    n = len(bufs)
    halves = [a.shape[1] // 2 for a in bufs]
    nck = [_n_chunks(h, a.shape[2] * a.dtype.itemsize) for h, a in zip(halves, bufs)]
    base = [3 * sum(nck[:i]) for i in range(n)]
    nsem = 3 * sum(nck)

    def plan():
        x, y, c = lax.axis_index("x"), lax.axis_index("y"), lax.axis_index("c")
        pieces = []
        for pi, (px, py) in enumerate([(x, 1 - y), (1 - x, y), (1 - x, 1 - y)]):
            for i in range(n):
                ch = halves[i] // nck[i]
                for j in range(nck[i]):
                    pieces.append((base[i] + pi * nck[i] + j, px, py, 2 * px + py, i, j * ch, ch))
        return x, y, c, 2 * x + y, pieces

    def rows(i, off, ch, core):
        return pl.ds(pl.multiple_of(core * halves[i] + off, 16), ch)

    def over_ici(outs, sems, c, slot, k, px, py, i, off, ch):
        ref = outs[i].at[slot, rows(i, off, ch, c)]
        return pltpu.make_async_remote_copy(src_ref=ref, dst_ref=ref, send_sem=sems[0].at[k], recv_sem=sems[1].at[k],
                                            device_id=(px, py, c), device_id_type=MESH)

    def over_d2d(outs, sems, x, y, c, slot, k, i, off, ch, core):
        ref = outs[i].at[slot, rows(i, off, ch, core)]
        return pltpu.make_async_remote_copy(src_ref=ref, dst_ref=ref, send_sem=sems[2].at[k], recv_sem=sems[3].at[k],
                                            device_id=(x, y, 1 - c), device_id_type=MESH)

    def start(ins, outs, sems):
        x, y, c, me, pieces = plan()
        for k, px, py, them, i, off, ch in pieces:
            over_ici(outs, sems, c, me, k, px, py, i, off, ch).start()

    def mid(ins, outs, sems):
        x, y, c, me, pieces = plan()
        for k, px, py, them, i, off, ch in pieces:
            over_ici(outs, sems, c, them, k, px, py, i, off, ch).wait_recv()
            over_d2d(outs, sems, x, y, c, them, k, i, off, ch, c).start()

    def end(ins, outs, sems):
        x, y, c, me, pieces = plan()
        for k, px, py, them, i, off, ch in pieces:
            over_ici(outs, sems, c, me, k, px, py, i, off, ch).wait_send()
            over_d2d(outs, sems, x, y, c, them, k, i, off, ch, c).wait_send()
        for k, px, py, them, i, off, ch in pieces:
            over_d2d(outs, sems, x, y, c, them, k, i, off, ch, 1 - c).wait_recv()

    return Rider(ins=list(bufs), out_shapes=[jax.ShapeDtypeStruct(a.shape, a.dtype) for a in bufs],
                 aliases={i: i for i in range(n)}, sems=[pltpu.SemaphoreType.DMA((nsem,))] * 4, start=start, mid=mid, end=end)


def _scatter_rider(srcs):
    n = len(srcs)
    nck = [_n_chunks(a.shape[1], a.shape[2] * a.dtype.itemsize) for a in srcs]
    base = [3 * sum(nck[:i]) for i in range(n)]
    nsem = 3 * sum(nck)

    def copies(ins, outs, sems):
        x, y, c = lax.axis_index("x"), lax.axis_index("y"), lax.axis_index("c")
        me = 2 * x + y
        for pi, (px, py) in enumerate([(x, 1 - y), (1 - x, y), (1 - x, 1 - y)]):
            for i in range(n):
                ch = srcs[i].shape[1] // nck[i]
                for j in range(nck[i]):
                    k = base[i] + pi * nck[i] + j
                    rs = pl.ds(j * ch, ch)
                    yield pltpu.make_async_remote_copy(
                        src_ref=ins[i].at[2 * px + py, rs], dst_ref=outs[i].at[me, rs], send_sem=sems[0].at[k],
                        recv_sem=sems[1].at[k], device_id=(px, py, c), device_id_type=MESH)

    def start(ins, outs, sems):
        for cp in copies(ins, outs, sems):
            cp.start()

    def end(ins, outs, sems):
        for cp in copies(ins, outs, sems):
            cp.wait()

    return Rider(ins=list(srcs), out_shapes=[jax.ShapeDtypeStruct(a.shape, a.dtype) for a in srcs], aliases={},
                 sems=[pltpu.SemaphoreType.DMA((nsem,))] * 2, start=start, mid=None, end=end)


def _ride_alone(rider, name):
    n_in, n_out = len(rider.ins), len(rider.out_shapes)

    def body(*refs):
        ins, outs, sems = refs[:n_in], refs[n_in:n_in + n_out], refs[n_in + n_out:]
        rider.start(ins, outs, sems)
        if rider.mid is not None:
            rider.mid(ins, outs, sems)
        rider.end(ins, outs, sems)

    any_spec = pl.BlockSpec(memory_space=pl.ANY)
    outs = pl.pallas_call(
        body, name=name, out_shape=list(rider.out_shapes), in_specs=[any_spec] * n_in, out_specs=[any_spec] * n_out,
        scratch_shapes=list(rider.sems), input_output_aliases=dict(rider.aliases),
        compiler_params=pltpu.CompilerParams(has_side_effects=True),
    )(*rider.ins)
    return list(outs)


def _swap_cores_inplace(bufs, name):
    n = len(bufs)
    nck = [_n_chunks(a.shape[2], a.shape[3] * a.dtype.itemsize) for a in bufs]
    base = [sum(a.shape[0] * k for a, k in zip(bufs[:i], nck[:i])) for i in range(n)]
    nsem = sum(a.shape[0] * k for a, k in zip(bufs, nck))

    def body(*refs):
        outs = refs[n:2 * n]
        send, recv = refs[2 * n:]
        x, y, c = lax.axis_index("x"), lax.axis_index("y"), lax.axis_index("c")

        def copies(core):
            for i in range(n):
                ch = bufs[i].shape[2] // nck[i]
                for p in range(bufs[i].shape[0]):
                    for j in range(nck[i]):
                        k = base[i] + p * nck[i] + j
                        ref = outs[i].at[p, core, pl.ds(j * ch, ch)]
                        yield pltpu.make_async_remote_copy(src_ref=ref, dst_ref=ref, send_sem=send.at[k], recv_sem=recv.at[k],
                                                           device_id=(x, y, 1 - c), device_id_type=MESH)

        for cp in copies(c):
            cp.start()
        for cp in copies(c):
            cp.wait_send()
        for cp in copies(1 - c):
            cp.wait_recv()

    any_spec = pl.BlockSpec(memory_space=pl.ANY)
    outs = pl.pallas_call(
        body, name=name, out_shape=[jax.ShapeDtypeStruct(a.shape, a.dtype) for a in bufs], in_specs=[any_spec] * n,
        out_specs=[any_spec] * n, scratch_shapes=[pltpu.SemaphoreType.DMA((nsem,))] * 2,
        input_output_aliases={i: i for i in range(n)}, compiler_params=pltpu.CompilerParams(has_side_effects=True),
    )(*bufs)
    return list(outs)


def _place_cast(ws, row0, rows, slot, slots, name):
    n = len(ws)
    C = ws[0].shape[1]
    tr = _row_block(rows, C)
    blk0 = row0 // tr

    def body(slot_ref, *refs):
        del slot_ref
        for w_ref, o_ref in zip(refs[:n], refs[n:]):
            o_ref[...] = w_ref[...].astype(BF16)

    return pl.pallas_call(
        body, name=name,
        grid_spec=pltpu.PrefetchScalarGridSpec(
            num_scalar_prefetch=1, grid=(rows // tr,), in_specs=[pl.BlockSpec((tr, C), lambda i, sr: (blk0 + i, 0))] * n,
            out_specs=[pl.BlockSpec((None, tr, C), lambda i, sr: (sr[0], i, 0))] * n),
        out_shape=[jax.ShapeDtypeStruct((slots, rows, C), BF16)] * n,
        compiler_params=_cparams("parallel"),
    )(slot.reshape(1).astype(jnp.int32), *ws)


def _swap_halves(arrs, name):
    n = len(arrs)
    S = arrs[0].shape[0]
    halves = [a.shape[1] // 2 for a in arrs]
    nck = [_n_chunks(h, a.shape[2] * a.dtype.itemsize) for h, a in zip(halves, arrs)]
    base = [S * sum(nck[:i]) for i in range(n)]
    nsem = S * sum(nck)

    def body(*refs):
        ins, outs = refs[:n], refs[n:2 * n]
        send, recv = refs[2 * n:]
        x, y, c = lax.axis_index("x"), lax.axis_index("y"), lax.axis_index("c")
        copies = []
        for i in range(n):
            ch = halves[i] // nck[i]
            for s in range(S):
                for j in range(nck[i]):
                    k = base[i] + s * nck[i] + j
                    src = ins[i].at[s, pl.ds(pl.multiple_of((1 - c) * halves[i] + j * ch, 16), ch)]
                    cp = pltpu.make_async_remote_copy(src_ref=src, dst_ref=outs[i].at[s, pl.ds(j * ch, ch)], send_sem=send.at[k],
                                                      recv_sem=recv.at[k], device_id=(x, y, 1 - c), device_id_type=MESH)
                    cp.start()
                    copies.append(cp)
        for cp in copies:
            cp.wait()

    any_spec = pl.BlockSpec(memory_space=pl.ANY)
    outs = pl.pallas_call(
        body, name=name, out_shape=[jax.ShapeDtypeStruct((S, h, a.shape[2]), a.dtype) for h, a in zip(halves, arrs)],
        in_specs=[any_spec] * n, out_specs=[any_spec] * n,
        scratch_shapes=[pltpu.SemaphoreType.DMA((nsem,))] * 2,
        compiler_params=pltpu.CompilerParams(has_side_effects=True),
    )(*arrs)
    return list(outs)


def _pair_sum(gs, rs, core, name):
    n = len(gs)
    S, rows, C = gs[0].shape
    half = rows // 2
    tr = _row_block(half, C)
    nb = half // tr

    def body(core_ref, *refs):
        del core_ref
        for g_ref, r_ref, o_ref in zip(refs[:n], refs[n:2 * n], refs[2 * n:]):
            o_ref[...] = (g_ref[...].astype(F32) + r_ref[...].astype(F32)).astype(BF16)

    blk = pl.BlockSpec((None, tr, C), lambda s, i, cr: (s, i, 0))
    mine = pl.BlockSpec((None, tr, C), lambda s, i, cr: (s, cr[0] * nb + i, 0))
    return pl.pallas_call(
        body, name=name,
        grid_spec=pltpu.PrefetchScalarGridSpec(num_scalar_prefetch=1, grid=(S, nb), in_specs=[mine] * n + [blk] * n,
                                               out_specs=[blk] * n),
        out_shape=[jax.ShapeDtypeStruct((S, half, C), BF16)] * n,
        compiler_params=_cparams("parallel", "parallel"),
    )(core.reshape(1).astype(jnp.int32), *gs, *rs)


FFN_FWD_PARTS = 1
FFN_BWD_PARTS = 2


def _row_parts(rows, parts):
    parts = parts if rows % (16 * parts) == 0 else 1
    return [pl.ds(p * (rows // parts), rows // parts) for p in range(parts)]


def _ffn_fwd(h, mod, g, w1, w3, w2, k, dm, name, rider=None):
    T, D = h.shape
    S, F = w1.shape[0], w1.shape[-2]
    tm = dm.tm
    r0 = 6 if k else 0
    grp = _grp(dm)

    def body(h_ref, mod_ref, g_ref, w1_ref, w3_ref, w2_ref, ho_ref, a_ref, b_ref, hn_ref, y_ref, hn_s, acc):
        s = pl.program_id(1)

        @pl.when(s == 0)
        def _():
            hn = _pre(h_ref[...], g_ref[...], mod_ref[0, r0:r0 + 1, :], mod_ref[0, r0 + 1:r0 + 2, :]).astype(BF16)
            hn_s[...] = hn
            hn_ref[...] = hn
            acc[...] = jnp.zeros_like(acc)

        for rows in _row_parts(tm, FFN_FWD_PARTS):
            hn = hn_s[rows, :]
            a = _dot_nt(hn, w1_ref[...])
            b = _dot_nt(hn, w3_ref[...])
            a_ref[0, rows, :] = a.astype(BF16)
            b_ref[0, rows, :] = b.astype(BF16)
            sw = (a * jax.nn.sigmoid(a) * b).astype(BF16)
            acc[rows, :] += _dot(sw, w2_ref[...])

        @pl.when(s == S - 1)
        def _():
            y = acc[...]
            y_ref[...] = y.astype(BF16)
            ho_ref[...] = h_ref[...] + 0.5 * mod_ref[0, r0 + 2:r0 + 3, :] * y

    row = pl.BlockSpec((tm, D), lambda i, s: (i, 0))
    wrow = pl.BlockSpec((None, F, D), lambda i, s: (s, 0, 0))
    ab = pl.BlockSpec((1, tm, F), lambda i, s: (s, i, 0))
    return _hosted(
        body, rider, name=name, grid=(T // tm, S),
        in_specs=[row, pl.BlockSpec((1, 9, D), lambda i, s: (grp(i), 0, 0)), pl.BlockSpec((1, D), lambda i, s: (0, 0)),
                  wrow, wrow, wrow],
        out_specs=[row, ab, ab, row, row],
        out_shape=[jax.ShapeDtypeStruct((T, D), F32), jax.ShapeDtypeStruct((S, T, F), BF16),
                   jax.ShapeDtypeStruct((S, T, F), BF16), jax.ShapeDtypeStruct((T, D), BF16),
                   jax.ShapeDtypeStruct((T, D), BF16)],
        scratch_shapes=[pltpu.VMEM((tm, D), BF16), pltpu.VMEM((tm, D), F32)],
        sem=("parallel", "arbitrary"), args=(h, mod, g, w1, w3, w2))


def _ffn_bwd(dh, h, mod, g, y, a, b, w1, w3, w2, k, dm, name, rider=None):
    T, D = h.shape
    S, F = w1.shape[0], w1.shape[-2]
    tm = dm.tm
    r0 = 6 if k else 0
    grp = _grp(dm)

    def body(dh_ref, h_ref, mod_ref, g_ref, y_ref, a_ref, b_ref, w1_ref, w3_ref, w2_ref,
             dho_ref, da_ref, db_ref, sw_ref, dy_ref, part_ref, dy_s, acc):
        s = pl.program_id(1)

        @pl.when(s == 0)
        def _():
            dy = (0.5 * mod_ref[0, r0 + 2:r0 + 3, :] * dh_ref[...]).astype(BF16)
            dy_s[...] = dy
            dy_ref[...] = dy
            acc[...] = jnp.zeros_like(acc)

        for rows in _row_parts(tm, FFN_BWD_PARTS):
            ds = _dot_nt(dy_s[rows, :], w2_ref[...]).astype(BF16)
            av = a_ref[0, rows, :]
            bv = b_ref[0, rows, :]
            sig = jax.nn.sigmoid(av)
            sil = av * sig
            sw_ref[0, rows, :] = sil * bv
            db = ds * sil
            da = ds * bv * (sig + sil * (1.0 - sig))
            da_ref[0, rows, :] = da
            db_ref[0, rows, :] = db
            acc[rows, :] += _dot(da, w1_ref[...]) + _dot(db, w3_ref[...])

        @pl.when(s == S - 1)
        def _():
            dhv = dh_ref[...]
            dhb, dshift, dscale, dg = _pre_bwd(acc[...], h_ref[...], g_ref[...], mod_ref[0, r0 + 1:r0 + 2, :])
            dho_ref[...] = dhv + dhb
            dgate = 0.5 * jnp.sum(dhv * y_ref[...].astype(F32), axis=0, keepdims=True)
            _write_part(part_ref, dshift, dscale, dgate, dg)

    row = pl.BlockSpec((tm, D), lambda i, s: (i, 0))
    wrow = pl.BlockSpec((None, F, D), lambda i, s: (s, 0, 0))
    ab = pl.BlockSpec((1, tm, F), lambda i, s: (s, i, 0))
    stf = jax.ShapeDtypeStruct((S, T, F), BF16)
    return _hosted(
        body, rider, name=name, grid=(T // tm, S),
        in_specs=[row, row, pl.BlockSpec((1, 9, D), lambda i, s: (grp(i), 0, 0)), pl.BlockSpec((1, D), lambda i, s: (0, 0)),
                  row, ab, ab, wrow, wrow, wrow],
        out_specs=[row, ab, ab, ab, row, pl.BlockSpec((1, 8, D), lambda i, s: (i, 0, 0))],
        out_shape=[jax.ShapeDtypeStruct((T, D), F32), stf, stf, stf, jax.ShapeDtypeStruct((T, D), BF16),
                   jax.ShapeDtypeStruct((T // tm, 8, D), F32)],
        scratch_shapes=[pltpu.VMEM((tm, D), BF16), pltpu.VMEM((tm, D), F32)],
        sem=("parallel", "arbitrary"), args=(dh, h, mod, g, y, a, b, w1, w3, w2))


def _mm_tn(a, b, a_spec, b_spec, out_shape, out_spec, grid, name):
    nk = grid[-1]
    kax = len(grid) - 1
    blk = tuple(d for d in out_spec.block_shape if d is not None)

    def body(a_ref, b_ref, o_ref, acc):
        kk = pl.program_id(kax)

        @pl.when(kk == 0)
        def _():
            acc[...] = jnp.zeros_like(acc)

        av = a_ref[...].astype(BF16)
        if len(b_ref.shape) == 3:
            for p in range(b_ref.shape[0]):
                acc[p] += _dot_tn(av, b_ref[p].astype(BF16))
        else:
            acc[...] += _dot_tn(av, b_ref[...].astype(BF16))

        @pl.when(kk == nk - 1)
        def _():
            o_ref[...] = acc[...].astype(o_ref.dtype)

    return pl.pallas_call(
        body, name=name, grid=grid,
        in_specs=[a_spec, b_spec], out_specs=out_spec, out_shape=jax.ShapeDtypeStruct(out_shape, BF16),
        scratch_shapes=[pltpu.VMEM(blk, F32)],
        compiler_params=_cparams(*(["parallel"] * kax + ["arbitrary"])),
    )(a, b)


def _sc_w_in_specs(D, j):
    wq = D // N_CHIPS

    def spec(piece):
        col = lambda q: piece * N_CHIPS + q
        return pl.BlockSpec((None, None, D, wq), lambda i, q: (col(q) // 3, j, 0, col(q) % 3))

    return [spec(0), spec(1), spec(2)]


def _sc_in_fwd(h, mod, g, w_in, j, dm, name, rider=None):
    T, D = h.shape
    tm = dm.tm
    wq = D // N_CHIPS
    grp = _grp(dm)

    def body(h_ref, mod_ref, g_ref, wb_ref, wc_ref, wu_ref, p_ref, hn_ref, hn_s):
        @pl.when(pl.program_id(1) == 0)
        def _():
            hn = _pre(h_ref[...], g_ref[...], mod_ref[0, 3:4, :], mod_ref[0, 4:5, :]).astype(BF16)
            hn_s[...] = hn
            hn_ref[...] = hn

        for piece, w_ref in enumerate((wb_ref, wc_ref, wu_ref)):
            p_ref[piece] = _dot(hn_s[...], w_ref[...])

    row = pl.BlockSpec((tm, D), lambda i, q: (i, 0))
    return _hosted(
        body, rider, name=name, grid=(T // tm, N_CHIPS),
        in_specs=[row, pl.BlockSpec((1, 9, D), lambda i, q: (grp(i), 0, 0)), pl.BlockSpec((1, D), lambda i, q: (0, 0))]
        + _sc_w_in_specs(D, j),
        out_specs=[pl.BlockSpec((3, tm, wq), lambda i, q: (0, i, q)), row],
        out_shape=[jax.ShapeDtypeStruct((3, T, D), F32), jax.ShapeDtypeStruct((T, D), BF16)],
        scratch_shapes=[pltpu.VMEM((tm, D), BF16)],
        sem=("parallel", "arbitrary"), args=(h, mod, g, w_in, w_in, w_in))


def _conv_cols(dm):
    return 256 if dm.D % 256 == 0 else 128


def _seg_masks(r, dm):
    bn = dm.B * dm.N
    lat = r < bn
    off = jnp.where(lat, lax.rem(r, dm.N), lax.rem(r - bn, dm.CTX))
    seg = jnp.where(lat, dm.N, dm.CTX)
    inside = (r >= 0) & (r < dm.T)
    return ((off != 0) & inside).astype(F32), ((off != seg - 1) & inside).astype(F32)


def _conv_specs(dm):
    tb, cb, nr8 = dm.tm, _conv_cols(dm), dm.T // 8
    prev8 = lambda c, i: jnp.maximum(i * (tb // 8) - 1, 0)
    next8 = lambda c, i: jnp.minimum((i + 1) * (tb // 8), nr8 - 1)
    return dict(
        tb=tb, cb=cb,
        p=pl.BlockSpec((3, tb, cb), lambda c, i: (0, i, c)),
        p_prev=pl.BlockSpec((3, 8, cb), lambda c, i: (0, prev8(c, i), c)),
        p_next=pl.BlockSpec((3, 8, cb), lambda c, i: (0, next8(c, i), c)),
        row=pl.BlockSpec((tb, cb), lambda c, i: (i, c)),
        row_prev=pl.BlockSpec((8, cb), lambda c, i: (prev8(c, i), c)),
        row_next=pl.BlockSpec((8, cb), lambda c, i: (next8(c, i), c)),
        w=pl.BlockSpec((3, cb), lambda c, i: (0, c)),
    )


def _shift_rows(x, before, after, tb):
    rid = lax.broadcasted_iota(jnp.int32, x.shape, 0)
    down = jnp.where(rid == 0, before, pltpu.roll(x, 1, 0))
    up = jnp.where(rid == tb - 1, after, pltpu.roll(x, tb - 1, 0))
    return down, up


def _conv_fwd(p, wc, dm, name):
    T, D = dm.T, dm.D
    sp = _conv_specs(dm)
    tb, cb = sp["tb"], sp["cb"]

    def body(p_ref, pp_ref, pn_ref, w_ref, z_ref):
        r = pl.program_id(1) * tb + lax.broadcasted_iota(jnp.int32, (tb, cb), 0)
        mp, mn = _seg_masks(r, dm)
        cu = p_ref[1] * p_ref[2]
        prev, nxt = _shift_rows(cu, pp_ref[1, 7:8, :] * pp_ref[2, 7:8, :], pn_ref[1, 0:1, :] * pn_ref[2, 0:1, :], tb)
        conv = w_ref[0:1, :] * (prev * mp) + w_ref[1:2, :] * cu + w_ref[2:3, :] * (nxt * mn)
        z_ref[...] = (p_ref[0] * conv).astype(BF16)

    return pl.pallas_call(
        body, name=name, grid=(D // cb, T // tb),
        in_specs=[sp["p"], sp["p_prev"], sp["p_next"], sp["w"]], out_specs=sp["row"],
        out_shape=jax.ShapeDtypeStruct((T, D), BF16),
        compiler_params=_cparams("parallel", "parallel"),
    )(p, p, p, wc)


def _conv_bwd(dz, p, wc, dm, name):
    T, D = dm.T, dm.D
    sp = _conv_specs(dm)
    tb, cb = sp["tb"], sp["cb"]

    def body(dz_ref, dzp_ref, dzn_ref, p_ref, pp_ref, pn_ref, w_ref, dp_ref, dw_ref):
        i = pl.program_id(1)
        r = i * tb + lax.broadcasted_iota(jnp.int32, (tb, cb), 0)
        mp, mn = _seg_masks(r, dm)
        rb = i * tb + lax.broadcasted_iota(jnp.int32, (1, cb), 0)
        _, mn_before = _seg_masks(rb - 1, dm)
        mp_after, _ = _seg_masks(rb + tb, dm)
        bg, cg, u = p_ref[0], p_ref[1], p_ref[2]
        cu = cg * u
        prev, nxt = _shift_rows(cu, pp_ref[1, 7:8, :] * pp_ref[2, 7:8, :], pn_ref[1, 0:1, :] * pn_ref[2, 0:1, :], tb)
        prev = prev * mp
        nxt = nxt * mn
        w0, w1, w2 = w_ref[0:1, :], w_ref[1:2, :], w_ref[2:3, :]
        conv = w0 * prev + w1 * cu + w2 * nxt
        dz = dz_ref[...]
        dp_ref[0] = dz * conv
        dconv = dz * bg

        @pl.when(i == 0)
        def _():
            dw_ref[...] = jnp.zeros_like(dw_ref)

        dw_ref[0:1, :] += jnp.sum(dconv * prev, axis=0, keepdims=True)
        dw_ref[1:2, :] += jnp.sum(dconv * cu, axis=0, keepdims=True)
        dw_ref[2:3, :] += jnp.sum(dconv * nxt, axis=0, keepdims=True)
        dconv_before = dzp_ref[7:8, :] * pp_ref[0, 7:8, :] * mn_before
        dconv_after = dzn_ref[0:1, :] * pn_ref[0, 0:1, :] * mp_after
        from_prev, _ = _shift_rows(dconv * mn, dconv_before, dconv_after, tb)
        _, from_next = _shift_rows(dconv * mp, dconv_before, dconv_after, tb)
        dcu = w1 * dconv + w0 * from_next + w2 * from_prev
        dp_ref[1] = dcu * u
        dp_ref[2] = dcu * cg

    return pl.pallas_call(
        body, name=name, grid=(D // cb, T // tb),
        in_specs=[sp["row"], sp["row_prev"], sp["row_next"], sp["p"], sp["p_prev"], sp["p_next"], sp["w"]],
        out_specs=[sp["p"], sp["w"]],
        out_shape=[jax.ShapeDtypeStruct((3, T, D), F32), jax.ShapeDtypeStruct((3, D), F32)],
        compiler_params=_cparams("parallel", "arbitrary"),
    )(dz, dz, dz, p, p, p, wc)


def _out_fwd(z, w, h, mod, j, dm, name):
    T, D = h.shape
    K = z.shape[1]
    tm = dm.tm
    grp = _grp(dm)

    def body(z_ref, w_ref, h_ref, mod_ref, ho_ref, y_ref):
        y = _dot(z_ref[...], w_ref[...])
        y_ref[...] = y.astype(BF16)
        ho_ref[...] = h_ref[...] + mod_ref[0, 5:6, :] * y

    row = pl.BlockSpec((tm, D), lambda i: (i, 0))
    return pl.pallas_call(
        body, name=name, grid=(T // tm,),
        in_specs=[pl.BlockSpec((tm, K), lambda i: (i, 0)), pl.BlockSpec((None, K, D), lambda i: (j, 0, 0)), row,
                  pl.BlockSpec((1, 9, D), lambda i: (grp(i), 0, 0))],
        out_specs=[row, row],
        out_shape=[jax.ShapeDtypeStruct((T, D), F32), jax.ShapeDtypeStruct((T, D), BF16)],
        compiler_params=_cparams("parallel"),
    )(z, w, h, mod)


def _out_bwd(dh, y, w, mod, j, dm, name):
    T, D = dh.shape
    K = w.shape[1]
    tm = dm.tm
    grp = _grp(dm)

    def body(dh_ref, y_ref, w_ref, mod_ref, dy_ref, dz_ref, part_ref):
        dhv = dh_ref[...]
        dy = (mod_ref[0, 5:6, :] * dhv).astype(BF16)
        dy_ref[...] = dy
        dz_ref[...] = _dot_nt(dy, w_ref[...])
        _write_part(part_ref, dgate=jnp.sum(dhv * y_ref[...].astype(F32), axis=0, keepdims=True))

    row = pl.BlockSpec((tm, D), lambda i: (i, 0))
    return pl.pallas_call(
        body, name=name, grid=(T // tm,),
        in_specs=[row, row, pl.BlockSpec((None, K, D), lambda i: (j, 0, 0)), pl.BlockSpec((1, 9, D), lambda i: (grp(i), 0, 0))],
        out_specs=[row, pl.BlockSpec((tm, K), lambda i: (i, 0)), pl.BlockSpec((1, 8, D), lambda i: (i, 0, 0))],
        out_shape=[jax.ShapeDtypeStruct((T, D), BF16), jax.ShapeDtypeStruct((T, K), F32),
                   jax.ShapeDtypeStruct((T // tm, 8, D), F32)],
        compiler_params=_cparams("parallel"),
    )(dh, y, w, mod)


def _sc_in_bwd(dh, dp, h, mod, g, w_in, j, dm, name):
    T, D = h.shape
    tm = dm.tm
    wq = D // N_CHIPS
    nq = N_CHIPS
    grp = _grp(dm)

    def body(dh_ref, dp_ref, h_ref, mod_ref, g_ref, wb_ref, wc_ref, wu_ref, dho_ref, part_ref, acc):
        q = pl.program_id(1)

        @pl.when(q == 0)
        def _():
            acc[...] = jnp.zeros_like(acc)

        acc[...] += sum(_dot_nt(dp_ref[piece].astype(BF16), w_ref[...]) for piece, w_ref in enumerate((wb_ref, wc_ref, wu_ref)))

        @pl.when(q == nq - 1)
        def _():
            dhb, dshift, dscale, dg = _pre_bwd(acc[...], h_ref[...], g_ref[...], mod_ref[0, 4:5, :])
            dho_ref[...] = dh_ref[...] + dhb
            _write_part(part_ref, dshift, dscale, None, dg)

    row = pl.BlockSpec((tm, D), lambda i, q: (i, 0))
    return pl.pallas_call(
        body, name=name, grid=(T // tm, nq),
        in_specs=[row, pl.BlockSpec((3, tm, wq), lambda i, q: (0, i, q)), row,
                  pl.BlockSpec((1, 9, D), lambda i, q: (grp(i), 0, 0)), pl.BlockSpec((1, D), lambda i, q: (0, 0))]
        + _sc_w_in_specs(D, j),
        out_specs=[row, pl.BlockSpec((1, 8, D), lambda i, q: (i, 0, 0))],
        out_shape=[jax.ShapeDtypeStruct((T, D), F32), jax.ShapeDtypeStruct((T // tm, 8, D), F32)],
        scratch_shapes=[pltpu.VMEM((tm, D), F32)],
        compiler_params=_cparams("parallel", "arbitrary"),
    )(dh, dp, h, mod, g, w_in, w_in, w_in)


def _rope(t, c, s1, s2):
    return t * c + pltpu.roll(t, HEAD_PAD - 16, 1) * s1 + pltpu.roll(t, 16, 1) * s2


def _rope_t(dy, c, s1, s2):
    return dy * c + pltpu.roll(dy * s1, 16, 1) + pltpu.roll(dy * s2, HEAD_PAD - 16, 1)


def _mla_heads_fwd(z, g_ref, wuq_ref, wukv_ref):
    cq, ckv, krp = z[:, :Q_LORA], z[:, Q_LORA:Q_LORA + KV_LORA], z[:, Q_LORA + KV_LORA:]
    cqh, rq = _rms(cq, Q_LORA)
    ckvh, rkv = _rms(ckv, KV_LORA)
    cqn = (cqh * g_ref[0:1, :]).astype(BF16)
    ckvn = (ckvh * g_ref[1:2, :KV_LORA]).astype(BF16)
    qraw = _dot(cqn, wuq_ref[...])
    kvraw = _dot(ckvn, wukv_ref[...])
    return dict(krp=krp, cqh=cqh, rq=rq, ckvh=ckvh, rkv=rkv, cqn=cqn, ckvn=ckvn, qraw=qraw, kvraw=kvraw)


def _mla_proj_fwd(h, mod, g, gains, tabs, w_a, w_uq, w_ukv, j, dm, name):
    T, D = h.shape
    tm = min(dm.tm, 256)
    grp = lambda i: jnp.minimum(i // (dm.N // tm), dm.B)
    HP = HEAD_PAD

    def body(h_ref, mod_ref, g_ref, gn_ref, tab_ref, wa_ref, wuq_ref, wukv_ref, hn_ref, q_ref, k_ref, v_ref):
        hn = _pre(h_ref[...], g_ref[...], mod_ref[0, 3:4, :], mod_ref[0, 4:5, :]).astype(BF16)
        hn_ref[...] = hn
        f = _mla_heads_fwd(_dot(hn, wa_ref[...]), gn_ref, wuq_ref, wukv_ref)
        c, s1, s2 = tab_ref[0], tab_ref[1], tab_ref[2]
        for hd in range(HEADS):
            qh, _ = _rms(f["qraw"][:, hd * HP:(hd + 1) * HP], QK_HEAD)
            q_ref[:, hd * HP:(hd + 1) * HP] = (_rope(qh * gn_ref[2:3, :], c, s1, s2) * QK_SCALE).astype(BF16)
            kpre = jnp.concatenate([f["kvraw"][:, hd * HP:hd * HP + QK_NOPE], f["krp"]], axis=1)
            kh, _ = _rms(kpre, QK_HEAD)
            k_ref[:, hd * HP:(hd + 1) * HP] = _rope(kh * gn_ref[3:4, :], c, s1, s2).astype(BF16)
            v_ref[:, hd * V_HEAD:(hd + 1) * V_HEAD] = f["kvraw"][:, hd * HP + QK_NOPE:(hd + 1) * HP].astype(BF16)

    row = pl.BlockSpec((tm, D), lambda i: (i, 0))
    HQ = HEADS * HP
    return pl.pallas_call(
        body, name=name, grid=(T // tm,),
        in_specs=[row, pl.BlockSpec((1, 9, D), lambda i: (grp(i), 0, 0)), pl.BlockSpec((1, D), lambda i: (0, 0)),
                  pl.BlockSpec((None, 8, HP), lambda i: (j, 0, 0)), pl.BlockSpec((3, tm, HP), lambda i: (0, i, 0)),
                  pl.BlockSpec((None, D, 512), lambda i: (j, 0, 0)), pl.BlockSpec((None, Q_LORA, HQ), lambda i: (j, 0, 0)),
                  pl.BlockSpec((None, KV_LORA, HQ), lambda i: (j, 0, 0))],
        out_specs=[row, pl.BlockSpec((tm, HQ), lambda i: (i, 0)), pl.BlockSpec((tm, HQ), lambda i: (i, 0)),
                   pl.BlockSpec((tm, HEADS * V_HEAD), lambda i: (i, 0))],
        out_shape=[jax.ShapeDtypeStruct((T, D), BF16), jax.ShapeDtypeStruct((T, HQ), BF16),
                   jax.ShapeDtypeStruct((T, HQ), BF16), jax.ShapeDtypeStruct((T, HEADS * V_HEAD), BF16)],
        compiler_params=_cparams("parallel"),
    )(h, mod, g, gains, tabs, w_a, w_uq, w_ukv)


def _mla_proj_bwd(dh, dq, dkl, dkc, dvl, dvc, h, mod, g, gains, tabs, w_a, w_uq, w_ukv, j, dm, name):
    T, D = h.shape
    tm = min(dm.tm, 128)
    nblk = T // tm
    grp = lambda i: jnp.minimum(i // (dm.N // tm), dm.B)
    HP = HEAD_PAD
    HQ = HEADS * HP

    nlat = dm.B * dm.N // tm

    def body(dh_ref, dq_ref, dkl_ref, dkc_ref, dvl_ref, dvc_ref, h_ref, mod_ref, g_ref, gn_ref, tab_ref, wa_ref, wuq_ref, wukv_ref,
             dho_ref, part_ref, gwa_ref, gwuq_ref, gwukv_ref, dgn_ref, dqraw_s, dkvraw_s):
        i = pl.program_id(0)
        pick = lambda lat_ref, ctx_ref, cols: jnp.where(i < nlat, lat_ref[:, cols], ctx_ref[:, cols])

        @pl.when(i == 0)
        def _():
            gwa_ref[...] = jnp.zeros_like(gwa_ref)
            gwuq_ref[...] = jnp.zeros_like(gwuq_ref)
            gwukv_ref[...] = jnp.zeros_like(gwukv_ref)
            dgn_ref[...] = jnp.zeros_like(dgn_ref)

        hv = h_ref[...]
        hn = _pre(hv, g_ref[...], mod_ref[0, 3:4, :], mod_ref[0, 4:5, :]).astype(BF16)
        f = _mla_heads_fwd(_dot(hn, wa_ref[...]), gn_ref, wuq_ref, wukv_ref)
        c, s1, s2 = tab_ref[0], tab_ref[1], tab_ref[2]
        gq, gk = gn_ref[2:3, :], gn_ref[3:4, :]
        dgq = jnp.zeros((1, HP), F32)
        dgk = jnp.zeros((1, HP), F32)
        dkrp = jnp.zeros((tm, HP - QK_NOPE), F32)
        for hd in range(HEADS):
            qh, rq = _rms(f["qraw"][:, hd * HP:(hd + 1) * HP], QK_HEAD)
            dqn = _rope_t(dq_ref[:, hd * HP:(hd + 1) * HP] * QK_SCALE, c, s1, s2)
            dgq = dgq + jnp.sum(dqn * qh, axis=0, keepdims=True)
            dqraw_s[:, hd * HP:(hd + 1) * HP] = _rms_bwd(dqn * gq, qh, rq, QK_HEAD)
            kpre = jnp.concatenate([f["kvraw"][:, hd * HP:hd * HP + QK_NOPE], f["krp"]], axis=1)
            kh, rk = _rms(kpre, QK_HEAD)
            dkn = _rope_t(pick(dkl_ref, dkc_ref, slice(hd * HP, (hd + 1) * HP)), c, s1, s2)
            dgk = dgk + jnp.sum(dkn * kh, axis=0, keepdims=True)
            dkpre = _rms_bwd(dkn * gk, kh, rk, QK_HEAD)
            dkvraw_s[:, hd * HP:hd * HP + QK_NOPE] = dkpre[:, :QK_NOPE]
            dkrp = dkrp + dkpre[:, QK_NOPE:]
            dkvraw_s[:, hd * HP + QK_NOPE:(hd + 1) * HP] = pick(dvl_ref, dvc_ref, slice(hd * V_HEAD, (hd + 1) * V_HEAD))
        dqraw = dqraw_s[...].astype(BF16)
        dkvraw = dkvraw_s[...].astype(BF16)
        gwuq_ref[...] += _dot_tn(f["cqn"], dqraw)
        gwukv_ref[...] += _dot_tn(f["ckvn"], dkvraw)
        dcqn = _dot_nt(dqraw, wuq_ref[...])
        dckvn = _dot_nt(dkvraw, wukv_ref[...])
        dgqa = jnp.sum(dcqn * f["cqh"], axis=0, keepdims=True)
        dgkva = jnp.sum(dckvn * f["ckvh"], axis=0, keepdims=True)
        dcq = _rms_bwd(dcqn * gn_ref[0:1, :], f["cqh"], f["rq"], Q_LORA)
        dckv = _rms_bwd(dckvn * gn_ref[1:2, :KV_LORA], f["ckvh"], f["rkv"], KV_LORA)
        dz = jnp.concatenate([dcq, dckv, dkrp], axis=1).astype(BF16)
        gwa_ref[...] += _dot_tn(hn, dz)
        dhn = _dot_nt(dz, wa_ref[...])
        dhb, dshift, dscale, dg = _pre_bwd(dhn, hv, g_ref[...], mod_ref[0, 4:5, :])
        dho_ref[...] = dh_ref[...] + dhb
        _write_part(part_ref, dshift, dscale, None, dg)
        dgn_ref[0:1, :] += dgqa
        dgn_ref[1:2, :KV_LORA] += dgkva
        dgn_ref[2:3, :] += dgq
        dgn_ref[3:4, :] += dgk

    row = pl.BlockSpec((tm, D), lambda i: (i, 0))
    wide = pl.BlockSpec((tm, HQ), lambda i: (i, 0))
    const2 = lambda i: (0, 0)
    return pl.pallas_call(
        body, name=name, grid=(nblk,),
        in_specs=[row, wide, pl.BlockSpec((tm, HQ), lambda i: (jnp.minimum(i, nlat - 1), 0)),
                  pl.BlockSpec((tm, HQ), lambda i: (jnp.maximum(i - nlat, 0), 0)),
                  pl.BlockSpec((tm, HEADS * V_HEAD), lambda i: (jnp.minimum(i, nlat - 1), 0)),
                  pl.BlockSpec((tm, HEADS * V_HEAD), lambda i: (jnp.maximum(i - nlat, 0), 0)), row,
                  pl.BlockSpec((1, 9, D), lambda i: (grp(i), 0, 0)), pl.BlockSpec((1, D), const2),
                  pl.BlockSpec((None, 8, HP), lambda i: (j, 0, 0)), pl.BlockSpec((3, tm, HP), lambda i: (0, i, 0)),
                  pl.BlockSpec((None, D, 512), lambda i: (j, 0, 0)), pl.BlockSpec((None, Q_LORA, HQ), lambda i: (j, 0, 0)),
                  pl.BlockSpec((None, KV_LORA, HQ), lambda i: (j, 0, 0))],
        out_specs=[row, pl.BlockSpec((1, 8, D), lambda i: (i, 0, 0)), pl.BlockSpec((D, 512), const2),
                   pl.BlockSpec((Q_LORA, HQ), const2), pl.BlockSpec((KV_LORA, HQ), const2), pl.BlockSpec((8, HP), const2)],
        out_shape=[jax.ShapeDtypeStruct((T, D), F32), jax.ShapeDtypeStruct((nblk, 8, D), F32),
                   jax.ShapeDtypeStruct((D, 512), F32), jax.ShapeDtypeStruct((Q_LORA, HQ), F32),
                   jax.ShapeDtypeStruct((KV_LORA, HQ), F32), jax.ShapeDtypeStruct((8, HP), F32)],
        scratch_shapes=[pltpu.VMEM((tm, HQ), F32), pltpu.VMEM((tm, HQ), F32)],
        compiler_params=_cparams("arbitrary"),
    )(dh, dq, dkl, dkc, dvl, dvc, h, mod, g, gains, tabs, w_a, w_uq, w_ukv)


def _attn_specs(dm):
    tq = dm.CTX
    nq = dm.N // tq
    cblk0 = dm.B * nq
    HP = HEAD_PAD
    qrow = lambda b, i: jnp.where(i < nq, b * nq + i, cblk0 + b)
    return dict(
        tq=tq, nq=nq,
        q=pl.BlockSpec((tq, HP), lambda b, hd, i: (qrow(b, i), hd)),
        k_lat=pl.BlockSpec((dm.N, HP), lambda b, hd, i: (b, hd)),
        k_ctx=pl.BlockSpec((tq, HP), lambda b, hd, i: (cblk0 + b, hd)),
        v_lat=pl.BlockSpec((dm.N, V_HEAD), lambda b, hd, i: (b, hd)),
        v_ctx=pl.BlockSpec((tq, V_HEAD), lambda b, hd, i: (cblk0 + b, hd)),
        o=pl.BlockSpec((tq, V_HEAD), lambda b, hd, i: (qrow(b, i), hd)),
    )


def _attn_exp(q, keys, first_off=None):
    s = [_dot_nt(q, kk) for kk in keys]
    if first_off is not None:
        s[0] = s[0] + first_off
    m = functools.reduce(jnp.maximum, [jnp.max(x, axis=-1, keepdims=True) for x in s])
    e = [jnp.exp(x - m) for x in s]
    return e, 1.0 / sum(jnp.sum(x, axis=-1, keepdims=True) for x in e)


def _attn_fwd(q, k, v, dm, name, rider=None):
    T = dm.T
    sp = _attn_specs(dm)
    nq = sp["nq"]

    def body(q_ref, kl_ref, kc_ref, vl_ref, vc_ref, o_ref):
        i = pl.program_id(2)

        @pl.when(i < nq)
        def _():
            (el, ec), inv = _attn_exp(q_ref[...], [kl_ref[...], kc_ref[...]])
            o_ref[...] = ((_dot(el.astype(BF16), vl_ref[...]) + _dot(ec.astype(BF16), vc_ref[...])) * inv).astype(BF16)

        @pl.when(i == nq)
        def _():
            (ec,), inv = _attn_exp(q_ref[...], [kc_ref[...]])
            o_ref[...] = (_dot(ec.astype(BF16), vc_ref[...]) * inv).astype(BF16)

    (o,), got = _hosted(
        body, rider, name=name, grid=(dm.B, HEADS, nq + 1),
        in_specs=[sp["q"], sp["k_lat"], sp["k_ctx"], sp["v_lat"], sp["v_ctx"]], out_specs=[sp["o"]],
        out_shape=[jax.ShapeDtypeStruct((T, HEADS * V_HEAD), BF16)], scratch_shapes=[],
        sem=("parallel", "parallel", "arbitrary"), args=(q, k, k, v, v))
    return o, got


def _attn_bwd(q, k, v, o, do, dm, name):
    T = dm.T
    sp = _attn_specs(dm)
    nq, tq = sp["nq"], sp["tq"]
    HP, HQ, HV = HEAD_PAD, HEADS * HEAD_PAD, HEADS * V_HEAD

    def body(q_ref, kl_ref, kc_ref, vl_ref, vc_ref, o_ref, do_ref, dq_ref, dkl_ref, dkc_ref, dvl_ref, dvc_ref):
        i = pl.program_id(2)

        @pl.when(i == 0)
        def _():
            dkl_ref[...] = jnp.zeros_like(dkl_ref)
            dkc_ref[...] = jnp.zeros_like(dkc_ref)
            dvl_ref[...] = jnp.zeros_like(dvl_ref)
            dvc_ref[...] = jnp.zeros_like(dvc_ref)

        qv = q_ref[...]
        dov = do_ref[...]
        dob = dov.astype(BF16)
        delta = jnp.sum(dov * o_ref[...].astype(F32), axis=-1, keepdims=True)
        (el, ec), inv = _attn_exp(qv, [kl_ref[...], kc_ref[...]], jnp.where(i == nq, NEG, 0.0))
        pl_, pc = el * inv, ec * inv
        dsl = (pl_ * (_dot_nt(dob, vl_ref[...]) - delta)).astype(BF16)
        dsc = (pc * (_dot_nt(dob, vc_ref[...]) - delta)).astype(BF16)
        dq_ref[...] = _dot(dsl, kl_ref[...]) + _dot(dsc, kc_ref[...])
        dkl_ref[...] += _dot_tn(dsl, qv)
        dkc_ref[...] += _dot_tn(dsc, qv)
        dvl_ref[...] += _dot_tn(pl_.astype(BF16), dob)
        dvc_ref[...] += _dot_tn(pc.astype(BF16), dob)

    return pl.pallas_call(
        body, name=name, grid=(dm.B, HEADS, nq + 1),
        in_specs=[sp["q"], sp["k_lat"], sp["k_ctx"], sp["v_lat"], sp["v_ctx"], sp["o"], sp["o"]],
        out_specs=[sp["q"], sp["k_lat"], pl.BlockSpec((tq, HP), lambda b, hd, i: (b, hd)),
                   sp["v_lat"], pl.BlockSpec((tq, V_HEAD), lambda b, hd, i: (b, hd))],
        out_shape=[jax.ShapeDtypeStruct((T, HQ), F32), jax.ShapeDtypeStruct((dm.B * dm.N, HQ), F32),
                   jax.ShapeDtypeStruct((dm.B * dm.CTX, HQ), F32), jax.ShapeDtypeStruct((dm.B * dm.N, HV), F32),
                   jax.ShapeDtypeStruct((dm.B * dm.CTX, HV), F32)],
        compiler_params=_cparams("parallel", "parallel", "arbitrary"),
    )(q, k, k, v, v, o, do)


def _loss_grad(h, target, dm, name):
    T, D = h.shape
    tm = dm.tm
    nlat = dm.B * dm.N // tm

    def body(h_ref, t_ref, dh_ref, ls_ref):
        lat = (pl.program_id(0) < nlat).astype(F32)
        diff = (h_ref[...] - t_ref[...]) * lat
        dh_ref[...] = diff * (1.0 / D)
        ls_ref[...] = jnp.zeros(ls_ref.shape, F32) + (0.5 / D) * jnp.sum(diff * diff)

    return pl.pallas_call(
        body, name=name, grid=(T // tm,),
        in_specs=[pl.BlockSpec((tm, D), lambda i: (i, 0)), pl.BlockSpec((tm, D), lambda i: (jnp.minimum(i, nlat - 1), 0))],
        out_specs=[pl.BlockSpec((tm, D), lambda i: (i, 0)), pl.BlockSpec((1, 8, 128), lambda i: (i, 0, 0))],
        out_shape=[jax.ShapeDtypeStruct((T, D), F32), jax.ShapeDtypeStruct((T // tm, 8, 128), F32)],
        compiler_params=_cparams("parallel"),
    )(h, target)


def _col_block(cols, target=1152):
    return max(t for t in range(128, min(cols, target) + 1, 128) if cols % t == 0)


def _mod_fwd(cond, w_mod, b_mod, name):
    L, D, C = w_mod.shape
    R = cond.shape[0]
    cb = _col_block(C)

    def body(c_ref, w_ref, b_ref, o_ref):
        cv = c_ref[...]
        sc = (cv * jax.nn.sigmoid(cv)).astype(BF16)
        o_ref[...] = _dot(sc, w_ref[...].astype(BF16)) + b_ref[...]

    return pl.pallas_call(
        body, name=name, grid=(L, C // cb),
        in_specs=[pl.BlockSpec((R, D), lambda l, c: (0, 0)), pl.BlockSpec((None, D, cb), lambda l, c: (l, 0, c)),
                  pl.BlockSpec((None, 1, cb), lambda l, c: (l, 0, c))],
        out_specs=pl.BlockSpec((None, R, cb), lambda l, c: (l, 0, c)),
        out_shape=jax.ShapeDtypeStruct((L, R, C), F32),
        compiler_params=_cparams("parallel", "parallel"),
    )(cond, w_mod, b_mod)


def _mod_bwd(cond, dmod, w_mod, name):
    L, D, C = w_mod.shape
    R = cond.shape[0]
    cb = _col_block(C)
    nc = C // cb

    def body(c_ref, dm_ref, w_ref, gw_ref, ds_ref):
        cv = c_ref[...]
        sc = (cv * jax.nn.sigmoid(cv)).astype(BF16)
        dmv = dm_ref[...].astype(BF16)
        gw_ref[...] = _dot_tn(sc, dmv)
        part = _dot_nt(dmv, w_ref[...].astype(BF16))

        @pl.when(pl.program_id(1) == 0)
        def _():
            ds_ref[...] = part

        @pl.when(pl.program_id(1) > 0)
        def _():
            ds_ref[...] += part

    return pl.pallas_call(
        body, name=name, grid=(L, nc),
        in_specs=[pl.BlockSpec((R, D), lambda l, c: (0, 0)), pl.BlockSpec((None, R, cb), lambda l, c: (l, 0, c)),
                  pl.BlockSpec((None, D, cb), lambda l, c: (l, 0, c))],
        out_specs=[pl.BlockSpec((None, D, cb), lambda l, c: (l, 0, c)), pl.BlockSpec((None, R, D), lambda l, c: (l, 0, 0))],
        out_shape=[jax.ShapeDtypeStruct((L, D, C), F32), jax.ShapeDtypeStruct((L, R, D), F32)],
        compiler_params=_cparams("parallel", "arbitrary"),
    )(cond, dmod, w_mod)


def _row_block(rows, cols, budget=1 << 20):
    best = None
    for t in range(16, rows + 1, 16):
        if rows % t == 0 and t * cols * 4 <= budget:
            best = t
    return best if best is not None else rows


def _sum_slots(recvs, owns, chip, core, bufs, pieces, piece, name):
    n = len(recvs)
    S, R, C = recvs[0].shape
    tr = _row_block(R, C, budget=512 << 10)

    def body(ids_ref, *refs):
        for r_ref, p_ref, o_ref in zip(refs[:n], refs[n:2 * n], refs[-n:]):
            acc = None
            for s in range(S):
                v = jnp.where(ids_ref[0] == s, p_ref[s], r_ref[s]).astype(F32)
                acc = v if acc is None else acc + v
            o_ref[...] = acc

    blk = pl.BlockSpec((S, tr, C), lambda i, ids: (0, i, 0))
    held = [] if bufs is None else list(bufs)
    return pl.pallas_call(
        body, name=name,
        grid_spec=pltpu.PrefetchScalarGridSpec(
            num_scalar_prefetch=1, grid=(R // tr,), in_specs=[blk] * (2 * n) + [pl.BlockSpec(memory_space=pl.ANY)] * len(held),
            out_specs=[pl.BlockSpec((None, None, tr, C), lambda i, ids: (piece, ids[1], i, 0))] * n),
        out_shape=[jax.ShapeDtypeStruct((pieces, 2, R, C), F32)] * n,
        input_output_aliases={1 + 2 * n + i: i for i in range(len(held))}, compiler_params=_cparams("parallel"),
    )(jnp.stack([chip, core]).astype(jnp.int32), *recvs, *owns, *held)


def _adamw(w, gs, m, v, name):
    ng = len(gs)
    R, C = w.shape
    tr = _row_block(R, C)
    c1 = 1.0 / (1.0 - ADAM_B1 ** ADAM_STEP)
    c2 = 1.0 / (1.0 - ADAM_B2 ** ADAM_STEP)

    def body(w_ref, *refs):
        m_ref, v_ref, g_ref, d_ref, mo_ref, vo_ref = refs[ng:]
        g = refs[0][...]
        for g_more in refs[1:ng]:
            g = g + g_more[...]
        g_ref[...] = g
        mn = ADAM_B1 * m_ref[...] + (1.0 - ADAM_B1) * g
        vn = ADAM_B2 * v_ref[...] + (1.0 - ADAM_B2) * (g * g)
        mo_ref[...] = mn
        vo_ref[...] = vn
        d_ref[...] = -ADAM_LR * ((mn * c1) / (jnp.sqrt(vn * c2) + ADAM_EPS) + ADAM_WD * w_ref[...])

    blk = pl.BlockSpec((tr, C), lambda i: (i, 0))
    sd = jax.ShapeDtypeStruct((R, C), F32)
    return pl.pallas_call(
        body, name=name, grid=(R // tr,), in_specs=[blk] * (3 + ng), out_specs=[blk] * 4, out_shape=[sd] * 4,
        compiler_params=_cparams("parallel"),
    )(w, *gs, m, v)


def _rope_tables(dm):
    n = dm.N
    t = jnp.arange(n)
    r = (t // GRID_W).astype(F32)
    col = (t % GRID_W).astype(F32)
    nf = QK_ROPE // 4
    inv = ROPE_BASE ** (-jnp.arange(nf, dtype=F32) / nf)
    ang = jnp.stack([r[:, None] * inv, col[:, None] * inv], axis=1)
    cos, sin = jnp.cos(ang), jnp.sin(ang)
    zero = jnp.zeros_like(sin)
    c64 = jnp.stack([cos, cos], axis=2).reshape(n, QK_ROPE)
    s1 = jnp.stack([-sin, zero], axis=2).reshape(n, QK_ROPE)
    s2 = jnp.stack([zero, sin], axis=2).reshape(n, QK_ROPE)

    def pad(x, fill):
        return jnp.concatenate([jnp.full((n, QK_NOPE), fill, F32), x, jnp.full((n, HEAD_PAD - QK_HEAD), fill, F32)], axis=1)

    lat = jnp.stack([pad(c64, 1.0), pad(s1, 0.0), pad(s2, 0.0)])
    lat = jnp.tile(lat, (1, dm.B, 1))
    nctx = dm.B * dm.CTX
    ctx = jnp.stack([jnp.ones((nctx, HEAD_PAD), F32), jnp.zeros((nctx, HEAD_PAD), F32), jnp.zeros((nctx, HEAD_PAD), F32)])
    return jnp.concatenate([lat, ctx], axis=1)


def _fold_parts(part, dm):
    nblk = part.shape[0]
    nb = (dm.N * nblk) // dm.T
    groups = [part[b * nb:(b + 1) * nb].sum(axis=0) for b in range(dm.B)]
    groups.append(part[dm.B * nb:].sum(axis=0))
    return jnp.stack(groups)


def grouped(items, key):
    groups = {}
    for it in items:
        groups.setdefault(key(it), []).append(it)
    return list(groups.values())


def _flat2(a):
    return a.reshape(-1, a.shape[-1])


def kernel(x, c, ctx, c_ctx, w_mod, b_mod, g_norm, ffn_w1, ffn_w3, ffn_w2, sc_w_in, sc_conv, sc_w_out, mla_w_a, mla_g_qa, mla_w_uq, mla_g_kva, mla_w_ukv, mla_g_q, mla_g_k, mla_w_o, loss_target, m_c_ctx, m_w_mod, m_b_mod, m_g_norm, m_ffn_w1, m_ffn_w3, m_ffn_w2, m_sc_w_in, m_sc_conv, m_sc_w_out, m_mla_w_a, m_mla_g_qa, m_mla_w_uq, m_mla_g_kva, m_mla_w_ukv, m_mla_g_q, m_mla_g_k, m_mla_w_o, v_c_ctx, v_w_mod, v_b_mod, v_g_norm, v_ffn_w1, v_ffn_w3, v_ffn_w2, v_sc_w_in, v_sc_conv, v_sc_w_out, v_mla_w_a, v_mla_g_qa, v_mla_w_uq, v_mla_g_kva, v_mla_w_ukv, v_mla_g_q, v_mla_g_k, v_mla_w_o):
    B, N, D = x.shape
    CTX = ctx.shape[1]
    T = B * (N + CTX)
    tm = next(t for t in (512, 256, 128, 64, 32, 16) if N % t == 0 and (B * CTX) % t == 0)
    dm = Dims(B, N, CTX, D, T, tm)
    L = w_mod.shape[0]
    La, Lb = sc_w_in.shape[0], mla_w_a.shape[0]
    S = N_CHIPS
    ndev = 2 * S
    xi, yi, ci = lax.axis_index("x"), lax.axis_index("y"), lax.axis_index("c")
    chip = 2 * xi + yi
    dev = 2 * chip + ci
    weights = dict(c_ctx=c_ctx, w_mod=w_mod, b_mod=b_mod, g_norm=g_norm, ffn_w1=ffn_w1, ffn_w3=ffn_w3, ffn_w2=ffn_w2,
                   sc_w_in=sc_w_in, sc_conv=sc_conv, sc_w_out=sc_w_out, mla_w_a=mla_w_a, mla_g_qa=mla_g_qa,
                   mla_w_uq=mla_w_uq, mla_g_kva=mla_g_kva, mla_w_ukv=mla_w_ukv, mla_g_q=mla_g_q, mla_g_k=mla_g_k,
                   mla_w_o=mla_w_o)
    mom = dict(c_ctx=(m_c_ctx, v_c_ctx), w_mod=(m_w_mod, v_w_mod), b_mod=(m_b_mod, v_b_mod), g_norm=(m_g_norm, v_g_norm),
               ffn_w1=(m_ffn_w1, v_ffn_w1), ffn_w3=(m_ffn_w3, v_ffn_w3), ffn_w2=(m_ffn_w2, v_ffn_w2),
               sc_w_in=(m_sc_w_in, v_sc_w_in), sc_conv=(m_sc_conv, v_sc_conv), sc_w_out=(m_sc_w_out, v_sc_w_out),
               mla_w_a=(m_mla_w_a, v_mla_w_a), mla_g_qa=(m_mla_g_qa, v_mla_g_qa), mla_w_uq=(m_mla_w_uq, v_mla_w_uq),
               mla_g_kva=(m_mla_g_kva, v_mla_g_kva), mla_w_ukv=(m_mla_w_ukv, v_mla_w_ukv), mla_g_q=(m_mla_g_q, v_mla_g_q),
               mla_g_k=(m_mla_g_k, v_mla_g_k), mla_w_o=(m_mla_w_o, v_mla_w_o))

    big = ["ffn_w1", "ffn_w3", "ffn_w2", "sc_w_in", "sc_w_out", "mla_w_a", "mla_w_uq", "mla_w_ukv", "mla_w_o"]
    F = ffn_w1.shape[-1]
    transposed = ("ffn_w1", "ffn_w3")
    for n in transposed:
        weights[n] = jnp.swapaxes(weights[n], 2, 3)
        mom[n] = tuple(jnp.swapaxes(a, 2, 3) for a in mom[n])
    mixer_names =(["sc_w_in", "sc_w_out"], ["mla_w_a", "mla_w_uq", "mla_w_ukv", "mla_w_o"])

    def placed(names, piece, npieces):
        w2 = [_flat2(weights[n]) for n in names]
        rows = w2[0].shape[0] // npieces
        return dict(zip(names, _place_cast(w2, piece * rows, rows, chip, S, "place_weight")))

    bufs = {}
    for l in range(L):
        for k in range(2):
            bufs["f", l, k] = placed(["ffn_w1", "ffn_w3", "ffn_w2"], 2 * l + k, 2 * L)
        bufs["m", l] = {}
        for n in mixer_names[l % 2]:
            bufs["m", l].update(placed([n], l // 2, weights[n].shape[0]))
    order = [stage for l in range(L) for stage in (("f", l, 0), ("m", l), ("f", l, 1))]

    def gather_after(stage):
        at = order.index(stage)
        if at + 1 == len(order):
            return None, lambda got: None
        nxt = bufs[order[at + 1]]
        names = list(nxt)
        return _gather_rider([nxt[n] for n in names]), lambda got: nxt.update(zip(names, got))

    names = list(bufs[order[0]])
    bufs[order[0]].update(zip(names, _ride_alone(_gather_rider([bufs[order[0]][n] for n in names]), "gather_weights")))

    def ffn_weights(l, k):
        b = bufs["f", l, k]
        return b["ffn_w1"], b["ffn_w3"], b["ffn_w2"]

    def mixer_weights(l):
        b = bufs["m", l]
        w = {}
        if l % 2 == 0:
            w["w_in"] = b["sc_w_in"][:, None]
            w["w_out"] = b["sc_w_out"].reshape(1, D, D)
        else:
            w["w_a"] = jnp.pad(b["mla_w_a"].reshape(1, D, -1), ((0, 0), (0, 0), (0, 512 - (Q_LORA + KV_LORA + QK_ROPE))))
            wuq = jnp.moveaxis(b["mla_w_uq"], 0, 1).reshape(1, Q_LORA, HEADS, QK_HEAD)
            w["w_uq"] = jnp.pad(wuq, ((0, 0), (0, 0), (0, 0), (0, HEAD_PAD - QK_HEAD))).reshape(1, Q_LORA, HEADS * HEAD_PAD)
            w["w_ukv"] = jnp.moveaxis(b["mla_w_ukv"], 0, 1).reshape(1, KV_LORA, HEADS * HEAD_PAD)
            w["w_o"] = b["mla_w_o"].reshape(1, HEADS * V_HEAD, D)
        return w

    vecs = ["g_norm", "sc_conv", "mla_g_qa"]
    gathered = _gather_small([_flat2(weights[n]) for n in vecs], ("x", "y"), "gather_vectors")
    gw = {n: g.reshape((S,) + weights[n].shape) for n, g in zip(vecs, gathered)}
    gnorm = jnp.moveaxis(gw["g_norm"], 0, 2).reshape(L, 3, D)
    convw = jnp.moveaxis(gw["sc_conv"], 0, 2).reshape(La, 3, D)
    gqa = jnp.moveaxis(gw["mla_g_qa"], 0, 1).reshape(Lb, Q_LORA)
    padl = lambda a: jnp.pad(a, ((0, 0), (0, HEAD_PAD - a.shape[1])))
    gains = jnp.stack([padl(gqa), padl(mla_g_kva), padl(mla_g_q), padl(mla_g_k)], axis=1)
    gains = jnp.pad(gains, ((0, 0), (0, 4), (0, 0)))

    R = -(-(ndev * B + 1) // 16) * 16
    call = _gather_small([c], ("x", "y", "c"), "gather_cond")[0].reshape(ndev * B, D)
    cond = jnp.concatenate([call, c_ctx[None], jnp.zeros((R - ndev * B - 1, D), F32)], axis=0)
    C = w_mod.shape[-1]
    bm = lax.dynamic_slice_in_dim(b_mod, chip * C, C, axis=1)[:, None, :]
    mshard = _mod_fwd(cond, w_mod, bm, "mod_fwd")
    mfull = _gather_small([mshard.reshape(L * R, C)], ("x", "y"), "gather_mod")[0].reshape(S, L, R, C)
    mfull = jnp.moveaxis(mfull, 0, 2).reshape(L, R, S * C)
    mine = lax.dynamic_slice_in_dim(mfull, dev * B, B, axis=1)
    mod = jnp.concatenate([mine, mfull[:, ndev * B:ndev * B + 1]], axis=1).reshape(L, B + 1, 9, D)

    tabs = _rope_tables(dm)
    h = jnp.concatenate([x.reshape(B * N, D), ctx.reshape(B * CTX, D)], axis=0)

    saved = []
    lw = [None] * L
    for l in range(L):
        kind, j = l % 2, l // 2
        sv = {}
        sv["h0"] = h
        rider, keep = gather_after(("f", l, 0))
        (h, sv["a1"], sv["b1"], sv["hn1"], sv["y1"]), got = _ffn_fwd(h, mod[l], gnorm[l, 0:1], *ffn_weights(l, 0), 0, dm,
                                                                      "ffn_fwd", rider)
        keep(got)
        sv["h1"] = h
        W = lw[l] = mixer_weights(l)
        rider, keep = gather_after(("m", l))
        if kind == 0:
            (sv["p"], sv["hnm"]), got = _sc_in_fwd(h, mod[l], gnorm[l, 1:2], W["w_in"], 0, dm, "sc_in_fwd", rider)
            sv["z"] = _conv_fwd(sv["p"], convw[j], dm, "conv_fwd")
            h, sv["ym"] = _out_fwd(sv["z"], W["w_out"], h, mod[l], 0, dm, "sc_out_fwd")
        else:
            sv["hnm"], sv["q"], sv["k"], sv["v"] = _mla_proj_fwd(h, mod[l], gnorm[l, 1:2], gains[j:j + 1], tabs, W["w_a"], W["w_uq"],
                                                                 W["w_ukv"], 0, dm, "mla_proj_fwd")
            sv["o"], got = _attn_fwd(sv["q"], sv["k"], sv["v"], dm, "attn_fwd", rider)
            h, sv["ym"] = _out_fwd(sv["o"], W["w_o"], h, mod[l], 0, dm, "mla_out_fwd")
        keep(got)
        sv["h2"] = h
        rider, keep = gather_after(("f", l, 1))
        (h, sv["a2"], sv["b2"], sv["hn2"], sv["y2"]), got = _ffn_fwd(h, mod[l], gnorm[l, 2:3], *ffn_weights(l, 1), 1, dm,
                                                                      "ffn_fwd", rider)
        keep(got)
        saved.append(sv)

    dh, lsum = _loss_grad(h, loss_target.reshape(B * N, D), dm, "loss_grad")
    loss = lax.psum(jnp.sum(lsum[:, 0, 0]), ("x", "y", "c"))

    wq = D // S
    gsum = {n: None for n in big}
    npieces = {n: weights[n].shape[0] * (weights[n].shape[1] if n.startswith("ffn") else 1) for n in big}
    dmod = [None] * L
    dgn = [None] * L
    dconv = [None] * La
    dgains = [None] * Lb
    tk = tm * next(f for f in (3, 2, 1) if (T // tm) % f == 0)
    nk = T // tk
    full_a = pl.BlockSpec((tk, D), lambda s, kk: (kk, 0))
    shard_b = pl.BlockSpec((None, tk, F), lambda s, kk: (s, kk, 0))
    per_slot = lambda r_, c_: pl.BlockSpec((None, r_, c_), lambda s, kk: (s, 0, 0))

    def make_job(grads):
        parts = [g_ for _, _, g_ in grads]
        theirs = _swap_halves(parts, "swap_halves")
        pairs = [None] * len(grads)
        for idx in grouped(range(len(grads)), lambda i: parts[i].shape):
            outs = _pair_sum([parts[i] for i in idx], [theirs[i] for i in idx], ci, "pair_sum")
            for i, o in zip(idx, outs):
                pairs[i] = o
        return [(n, p, pair) for (n, p, _), pair in zip(grads, pairs)]

    def finish_job(job, recv):
        key = lambda i: (recv[i].shape, npieces[job[i][0]], job[i][1], gsum[job[i][0]] is None)
        for idx in grouped(range(len(job)), key):
            names_ = [job[i][0] for i in idx]
            held = None if gsum[names_[0]] is None else [gsum[n] for n in names_]
            outs = _sum_slots([recv[i] for i in idx], [job[i][2] for i in idx], chip, ci, held, npieces[names_[0]],
                              job[idx[0]][1], "sum_slots")
            gsum.update(zip(names_, outs))

    def ffn_back(dh, sv, l, k, job):
        sfx = "1" if k == 0 else "2"
        rider = _scatter_rider([pair for _, _, pair in job]) if job else None
        (dh, da, db, sw, dy, part), recv = _ffn_bwd(dh, sv["h0" if k == 0 else "h2"], mod[l], gnorm[l, 2 * k:2 * k + 1], sv["y" + sfx],
                                                    sv["a" + sfx], sv["b" + sfx], *ffn_weights(l, k), k, dm, "ffn_bwd", rider)
        finish_job(job, recv)
        g1 = _mm_tn(da, sv["hn" + sfx], shard_b, full_a, (S, F, D), per_slot(F, D), (S, nk), "gw1")
        g3 = _mm_tn(db, sv["hn" + sfx], shard_b, full_a, (S, F, D), per_slot(F, D), (S, nk), "gw3")
        g2 = _mm_tn(sw, dy, shard_b, full_a, (S, F, D), per_slot(F, D), (S, nk), "gw2")
        p = 2 * l + k
        return dh, _fold_parts(part, dm), [("ffn_w1", p, g1), ("ffn_w3", p, g3), ("ffn_w2", p, g2)]

    one = (1, nk)
    a1 = lambda kdim: pl.BlockSpec((tk, kdim), lambda s, kk: (kk, 0))
    pending = []
    for l in reversed(range(L)):
        kind, j = l % 2, l // 2
        sv = saved[l]
        W = lw[l]
        dh, p2, grads = ffn_back(dh, sv, l, 1, pending)
        job2 = make_job(grads)
        if kind == 0:
            dy, dz, pg = _out_bwd(dh, sv["ym"], W["w_out"], mod[l], 0, dm, "sc_out_bwd")
            g_out = _mm_tn(sv["z"], dy, a1(D), a1(D), (1, D, D), per_slot(D, D), one, "gw_sc_out")
            dp, dconv[j] = _conv_bwd(dz, sv["p"], convw[j], dm, "conv_bwd")
            g_in = _mm_tn(sv["hnm"], dp, pl.BlockSpec((tk, D), lambda q, kk: (kk, 0)),
                          pl.BlockSpec((3, tk, wq), lambda q, kk: (0, kk, q)), (3, S, D, wq),
                          pl.BlockSpec((3, None, D, wq), lambda q, kk: (0, q, 0, 0)), (S, nk), "gw_sc_in")
            dh, pm = _sc_in_bwd(dh, dp, sv["h1"], mod[l], gnorm[l, 1:2], W["w_in"], 0, dm, "sc_in_bwd")
            grads = [("sc_w_in", j, jnp.moveaxis(g_in.reshape(S, 3, D, wq), 1, 2).reshape(S, D, 3 * wq)),
                     ("sc_w_out", j, g_out.reshape(S, D // S, D))]
        else:
            dy, do, pg = _out_bwd(dh, sv["ym"], W["w_o"], mod[l], 0, dm, "mla_out_bwd")
            g_o = _mm_tn(sv["o"], dy, a1(HEADS * V_HEAD), a1(D), (1, HEADS * V_HEAD, D), per_slot(HEADS * V_HEAD, D), one, "gw_mla_o")
            dq, dkl, dkc, dvl, dvc = _attn_bwd(sv["q"], sv["k"], sv["v"], sv["o"], do, dm, "attn_bwd")
            dh, pm, g_a, g_uq, g_ukv, dgains[j] = _mla_proj_bwd(
                dh, dq, dkl, dkc, dvl, dvc, sv["h1"], mod[l], gnorm[l, 1:2], gains[j:j + 1], tabs, W["w_a"], W["w_uq"], W["w_ukv"], 0, dm, "mla_proj_bwd")
            g_uq = g_uq.reshape(Q_LORA, HEADS, HEAD_PAD)[..., :QK_HEAD].reshape(Q_LORA, S, -1)
            grads = [("mla_w_a", j, g_a[:, :Q_LORA + KV_LORA + QK_ROPE].reshape(S, D // S, -1).astype(BF16)),
                     ("mla_w_uq", j, jnp.moveaxis(g_uq, 1, 0).astype(BF16)),
                     ("mla_w_ukv", j, jnp.moveaxis(g_ukv.reshape(KV_LORA, S, -1), 1, 0).astype(BF16)),
                     ("mla_w_o", j, g_o.reshape(S, HEADS * V_HEAD // S, D))]
        jobm = make_job(grads)
        pm = _fold_parts(pm, dm) + _fold_parts(pg, dm)
        dh, p0, grads = ffn_back(dh, sv, l, 0, job2 + jobm)
        pending = make_job(grads)
        dmod[l] = jnp.concatenate([p0[:, 0:3], pm[:, 0:3], p2[:, 0:3]], axis=1).reshape(B + 1, 9 * D)
        dgn[l] = jnp.stack([p0[:, 3].sum(0), pm[:, 3].sum(0), p2[:, 3].sum(0)])
    grad_x = dh[:B * N].reshape(B, N, D)

    dgains_a = jnp.stack(dgains)
    small = [jnp.stack(dmod).reshape(-1), jnp.stack(dgn).reshape(-1), jnp.stack(dconv).reshape(-1), dgains_a.reshape(-1)]
    sizes = [s_.shape[0] for s_ in small]
    flat = jnp.concatenate(small)
    pad = (-flat.shape[0]) % 1024
    flat = jnp.pad(flat, (0, pad)).reshape(-1, 128)
    allsmall = _gather_small([flat], ("x", "y", "c"), "gather_small")[0].reshape(ndev, -1)
    offs = [0]
    for s_ in sizes:
        offs.append(offs[-1] + s_)
    dmod_all = allsmall[:, offs[0]:offs[1]].reshape(ndev, L, B + 1, 9 * D)
    tot = allsmall[:, offs[1]:offs[4]].sum(axis=0)
    g_gnorm = tot[:offs[2] - offs[1]].reshape(L, 3, D)
    g_conv = tot[offs[2] - offs[1]:offs[3] - offs[1]].reshape(La, 3, D)
    g_gains = tot[offs[3] - offs[1]:].reshape(Lb, 8, HEAD_PAD)
    dM = jnp.concatenate([jnp.moveaxis(dmod_all[:, :, :B], 0, 1).reshape(L, ndev * B, 9 * D),
                          dmod_all[:, :, B].sum(axis=0)[:, None, :], jnp.zeros((L, R - ndev * B - 1, 9 * D), F32)], axis=1)
    g_bmod = dM.sum(axis=1)
    dM_mine = lax.dynamic_slice_in_dim(dM, chip * C, C, axis=2)
    g_wmod, dsil = _mod_bwd(cond, dM_mine, w_mod, "mod_bwd")
    dsil_ctx = dsil[:, ndev * B].sum(axis=0)
    dsil_all = _gather_small([jnp.pad(dsil_ctx.reshape(-1, 128), ((0, (-(D // 128)) % 8), (0, 0)))], ("x", "y"), "gather_dctx")[0]
    dsil_tot = dsil_all.sum(axis=0)[:D // 128].reshape(D)
    sg = jax.nn.sigmoid(c_ctx)
    g_cctx = dsil_tot * (sg * (1.0 + c_ctx * (1.0 - sg)))

    chip_cols = lambda a, width: lax.dynamic_slice_in_dim(a, chip * width, width, axis=a.ndim - 1)
    small_grads = dict(
        c_ctx=g_cctx, b_mod=g_bmod, g_norm=chip_cols(g_gnorm, D // S), sc_conv=chip_cols(g_conv, D // S),
        mla_g_qa=chip_cols(g_gains[:, 0, :Q_LORA], Q_LORA // S), mla_g_kva=g_gains[:, 1, :KV_LORA],
        mla_g_q=g_gains[:, 2, :QK_HEAD], mla_g_k=g_gains[:, 3, :QK_HEAD])

    finish_job(pending, _ride_alone(_scatter_rider([pair for _, _, pair in pending]), "scatter_grads"))
    gsum = dict(zip(big, _swap_cores_inplace([gsum[n] for n in big], "swap_cores")))

    grads, deltas, new_m, new_v = {}, {}, {}, {}
    for n, w in weights.items():
        shape = w.shape
        w2 = _flat2(w) if w.ndim > 1 else w.reshape(1, -1)
        m2, v2 = (a.reshape(w2.shape) for a in mom[n])
        if n in gsum:
            gs = [gsum[n].reshape(w2.shape)]
        elif n == "w_mod":
            gs = [_flat2(g_wmod)]
        else:
            gs = [small_grads[n].reshape(w2.shape)]
        g_, d_, m_, v_ = _adamw(w2, gs, m2, v2, "adamw")
        grads[n], deltas[n], new_m[n], new_v[n] = (a.reshape(shape) for a in (g_, d_, m_, v_))
    for n in transposed:
        grads[n], deltas[n], new_m[n], new_v[n] = (jnp.swapaxes(a, 2, 3) for a in (grads[n], deltas[n], new_m[n], new_v[n]))

    names = list(weights)
    return (loss, grad_x, *[grads[n] for n in names], *[deltas[n] for n in names], *[new_m[n] for n in names],
            *[new_v[n] for n in names])
```
